```python
import jax, jax.numpy as jnp
from jax import lax
import numpy as np

D_MODEL = 1024
BATCH = 2
SEQ = 8192
DEPTH = 2

GRID_W = 64
CTX_LEN = 256

RET_W = D_MODEL // 2
RET_HEADS = 4
RET_HEAD_DIM = RET_W // RET_HEADS
RET_CHUNK = 128
ROPE_BASE = 10000.0
ROPE_FREQS = RET_HEAD_DIM // 4
FNET_W = D_MODEL // 4
FNET_GROUPS = 4
FNET_GROUP_DIM = FNET_W // FNET_GROUPS
GMLP_W = D_MODEL // 4
GMLP_GROUPS = 4
GMLP_GROUP_DIM = GMLP_W // GMLP_GROUPS
GMLP_CHUNK = 128
N_BRANCH = 3
D_FF = 4 * D_MODEL
EPS = 1e-6
IN_W = 4 * RET_W + FNET_W + 2 * GMLP_W + N_BRANCH * D_MODEL
IN_SPLITS = (RET_W, 2 * RET_W, 3 * RET_W, 4 * RET_W, 4 * RET_W + FNET_W,
             4 * RET_W + FNET_W + GMLP_W, 4 * RET_W + FNET_W + 2 * GMLP_W)

kernel_name = "hybrid_retention_fourier_sgu_prefix_dit"


def rms_norm(x, gain=None):
    xf = x.astype(jnp.float32)
    y = xf * lax.rsqrt(jnp.mean(xf * xf, axis=-1, keepdims=True) + EPS)
    if gain is not None:
        y = y * gain.astype(jnp.float32)
    return y.astype(x.dtype)


def modulate(h, shift, scale):
    return h * (1.0 + scale) + shift


def grid_rope_angles(n_tokens):
    rows = n_tokens // GRID_W
    row = jnp.repeat(jnp.arange(rows, dtype=jnp.float32), GRID_W)
    col = jnp.tile(jnp.arange(GRID_W, dtype=jnp.float32), rows)
    freqs = ROPE_BASE ** (-jnp.arange(ROPE_FREQS, dtype=jnp.float32) / ROPE_FREQS)
    return jnp.concatenate([row[:, None] * freqs, col[:, None] * freqs], axis=-1)


def apply_grid_rope(a, angles):
    cos = jnp.cos(angles)[None, :, None, :]
    sin = jnp.sin(angles)[None, :, None, :]
    a1, a2 = jnp.split(a, 2, axis=-1)
    return jnp.concatenate([a1 * cos - a2 * sin, a1 * sin + a2 * cos], axis=-1)


def ret_scan_states(k, v, log_g, s0):
    B, L, H, dk = k.shape
    n = L // RET_CHUNK
    kc = k.reshape(B, n, RET_CHUNK, H, dk)
    vc = v.reshape(B, n, RET_CHUNK, H, -1)
    pos = jnp.arange(RET_CHUNK, dtype=jnp.float32)
    w_state = jnp.exp(log_g[:, None] * (RET_CHUNK - 1.0 - pos)[None, :])
    u = jnp.einsum('bnshd,hs,bnshe->nbhde', kc, w_state, vc)
    decay = jnp.exp(log_g * RET_CHUNK)[None, :, None, None]

    def step(s, u_i):
        return decay * s + u_i, s

    s_final, s_starts = lax.scan(step, s0, u)
    return s_starts, s_final


def ret_chunk_out(q, k, v, log_g, s_starts, inclusive):
    B, L, H, dk = q.shape
    n = L // RET_CHUNK
    qc = q.reshape(B, n, RET_CHUNK, H, dk)
    kc = k.reshape(B, n, RET_CHUNK, H, dk)
    vc = v.reshape(B, n, RET_CHUNK, H, -1)
    pos = jnp.arange(RET_CHUNK, dtype=jnp.float32)
    diff = pos[:, None] - pos[None, :]
    keep = (diff >= 0) if inclusive else (diff > 0)
    dmask = jnp.where(keep[None], jnp.exp(log_g[:, None, None] * jnp.maximum(diff, 0.0)[None]), 0.0)
    scores = jnp.einsum('bnchd,bnshd->bnhcs', qc, kc) * dmask[None, None]
    inner = jnp.einsum('bnhcs,bnshe->bnche', scores, vc)
    q_decay = jnp.exp(log_g[:, None] * (pos + 1.0)[None, :])
    cross = jnp.einsum('bnchd,hc,nbhde->bnche', qc, q_decay, s_starts)
    return (inner + cross).reshape(B, L, H, -1)


def retention_bidir(q, k, v, log_g_f, log_g_b, s0_f, s0_b):
    st_f, sf = ret_scan_states(k, v, log_g_f, s0_f)
    o_f = ret_chunk_out(q, k, v, log_g_f, st_f, True)
    qr, kr, vr = jnp.flip(q, axis=1), jnp.flip(k, axis=1), jnp.flip(v, axis=1)
    st_b, sb = ret_scan_states(kr, vr, log_g_b, s0_b)
    o_b = jnp.flip(ret_chunk_out(qr, kr, vr, log_g_b, st_b, False), axis=1)
    return o_f + o_b, sf, sb


def fourier_mix(f):
    B, L, _ = f.shape
    fg = f.reshape(B, L, FNET_GROUPS, FNET_GROUP_DIM).astype(jnp.float32)
    y = jnp.fft.fft2(fg, axes=(1, 3), norm='ortho').real
    return y.reshape(B, L, FNET_W).astype(f.dtype)


def sgu_mix(u, v, w_s, b_s, g_norm):
    B, L, _ = v.shape
    u = jax.nn.gelu(u)
    v = jax.nn.gelu(v).reshape(B, L, GMLP_GROUPS, GMLP_GROUP_DIM)
    v = rms_norm(v, g_norm.reshape(GMLP_GROUPS, GMLP_GROUP_DIM))
    v = v.reshape(B, L // GMLP_CHUNK, GMLP_CHUNK, GMLP_GROUPS, GMLP_GROUP_DIM)
    s = jnp.einsum('gts,bnsgc->bntgc', w_s, v) + jnp.transpose(b_s)[None, None, :, :, None]
    return u * s.reshape(B, L, GMLP_W)


def token_mixer(h, w_in, log_g, w_s, b_s, g_norm, w_a, w_b, w_c, w_o, s0_f, s0_b, angles):
    B, L, _ = h.shape
    z = h @ w_in
    q, k, v, g, f, u, vs, gates = jnp.split(z, IN_SPLITS, axis=-1)
    q = q.reshape(B, L, RET_HEADS, RET_HEAD_DIM).astype(jnp.float32)
    k = k.reshape(B, L, RET_HEADS, RET_HEAD_DIM).astype(jnp.float32) * (RET_HEAD_DIM ** -0.5)
    v = v.reshape(B, L, RET_HEADS, RET_HEAD_DIM).astype(jnp.float32)
    if angles is not None:
        q = apply_grid_rope(q, angles)
        k = apply_grid_rope(k, angles)
    o, sf, sb = retention_bidir(q, k, v, log_g[0], log_g[1], s0_f, s0_b)
    ret = rms_norm(o).reshape(B, L, RET_W).astype(h.dtype) * jax.nn.silu(g)
    four = fourier_mix(f)
    sgu = sgu_mix(u, vs, w_s, b_s, g_norm)
    ga, gb, gc = jnp.split(jax.nn.sigmoid(gates), N_BRANCH, axis=-1)
    merged = ga * (ret @ w_a) + gb * (four @ w_b) + gc * (sgu @ w_c)
    return merged @ w_o, sf, sb


def sq_relu_mlp(h, w_up, w_down):
    a = jax.nn.relu(h @ w_up)
    return (a * a) @ w_down


def setup_inputs(seed: int = 0) -> dict:
    key = jax.random.key(seed)
    ks = jax.random.split(key, 24)
    f32 = jnp.float32
    nrm = lambda k, shape, s: jax.random.normal(k, shape, f32) * s
    base_logit = jnp.log(2.0 ** (5.0 + jnp.arange(RET_HEADS, dtype=f32)) - 1.0)
    return {
        'x': nrm(ks[0], (BATCH, SEQ, D_MODEL), 1.0),
        'c': nrm(ks[1], (BATCH, D_MODEL), 1.0),
        'ctx': nrm(ks[2], (BATCH, CTX_LEN, D_MODEL), 1.0),
        'c_ctx': nrm(ks[3], (D_MODEL,), 1.0),
        'w_mod': nrm(ks[4], (DEPTH, D_MODEL, 6 * D_MODEL), 0.5 * D_MODEL ** -0.5),
        'b_mod': nrm(ks[5], (DEPTH, 6 * D_MODEL), 0.02),
        'g_pre_mix': 1.0 + nrm(ks[6], (DEPTH, D_MODEL), 0.02),
        'g_post_mix': 1.0 + nrm(ks[7], (DEPTH, D_MODEL), 0.02),
        'g_pre_mlp': 1.0 + nrm(ks[8], (DEPTH, D_MODEL), 0.02),
        'g_post_mlp': 1.0 + nrm(ks[9], (DEPTH, D_MODEL), 0.02),
        'w_in': nrm(ks[10], (DEPTH, D_MODEL, IN_W), D_MODEL ** -0.5),
        'ret_decay_logit': base_logit[None, None, :] + nrm(ks[11], (DEPTH, 2, RET_HEADS), 0.1),
        'sgu_w_s': nrm(ks[12], (DEPTH, GMLP_GROUPS, GMLP_CHUNK, GMLP_CHUNK), GMLP_CHUNK ** -0.5),
        'sgu_b_s': 1.0 + nrm(ks[13], (DEPTH, GMLP_GROUPS, GMLP_CHUNK), 0.02),
        'sgu_norm': 1.0 + nrm(ks[14], (DEPTH, GMLP_W), 0.02),
        'w_branch_a': nrm(ks[15], (DEPTH, RET_W, D_MODEL), RET_W ** -0.5),
        'w_branch_b': nrm(ks[16], (DEPTH, FNET_W, D_MODEL), FNET_W ** -0.5),
        'w_branch_c': nrm(ks[17], (DEPTH, GMLP_W, D_MODEL), GMLP_W ** -0.5),
        'w_out': nrm(ks[18], (DEPTH, D_MODEL, D_MODEL), D_MODEL ** -0.5),
        'w_up': nrm(ks[19], (DEPTH, D_MODEL, D_FF), D_MODEL ** -0.5),
        'w_down': nrm(ks[20], (DEPTH, D_FF, D_MODEL), D_FF ** -0.5),
    }


def reference(x, c, ctx, c_ctx, w_mod, b_mod, g_pre_mix, g_post_mix, g_pre_mlp, g_post_mlp,
              w_in, ret_decay_logit, sgu_w_s, sgu_b_s, sgu_norm, w_branch_a, w_branch_b,
              w_branch_c, w_out, w_up, w_down):
    B, L, _ = x.shape
    angles = grid_rope_angles(L)
    silu_c = jax.nn.silu(c)
    silu_cc = jax.nn.silu(c_ctx)
    s0 = jnp.zeros((B, RET_HEADS, RET_HEAD_DIM, RET_HEAD_DIM), jnp.float32)
    for l in range(DEPTH):
        last = l == DEPTH - 1
        mod = (silu_c @ w_mod[l] + b_mod[l])[:, None, :]
        sh1, sc1, gt1, sh2, sc2, gt2 = jnp.split(mod, 6, axis=-1)
        mod_c = silu_cc @ w_mod[l] + b_mod[l]
        csh1, csc1, cgt1, csh2, csc2, cgt2 = jnp.split(mod_c, 6, axis=-1)
        log_g = jax.nn.log_sigmoid(ret_decay_logit[l].astype(jnp.float32))

        hc = modulate(rms_norm(ctx, g_pre_mix[l]), csh1, csc1)
        if last:
            kc_, vc_ = jnp.split(hc @ w_in[l][:, RET_W:3 * RET_W], 2, axis=-1)
            kc_ = kc_.reshape(B, CTX_LEN, RET_HEADS, RET_HEAD_DIM).astype(jnp.float32) * (RET_HEAD_DIM ** -0.5)
            vc_ = vc_.reshape(B, CTX_LEN, RET_HEADS, RET_HEAD_DIM).astype(jnp.float32)
            _, sf = ret_scan_states(kc_, vc_, log_g[0], s0)
            _, sb = ret_scan_states(jnp.flip(kc_, axis=1), jnp.flip(vc_, axis=1), log_g[1], s0)
        else:
            yc, sf, sb = token_mixer(hc, w_in[l], log_g, sgu_w_s[l], sgu_b_s[l], sgu_norm[l],
                                     w_branch_a[l], w_branch_b[l], w_branch_c[l], w_out[l],
                                     s0, s0, None)
            ctx_mid = ctx + cgt1 * rms_norm(yc, g_post_mix[l])
            hc2 = modulate(rms_norm(ctx_mid, g_pre_mlp[l]), csh2, csc2)
            ctx_next = ctx_mid + cgt2 * rms_norm(sq_relu_mlp(hc2, w_up[l], w_down[l]), g_post_mlp[l])

        h = modulate(rms_norm(x, g_pre_mix[l]), sh1, sc1)
        y, _, _ = token_mixer(h, w_in[l], log_g, sgu_w_s[l], sgu_b_s[l], sgu_norm[l],
                              w_branch_a[l], w_branch_b[l], w_branch_c[l], w_out[l],
                              sf, sb, angles)
        x = x + gt1 * rms_norm(y, g_post_mix[l])
        h2 = modulate(rms_norm(x, g_pre_mlp[l]), sh2, sc2)
        x = x + gt2 * rms_norm(sq_relu_mlp(h2, w_up[l], w_down[l]), g_post_mlp[l])

        if not last:
            ctx = ctx_next
    return x
```

```python
import functools

import numpy as np
import jax
import jax.numpy as jnp
from jax import lax
from jax.experimental import pallas as pl
from jax.experimental.pallas import tpu as pltpu

F32 = jnp.float32
BF16 = jnp.bfloat16

D_MODEL = 1024
SEQ = 8192
CTX_LEN = 256
TOK = CTX_LEN + SEQ
GRID_W = 64
RET_W = 512
HEADS = 4
HEAD_DIM = 128
CHUNK = 128
ROPE_BASE = 10000.0
ROPE_FREQS = HEAD_DIM // 4
FNET_W = 256
FNET_GROUP = 64
GMLP_W = 256
GMLP_GROUP = 64
D_FF = 4 * D_MODEL
EPS = 1e-6
IN_W = 4 * RET_W + FNET_W + 2 * GMLP_W + 3 * D_MODEL
COL_F = 4 * RET_W
COL_U = COL_F + FNET_W
COL_VS = COL_U + GMLP_W
COL_GATE = COL_VS + GMLP_W

TM = 256
N_TILES = TOK // TM
N_CHUNKS = TOK // CHUNK
FFT_N1 = 64
FFT_N2 = 128
HALF = 128

_VMEM_LIMIT = 56 * 1024 * 1024


def _dot(a, b):
    return jnp.dot(a, b, preferred_element_type=F32)


def _split(x):
    hi = x.astype(BF16)
    lo = (x - hi.astype(F32)).astype(BF16)
    return hi, lo


def _dot3(ah, al, bh, bl):
    return _dot(ah, bh) + _dot(al, bh) + _dot(ah, bl)


def _np_split(a64):
    hi = np.asarray(a64, np.float32).astype(BF16)
    lo = (np.asarray(a64, np.float32) - hi.astype(np.float32)).astype(BF16)
    return jnp.asarray(hi), jnp.asarray(lo)


def _rms(x):
    return x * lax.rsqrt(jnp.mean(x * x, axis=-1, keepdims=True) + EPS)


def _gelu(x):
    return x * (0.5 * (1.0 + jnp.tanh(0.7978845608028654 * (x + 0.044715 * (x * x * x)))))


def _sigmoid(x):
    return 1.0 / (1.0 + jnp.exp(-x))


def _const_spec(shape, nargs):
    zeros = (0,) * len(shape)
    if nargs == 1:
        return pl.BlockSpec(shape, lambda a: zeros, pipeline_mode=pl.Buffered(1))
    if nargs == 2:
        return pl.BlockSpec(shape, lambda a, b: zeros, pipeline_mode=pl.Buffered(1))
    return pl.BlockSpec(shape, lambda a, b, c: zeros, pipeline_mode=pl.Buffered(1))


def _mod_kernel(a_ref, w_ref, b_ref, o_ref):
    a = a_ref[...]
    a = a * _sigmoid(a)
    ah, al = _split(a)
    w = w_ref[0].astype(BF16)
    o_ref[0] = _dot(ah, w) + _dot(al, w) + b_ref[0]


def _modulation(cond_rows, w_mod, b_mod):
    depth = w_mod.shape[0]
    tn = 1536
    return pl.pallas_call(
        _mod_kernel,
        grid=(depth, (6 * D_MODEL) // tn),
        in_specs=[
            pl.BlockSpec((8, D_MODEL), lambda l, j: (0, 0)),
            pl.BlockSpec((1, D_MODEL, tn), lambda l, j: (l, 0, j)),
            pl.BlockSpec((1, 1, tn), lambda l, j: (l, 0, j)),
        ],
        out_specs=pl.BlockSpec((1, 8, tn), lambda l, j: (l, 0, j)),
        out_shape=jax.ShapeDtypeStruct((depth, 8, 6 * D_MODEL), F32),
        compiler_params=pltpu.CompilerParams(
            dimension_semantics=("arbitrary", "arbitrary"), vmem_limit_bytes=_VMEM_LIMIT),
        name="modulation",
    )(cond_rows, w_mod, b_mod.reshape(depth, 1, 6 * D_MODEL))


def _in_proj_kernel(x_ref, mod_ref, gpre_ref, cos_ref, sin_ref, w_ref, pavg_ref, gn_ref,
                    q_ref, k_ref, v_ref, sg_ref, fl_ref, fc_ref, u_ref, vn_ref, gate_ref):
    i = pl.program_id(1)
    x = x_ref[0]
    shift = mod_ref[0, :, 0:D_MODEL]
    scale = mod_ref[0, :, D_MODEL:2 * D_MODEL]
    h = _rms(x) * gpre_ref[...]
    hb = (h * (1.0 + scale) + shift).astype(BF16)

    def proj(lo, hi):
        return _dot(hb, w_ref[:, lo:hi])

    cos = cos_ref[...]
    sin = sin_ref[...]

    def rope_store(z, dst):
        for hd in range(HEADS):
            a = z[:, hd * HEAD_DIM:(hd + 1) * HEAD_DIM]
            r = a * cos + pltpu.roll(a, HEAD_DIM // 2, 1) * sin
            dst[0, :, hd * HEAD_DIM:(hd + 1) * HEAD_DIM] = r.astype(BF16)

    rope_store(proj(0, RET_W), q_ref)
    rope_store(proj(RET_W, 2 * RET_W) * (HEAD_DIM ** -0.5), k_ref)
    v_ref[0] = proj(2 * RET_W, 3 * RET_W).astype(BF16)
    g = proj(3 * RET_W, 4 * RET_W)
    sg_ref[0] = (g * _sigmoid(g)).astype(BF16)

    f = proj(COL_F, COL_U)

    @pl.when(i == 0)
    def _():
        for hf in range(2):
            fc_ref[0, hf] = f[:, hf * HALF:(hf + 1) * HALF]

    @pl.when(i > 0)
    def _():
        for hf in range(2):
            fl_ref[0, hf] = f[:, hf * HALF:(hf + 1) * HALF]

    u_ref[0] = _gelu(proj(COL_U, COL_VS)).astype(BF16)
    vg = _gelu(proj(COL_VS, COL_GATE))
    sh, sl = _split(vg * vg)
    pavg = pavg_ref[...]
    ms = _dot(sh, pavg) + _dot(sl, pavg)
    vn_ref[0] = (vg * lax.rsqrt(ms + EPS) * gn_ref[...]).astype(BF16)

    for c in range(3):
        z = proj(COL_GATE + c * D_MODEL, COL_GATE + (c + 1) * D_MODEL)
        gate_ref[0, :, c * D_MODEL:(c + 1) * D_MODEL] = _sigmoid(z).astype(BF16)


def _in_proj(xc, mod_l, g_pre, cos_t, sin_t, w_in, pavg, g_norm):
    nb = xc.shape[0]
    tok_spec = lambda w: pl.BlockSpec((1, TM, w), lambda b, i: (b, i, 0))
    out_shapes = (
        jax.ShapeDtypeStruct((nb, TOK, RET_W), BF16),
        jax.ShapeDtypeStruct((nb, TOK, RET_W), BF16),
        jax.ShapeDtypeStruct((nb, TOK, RET_W), BF16),
        jax.ShapeDtypeStruct((nb, TOK, RET_W), BF16),
        jax.ShapeDtypeStruct((nb, 2, SEQ, HALF), F32),
        jax.ShapeDtypeStruct((nb, 2, CTX_LEN, HALF), F32),
        jax.ShapeDtypeStruct((nb, TOK, GMLP_W), BF16),
        jax.ShapeDtypeStruct((nb, TOK, GMLP_W), BF16),
        jax.ShapeDtypeStruct((nb, TOK, 3 * D_MODEL), BF16),
    )
    return pl.pallas_call(
        _in_proj_kernel,
        grid=(nb, N_TILES),
        in_specs=[
            tok_spec(D_MODEL),
            pl.BlockSpec((1, 1, 2 * D_MODEL), lambda b, i: (jnp.where(i == 0, 2, b), 0, 0)),
            _const_spec((1, D_MODEL), 2),
            pl.BlockSpec((TM, HEAD_DIM), lambda b, i: (i, 0)),
            pl.BlockSpec((TM, HEAD_DIM), lambda b, i: (i, 0)),
            _const_spec((D_MODEL, IN_W), 2),
            _const_spec((GMLP_W, GMLP_W), 2),
            _const_spec((1, GMLP_W), 2),
        ],
        out_specs=(
            tok_spec(RET_W), tok_spec(RET_W), tok_spec(RET_W), tok_spec(RET_W),
            pl.BlockSpec((1, 2, TM, HALF), lambda b, i: (b, 0, jnp.maximum(i - 1, 0), 0)),
            pl.BlockSpec((1, 2, CTX_LEN, HALF), lambda b, i: (b, 0, 0, 0)),
            tok_spec(GMLP_W), tok_spec(GMLP_W), tok_spec(3 * D_MODEL),
        ),
        out_shape=out_shapes,
        compiler_params=pltpu.CompilerParams(
            dimension_semantics=("arbitrary", "arbitrary"), vmem_limit_bytes=_VMEM_LIMIT),
        name="in_proj",
    )(xc, mod_l, g_pre, cos_t, sin_t, w_in, pavg, g_norm)


_T_MASK, _T_WF, _T_WB, _T_QF, _T_QB, _T_DF, _T_DB = range(7)


def _ret_kernel(logit_ref, q_ref, k_ref, v_ref, sg_ref, o_ref, tab_ref, sf_ref, sb_ref, sball_ref):
    b = pl.program_id(0)
    phase = pl.program_id(1)
    j = pl.program_id(2)

    @pl.when((b == 0) & (phase == 0) & (j == 0))
    def _():
        x = logit_ref[...]
        lg = -(jnp.maximum(-x, 0.0) + jnp.log(1.0 + jnp.exp(-jnp.abs(x))))
        row = lax.broadcasted_iota(jnp.int32, (CHUNK, CHUNK), 0).astype(F32)
        col = lax.broadcasted_iota(jnp.int32, (CHUNK, CHUNK), 1).astype(F32)
        diff = row - col
        for hd in range(HEADS):
            lf = jnp.broadcast_to(lg[hd:hd + 1, :], (CHUNK, CHUNK))
            lb = jnp.broadcast_to(lg[HEADS + hd:HEADS + hd + 1, :], (CHUNK, CHUNK))
            tab_ref[_T_MASK, hd] = jnp.where(diff >= 0.0, jnp.exp(lf * jnp.maximum(diff, 0.0)),
                                             jnp.exp(lb * jnp.maximum(-diff, 0.0)))
            tab_ref[_T_WF, hd] = jnp.exp(lf * (CHUNK - 1.0 - row))
            tab_ref[_T_WB, hd] = jnp.exp(lb * row)
            tab_ref[_T_QF, hd] = jnp.exp(lf * (row + 1.0))
            tab_ref[_T_QB, hd] = jnp.exp(lb * (CHUNK - row))
            tab_ref[_T_DF, hd] = jnp.exp(lf * float(CHUNK))
            tab_ref[_T_DB, hd] = jnp.exp(lb * float(CHUNK))

    def state_update(s_ref, hd, kh, vh, w_idx, d_idx):
        kw = (kh.astype(F32) * tab_ref[w_idx, hd]).astype(BF16)
        u = lax.dot_general(kw, vh, (((0,), (0,)), ((), ())), preferred_element_type=F32)
        s_ref[hd] = tab_ref[d_idx, hd] * s_ref[hd] + u

    @pl.when(phase == 0)
    def _():
        @pl.when(j == 0)
        def _():
            sb_ref[...] = jnp.zeros_like(sb_ref)

        blk = jnp.where(j == 0, 0, N_TILES - j)
        for cc in (1, 0):
            rows = slice(cc * CHUNK, (cc + 1) * CHUNK)
            for hd in range(HEADS):
                cols = slice(hd * HEAD_DIM, (hd + 1) * HEAD_DIM)
                sball_ref[2 * blk + cc, hd] = sb_ref[hd].astype(BF16)
                state_update(sb_ref, hd, k_ref[0, rows, cols], v_ref[0, rows, cols], _T_WB, _T_DB)

    @pl.when(phase == 1)
    def _():
        @pl.when(j == 0)
        def _():
            sf_ref[...] = jnp.zeros_like(sf_ref)

        for cc in (0, 1):
            rows = slice(cc * CHUNK, (cc + 1) * CHUNK)
            for hd in range(HEADS):
                cols = slice(hd * HEAD_DIM, (hd + 1) * HEAD_DIM)
                qh = q_ref[0, rows, cols]
                kh = k_ref[0, rows, cols]
                vh = v_ref[0, rows, cols]
                sc = lax.dot_general(qh, kh, (((1,), (1,)), ((), ())), preferred_element_type=F32)
                inner = _dot((sc * tab_ref[_T_MASK, hd]).astype(BF16), vh)
                qf = qh.astype(F32)
                qq = jnp.concatenate([(qf * tab_ref[_T_QF, hd]).astype(BF16),
                                      (qf * tab_ref[_T_QB, hd]).astype(BF16)], axis=1)
                ss = jnp.concatenate([sf_ref[hd].astype(BF16), sball_ref[2 * j + cc, hd]], axis=0)
                o = inner + _dot(qq, ss)
                o_ref[0, rows, cols] = (_rms(o) * sg_ref[0, rows, cols].astype(F32)).astype(BF16)
                state_update(sf_ref, hd, kh, vh, _T_WF, _T_DF)


def _retention(logits, q, k, v, sg):
    nb = q.shape[0]
    kv_idx = lambda b, p, j: (b, jnp.where(p == 0, jnp.where(j == 0, 0, N_TILES - j), j), 0)
    q_idx = lambda b, p, j: (b, jnp.where(p == 0, 0, j), 0)
    blk = (1, TM, RET_W)
    return pl.pallas_call(
        _ret_kernel,
        grid=(nb, 2, N_TILES),
        in_specs=[
            _const_spec((2 * HEADS, 1), 3),
            pl.BlockSpec(blk, q_idx), pl.BlockSpec(blk, kv_idx), pl.BlockSpec(blk, kv_idx),
            pl.BlockSpec(blk, q_idx),
        ],
        out_specs=pl.BlockSpec(blk, q_idx),
        out_shape=jax.ShapeDtypeStruct((nb, TOK, RET_W), BF16),
        scratch_shapes=[
            pltpu.VMEM((7, HEADS, CHUNK, CHUNK), F32),
            pltpu.VMEM((HEADS, HEAD_DIM, HEAD_DIM), F32),
            pltpu.VMEM((HEADS, HEAD_DIM, HEAD_DIM), F32),
            pltpu.VMEM((N_CHUNKS, HEADS, HEAD_DIM, HEAD_DIM), BF16),
        ],
        compiler_params=pltpu.CompilerParams(
            dimension_semantics=("arbitrary", "arbitrary", "arbitrary"), vmem_limit_bytes=_VMEM_LIMIT),
        name="retention",
    )(logits, q, k, v, sg)


def _dft_cos_sin(n):
    idx = np.arange(n, dtype=np.float64)
    ang = 2.0 * np.pi * ((idx[:, None] * idx[None, :]) % n) / n
    return np.cos(ang), np.sin(ang)


def _channel_dft(scale):
    c, s = _dft_cos_sin(FNET_GROUP)
    eye = np.eye(FNET_W // FNET_GROUP)
    return np.concatenate([np.kron(eye, c), np.kron(eye, s)], axis=0) * scale


def _fft_constants():
    k1 = np.arange(FFT_N1, dtype=np.float64)[None, :, None]
    n1 = np.arange(FFT_N1, dtype=np.float64)[None, None, :]
    n2 = np.arange(FFT_N2, dtype=np.float64)[:, None, None]
    ang = 2.0 * np.pi * ((k1 * (FFT_N2 * n1 + n2)) % SEQ) / SEQ
    m1 = np.concatenate([np.cos(ang), -np.sin(ang)], axis=1)
    c, s = _dft_cos_sin(FFT_N2)
    m3 = np.block([[c, s], [-s, c]])
    return _np_split(m1), _np_split(m3), _np_split(_channel_dft((SEQ * FNET_GROUP) ** -0.5))


def _ctx_fft_constants():
    c, s = _dft_cos_sin(CTX_LEN)
    return (_np_split(np.concatenate([c, -s], axis=0)),
            _np_split(_channel_dft((CTX_LEN * FNET_GROUP) ** -0.5)))


def _fft1_kernel(x_ref, mh_ref, ml_ref, o_ref):
    for jj in range(8):
        x = jnp.concatenate([x_ref[hf, :, jj, :] for hf in range(2)], axis=1)
        xh, xl = _split(x)
        a = _dot3(mh_ref[jj], ml_ref[jj], xh, xl)
        for hf in range(2):
            for ri in range(2):
                o_ref[hf, ri, jj] = a[ri * FFT_N1:(ri + 1) * FFT_N1, hf * HALF:(hf + 1) * HALF]


def _fft3_kernel(a_ref, wh_ref, wl_ref, ch_ref, cl_ref, o_ref):
    for jj in range(8):
        rhs = jnp.concatenate(
            [jnp.concatenate([a_ref[hf, ri, :, jj, :] for ri in range(2)], axis=0) for hf in range(2)],
            axis=1)
        rh, rl = _split(rhs)
        y = _dot3(wh_ref[...], wl_ref[...], rh, rl)
        zh, zl = _split(jnp.concatenate([y[:FFT_N2], y[FFT_N2:]], axis=1))
        four = _dot3(zh, zl, ch_ref[...], cl_ref[...])
        for hf in range(2):
            o_ref[hf, :, jj, :] = four[:, hf * HALF:(hf + 1) * HALF]


def _fourier_latent(f_lat):
    nb = f_lat.shape[0]
    (m1h, m1l), (m3h, m3l), (cdh, cdl) = _fft_constants()
    params = pltpu.CompilerParams(dimension_semantics=("arbitrary", "arbitrary"), vmem_limit_bytes=_VMEM_LIMIT)
    a = pl.pallas_call(
        _fft1_kernel,
        grid=(nb, FFT_N2 // 8),
        in_specs=[
            pl.BlockSpec((None, 2, FFT_N1, 8, HALF), lambda b, j: (b, 0, 0, j, 0)),
            pl.BlockSpec((8, 2 * FFT_N1, FFT_N1), lambda b, j: (j, 0, 0)),
            pl.BlockSpec((8, 2 * FFT_N1, FFT_N1), lambda b, j: (j, 0, 0)),
        ],
        out_specs=pl.BlockSpec((None, 2, 2, 8, FFT_N1, HALF), lambda b, j: (b, 0, 0, j, 0, 0)),
        out_shape=jax.ShapeDtypeStruct((nb, 2, 2, FFT_N2, FFT_N1, HALF), F32),
        compiler_params=params,
        name="fourier_stage1",
    )(f_lat.reshape(nb, 2, FFT_N1, FFT_N2, HALF), m1h, m1l)
    out = pl.pallas_call(
        _fft3_kernel,
        grid=(nb, FFT_N1 // 8),
        in_specs=[
            pl.BlockSpec((None, 2, 2, FFT_N2, 8, HALF), lambda b, j: (b, 0, 0, 0, j, 0)),
            _const_spec((2 * FFT_N2, 2 * FFT_N2), 2), _const_spec((2 * FFT_N2, 2 * FFT_N2), 2),
            _const_spec((2 * FNET_W, FNET_W), 2), _const_spec((2 * FNET_W, FNET_W), 2),
        ],
        out_specs=pl.BlockSpec((None, 2, FFT_N2, 8, HALF), lambda b, j: (b, 0, 0, j, 0)),
        out_shape=jax.ShapeDtypeStruct((nb, 2, FFT_N2, FFT_N1, HALF), F32),
        compiler_params=params,
        name="fourier_stage2",
    )(a, m3h, m3l, cdh, cdl)
    return out.reshape(nb, 2, SEQ, HALF)


def _fft_ctx_kernel(x_ref, wh_ref, wl_ref, ch_ref, cl_ref, o_ref):
    x = jnp.concatenate([x_ref[hf] for hf in range(2)], axis=1)
    xh, xl = _split(x)
    z = _dot3(wh_ref[...], wl_ref[...], xh, xl)
    zh, zl = _split(jnp.concatenate([z[:CTX_LEN], z[CTX_LEN:]], axis=1))
    four = _dot3(zh, zl, ch_ref[...], cl_ref[...])
    for hf in range(2):
        o_ref[hf] = four[:, hf * HALF:(hf + 1) * HALF]


def _fourier_ctx(f_ctx):
    nb = f_ctx.shape[0]
    (wh, wl), (cdh, cdl) = _ctx_fft_constants()
    return pl.pallas_call(
        _fft_ctx_kernel,
        grid=(nb,),
        in_specs=[
            pl.BlockSpec((None, 2, CTX_LEN, HALF), lambda b: (b, 0, 0, 0)),
            _const_spec((2 * CTX_LEN, CTX_LEN), 1), _const_spec((2 * CTX_LEN, CTX_LEN), 1),
            _const_spec((2 * FNET_W, FNET_W), 1), _const_spec((2 * FNET_W, FNET_W), 1),
        ],
        out_specs=pl.BlockSpec((None, 2, CTX_LEN, HALF), lambda b: (b, 0, 0, 0)),
        out_shape=jax.ShapeDtypeStruct((nb, 2, CTX_LEN, HALF), F32),
        compiler_params=pltpu.CompilerParams(dimension_semantics=("arbitrary",), vmem_limit_bytes=_VMEM_LIMIT),
        name="fourier_ctx",
    )(f_ctx, wh, wl, cdh, cdl)


def _mix_mlp_kernel(x_ref, ret_ref, fl_ref, fc_ref, u_ref, vn_ref, gate_ref, mod_ref, ws_ref, bs_ref,
                    wa_ref, wb_ref, wc_ref, wo_ref, gpm_ref, gprm_ref, gpom_ref, wup_ref, wdn_ref,
                    o_ref, *, ctx_out):
    i = pl.program_id(1)

    def body():
        group = lax.broadcasted_iota(jnp.int32, (CHUNK, GMLP_W), 1) // GMLP_GROUP
        parts = []
        for cc in range(TM // CHUNK):
            rows = slice(cc * CHUNK, (cc + 1) * CHUNK)
            vn = vn_ref[0, rows, :]
            s = bs_ref[...]
            for g in range(GMLP_W // GMLP_GROUP):
                s = s + jnp.where(group == g, _dot(ws_ref[g], vn), 0.0)
            parts.append(u_ref[0, rows, :].astype(F32) * s)
        sgu = jnp.concatenate(parts, axis=0).astype(BF16)

        if ctx_out:
            four = jnp.where(i == 0,
                             jnp.concatenate([fc_ref[0, hf] for hf in range(2)], axis=1),
                             jnp.concatenate([fl_ref[0, hf] for hf in range(2)], axis=1))
        else:
            four = jnp.concatenate([fl_ref[0, hf] for hf in range(2)], axis=1)

        ga = gate_ref[0, :, 0:D_MODEL].astype(F32)
        gb = gate_ref[0, :, D_MODEL:2 * D_MODEL].astype(F32)
        gc = gate_ref[0, :, 2 * D_MODEL:3 * D_MODEL].astype(F32)
        merged = (ga * _dot(ret_ref[0], wa_ref[...]) + gb * _dot(four.astype(BF16), wb_ref[...])
                  + gc * _dot(sgu, wc_ref[...]))
        y = _dot(merged.astype(BF16), wo_ref[...])

        gate1 = mod_ref[0, :, 0:D_MODEL]
        shift2 = mod_ref[0, :, D_MODEL:2 * D_MODEL]
        scale2 = mod_ref[0, :, 2 * D_MODEL:3 * D_MODEL]
        gate2 = mod_ref[0, :, 3 * D_MODEL:4 * D_MODEL]
        x1 = x_ref[0] + gate1 * (_rms(y) * gpm_ref[...])
        h2 = ((_rms(x1) * gprm_ref[...]) * (1.0 + scale2) + shift2).astype(BF16)
        a = jnp.maximum(_dot(h2, wup_ref[...]), 0.0)
        m = _dot((a * a).astype(BF16), wdn_ref[...])
        o_ref[0] = x1 + gate2 * (_rms(m) * gpom_ref[...])

    if ctx_out:
        body()
    else:
        pl.when(i > 0)(body)


def _mix_mlp(xc, ret, four_lat, four_ctx, ug, vn, gates, mod_l, ws, bs_tab, wa, wb, wc, wo,
             g_post_mix, g_pre_mlp, g_post_mlp, w_up, w_down, *, ctx_out):
    nb = xc.shape[0]
    tok_spec = lambda w: pl.BlockSpec((1, TM, w), lambda b, i: (b, i, 0))
    if ctx_out:
        out_spec = tok_spec(D_MODEL)
        out_shape = jax.ShapeDtypeStruct((nb, TOK, D_MODEL), F32)
    else:
        out_spec = pl.BlockSpec((1, TM, D_MODEL), lambda b, i: (b, jnp.maximum(i - 1, 0), 0))
        out_shape = jax.ShapeDtypeStruct((nb, SEQ, D_MODEL), F32)
    return pl.pallas_call(
        functools.partial(_mix_mlp_kernel, ctx_out=ctx_out),
        grid=(nb, N_TILES),
        in_specs=[
            tok_spec(D_MODEL), tok_spec(RET_W),
            pl.BlockSpec((1, 2, TM, HALF), lambda b, i: (b, 0, jnp.maximum(i - 1, 0), 0)),
            pl.BlockSpec((1, 2, CTX_LEN, HALF), lambda b, i: (b, 0, 0, 0)),
            tok_spec(GMLP_W), tok_spec(GMLP_W), tok_spec(3 * D_MODEL),
            pl.BlockSpec((1, 1, 4 * D_MODEL), lambda b, i: (jnp.where(i == 0, 2, b), 0, 0)),
            _const_spec((GMLP_W // GMLP_GROUP, CHUNK, CHUNK), 2),
            _const_spec((CHUNK, GMLP_W), 2),
            _const_spec((RET_W, D_MODEL), 2), _const_spec((FNET_W, D_MODEL), 2),
            _const_spec((GMLP_W, D_MODEL), 2), _const_spec((D_MODEL, D_MODEL), 2),
            _const_spec((1, D_MODEL), 2), _const_spec((1, D_MODEL), 2), _const_spec((1, D_MODEL), 2),
            _const_spec((D_MODEL, D_FF), 2), _const_spec((D_FF, D_MODEL), 2),
        ],
        out_specs=out_spec,
        out_shape=out_shape,
        compiler_params=pltpu.CompilerParams(
            dimension_semantics=("arbitrary", "arbitrary"), vmem_limit_bytes=_VMEM_LIMIT),
        name="mix_mlp",
    )(xc, ret, four_lat, four_ctx, ug, vn, gates, mod_l, ws, bs_tab, wa, wb, wc, wo,
      g_post_mix, g_pre_mlp, g_post_mlp, w_up, w_down)


def _rope_tables():
    rows = SEQ // GRID_W
    row = jnp.repeat(jnp.arange(rows, dtype=F32), GRID_W)
    col = jnp.tile(jnp.arange(GRID_W, dtype=F32), rows)
    freqs = ROPE_BASE ** (-jnp.arange(ROPE_FREQS, dtype=F32) / ROPE_FREQS)
    ang = jnp.concatenate([row[:, None] * freqs, col[:, None] * freqs], axis=-1)
    cos = jnp.cos(ang)
    sin = jnp.sin(ang)
    cos_t = jnp.concatenate([jnp.ones((CTX_LEN, HEAD_DIM), F32), jnp.concatenate([cos, cos], axis=-1)], axis=0)
    sin_t = jnp.concatenate([jnp.zeros((CTX_LEN, HEAD_DIM), F32), jnp.concatenate([-sin, sin], axis=-1)], axis=0)
    return cos_t, sin_t


def kernel(x, c, ctx, c_ctx, w_mod, b_mod, g_pre_mix, g_post_mix, g_pre_mlp, g_post_mlp, w_in,
           ret_decay_logit, sgu_w_s, sgu_b_s, sgu_norm, w_branch_a, w_branch_b, w_branch_c, w_out,
           w_up, w_down):
    nb = x.shape[0]
    depth = w_mod.shape[0]
    assert x.shape == (nb, SEQ, D_MODEL) and ctx.shape == (nb, CTX_LEN, D_MODEL) and nb == 2

    cond_rows = jnp.concatenate([c, c_ctx[None, :], jnp.zeros((8 - nb - 1, D_MODEL), F32)], axis=0)
    mod = _modulation(cond_rows, w_mod, b_mod)
    cos_t, sin_t = _rope_tables()
    pavg = jnp.asarray(np.kron(np.eye(GMLP_W // GMLP_GROUP), np.full((GMLP_GROUP, GMLP_GROUP), 1.0 / GMLP_GROUP)), BF16)

    xc = jnp.concatenate([ctx, x], axis=1)
    for l in range(depth):
        last = l == depth - 1
        mod_l = mod[l].reshape(8, 1, 6 * D_MODEL)
        q, k, v, sg, f_lat, f_ctx, ug, vn, gates = _in_proj(
            xc, mod_l[:, :, :2 * D_MODEL], g_pre_mix[l][None, :], cos_t, sin_t, w_in[l].astype(BF16), pavg,
            sgu_norm[l][None, :])
        ret = _retention(ret_decay_logit[l].reshape(2 * HEADS, 1), q, k, v, sg)
        four_lat = _fourier_latent(f_lat)
        four_ctx = f_ctx if last else _fourier_ctx(f_ctx)
        bs_tab = jnp.repeat(jnp.transpose(sgu_b_s[l]), GMLP_GROUP, axis=1)
        xc = _mix_mlp(
            xc, ret, four_lat, four_ctx, ug, vn, gates, mod_l[:, :, 2 * D_MODEL:], sgu_w_s[l].astype(BF16), bs_tab,
            w_branch_a[l].astype(BF16), w_branch_b[l].astype(BF16), w_branch_c[l].astype(BF16),
            w_out[l].astype(BF16), g_post_mix[l][None, :], g_pre_mlp[l][None, :], g_post_mlp[l][None, :],
            w_up[l].astype(BF16), w_down[l].astype(BF16), ctx_out=not last)
    return xc
```

```python
import functools

import numpy as np
import jax
import jax.numpy as jnp
from jax import lax
from jax.experimental import pallas as pl
from jax.experimental.pallas import tpu as pltpu

F32 = jnp.float32
BF16 = jnp.bfloat16

D_MODEL = 1024
SEQ = 8192
CTX_LEN = 256
GRID_W = 64
RET_W = 512
HEADS = 4
HEAD_DIM = 128
ROPE_BASE = 10000.0
ROPE_FREQS = HEAD_DIM // 4
FNET_W = 256
FNET_GROUP = 64
GMLP_W = 256
GMLP_GROUP = 64
GMLP_CHUNK = 128
D_FF = 4 * D_MODEL
EPS = 1e-6
IN_W = 4 * RET_W + FNET_W + 2 * GMLP_W + 3 * D_MODEL
COL_F = 4 * RET_W
COL_U = COL_F + FNET_W
COL_VS = COL_U + GMLP_W
COL_GATE = COL_VS + GMLP_W

TM_LAT = 512
RCHUNK = 256
RET_TILE = 1024
RET_STEPS = SEQ // RET_TILE
RET_CPT = RET_TILE // RCHUNK
N_RCHUNKS = 1 + SEQ // RCHUNK
FF_CHUNK = 1024
FFT_N1 = 64
FFT_N2 = 128
HALF = 128

_VMEM_LIMIT = 56 * 1024 * 1024


def _dot(a, b):
    return jnp.dot(a, b, preferred_element_type=F32)


def _split(x):
    hi = x.astype(BF16)
    lo = (x - hi.astype(F32)).astype(BF16)
    return hi, lo


def _dot3(ah, al, bh, bl):
    return _dot(ah, bh) + _dot(al, bh) + _dot(ah, bl)


def _np_split(a64):
    hi = np.asarray(a64, np.float32).astype(BF16)
    lo = (np.asarray(a64, np.float32) - hi.astype(np.float32)).astype(BF16)
    return jnp.asarray(hi), jnp.asarray(lo)


def _rms(x):
    return x * lax.rsqrt(jnp.mean(x * x, axis=-1, keepdims=True) + EPS)


def _gelu(x):
    return x * (0.5 * (1.0 + jnp.tanh(0.7978845608028654 * (x + 0.044715 * (x * x * x)))))


def _sigmoid(x):
    return 1.0 / (1.0 + jnp.exp(-x))


def _const_spec(shape, nargs):
    zeros = (0,) * len(shape)
    if nargs == 1:
        return pl.BlockSpec(shape, lambda a: zeros, pipeline_mode=pl.Buffered(1))
    if nargs == 2:
        return pl.BlockSpec(shape, lambda a, b: zeros, pipeline_mode=pl.Buffered(1))
    return pl.BlockSpec(shape, lambda a, b, c: zeros, pipeline_mode=pl.Buffered(1))


def _params(n_axes):
    return pltpu.CompilerParams(dimension_semantics=("arbitrary",) * n_axes, vmem_limit_bytes=_VMEM_LIMIT)


def _mod_kernel(a_ref, w_ref, b_ref, o_ref):
    a = a_ref[...]
    a = a * _sigmoid(a)
    ah, al = _split(a)
    w = w_ref[0].astype(BF16)
    o_ref[0] = _dot(ah, w) + _dot(al, w) + b_ref[0]


def _modulation(cond_rows, w_mod, b_mod):
    depth = w_mod.shape[0]
    tn = 1536
    return pl.pallas_call(
        _mod_kernel,
        grid=(depth, (6 * D_MODEL) // tn),
        in_specs=[
            pl.BlockSpec((8, D_MODEL), lambda l, j: (0, 0)),
            pl.BlockSpec((1, D_MODEL, tn), lambda l, j: (l, 0, j)),
            pl.BlockSpec((1, 1, tn), lambda l, j: (l, 0, j)),
        ],
        out_specs=pl.BlockSpec((1, 8, tn), lambda l, j: (l, 0, j)),
        out_shape=jax.ShapeDtypeStruct((depth, 8, 6 * D_MODEL), F32),
        compiler_params=_params(2),
        name="modulation",
    )(cond_rows, w_mod, b_mod.reshape(depth, 1, 6 * D_MODEL))


def _in_proj_kernel(*refs, rope, kv_only):
    if rope:
        x_ref, mod_ref, gpre_ref, w_ref, pavg_ref, gn_ref, cos_ref, sin_ref = refs[:8]
        outs = refs[8:]
    else:
        x_ref, mod_ref, gpre_ref, w_ref, pavg_ref, gn_ref = refs[:6]
        outs = refs[6:]
    x = x_ref[0]
    shift = mod_ref[0, :, 0:D_MODEL]
    scale = mod_ref[0, :, D_MODEL:2 * D_MODEL]
    h = _rms(x) * gpre_ref[...]
    hb = (h * (1.0 + scale) + shift).astype(BF16)

    def proj(lo, hi):
        return _dot(hb, w_ref[:, lo:hi])

    def rope_store(z, dst):
        if not rope:
            dst[0] = z.astype(BF16)
            return
        cos = cos_ref[...]
        sin = sin_ref[...]
        for hd in range(HEADS):
            a = z[:, hd * HEAD_DIM:(hd + 1) * HEAD_DIM]
            r = a * cos + pltpu.roll(a, HEAD_DIM // 2, 1) * sin
            dst[0, :, hd * HEAD_DIM:(hd + 1) * HEAD_DIM] = r.astype(BF16)

    if kv_only:
        k_ref, v_ref = outs
    else:
        q_ref, k_ref, v_ref, sg_ref, f_ref, u_ref, vn_ref, gate_ref = outs
        rope_store(proj(0, RET_W), q_ref)
    rope_store(proj(RET_W, 2 * RET_W) * (HEAD_DIM ** -0.5), k_ref)
    v_ref[0] = proj(2 * RET_W, 3 * RET_W).astype(BF16)
    if kv_only:
        return
    g = proj(3 * RET_W, 4 * RET_W)
    sg_ref[0] = (g * _sigmoid(g)).astype(BF16)

    f = proj(COL_F, COL_U)
    for hf in range(2):
        f_ref[0, hf] = f[:, hf * HALF:(hf + 1) * HALF]

    u_ref[0] = _gelu(proj(COL_U, COL_VS)).astype(BF16)
    vg = _gelu(proj(COL_VS, COL_GATE))
    sh, sl = _split(vg * vg)
    pavg = pavg_ref[...]
    ms = _dot(sh, pavg) + _dot(sl, pavg)
    vn_ref[0] = (vg * lax.rsqrt(ms + EPS) * gn_ref[...]).astype(BF16)

    for c in range(3):
        z = proj(COL_GATE + c * D_MODEL, COL_GATE + (c + 1) * D_MODEL)
        gate_ref[0, :, c * D_MODEL:(c + 1) * D_MODEL] = _sigmoid(z).astype(BF16)


def _in_proj(xs, mod_l, g_pre, w_in, pavg, g_norm, rope_tabs, *, tm, ctx, kv_only=False):
    nb, rows, _ = xs.shape
    rope = rope_tabs is not None
    tok_spec = lambda w: pl.BlockSpec((1, tm, w), lambda b, i: (b, i, 0))
    mod_idx = (lambda b, i: (2, 0, 0)) if ctx else (lambda b, i: (b, 0, 0))
    in_specs = [
        tok_spec(D_MODEL),
        pl.BlockSpec((1, 1, 2 * D_MODEL), mod_idx),
        _const_spec((1, D_MODEL), 2),
        _const_spec((D_MODEL, IN_W), 2),
        _const_spec((GMLP_W, GMLP_W), 2),
        _const_spec((1, GMLP_W), 2),
    ]
    args = [xs, mod_l, g_pre, w_in, pavg, g_norm]
    if rope:
        in_specs += [pl.BlockSpec((tm, HEAD_DIM), lambda b, i: (i, 0))] * 2
        args += list(rope_tabs)
    bf = lambda w: jax.ShapeDtypeStruct((nb, rows, w), BF16)
    if kv_only:
        out_shapes = (bf(RET_W), bf(RET_W))
        out_specs = (tok_spec(RET_W), tok_spec(RET_W))
    else:
        out_shapes = (
            bf(RET_W), bf(RET_W), bf(RET_W), bf(RET_W),
            jax.ShapeDtypeStruct((nb, 2, rows, HALF), F32),
            bf(GMLP_W), bf(GMLP_W),
            bf(3 * D_MODEL),
        )
        out_specs = (
            tok_spec(RET_W), tok_spec(RET_W), tok_spec(RET_W), tok_spec(RET_W),
            pl.BlockSpec((1, 2, tm, HALF), lambda b, i: (b, 0, i, 0)),
            tok_spec(GMLP_W), tok_spec(GMLP_W), tok_spec(3 * D_MODEL),
        )
    return pl.pallas_call(
        functools.partial(_in_proj_kernel, rope=rope, kv_only=kv_only),
        grid=(nb, rows // tm),
        in_specs=in_specs,
        out_specs=out_specs,
        out_shape=out_shapes,
        compiler_params=_params(2),
        name="in_proj_ctx" if ctx else "in_proj",
    )(*args)


def _ret_kernel(*refs, ctx_out):
    (logit_ref, ql_ref, kl_ref, vl_ref, sgl_ref, kc_ref, vc_ref) = refs[:7]
    if ctx_out:
        qc_ref, sgc_ref, ol_ref, oc_ref = refs[7:11]
        scratch = refs[11:]
    else:
        ol_ref = refs[7]
        scratch = refs[8:]
    mask_ref, wf_ref, wb_ref, qf_ref, qb_ref, df_ref, db_ref, sf_ref, sb_ref, sball_ref = scratch
    b = pl.program_id(0)
    phase = pl.program_id(1)
    j = pl.program_id(2)

    @pl.when((b == 0) & (phase == 0) & (j == 0))
    def _():
        x = logit_ref[...]
        lg = -(jnp.maximum(-x, 0.0) + jnp.log(1.0 + jnp.exp(-jnp.abs(x))))
        row = lax.broadcasted_iota(jnp.int32, (RCHUNK, RCHUNK), 0).astype(F32)
        col = lax.broadcasted_iota(jnp.int32, (RCHUNK, RCHUNK), 1).astype(F32)
        diff = row - col
        pos = row[:, :HEAD_DIM]
        for hd in range(HEADS):
            lf = lg[hd:hd + 1, :]
            lb = lg[HEADS + hd:HEADS + hd + 1, :]
            mask_ref[hd] = jnp.where(diff >= 0.0, jnp.exp(lf * jnp.maximum(diff, 0.0)),
                                     jnp.exp(lb * jnp.maximum(-diff, 0.0)))
            wf_ref[hd] = jnp.exp(lf * (RCHUNK - 1.0 - pos))
            wb_ref[hd] = jnp.exp(lb * pos)
            qf_ref[hd] = jnp.exp(lf * (pos + 1.0))
            qb_ref[hd] = jnp.exp(lb * (RCHUNK - pos))
            df_ref[hd] = jnp.exp(jnp.broadcast_to(lf, (HEAD_DIM, HEAD_DIM)) * float(RCHUNK))
            db_ref[hd] = jnp.exp(jnp.broadcast_to(lb, (HEAD_DIM, HEAD_DIM)) * float(RCHUNK))

    def absorb(s, hd, kh, vh, w_ref, d_ref):
        kw = (kh.astype(F32) * w_ref[hd]).astype(BF16)
        u = lax.dot_general(kw, vh, (((0,), (0,)), ((), ())), preferred_element_type=F32)
        return d_ref[hd] * s + u

    def backward_chunks(k_ref, v_ref, n_chunks, first_chunk):
        for hd in range(HEADS):
            cols = slice(hd * HEAD_DIM, (hd + 1) * HEAD_DIM)
            s = sb_ref[hd]
            for c in reversed(range(n_chunks)):
                rows = slice(c * RCHUNK, (c + 1) * RCHUNK)
                sball_ref[first_chunk + c, hd] = s.astype(BF16)
                s = absorb(s, hd, k_ref[0, rows, cols], v_ref[0, rows, cols], wb_ref, db_ref)
            sb_ref[hd] = s

    def forward_chunks(q_ref, k_ref, v_ref, sg_ref, o_ref, n_chunks, first_chunk):
        for hd in range(HEADS):
            cols = slice(hd * HEAD_DIM, (hd + 1) * HEAD_DIM)
            s = sf_ref[hd]
            for c in range(n_chunks):
                rows = slice(c * RCHUNK, (c + 1) * RCHUNK)
                kh = k_ref[0, rows, cols]
                vh = v_ref[0, rows, cols]
                if o_ref is not None:
                    qh = q_ref[0, rows, cols]
                    sc = lax.dot_general(qh, kh, (((1,), (1,)), ((), ())), preferred_element_type=F32)
                    qf = qh.astype(F32)
                    lhs = jnp.concatenate([(sc * mask_ref[hd]).astype(BF16),
                                           (qf * qf_ref[hd]).astype(BF16),
                                           (qf * qb_ref[hd]).astype(BF16)], axis=1)
                    rhs = jnp.concatenate([vh, s.astype(BF16), sball_ref[first_chunk + c, hd]], axis=0)
                    o = _dot(lhs, rhs)
                    o_ref[0, rows, cols] = (_rms(o) * sg_ref[0, rows, cols].astype(F32)).astype(BF16)
                s = absorb(s, hd, kh, vh, wf_ref, df_ref)
            sf_ref[hd] = s

    @pl.when(phase == 0)
    def _():
        @pl.when(j == 0)
        def _():
            sb_ref[...] = jnp.zeros_like(sb_ref)
            backward_chunks(kc_ref, vc_ref, 1, 0)

        backward_chunks(kl_ref, vl_ref, RET_CPT, 1 + (RET_STEPS - 1 - j) * RET_CPT)

    @pl.when(phase == 1)
    def _():
        @pl.when(j == 0)
        def _():
            sf_ref[...] = jnp.zeros_like(sf_ref)
            if ctx_out:
                forward_chunks(qc_ref, kc_ref, vc_ref, sgc_ref, oc_ref, 1, 0)
            else:
                forward_chunks(None, kc_ref, vc_ref, None, None, 1, 0)

        forward_chunks(ql_ref, kl_ref, vl_ref, sgl_ref, ol_ref, RET_CPT, 1 + j * RET_CPT)


def _retention(logits, lat, ctx, *, ctx_out):
    nb = lat[0].shape[0]
    kv_idx = lambda b, p, j: (b, jnp.where(p == 0, RET_STEPS - 1 - j, j), 0)
    q_idx = lambda b, p, j: (b, jnp.where(p == 0, 0, j), 0)
    c_idx = lambda b, p, j: (b, 0, 0)
    lat_blk = (1, RET_TILE, RET_W)
    ctx_blk = (1, CTX_LEN, RET_W)
    in_specs = [
        _const_spec((2 * HEADS, 1), 3),
        pl.BlockSpec(lat_blk, q_idx), pl.BlockSpec(lat_blk, kv_idx), pl.BlockSpec(lat_blk, kv_idx),
        pl.BlockSpec(lat_blk, q_idx),
        pl.BlockSpec(ctx_blk, c_idx), pl.BlockSpec(ctx_blk, c_idx),
    ]
    if ctx_out:
        qc, kc, vc, sgc = ctx
        args = [logits, *lat, kc, vc, qc, sgc]
        in_specs += [pl.BlockSpec(ctx_blk, c_idx), pl.BlockSpec(ctx_blk, c_idx)]
        out_specs = (pl.BlockSpec(lat_blk, q_idx), pl.BlockSpec(ctx_blk, c_idx))
        out_shape = (jax.ShapeDtypeStruct((nb, SEQ, RET_W), BF16), jax.ShapeDtypeStruct((nb, CTX_LEN, RET_W), BF16))
    else:
        kc, vc = ctx
        args = [logits, *lat, kc, vc]
        out_specs = pl.BlockSpec(lat_blk, q_idx)
        out_shape = jax.ShapeDtypeStruct((nb, SEQ, RET_W), BF16)
    state = pltpu.VMEM((HEADS, HEAD_DIM, HEAD_DIM), F32)
    pos_tab = pltpu.VMEM((HEADS, RCHUNK, HEAD_DIM), F32)
    return pl.pallas_call(
        functools.partial(_ret_kernel, ctx_out=ctx_out),
        grid=(nb, 2, RET_STEPS),
        in_specs=in_specs,
        out_specs=out_specs,
        out_shape=out_shape,
        scratch_shapes=[
            pltpu.VMEM((HEADS, RCHUNK, RCHUNK), F32),
            pos_tab, pos_tab, pos_tab, pos_tab,
            state, state,
            state, state,
            pltpu.VMEM((N_RCHUNKS, HEADS, HEAD_DIM, HEAD_DIM), BF16),
        ],
        compiler_params=_params(3),
        name="retention",
    )(*args)


def _dft_cos_sin(n):
    idx = np.arange(n, dtype=np.float64)
    ang = 2.0 * np.pi * ((idx[:, None] * idx[None, :]) % n) / n
    return np.cos(ang), np.sin(ang)


def _channel_dft(scale):
    c, s = _dft_cos_sin(FNET_GROUP)
    eye = np.eye(FNET_W // FNET_GROUP)
    return np.concatenate([np.kron(eye, c), np.kron(eye, s)], axis=0) * scale


def _fft_constants():
    k1 = np.arange(FFT_N1, dtype=np.float64)[None, :, None]
    n1 = np.arange(FFT_N1, dtype=np.float64)[None, None, :]
    n2 = np.arange(FFT_N2, dtype=np.float64)[:, None, None]
    ang = 2.0 * np.pi * ((k1 * (FFT_N2 * n1 + n2)) % SEQ) / SEQ
    m1 = np.concatenate([np.cos(ang), -np.sin(ang)], axis=1)
    c, s = _dft_cos_sin(FFT_N2)
    m3 = np.block([[c, s], [-s, c]])
    return _np_split(m1), _np_split(m3), _np_split(_channel_dft((SEQ * FNET_GROUP) ** -0.5))


def _ctx_fft_constants():
    c, s = _dft_cos_sin(CTX_LEN)
    return (_np_split(np.concatenate([c, -s], axis=0)),
            _np_split(_channel_dft((CTX_LEN * FNET_GROUP) ** -0.5)))


def _fft1_kernel(x_ref, mh_ref, ml_ref, o_ref):
    for jj in range(8):
        x = jnp.concatenate([x_ref[hf, :, jj, :] for hf in range(2)], axis=1)
        xh, xl = _split(x)
        a = _dot3(mh_ref[jj], ml_ref[jj], xh, xl)
        for hf in range(2):
            for ri in range(2):
                o_ref[hf, ri, jj] = a[ri * FFT_N1:(ri + 1) * FFT_N1, hf * HALF:(hf + 1) * HALF]


def _fft3_kernel(a_ref, wh_ref, wl_ref, ch_ref, cl_ref, o_ref):
    for jj in range(8):
        rhs = jnp.concatenate(
            [jnp.concatenate([a_ref[hf, ri, :, jj, :] for ri in range(2)], axis=0) for hf in range(2)],
            axis=1)
        rh, rl = _split(rhs)
        y = _dot3(wh_ref[...], wl_ref[...], rh, rl)
        zh, zl = _split(jnp.concatenate([y[:FFT_N2], y[FFT_N2:]], axis=1))
        four = _dot3(zh, zl, ch_ref[...], cl_ref[...])
        for hf in range(2):
            o_ref[hf, :, jj, :] = four[:, hf * HALF:(hf + 1) * HALF]


def _fourier_latent(f_lat):
    nb = f_lat.shape[0]
    (m1h, m1l), (m3h, m3l), (cdh, cdl) = _fft_constants()
    a = pl.pallas_call(
        _fft1_kernel,
        grid=(nb, FFT_N2 // 8),
        in_specs=[
            pl.BlockSpec((None, 2, FFT_N1, 8, HALF), lambda b, j: (b, 0, 0, j, 0)),
            pl.BlockSpec((8, 2 * FFT_N1, FFT_N1), lambda b, j: (j, 0, 0)),
            pl.BlockSpec((8, 2 * FFT_N1, FFT_N1), lambda b, j: (j, 0, 0)),
        ],
        out_specs=pl.BlockSpec((None, 2, 2, 8, FFT_N1, HALF), lambda b, j: (b, 0, 0, j, 0, 0)),
        out_shape=jax.ShapeDtypeStruct((nb, 2, 2, FFT_N2, FFT_N1, HALF), F32),
        compiler_params=_params(2),
        name="fourier_stage1",
    )(f_lat.reshape(nb, 2, FFT_N1, FFT_N2, HALF), m1h, m1l)
    out = pl.pallas_call(
        _fft3_kernel,
        grid=(nb, FFT_N1 // 8),
        in_specs=[
            pl.BlockSpec((None, 2, 2, FFT_N2, 8, HALF), lambda b, j: (b, 0, 0, 0, j, 0)),
            _const_spec((2 * FFT_N2, 2 * FFT_N2), 2), _const_spec((2 * FFT_N2, 2 * FFT_N2), 2),
            _const_spec((2 * FNET_W, FNET_W), 2), _const_spec((2 * FNET_W, FNET_W), 2),
        ],
        out_specs=pl.BlockSpec((None, 2, FFT_N2, 8, HALF), lambda b, j: (b, 0, 0, j, 0)),
        out_shape=jax.ShapeDtypeStruct((nb, 2, FFT_N2, FFT_N1, HALF), F32),
        compiler_params=_params(2),
        name="fourier_stage2",
    )(a, m3h, m3l, cdh, cdl)
    return out.reshape(nb, 2, SEQ, HALF)


def _fft_ctx_kernel(x_ref, wh_ref, wl_ref, ch_ref, cl_ref, o_ref):
    x = jnp.concatenate([x_ref[hf] for hf in range(2)], axis=1)
    xh, xl = _split(x)
    z = _dot3(wh_ref[...], wl_ref[...], xh, xl)
    zh, zl = _split(jnp.concatenate([z[:CTX_LEN], z[CTX_LEN:]], axis=1))
    four = _dot3(zh, zl, ch_ref[...], cl_ref[...])
    for hf in range(2):
        o_ref[hf] = four[:, hf * HALF:(hf + 1) * HALF]


def _fourier_ctx(f_ctx):
    nb = f_ctx.shape[0]
    (wh, wl), (cdh, cdl) = _ctx_fft_constants()
    return pl.pallas_call(
        _fft_ctx_kernel,
        grid=(nb,),
        in_specs=[
            pl.BlockSpec((None, 2, CTX_LEN, HALF), lambda b: (b, 0, 0, 0)),
            _const_spec((2 * CTX_LEN, CTX_LEN), 1), _const_spec((2 * CTX_LEN, CTX_LEN), 1),
            _const_spec((2 * FNET_W, FNET_W), 1), _const_spec((2 * FNET_W, FNET_W), 1),
        ],
        out_specs=pl.BlockSpec((None, 2, CTX_LEN, HALF), lambda b: (b, 0, 0, 0)),
        out_shape=jax.ShapeDtypeStruct((nb, 2, CTX_LEN, HALF), F32),
        compiler_params=_params(1),
        name="fourier_ctx",
    )(f_ctx, wh, wl, cdh, cdl)


def _mix_mlp_kernel(x_ref, ret_ref, four_ref, u_ref, vn_ref, gate_ref, mod_ref, ws_ref, bs_ref,
                    wa_ref, wb_ref, wc_ref, wo_ref, gpm_ref, gprm_ref, gpom_ref, wup_ref, wdn_ref, o_ref, *, tm):
    group = lax.broadcasted_iota(jnp.int32, (GMLP_CHUNK, GMLP_W), 1) // GMLP_GROUP
    parts = []
    for cc in range(tm // GMLP_CHUNK):
        rows = slice(cc * GMLP_CHUNK, (cc + 1) * GMLP_CHUNK)
        vn = vn_ref[0, rows, :]
        s = bs_ref[...]
        for g in range(GMLP_W // GMLP_GROUP):
            s = s + jnp.where(group == g, _dot(ws_ref[g], vn), 0.0)
        parts.append((u_ref[0, rows, :].astype(F32) * s).astype(BF16))
    sgu = jnp.concatenate(parts, axis=0)

    four = jnp.concatenate([four_ref[0, hf] for hf in range(2)], axis=1).astype(BF16)
    ga = gate_ref[0, :, 0:D_MODEL].astype(F32)
    gb = gate_ref[0, :, D_MODEL:2 * D_MODEL].astype(F32)
    gc = gate_ref[0, :, 2 * D_MODEL:3 * D_MODEL].astype(F32)
    merged = (ga * _dot(ret_ref[0], wa_ref[...]) + gb * _dot(four, wb_ref[...]) + gc * _dot(sgu, wc_ref[...]))
    y = _dot(merged.astype(BF16), wo_ref[...])

    gate1 = mod_ref[0, :, 0:D_MODEL]
    shift2 = mod_ref[0, :, D_MODEL:2 * D_MODEL]
    scale2 = mod_ref[0, :, 2 * D_MODEL:3 * D_MODEL]
    gate2 = mod_ref[0, :, 3 * D_MODEL:4 * D_MODEL]
    x1 = x_ref[0] + gate1 * (_rms(y) * gpm_ref[...])
    h2 = ((_rms(x1) * gprm_ref[...]) * (1.0 + scale2) + shift2).astype(BF16)
    m = None
    for c in range(D_FF // FF_CHUNK):
        cols = slice(c * FF_CHUNK, (c + 1) * FF_CHUNK)
        a = jnp.maximum(_dot(h2, wup_ref[:, cols]), 0.0)
        part = _dot((a * a).astype(BF16), wdn_ref[cols, :])
        m = part if m is None else m + part
    o_ref[0] = x1 + gate2 * (_rms(m) * gpom_ref[...])


def _mix_mlp(xs, ret, four, ug, vn, gates, mod_l, ws, bs_tab, wa, wb, wc, wo,
             g_post_mix, g_pre_mlp, g_post_mlp, w_up, w_down, *, tm, ctx):
    nb, rows, _ = xs.shape
    tok_spec = lambda w: pl.BlockSpec((1, tm, w), lambda b, i: (b, i, 0))
    mod_idx = (lambda b, i: (2, 0, 0)) if ctx else (lambda b, i: (b, 0, 0))
    return pl.pallas_call(
        functools.partial(_mix_mlp_kernel, tm=tm),
        grid=(nb, rows // tm),
        in_specs=[
            tok_spec(D_MODEL), tok_spec(RET_W),
            pl.BlockSpec((1, 2, tm, HALF), lambda b, i: (b, 0, i, 0)),
            tok_spec(GMLP_W), tok_spec(GMLP_W), tok_spec(3 * D_MODEL),
            pl.BlockSpec((1, 1, 4 * D_MODEL), mod_idx),
            _const_spec((GMLP_W // GMLP_GROUP, GMLP_CHUNK, GMLP_CHUNK), 2),
            _const_spec((GMLP_CHUNK, GMLP_W), 2),
            _const_spec((RET_W, D_MODEL), 2), _const_spec((FNET_W, D_MODEL), 2),
            _const_spec((GMLP_W, D_MODEL), 2), _const_spec((D_MODEL, D_MODEL), 2),
            _const_spec((1, D_MODEL), 2), _const_spec((1, D_MODEL), 2), _const_spec((1, D_MODEL), 2),
            _const_spec((D_MODEL, D_FF), 2), _const_spec((D_FF, D_MODEL), 2),
        ],
        out_specs=tok_spec(D_MODEL),
        out_shape=jax.ShapeDtypeStruct((nb, rows, D_MODEL), F32),
        compiler_params=_params(2),
        name="mix_mlp_ctx" if ctx else "mix_mlp",
    )(xs, ret, four, ug, vn, gates, mod_l, ws, bs_tab, wa, wb, wc, wo,
      g_post_mix, g_pre_mlp, g_post_mlp, w_up, w_down)


def _rope_tables():
    rows = SEQ // GRID_W
    row = jnp.repeat(jnp.arange(rows, dtype=F32), GRID_W)
    col = jnp.tile(jnp.arange(GRID_W, dtype=F32), rows)
    freqs = ROPE_BASE ** (-jnp.arange(ROPE_FREQS, dtype=F32) / ROPE_FREQS)
    ang = jnp.concatenate([row[:, None] * freqs, col[:, None] * freqs], axis=-1)
    cos = jnp.cos(ang)
    sin = jnp.sin(ang)
    return jnp.concatenate([cos, cos], axis=-1), jnp.concatenate([-sin, sin], axis=-1)


def kernel(x, c, ctx, c_ctx, w_mod, b_mod, g_pre_mix, g_post_mix, g_pre_mlp, g_post_mlp, w_in,
           ret_decay_logit, sgu_w_s, sgu_b_s, sgu_norm, w_branch_a, w_branch_b, w_branch_c, w_out,
           w_up, w_down):
    nb = x.shape[0]
    depth = w_mod.shape[0]
    assert x.shape == (nb, SEQ, D_MODEL) and ctx.shape == (nb, CTX_LEN, D_MODEL) and nb == 2

    cond_rows = jnp.concatenate([c, c_ctx[None, :], jnp.zeros((8 - nb - 1, D_MODEL), F32)], axis=0)
    mod = _modulation(cond_rows, w_mod, b_mod)
    rope_tabs = _rope_tables()
    pavg = jnp.asarray(np.kron(np.eye(GMLP_W // GMLP_GROUP), np.full((GMLP_GROUP, GMLP_GROUP), 1.0 / GMLP_GROUP)), BF16)

    for l in range(depth):
        last = l == depth - 1
        mod_l = mod[l].reshape(8, 1, 6 * D_MODEL)
        mod_in, mod_out = mod_l[:, :, :2 * D_MODEL], mod_l[:, :, 2 * D_MODEL:]
        w_in_l = w_in[l].astype(BF16)
        g_pre = g_pre_mix[l][None, :]
        g_norm = sgu_norm[l][None, :]
        logits = ret_decay_logit[l].reshape(2 * HEADS, 1)

        q, k, v, sg, f_lat, ug, vn, gates = _in_proj(
            x, mod_in, g_pre, w_in_l, pavg, g_norm, rope_tabs, tm=TM_LAT, ctx=False)
        if last:
            kc, vc = _in_proj(ctx, mod_in, g_pre, w_in_l, pavg, g_norm, None, tm=CTX_LEN, ctx=True, kv_only=True)
            ret = _retention(logits, (q, k, v, sg), (kc, vc), ctx_out=False)
        else:
            qc, kc, vc, sgc, f_ctx, ugc, vnc, gatesc = _in_proj(
                ctx, mod_in, g_pre, w_in_l, pavg, g_norm, None, tm=CTX_LEN, ctx=True)
            ret, retc = _retention(logits, (q, k, v, sg), (qc, kc, vc, sgc), ctx_out=True)

        mix_w = (sgu_w_s[l].astype(BF16), jnp.repeat(jnp.transpose(sgu_b_s[l]), GMLP_GROUP, axis=1),
                 w_branch_a[l].astype(BF16), w_branch_b[l].astype(BF16), w_branch_c[l].astype(BF16),
                 w_out[l].astype(BF16), g_post_mix[l][None, :], g_pre_mlp[l][None, :], g_post_mlp[l][None, :],
                 w_up[l].astype(BF16), w_down[l].astype(BF16))
        x = _mix_mlp(x, ret, _fourier_latent(f_lat), ug, vn, gates, mod_out, *mix_w, tm=TM_LAT, ctx=False)
        if not last:
            ctx = _mix_mlp(ctx, retc, _fourier_ctx(f_ctx), ugc, vnc, gatesc, mod_out, *mix_w, tm=CTX_LEN, ctx=True)
    return x
```

```python
import functools

import numpy as np
import jax
import jax.numpy as jnp
from jax import lax
from jax.experimental import pallas as pl
from jax.experimental.pallas import tpu as pltpu

F32 = jnp.float32
BF16 = jnp.bfloat16

D_MODEL = 1024
SEQ = 8192
CTX_LEN = 256
GRID_W = 64
RET_W = 512
HEADS = 4
HEAD_DIM = 128
ROPE_BASE = 10000.0
ROPE_FREQS = HEAD_DIM // 4
FNET_W = 256
FNET_GROUP = 64
GMLP_W = 256
GMLP_GROUP = 64
GMLP_CHUNK = 128
D_FF = 4 * D_MODEL
EPS = 1e-6
IN_W = 4 * RET_W + FNET_W + 2 * GMLP_W + 3 * D_MODEL
COL_F = 4 * RET_W
COL_U = COL_F + FNET_W
COL_VS = COL_U + GMLP_W
COL_GATE = COL_VS + GMLP_W

TM_LAT = 512
RCHUNK = 256
RET_TILE = 1024
RET_STEPS = SEQ // RET_TILE
RET_CPT = RET_TILE // RCHUNK
N_RCHUNKS = 1 + SEQ // RCHUNK
FF_CHUNK = 1024
FFT_N1 = 64
FFT_N2 = 128
HALF = 128
FFT_STEP = 16

_VMEM_LIMIT = 56 * 1024 * 1024


def _dot(a, b):
    return jnp.dot(a, b, preferred_element_type=F32)


def _split(x):
    hi = x.astype(BF16)
    lo = (x - hi.astype(F32)).astype(BF16)
    return hi, lo


def _np_split(a64):
    hi = np.asarray(a64, np.float32).astype(BF16)
    lo = (np.asarray(a64, np.float32) - hi.astype(np.float32)).astype(BF16)
    return jnp.asarray(hi), jnp.asarray(lo)


def _rms(x):
    return x * lax.rsqrt(jnp.mean(x * x, axis=-1, keepdims=True) + EPS)


def _gelu(x):
    return x * (0.5 * (1.0 + jnp.tanh(0.7978845608028654 * (x + 0.044715 * (x * x * x)))))


def _sigmoid(x):
    return 1.0 / (1.0 + jnp.exp(-x))


def _const_spec(shape, nargs):
    zeros = (0,) * len(shape)
    if nargs == 1:
        return pl.BlockSpec(shape, lambda a: zeros, pipeline_mode=pl.Buffered(1))
    if nargs == 2:
        return pl.BlockSpec(shape, lambda a, b: zeros, pipeline_mode=pl.Buffered(1))
    return pl.BlockSpec(shape, lambda a, b, c: zeros, pipeline_mode=pl.Buffered(1))


def _params(n_axes):
    return pltpu.CompilerParams(dimension_semantics=("arbitrary",) * n_axes, vmem_limit_bytes=_VMEM_LIMIT)


def _mod_kernel(a_ref, w_ref, b_ref, o_ref):
    a = a_ref[...]
    a = a * _sigmoid(a)
    ah, al = _split(a)
    w = w_ref[0].astype(BF16)
    o_ref[0] = _dot(ah, w) + _dot(al, w) + b_ref[0]


def _modulation(cond_rows, w_mod, b_mod):
    depth = w_mod.shape[0]
    tn = 1536
    return pl.pallas_call(
        _mod_kernel,
        grid=(depth, (6 * D_MODEL) // tn),
        in_specs=[
            pl.BlockSpec((8, D_MODEL), lambda l, j: (0, 0)),
            pl.BlockSpec((1, D_MODEL, tn), lambda l, j: (l, 0, j)),
            pl.BlockSpec((1, 1, tn), lambda l, j: (l, 0, j)),
        ],
        out_specs=pl.BlockSpec((1, 8, tn), lambda l, j: (l, 0, j)),
        out_shape=jax.ShapeDtypeStruct((depth, 8, 6 * D_MODEL), F32),
        compiler_params=_params(2),
        name="modulation",
    )(cond_rows, w_mod, b_mod.reshape(depth, 1, 6 * D_MODEL))


def _in_proj_kernel(*refs, rope, kv_only):
    if rope:
        x_ref, mod_ref, gpre_ref, w_ref, pavg_ref, gn_ref, cos_ref, sin_ref = refs[:8]
        outs = refs[8:]
    else:
        x_ref, mod_ref, gpre_ref, w_ref, pavg_ref, gn_ref = refs[:6]
        outs = refs[6:]
    x = x_ref[0]
    shift = mod_ref[0, :, 0:D_MODEL]
    scale = mod_ref[0, :, D_MODEL:2 * D_MODEL]
    h = _rms(x) * gpre_ref[...]
    hb = (h * (1.0 + scale) + shift).astype(BF16)

    def proj(lo, hi):
        return _dot(hb, w_ref[:, lo:hi])

    def rope_store(z, dst):
        if not rope:
            dst[0] = z.astype(BF16)
            return
        cos = cos_ref[...]
        sin = sin_ref[...]
        for hd in range(HEADS):
            a = z[:, hd * HEAD_DIM:(hd + 1) * HEAD_DIM]
            r = a * cos + pltpu.roll(a, HEAD_DIM // 2, 1) * sin
            dst[0, :, hd * HEAD_DIM:(hd + 1) * HEAD_DIM] = r.astype(BF16)

    if kv_only:
        k_ref, v_ref = outs
    else:
        q_ref, k_ref, v_ref, sg_ref, f_ref, u_ref, vn_ref, gate_ref = outs
        rope_store(proj(0, RET_W), q_ref)
    rope_store(proj(RET_W, 2 * RET_W) * (HEAD_DIM ** -0.5), k_ref)
    v_ref[0] = proj(2 * RET_W, 3 * RET_W).astype(BF16)
    if kv_only:
        return
    g = proj(3 * RET_W, 4 * RET_W)
    sg_ref[0] = (g * _sigmoid(g)).astype(BF16)

    f = proj(COL_F, COL_U)
    for hf in range(2):
        f_ref[0, hf] = f[:, hf * HALF:(hf + 1) * HALF]

    u_ref[0] = _gelu(proj(COL_U, COL_VS)).astype(BF16)
    vg = _gelu(proj(COL_VS, COL_GATE))
    sh, sl = _split(vg * vg)
    pavg = pavg_ref[...]
    ms = _dot(sh, pavg) + _dot(sl, pavg)
    vn_ref[0] = (vg * lax.rsqrt(ms + EPS) * gn_ref[...]).astype(BF16)

    for c in range(3):
        z = proj(COL_GATE + c * D_MODEL, COL_GATE + (c + 1) * D_MODEL)
        gate_ref[0, :, c * D_MODEL:(c + 1) * D_MODEL] = _sigmoid(z).astype(BF16)


def _in_proj(xs, mod_l, g_pre, w_in, pavg, g_norm, rope_tabs, *, tm, ctx, kv_only=False):
    nb, rows, _ = xs.shape
    rope = rope_tabs is not None
    tok_spec = lambda w: pl.BlockSpec((1, tm, w), lambda b, i: (b, i, 0))
    mod_idx = (lambda b, i: (2, 0, 0)) if ctx else (lambda b, i: (b, 0, 0))
    in_specs = [
        tok_spec(D_MODEL),
        pl.BlockSpec((1, 1, 2 * D_MODEL), mod_idx),
        _const_spec((1, D_MODEL), 2),
        _const_spec((D_MODEL, IN_W), 2),
        _const_spec((GMLP_W, GMLP_W), 2),
        _const_spec((1, GMLP_W), 2),
    ]
    args = [xs, mod_l, g_pre, w_in, pavg, g_norm]
    if rope:
        in_specs += [pl.BlockSpec((tm, HEAD_DIM), lambda b, i: (i, 0))] * 2
        args += list(rope_tabs)
    bf = lambda w: jax.ShapeDtypeStruct((nb, rows, w), BF16)
    if kv_only:
        out_shapes = (bf(RET_W), bf(RET_W))
        out_specs = (tok_spec(RET_W), tok_spec(RET_W))
    else:
        out_shapes = (
            bf(RET_W), bf(RET_W), bf(RET_W), bf(RET_W),
            jax.ShapeDtypeStruct((nb, 2, rows, HALF), F32),
            bf(GMLP_W), bf(GMLP_W),
            bf(3 * D_MODEL),
        )
        out_specs = (
            tok_spec(RET_W), tok_spec(RET_W), tok_spec(RET_W), tok_spec(RET_W),
            pl.BlockSpec((1, 2, tm, HALF), lambda b, i: (b, 0, i, 0)),
            tok_spec(GMLP_W), tok_spec(GMLP_W), tok_spec(3 * D_MODEL),
        )
    return pl.pallas_call(
        functools.partial(_in_proj_kernel, rope=rope, kv_only=kv_only),
        grid=(nb, rows // tm),
        in_specs=in_specs,
        out_specs=out_specs,
        out_shape=out_shapes,
        compiler_params=_params(2),
        name="in_proj_ctx" if ctx else "in_proj",
    )(*args)


def _ret_kernel(*refs, ctx_out):
    (logit_ref, ql_ref, kl_ref, vl_ref, sgl_ref, kc_ref, vc_ref) = refs[:7]
    if ctx_out:
        qc_ref, sgc_ref, ol_ref, oc_ref = refs[7:11]
        scratch = refs[11:]
    else:
        ol_ref = refs[7]
        scratch = refs[8:]
    mask_ref, wf_ref, wb_ref, qf_ref, qb_ref, df_ref, db_ref, sf_ref, sb_ref, sball_ref = scratch
    b = pl.program_id(0)
    phase = pl.program_id(1)
    j = pl.program_id(2)

    @pl.when((b == 0) & (phase == 0) & (j == 0))
    def _():
        x = logit_ref[...]
        lg = -(jnp.maximum(-x, 0.0) + jnp.log(1.0 + jnp.exp(-jnp.abs(x))))
        row = lax.broadcasted_iota(jnp.int32, (RCHUNK, RCHUNK), 0).astype(F32)
        col = lax.broadcasted_iota(jnp.int32, (RCHUNK, RCHUNK), 1).astype(F32)
        diff = row - col
        pos = row[:, :HEAD_DIM]
        for hd in range(HEADS):
            lf = lg[hd:hd + 1, :]
            lb = lg[HEADS + hd:HEADS + hd + 1, :]
            mask_ref[hd] = jnp.where(diff >= 0.0, jnp.exp(lf * jnp.maximum(diff, 0.0)),
                                     jnp.exp(lb * jnp.maximum(-diff, 0.0)))
            wf_ref[hd] = jnp.exp(lf * (RCHUNK - 1.0 - pos))
            wb_ref[hd] = jnp.exp(lb * pos)
            qf_ref[hd] = jnp.exp(lf * (pos + 1.0))
            qb_ref[hd] = jnp.exp(lb * (RCHUNK - pos))
            df_ref[hd] = jnp.exp(jnp.broadcast_to(lf, (HEAD_DIM, HEAD_DIM)) * float(RCHUNK))
            db_ref[hd] = jnp.exp(jnp.broadcast_to(lb, (HEAD_DIM, HEAD_DIM)) * float(RCHUNK))

    def absorb(s, hd, kh, vh, w_ref, d_ref):
        kw = (kh.astype(F32) * w_ref[hd]).astype(BF16)
        u = lax.dot_general(kw, vh, (((0,), (0,)), ((), ())), preferred_element_type=F32)
        return d_ref[hd] * s + u

    def backward_chunks(k_ref, v_ref, n_chunks, first_chunk):
        for hd in range(HEADS):
            cols = slice(hd * HEAD_DIM, (hd + 1) * HEAD_DIM)
            s = sb_ref[hd]
            for c in reversed(range(n_chunks)):
                rows = slice(c * RCHUNK, (c + 1) * RCHUNK)
                sball_ref[first_chunk + c, hd] = s.astype(BF16)
                s = absorb(s, hd, k_ref[0, rows, cols], v_ref[0, rows, cols], wb_ref, db_ref)
            sb_ref[hd] = s

    def forward_chunks(q_ref, k_ref, v_ref, sg_ref, o_ref, n_chunks, first_chunk):
        for hd in range(HEADS):
            cols = slice(hd * HEAD_DIM, (hd + 1) * HEAD_DIM)
            s = sf_ref[hd]
            for c in range(n_chunks):
                rows = slice(c * RCHUNK, (c + 1) * RCHUNK)
                kh = k_ref[0, rows, cols]
                vh = v_ref[0, rows, cols]
                if o_ref is not None:
                    qh = q_ref[0, rows, cols]
                    sc = lax.dot_general(qh, kh, (((1,), (1,)), ((), ())), preferred_element_type=F32)
                    qf = qh.astype(F32)
                    lhs = jnp.concatenate([(sc * mask_ref[hd]).astype(BF16),
                                           (qf * qf_ref[hd]).astype(BF16),
                                           (qf * qb_ref[hd]).astype(BF16)], axis=1)
                    rhs = jnp.concatenate([vh, s.astype(BF16), sball_ref[first_chunk + c, hd]], axis=0)
                    o = _dot(lhs, rhs)
                    o_ref[0, rows, cols] = (_rms(o) * sg_ref[0, rows, cols].astype(F32)).astype(BF16)
                s = absorb(s, hd, kh, vh, wf_ref, df_ref)
            sf_ref[hd] = s

    @pl.when(phase == 0)
    def _():
        @pl.when(j == 0)
        def _():
            sb_ref[...] = jnp.zeros_like(sb_ref)
            backward_chunks(kc_ref, vc_ref, 1, 0)

        backward_chunks(kl_ref, vl_ref, RET_CPT, 1 + (RET_STEPS - 1 - j) * RET_CPT)

    @pl.when(phase == 1)
    def _():
        @pl.when(j == 0)
        def _():
            sf_ref[...] = jnp.zeros_like(sf_ref)
            if ctx_out:
                forward_chunks(qc_ref, kc_ref, vc_ref, sgc_ref, oc_ref, 1, 0)
            else:
                forward_chunks(None, kc_ref, vc_ref, None, None, 1, 0)

        forward_chunks(ql_ref, kl_ref, vl_ref, sgl_ref, ol_ref, RET_CPT, 1 + j * RET_CPT)


def _retention(logits, lat, ctx, *, ctx_out):
    nb = lat[0].shape[0]
    kv_idx = lambda b, p, j: (b, jnp.where(p == 0, RET_STEPS - 1 - j, j), 0)
    q_idx = lambda b, p, j: (b, jnp.where(p == 0, 0, j), 0)
    c_idx = lambda b, p, j: (b, 0, 0)
    lat_blk = (1, RET_TILE, RET_W)
    ctx_blk = (1, CTX_LEN, RET_W)
    in_specs = [
        _const_spec((2 * HEADS, 1), 3),
        pl.BlockSpec(lat_blk, q_idx), pl.BlockSpec(lat_blk, kv_idx), pl.BlockSpec(lat_blk, kv_idx),
        pl.BlockSpec(lat_blk, q_idx),
        pl.BlockSpec(ctx_blk, c_idx), pl.BlockSpec(ctx_blk, c_idx),
    ]
    if ctx_out:
        qc, kc, vc, sgc = ctx
        args = [logits, *lat, kc, vc, qc, sgc]
        in_specs += [pl.BlockSpec(ctx_blk, c_idx), pl.BlockSpec(ctx_blk, c_idx)]
        out_specs = (pl.BlockSpec(lat_blk, q_idx), pl.BlockSpec(ctx_blk, c_idx))
        out_shape = (jax.ShapeDtypeStruct((nb, SEQ, RET_W), BF16), jax.ShapeDtypeStruct((nb, CTX_LEN, RET_W), BF16))
    else:
        kc, vc = ctx
        args = [logits, *lat, kc, vc]
        out_specs = pl.BlockSpec(lat_blk, q_idx)
        out_shape = jax.ShapeDtypeStruct((nb, SEQ, RET_W), BF16)
    state = pltpu.VMEM((HEADS, HEAD_DIM, HEAD_DIM), F32)
    pos_tab = pltpu.VMEM((HEADS, RCHUNK, HEAD_DIM), F32)
    return pl.pallas_call(
        functools.partial(_ret_kernel, ctx_out=ctx_out),
        grid=(nb, 2, RET_STEPS),
        in_specs=in_specs,
        out_specs=out_specs,
        out_shape=out_shape,
        scratch_shapes=[
            pltpu.VMEM((HEADS, RCHUNK, RCHUNK), F32),
            pos_tab, pos_tab, pos_tab, pos_tab,
            state, state,
            state, state,
            pltpu.VMEM((N_RCHUNKS, HEADS, HEAD_DIM, HEAD_DIM), BF16),
        ],
        compiler_params=_params(3),
        name="retention",
    )(*args)


def _dft_cos_sin(n):
    idx = np.arange(n, dtype=np.float64)
    ang = 2.0 * np.pi * ((idx[:, None] * idx[None, :]) % n) / n
    return np.cos(ang), np.sin(ang)


def _channel_dft(scale):
    c, s = _dft_cos_sin(FNET_GROUP)
    eye = np.eye(FNET_W // FNET_GROUP)
    return np.concatenate([np.kron(eye, c), np.kron(eye, s)], axis=0) * scale


def _fft_constants():
    k1 = np.arange(FFT_N1, dtype=np.float64)[None, :, None]
    n1 = np.arange(FFT_N1, dtype=np.float64)[None, None, :]
    n2 = np.arange(FFT_N2, dtype=np.float64)[:, None, None]
    ang = 2.0 * np.pi * ((k1 * (FFT_N2 * n1 + n2)) % SEQ) / SEQ
    m1 = np.concatenate([np.cos(ang), -np.sin(ang)], axis=1)
    c, s = _dft_cos_sin(FFT_N2)
    m3 = np.block([[c, s], [-s, c]])
    return _np_split(m1), _np_split(m3), _np_split(_channel_dft((SEQ * FNET_GROUP) ** -0.5))


def _ctx_fft_constants():
    c, s = _dft_cos_sin(CTX_LEN)
    return (_np_split(np.concatenate([c, -s], axis=0)),
            _np_split(_channel_dft((CTX_LEN * FNET_GROUP) ** -0.5)))


def _dot2(ah, al, b):
    return _dot(ah, b) + _dot(al, b)


def _fft1_kernel(x_ref, mh_ref, ml_ref, o_ref):
    for jj in range(FFT_STEP):
        x = jnp.concatenate([x_ref[hf, :, jj, :] for hf in range(2)], axis=1).astype(BF16)
        a = _dot2(mh_ref[jj], ml_ref[jj], x)
        for hf in range(2):
            for ri in range(2):
                o_ref[hf, ri, jj] = a[ri * FFT_N1:(ri + 1) * FFT_N1, hf * HALF:(hf + 1) * HALF]


def _fft3_kernel(a_ref, wh_ref, wl_ref, ch_ref, cl_ref, o_ref):
    zs = []
    for jj in range(FFT_STEP):
        rhs = jnp.concatenate(
            [jnp.concatenate([a_ref[hf, ri, :, jj, :] for ri in range(2)], axis=0) for hf in range(2)],
            axis=1).astype(BF16)
        y = _dot2(wh_ref[...], wl_ref[...], rhs)
        zs.append(jnp.concatenate([y[:FFT_N2], y[FFT_N2:]], axis=1).astype(BF16))
    z = jnp.concatenate(zs, axis=0)
    four = _dot(z, ch_ref[...]) + _dot(z, cl_ref[...])
    for jj in range(FFT_STEP):
        for hf in range(2):
            o_ref[hf, :, jj, :] = four[jj * FFT_N2:(jj + 1) * FFT_N2, hf * HALF:(hf + 1) * HALF]


def _fourier_latent(f_lat):
    nb = f_lat.shape[0]
    (m1h, m1l), (m3h, m3l), (cdh, cdl) = _fft_constants()
    a = pl.pallas_call(
        _fft1_kernel,
        grid=(nb, FFT_N2 // FFT_STEP),
        in_specs=[
            pl.BlockSpec((None, 2, FFT_N1, FFT_STEP, HALF), lambda b, j: (b, 0, 0, j, 0)),
            pl.BlockSpec((FFT_STEP, 2 * FFT_N1, FFT_N1), lambda b, j: (j, 0, 0)),
            pl.BlockSpec((FFT_STEP, 2 * FFT_N1, FFT_N1), lambda b, j: (j, 0, 0)),
        ],
        out_specs=pl.BlockSpec((None, 2, 2, FFT_STEP, FFT_N1, HALF), lambda b, j: (b, 0, 0, j, 0, 0)),
        out_shape=jax.ShapeDtypeStruct((nb, 2, 2, FFT_N2, FFT_N1, HALF), F32),
        compiler_params=_params(2),
        name="fourier_stage1",
    )(f_lat.reshape(nb, 2, FFT_N1, FFT_N2, HALF), m1h, m1l)
    out = pl.pallas_call(
        _fft3_kernel,
        grid=(nb, FFT_N1 // FFT_STEP),
        in_specs=[
            pl.BlockSpec((None, 2, 2, FFT_N2, FFT_STEP, HALF), lambda b, j: (b, 0, 0, 0, j, 0)),
            _const_spec((2 * FFT_N2, 2 * FFT_N2), 2), _const_spec((2 * FFT_N2, 2 * FFT_N2), 2),
            _const_spec((2 * FNET_W, FNET_W), 2), _const_spec((2 * FNET_W, FNET_W), 2),
        ],
        out_specs=pl.BlockSpec((None, 2, FFT_N2, FFT_STEP, HALF), lambda b, j: (b, 0, 0, j, 0)),
        out_shape=jax.ShapeDtypeStruct((nb, 2, FFT_N2, FFT_N1, HALF), F32),
        compiler_params=_params(2),
        name="fourier_stage2",
    )(a, m3h, m3l, cdh, cdl)
    return out.reshape(nb, 2, SEQ, HALF)


def _fft_ctx_kernel(x_ref, wh_ref, wl_ref, ch_ref, cl_ref, o_ref):
    x = jnp.concatenate([x_ref[hf] for hf in range(2)], axis=1)
    z = _dot2(wh_ref[...], wl_ref[...], x.astype(BF16))
    z = jnp.concatenate([z[:CTX_LEN], z[CTX_LEN:]], axis=1).astype(BF16)
    four = _dot(z, ch_ref[...]) + _dot(z, cl_ref[...])
    for hf in range(2):
        o_ref[hf] = four[:, hf * HALF:(hf + 1) * HALF]


def _fourier_ctx(f_ctx):
    nb = f_ctx.shape[0]
    (wh, wl), (cdh, cdl) = _ctx_fft_constants()
    return pl.pallas_call(
        _fft_ctx_kernel,
        grid=(nb,),
        in_specs=[
            pl.BlockSpec((None, 2, CTX_LEN, HALF), lambda b: (b, 0, 0, 0)),
            _const_spec((2 * CTX_LEN, CTX_LEN), 1), _const_spec((2 * CTX_LEN, CTX_LEN), 1),
            _const_spec((2 * FNET_W, FNET_W), 1), _const_spec((2 * FNET_W, FNET_W), 1),
        ],
        out_specs=pl.BlockSpec((None, 2, CTX_LEN, HALF), lambda b: (b, 0, 0, 0)),
        out_shape=jax.ShapeDtypeStruct((nb, 2, CTX_LEN, HALF), F32),
        compiler_params=_params(1),
        name="fourier_ctx",
    )(f_ctx, wh, wl, cdh, cdl)


def _mix_mlp_kernel(x_ref, ret_ref, four_ref, u_ref, vn_ref, gate_ref, mod_ref, ws_ref, bs_ref,
                    wa_ref, wb_ref, wc_ref, wo_ref, gpm_ref, gprm_ref, gpom_ref, wup_ref, wdn_ref, o_ref, *, tm):
    group = lax.broadcasted_iota(jnp.int32, (GMLP_CHUNK, GMLP_W), 1) // GMLP_GROUP
    parts = []
    for cc in range(tm // GMLP_CHUNK):
        rows = slice(cc * GMLP_CHUNK, (cc + 1) * GMLP_CHUNK)
        vn = vn_ref[0, rows, :]
        s = bs_ref[...]
        for g in range(GMLP_W // GMLP_GROUP):
            s = s + jnp.where(group == g, _dot(ws_ref[g], vn), 0.0)
        parts.append((u_ref[0, rows, :].astype(F32) * s).astype(BF16))
    sgu = jnp.concatenate(parts, axis=0)

    four = jnp.concatenate([four_ref[0, hf] for hf in range(2)], axis=1).astype(BF16)
    ga = gate_ref[0, :, 0:D_MODEL].astype(F32)
    gb = gate_ref[0, :, D_MODEL:2 * D_MODEL].astype(F32)
    gc = gate_ref[0, :, 2 * D_MODEL:3 * D_MODEL].astype(F32)
    merged = (ga * _dot(ret_ref[0], wa_ref[...]) + gb * _dot(four, wb_ref[...]) + gc * _dot(sgu, wc_ref[...]))
    y = _dot(merged.astype(BF16), wo_ref[...])

    gate1 = mod_ref[0, :, 0:D_MODEL]
    shift2 = mod_ref[0, :, D_MODEL:2 * D_MODEL]
    scale2 = mod_ref[0, :, 2 * D_MODEL:3 * D_MODEL]
    gate2 = mod_ref[0, :, 3 * D_MODEL:4 * D_MODEL]
    x1 = x_ref[0] + gate1 * (_rms(y) * gpm_ref[...])
    h2 = ((_rms(x1) * gprm_ref[...]) * (1.0 + scale2) + shift2).astype(BF16)
    m = None
    for c in range(D_FF // FF_CHUNK):
        cols = slice(c * FF_CHUNK, (c + 1) * FF_CHUNK)
        a = jnp.maximum(_dot(h2, wup_ref[:, cols]), 0.0)
        part = _dot((a * a).astype(BF16), wdn_ref[cols, :])
        m = part if m is None else m + part
    o_ref[0] = x1 + gate2 * (_rms(m) * gpom_ref[...])


def _mix_mlp(xs, ret, four, ug, vn, gates, mod_l, ws, bs_tab, wa, wb, wc, wo,
             g_post_mix, g_pre_mlp, g_post_mlp, w_up, w_down, *, tm, ctx):
    nb, rows, _ = xs.shape
    tok_spec = lambda w: pl.BlockSpec((1, tm, w), lambda b, i: (b, i, 0))
    mod_idx = (lambda b, i: (2, 0, 0)) if ctx else (lambda b, i: (b, 0, 0))
    return pl.pallas_call(
        functools.partial(_mix_mlp_kernel, tm=tm),
        grid=(nb, rows // tm),
        in_specs=[
            tok_spec(D_MODEL), tok_spec(RET_W),
            pl.BlockSpec((1, 2, tm, HALF), lambda b, i: (b, 0, i, 0)),
            tok_spec(GMLP_W), tok_spec(GMLP_W), tok_spec(3 * D_MODEL),
            pl.BlockSpec((1, 1, 4 * D_MODEL), mod_idx),
            _const_spec((GMLP_W // GMLP_GROUP, GMLP_CHUNK, GMLP_CHUNK), 2),
            _const_spec((GMLP_CHUNK, GMLP_W), 2),
            _const_spec((RET_W, D_MODEL), 2), _const_spec((FNET_W, D_MODEL), 2),
            _const_spec((GMLP_W, D_MODEL), 2), _const_spec((D_MODEL, D_MODEL), 2),
            _const_spec((1, D_MODEL), 2), _const_spec((1, D_MODEL), 2), _const_spec((1, D_MODEL), 2),
            _const_spec((D_MODEL, D_FF), 2), _const_spec((D_FF, D_MODEL), 2),
        ],
        out_specs=tok_spec(D_MODEL),
        out_shape=jax.ShapeDtypeStruct((nb, rows, D_MODEL), F32),
        compiler_params=_params(2),
        name="mix_mlp_ctx" if ctx else "mix_mlp",
    )(xs, ret, four, ug, vn, gates, mod_l, ws, bs_tab, wa, wb, wc, wo,
      g_post_mix, g_pre_mlp, g_post_mlp, w_up, w_down)


def _rope_tables():
    rows = SEQ // GRID_W
    freqs = ROPE_BASE ** (-jnp.arange(ROPE_FREQS, dtype=F32) / ROPE_FREQS)
    ang_r = jnp.arange(rows, dtype=F32)[:, None] * freqs
    ang_c = jnp.arange(GRID_W, dtype=F32)[:, None] * freqs

    def table(fn):
        t = jnp.concatenate([jnp.repeat(fn(ang_r), GRID_W, axis=0), jnp.tile(fn(ang_c), (rows, 1))], axis=-1)
        return t

    cos = table(jnp.cos)
    sin = table(jnp.sin)
    return jnp.concatenate([cos, cos], axis=-1), jnp.concatenate([-sin, sin], axis=-1)


def kernel(x, c, ctx, c_ctx, w_mod, b_mod, g_pre_mix, g_post_mix, g_pre_mlp, g_post_mlp, w_in,
           ret_decay_logit, sgu_w_s, sgu_b_s, sgu_norm, w_branch_a, w_branch_b, w_branch_c, w_out,
           w_up, w_down):
    nb = x.shape[0]
    depth = w_mod.shape[0]
    assert x.shape == (nb, SEQ, D_MODEL) and ctx.shape == (nb, CTX_LEN, D_MODEL) and nb == 2

    cond_rows = jnp.concatenate([c, c_ctx[None, :], jnp.zeros((8 - nb - 1, D_MODEL), F32)], axis=0)
    mod = _modulation(cond_rows, w_mod, b_mod)
    rope_tabs = _rope_tables()
    pavg = jnp.asarray(np.kron(np.eye(GMLP_W // GMLP_GROUP), np.full((GMLP_GROUP, GMLP_GROUP), 1.0 / GMLP_GROUP)), BF16)

    for l in range(depth):
        last = l == depth - 1
        mod_l = mod[l].reshape(8, 1, 6 * D_MODEL)
        mod_in, mod_out = mod_l[:, :, :2 * D_MODEL], mod_l[:, :, 2 * D_MODEL:]
        w_in_l = w_in[l].astype(BF16)
        g_pre = g_pre_mix[l][None, :]
        g_norm = sgu_norm[l][None, :]
        logits = ret_decay_logit[l].reshape(2 * HEADS, 1)

        q, k, v, sg, f_lat, ug, vn, gates = _in_proj(
            x, mod_in, g_pre, w_in_l, pavg, g_norm, rope_tabs, tm=TM_LAT, ctx=False)
        if last:
            kc, vc = _in_proj(ctx, mod_in, g_pre, w_in_l, pavg, g_norm, None, tm=CTX_LEN, ctx=True, kv_only=True)
            ret = _retention(logits, (q, k, v, sg), (kc, vc), ctx_out=False)
        else:
            qc, kc, vc, sgc, f_ctx, ugc, vnc, gatesc = _in_proj(
                ctx, mod_in, g_pre, w_in_l, pavg, g_norm, None, tm=CTX_LEN, ctx=True)
            ret, retc = _retention(logits, (q, k, v, sg), (qc, kc, vc, sgc), ctx_out=True)

        mix_w = (sgu_w_s[l].astype(BF16), jnp.repeat(jnp.transpose(sgu_b_s[l]), GMLP_GROUP, axis=1),
                 w_branch_a[l].astype(BF16), w_branch_b[l].astype(BF16), w_branch_c[l].astype(BF16),
                 w_out[l].astype(BF16), g_post_mix[l][None, :], g_pre_mlp[l][None, :], g_post_mlp[l][None, :],
                 w_up[l].astype(BF16), w_down[l].astype(BF16))
        x = _mix_mlp(x, ret, _fourier_latent(f_lat), ug, vn, gates, mod_out, *mix_w, tm=TM_LAT, ctx=False)
        if not last:
            ctx = _mix_mlp(ctx, retc, _fourier_ctx(f_ctx), ugc, vnc, gatesc, mod_out, *mix_w, tm=CTX_LEN, ctx=True)
    return x
```

```python
import functools

import numpy as np
import jax
import jax.numpy as jnp
from jax import lax
from jax.experimental import pallas as pl
from jax.experimental.pallas import tpu as pltpu

F32 = jnp.float32
BF16 = jnp.bfloat16

D_MODEL = 1024
SEQ = 8192
CTX_LEN = 256
GRID_W = 64
RET_W = 512
HEADS = 4
HEAD_DIM = 128
ROPE_BASE = 10000.0
ROPE_FREQS = HEAD_DIM // 4
FNET_W = 256
FNET_GROUP = 64
GMLP_W = 256
GMLP_GROUP = 64
GMLP_CHUNK = 128
D_FF = 4 * D_MODEL
EPS = 1e-6
IN_W = 4 * RET_W + FNET_W + 2 * GMLP_W + 3 * D_MODEL
COL_F = 4 * RET_W
COL_U = COL_F + FNET_W
COL_VS = COL_U + GMLP_W
COL_GATE = COL_VS + GMLP_W

TM_LAT = 512
RCHUNK = 256
RET_TILE = 1024
RET_STEPS = SEQ // RET_TILE
RET_CPT = RET_TILE // RCHUNK
N_RCHUNKS = 1 + SEQ // RCHUNK
FF_CHUNK = 1024
FFT_N1 = 64
FFT_N2 = 128
HALF = 128
FFT_STEP = 16

_VMEM_LIMIT = 56 * 1024 * 1024


def _dot(a, b):
    return jnp.dot(a, b, preferred_element_type=F32)


def _split(x):
    hi = x.astype(BF16)
    lo = (x - hi.astype(F32)).astype(BF16)
    return hi, lo


def _np_split(a64):
    hi = np.asarray(a64, np.float32).astype(BF16)
    lo = (np.asarray(a64, np.float32) - hi.astype(np.float32)).astype(BF16)
    return jnp.asarray(hi), jnp.asarray(lo)


def _rms(x):
    return x * lax.rsqrt(jnp.mean(x * x, axis=-1, keepdims=True) + EPS)


def _gelu(x):
    return x * (0.5 * (1.0 + jnp.tanh(0.7978845608028654 * (x + 0.044715 * (x * x * x)))))


def _sigmoid(x):
    return 1.0 / (1.0 + jnp.exp(-x))


def _const_spec(shape, nargs=None):
    zeros = (0,) * len(shape)
    return pl.BlockSpec(shape, lambda *_: zeros, pipeline_mode=pl.Buffered(1))


def _layer_spec(shape, layer):
    idx = (layer,) + (0,) * len(shape)
    return pl.BlockSpec((None,) + tuple(shape), lambda *_: idx, pipeline_mode=pl.Buffered(1))


def _params(n_axes):
    return pltpu.CompilerParams(dimension_semantics=("arbitrary",) * n_axes, vmem_limit_bytes=_VMEM_LIMIT)


def _mod_kernel(a_ref, w_ref, b_ref, o_ref):
    a = a_ref[...]
    a = a * _sigmoid(a)
    ah, al = _split(a)
    w = w_ref[0].astype(BF16)
    o_ref[0] = _dot(ah, w) + _dot(al, w) + b_ref[0]


def _modulation(cond_rows, w_mod, b_mod):
    depth = w_mod.shape[0]
    tn = 1536
    return pl.pallas_call(
        _mod_kernel,
        grid=(depth, (6 * D_MODEL) // tn),
        in_specs=[
            pl.BlockSpec((8, D_MODEL), lambda l, j: (0, 0)),
            pl.BlockSpec((1, D_MODEL, tn), lambda l, j: (l, 0, j)),
            pl.BlockSpec((1, 1, tn), lambda l, j: (l, 0, j)),
        ],
        out_specs=pl.BlockSpec((1, 8, tn), lambda l, j: (l, 0, j)),
        out_shape=jax.ShapeDtypeStruct((depth, 8, 6 * D_MODEL), F32),
        compiler_params=_params(2),
        name="modulation",
    )(cond_rows, w_mod, b_mod.reshape(depth, 1, 6 * D_MODEL))


def _in_proj_kernel(*refs, rope, kv_only):
    if rope:
        x_ref, mod_ref, gpre_ref, w_ref, pavg_ref, gn_ref, cos_ref, sin_ref = refs[:8]
        outs = refs[8:]
    else:
        x_ref, mod_ref, gpre_ref, w_ref, pavg_ref, gn_ref = refs[:6]
        outs = refs[6:]
    x = x_ref[0]
    shift = mod_ref[0, :, 0:D_MODEL]
    scale = mod_ref[0, :, D_MODEL:2 * D_MODEL]
    h = _rms(x) * gpre_ref[...]
    hb = (h * (1.0 + scale) + shift).astype(BF16)

    def proj(lo, hi):
        return _dot(hb, w_ref[:, lo:hi])

    def rope_store(z, dst):
        if not rope:
            dst[0] = z.astype(BF16)
            return
        cos = cos_ref[...]
        sin = sin_ref[...]
        for hd in range(HEADS):
            a = z[:, hd * HEAD_DIM:(hd + 1) * HEAD_DIM]
            r = a * cos + pltpu.roll(a, HEAD_DIM // 2, 1) * sin
            dst[0, :, hd * HEAD_DIM:(hd + 1) * HEAD_DIM] = r.astype(BF16)

    if kv_only:
        k_ref, v_ref = outs
    else:
        q_ref, k_ref, v_ref, sg_ref, f_ref, u_ref, vn_ref, gate_ref = outs
        rope_store(proj(0, RET_W), q_ref)
    rope_store(proj(RET_W, 2 * RET_W) * (HEAD_DIM ** -0.5), k_ref)
    v_ref[0] = proj(2 * RET_W, 3 * RET_W).astype(BF16)
    if kv_only:
        return
    g = proj(3 * RET_W, 4 * RET_W)
    sg_ref[0] = (g * _sigmoid(g)).astype(BF16)

    f = proj(COL_F, COL_U)
    for hf in range(2):
        f_ref[0, hf] = f[:, hf * HALF:(hf + 1) * HALF]

    u_ref[0] = _gelu(proj(COL_U, COL_VS)).astype(BF16)
    vg = _gelu(proj(COL_VS, COL_GATE))
    sh, sl = _split(vg * vg)
    pavg = pavg_ref[...]
    ms = _dot(sh, pavg) + _dot(sl, pavg)
    vn_ref[0] = (vg * lax.rsqrt(ms + EPS) * gn_ref[...]).astype(BF16)

    for c in range(3):
        z = proj(COL_GATE + c * D_MODEL, COL_GATE + (c + 1) * D_MODEL)
        gate_ref[0, :, c * D_MODEL:(c + 1) * D_MODEL] = _sigmoid(z).astype(BF16)


def _mod_spec(layer, ctx):
    idx = (lambda b, i: (layer, 2, 0, 0)) if ctx else (lambda b, i: (layer, b, 0, 0))
    return pl.BlockSpec((None, 1, 1, 6 * D_MODEL), idx)


def _in_proj(xs, layer, mod, g_pre, w_in, pavg, g_norm, rope_tabs, *, tm, ctx, kv_only=False):
    nb, rows, _ = xs.shape
    rope = rope_tabs is not None
    tok_spec = lambda w: pl.BlockSpec((1, tm, w), lambda b, i: (b, i, 0))
    in_specs = [
        tok_spec(D_MODEL),
        _mod_spec(layer, ctx),
        _layer_spec((1, D_MODEL), layer),
        _layer_spec((D_MODEL, IN_W), layer),
        _const_spec((GMLP_W, GMLP_W)),
        _layer_spec((1, GMLP_W), layer),
    ]
    args = [xs, mod, g_pre, w_in, pavg, g_norm]
    if rope:
        in_specs += [pl.BlockSpec((tm, HEAD_DIM), lambda b, i: (i, 0))] * 2
        args += list(rope_tabs)
    bf = lambda w: jax.ShapeDtypeStruct((nb, rows, w), BF16)
    if kv_only:
        out_shapes = (bf(RET_W), bf(RET_W))
        out_specs = (tok_spec(RET_W), tok_spec(RET_W))
    else:
        out_shapes = (
            bf(RET_W), bf(RET_W), bf(RET_W), bf(RET_W),
            jax.ShapeDtypeStruct((nb, 2, rows, HALF), F32),
            bf(GMLP_W), bf(GMLP_W),
            bf(3 * D_MODEL),
        )
        out_specs = (
            tok_spec(RET_W), tok_spec(RET_W), tok_spec(RET_W), tok_spec(RET_W),
            pl.BlockSpec((1, 2, tm, HALF), lambda b, i: (b, 0, i, 0)),
            tok_spec(GMLP_W), tok_spec(GMLP_W), tok_spec(3 * D_MODEL),
        )
    return pl.pallas_call(
        functools.partial(_in_proj_kernel, rope=rope, kv_only=kv_only),
        grid=(nb, rows // tm),
        in_specs=in_specs,
        out_specs=out_specs,
        out_shape=out_shapes,
        compiler_params=_params(2),
        name="in_proj_ctx" if ctx else "in_proj",
    )(*args)


def _ret_kernel(*refs, ctx_out):
    (logit_ref, ql_ref, kl_ref, vl_ref, sgl_ref, kc_ref, vc_ref) = refs[:7]
    if ctx_out:
        qc_ref, sgc_ref, ol_ref, oc_ref = refs[7:11]
        scratch = refs[11:]
    else:
        ol_ref = refs[7]
        scratch = refs[8:]
    mask_ref, wf_ref, wb_ref, qf_ref, qb_ref, df_ref, db_ref, sf_ref, sb_ref, sball_ref = scratch
    b = pl.program_id(0)
    phase = pl.program_id(1)
    j = pl.program_id(2)

    @pl.when((b == 0) & (phase == 0) & (j == 0))
    def _():
        x = logit_ref[...]
        lg = -(jnp.maximum(-x, 0.0) + jnp.log(1.0 + jnp.exp(-jnp.abs(x))))
        row = lax.broadcasted_iota(jnp.int32, (RCHUNK, RCHUNK), 0).astype(F32)
        col = lax.broadcasted_iota(jnp.int32, (RCHUNK, RCHUNK), 1).astype(F32)
        diff = row - col
        pos = row[:, :HEAD_DIM]
        for hd in range(HEADS):
            lf = lg[hd:hd + 1, :]
            lb = lg[HEADS + hd:HEADS + hd + 1, :]
            mask_ref[hd] = jnp.where(diff >= 0.0, jnp.exp(lf * jnp.maximum(diff, 0.0)),
                                     jnp.exp(lb * jnp.maximum(-diff, 0.0)))
            wf_ref[hd] = jnp.exp(lf * (RCHUNK - 1.0 - pos))
            wb_ref[hd] = jnp.exp(lb * pos)
            qf_ref[hd] = jnp.exp(lf * (pos + 1.0))
            qb_ref[hd] = jnp.exp(lb * (RCHUNK - pos))
            df_ref[hd] = jnp.exp(jnp.broadcast_to(lf, (HEAD_DIM, HEAD_DIM)) * float(RCHUNK))
            db_ref[hd] = jnp.exp(jnp.broadcast_to(lb, (HEAD_DIM, HEAD_DIM)) * float(RCHUNK))

    def absorb(s, hd, kh, vh, w_ref, d_ref):
        kw = (kh.astype(F32) * w_ref[hd]).astype(BF16)
        u = lax.dot_general(kw, vh, (((0,), (0,)), ((), ())), preferred_element_type=F32)
        return d_ref[hd] * s + u

    def backward_chunks(k_ref, v_ref, n_chunks, first_chunk):
        for hd in range(HEADS):
            cols = slice(hd * HEAD_DIM, (hd + 1) * HEAD_DIM)
            s = sb_ref[hd]
            for c in reversed(range(n_chunks)):
                rows = slice(c * RCHUNK, (c + 1) * RCHUNK)
                sball_ref[first_chunk + c, hd] = s.astype(BF16)
                s = absorb(s, hd, k_ref[0, rows, cols], v_ref[0, rows, cols], wb_ref, db_ref)
            sb_ref[hd] = s

    def forward_chunks(q_ref, k_ref, v_ref, sg_ref, o_ref, n_chunks, first_chunk):
        for hd in range(HEADS):
            cols = slice(hd * HEAD_DIM, (hd + 1) * HEAD_DIM)
            s = sf_ref[hd]
            for c in range(n_chunks):
                rows = slice(c * RCHUNK, (c + 1) * RCHUNK)
                kh = k_ref[0, rows, cols]
                vh = v_ref[0, rows, cols]
                if o_ref is not None:
                    qh = q_ref[0, rows, cols]
                    sc = lax.dot_general(qh, kh, (((1,), (1,)), ((), ())), preferred_element_type=F32)
                    qf = qh.astype(F32)
                    lhs = jnp.concatenate([(sc * mask_ref[hd]).astype(BF16),
                                           (qf * qf_ref[hd]).astype(BF16),
                                           (qf * qb_ref[hd]).astype(BF16)], axis=1)
                    rhs = jnp.concatenate([vh, s.astype(BF16), sball_ref[first_chunk + c, hd]], axis=0)
                    o = _dot(lhs, rhs)
                    o_ref[0, rows, cols] = (_rms(o) * sg_ref[0, rows, cols].astype(F32)).astype(BF16)
                s = absorb(s, hd, kh, vh, wf_ref, df_ref)
            sf_ref[hd] = s

    @pl.when(phase == 0)
    def _():
        @pl.when(j == 0)
        def _():
            sb_ref[...] = jnp.zeros_like(sb_ref)
            backward_chunks(kc_ref, vc_ref, 1, 0)

        backward_chunks(kl_ref, vl_ref, RET_CPT, 1 + (RET_STEPS - 1 - j) * RET_CPT)

    @pl.when(phase == 1)
    def _():
        @pl.when(j == 0)
        def _():
            sf_ref[...] = jnp.zeros_like(sf_ref)
            if ctx_out:
                forward_chunks(qc_ref, kc_ref, vc_ref, sgc_ref, oc_ref, 1, 0)
            else:
                forward_chunks(None, kc_ref, vc_ref, None, None, 1, 0)

        forward_chunks(ql_ref, kl_ref, vl_ref, sgl_ref, ol_ref, RET_CPT, 1 + j * RET_CPT)


def _retention(layer, logits, lat, ctx, *, ctx_out):
    nb = lat[0].shape[0]
    kv_idx = lambda b, p, j: (b, jnp.where(p == 0, RET_STEPS - 1 - j, j), 0)
    q_idx = lambda b, p, j: (b, jnp.where(p == 0, 0, j), 0)
    c_idx = lambda b, p, j: (b, 0, 0)
    lat_blk = (1, RET_TILE, RET_W)
    ctx_blk = (1, CTX_LEN, RET_W)
    in_specs = [
        _layer_spec((2 * HEADS, 1), layer),
        pl.BlockSpec(lat_blk, q_idx), pl.BlockSpec(lat_blk, kv_idx), pl.BlockSpec(lat_blk, kv_idx),
        pl.BlockSpec(lat_blk, q_idx),
        pl.BlockSpec(ctx_blk, c_idx), pl.BlockSpec(ctx_blk, c_idx),
    ]
    if ctx_out:
        qc, kc, vc, sgc = ctx
        args = [logits, *lat, kc, vc, qc, sgc]
        in_specs += [pl.BlockSpec(ctx_blk, c_idx), pl.BlockSpec(ctx_blk, c_idx)]
        out_specs = (pl.BlockSpec(lat_blk, q_idx), pl.BlockSpec(ctx_blk, c_idx))
        out_shape = (jax.ShapeDtypeStruct((nb, SEQ, RET_W), BF16), jax.ShapeDtypeStruct((nb, CTX_LEN, RET_W), BF16))
    else:
        kc, vc = ctx
        args = [logits, *lat, kc, vc]
        out_specs = pl.BlockSpec(lat_blk, q_idx)
        out_shape = jax.ShapeDtypeStruct((nb, SEQ, RET_W), BF16)
    state = pltpu.VMEM((HEADS, HEAD_DIM, HEAD_DIM), F32)
    pos_tab = pltpu.VMEM((HEADS, RCHUNK, HEAD_DIM), F32)
    return pl.pallas_call(
        functools.partial(_ret_kernel, ctx_out=ctx_out),
        grid=(nb, 2, RET_STEPS),
        in_specs=in_specs,
        out_specs=out_specs,
        out_shape=out_shape,
        scratch_shapes=[
            pltpu.VMEM((HEADS, RCHUNK, RCHUNK), F32),
            pos_tab, pos_tab, pos_tab, pos_tab,
            state, state,
            state, state,
            pltpu.VMEM((N_RCHUNKS, HEADS, HEAD_DIM, HEAD_DIM), BF16),
        ],
        compiler_params=_params(3),
        name="retention",
    )(*args)


def _dft_cos_sin(n):
    idx = np.arange(n, dtype=np.float64)
    ang = 2.0 * np.pi * ((idx[:, None] * idx[None, :]) % n) / n
    return np.cos(ang), np.sin(ang)


def _channel_dft(scale):
    c, s = _dft_cos_sin(FNET_GROUP)
    eye = np.eye(FNET_W // FNET_GROUP)
    return np.concatenate([np.kron(eye, c), np.kron(eye, s)], axis=0) * scale


def _fft_constants():
    k1 = np.arange(FFT_N1, dtype=np.float64)[None, :, None]
    n1 = np.arange(FFT_N1, dtype=np.float64)[None, None, :]
    n2 = np.arange(FFT_N2, dtype=np.float64)[:, None, None]
    ang = 2.0 * np.pi * ((k1 * (FFT_N2 * n1 + n2)) % SEQ) / SEQ
    m1 = np.concatenate([np.cos(ang), -np.sin(ang)], axis=1)
    c, s = _dft_cos_sin(FFT_N2)
    m3 = np.block([[c, s], [-s, c]])
    return _np_split(m1), _np_split(m3), _np_split(_channel_dft((SEQ * FNET_GROUP) ** -0.5))


def _ctx_fft_constants():
    c, s = _dft_cos_sin(CTX_LEN)
    return (_np_split(np.concatenate([c, -s], axis=0)),
            _np_split(_channel_dft((CTX_LEN * FNET_GROUP) ** -0.5)))


def _dot2(ah, al, b):
    return _dot(ah, b) + _dot(al, b)


def _fft1_kernel(x_ref, mh_ref, ml_ref, o_ref):
    for jj in range(FFT_STEP):
        x = jnp.concatenate([x_ref[hf, :, jj, :] for hf in range(2)], axis=1).astype(BF16)
        a = _dot2(mh_ref[jj], ml_ref[jj], x)
        for hf in range(2):
            for ri in range(2):
                o_ref[hf, ri, jj] = a[ri * FFT_N1:(ri + 1) * FFT_N1, hf * HALF:(hf + 1) * HALF]


def _fft3_kernel(a_ref, wh_ref, wl_ref, ch_ref, cl_ref, o_ref):
    zs = []
    for jj in range(FFT_STEP):
        rhs = jnp.concatenate(
            [jnp.concatenate([a_ref[hf, ri, :, jj, :] for ri in range(2)], axis=0) for hf in range(2)],
            axis=1).astype(BF16)
        y = _dot2(wh_ref[...], wl_ref[...], rhs)
        zs.append(jnp.concatenate([y[:FFT_N2], y[FFT_N2:]], axis=1).astype(BF16))
    z = jnp.concatenate(zs, axis=0)
    four = _dot(z, ch_ref[...]) + _dot(z, cl_ref[...])
    for jj in range(FFT_STEP):
        for hf in range(2):
            o_ref[hf, :, jj, :] = four[jj * FFT_N2:(jj + 1) * FFT_N2, hf * HALF:(hf + 1) * HALF]


def _fourier_latent(f_lat):
    nb = f_lat.shape[0]
    (m1h, m1l), (m3h, m3l), (cdh, cdl) = _fft_constants()
    a = pl.pallas_call(
        _fft1_kernel,
        grid=(nb, FFT_N2 // FFT_STEP),
        in_specs=[
            pl.BlockSpec((None, 2, FFT_N1, FFT_STEP, HALF), lambda b, j: (b, 0, 0, j, 0)),
            pl.BlockSpec((FFT_STEP, 2 * FFT_N1, FFT_N1), lambda b, j: (j, 0, 0)),
            pl.BlockSpec((FFT_STEP, 2 * FFT_N1, FFT_N1), lambda b, j: (j, 0, 0)),
        ],
        out_specs=pl.BlockSpec((None, 2, 2, FFT_STEP, FFT_N1, HALF), lambda b, j: (b, 0, 0, j, 0, 0)),
        out_shape=jax.ShapeDtypeStruct((nb, 2, 2, FFT_N2, FFT_N1, HALF), F32),
        compiler_params=_params(2),
        name="fourier_stage1",
    )(f_lat.reshape(nb, 2, FFT_N1, FFT_N2, HALF), m1h, m1l)
    out = pl.pallas_call(
        _fft3_kernel,
        grid=(nb, FFT_N1 // FFT_STEP),
        in_specs=[
            pl.BlockSpec((None, 2, 2, FFT_N2, FFT_STEP, HALF), lambda b, j: (b, 0, 0, 0, j, 0)),
            _const_spec((2 * FFT_N2, 2 * FFT_N2), 2), _const_spec((2 * FFT_N2, 2 * FFT_N2), 2),
            _const_spec((2 * FNET_W, FNET_W), 2), _const_spec((2 * FNET_W, FNET_W), 2),
        ],
        out_specs=pl.BlockSpec((None, 2, FFT_N2, FFT_STEP, HALF), lambda b, j: (b, 0, 0, j, 0)),
        out_shape=jax.ShapeDtypeStruct((nb, 2, FFT_N2, FFT_N1, HALF), F32),
        compiler_params=_params(2),
        name="fourier_stage2",
    )(a, m3h, m3l, cdh, cdl)
    return out.reshape(nb, 2, SEQ, HALF)


def _fft_ctx_kernel(x_ref, wh_ref, wl_ref, ch_ref, cl_ref, o_ref):
    x = jnp.concatenate([x_ref[hf] for hf in range(2)], axis=1)
    z = _dot2(wh_ref[...], wl_ref[...], x.astype(BF16))
    z = jnp.concatenate([z[:CTX_LEN], z[CTX_LEN:]], axis=1).astype(BF16)
    four = _dot(z, ch_ref[...]) + _dot(z, cl_ref[...])
    for hf in range(2):
        o_ref[hf] = four[:, hf * HALF:(hf + 1) * HALF]


def _fourier_ctx(f_ctx):
    nb = f_ctx.shape[0]
    (wh, wl), (cdh, cdl) = _ctx_fft_constants()
    return pl.pallas_call(
        _fft_ctx_kernel,
        grid=(nb,),
        in_specs=[
            pl.BlockSpec((None, 2, CTX_LEN, HALF), lambda b: (b, 0, 0, 0)),
            _const_spec((2 * CTX_LEN, CTX_LEN), 1), _const_spec((2 * CTX_LEN, CTX_LEN), 1),
            _const_spec((2 * FNET_W, FNET_W), 1), _const_spec((2 * FNET_W, FNET_W), 1),
        ],
        out_specs=pl.BlockSpec((None, 2, CTX_LEN, HALF), lambda b: (b, 0, 0, 0)),
        out_shape=jax.ShapeDtypeStruct((nb, 2, CTX_LEN, HALF), F32),
        compiler_params=_params(1),
        name="fourier_ctx",
    )(f_ctx, wh, wl, cdh, cdl)


def _mix_mlp_kernel(x_ref, ret_ref, four_ref, u_ref, vn_ref, gate_ref, mod_ref, ws_ref, bs_ref,
                    wa_ref, wb_ref, wc_ref, wo_ref, gpm_ref, gprm_ref, gpom_ref, wup_ref, wdn_ref, o_ref, *, tm):
    group = lax.broadcasted_iota(jnp.int32, (GMLP_CHUNK, GMLP_W), 1) // GMLP_GROUP
    parts = []
    for cc in range(tm // GMLP_CHUNK):
        rows = slice(cc * GMLP_CHUNK, (cc + 1) * GMLP_CHUNK)
        vn = vn_ref[0, rows, :]
        s = bs_ref[...]
        for g in range(GMLP_W // GMLP_GROUP):
            s = s + jnp.where(group == g, _dot(ws_ref[g], vn), 0.0)
        parts.append((u_ref[0, rows, :].astype(F32) * s).astype(BF16))
    sgu = jnp.concatenate(parts, axis=0)

    four = jnp.concatenate([four_ref[0, hf] for hf in range(2)], axis=1).astype(BF16)
    ga = gate_ref[0, :, 0:D_MODEL].astype(F32)
    gb = gate_ref[0, :, D_MODEL:2 * D_MODEL].astype(F32)
    gc = gate_ref[0, :, 2 * D_MODEL:3 * D_MODEL].astype(F32)
    merged = (ga * _dot(ret_ref[0], wa_ref[...]) + gb * _dot(four, wb_ref[...]) + gc * _dot(sgu, wc_ref[...]))
    y = _dot(merged.astype(BF16), wo_ref[...])

    gate1 = mod_ref[0, :, 2 * D_MODEL:3 * D_MODEL]
    shift2 = mod_ref[0, :, 3 * D_MODEL:4 * D_MODEL]
    scale2 = mod_ref[0, :, 4 * D_MODEL:5 * D_MODEL]
    gate2 = mod_ref[0, :, 5 * D_MODEL:6 * D_MODEL]
    x1 = x_ref[0] + gate1 * (_rms(y) * gpm_ref[...])
    h2 = ((_rms(x1) * gprm_ref[...]) * (1.0 + scale2) + shift2).astype(BF16)
    m = None
    for c in range(D_FF // FF_CHUNK):
        cols = slice(c * FF_CHUNK, (c + 1) * FF_CHUNK)
        a = jnp.maximum(_dot(h2, wup_ref[:, cols]), 0.0)
        part = _dot((a * a).astype(BF16), wdn_ref[cols, :])
        m = part if m is None else m + part
    o_ref[0] = x1 + gate2 * (_rms(m) * gpom_ref[...])


def _mix_mlp(xs, ret, four, ug, vn, gates, layer, mod, ws, bs_tab, wa, wb, wc, wo,
             g_post_mix, g_pre_mlp, g_post_mlp, w_up, w_down, *, tm, ctx):
    nb, rows, _ = xs.shape
    tok_spec = lambda w: pl.BlockSpec((1, tm, w), lambda b, i: (b, i, 0))
    lspec = lambda *shape: _layer_spec(shape, layer)
    return pl.pallas_call(
        functools.partial(_mix_mlp_kernel, tm=tm),
        grid=(nb, rows // tm),
        in_specs=[
            tok_spec(D_MODEL), tok_spec(RET_W),
            pl.BlockSpec((1, 2, tm, HALF), lambda b, i: (b, 0, i, 0)),
            tok_spec(GMLP_W), tok_spec(GMLP_W), tok_spec(3 * D_MODEL),
            _mod_spec(layer, ctx),
            lspec(GMLP_W // GMLP_GROUP, GMLP_CHUNK, GMLP_CHUNK),
            lspec(GMLP_CHUNK, GMLP_W),
            lspec(RET_W, D_MODEL), lspec(FNET_W, D_MODEL), lspec(GMLP_W, D_MODEL), lspec(D_MODEL, D_MODEL),
            lspec(1, D_MODEL), lspec(1, D_MODEL), lspec(1, D_MODEL),
            lspec(D_MODEL, D_FF), lspec(D_FF, D_MODEL),
        ],
        out_specs=tok_spec(D_MODEL),
        out_shape=jax.ShapeDtypeStruct((nb, rows, D_MODEL), F32),
        compiler_params=_params(2),
        name="mix_mlp_ctx" if ctx else "mix_mlp",
    )(xs, ret, four, ug, vn, gates, mod, ws, bs_tab, wa, wb, wc, wo,
      g_post_mix, g_pre_mlp, g_post_mlp, w_up, w_down)


def _rope_tables():
    rows = SEQ // GRID_W
    freqs = ROPE_BASE ** (-jnp.arange(ROPE_FREQS, dtype=F32) / ROPE_FREQS)
    ang_r = jnp.arange(rows, dtype=F32)[:, None] * freqs
    ang_c = jnp.arange(GRID_W, dtype=F32)[:, None] * freqs

    def table(fn):
        t = jnp.concatenate([jnp.repeat(fn(ang_r), GRID_W, axis=0), jnp.tile(fn(ang_c), (rows, 1))], axis=-1)
        return t

    cos = table(jnp.cos)
    sin = table(jnp.sin)
    return jnp.concatenate([cos, cos], axis=-1), jnp.concatenate([-sin, sin], axis=-1)


def kernel(x, c, ctx, c_ctx, w_mod, b_mod, g_pre_mix, g_post_mix, g_pre_mlp, g_post_mlp, w_in,
           ret_decay_logit, sgu_w_s, sgu_b_s, sgu_norm, w_branch_a, w_branch_b, w_branch_c, w_out,
           w_up, w_down):
    nb = x.shape[0]
    depth = w_mod.shape[0]
    assert x.shape == (nb, SEQ, D_MODEL) and ctx.shape == (nb, CTX_LEN, D_MODEL) and nb == 2

    cond_rows = jnp.concatenate([c, c_ctx[None, :], jnp.zeros((8 - nb - 1, D_MODEL), F32)], axis=0)
    mod = _modulation(cond_rows, w_mod, b_mod).reshape(depth, 8, 1, 6 * D_MODEL)
    rope_tabs = _rope_tables()
    pavg = jnp.asarray(np.kron(np.eye(GMLP_W // GMLP_GROUP), np.full((GMLP_GROUP, GMLP_GROUP), 1.0 / GMLP_GROUP)), BF16)

    vec = lambda p: p[:, None, :]
    in_w = (vec(g_pre_mix), w_in.astype(BF16), pavg, vec(sgu_norm))
    logits = ret_decay_logit.reshape(depth, 2 * HEADS, 1)
    mix_w = (sgu_w_s.astype(BF16), jnp.repeat(jnp.swapaxes(sgu_b_s, 1, 2), GMLP_GROUP, axis=2),
             w_branch_a.astype(BF16), w_branch_b.astype(BF16), w_branch_c.astype(BF16), w_out.astype(BF16),
             vec(g_post_mix), vec(g_pre_mlp), vec(g_post_mlp), w_up.astype(BF16), w_down.astype(BF16))

    for l in range(depth):
        last = l == depth - 1
        q, k, v, sg, f_lat, ug, vn, gates = _in_proj(x, l, mod, *in_w, rope_tabs, tm=TM_LAT, ctx=False)
        if last:
            kc, vc = _in_proj(ctx, l, mod, *in_w, None, tm=CTX_LEN, ctx=True, kv_only=True)
            ret = _retention(l, logits, (q, k, v, sg), (kc, vc), ctx_out=False)
        else:
            qc, kc, vc, sgc, f_ctx, ugc, vnc, gatesc = _in_proj(ctx, l, mod, *in_w, None, tm=CTX_LEN, ctx=True)
            ret, retc = _retention(l, logits, (q, k, v, sg), (qc, kc, vc, sgc), ctx_out=True)
        x = _mix_mlp(x, ret, _fourier_latent(f_lat), ug, vn, gates, l, mod, *mix_w, tm=TM_LAT, ctx=False)
        if not last:
            ctx = _mix_mlp(ctx, retc, _fourier_ctx(f_ctx), ugc, vnc, gatesc, l, mod, *mix_w, tm=CTX_LEN, ctx=True)
    return x
```

```python
import functools

import numpy as np
import jax
import jax.numpy as jnp
from jax import lax
from jax.experimental import pallas as pl
from jax.experimental.pallas import tpu as pltpu

F32 = jnp.float32
BF16 = jnp.bfloat16

D_MODEL = 1024
SEQ = 8192
CTX_LEN = 256
GRID_W = 64
RET_W = 512
HEADS = 4
HEAD_DIM = 128
ROPE_BASE = 10000.0
ROPE_FREQS = HEAD_DIM // 4
FNET_W = 256
FNET_GROUP = 64
GMLP_W = 256
GMLP_GROUP = 64
GMLP_CHUNK = 128
D_FF = 4 * D_MODEL
EPS = 1e-6
IN_W = 4 * RET_W + FNET_W + 2 * GMLP_W + 3 * D_MODEL
COL_F = 4 * RET_W
COL_U = COL_F + FNET_W
COL_VS = COL_U + GMLP_W
COL_GATE = COL_VS + GMLP_W

TM_LAT = 512
RCHUNK = 256
RET_TILE = 1024
RET_STEPS = SEQ // RET_TILE
RET_CPT = RET_TILE // RCHUNK
N_RCHUNKS = 1 + SEQ // RCHUNK
FF_CHUNK = 1024
FFT_N1 = 64
FFT_N2 = 128
HALF = 128
FFT_STEP = 16

_VMEM_LIMIT = 56 * 1024 * 1024


def _dot(a, b):
    return jnp.dot(a, b, preferred_element_type=F32)


def _split(x):
    hi = x.astype(BF16)
    lo = (x - hi.astype(F32)).astype(BF16)
    return hi, lo


def _np_split(a64):
    hi = np.asarray(a64, np.float32).astype(BF16)
    lo = (np.asarray(a64, np.float32) - hi.astype(np.float32)).astype(BF16)
    return hi, lo


def _rms(x):
    return x * lax.rsqrt(jnp.mean(x * x, axis=-1, keepdims=True) + EPS)


def _gelu(x):
    return x * (0.5 * (1.0 + jnp.tanh(0.7978845608028654 * (x + 0.044715 * (x * x * x)))))


def _sigmoid(x):
    return 1.0 / (1.0 + jnp.exp(-x))


def _const_spec(shape, nargs=None):
    zeros = (0,) * len(shape)
    return pl.BlockSpec(shape, lambda *_: zeros, pipeline_mode=pl.Buffered(1))


def _layer_spec(shape, layer):
    idx = (layer,) + (0,) * len(shape)
    return pl.BlockSpec((None,) + tuple(shape), lambda *_: idx, pipeline_mode=pl.Buffered(1))


def _params(n_axes):
    return pltpu.CompilerParams(dimension_semantics=("arbitrary",) * n_axes, vmem_limit_bytes=_VMEM_LIMIT)


def _mod_kernel(a_ref, w_ref, b_ref, o_ref):
    a = a_ref[...]
    a = a * _sigmoid(a)
    ah, al = _split(a)
    w = w_ref[0].astype(BF16)
    o_ref[0] = _dot(ah, w) + _dot(al, w) + b_ref[0]


def _modulation(cond_rows, w_mod, b_mod):
    depth = w_mod.shape[0]
    tn = 1536
    return pl.pallas_call(
        _mod_kernel,
        grid=(depth, (6 * D_MODEL) // tn),
        in_specs=[
            pl.BlockSpec((8, D_MODEL), lambda l, j: (0, 0)),
            pl.BlockSpec((1, D_MODEL, tn), lambda l, j: (l, 0, j)),
            pl.BlockSpec((1, 1, tn), lambda l, j: (l, 0, j)),
        ],
        out_specs=pl.BlockSpec((1, 8, tn), lambda l, j: (l, 0, j)),
        out_shape=jax.ShapeDtypeStruct((depth, 8, 6 * D_MODEL), F32),
        compiler_params=_params(2),
        name="modulation",
    )(cond_rows, w_mod, b_mod.reshape(depth, 1, 6 * D_MODEL))


def _in_proj_kernel(*refs, rope, kv_only):
    if rope:
        x_ref, mod_ref, gpre_ref, w_ref, pavg_ref, gn_ref, cos_ref, sin_ref = refs[:8]
        outs = refs[8:]
    else:
        x_ref, mod_ref, gpre_ref, w_ref, pavg_ref, gn_ref = refs[:6]
        outs = refs[6:]
    x = x_ref[0]
    shift = mod_ref[0, :, 0:D_MODEL]
    scale = mod_ref[0, :, D_MODEL:2 * D_MODEL]
    h = _rms(x) * gpre_ref[...]
    hb = (h * (1.0 + scale) + shift).astype(BF16)

    def proj(lo, hi):
        return _dot(hb, w_ref[:, lo:hi])

    def rope_store(z, dst):
        if not rope:
            dst[0] = z.astype(BF16)
            return
        cos = cos_ref[...]
        sin = sin_ref[...]
        for hd in range(HEADS):
            a = z[:, hd * HEAD_DIM:(hd + 1) * HEAD_DIM]
            r = a * cos + pltpu.roll(a, HEAD_DIM // 2, 1) * sin
            dst[0, :, hd * HEAD_DIM:(hd + 1) * HEAD_DIM] = r.astype(BF16)

    if kv_only:
        k_ref, v_ref = outs
    else:
        q_ref, k_ref, v_ref, sg_ref, f_ref, u_ref, vn_ref, gate_ref = outs
        rope_store(proj(0, RET_W), q_ref)
    rope_store(proj(RET_W, 2 * RET_W) * (HEAD_DIM ** -0.5), k_ref)
    v_ref[0] = proj(2 * RET_W, 3 * RET_W).astype(BF16)
    if kv_only:
        return
    g = proj(3 * RET_W, 4 * RET_W)
    sg_ref[0] = (g * _sigmoid(g)).astype(BF16)

    f = proj(COL_F, COL_U)
    for hf in range(2):
        fh = f[:, hf * HALF:(hf + 1) * HALF]
        if rope:
            for n1 in range(f.shape[0] // FFT_N2):
                for nj in range(FFT_N2 // FFT_STEP):
                    f_ref[0, hf, nj, n1 * FFT_STEP:(n1 + 1) * FFT_STEP, :] = (
                        fh[n1 * FFT_N2 + nj * FFT_STEP:n1 * FFT_N2 + (nj + 1) * FFT_STEP])
        else:
            f_ref[0, hf] = fh

    u_ref[0] = _gelu(proj(COL_U, COL_VS)).astype(BF16)
    vg = _gelu(proj(COL_VS, COL_GATE))
    sh, sl = _split(vg * vg)
    pavg = pavg_ref[...]
    ms = _dot(sh, pavg) + _dot(sl, pavg)
    vn_ref[0] = (vg * lax.rsqrt(ms + EPS) * gn_ref[...]).astype(BF16)

    for c in range(3):
        z = proj(COL_GATE + c * D_MODEL, COL_GATE + (c + 1) * D_MODEL)
        gate_ref[0, :, c * D_MODEL:(c + 1) * D_MODEL] = _sigmoid(z).astype(BF16)


def _mod_spec(layer, ctx):
    idx = (lambda b, i: (layer, 2, 0, 0)) if ctx else (lambda b, i: (layer, b, 0, 0))
    return pl.BlockSpec((None, 1, 1, 6 * D_MODEL), idx)


def _in_proj(xs, layer, mod, g_pre, w_in, pavg, g_norm, rope_tabs, *, tm, ctx, kv_only=False):
    nb, rows, _ = xs.shape
    rope = rope_tabs is not None
    tok_spec = lambda w: pl.BlockSpec((1, tm, w), lambda b, i: (b, i, 0))
    in_specs = [
        tok_spec(D_MODEL),
        _mod_spec(layer, ctx),
        _layer_spec((1, D_MODEL), layer),
        _layer_spec((D_MODEL, IN_W), layer),
        _const_spec((GMLP_W, GMLP_W)),
        _layer_spec((1, GMLP_W), layer),
    ]
    args = [xs, mod, g_pre, w_in, pavg, g_norm]
    if rope:
        in_specs += [pl.BlockSpec((tm, HEAD_DIM), lambda b, i: (i, 0))] * 2
        args += list(rope_tabs)
    bf = lambda w: jax.ShapeDtypeStruct((nb, rows, w), BF16)
    if kv_only:
        out_shapes = (bf(RET_W), bf(RET_W))
        out_specs = (tok_spec(RET_W), tok_spec(RET_W))
    else:
        if rope:
            n_nj = FFT_N2 // FFT_STEP
            f_shape = jax.ShapeDtypeStruct((nb, 2, n_nj, FFT_N1 * FFT_STEP, HALF), F32)
            f_spec = pl.BlockSpec((1, 2, n_nj, (tm // FFT_N2) * FFT_STEP, HALF), lambda b, i: (b, 0, 0, i, 0))
        else:
            f_shape = jax.ShapeDtypeStruct((nb, 2, rows, HALF), F32)
            f_spec = pl.BlockSpec((1, 2, tm, HALF), lambda b, i: (b, 0, i, 0))
        out_shapes = (
            bf(RET_W), bf(RET_W), bf(RET_W), bf(RET_W),
            f_shape,
            bf(GMLP_W), bf(GMLP_W),
            bf(3 * D_MODEL),
        )
        out_specs = (
            tok_spec(RET_W), tok_spec(RET_W), tok_spec(RET_W), tok_spec(RET_W),
            f_spec,
            tok_spec(GMLP_W), tok_spec(GMLP_W), tok_spec(3 * D_MODEL),
        )
    return pl.pallas_call(
        functools.partial(_in_proj_kernel, rope=rope, kv_only=kv_only),
        grid=(nb, rows // tm),
        in_specs=in_specs,
        out_specs=out_specs,
        out_shape=out_shapes,
        compiler_params=_params(2),
        name="in_proj_ctx" if ctx else "in_proj",
    )(*args)


def _ret_kernel(*refs, ctx_out):
    (logit_ref, ql_ref, kl_ref, vl_ref, sgl_ref, kc_ref, vc_ref) = refs[:7]
    if ctx_out:
        qc_ref, sgc_ref, ol_ref, oc_ref = refs[7:11]
        scratch = refs[11:]
    else:
        ol_ref = refs[7]
        scratch = refs[8:]
    mask_ref, wf_ref, wb_ref, qf_ref, qb_ref, df_ref, db_ref, sf_ref, sb_ref, sball_ref = scratch
    b = pl.program_id(0)
    phase = pl.program_id(1)
    j = pl.program_id(2)

    @pl.when((b == 0) & (phase == 0) & (j == 0))
    def _():
        x = logit_ref[...]
        lg = -(jnp.maximum(-x, 0.0) + jnp.log(1.0 + jnp.exp(-jnp.abs(x))))
        row = lax.broadcasted_iota(jnp.int32, (RCHUNK, RCHUNK), 0).astype(F32)
        col = lax.broadcasted_iota(jnp.int32, (RCHUNK, RCHUNK), 1).astype(F32)
        diff = row - col
        pos = row[:, :HEAD_DIM]
        for hd in range(HEADS):
            lf = lg[hd:hd + 1, :]
            lb = lg[HEADS + hd:HEADS + hd + 1, :]
            mask_ref[hd] = jnp.where(diff >= 0.0, jnp.exp(lf * jnp.maximum(diff, 0.0)),
                                     jnp.exp(lb * jnp.maximum(-diff, 0.0)))
            wf_ref[hd] = jnp.exp(lf * (RCHUNK - 1.0 - pos))
            wb_ref[hd] = jnp.exp(lb * pos)
            qf_ref[hd] = jnp.exp(lf * (pos + 1.0))
            qb_ref[hd] = jnp.exp(lb * (RCHUNK - pos))
            df_ref[hd] = jnp.exp(jnp.broadcast_to(lf, (HEAD_DIM, HEAD_DIM)) * float(RCHUNK))
            db_ref[hd] = jnp.exp(jnp.broadcast_to(lb, (HEAD_DIM, HEAD_DIM)) * float(RCHUNK))

    def absorb(s, hd, kh, vh, w_ref, d_ref):
        kw = (kh.astype(F32) * w_ref[hd]).astype(BF16)
        u = lax.dot_general(kw, vh, (((0,), (0,)), ((), ())), preferred_element_type=F32)
        return d_ref[hd] * s + u

    def backward_chunks(k_ref, v_ref, n_chunks, first_chunk):
        for hd in range(HEADS):
            cols = slice(hd * HEAD_DIM, (hd + 1) * HEAD_DIM)
            s = sb_ref[hd]
            for c in reversed(range(n_chunks)):
                rows = slice(c * RCHUNK, (c + 1) * RCHUNK)
                sball_ref[first_chunk + c, hd] = s.astype(BF16)
                s = absorb(s, hd, k_ref[0, rows, cols], v_ref[0, rows, cols], wb_ref, db_ref)
            sb_ref[hd] = s

    def forward_chunks(q_ref, k_ref, v_ref, sg_ref, o_ref, n_chunks, first_chunk):
        for hd in range(HEADS):
            cols = slice(hd * HEAD_DIM, (hd + 1) * HEAD_DIM)
            s = sf_ref[hd]
            for c in range(n_chunks):
                rows = slice(c * RCHUNK, (c + 1) * RCHUNK)
                kh = k_ref[0, rows, cols]
                vh = v_ref[0, rows, cols]
                if o_ref is not None:
                    qh = q_ref[0, rows, cols]
                    sc = lax.dot_general(qh, kh, (((1,), (1,)), ((), ())), preferred_element_type=F32)
                    qf = qh.astype(F32)
                    lhs = jnp.concatenate([(sc * mask_ref[hd]).astype(BF16),
                                           (qf * qf_ref[hd]).astype(BF16),
                                           (qf * qb_ref[hd]).astype(BF16)], axis=1)
                    rhs = jnp.concatenate([vh, s.astype(BF16), sball_ref[first_chunk + c, hd]], axis=0)
                    o = _dot(lhs, rhs)
                    o_ref[0, rows, cols] = (_rms(o) * sg_ref[0, rows, cols].astype(F32)).astype(BF16)
                s = absorb(s, hd, kh, vh, wf_ref, df_ref)
            sf_ref[hd] = s

    @pl.when(phase == 0)
    def _():
        @pl.when(j == 0)
        def _():
            sb_ref[...] = jnp.zeros_like(sb_ref)
            backward_chunks(kc_ref, vc_ref, 1, 0)

        backward_chunks(kl_ref, vl_ref, RET_CPT, 1 + (RET_STEPS - 1 - j) * RET_CPT)

    @pl.when(phase == 1)
    def _():
        @pl.when(j == 0)
        def _():
            sf_ref[...] = jnp.zeros_like(sf_ref)
            if ctx_out:
                forward_chunks(qc_ref, kc_ref, vc_ref, sgc_ref, oc_ref, 1, 0)
            else:
                forward_chunks(None, kc_ref, vc_ref, None, None, 1, 0)

        forward_chunks(ql_ref, kl_ref, vl_ref, sgl_ref, ol_ref, RET_CPT, 1 + j * RET_CPT)


def _retention(layer, logits, lat, ctx, *, ctx_out):
    nb = lat[0].shape[0]
    kv_idx = lambda b, p, j: (b, jnp.where(p == 0, RET_STEPS - 1 - j, j), 0)
    q_idx = lambda b, p, j: (b, jnp.where(p == 0, 0, j), 0)
    c_idx = lambda b, p, j: (b, 0, 0)
    lat_blk = (1, RET_TILE, RET_W)
    ctx_blk = (1, CTX_LEN, RET_W)
    in_specs = [
        _layer_spec((2 * HEADS, 1), layer),
        pl.BlockSpec(lat_blk, q_idx), pl.BlockSpec(lat_blk, kv_idx), pl.BlockSpec(lat_blk, kv_idx),
        pl.BlockSpec(lat_blk, q_idx),
        pl.BlockSpec(ctx_blk, c_idx), pl.BlockSpec(ctx_blk, c_idx),
    ]
    if ctx_out:
        qc, kc, vc, sgc = ctx
        args = [logits, *lat, kc, vc, qc, sgc]
        in_specs += [pl.BlockSpec(ctx_blk, c_idx), pl.BlockSpec(ctx_blk, c_idx)]
        out_specs = (pl.BlockSpec(lat_blk, q_idx), pl.BlockSpec(ctx_blk, c_idx))
        out_shape = (jax.ShapeDtypeStruct((nb, SEQ, RET_W), BF16), jax.ShapeDtypeStruct((nb, CTX_LEN, RET_W), BF16))
    else:
        kc, vc = ctx
        args = [logits, *lat, kc, vc]
        out_specs = pl.BlockSpec(lat_blk, q_idx)
        out_shape = jax.ShapeDtypeStruct((nb, SEQ, RET_W), BF16)
    state = pltpu.VMEM((HEADS, HEAD_DIM, HEAD_DIM), F32)
    pos_tab = pltpu.VMEM((HEADS, RCHUNK, HEAD_DIM), F32)
    return pl.pallas_call(
        functools.partial(_ret_kernel, ctx_out=ctx_out),
        grid=(nb, 2, RET_STEPS),
        in_specs=in_specs,
        out_specs=out_specs,
        out_shape=out_shape,
        scratch_shapes=[
            pltpu.VMEM((HEADS, RCHUNK, RCHUNK), F32),
            pos_tab, pos_tab, pos_tab, pos_tab,
            state, state,
            state, state,
            pltpu.VMEM((N_RCHUNKS, HEADS, HEAD_DIM, HEAD_DIM), BF16),
        ],
        compiler_params=_params(3),
        name="retention",
    )(*args)


def _dft_cos_sin(n):
    idx = np.arange(n, dtype=np.float64)
    ang = 2.0 * np.pi * ((idx[:, None] * idx[None, :]) % n) / n
    return np.cos(ang), np.sin(ang)


def _channel_dft(scale):
    c, s = _dft_cos_sin(FNET_GROUP)
    eye = np.eye(FNET_W // FNET_GROUP)
    return np.concatenate([np.kron(eye, c), np.kron(eye, s)], axis=0) * scale


def _fft_constants():
    k1 = np.arange(FFT_N1, dtype=np.float64)[None, :, None]
    n1 = np.arange(FFT_N1, dtype=np.float64)[None, None, :]
    n2 = np.arange(FFT_N2, dtype=np.float64)[:, None, None]
    ang = 2.0 * np.pi * ((k1 * (FFT_N2 * n1 + n2)) % SEQ) / SEQ
    m1 = np.concatenate([np.cos(ang), -np.sin(ang)], axis=1)
    c, s = _dft_cos_sin(FFT_N2)
    m3 = np.block([[c, s], [-s, c]])
    return (_np_split_cat(m1, -1), _np_split_cat(m3, -1),
            _np_split_cat(_channel_dft((SEQ * FNET_GROUP) ** -0.5), 0))


def _ctx_fft_constants():
    c, s = _dft_cos_sin(CTX_LEN)
    return (_np_split_cat(np.concatenate([c, -s], axis=0), -1),
            _np_split_cat(_channel_dft((CTX_LEN * FNET_GROUP) ** -0.5), 0))


def _np_split_cat(a64, axis):
    hi, lo = _np_split(a64)
    return jnp.asarray(np.concatenate([hi, lo], axis=axis))


def _twice(x, axis):
    return jnp.concatenate([x, x], axis=axis)


def _fft1_kernel(x_ref, m_ref, o_ref):
    for jj in range(FFT_STEP):
        x = jnp.concatenate([x_ref[hf, pl.ds(jj, FFT_N1, stride=FFT_STEP), :] for hf in range(2)],
                            axis=1).astype(BF16)
        a = _dot(m_ref[jj], _twice(x, 0))
        for hf in range(2):
            for ri in range(2):
                for kb in range(FFT_N1 // FFT_STEP):
                    r0 = ri * FFT_N1 + kb * FFT_STEP
                    o_ref[hf, ri, kb, jj * FFT_STEP:(jj + 1) * FFT_STEP, :] = (
                        a[r0:r0 + FFT_STEP, hf * HALF:(hf + 1) * HALF])


def _fft3_kernel(a_ref, w_ref, cd_ref, o_ref):
    zs = []
    for jj in range(FFT_STEP):
        rhs = jnp.concatenate(
            [jnp.concatenate([a_ref[hf, ri, pl.ds(jj, FFT_N2, stride=FFT_STEP), :] for ri in range(2)], axis=0)
             for hf in range(2)], axis=1).astype(BF16)
        y = _dot(w_ref[...], _twice(rhs, 0))
        zs.append(jnp.concatenate([y[:FFT_N2], y[FFT_N2:]], axis=1).astype(BF16))
    z = jnp.concatenate(zs, axis=0)
    four = _dot(_twice(z, 1), cd_ref[...])
    for jj in range(FFT_STEP):
        for hf in range(2):
            o_ref[hf, jj] = four[jj * FFT_N2:(jj + 1) * FFT_N2, hf * HALF:(hf + 1) * HALF]


def _fourier_latent(f_blk):
    nb = f_blk.shape[0]
    m1, m3, cd = _fft_constants()
    n_kb = FFT_N1 // FFT_STEP
    a = pl.pallas_call(
        _fft1_kernel,
        grid=(nb, FFT_N2 // FFT_STEP),
        in_specs=[
            pl.BlockSpec((None, 2, None, FFT_N1 * FFT_STEP, HALF), lambda b, j: (b, 0, j, 0, 0)),
            pl.BlockSpec((FFT_STEP, 2 * FFT_N1, 2 * FFT_N1), lambda b, j: (j, 0, 0)),
        ],
        out_specs=pl.BlockSpec((None, 2, 2, n_kb, FFT_STEP * FFT_STEP, HALF), lambda b, j: (b, 0, 0, 0, j, 0)),
        out_shape=jax.ShapeDtypeStruct((nb, 2, 2, n_kb, FFT_N2 * FFT_STEP, HALF), F32),
        compiler_params=_params(2),
        name="fourier_stage1",
    )(f_blk, m1)
    return pl.pallas_call(
        _fft3_kernel,
        grid=(nb, n_kb),
        in_specs=[
            pl.BlockSpec((None, 2, 2, None, FFT_N2 * FFT_STEP, HALF), lambda b, j: (b, 0, 0, j, 0, 0)),
            _const_spec((2 * FFT_N2, 4 * FFT_N2)), _const_spec((4 * FNET_W, FNET_W)),
        ],
        out_specs=pl.BlockSpec((None, 2, FFT_STEP, FFT_N2, HALF), lambda b, j: (b, 0, j, 0, 0)),
        out_shape=jax.ShapeDtypeStruct((nb, 2, FFT_N1, FFT_N2, HALF), F32),
        compiler_params=_params(2),
        name="fourier_stage2",
    )(a, m3, cd)


def _fft_ctx_kernel(x_ref, w_ref, cd_ref, o_ref):
    x = jnp.concatenate([x_ref[hf] for hf in range(2)], axis=1).astype(BF16)
    z = _dot(w_ref[...], _twice(x, 0))
    z = jnp.concatenate([z[:CTX_LEN], z[CTX_LEN:]], axis=1).astype(BF16)
    four = _dot(_twice(z, 1), cd_ref[...])
    for hf in range(2):
        o_ref[hf] = four[:, hf * HALF:(hf + 1) * HALF]


def _fourier_ctx(f_ctx):
    nb = f_ctx.shape[0]
    w, cd = _ctx_fft_constants()
    return pl.pallas_call(
        _fft_ctx_kernel,
        grid=(nb,),
        in_specs=[
            pl.BlockSpec((None, 2, CTX_LEN, HALF), lambda b: (b, 0, 0, 0)),
            _const_spec((2 * CTX_LEN, 2 * CTX_LEN)), _const_spec((4 * FNET_W, FNET_W)),
        ],
        out_specs=pl.BlockSpec((None, 2, CTX_LEN, HALF), lambda b: (b, 0, 0, 0)),
        out_shape=jax.ShapeDtypeStruct((nb, 2, CTX_LEN, HALF), F32),
        compiler_params=_params(1),
        name="fourier_ctx",
    )(f_ctx, w, cd)


def _mix_mlp_kernel(x_ref, ret_ref, four_ref, u_ref, vn_ref, gate_ref, mod_ref, ws_ref, bs_ref,
                    wa_ref, wb_ref, wc_ref, wo_ref, gpm_ref, gprm_ref, gpom_ref, wup_ref, wdn_ref, o_ref, *, tm, ctx):
    group = lax.broadcasted_iota(jnp.int32, (GMLP_CHUNK, GMLP_W), 1) // GMLP_GROUP
    parts = []
    for cc in range(tm // GMLP_CHUNK):
        rows = slice(cc * GMLP_CHUNK, (cc + 1) * GMLP_CHUNK)
        vn = vn_ref[0, rows, :]
        s = bs_ref[...]
        for g in range(GMLP_W // GMLP_GROUP):
            s = s + jnp.where(group == g, _dot(ws_ref[g], vn), 0.0)
        parts.append((u_ref[0, rows, :].astype(F32) * s).astype(BF16))
    sgu = jnp.concatenate(parts, axis=0)

    if ctx:
        four = jnp.concatenate([four_ref[0, hf] for hf in range(2)], axis=1).astype(BF16)
    else:
        four = jnp.concatenate(
            [jnp.concatenate([four_ref[0, hf, :, k2, :] for k2 in range(tm // FFT_N1)], axis=0) for hf in range(2)],
            axis=1).astype(BF16)
    ga = gate_ref[0, :, 0:D_MODEL].astype(F32)
    gb = gate_ref[0, :, D_MODEL:2 * D_MODEL].astype(F32)
    gc = gate_ref[0, :, 2 * D_MODEL:3 * D_MODEL].astype(F32)
    merged = (ga * _dot(ret_ref[0], wa_ref[...]) + gb * _dot(four, wb_ref[...]) + gc * _dot(sgu, wc_ref[...]))
    y = _dot(merged.astype(BF16), wo_ref[...])

    gate1 = mod_ref[0, :, 2 * D_MODEL:3 * D_MODEL]
    shift2 = mod_ref[0, :, 3 * D_MODEL:4 * D_MODEL]
    scale2 = mod_ref[0, :, 4 * D_MODEL:5 * D_MODEL]
    gate2 = mod_ref[0, :, 5 * D_MODEL:6 * D_MODEL]
    x1 = x_ref[0] + gate1 * (_rms(y) * gpm_ref[...])
    h2 = ((_rms(x1) * gprm_ref[...]) * (1.0 + scale2) + shift2).astype(BF16)
    m = None
    for c in range(D_FF // FF_CHUNK):
        cols = slice(c * FF_CHUNK, (c + 1) * FF_CHUNK)
        a = jnp.maximum(_dot(h2, wup_ref[:, cols]), 0.0)
        part = _dot((a * a).astype(BF16), wdn_ref[cols, :])
        m = part if m is None else m + part
    o_ref[0] = x1 + gate2 * (_rms(m) * gpom_ref[...])


def _mix_mlp(xs, ret, four, ug, vn, gates, layer, mod, ws, bs_tab, wa, wb, wc, wo,
             g_post_mix, g_pre_mlp, g_post_mlp, w_up, w_down, *, tm, ctx):
    nb, rows, _ = xs.shape
    tok_spec = lambda w: pl.BlockSpec((1, tm, w), lambda b, i: (b, i, 0))
    lspec = lambda *shape: _layer_spec(shape, layer)
    if ctx:
        four_spec = pl.BlockSpec((1, 2, tm, HALF), lambda b, i: (b, 0, i, 0))
    else:
        four_spec = pl.BlockSpec((1, 2, FFT_N1, tm // FFT_N1, HALF), lambda b, i: (b, 0, 0, i, 0))
    return pl.pallas_call(
        functools.partial(_mix_mlp_kernel, tm=tm, ctx=ctx),
        grid=(nb, rows // tm),
        in_specs=[
            tok_spec(D_MODEL), tok_spec(RET_W),
            four_spec,
            tok_spec(GMLP_W), tok_spec(GMLP_W), tok_spec(3 * D_MODEL),
            _mod_spec(layer, ctx),
            lspec(GMLP_W // GMLP_GROUP, GMLP_CHUNK, GMLP_CHUNK),
            lspec(GMLP_CHUNK, GMLP_W),
            lspec(RET_W, D_MODEL), lspec(FNET_W, D_MODEL), lspec(GMLP_W, D_MODEL), lspec(D_MODEL, D_MODEL),
            lspec(1, D_MODEL), lspec(1, D_MODEL), lspec(1, D_MODEL),
            lspec(D_MODEL, D_FF), lspec(D_FF, D_MODEL),
        ],
        out_specs=tok_spec(D_MODEL),
        out_shape=jax.ShapeDtypeStruct((nb, rows, D_MODEL), F32),
        compiler_params=_params(2),
        name="mix_mlp_ctx" if ctx else "mix_mlp",
    )(xs, ret, four, ug, vn, gates, mod, ws, bs_tab, wa, wb, wc, wo,
      g_post_mix, g_pre_mlp, g_post_mlp, w_up, w_down)


def _rope_tables():
    rows = SEQ // GRID_W
    freqs = ROPE_BASE ** (-jnp.arange(ROPE_FREQS, dtype=F32) / ROPE_FREQS)
    ang_r = jnp.arange(rows, dtype=F32)[:, None] * freqs
    ang_c = jnp.arange(GRID_W, dtype=F32)[:, None] * freqs

    def table(fn):
        t = jnp.concatenate([jnp.repeat(fn(ang_r), GRID_W, axis=0), jnp.tile(fn(ang_c), (rows, 1))], axis=-1)
        return t

    cos = table(jnp.cos)
    sin = table(jnp.sin)
    return jnp.concatenate([cos, cos], axis=-1), jnp.concatenate([-sin, sin], axis=-1)


def kernel(x, c, ctx, c_ctx, w_mod, b_mod, g_pre_mix, g_post_mix, g_pre_mlp, g_post_mlp, w_in,
           ret_decay_logit, sgu_w_s, sgu_b_s, sgu_norm, w_branch_a, w_branch_b, w_branch_c, w_out,
           w_up, w_down):
    nb = x.shape[0]
    depth = w_mod.shape[0]
    assert x.shape == (nb, SEQ, D_MODEL) and ctx.shape == (nb, CTX_LEN, D_MODEL) and nb == 2

    cond_rows = jnp.concatenate([c, c_ctx[None, :], jnp.zeros((8 - nb - 1, D_MODEL), F32)], axis=0)
    mod = _modulation(cond_rows, w_mod, b_mod).reshape(depth, 8, 1, 6 * D_MODEL)
    rope_tabs = _rope_tables()
    pavg = jnp.asarray(np.kron(np.eye(GMLP_W // GMLP_GROUP), np.full((GMLP_GROUP, GMLP_GROUP), 1.0 / GMLP_GROUP)), BF16)

    vec = lambda p: p[:, None, :]
    in_w = (vec(g_pre_mix), w_in.astype(BF16), pavg, vec(sgu_norm))
    logits = ret_decay_logit.reshape(depth, 2 * HEADS, 1)
    mix_w = (sgu_w_s.astype(BF16), jnp.repeat(jnp.swapaxes(sgu_b_s, 1, 2), GMLP_GROUP, axis=2),
             w_branch_a.astype(BF16), w_branch_b.astype(BF16), w_branch_c.astype(BF16), w_out.astype(BF16),
             vec(g_post_mix), vec(g_pre_mlp), vec(g_post_mlp), w_up.astype(BF16), w_down.astype(BF16))

    for l in range(depth):
        last = l == depth - 1
        q, k, v, sg, f_lat, ug, vn, gates = _in_proj(x, l, mod, *in_w, rope_tabs, tm=TM_LAT, ctx=False)
        if last:
            kc, vc = _in_proj(ctx, l, mod, *in_w, None, tm=CTX_LEN, ctx=True, kv_only=True)
            ret = _retention(l, logits, (q, k, v, sg), (kc, vc), ctx_out=False)
        else:
            qc, kc, vc, sgc, f_ctx, ugc, vnc, gatesc = _in_proj(ctx, l, mod, *in_w, None, tm=CTX_LEN, ctx=True)
            ret, retc = _retention(l, logits, (q, k, v, sg), (qc, kc, vc, sgc), ctx_out=True)
        x = _mix_mlp(x, ret, _fourier_latent(f_lat), ug, vn, gates, l, mod, *mix_w, tm=TM_LAT, ctx=False)
        if not last:
            ctx = _mix_mlp(ctx, retc, _fourier_ctx(f_ctx), ugc, vnc, gatesc, l, mod, *mix_w, tm=CTX_LEN, ctx=True)
    return x
```

```python
import functools

import numpy as np
import jax
import jax.numpy as jnp
from jax import lax
from jax.experimental import pallas as pl
from jax.experimental.pallas import tpu as pltpu

F32 = jnp.float32
BF16 = jnp.bfloat16

D_MODEL = 1024
SEQ = 8192
CTX_LEN = 256
GRID_W = 64
RET_W = 512
HEADS = 4
HEAD_DIM = 128
ROPE_BASE = 10000.0
ROPE_FREQS = HEAD_DIM // 4
FNET_W = 256
FNET_GROUP = 64
GMLP_W = 256
GMLP_GROUP = 64
GMLP_CHUNK = 128
D_FF = 4 * D_MODEL
EPS = 1e-6
IN_W = 4 * RET_W + FNET_W + 2 * GMLP_W + 3 * D_MODEL
COL_F = 4 * RET_W
COL_U = COL_F + FNET_W
COL_VS = COL_U + GMLP_W
COL_GATE = COL_VS + GMLP_W

TM_LAT = 512
TM_IN = 1024
RCHUNK = 256
RET_TILE = 1024
RET_STEPS = SEQ // RET_TILE
RET_CPT = RET_TILE // RCHUNK
N_RCHUNKS = 1 + SEQ // RCHUNK
FF_CHUNK = 1024
FFT_N1 = 64
FFT_N2 = 128
HALF = 128
FFT_STEP = 16

_VMEM_LIMIT = 56 * 1024 * 1024


def _dot(a, b):
    return jnp.dot(a, b, preferred_element_type=F32)


def _split(x):
    hi = x.astype(BF16)
    lo = (x - hi.astype(F32)).astype(BF16)
    return hi, lo


def _np_split(a64):
    hi = np.asarray(a64, np.float32).astype(BF16)
    lo = (np.asarray(a64, np.float32) - hi.astype(np.float32)).astype(BF16)
    return hi, lo


def _rms(x):
    return x * lax.rsqrt(jnp.mean(x * x, axis=-1, keepdims=True) + EPS)


def _gelu(x):
    return x * (0.5 * (1.0 + jnp.tanh(0.7978845608028654 * (x + 0.044715 * (x * x * x)))))


def _sigmoid(x):
    return 1.0 / (1.0 + jnp.exp(-x))


def _const_spec(shape, nargs=None):
    zeros = (0,) * len(shape)
    return pl.BlockSpec(shape, lambda *_: zeros, pipeline_mode=pl.Buffered(1))


def _layer_spec(shape, layer):
    idx = (layer,) + (0,) * len(shape)
    return pl.BlockSpec((None,) + tuple(shape), lambda *_: idx, pipeline_mode=pl.Buffered(1))


def _params(n_axes):
    return pltpu.CompilerParams(dimension_semantics=("arbitrary",) * n_axes, vmem_limit_bytes=_VMEM_LIMIT)


def _mod_kernel(a_ref, w_ref, b_ref, o_ref):
    a = a_ref[...]
    a = a * _sigmoid(a)
    ah, al = _split(a)
    w = w_ref[0].astype(BF16)
    o_ref[0] = _dot(ah, w) + _dot(al, w) + b_ref[0]


def _modulation(cond_rows, w_mod, b_mod):
    depth = w_mod.shape[0]
    tn = 1536
    return pl.pallas_call(
        _mod_kernel,
        grid=(depth, (6 * D_MODEL) // tn),
        in_specs=[
            pl.BlockSpec((8, D_MODEL), lambda l, j: (0, 0)),
            pl.BlockSpec((1, D_MODEL, tn), lambda l, j: (l, 0, j)),
            pl.BlockSpec((1, 1, tn), lambda l, j: (l, 0, j)),
        ],
        out_specs=pl.BlockSpec((1, 8, tn), lambda l, j: (l, 0, j)),
        out_shape=jax.ShapeDtypeStruct((depth, 8, 6 * D_MODEL), F32),
        compiler_params=_params(2),
        name="modulation",
    )(cond_rows, w_mod, b_mod.reshape(depth, 1, 6 * D_MODEL))


def _in_proj_kernel(*refs, rope, kv_only):
    if rope:
        x_ref, mod_ref, gpre_ref, w_ref, pavg_ref, gn_ref, cos_ref, sin_ref = refs[:8]
        outs = refs[8:]
    else:
        x_ref, mod_ref, gpre_ref, w_ref, pavg_ref, gn_ref = refs[:6]
        outs = refs[6:]
    x = x_ref[0]
    shift = mod_ref[0, :, 0:D_MODEL]
    scale = mod_ref[0, :, D_MODEL:2 * D_MODEL]
    h = _rms(x) * gpre_ref[...]
    hb = (h * (1.0 + scale) + shift).astype(BF16)

    def proj(lo, hi):
        return _dot(hb, w_ref[:, lo:hi])

    def rope_store(z, dst):
        if not rope:
            dst[0] = z.astype(BF16)
            return
        cos = cos_ref[...]
        sin = sin_ref[...]
        for hd in range(HEADS):
            a = z[:, hd * HEAD_DIM:(hd + 1) * HEAD_DIM]
            r = a * cos + pltpu.roll(a, HEAD_DIM // 2, 1) * sin
            dst[0, :, hd * HEAD_DIM:(hd + 1) * HEAD_DIM] = r.astype(BF16)

    if kv_only:
        k_ref, v_ref = outs
    else:
        q_ref, k_ref, v_ref, sg_ref, f_ref, u_ref, vn_ref, gate_ref = outs
        rope_store(proj(0, RET_W), q_ref)
    rope_store(proj(RET_W, 2 * RET_W) * (HEAD_DIM ** -0.5), k_ref)
    v_ref[0] = proj(2 * RET_W, 3 * RET_W).astype(BF16)
    if kv_only:
        return
    g = proj(3 * RET_W, 4 * RET_W)
    sg_ref[0] = (g * _sigmoid(g)).astype(BF16)

    f = proj(COL_F, COL_U)
    for hf in range(2):
        fh = f[:, hf * HALF:(hf + 1) * HALF]
        if rope:
            for n1 in range(f.shape[0] // FFT_N2):
                for nj in range(FFT_N2 // FFT_STEP):
                    f_ref[0, hf, nj, n1 * FFT_STEP:(n1 + 1) * FFT_STEP, :] = (
                        fh[n1 * FFT_N2 + nj * FFT_STEP:n1 * FFT_N2 + (nj + 1) * FFT_STEP])
        else:
            f_ref[0, hf] = fh

    u_ref[0] = _gelu(proj(COL_U, COL_VS)).astype(BF16)
    vg = _gelu(proj(COL_VS, COL_GATE))
    sh, sl = _split(vg * vg)
    pavg = pavg_ref[...]
    ms = _dot(sh, pavg) + _dot(sl, pavg)
    vn_ref[0] = (vg * lax.rsqrt(ms + EPS) * gn_ref[...]).astype(BF16)

    for c in range(3):
        z = proj(COL_GATE + c * D_MODEL, COL_GATE + (c + 1) * D_MODEL)
        gate_ref[0, :, c * D_MODEL:(c + 1) * D_MODEL] = _sigmoid(z).astype(BF16)


def _mod_spec(layer, ctx):
    idx = (lambda b, i: (layer, 2, 0, 0)) if ctx else (lambda b, i: (layer, b, 0, 0))
    return pl.BlockSpec((None, 1, 1, 6 * D_MODEL), idx)


def _in_proj(xs, layer, mod, g_pre, w_in, pavg, g_norm, rope_tabs, *, tm, ctx, kv_only=False):
    nb, rows, _ = xs.shape
    rope = rope_tabs is not None
    tok_spec = lambda w: pl.BlockSpec((1, tm, w), lambda b, i: (b, i, 0))
    in_specs = [
        tok_spec(D_MODEL),
        _mod_spec(layer, ctx),
        _layer_spec((1, D_MODEL), layer),
        _layer_spec((D_MODEL, IN_W), layer),
        _const_spec((GMLP_W, GMLP_W)),
        _layer_spec((1, GMLP_W), layer),
    ]
    args = [xs, mod, g_pre, w_in, pavg, g_norm]
    if rope:
        in_specs += [pl.BlockSpec((tm, HEAD_DIM), lambda b, i: (i, 0))] * 2
        args += list(rope_tabs)
    bf = lambda w: jax.ShapeDtypeStruct((nb, rows, w), BF16)
    if kv_only:
        out_shapes = (bf(RET_W), bf(RET_W))
        out_specs = (tok_spec(RET_W), tok_spec(RET_W))
    else:
        if rope:
            n_nj = FFT_N2 // FFT_STEP
            f_shape = jax.ShapeDtypeStruct((nb, 2, n_nj, FFT_N1 * FFT_STEP, HALF), F32)
            f_spec = pl.BlockSpec((1, 2, n_nj, (tm // FFT_N2) * FFT_STEP, HALF), lambda b, i: (b, 0, 0, i, 0))
        else:
            f_shape = jax.ShapeDtypeStruct((nb, 2, rows, HALF), F32)
            f_spec = pl.BlockSpec((1, 2, tm, HALF), lambda b, i: (b, 0, i, 0))
        out_shapes = (
            bf(RET_W), bf(RET_W), bf(RET_W), bf(RET_W),
            f_shape,
            bf(GMLP_W), bf(GMLP_W),
            bf(3 * D_MODEL),
        )
        out_specs = (
            tok_spec(RET_W), tok_spec(RET_W), tok_spec(RET_W), tok_spec(RET_W),
            f_spec,
            tok_spec(GMLP_W), tok_spec(GMLP_W), tok_spec(3 * D_MODEL),
        )
    return pl.pallas_call(
        functools.partial(_in_proj_kernel, rope=rope, kv_only=kv_only),
        grid=(nb, rows // tm),
        in_specs=in_specs,
        out_specs=out_specs,
        out_shape=out_shapes,
        compiler_params=_params(2),
        name="in_proj_ctx" if ctx else "in_proj",
    )(*args)


def _ret_kernel(*refs, ctx_out):
    (logit_ref, ql_ref, kl_ref, vl_ref, sgl_ref, kc_ref, vc_ref) = refs[:7]
    if ctx_out:
        qc_ref, sgc_ref, ol_ref, oc_ref = refs[7:11]
        scratch = refs[11:]
    else:
        ol_ref = refs[7]
        scratch = refs[8:]
    mask_ref, wf_ref, wb_ref, qf_ref, qb_ref, df_ref, db_ref, sf_ref, sb_ref, sball_ref = scratch
    b = pl.program_id(0)
    phase = pl.program_id(1)
    j = pl.program_id(2)

    @pl.when((b == 0) & (phase == 0) & (j == 0))
    def _():
        x = logit_ref[...]
        lg = -(jnp.maximum(-x, 0.0) + jnp.log(1.0 + jnp.exp(-jnp.abs(x))))
        row = lax.broadcasted_iota(jnp.int32, (RCHUNK, RCHUNK), 0).astype(F32)
        col = lax.broadcasted_iota(jnp.int32, (RCHUNK, RCHUNK), 1).astype(F32)
        diff = row - col
        pos = row[:, :HEAD_DIM]
        for hd in range(HEADS):
            lf = lg[hd:hd + 1, :]
            lb = lg[HEADS + hd:HEADS + hd + 1, :]
            mask_ref[hd] = jnp.where(diff >= 0.0, jnp.exp(lf * jnp.maximum(diff, 0.0)),
                                     jnp.exp(lb * jnp.maximum(-diff, 0.0)))
            wf_ref[hd] = jnp.exp(lf * (RCHUNK - 1.0 - pos))
            wb_ref[hd] = jnp.exp(lb * pos)
            qf_ref[hd] = jnp.exp(lf * (pos + 1.0))
            qb_ref[hd] = jnp.exp(lb * (RCHUNK - pos))
            df_ref[hd] = jnp.exp(jnp.broadcast_to(lf, (HEAD_DIM, HEAD_DIM)) * float(RCHUNK))
            db_ref[hd] = jnp.exp(jnp.broadcast_to(lb, (HEAD_DIM, HEAD_DIM)) * float(RCHUNK))

    def absorb(s, hd, kh, vh, w_ref, d_ref):
        kw = (kh.astype(F32) * w_ref[hd]).astype(BF16)
        u = lax.dot_general(kw, vh, (((0,), (0,)), ((), ())), preferred_element_type=F32)
        return d_ref[hd] * s + u

    def backward_chunks(k_ref, v_ref, n_chunks, first_chunk):
        for hd in range(HEADS):
            cols = slice(hd * HEAD_DIM, (hd + 1) * HEAD_DIM)
            s = sb_ref[hd]
            for c in reversed(range(n_chunks)):
                rows = slice(c * RCHUNK, (c + 1) * RCHUNK)
                sball_ref[first_chunk + c, hd] = s.astype(BF16)
                s = absorb(s, hd, k_ref[0, rows, cols], v_ref[0, rows, cols], wb_ref, db_ref)
            sb_ref[hd] = s

    def forward_chunks(q_ref, k_ref, v_ref, sg_ref, o_ref, n_chunks, first_chunk):
        for hd in range(HEADS):
            cols = slice(hd * HEAD_DIM, (hd + 1) * HEAD_DIM)
            s = sf_ref[hd]
            for c in range(n_chunks):
                rows = slice(c * RCHUNK, (c + 1) * RCHUNK)
                kh = k_ref[0, rows, cols]
                vh = v_ref[0, rows, cols]
                if o_ref is not None:
                    qh = q_ref[0, rows, cols]
                    sc = lax.dot_general(qh, kh, (((1,), (1,)), ((), ())), preferred_element_type=F32)
                    qf = qh.astype(F32)
                    lhs = jnp.concatenate([(sc * mask_ref[hd]).astype(BF16),
                                           (qf * qf_ref[hd]).astype(BF16),
                                           (qf * qb_ref[hd]).astype(BF16)], axis=1)
                    rhs = jnp.concatenate([vh, s.astype(BF16), sball_ref[first_chunk + c, hd]], axis=0)
                    o = _dot(lhs, rhs)
                    o_ref[0, rows, cols] = (_rms(o) * sg_ref[0, rows, cols].astype(F32)).astype(BF16)
                s = absorb(s, hd, kh, vh, wf_ref, df_ref)
            sf_ref[hd] = s

    @pl.when(phase == 0)
    def _():
        @pl.when(j == 0)
        def _():
            sb_ref[...] = jnp.zeros_like(sb_ref)
            backward_chunks(kc_ref, vc_ref, 1, 0)

        backward_chunks(kl_ref, vl_ref, RET_CPT, 1 + (RET_STEPS - 1 - j) * RET_CPT)

    @pl.when(phase == 1)
    def _():
        @pl.when(j == 0)
        def _():
            sf_ref[...] = jnp.zeros_like(sf_ref)
            if ctx_out:
                forward_chunks(qc_ref, kc_ref, vc_ref, sgc_ref, oc_ref, 1, 0)
            else:
                forward_chunks(None, kc_ref, vc_ref, None, None, 1, 0)

        forward_chunks(ql_ref, kl_ref, vl_ref, sgl_ref, ol_ref, RET_CPT, 1 + j * RET_CPT)


def _retention(layer, logits, lat, ctx, *, ctx_out):
    nb = lat[0].shape[0]
    kv_idx = lambda b, p, j: (b, jnp.where(p == 0, RET_STEPS - 1 - j, j), 0)
    q_idx = lambda b, p, j: (b, jnp.where(p == 0, 0, j), 0)
    c_idx = lambda b, p, j: (b, 0, 0)
    lat_blk = (1, RET_TILE, RET_W)
    ctx_blk = (1, CTX_LEN, RET_W)
    in_specs = [
        _layer_spec((2 * HEADS, 1), layer),
        pl.BlockSpec(lat_blk, q_idx), pl.BlockSpec(lat_blk, kv_idx), pl.BlockSpec(lat_blk, kv_idx),
        pl.BlockSpec(lat_blk, q_idx),
        pl.BlockSpec(ctx_blk, c_idx), pl.BlockSpec(ctx_blk, c_idx),
    ]
    if ctx_out:
        qc, kc, vc, sgc = ctx
        args = [logits, *lat, kc, vc, qc, sgc]
        in_specs += [pl.BlockSpec(ctx_blk, c_idx), pl.BlockSpec(ctx_blk, c_idx)]
        out_specs = (pl.BlockSpec(lat_blk, q_idx), pl.BlockSpec(ctx_blk, c_idx))
        out_shape = (jax.ShapeDtypeStruct((nb, SEQ, RET_W), BF16), jax.ShapeDtypeStruct((nb, CTX_LEN, RET_W), BF16))
    else:
        kc, vc = ctx
        args = [logits, *lat, kc, vc]
        out_specs = pl.BlockSpec(lat_blk, q_idx)
        out_shape = jax.ShapeDtypeStruct((nb, SEQ, RET_W), BF16)
    state = pltpu.VMEM((HEADS, HEAD_DIM, HEAD_DIM), F32)
    pos_tab = pltpu.VMEM((HEADS, RCHUNK, HEAD_DIM), F32)
    return pl.pallas_call(
        functools.partial(_ret_kernel, ctx_out=ctx_out),
        grid=(nb, 2, RET_STEPS),
        in_specs=in_specs,
        out_specs=out_specs,
        out_shape=out_shape,
        scratch_shapes=[
            pltpu.VMEM((HEADS, RCHUNK, RCHUNK), F32),
            pos_tab, pos_tab, pos_tab, pos_tab,
            state, state,
            state, state,
            pltpu.VMEM((N_RCHUNKS, HEADS, HEAD_DIM, HEAD_DIM), BF16),
        ],
        compiler_params=_params(3),
        name="retention",
    )(*args)


def _dft_cos_sin(n):
    idx = np.arange(n, dtype=np.float64)
    ang = 2.0 * np.pi * ((idx[:, None] * idx[None, :]) % n) / n
    return np.cos(ang), np.sin(ang)


def _channel_dft(scale):
    c, s = _dft_cos_sin(FNET_GROUP)
    eye = np.eye(FNET_W // FNET_GROUP)
    return np.concatenate([np.kron(eye, c), np.kron(eye, s)], axis=0) * scale


def _fft_constants():
    k1 = np.arange(FFT_N1, dtype=np.float64)[None, :, None]
    n1 = np.arange(FFT_N1, dtype=np.float64)[None, None, :]
    n2 = np.arange(FFT_N2, dtype=np.float64)[:, None, None]
    ang = 2.0 * np.pi * ((k1 * (FFT_N2 * n1 + n2)) % SEQ) / SEQ
    m1 = np.concatenate([np.cos(ang), -np.sin(ang)], axis=1)
    c, s = _dft_cos_sin(FFT_N2)
    m3 = np.block([[c, s], [-s, c]])
    return (_np_split_cat(m1, -1), _np_split_cat(m3, -1),
            _np_split_cat(_channel_dft((SEQ * FNET_GROUP) ** -0.5), 0))


def _ctx_fft_constants():
    c, s = _dft_cos_sin(CTX_LEN)
    return (_np_split_cat(np.concatenate([c, -s], axis=0), -1),
            _np_split_cat(_channel_dft((CTX_LEN * FNET_GROUP) ** -0.5), 0))


def _np_split_cat(a64, axis):
    hi, lo = _np_split(a64)
    return jnp.asarray(np.concatenate([hi, lo], axis=axis))


def _twice(x, axis):
    return jnp.concatenate([x, x], axis=axis)


def _fft1_kernel(x_ref, m_ref, o_ref):
    for jj in range(FFT_STEP):
        x = jnp.concatenate([x_ref[hf, pl.ds(jj, FFT_N1, stride=FFT_STEP), :] for hf in range(2)],
                            axis=1).astype(BF16)
        a = _dot(m_ref[jj], _twice(x, 0))
        for hf in range(2):
            for ri in range(2):
                for kb in range(FFT_N1 // FFT_STEP):
                    r0 = ri * FFT_N1 + kb * FFT_STEP
                    o_ref[hf, ri, kb, jj * FFT_STEP:(jj + 1) * FFT_STEP, :] = (
                        a[r0:r0 + FFT_STEP, hf * HALF:(hf + 1) * HALF])


def _fft3_kernel(a_ref, w_ref, cd_ref, o_ref):
    zs = []
    for jj in range(FFT_STEP):
        rhs = jnp.concatenate(
            [jnp.concatenate([a_ref[hf, ri, pl.ds(jj, FFT_N2, stride=FFT_STEP), :] for ri in range(2)], axis=0)
             for hf in range(2)], axis=1).astype(BF16)
        y = _dot(w_ref[...], _twice(rhs, 0))
        zs.append(jnp.concatenate([y[:FFT_N2], y[FFT_N2:]], axis=1).astype(BF16))
    z = jnp.concatenate(zs, axis=0)
    four = _dot(_twice(z, 1), cd_ref[...])
    for jj in range(FFT_STEP):
        for hf in range(2):
            o_ref[hf, jj] = four[jj * FFT_N2:(jj + 1) * FFT_N2, hf * HALF:(hf + 1) * HALF]


def _fourier_latent(f_blk):
    nb = f_blk.shape[0]
    m1, m3, cd = _fft_constants()
    n_kb = FFT_N1 // FFT_STEP
    a = pl.pallas_call(
        _fft1_kernel,
        grid=(nb, FFT_N2 // FFT_STEP),
        in_specs=[
            pl.BlockSpec((None, 2, None, FFT_N1 * FFT_STEP, HALF), lambda b, j: (b, 0, j, 0, 0)),
            pl.BlockSpec((FFT_STEP, 2 * FFT_N1, 2 * FFT_N1), lambda b, j: (j, 0, 0)),
        ],
        out_specs=pl.BlockSpec((None, 2, 2, n_kb, FFT_STEP * FFT_STEP, HALF), lambda b, j: (b, 0, 0, 0, j, 0)),
        out_shape=jax.ShapeDtypeStruct((nb, 2, 2, n_kb, FFT_N2 * FFT_STEP, HALF), F32),
        compiler_params=_params(2),
        name="fourier_stage1",
    )(f_blk, m1)
    return pl.pallas_call(
        _fft3_kernel,
        grid=(nb, n_kb),
        in_specs=[
            pl.BlockSpec((None, 2, 2, None, FFT_N2 * FFT_STEP, HALF), lambda b, j: (b, 0, 0, j, 0, 0)),
            _const_spec((2 * FFT_N2, 4 * FFT_N2)), _const_spec((4 * FNET_W, FNET_W)),
        ],
        out_specs=pl.BlockSpec((None, 2, FFT_STEP, FFT_N2, HALF), lambda b, j: (b, 0, j, 0, 0)),
        out_shape=jax.ShapeDtypeStruct((nb, 2, FFT_N1, FFT_N2, HALF), F32),
        compiler_params=_params(2),
        name="fourier_stage2",
    )(a, m3, cd)


def _fft_ctx_kernel(x_ref, w_ref, cd_ref, o_ref):
    x = jnp.concatenate([x_ref[hf] for hf in range(2)], axis=1).astype(BF16)
    z = _dot(w_ref[...], _twice(x, 0))
    z = jnp.concatenate([z[:CTX_LEN], z[CTX_LEN:]], axis=1).astype(BF16)
    four = _dot(_twice(z, 1), cd_ref[...])
    for hf in range(2):
        o_ref[hf] = four[:, hf * HALF:(hf + 1) * HALF]


def _fourier_ctx(f_ctx):
    nb = f_ctx.shape[0]
    w, cd = _ctx_fft_constants()
    return pl.pallas_call(
        _fft_ctx_kernel,
        grid=(nb,),
        in_specs=[
            pl.BlockSpec((None, 2, CTX_LEN, HALF), lambda b: (b, 0, 0, 0)),
            _const_spec((2 * CTX_LEN, 2 * CTX_LEN)), _const_spec((4 * FNET_W, FNET_W)),
        ],
        out_specs=pl.BlockSpec((None, 2, CTX_LEN, HALF), lambda b: (b, 0, 0, 0)),
        out_shape=jax.ShapeDtypeStruct((nb, 2, CTX_LEN, HALF), F32),
        compiler_params=_params(1),
        name="fourier_ctx",
    )(f_ctx, w, cd)


def _mix_mlp_kernel(x_ref, ret_ref, four_ref, u_ref, vn_ref, gate_ref, mod_ref, ws_ref, bs_ref,
                    wa_ref, wb_ref, wc_ref, wo_ref, gpm_ref, gprm_ref, gpom_ref, wup_ref, wdn_ref, o_ref, *, tm, ctx):
    group = lax.broadcasted_iota(jnp.int32, (GMLP_CHUNK, GMLP_W), 1) // GMLP_GROUP
    parts = []
    for cc in range(tm // GMLP_CHUNK):
        rows = slice(cc * GMLP_CHUNK, (cc + 1) * GMLP_CHUNK)
        vn = vn_ref[0, rows, :]
        s = bs_ref[...]
        for g in range(GMLP_W // GMLP_GROUP):
            s = s + jnp.where(group == g, _dot(ws_ref[g], vn), 0.0)
        parts.append((u_ref[0, rows, :].astype(F32) * s).astype(BF16))
    sgu = jnp.concatenate(parts, axis=0)

    if ctx:
        four = jnp.concatenate([four_ref[0, hf] for hf in range(2)], axis=1).astype(BF16)
    else:
        four = jnp.concatenate(
            [jnp.concatenate([four_ref[0, hf, :, k2, :] for k2 in range(tm // FFT_N1)], axis=0) for hf in range(2)],
            axis=1).astype(BF16)
    ga = gate_ref[0, :, 0:D_MODEL].astype(F32)
    gb = gate_ref[0, :, D_MODEL:2 * D_MODEL].astype(F32)
    gc = gate_ref[0, :, 2 * D_MODEL:3 * D_MODEL].astype(F32)
    merged = (ga * _dot(ret_ref[0], wa_ref[...]) + gb * _dot(four, wb_ref[...]) + gc * _dot(sgu, wc_ref[...]))
    y = _dot(merged.astype(BF16), wo_ref[...])

    gate1 = mod_ref[0, :, 2 * D_MODEL:3 * D_MODEL]
    shift2 = mod_ref[0, :, 3 * D_MODEL:4 * D_MODEL]
    scale2 = mod_ref[0, :, 4 * D_MODEL:5 * D_MODEL]
    gate2 = mod_ref[0, :, 5 * D_MODEL:6 * D_MODEL]
    x1 = x_ref[0] + gate1 * (_rms(y) * gpm_ref[...])
    h2 = ((_rms(x1) * gprm_ref[...]) * (1.0 + scale2) + shift2).astype(BF16)
    m = None
    for c in range(D_FF // FF_CHUNK):
        cols = slice(c * FF_CHUNK, (c + 1) * FF_CHUNK)
        a = jnp.maximum(_dot(h2, wup_ref[:, cols]), 0.0)
        part = _dot((a * a).astype(BF16), wdn_ref[cols, :])
        m = part if m is None else m + part
    o_ref[0] = x1 + gate2 * (_rms(m) * gpom_ref[...])


def _mix_mlp(xs, ret, four, ug, vn, gates, layer, mod, ws, bs_tab, wa, wb, wc, wo,
             g_post_mix, g_pre_mlp, g_post_mlp, w_up, w_down, *, tm, ctx):
    nb, rows, _ = xs.shape
    tok_spec = lambda w: pl.BlockSpec((1, tm, w), lambda b, i: (b, i, 0))
    lspec = lambda *shape: _layer_spec(shape, layer)
    if ctx:
        four_spec = pl.BlockSpec((1, 2, tm, HALF), lambda b, i: (b, 0, i, 0))
    else:
        four_spec = pl.BlockSpec((1, 2, FFT_N1, tm // FFT_N1, HALF), lambda b, i: (b, 0, 0, i, 0))
    return pl.pallas_call(
        functools.partial(_mix_mlp_kernel, tm=tm, ctx=ctx),
        grid=(nb, rows // tm),
        in_specs=[
            tok_spec(D_MODEL), tok_spec(RET_W),
            four_spec,
            tok_spec(GMLP_W), tok_spec(GMLP_W), tok_spec(3 * D_MODEL),
            _mod_spec(layer, ctx),
            lspec(GMLP_W // GMLP_GROUP, GMLP_CHUNK, GMLP_CHUNK),
            lspec(GMLP_CHUNK, GMLP_W),
            lspec(RET_W, D_MODEL), lspec(FNET_W, D_MODEL), lspec(GMLP_W, D_MODEL), lspec(D_MODEL, D_MODEL),
            lspec(1, D_MODEL), lspec(1, D_MODEL), lspec(1, D_MODEL),
            lspec(D_MODEL, D_FF), lspec(D_FF, D_MODEL),
        ],
        out_specs=tok_spec(D_MODEL),
        out_shape=jax.ShapeDtypeStruct((nb, rows, D_MODEL), F32),
        compiler_params=_params(2),
        name="mix_mlp_ctx" if ctx else "mix_mlp",
    )(xs, ret, four, ug, vn, gates, mod, ws, bs_tab, wa, wb, wc, wo,
      g_post_mix, g_pre_mlp, g_post_mlp, w_up, w_down)


def _rope_tables():
    rows = SEQ // GRID_W
    freqs = ROPE_BASE ** (-jnp.arange(ROPE_FREQS, dtype=F32) / ROPE_FREQS)
    ang_r = jnp.arange(rows, dtype=F32)[:, None] * freqs
    ang_c = jnp.arange(GRID_W, dtype=F32)[:, None] * freqs

    def table(fn):
        t = jnp.concatenate([jnp.repeat(fn(ang_r), GRID_W, axis=0), jnp.tile(fn(ang_c), (rows, 1))], axis=-1)
        return t

    cos = table(jnp.cos)
    sin = table(jnp.sin)
    return jnp.concatenate([cos, cos], axis=-1), jnp.concatenate([-sin, sin], axis=-1)


def kernel(x, c, ctx, c_ctx, w_mod, b_mod, g_pre_mix, g_post_mix, g_pre_mlp, g_post_mlp, w_in,
           ret_decay_logit, sgu_w_s, sgu_b_s, sgu_norm, w_branch_a, w_branch_b, w_branch_c, w_out,
           w_up, w_down):
    nb = x.shape[0]
    depth = w_mod.shape[0]
    assert x.shape == (nb, SEQ, D_MODEL) and ctx.shape == (nb, CTX_LEN, D_MODEL) and nb == 2

    cond_rows = jnp.concatenate([c, c_ctx[None, :], jnp.zeros((8 - nb - 1, D_MODEL), F32)], axis=0)
    mod = _modulation(cond_rows, w_mod, b_mod).reshape(depth, 8, 1, 6 * D_MODEL)
    rope_tabs = _rope_tables()
    pavg = jnp.asarray(np.kron(np.eye(GMLP_W // GMLP_GROUP), np.full((GMLP_GROUP, GMLP_GROUP), 1.0 / GMLP_GROUP)), BF16)

    vec = lambda p: p[:, None, :]
    in_w = (vec(g_pre_mix), w_in.astype(BF16), pavg, vec(sgu_norm))
    logits = ret_decay_logit.reshape(depth, 2 * HEADS, 1)
    mix_w = (sgu_w_s.astype(BF16), jnp.repeat(jnp.swapaxes(sgu_b_s, 1, 2), GMLP_GROUP, axis=2),
             w_branch_a.astype(BF16), w_branch_b.astype(BF16), w_branch_c.astype(BF16), w_out.astype(BF16),
             vec(g_post_mix), vec(g_pre_mlp), vec(g_post_mlp), w_up.astype(BF16), w_down.astype(BF16))

    for l in range(depth):
        last = l == depth - 1
        q, k, v, sg, f_lat, ug, vn, gates = _in_proj(x, l, mod, *in_w, rope_tabs, tm=TM_IN, ctx=False)
        if last:
            kc, vc = _in_proj(ctx, l, mod, *in_w, None, tm=CTX_LEN, ctx=True, kv_only=True)
            ret = _retention(l, logits, (q, k, v, sg), (kc, vc), ctx_out=False)
        else:
            qc, kc, vc, sgc, f_ctx, ugc, vnc, gatesc = _in_proj(ctx, l, mod, *in_w, None, tm=CTX_LEN, ctx=True)
            ret, retc = _retention(l, logits, (q, k, v, sg), (qc, kc, vc, sgc), ctx_out=True)
        x = _mix_mlp(x, ret, _fourier_latent(f_lat), ug, vn, gates, l, mod, *mix_w, tm=TM_LAT, ctx=False)
        if not last:
            ctx = _mix_mlp(ctx, retc, _fourier_ctx(f_ctx), ugc, vnc, gatesc, l, mod, *mix_w, tm=CTX_LEN, ctx=True)
    return x
```

```python
import functools

import numpy as np
import jax
import jax.numpy as jnp
from jax import lax
from jax.experimental import pallas as pl
from jax.experimental.pallas import tpu as pltpu

F32 = jnp.float32
BF16 = jnp.bfloat16

D_MODEL = 1024
SEQ = 8192
CTX_LEN = 256
GRID_W = 64
RET_W = 512
HEADS = 4
HEAD_DIM = 128
ROPE_BASE = 10000.0
ROPE_FREQS = HEAD_DIM // 4
FNET_W = 256
FNET_GROUP = 64
GMLP_W = 256
GMLP_GROUP = 64
GMLP_CHUNK = 128
D_FF = 4 * D_MODEL
EPS = 1e-6
IN_W = 4 * RET_W + FNET_W + 2 * GMLP_W + 3 * D_MODEL
COL_F = 4 * RET_W
COL_U = COL_F + FNET_W
COL_VS = COL_U + GMLP_W
COL_GATE = COL_VS + GMLP_W

TM_LAT = 512
TM_IN = 1024
RCHUNK = 256
RET_TILE = 1024
RET_STEPS = SEQ // RET_TILE
RET_CPT = RET_TILE // RCHUNK
N_RCHUNKS = 1 + SEQ // RCHUNK
FF_CHUNK = 1024
FFT_N1 = 64
FFT_N2 = 128
HALF = 128
FFT_STEP = 16
MIX_K2 = TM_LAT // FFT_N1

_VMEM_LIMIT = 56 * 1024 * 1024


def _dot(a, b):
    return jnp.dot(a, b, preferred_element_type=F32)


def _split(x):
    hi = x.astype(BF16)
    lo = (x - hi.astype(F32)).astype(BF16)
    return hi, lo


def _np_split(a64):
    hi = np.asarray(a64, np.float32).astype(BF16)
    lo = (np.asarray(a64, np.float32) - hi.astype(np.float32)).astype(BF16)
    return hi, lo


def _rms(x):
    return x * lax.rsqrt(jnp.mean(x * x, axis=-1, keepdims=True) + EPS)


def _gelu(x):
    return x * (0.5 * (1.0 + jnp.tanh(0.7978845608028654 * (x + 0.044715 * (x * x * x)))))


def _sigmoid(x):
    return 1.0 / (1.0 + jnp.exp(-x))


def _const_spec(shape, nargs=None):
    zeros = (0,) * len(shape)
    return pl.BlockSpec(shape, lambda *_: zeros, pipeline_mode=pl.Buffered(1))


def _layer_spec(shape, layer):
    idx = (layer,) + (0,) * len(shape)
    return pl.BlockSpec((None,) + tuple(shape), lambda *_: idx, pipeline_mode=pl.Buffered(1))


def _params(n_axes):
    return pltpu.CompilerParams(dimension_semantics=("arbitrary",) * n_axes, vmem_limit_bytes=_VMEM_LIMIT)


def _mod_kernel(a_ref, w_ref, b_ref, o_ref):
    a = a_ref[...]
    a = a * _sigmoid(a)
    ah, al = _split(a)
    w = w_ref[0].astype(BF16)
    o_ref[0] = _dot(ah, w) + _dot(al, w) + b_ref[0]


def _modulation(cond_rows, w_mod, b_mod):
    depth = w_mod.shape[0]
    tn = 1536
    return pl.pallas_call(
        _mod_kernel,
        grid=(depth, (6 * D_MODEL) // tn),
        in_specs=[
            pl.BlockSpec((8, D_MODEL), lambda l, j: (0, 0)),
            pl.BlockSpec((1, D_MODEL, tn), lambda l, j: (l, 0, j)),
            pl.BlockSpec((1, 1, tn), lambda l, j: (l, 0, j)),
        ],
        out_specs=pl.BlockSpec((1, 8, tn), lambda l, j: (l, 0, j)),
        out_shape=jax.ShapeDtypeStruct((depth, 8, 6 * D_MODEL), F32),
        compiler_params=_params(2),
        name="modulation",
    )(cond_rows, w_mod, b_mod.reshape(depth, 1, 6 * D_MODEL))


def _in_proj_kernel(*refs, rope, kv_only):
    if rope:
        x_ref, mod_ref, gpre_ref, w_ref, pavg_ref, gn_ref, cos_ref, sin_ref = refs[:8]
        outs = refs[8:]
    else:
        x_ref, mod_ref, gpre_ref, w_ref, pavg_ref, gn_ref = refs[:6]
        outs = refs[6:]
    x = x_ref[0]
    shift = mod_ref[0, :, 0:D_MODEL]
    scale = mod_ref[0, :, D_MODEL:2 * D_MODEL]
    h = _rms(x) * gpre_ref[...]
    hb = (h * (1.0 + scale) + shift).astype(BF16)

    def proj(lo, hi):
        return _dot(hb, w_ref[:, lo:hi])

    def rope_store(z, dst):
        if not rope:
            dst[0] = z.astype(BF16)
            return
        cos = cos_ref[...]
        sin = sin_ref[...]
        for hd in range(HEADS):
            a = z[:, hd * HEAD_DIM:(hd + 1) * HEAD_DIM]
            r = a * cos + pltpu.roll(a, HEAD_DIM // 2, 1) * sin
            dst[0, :, hd * HEAD_DIM:(hd + 1) * HEAD_DIM] = r.astype(BF16)

    if kv_only:
        k_ref, v_ref = outs
    else:
        q_ref, k_ref, v_ref, sg_ref, f_ref, u_ref, vn_ref, gate_ref = outs
        rope_store(proj(0, RET_W), q_ref)
    rope_store(proj(RET_W, 2 * RET_W) * (HEAD_DIM ** -0.5), k_ref)
    v_ref[0] = proj(2 * RET_W, 3 * RET_W).astype(BF16)
    if kv_only:
        return
    g = proj(3 * RET_W, 4 * RET_W)
    sg_ref[0] = (g * _sigmoid(g)).astype(BF16)

    f = proj(COL_F, COL_U)
    for hf in range(2):
        fh = f[:, hf * HALF:(hf + 1) * HALF]
        if rope:
            for n1 in range(f.shape[0] // FFT_N2):
                for nj in range(FFT_N2 // FFT_STEP):
                    f_ref[0, hf, nj, n1 * FFT_STEP:(n1 + 1) * FFT_STEP, :] = (
                        fh[n1 * FFT_N2 + nj * FFT_STEP:n1 * FFT_N2 + (nj + 1) * FFT_STEP])
        else:
            f_ref[0, hf] = fh

    u_ref[0] = _gelu(proj(COL_U, COL_VS)).astype(BF16)
    vg = _gelu(proj(COL_VS, COL_GATE))
    sh, sl = _split(vg * vg)
    pavg = pavg_ref[...]
    ms = _dot(sh, pavg) + _dot(sl, pavg)
    vn_ref[0] = (vg * lax.rsqrt(ms + EPS) * gn_ref[...]).astype(BF16)

    for c in range(3):
        z = proj(COL_GATE + c * D_MODEL, COL_GATE + (c + 1) * D_MODEL)
        gate_ref[0, :, c * D_MODEL:(c + 1) * D_MODEL] = _sigmoid(z).astype(BF16)


def _mod_spec(layer, ctx):
    idx = (lambda b, i: (layer, 2, 0, 0)) if ctx else (lambda b, i: (layer, b, 0, 0))
    return pl.BlockSpec((None, 1, 1, 6 * D_MODEL), idx)


def _in_proj(xs, layer, mod, g_pre, w_in, pavg, g_norm, rope_tabs, *, tm, ctx, kv_only=False):
    nb, rows, _ = xs.shape
    rope = rope_tabs is not None
    tok_spec = lambda w: pl.BlockSpec((1, tm, w), lambda b, i: (b, i, 0))
    in_specs = [
        tok_spec(D_MODEL),
        _mod_spec(layer, ctx),
        _layer_spec((1, D_MODEL), layer),
        _layer_spec((D_MODEL, IN_W), layer),
        _const_spec((GMLP_W, GMLP_W)),
        _layer_spec((1, GMLP_W), layer),
    ]
    args = [xs, mod, g_pre, w_in, pavg, g_norm]
    if rope:
        in_specs += [pl.BlockSpec((tm, HEAD_DIM), lambda b, i: (i, 0))] * 2
        args += list(rope_tabs)
    bf = lambda w: jax.ShapeDtypeStruct((nb, rows, w), BF16)
    if kv_only:
        out_shapes = (bf(RET_W), bf(RET_W))
        out_specs = (tok_spec(RET_W), tok_spec(RET_W))
    else:
        if rope:
            n_nj = FFT_N2 // FFT_STEP
            f_shape = jax.ShapeDtypeStruct((nb, 2, n_nj, FFT_N1 * FFT_STEP, HALF), F32)
            f_spec = pl.BlockSpec((1, 2, n_nj, (tm // FFT_N2) * FFT_STEP, HALF), lambda b, i: (b, 0, 0, i, 0))
        else:
            f_shape = jax.ShapeDtypeStruct((nb, 2, rows, HALF), F32)
            f_spec = pl.BlockSpec((1, 2, tm, HALF), lambda b, i: (b, 0, i, 0))
        out_shapes = (
            bf(RET_W), bf(RET_W), bf(RET_W), bf(RET_W),
            f_shape,
            bf(GMLP_W), bf(GMLP_W),
            bf(3 * D_MODEL),
        )
        out_specs = (
            tok_spec(RET_W), tok_spec(RET_W), tok_spec(RET_W), tok_spec(RET_W),
            f_spec,
            tok_spec(GMLP_W), tok_spec(GMLP_W), tok_spec(3 * D_MODEL),
        )
    return pl.pallas_call(
        functools.partial(_in_proj_kernel, rope=rope, kv_only=kv_only),
        grid=(nb, rows // tm),
        in_specs=in_specs,
        out_specs=out_specs,
        out_shape=out_shapes,
        compiler_params=_params(2),
        name="in_proj_ctx" if ctx else "in_proj",
    )(*args)


def _ret_kernel(*refs, ctx_out):
    (logit_ref, ql_ref, kl_ref, vl_ref, sgl_ref, kc_ref, vc_ref) = refs[:7]
    if ctx_out:
        qc_ref, sgc_ref, ol_ref, oc_ref = refs[7:11]
        scratch = refs[11:]
    else:
        ol_ref = refs[7]
        scratch = refs[8:]
    mask_ref, wf_ref, wb_ref, qf_ref, qb_ref, df_ref, db_ref, sf_ref, sb_ref, sball_ref = scratch
    b = pl.program_id(0)
    phase = pl.program_id(1)
    j = pl.program_id(2)

    @pl.when((b == 0) & (phase == 0) & (j == 0))
    def _():
        x = logit_ref[...]
        lg = -(jnp.maximum(-x, 0.0) + jnp.log(1.0 + jnp.exp(-jnp.abs(x))))
        row = lax.broadcasted_iota(jnp.int32, (RCHUNK, RCHUNK), 0).astype(F32)
        col = lax.broadcasted_iota(jnp.int32, (RCHUNK, RCHUNK), 1).astype(F32)
        diff = row - col
        pos = row[:, :HEAD_DIM]
        for hd in range(HEADS):
            lf = lg[hd:hd + 1, :]
            lb = lg[HEADS + hd:HEADS + hd + 1, :]
            mask_ref[hd] = jnp.where(diff >= 0.0, jnp.exp(lf * jnp.maximum(diff, 0.0)),
                                     jnp.exp(lb * jnp.maximum(-diff, 0.0)))
            wf_ref[hd] = jnp.exp(lf * (RCHUNK - 1.0 - pos))
            wb_ref[hd] = jnp.exp(lb * pos)
            qf_ref[hd] = jnp.exp(lf * (pos + 1.0))
            qb_ref[hd] = jnp.exp(lb * (RCHUNK - pos))
            df_ref[hd] = jnp.exp(jnp.broadcast_to(lf, (HEAD_DIM, HEAD_DIM)) * float(RCHUNK))
            db_ref[hd] = jnp.exp(jnp.broadcast_to(lb, (HEAD_DIM, HEAD_DIM)) * float(RCHUNK))

    def absorb(s, hd, kh, vh, w_ref, d_ref):
        kw = (kh.astype(F32) * w_ref[hd]).astype(BF16)
        u = lax.dot_general(kw, vh, (((0,), (0,)), ((), ())), preferred_element_type=F32)
        return d_ref[hd] * s + u

    def backward_chunks(k_ref, v_ref, n_chunks, first_chunk):
        for hd in range(HEADS):
            cols = slice(hd * HEAD_DIM, (hd + 1) * HEAD_DIM)
            s = sb_ref[hd]
            for c in reversed(range(n_chunks)):
                rows = slice(c * RCHUNK, (c + 1) * RCHUNK)
                sball_ref[first_chunk + c, hd] = s.astype(BF16)
                s = absorb(s, hd, k_ref[0, rows, cols], v_ref[0, rows, cols], wb_ref, db_ref)
            sb_ref[hd] = s

    def forward_chunks(q_ref, k_ref, v_ref, sg_ref, o_ref, n_chunks, first_chunk):
        for hd in range(HEADS):
            cols = slice(hd * HEAD_DIM, (hd + 1) * HEAD_DIM)
            s = sf_ref[hd]
            for c in range(n_chunks):
                rows = slice(c * RCHUNK, (c + 1) * RCHUNK)
                kh = k_ref[0, rows, cols]
                vh = v_ref[0, rows, cols]
                if o_ref is not None:
                    qh = q_ref[0, rows, cols]
                    sc = lax.dot_general(qh, kh, (((1,), (1,)), ((), ())), preferred_element_type=F32)
                    qf = qh.astype(F32)
                    lhs = jnp.concatenate([(sc * mask_ref[hd]).astype(BF16),
                                           (qf * qf_ref[hd]).astype(BF16),
                                           (qf * qb_ref[hd]).astype(BF16)], axis=1)
                    rhs = jnp.concatenate([vh, s.astype(BF16), sball_ref[first_chunk + c, hd]], axis=0)
                    o = _dot(lhs, rhs)
                    o_ref[0, rows, cols] = (_rms(o) * sg_ref[0, rows, cols].astype(F32)).astype(BF16)
                s = absorb(s, hd, kh, vh, wf_ref, df_ref)
            sf_ref[hd] = s

    @pl.when(phase == 0)
    def _():
        @pl.when(j == 0)
        def _():
            sb_ref[...] = jnp.zeros_like(sb_ref)
            backward_chunks(kc_ref, vc_ref, 1, 0)

        backward_chunks(kl_ref, vl_ref, RET_CPT, 1 + (RET_STEPS - 1 - j) * RET_CPT)

    @pl.when(phase == 1)
    def _():
        @pl.when(j == 0)
        def _():
            sf_ref[...] = jnp.zeros_like(sf_ref)
            if ctx_out:
                forward_chunks(qc_ref, kc_ref, vc_ref, sgc_ref, oc_ref, 1, 0)
            else:
                forward_chunks(None, kc_ref, vc_ref, None, None, 1, 0)

        forward_chunks(ql_ref, kl_ref, vl_ref, sgl_ref, ol_ref, RET_CPT, 1 + j * RET_CPT)


def _retention(layer, logits, lat, ctx, *, ctx_out):
    nb = lat[0].shape[0]
    kv_idx = lambda b, p, j: (b, jnp.where(p == 0, RET_STEPS - 1 - j, j), 0)
    q_idx = lambda b, p, j: (b, jnp.where(p == 0, 0, j), 0)
    c_idx = lambda b, p, j: (b, 0, 0)
    lat_blk = (1, RET_TILE, RET_W)
    ctx_blk = (1, CTX_LEN, RET_W)
    in_specs = [
        _layer_spec((2 * HEADS, 1), layer),
        pl.BlockSpec(lat_blk, q_idx), pl.BlockSpec(lat_blk, kv_idx), pl.BlockSpec(lat_blk, kv_idx),
        pl.BlockSpec(lat_blk, q_idx),
        pl.BlockSpec(ctx_blk, c_idx), pl.BlockSpec(ctx_blk, c_idx),
    ]
    if ctx_out:
        qc, kc, vc, sgc = ctx
        args = [logits, *lat, kc, vc, qc, sgc]
        in_specs += [pl.BlockSpec(ctx_blk, c_idx), pl.BlockSpec(ctx_blk, c_idx)]
        out_specs = (pl.BlockSpec(lat_blk, q_idx), pl.BlockSpec(ctx_blk, c_idx))
        out_shape = (jax.ShapeDtypeStruct((nb, SEQ, RET_W), BF16), jax.ShapeDtypeStruct((nb, CTX_LEN, RET_W), BF16))
    else:
        kc, vc = ctx
        args = [logits, *lat, kc, vc]
        out_specs = pl.BlockSpec(lat_blk, q_idx)
        out_shape = jax.ShapeDtypeStruct((nb, SEQ, RET_W), BF16)
    state = pltpu.VMEM((HEADS, HEAD_DIM, HEAD_DIM), F32)
    pos_tab = pltpu.VMEM((HEADS, RCHUNK, HEAD_DIM), F32)
    return pl.pallas_call(
        functools.partial(_ret_kernel, ctx_out=ctx_out),
        grid=(nb, 2, RET_STEPS),
        in_specs=in_specs,
        out_specs=out_specs,
        out_shape=out_shape,
        scratch_shapes=[
            pltpu.VMEM((HEADS, RCHUNK, RCHUNK), F32),
            pos_tab, pos_tab, pos_tab, pos_tab,
            state, state,
            state, state,
            pltpu.VMEM((N_RCHUNKS, HEADS, HEAD_DIM, HEAD_DIM), BF16),
        ],
        compiler_params=_params(3),
        name="retention",
    )(*args)


def _dft_cos_sin(n):
    idx = np.arange(n, dtype=np.float64)
    ang = 2.0 * np.pi * ((idx[:, None] * idx[None, :]) % n) / n
    return np.cos(ang), np.sin(ang)


def _channel_dft(scale):
    c, s = _dft_cos_sin(FNET_GROUP)
    eye = np.eye(FNET_W // FNET_GROUP)
    return np.concatenate([np.kron(eye, c), np.kron(eye, s)], axis=0) * scale


def _fft_constants():
    k1 = np.arange(FFT_N1, dtype=np.float64)[None, :, None]
    n1 = np.arange(FFT_N1, dtype=np.float64)[None, None, :]
    n2 = np.arange(FFT_N2, dtype=np.float64)[:, None, None]
    ang = 2.0 * np.pi * ((k1 * (FFT_N2 * n1 + n2)) % SEQ) / SEQ
    m1 = np.concatenate([np.cos(ang), -np.sin(ang)], axis=1)
    c, s = _dft_cos_sin(FFT_N2)
    m3 = np.block([[c, s], [-s, c]])
    return (_np_split_cat(m1, -1), _np_split_cat(m3, -1),
            _np_split_cat(_channel_dft((SEQ * FNET_GROUP) ** -0.5), 0))


def _ctx_fft_constants():
    c, s = _dft_cos_sin(CTX_LEN)
    return (_np_split_cat(np.concatenate([c, -s], axis=0), -1),
            _np_split_cat(_channel_dft((CTX_LEN * FNET_GROUP) ** -0.5), 0))


def _np_split_cat(a64, axis):
    hi, lo = _np_split(a64)
    return jnp.asarray(np.concatenate([hi, lo], axis=axis))


def _twice(x, axis):
    return jnp.concatenate([x, x], axis=axis)


def _fft1_kernel(x_ref, m_ref, o_ref):
    for jj in range(FFT_STEP):
        x = jnp.concatenate([x_ref[hf, pl.ds(jj, FFT_N1, stride=FFT_STEP), :] for hf in range(2)],
                            axis=1).astype(BF16)
        a = _dot(m_ref[jj], _twice(x, 0))
        for hf in range(2):
            for ri in range(2):
                for kb in range(FFT_N1 // FFT_STEP):
                    r0 = ri * FFT_N1 + kb * FFT_STEP
                    o_ref[hf, ri, kb, jj * FFT_STEP:(jj + 1) * FFT_STEP, :] = (
                        a[r0:r0 + FFT_STEP, hf * HALF:(hf + 1) * HALF])


def _fft3_kernel(a_ref, w_ref, cd_ref, o_ref):
    zs = []
    for jj in range(FFT_STEP):
        rhs = jnp.concatenate(
            [jnp.concatenate([a_ref[hf, ri, pl.ds(jj, FFT_N2, stride=FFT_STEP), :] for ri in range(2)], axis=0)
             for hf in range(2)], axis=1).astype(BF16)
        y = _dot(w_ref[...], _twice(rhs, 0))
        zs.append(jnp.concatenate([y[:FFT_N2], y[FFT_N2:]], axis=1).astype(BF16))
    z = jnp.concatenate(zs, axis=0)
    four = _dot(_twice(z, 1), cd_ref[...])
    for jj in range(FFT_STEP):
        for kt in range(FFT_N2 // MIX_K2):
            for hf in range(2):
                r0 = jj * FFT_N2 + kt * MIX_K2
                o_ref[hf, kt, jj * MIX_K2:(jj + 1) * MIX_K2, :] = four[r0:r0 + MIX_K2, hf * HALF:(hf + 1) * HALF]


def _fourier_latent(f_blk):
    nb = f_blk.shape[0]
    m1, m3, cd = _fft_constants()
    n_kb = FFT_N1 // FFT_STEP
    a = pl.pallas_call(
        _fft1_kernel,
        grid=(nb, FFT_N2 // FFT_STEP),
        in_specs=[
            pl.BlockSpec((None, 2, None, FFT_N1 * FFT_STEP, HALF), lambda b, j: (b, 0, j, 0, 0)),
            pl.BlockSpec((FFT_STEP, 2 * FFT_N1, 2 * FFT_N1), lambda b, j: (j, 0, 0)),
        ],
        out_specs=pl.BlockSpec((None, 2, 2, n_kb, FFT_STEP * FFT_STEP, HALF), lambda b, j: (b, 0, 0, 0, j, 0)),
        out_shape=jax.ShapeDtypeStruct((nb, 2, 2, n_kb, FFT_N2 * FFT_STEP, HALF), F32),
        compiler_params=_params(2),
        name="fourier_stage1",
    )(f_blk, m1)
    return pl.pallas_call(
        _fft3_kernel,
        grid=(nb, n_kb),
        in_specs=[
            pl.BlockSpec((None, 2, 2, None, FFT_N2 * FFT_STEP, HALF), lambda b, j: (b, 0, 0, j, 0, 0)),
            _const_spec((2 * FFT_N2, 4 * FFT_N2)), _const_spec((4 * FNET_W, FNET_W)),
        ],
        out_specs=pl.BlockSpec((None, 2, FFT_N2 // MIX_K2, FFT_STEP * MIX_K2, HALF), lambda b, j: (b, 0, 0, j, 0)),
        out_shape=jax.ShapeDtypeStruct((nb, 2, FFT_N2 // MIX_K2, FFT_N1 * MIX_K2, HALF), F32),
        compiler_params=_params(2),
        name="fourier_stage2",
    )(a, m3, cd)


def _fft_ctx_kernel(x_ref, w_ref, cd_ref, o_ref):
    x = jnp.concatenate([x_ref[hf] for hf in range(2)], axis=1).astype(BF16)
    z = _dot(w_ref[...], _twice(x, 0))
    z = jnp.concatenate([z[:CTX_LEN], z[CTX_LEN:]], axis=1).astype(BF16)
    four = _dot(_twice(z, 1), cd_ref[...])
    for hf in range(2):
        o_ref[hf] = four[:, hf * HALF:(hf + 1) * HALF]


def _fourier_ctx(f_ctx):
    nb = f_ctx.shape[0]
    w, cd = _ctx_fft_constants()
    return pl.pallas_call(
        _fft_ctx_kernel,
        grid=(nb,),
        in_specs=[
            pl.BlockSpec((None, 2, CTX_LEN, HALF), lambda b: (b, 0, 0, 0)),
            _const_spec((2 * CTX_LEN, 2 * CTX_LEN)), _const_spec((4 * FNET_W, FNET_W)),
        ],
        out_specs=pl.BlockSpec((None, 2, CTX_LEN, HALF), lambda b: (b, 0, 0, 0)),
        out_shape=jax.ShapeDtypeStruct((nb, 2, CTX_LEN, HALF), F32),
        compiler_params=_params(1),
        name="fourier_ctx",
    )(f_ctx, w, cd)


def _mix_mlp_kernel(x_ref, ret_ref, four_ref, u_ref, vn_ref, gate_ref, mod_ref, ws_ref, bs_ref,
                    wa_ref, wb_ref, wc_ref, wo_ref, gpm_ref, gprm_ref, gpom_ref, wup_ref, wdn_ref, o_ref, *, tm, ctx):
    group = lax.broadcasted_iota(jnp.int32, (GMLP_CHUNK, GMLP_W), 1) // GMLP_GROUP
    parts = []
    for cc in range(tm // GMLP_CHUNK):
        rows = slice(cc * GMLP_CHUNK, (cc + 1) * GMLP_CHUNK)
        vn = vn_ref[0, rows, :]
        s = bs_ref[...]
        for g in range(GMLP_W // GMLP_GROUP):
            s = s + jnp.where(group == g, _dot(ws_ref[g], vn), 0.0)
        parts.append((u_ref[0, rows, :].astype(F32) * s).astype(BF16))
    sgu = jnp.concatenate(parts, axis=0)

    if ctx:
        four = jnp.concatenate([four_ref[0, hf] for hf in range(2)], axis=1).astype(BF16)
    else:
        nk2 = tm // FFT_N1
        four = jnp.concatenate(
            [jnp.concatenate([four_ref[0, hf, pl.ds(k2, FFT_N1, stride=nk2), :] for k2 in range(nk2)], axis=0)
             for hf in range(2)], axis=1).astype(BF16)
    ga = gate_ref[0, :, 0:D_MODEL].astype(F32)
    gb = gate_ref[0, :, D_MODEL:2 * D_MODEL].astype(F32)
    gc = gate_ref[0, :, 2 * D_MODEL:3 * D_MODEL].astype(F32)
    merged = (ga * _dot(ret_ref[0], wa_ref[...]) + gb * _dot(four, wb_ref[...])
              + gc * _dot(sgu, wc_ref[...])).astype(BF16)

    gate1 = mod_ref[0, :, 2 * D_MODEL:3 * D_MODEL]
    shift2 = mod_ref[0, :, 3 * D_MODEL:4 * D_MODEL]
    scale2 = mod_ref[0, :, 4 * D_MODEL:5 * D_MODEL]
    gate2 = mod_ref[0, :, 5 * D_MODEL:6 * D_MODEL]

    halves = [slice(r * (tm // 2), (r + 1) * (tm // 2)) for r in range(2)]
    x1, h2 = [], []
    for rows in halves:
        y = _dot(merged[rows], wo_ref[...])
        x1.append(x_ref[0, rows, :] + gate1 * (_rms(y) * gpm_ref[...]))
        h2.append(((_rms(x1[-1]) * gprm_ref[...]) * (1.0 + scale2) + shift2).astype(BF16))
    h2 = jnp.concatenate(h2, axis=0)

    n_ff = D_FF // FF_CHUNK
    m = None
    for c in range(n_ff - 1):
        cols = slice(c * FF_CHUNK, (c + 1) * FF_CHUNK)
        a = jnp.maximum(_dot(h2, wup_ref[:, cols]), 0.0)
        part = _dot((a * a).astype(BF16), wdn_ref[cols, :])
        m = part if m is None else m + part
    cols = slice((n_ff - 1) * FF_CHUNK, n_ff * FF_CHUNK)
    a = jnp.maximum(_dot(h2, wup_ref[:, cols]), 0.0)
    a = (a * a).astype(BF16)
    for r, rows in enumerate(halves):
        mr = m[rows] + _dot(a[rows], wdn_ref[cols, :])
        o_ref[0, rows, :] = x1[r] + gate2 * (_rms(mr) * gpom_ref[...])


def _mix_mlp(xs, ret, four, ug, vn, gates, layer, mod, ws, bs_tab, wa, wb, wc, wo,
             g_post_mix, g_pre_mlp, g_post_mlp, w_up, w_down, *, tm, ctx):
    nb, rows, _ = xs.shape
    tok_spec = lambda w: pl.BlockSpec((1, tm, w), lambda b, i: (b, i, 0))
    lspec = lambda *shape: _layer_spec(shape, layer)
    if ctx:
        four_spec = pl.BlockSpec((1, 2, tm, HALF), lambda b, i: (b, 0, i, 0))
    else:
        four_spec = pl.BlockSpec((1, 2, None, tm, HALF), lambda b, i: (b, 0, i, 0, 0))
    return pl.pallas_call(
        functools.partial(_mix_mlp_kernel, tm=tm, ctx=ctx),
        grid=(nb, rows // tm),
        in_specs=[
            tok_spec(D_MODEL), tok_spec(RET_W),
            four_spec,
            tok_spec(GMLP_W), tok_spec(GMLP_W), tok_spec(3 * D_MODEL),
            _mod_spec(layer, ctx),
            lspec(GMLP_W // GMLP_GROUP, GMLP_CHUNK, GMLP_CHUNK),
            lspec(GMLP_CHUNK, GMLP_W),
            lspec(RET_W, D_MODEL), lspec(FNET_W, D_MODEL), lspec(GMLP_W, D_MODEL), lspec(D_MODEL, D_MODEL),
            lspec(1, D_MODEL), lspec(1, D_MODEL), lspec(1, D_MODEL),
            lspec(D_MODEL, D_FF), lspec(D_FF, D_MODEL),
        ],
        out_specs=tok_spec(D_MODEL),
        out_shape=jax.ShapeDtypeStruct((nb, rows, D_MODEL), F32),
        compiler_params=_params(2),
        name="mix_mlp_ctx" if ctx else "mix_mlp",
    )(xs, ret, four, ug, vn, gates, mod, ws, bs_tab, wa, wb, wc, wo,
      g_post_mix, g_pre_mlp, g_post_mlp, w_up, w_down)


def _rope_tables():
    rows = SEQ // GRID_W
    freqs = ROPE_BASE ** (-jnp.arange(ROPE_FREQS, dtype=F32) / ROPE_FREQS)
    ang_r = jnp.arange(rows, dtype=F32)[:, None] * freqs
    ang_c = jnp.arange(GRID_W, dtype=F32)[:, None] * freqs

    def table(fn):
        t = jnp.concatenate([jnp.repeat(fn(ang_r), GRID_W, axis=0), jnp.tile(fn(ang_c), (rows, 1))], axis=-1)
        return t

    cos = table(jnp.cos)
    sin = table(jnp.sin)
    return jnp.concatenate([cos, cos], axis=-1), jnp.concatenate([-sin, sin], axis=-1)


def kernel(x, c, ctx, c_ctx, w_mod, b_mod, g_pre_mix, g_post_mix, g_pre_mlp, g_post_mlp, w_in,
           ret_decay_logit, sgu_w_s, sgu_b_s, sgu_norm, w_branch_a, w_branch_b, w_branch_c, w_out,
           w_up, w_down):
    nb = x.shape[0]
    depth = w_mod.shape[0]
    assert x.shape == (nb, SEQ, D_MODEL) and ctx.shape == (nb, CTX_LEN, D_MODEL) and nb == 2

    cond_rows = jnp.concatenate([c, c_ctx[None, :], jnp.zeros((8 - nb - 1, D_MODEL), F32)], axis=0)
    mod = _modulation(cond_rows, w_mod, b_mod).reshape(depth, 8, 1, 6 * D_MODEL)
    rope_tabs = _rope_tables()
    pavg = jnp.asarray(np.kron(np.eye(GMLP_W // GMLP_GROUP), np.full((GMLP_GROUP, GMLP_GROUP), 1.0 / GMLP_GROUP)), BF16)

    vec = lambda p: p[:, None, :]
    in_w = (vec(g_pre_mix), w_in.astype(BF16), pavg, vec(sgu_norm))
    logits = ret_decay_logit.reshape(depth, 2 * HEADS, 1)
    mix_w = (sgu_w_s.astype(BF16), jnp.repeat(jnp.swapaxes(sgu_b_s, 1, 2), GMLP_GROUP, axis=2),
             w_branch_a.astype(BF16), w_branch_b.astype(BF16), w_branch_c.astype(BF16), w_out.astype(BF16),
             vec(g_post_mix), vec(g_pre_mlp), vec(g_post_mlp), w_up.astype(BF16), w_down.astype(BF16))

    for l in range(depth):
        last = l == depth - 1
        q, k, v, sg, f_lat, ug, vn, gates = _in_proj(x, l, mod, *in_w, rope_tabs, tm=TM_IN, ctx=False)
        if last:
            kc, vc = _in_proj(ctx, l, mod, *in_w, None, tm=CTX_LEN, ctx=True, kv_only=True)
            ret = _retention(l, logits, (q, k, v, sg), (kc, vc), ctx_out=False)
        else:
            qc, kc, vc, sgc, f_ctx, ugc, vnc, gatesc = _in_proj(ctx, l, mod, *in_w, None, tm=CTX_LEN, ctx=True)
            ret, retc = _retention(l, logits, (q, k, v, sg), (qc, kc, vc, sgc), ctx_out=True)
        x = _mix_mlp(x, ret, _fourier_latent(f_lat), ug, vn, gates, l, mod, *mix_w, tm=TM_LAT, ctx=False)
        if not last:
            ctx = _mix_mlp(ctx, retc, _fourier_ctx(f_ctx), ugc, vnc, gatesc, l, mod, *mix_w, tm=CTX_LEN, ctx=True)
    return x
```

```python
import functools

import numpy as np
import jax
import jax.numpy as jnp
from jax import lax
from jax.experimental import pallas as pl
from jax.experimental.pallas import tpu as pltpu

F32 = jnp.float32
BF16 = jnp.bfloat16

D_MODEL = 1024
SEQ = 8192
CTX_LEN = 256
GRID_W = 64
RET_W = 512
HEADS = 4
HEAD_DIM = 128
ROPE_BASE = 10000.0
ROPE_FREQS = HEAD_DIM // 4
FNET_W = 256
FNET_GROUP = 64
GMLP_W = 256
GMLP_GROUP = 64
GMLP_CHUNK = 128
D_FF = 4 * D_MODEL
EPS = 1e-6
IN_W = 4 * RET_W + FNET_W + 2 * GMLP_W + 3 * D_MODEL
COL_F = 4 * RET_W
COL_U = COL_F + FNET_W
COL_VS = COL_U + GMLP_W
COL_GATE = COL_VS + GMLP_W

TM_LAT = 512
TM_IN = 1024
RCHUNK = 256
RET_TILE = 1024
RET_STEPS = SEQ // RET_TILE
RET_CPT = RET_TILE // RCHUNK
N_RCHUNKS = 1 + SEQ // RCHUNK
FF_CHUNK = 1024
FFT_N1 = 64
FFT_N2 = 128
HALF = 128
FFT_STEP = 16
MIX_K2 = TM_LAT // FFT_N1

_VMEM_LIMIT = 56 * 1024 * 1024


def _dot(a, b):
    return jnp.dot(a, b, preferred_element_type=F32)


def _split(x):
    hi = x.astype(BF16)
    lo = (x - hi.astype(F32)).astype(BF16)
    return hi, lo


def _np_split(a64):
    hi = np.asarray(a64, np.float32).astype(BF16)
    lo = (np.asarray(a64, np.float32) - hi.astype(np.float32)).astype(BF16)
    return hi, lo


def _rms(x):
    return x * lax.rsqrt(jnp.mean(x * x, axis=-1, keepdims=True) + EPS)


def _gelu(x):
    return x * (0.5 * (1.0 + jnp.tanh(0.7978845608028654 * (x + 0.044715 * (x * x * x)))))


def _sigmoid(x):
    return 1.0 / (1.0 + jnp.exp(-x))


def _const_spec(shape, nargs=None):
    zeros = (0,) * len(shape)
    return pl.BlockSpec(shape, lambda *_: zeros, pipeline_mode=pl.Buffered(1))


def _layer_spec(shape, layer):
    idx = (layer,) + (0,) * len(shape)
    return pl.BlockSpec((None,) + tuple(shape), lambda *_: idx, pipeline_mode=pl.Buffered(1))


def _params(n_axes):
    return pltpu.CompilerParams(dimension_semantics=("arbitrary",) * n_axes, vmem_limit_bytes=_VMEM_LIMIT)


def _mod_kernel(a_ref, w_ref, b_ref, o_ref):
    a = a_ref[...]
    a = a * _sigmoid(a)
    ah, al = _split(a)
    w = w_ref[0].astype(BF16)
    o_ref[0] = _dot(ah, w) + _dot(al, w) + b_ref[0]


def _modulation(cond_rows, w_mod, b_mod):
    depth = w_mod.shape[0]
    tn = 1536
    return pl.pallas_call(
        _mod_kernel,
        grid=(depth, (6 * D_MODEL) // tn),
        in_specs=[
            pl.BlockSpec((8, D_MODEL), lambda l, j: (0, 0)),
            pl.BlockSpec((1, D_MODEL, tn), lambda l, j: (l, 0, j)),
            pl.BlockSpec((1, 1, tn), lambda l, j: (l, 0, j)),
        ],
        out_specs=pl.BlockSpec((1, 8, tn), lambda l, j: (l, 0, j)),
        out_shape=jax.ShapeDtypeStruct((depth, 8, 6 * D_MODEL), F32),
        compiler_params=_params(2),
        name="modulation",
    )(cond_rows, w_mod, b_mod.reshape(depth, 1, 6 * D_MODEL))


def _cast_plan(cast, grid):
    steps = int(np.prod(grid))
    in_specs, args, out_specs, out_shapes = [], [], [], []

    def step(*idx):
        t = idx[0]
        for n, i in zip(grid[1:], idx[1:]):
            t = t * n + i
        return t

    for stack, layer in cast:
        _, rows, cols = stack.shape
        n_chunks = min(steps, rows // 16)
        assert rows % n_chunks == 0 and steps % n_chunks == 0
        per = steps // n_chunks
        in_specs.append(pl.BlockSpec((None, rows // n_chunks, cols),
                                     lambda *idx, layer=layer, per=per: (layer, step(*idx) // per, 0)))
        out_specs.append(pl.BlockSpec((rows // n_chunks, cols), lambda *idx, per=per: (step(*idx) // per, 0)))
        out_shapes.append(jax.ShapeDtypeStruct((rows, cols), BF16))
        args.append(stack)
    return in_specs, args, out_specs, out_shapes


def _cast_chunks(src_refs, dst_refs):
    for src, dst in zip(src_refs, dst_refs, strict=True):
        dst[...] = src[...].astype(BF16)


def _in_proj_kernel(*refs, rope, kv_only, n_cast):
    n_in = 8 if rope else 6
    ins, cast_src = refs[:n_in], refs[n_in:n_in + n_cast]
    outs, cast_dst = refs[n_in + n_cast:len(refs) - n_cast], refs[len(refs) - n_cast:]
    _cast_chunks(cast_src, cast_dst)
    if rope:
        x_ref, mod_ref, gpre_ref, w_ref, pavg_ref, gn_ref, cos_ref, sin_ref = ins
    else:
        x_ref, mod_ref, gpre_ref, w_ref, pavg_ref, gn_ref = ins
    x = x_ref[0]
    shift = mod_ref[0, :, 0:D_MODEL]
    scale = mod_ref[0, :, D_MODEL:2 * D_MODEL]
    h = _rms(x) * gpre_ref[...]
    hb = (h * (1.0 + scale) + shift).astype(BF16)

    def proj(lo, hi):
        return _dot(hb, w_ref[:, lo:hi])

    def rope_store(z, dst):
        if not rope:
            dst[0] = z.astype(BF16)
            return
        cos = cos_ref[...]
        sin = sin_ref[...]
        for hd in range(HEADS):
            a = z[:, hd * HEAD_DIM:(hd + 1) * HEAD_DIM]
            r = a * cos + pltpu.roll(a, HEAD_DIM // 2, 1) * sin
            dst[0, :, hd * HEAD_DIM:(hd + 1) * HEAD_DIM] = r.astype(BF16)

    if kv_only:
        k_ref, v_ref = outs
    else:
        q_ref, k_ref, v_ref, sg_ref, f_ref, u_ref, vn_ref, gate_ref = outs
        rope_store(proj(0, RET_W), q_ref)
    rope_store(proj(RET_W, 2 * RET_W) * (HEAD_DIM ** -0.5), k_ref)
    v_ref[0] = proj(2 * RET_W, 3 * RET_W).astype(BF16)
    if kv_only:
        return
    g = proj(3 * RET_W, 4 * RET_W)
    sg_ref[0] = (g * _sigmoid(g)).astype(BF16)

    f = proj(COL_F, COL_U)
    for hf in range(2):
        fh = f[:, hf * HALF:(hf + 1) * HALF]
        if rope:
            for n1 in range(f.shape[0] // FFT_N2):
                for nj in range(FFT_N2 // FFT_STEP):
                    f_ref[0, hf, nj, n1 * FFT_STEP:(n1 + 1) * FFT_STEP, :] = (
                        fh[n1 * FFT_N2 + nj * FFT_STEP:n1 * FFT_N2 + (nj + 1) * FFT_STEP])
        else:
            f_ref[0, hf] = fh

    u_ref[0] = _gelu(proj(COL_U, COL_VS)).astype(BF16)
    vg = _gelu(proj(COL_VS, COL_GATE))
    sh, sl = _split(vg * vg)
    pavg = pavg_ref[...]
    ms = _dot(sh, pavg) + _dot(sl, pavg)
    vn_ref[0] = (vg * lax.rsqrt(ms + EPS) * gn_ref[...]).astype(BF16)

    for c in range(3):
        z = proj(COL_GATE + c * D_MODEL, COL_GATE + (c + 1) * D_MODEL)
        gate_ref[0, :, c * D_MODEL:(c + 1) * D_MODEL] = _sigmoid(z).astype(BF16)


def _mod_spec(layer, ctx):
    idx = (lambda b, i: (layer, 2, 0, 0)) if ctx else (lambda b, i: (layer, b, 0, 0))
    return pl.BlockSpec((None, 1, 1, 6 * D_MODEL), idx)


def _in_proj(xs, layer, mod, g_pre, w_in, pavg, g_norm, rope_tabs, *, tm, ctx, kv_only=False, cast=()):
    nb, rows, _ = xs.shape
    rope = rope_tabs is not None
    tok_spec = lambda w: pl.BlockSpec((1, tm, w), lambda b, i: (b, i, 0))
    in_specs = [
        tok_spec(D_MODEL),
        _mod_spec(layer, ctx),
        _layer_spec((1, D_MODEL), layer),
        _const_spec((D_MODEL, IN_W)),
        _const_spec((GMLP_W, GMLP_W)),
        _layer_spec((1, GMLP_W), layer),
    ]
    args = [xs, mod, g_pre, w_in, pavg, g_norm]
    if rope:
        in_specs += [pl.BlockSpec((tm, HEAD_DIM), lambda b, i: (i, 0))] * 2
        args += list(rope_tabs)
    c_in_specs, c_args, c_out_specs, c_out_shapes = _cast_plan(cast, (nb, rows // tm))
    in_specs += c_in_specs
    args += c_args
    bf = lambda w: jax.ShapeDtypeStruct((nb, rows, w), BF16)
    if kv_only:
        out_shapes = (bf(RET_W), bf(RET_W))
        out_specs = (tok_spec(RET_W), tok_spec(RET_W))
    else:
        if rope:
            n_nj = FFT_N2 // FFT_STEP
            f_shape = jax.ShapeDtypeStruct((nb, 2, n_nj, FFT_N1 * FFT_STEP, HALF), F32)
            f_spec = pl.BlockSpec((1, 2, n_nj, (tm // FFT_N2) * FFT_STEP, HALF), lambda b, i: (b, 0, 0, i, 0))
        else:
            f_shape = jax.ShapeDtypeStruct((nb, 2, rows, HALF), F32)
            f_spec = pl.BlockSpec((1, 2, tm, HALF), lambda b, i: (b, 0, i, 0))
        out_shapes = (
            bf(RET_W), bf(RET_W), bf(RET_W), bf(RET_W),
            f_shape,
            bf(GMLP_W), bf(GMLP_W),
            bf(3 * D_MODEL),
        )
        out_specs = (
            tok_spec(RET_W), tok_spec(RET_W), tok_spec(RET_W), tok_spec(RET_W),
            f_spec,
            tok_spec(GMLP_W), tok_spec(GMLP_W), tok_spec(3 * D_MODEL),
        )
    return pl.pallas_call(
        functools.partial(_in_proj_kernel, rope=rope, kv_only=kv_only, n_cast=len(cast)),
        grid=(nb, rows // tm),
        in_specs=in_specs,
        out_specs=tuple(out_specs) + tuple(c_out_specs),
        out_shape=tuple(out_shapes) + tuple(c_out_shapes),
        compiler_params=_params(2),
        name="in_proj_ctx" if ctx else "in_proj",
    )(*args)


def _ret_kernel(*refs, ctx_out, n_cast):
    n_in, n_out = (9, 2) if ctx_out else (7, 1)
    ins, refs = refs[:n_in], refs[n_in:]
    cast_src, refs = refs[:n_cast], refs[n_cast:]
    outs, refs = refs[:n_out], refs[n_out:]
    cast_dst, scratch = refs[:n_cast], refs[n_cast:]
    _cast_chunks(cast_src, cast_dst)
    (logit_ref, ql_ref, kl_ref, vl_ref, sgl_ref, kc_ref, vc_ref) = ins[:7]
    if ctx_out:
        qc_ref, sgc_ref = ins[7:]
        ol_ref, oc_ref = outs
    else:
        (ol_ref,) = outs
    mask_ref, wf_ref, wb_ref, qf_ref, qb_ref, df_ref, db_ref, sf_ref, sb_ref, sball_ref = scratch
    b = pl.program_id(0)
    phase = pl.program_id(1)
    j = pl.program_id(2)

    @pl.when((b == 0) & (phase == 0) & (j == 0))
    def _():
        x = logit_ref[...]
        lg = -(jnp.maximum(-x, 0.0) + jnp.log(1.0 + jnp.exp(-jnp.abs(x))))
        row = lax.broadcasted_iota(jnp.int32, (RCHUNK, RCHUNK), 0).astype(F32)
        col = lax.broadcasted_iota(jnp.int32, (RCHUNK, RCHUNK), 1).astype(F32)
        diff = row - col
        pos = row[:, :HEAD_DIM]
        for hd in range(HEADS):
            lf = lg[hd:hd + 1, :]
            lb = lg[HEADS + hd:HEADS + hd + 1, :]
            mask_ref[hd] = jnp.where(diff >= 0.0, jnp.exp(lf * jnp.maximum(diff, 0.0)),
                                     jnp.exp(lb * jnp.maximum(-diff, 0.0)))
            wf_ref[hd] = jnp.exp(lf * (RCHUNK - 1.0 - pos))
            wb_ref[hd] = jnp.exp(lb * pos)
            qf_ref[hd] = jnp.exp(lf * (pos + 1.0))
            qb_ref[hd] = jnp.exp(lb * (RCHUNK - pos))
            df_ref[hd] = jnp.exp(jnp.broadcast_to(lf, (HEAD_DIM, HEAD_DIM)) * float(RCHUNK))
            db_ref[hd] = jnp.exp(jnp.broadcast_to(lb, (HEAD_DIM, HEAD_DIM)) * float(RCHUNK))

    def absorb(s, hd, kh, vh, w_ref, d_ref):
        kw = (kh.astype(F32) * w_ref[hd]).astype(BF16)
        u = lax.dot_general(kw, vh, (((0,), (0,)), ((), ())), preferred_element_type=F32)
        return d_ref[hd] * s + u

    def backward_chunks(k_ref, v_ref, n_chunks, first_chunk):
        for hd in range(HEADS):
            cols = slice(hd * HEAD_DIM, (hd + 1) * HEAD_DIM)
            s = sb_ref[hd]
            for c in reversed(range(n_chunks)):
                rows = slice(c * RCHUNK, (c + 1) * RCHUNK)
                sball_ref[first_chunk + c, hd] = s.astype(BF16)
                s = absorb(s, hd, k_ref[0, rows, cols], v_ref[0, rows, cols], wb_ref, db_ref)
            sb_ref[hd] = s

    def forward_chunks(q_ref, k_ref, v_ref, sg_ref, o_ref, n_chunks, first_chunk):
        for hd in range(HEADS):
            cols = slice(hd * HEAD_DIM, (hd + 1) * HEAD_DIM)
            s = sf_ref[hd]
            for c in range(n_chunks):
                rows = slice(c * RCHUNK, (c + 1) * RCHUNK)
                kh = k_ref[0, rows, cols]
                vh = v_ref[0, rows, cols]
                if o_ref is not None:
                    qh = q_ref[0, rows, cols]
                    sc = lax.dot_general(qh, kh, (((1,), (1,)), ((), ())), preferred_element_type=F32)
                    qf = qh.astype(F32)
                    lhs = jnp.concatenate([(sc * mask_ref[hd]).astype(BF16),
                                           (qf * qf_ref[hd]).astype(BF16),
                                           (qf * qb_ref[hd]).astype(BF16)], axis=1)
                    rhs = jnp.concatenate([vh, s.astype(BF16), sball_ref[first_chunk + c, hd]], axis=0)
                    o = _dot(lhs, rhs)
                    o_ref[0, rows, cols] = (_rms(o) * sg_ref[0, rows, cols].astype(F32)).astype(BF16)
                s = absorb(s, hd, kh, vh, wf_ref, df_ref)
            sf_ref[hd] = s

    @pl.when(phase == 0)
    def _():
        @pl.when(j == 0)
        def _():
            sb_ref[...] = jnp.zeros_like(sb_ref)
            backward_chunks(kc_ref, vc_ref, 1, 0)

        backward_chunks(kl_ref, vl_ref, RET_CPT, 1 + (RET_STEPS - 1 - j) * RET_CPT)

    @pl.when(phase == 1)
    def _():
        @pl.when(j == 0)
        def _():
            sf_ref[...] = jnp.zeros_like(sf_ref)
            if ctx_out:
                forward_chunks(qc_ref, kc_ref, vc_ref, sgc_ref, oc_ref, 1, 0)
            else:
                forward_chunks(None, kc_ref, vc_ref, None, None, 1, 0)

        forward_chunks(ql_ref, kl_ref, vl_ref, sgl_ref, ol_ref, RET_CPT, 1 + j * RET_CPT)


def _retention(layer, logits, lat, ctx, *, ctx_out, cast=()):
    nb = lat[0].shape[0]
    kv_idx = lambda b, p, j: (b, jnp.where(p == 0, RET_STEPS - 1 - j, j), 0)
    q_idx = lambda b, p, j: (b, jnp.where(p == 0, 0, j), 0)
    c_idx = lambda b, p, j: (b, 0, 0)
    lat_blk = (1, RET_TILE, RET_W)
    ctx_blk = (1, CTX_LEN, RET_W)
    in_specs = [
        _layer_spec((2 * HEADS, 1), layer),
        pl.BlockSpec(lat_blk, q_idx), pl.BlockSpec(lat_blk, kv_idx), pl.BlockSpec(lat_blk, kv_idx),
        pl.BlockSpec(lat_blk, q_idx),
        pl.BlockSpec(ctx_blk, c_idx), pl.BlockSpec(ctx_blk, c_idx),
    ]
    if ctx_out:
        qc, kc, vc, sgc = ctx
        args = [logits, *lat, kc, vc, qc, sgc]
        in_specs += [pl.BlockSpec(ctx_blk, c_idx), pl.BlockSpec(ctx_blk, c_idx)]
        out_specs = (pl.BlockSpec(lat_blk, q_idx), pl.BlockSpec(ctx_blk, c_idx))
        out_shape = (jax.ShapeDtypeStruct((nb, SEQ, RET_W), BF16), jax.ShapeDtypeStruct((nb, CTX_LEN, RET_W), BF16))
    else:
        kc, vc = ctx
        args = [logits, *lat, kc, vc]
        out_specs = (pl.BlockSpec(lat_blk, q_idx),)
        out_shape = (jax.ShapeDtypeStruct((nb, SEQ, RET_W), BF16),)
    grid = (nb, 2, RET_STEPS)
    c_in_specs, c_args, c_out_specs, c_out_shapes = _cast_plan(cast, grid)
    state = pltpu.VMEM((HEADS, HEAD_DIM, HEAD_DIM), F32)
    pos_tab = pltpu.VMEM((HEADS, RCHUNK, HEAD_DIM), F32)
    return pl.pallas_call(
        functools.partial(_ret_kernel, ctx_out=ctx_out, n_cast=len(cast)),
        grid=grid,
        in_specs=in_specs + c_in_specs,
        out_specs=out_specs + tuple(c_out_specs),
        out_shape=out_shape + tuple(c_out_shapes),
        scratch_shapes=[
            pltpu.VMEM((HEADS, RCHUNK, RCHUNK), F32),
            pos_tab, pos_tab, pos_tab, pos_tab,
            state, state,
            state, state,
            pltpu.VMEM((N_RCHUNKS, HEADS, HEAD_DIM, HEAD_DIM), BF16),
        ],
        compiler_params=_params(3),
        name="retention",
    )(*args, *c_args)


def _dft_cos_sin(n):
    idx = np.arange(n, dtype=np.float64)
    ang = 2.0 * np.pi * ((idx[:, None] * idx[None, :]) % n) / n
    return np.cos(ang), np.sin(ang)


def _channel_dft(scale):
    c, s = _dft_cos_sin(FNET_GROUP)
    eye = np.eye(FNET_W // FNET_GROUP)
    return np.concatenate([np.kron(eye, c), np.kron(eye, s)], axis=0) * scale


def _fft_constants():
    k1 = np.arange(FFT_N1, dtype=np.float64)[None, :, None]
    n1 = np.arange(FFT_N1, dtype=np.float64)[None, None, :]
    n2 = np.arange(FFT_N2, dtype=np.float64)[:, None, None]
    ang = 2.0 * np.pi * ((k1 * (FFT_N2 * n1 + n2)) % SEQ) / SEQ
    m1 = np.concatenate([np.cos(ang), -np.sin(ang)], axis=1)
    c, s = _dft_cos_sin(FFT_N2)
    m3 = np.block([[c, s], [-s, c]])
    return (_np_split_cat(m1, -1), _np_split_cat(m3, -1),
            _np_split_cat(_channel_dft((SEQ * FNET_GROUP) ** -0.5), 0))


def _ctx_fft_constants():
    c, s = _dft_cos_sin(CTX_LEN)
    return (_np_split_cat(np.concatenate([c, -s], axis=0), -1),
            _np_split_cat(_channel_dft((CTX_LEN * FNET_GROUP) ** -0.5), 0))


def _np_split_cat(a64, axis):
    hi, lo = _np_split(a64)
    return jnp.asarray(np.concatenate([hi, lo], axis=axis))


def _twice(x, axis):
    return jnp.concatenate([x, x], axis=axis)


def _fft1_kernel(x_ref, m_ref, o_ref):
    for jj in range(FFT_STEP):
        x = jnp.concatenate([x_ref[hf, pl.ds(jj, FFT_N1, stride=FFT_STEP), :] for hf in range(2)],
                            axis=1).astype(BF16)
        a = _dot(m_ref[jj], _twice(x, 0))
        for hf in range(2):
            for ri in range(2):
                for kb in range(FFT_N1 // FFT_STEP):
                    r0 = ri * FFT_N1 + kb * FFT_STEP
                    o_ref[hf, ri, kb, jj * FFT_STEP:(jj + 1) * FFT_STEP, :] = (
                        a[r0:r0 + FFT_STEP, hf * HALF:(hf + 1) * HALF])


def _fft3_kernel(a_ref, w_ref, cd_ref, o_ref):
    zs = []
    for jj in range(FFT_STEP):
        rhs = jnp.concatenate(
            [jnp.concatenate([a_ref[hf, ri, pl.ds(jj, FFT_N2, stride=FFT_STEP), :] for ri in range(2)], axis=0)
             for hf in range(2)], axis=1).astype(BF16)
        y = _dot(w_ref[...], _twice(rhs, 0))
        zs.append(jnp.concatenate([y[:FFT_N2], y[FFT_N2:]], axis=1).astype(BF16))
    z = jnp.concatenate(zs, axis=0)
    four = _dot(_twice(z, 1), cd_ref[...])
    for jj in range(FFT_STEP):
        for kt in range(FFT_N2 // MIX_K2):
            for hf in range(2):
                r0 = jj * FFT_N2 + kt * MIX_K2
                o_ref[hf, kt, jj * MIX_K2:(jj + 1) * MIX_K2, :] = four[r0:r0 + MIX_K2, hf * HALF:(hf + 1) * HALF]


def _fourier_latent(f_blk):
    nb = f_blk.shape[0]
    m1, m3, cd = _fft_constants()
    n_kb = FFT_N1 // FFT_STEP
    a = pl.pallas_call(
        _fft1_kernel,
        grid=(nb, FFT_N2 // FFT_STEP),
        in_specs=[
            pl.BlockSpec((None, 2, None, FFT_N1 * FFT_STEP, HALF), lambda b, j: (b, 0, j, 0, 0)),
            pl.BlockSpec((FFT_STEP, 2 * FFT_N1, 2 * FFT_N1), lambda b, j: (j, 0, 0)),
        ],
        out_specs=pl.BlockSpec((None, 2, 2, n_kb, FFT_STEP * FFT_STEP, HALF), lambda b, j: (b, 0, 0, 0, j, 0)),
        out_shape=jax.ShapeDtypeStruct((nb, 2, 2, n_kb, FFT_N2 * FFT_STEP, HALF), F32),
        compiler_params=_params(2),
        name="fourier_stage1",
    )(f_blk, m1)
    return pl.pallas_call(
        _fft3_kernel,
        grid=(nb, n_kb),
        in_specs=[
            pl.BlockSpec((None, 2, 2, None, FFT_N2 * FFT_STEP, HALF), lambda b, j: (b, 0, 0, j, 0, 0)),
            _const_spec((2 * FFT_N2, 4 * FFT_N2)), _const_spec((4 * FNET_W, FNET_W)),
        ],
        out_specs=pl.BlockSpec((None, 2, FFT_N2 // MIX_K2, FFT_STEP * MIX_K2, HALF), lambda b, j: (b, 0, 0, j, 0)),
        out_shape=jax.ShapeDtypeStruct((nb, 2, FFT_N2 // MIX_K2, FFT_N1 * MIX_K2, HALF), F32),
        compiler_params=_params(2),
        name="fourier_stage2",
    )(a, m3, cd)


def _fft_ctx_kernel(x_ref, w_ref, cd_ref, o_ref):
    x = jnp.concatenate([x_ref[hf] for hf in range(2)], axis=1).astype(BF16)
    z = _dot(w_ref[...], _twice(x, 0))
    z = jnp.concatenate([z[:CTX_LEN], z[CTX_LEN:]], axis=1).astype(BF16)
    four = _dot(_twice(z, 1), cd_ref[...])
    for hf in range(2):
        o_ref[hf] = four[:, hf * HALF:(hf + 1) * HALF]


def _fourier_ctx(f_ctx):
    nb = f_ctx.shape[0]
    w, cd = _ctx_fft_constants()
    return pl.pallas_call(
        _fft_ctx_kernel,
        grid=(nb,),
        in_specs=[
            pl.BlockSpec((None, 2, CTX_LEN, HALF), lambda b: (b, 0, 0, 0)),
            _const_spec((2 * CTX_LEN, 2 * CTX_LEN)), _const_spec((4 * FNET_W, FNET_W)),
        ],
        out_specs=pl.BlockSpec((None, 2, CTX_LEN, HALF), lambda b: (b, 0, 0, 0)),
        out_shape=jax.ShapeDtypeStruct((nb, 2, CTX_LEN, HALF), F32),
        compiler_params=_params(1),
        name="fourier_ctx",
    )(f_ctx, w, cd)


def _mix_mlp_kernel(*refs, tm, ctx, n_cast):
    (x_ref, ret_ref, four_ref, u_ref, vn_ref, gate_ref, mod_ref, ws_ref, bs_ref,
     wa_ref, wb_ref, wc_ref, wo_ref, gpm_ref, gprm_ref, gpom_ref, wup_ref, wdn_ref) = refs[:18]
    o_ref = refs[18 + n_cast]
    _cast_chunks(refs[18:18 + n_cast], refs[19 + n_cast:])
    group = lax.broadcasted_iota(jnp.int32, (GMLP_CHUNK, GMLP_W), 1) // GMLP_GROUP
    parts = []
    for cc in range(tm // GMLP_CHUNK):
        rows = slice(cc * GMLP_CHUNK, (cc + 1) * GMLP_CHUNK)
        vn = vn_ref[0, rows, :]
        s = bs_ref[...]
        for g in range(GMLP_W // GMLP_GROUP):
            s = s + jnp.where(group == g, _dot(ws_ref[g], vn), 0.0)
        parts.append((u_ref[0, rows, :].astype(F32) * s).astype(BF16))
    sgu = jnp.concatenate(parts, axis=0)

    if ctx:
        four = jnp.concatenate([four_ref[0, hf] for hf in range(2)], axis=1).astype(BF16)
    else:
        nk2 = tm // FFT_N1
        four = jnp.concatenate(
            [jnp.concatenate([four_ref[0, hf, pl.ds(k2, FFT_N1, stride=nk2), :] for k2 in range(nk2)], axis=0)
             for hf in range(2)], axis=1).astype(BF16)
    ga = gate_ref[0, :, 0:D_MODEL].astype(F32)
    gb = gate_ref[0, :, D_MODEL:2 * D_MODEL].astype(F32)
    gc = gate_ref[0, :, 2 * D_MODEL:3 * D_MODEL].astype(F32)
    merged = (ga * _dot(ret_ref[0], wa_ref[...]) + gb * _dot(four, wb_ref[...])
              + gc * _dot(sgu, wc_ref[...])).astype(BF16)

    gate1 = mod_ref[0, :, 2 * D_MODEL:3 * D_MODEL]
    shift2 = mod_ref[0, :, 3 * D_MODEL:4 * D_MODEL]
    scale2 = mod_ref[0, :, 4 * D_MODEL:5 * D_MODEL]
    gate2 = mod_ref[0, :, 5 * D_MODEL:6 * D_MODEL]

    halves = [slice(r * (tm // 2), (r + 1) * (tm // 2)) for r in range(2)]
    x1, h2 = [], []
    for rows in halves:
        y = _dot(merged[rows], wo_ref[...])
        x1.append(x_ref[0, rows, :] + gate1 * (_rms(y) * gpm_ref[...]))
        h2.append(((_rms(x1[-1]) * gprm_ref[...]) * (1.0 + scale2) + shift2).astype(BF16))
    h2 = jnp.concatenate(h2, axis=0)

    n_ff = D_FF // FF_CHUNK
    m = None
    for c in range(n_ff - 1):
        cols = slice(c * FF_CHUNK, (c + 1) * FF_CHUNK)
        a = jnp.maximum(_dot(h2, wup_ref[:, cols]), 0.0)
        part = _dot((a * a).astype(BF16), wdn_ref[cols, :])
        m = part if m is None else m + part
    cols = slice((n_ff - 1) * FF_CHUNK, n_ff * FF_CHUNK)
    a = jnp.maximum(_dot(h2, wup_ref[:, cols]), 0.0)
    a = (a * a).astype(BF16)
    for r, rows in enumerate(halves):
        mr = m[rows] + _dot(a[rows], wdn_ref[cols, :])
        o_ref[0, rows, :] = x1[r] + gate2 * (_rms(mr) * gpom_ref[...])


def _mix_mlp(xs, ret, four, ug, vn, gates, layer, mod, ws, bs_tab, wa, wb, wc, wo,
             g_post_mix, g_pre_mlp, g_post_mlp, w_up, w_down, *, tm, ctx, cast=()):
    nb, rows, _ = xs.shape
    tok_spec = lambda w: pl.BlockSpec((1, tm, w), lambda b, i: (b, i, 0))
    lspec = lambda *shape: _layer_spec(shape, layer)
    if ctx:
        four_spec = pl.BlockSpec((1, 2, tm, HALF), lambda b, i: (b, 0, i, 0))
    else:
        four_spec = pl.BlockSpec((1, 2, None, tm, HALF), lambda b, i: (b, 0, i, 0, 0))
    c_in_specs, c_args, c_out_specs, c_out_shapes = _cast_plan(cast, (nb, rows // tm))
    outs = pl.pallas_call(
        functools.partial(_mix_mlp_kernel, tm=tm, ctx=ctx, n_cast=len(cast)),
        grid=(nb, rows // tm),
        in_specs=[
            tok_spec(D_MODEL), tok_spec(RET_W),
            four_spec,
            tok_spec(GMLP_W), tok_spec(GMLP_W), tok_spec(3 * D_MODEL),
            _mod_spec(layer, ctx),
            lspec(GMLP_W // GMLP_GROUP, GMLP_CHUNK, GMLP_CHUNK),
            lspec(GMLP_CHUNK, GMLP_W),
            _const_spec((RET_W, D_MODEL)), _const_spec((FNET_W, D_MODEL)), _const_spec((GMLP_W, D_MODEL)),
            _const_spec((D_MODEL, D_MODEL)),
            lspec(1, D_MODEL), lspec(1, D_MODEL), lspec(1, D_MODEL),
            _const_spec((D_MODEL, D_FF)), _const_spec((D_FF, D_MODEL)),
        ] + c_in_specs,
        out_specs=(tok_spec(D_MODEL),) + tuple(c_out_specs),
        out_shape=(jax.ShapeDtypeStruct((nb, rows, D_MODEL), F32),) + tuple(c_out_shapes),
        compiler_params=_params(2),
        name="mix_mlp_ctx" if ctx else "mix_mlp",
    )(xs, ret, four, ug, vn, gates, mod, ws, bs_tab, wa, wb, wc, wo,
      g_post_mix, g_pre_mlp, g_post_mlp, w_up, w_down, *c_args)
    return outs if cast else outs[0]


def _rope_tables():
    rows = SEQ // GRID_W
    freqs = ROPE_BASE ** (-jnp.arange(ROPE_FREQS, dtype=F32) / ROPE_FREQS)
    ang_r = jnp.arange(rows, dtype=F32)[:, None] * freqs
    ang_c = jnp.arange(GRID_W, dtype=F32)[:, None] * freqs

    def table(fn):
        t = jnp.concatenate([jnp.repeat(fn(ang_r), GRID_W, axis=0), jnp.tile(fn(ang_c), (rows, 1))], axis=-1)
        return t

    cos = table(jnp.cos)
    sin = table(jnp.sin)
    return jnp.concatenate([cos, cos], axis=-1), jnp.concatenate([-sin, sin], axis=-1)


def kernel(x, c, ctx, c_ctx, w_mod, b_mod, g_pre_mix, g_post_mix, g_pre_mlp, g_post_mlp, w_in,
           ret_decay_logit, sgu_w_s, sgu_b_s, sgu_norm, w_branch_a, w_branch_b, w_branch_c, w_out,
           w_up, w_down):
    nb = x.shape[0]
    depth = w_mod.shape[0]
    assert x.shape == (nb, SEQ, D_MODEL) and ctx.shape == (nb, CTX_LEN, D_MODEL) and nb == 2

    cond_rows = jnp.concatenate([c, c_ctx[None, :], jnp.zeros((8 - nb - 1, D_MODEL), F32)], axis=0)
    mod = _modulation(cond_rows, w_mod, b_mod).reshape(depth, 8, 1, 6 * D_MODEL)
    rope_tabs = _rope_tables()
    pavg = jnp.asarray(np.kron(np.eye(GMLP_W // GMLP_GROUP), np.full((GMLP_GROUP, GMLP_GROUP), 1.0 / GMLP_GROUP)), BF16)

    vec = lambda p: p[:, None, :]
    g_pre, g_norm = vec(g_pre_mix), vec(sgu_norm)
    logits = ret_decay_logit.reshape(depth, 2 * HEADS, 1)
    ws = sgu_w_s.astype(BF16)
    bs_tab = jnp.repeat(jnp.swapaxes(sgu_b_s, 1, 2), GMLP_GROUP, axis=2)
    gains = (vec(g_post_mix), vec(g_pre_mlp), vec(g_post_mlp))
    w_in_l = w_in[0].astype(BF16)
    for l in range(depth):
        last = l == depth - 1
        in_args = (l, mod, g_pre, w_in_l, pavg, g_norm)
        *proj, wa, wb, wc, wo = _in_proj(x, *in_args, rope_tabs, tm=TM_IN, ctx=False,
                                         cast=[(w, l) for w in (w_branch_a, w_branch_b, w_branch_c, w_out)])
        q, k, v, sg, f_lat, ug, vn, gates = proj
        mlp_cast = [(w_up, l), (w_down, l)]
        if last:
            kc, vc = _in_proj(ctx, *in_args, None, tm=CTX_LEN, ctx=True, kv_only=True)
            ret, wup, wdn = _retention(l, logits, (q, k, v, sg), (kc, vc), ctx_out=False, cast=mlp_cast)
        else:
            qc, kc, vc, sgc, f_ctx, ugc, vnc, gatesc = _in_proj(ctx, *in_args, None, tm=CTX_LEN, ctx=True)
            ret, retc, wup, wdn = _retention(l, logits, (q, k, v, sg), (qc, kc, vc, sgc), ctx_out=True,
                                             cast=mlp_cast)
        mix_w = (l, mod, ws, bs_tab, wa, wb, wc, wo, *gains, wup, wdn)
        four = _fourier_latent(f_lat)
        if last:
            x = _mix_mlp(x, ret, four, ug, vn, gates, *mix_w, tm=TM_LAT, ctx=False)
        else:
            x, w_in_l = _mix_mlp(x, ret, four, ug, vn, gates, *mix_w, tm=TM_LAT, ctx=False, cast=[(w_in, l + 1)])
            ctx = _mix_mlp(ctx, retc, _fourier_ctx(f_ctx), ugc, vnc, gatesc, *mix_w, tm=CTX_LEN, ctx=True)
    return x
```

```python
import functools

import numpy as np
import jax
import jax.numpy as jnp
from jax import lax
from jax.experimental import pallas as pl
from jax.experimental.pallas import tpu as pltpu

F32 = jnp.float32
BF16 = jnp.bfloat16

D_MODEL = 1024
SEQ = 8192
CTX_LEN = 256
GRID_W = 64
RET_W = 512
HEADS = 4
HEAD_DIM = 128
ROPE_BASE = 10000.0
ROPE_FREQS = HEAD_DIM // 4
FNET_W = 256
FNET_GROUP = 64
GMLP_W = 256
GMLP_GROUP = 64
GMLP_CHUNK = 128
D_FF = 4 * D_MODEL
EPS = 1e-6
IN_W = 4 * RET_W + FNET_W + 2 * GMLP_W + 3 * D_MODEL
COL_F = 4 * RET_W
COL_U = COL_F + FNET_W
COL_VS = COL_U + GMLP_W
COL_GATE = COL_VS + GMLP_W

TM_LAT = 512
TM_IN = 1024
RCHUNK = 256
RET_TILE = 2048
RET_STEPS = SEQ // RET_TILE
RET_CPT = RET_TILE // RCHUNK
N_RCHUNKS = 1 + SEQ // RCHUNK
FF_CHUNK = 1024
FFT_N1 = 64
FFT_N2 = 128
HALF = 128
FFT_STEP = 16
MIX_K2 = TM_LAT // FFT_N1

_VMEM_LIMIT = 56 * 1024 * 1024


def _dot(a, b):
    return jnp.dot(a, b, preferred_element_type=F32)


def _split(x):
    hi = x.astype(BF16)
    lo = (x - hi.astype(F32)).astype(BF16)
    return hi, lo


def _np_split(a64):
    hi = np.asarray(a64, np.float32).astype(BF16)
    lo = (np.asarray(a64, np.float32) - hi.astype(np.float32)).astype(BF16)
    return hi, lo


def _rms(x):
    return x * lax.rsqrt(jnp.mean(x * x, axis=-1, keepdims=True) + EPS)


def _gelu(x):
    return x * (0.5 * (1.0 + jnp.tanh(0.7978845608028654 * (x + 0.044715 * (x * x * x)))))


def _sigmoid(x):
    return 1.0 / (1.0 + jnp.exp(-x))


def _const_spec(shape, nargs=None):
    zeros = (0,) * len(shape)
    return pl.BlockSpec(shape, lambda *_: zeros, pipeline_mode=pl.Buffered(1))


def _layer_spec(shape, layer):
    idx = (layer,) + (0,) * len(shape)
    return pl.BlockSpec((None,) + tuple(shape), lambda *_: idx, pipeline_mode=pl.Buffered(1))


def _params(n_axes):
    return pltpu.CompilerParams(dimension_semantics=("arbitrary",) * n_axes, vmem_limit_bytes=_VMEM_LIMIT)


def _mod_kernel(a_ref, w_ref, b_ref, o_ref):
    a = a_ref[...]
    a = a * _sigmoid(a)
    ah, al = _split(a)
    w = w_ref[0].astype(BF16)
    o_ref[0] = _dot(ah, w) + _dot(al, w) + b_ref[0]


def _modulation(cond_rows, w_mod, b_mod):
    depth = w_mod.shape[0]
    tn = 1536
    return pl.pallas_call(
        _mod_kernel,
        grid=(depth, (6 * D_MODEL) // tn),
        in_specs=[
            pl.BlockSpec((8, D_MODEL), lambda l, j: (0, 0)),
            pl.BlockSpec((1, D_MODEL, tn), lambda l, j: (l, 0, j)),
            pl.BlockSpec((1, 1, tn), lambda l, j: (l, 0, j)),
        ],
        out_specs=pl.BlockSpec((1, 8, tn), lambda l, j: (l, 0, j)),
        out_shape=jax.ShapeDtypeStruct((depth, 8, 6 * D_MODEL), F32),
        compiler_params=_params(2),
        name="modulation",
    )(cond_rows, w_mod, b_mod.reshape(depth, 1, 6 * D_MODEL))


def _cast_plan(cast, grid):
    steps = int(np.prod(grid))
    in_specs, args, out_specs, out_shapes = [], [], [], []

    def step(*idx):
        t = idx[0]
        for n, i in zip(grid[1:], idx[1:]):
            t = t * n + i
        return t

    for stack, layer in cast:
        _, rows, cols = stack.shape
        n_chunks = min(steps, rows // 16)
        assert rows % n_chunks == 0 and steps % n_chunks == 0
        per = steps // n_chunks
        in_specs.append(pl.BlockSpec((None, rows // n_chunks, cols),
                                     lambda *idx, layer=layer, per=per: (layer, step(*idx) // per, 0)))
        out_specs.append(pl.BlockSpec((rows // n_chunks, cols), lambda *idx, per=per: (step(*idx) // per, 0)))
        out_shapes.append(jax.ShapeDtypeStruct((rows, cols), BF16))
        args.append(stack)
    return in_specs, args, out_specs, out_shapes


def _cast_chunks(src_refs, dst_refs):
    for src, dst in zip(src_refs, dst_refs, strict=True):
        dst[...] = src[...].astype(BF16)


def _in_proj_kernel(*refs, rope, kv_only, n_cast):
    n_in = 8 if rope else 6
    ins, cast_src = refs[:n_in], refs[n_in:n_in + n_cast]
    outs, cast_dst = refs[n_in + n_cast:len(refs) - n_cast], refs[len(refs) - n_cast:]
    _cast_chunks(cast_src, cast_dst)
    if rope:
        x_ref, mod_ref, gpre_ref, w_ref, pavg_ref, gn_ref, cos_ref, sin_ref = ins
    else:
        x_ref, mod_ref, gpre_ref, w_ref, pavg_ref, gn_ref = ins
    x = x_ref[0]
    shift = mod_ref[0, :, 0:D_MODEL]
    scale = mod_ref[0, :, D_MODEL:2 * D_MODEL]
    h = _rms(x) * gpre_ref[...]
    hb = (h * (1.0 + scale) + shift).astype(BF16)

    def proj(lo, hi):
        return _dot(hb, w_ref[:, lo:hi])

    def rope_store(z, dst):
        if not rope:
            dst[0] = z.astype(BF16)
            return
        cos = cos_ref[...]
        sin = sin_ref[...]
        for hd in range(HEADS):
            a = z[:, hd * HEAD_DIM:(hd + 1) * HEAD_DIM]
            r = a * cos + pltpu.roll(a, HEAD_DIM // 2, 1) * sin
            dst[0, :, hd * HEAD_DIM:(hd + 1) * HEAD_DIM] = r.astype(BF16)

    if kv_only:
        k_ref, v_ref = outs
    else:
        q_ref, k_ref, v_ref, sg_ref, f_ref, u_ref, vn_ref, gate_ref = outs
        rope_store(proj(0, RET_W), q_ref)
    rope_store(proj(RET_W, 2 * RET_W) * (HEAD_DIM ** -0.5), k_ref)
    v_ref[0] = proj(2 * RET_W, 3 * RET_W).astype(BF16)
    if kv_only:
        return
    g = proj(3 * RET_W, 4 * RET_W)
    sg_ref[0] = (g * _sigmoid(g)).astype(BF16)

    f = proj(COL_F, COL_U)
    for hf in range(2):
        fh = f[:, hf * HALF:(hf + 1) * HALF]
        if rope:
            for n1 in range(f.shape[0] // FFT_N2):
                for nj in range(FFT_N2 // FFT_STEP):
                    f_ref[0, hf, nj, n1 * FFT_STEP:(n1 + 1) * FFT_STEP, :] = (
                        fh[n1 * FFT_N2 + nj * FFT_STEP:n1 * FFT_N2 + (nj + 1) * FFT_STEP])
        else:
            f_ref[0, hf] = fh

    u_ref[0] = _gelu(proj(COL_U, COL_VS)).astype(BF16)
    vg = _gelu(proj(COL_VS, COL_GATE))
    sh, sl = _split(vg * vg)
    pavg = pavg_ref[...]
    ms = _dot(sh, pavg) + _dot(sl, pavg)
    vn_ref[0] = (vg * lax.rsqrt(ms + EPS) * gn_ref[...]).astype(BF16)

    for c in range(3):
        z = proj(COL_GATE + c * D_MODEL, COL_GATE + (c + 1) * D_MODEL)
        gate_ref[0, :, c * D_MODEL:(c + 1) * D_MODEL] = _sigmoid(z).astype(BF16)


def _mod_spec(layer, ctx):
    idx = (lambda b, i: (layer, 2, 0, 0)) if ctx else (lambda b, i: (layer, b, 0, 0))
    return pl.BlockSpec((None, 1, 1, 6 * D_MODEL), idx)


def _in_proj(xs, layer, mod, g_pre, w_in, pavg, g_norm, rope_tabs, *, tm, ctx, kv_only=False, cast=()):
    nb, rows, _ = xs.shape
    rope = rope_tabs is not None
    tok_spec = lambda w: pl.BlockSpec((1, tm, w), lambda b, i: (b, i, 0))
    in_specs = [
        tok_spec(D_MODEL),
        _mod_spec(layer, ctx),
        _layer_spec((1, D_MODEL), layer),
        _const_spec((D_MODEL, IN_W)),
        _const_spec((GMLP_W, GMLP_W)),
        _layer_spec((1, GMLP_W), layer),
    ]
    args = [xs, mod, g_pre, w_in, pavg, g_norm]
    if rope:
        in_specs += [pl.BlockSpec((tm, HEAD_DIM), lambda b, i: (i, 0))] * 2
        args += list(rope_tabs)
    c_in_specs, c_args, c_out_specs, c_out_shapes = _cast_plan(cast, (nb, rows // tm))
    in_specs += c_in_specs
    args += c_args
    bf = lambda w: jax.ShapeDtypeStruct((nb, rows, w), BF16)
    if kv_only:
        out_shapes = (bf(RET_W), bf(RET_W))
        out_specs = (tok_spec(RET_W), tok_spec(RET_W))
    else:
        if rope:
            n_nj = FFT_N2 // FFT_STEP
            f_shape = jax.ShapeDtypeStruct((nb, 2, n_nj, FFT_N1 * FFT_STEP, HALF), F32)
            f_spec = pl.BlockSpec((1, 2, n_nj, (tm // FFT_N2) * FFT_STEP, HALF), lambda b, i: (b, 0, 0, i, 0))
        else:
            f_shape = jax.ShapeDtypeStruct((nb, 2, rows, HALF), F32)
            f_spec = pl.BlockSpec((1, 2, tm, HALF), lambda b, i: (b, 0, i, 0))
        out_shapes = (
            bf(RET_W), bf(RET_W), bf(RET_W), bf(RET_W),
            f_shape,
            bf(GMLP_W), bf(GMLP_W),
            bf(3 * D_MODEL),
        )
        out_specs = (
            tok_spec(RET_W), tok_spec(RET_W), tok_spec(RET_W), tok_spec(RET_W),
            f_spec,
            tok_spec(GMLP_W), tok_spec(GMLP_W), tok_spec(3 * D_MODEL),
        )
    return pl.pallas_call(
        functools.partial(_in_proj_kernel, rope=rope, kv_only=kv_only, n_cast=len(cast)),
        grid=(nb, rows // tm),
        in_specs=in_specs,
        out_specs=tuple(out_specs) + tuple(c_out_specs),
        out_shape=tuple(out_shapes) + tuple(c_out_shapes),
        compiler_params=_params(2),
        name="in_proj_ctx" if ctx else "in_proj",
    )(*args)


def _ret_kernel(*refs, ctx_out, n_cast):
    n_in, n_out = (9, 2) if ctx_out else (7, 1)
    ins, refs = refs[:n_in], refs[n_in:]
    cast_src, refs = refs[:n_cast], refs[n_cast:]
    outs, refs = refs[:n_out], refs[n_out:]
    cast_dst, scratch = refs[:n_cast], refs[n_cast:]
    _cast_chunks(cast_src, cast_dst)
    (logit_ref, ql_ref, kl_ref, vl_ref, sgl_ref, kc_ref, vc_ref) = ins[:7]
    if ctx_out:
        qc_ref, sgc_ref = ins[7:]
        ol_ref, oc_ref = outs
    else:
        (ol_ref,) = outs
    mask_ref, wf_ref, wb_ref, qf_ref, qb_ref, df_ref, db_ref, sf_ref, sb_ref, sball_ref = scratch
    b = pl.program_id(0)
    phase = pl.program_id(1)
    j = pl.program_id(2)

    @pl.when((b == 0) & (phase == 0) & (j == 0))
    def _():
        x = logit_ref[...]
        lg = -(jnp.maximum(-x, 0.0) + jnp.log(1.0 + jnp.exp(-jnp.abs(x))))
        row = lax.broadcasted_iota(jnp.int32, (RCHUNK, RCHUNK), 0).astype(F32)
        col = lax.broadcasted_iota(jnp.int32, (RCHUNK, RCHUNK), 1).astype(F32)
        diff = row - col
        pos = row[:, :HEAD_DIM]
        for hd in range(HEADS):
            lf = lg[hd:hd + 1, :]
            lb = lg[HEADS + hd:HEADS + hd + 1, :]
            mask_ref[hd] = jnp.where(diff >= 0.0, jnp.exp(lf * jnp.maximum(diff, 0.0)),
                                     jnp.exp(lb * jnp.maximum(-diff, 0.0)))
            wf_ref[hd] = jnp.exp(lf * (RCHUNK - 1.0 - pos))
            wb_ref[hd] = jnp.exp(lb * pos)
            qf_ref[hd] = jnp.exp(lf * (pos + 1.0))
            qb_ref[hd] = jnp.exp(lb * (RCHUNK - pos))
            df_ref[hd] = jnp.exp(jnp.broadcast_to(lf, (HEAD_DIM, HEAD_DIM)) * float(RCHUNK))
            db_ref[hd] = jnp.exp(jnp.broadcast_to(lb, (HEAD_DIM, HEAD_DIM)) * float(RCHUNK))

    def chunk_updates(hd, k_ref, v_ref, n_chunks, w_ref):
        cols = slice(hd * HEAD_DIM, (hd + 1) * HEAD_DIM)
        us = []
        for c in range(n_chunks):
            rows = slice(c * RCHUNK, (c + 1) * RCHUNK)
            kw = (k_ref[0, rows, cols].astype(F32) * w_ref[hd]).astype(BF16)
            us.append(lax.dot_general(kw, v_ref[0, rows, cols], (((0,), (0,)), ((), ())),
                                      preferred_element_type=F32))
        return us

    def backward_chunks(k_ref, v_ref, n_chunks, first_chunk):
        for hd in range(HEADS):
            us = chunk_updates(hd, k_ref, v_ref, n_chunks, wb_ref)
            s = sb_ref[hd]
            for c in reversed(range(n_chunks)):
                sball_ref[first_chunk + c, hd] = s.astype(BF16)
                s = db_ref[hd] * s + us[c]
            sb_ref[hd] = s

    def forward_chunks(q_ref, k_ref, v_ref, sg_ref, o_ref, n_chunks, first_chunk):
        starts = []
        for hd in range(HEADS):
            us = chunk_updates(hd, k_ref, v_ref, n_chunks, wf_ref)
            st = [sf_ref[hd]]
            for c in range(n_chunks):
                st.append(df_ref[hd] * st[-1] + us[c])
            sf_ref[hd] = st[-1]
            starts.append(st)
        if o_ref is None:
            return
        for c in range(n_chunks):
            rows = slice(c * RCHUNK, (c + 1) * RCHUNK)
            for hd in range(HEADS):
                cols = slice(hd * HEAD_DIM, (hd + 1) * HEAD_DIM)
                qh = q_ref[0, rows, cols]
                sc = lax.dot_general(qh, k_ref[0, rows, cols], (((1,), (1,)), ((), ())),
                                     preferred_element_type=F32)
                qf = qh.astype(F32)
                lhs = jnp.concatenate([(sc * mask_ref[hd]).astype(BF16),
                                       (qf * qf_ref[hd]).astype(BF16),
                                       (qf * qb_ref[hd]).astype(BF16)], axis=1)
                rhs = jnp.concatenate([v_ref[0, rows, cols], starts[hd][c].astype(BF16),
                                       sball_ref[first_chunk + c, hd]], axis=0)
                o = _dot(lhs, rhs)
                o_ref[0, rows, cols] = (_rms(o) * sg_ref[0, rows, cols].astype(F32)).astype(BF16)

    @pl.when(phase == 0)
    def _():
        @pl.when(j == 0)
        def _():
            sb_ref[...] = jnp.zeros_like(sb_ref)
            backward_chunks(kc_ref, vc_ref, 1, 0)

        backward_chunks(kl_ref, vl_ref, RET_CPT, 1 + (RET_STEPS - 1 - j) * RET_CPT)

    @pl.when(phase == 1)
    def _():
        @pl.when(j == 0)
        def _():
            sf_ref[...] = jnp.zeros_like(sf_ref)
            if ctx_out:
                forward_chunks(qc_ref, kc_ref, vc_ref, sgc_ref, oc_ref, 1, 0)
            else:
                forward_chunks(None, kc_ref, vc_ref, None, None, 1, 0)

        forward_chunks(ql_ref, kl_ref, vl_ref, sgl_ref, ol_ref, RET_CPT, 1 + j * RET_CPT)


def _retention(layer, logits, lat, ctx, *, ctx_out, cast=()):
    nb = lat[0].shape[0]
    kv_idx = lambda b, p, j: (b, jnp.where(p == 0, RET_STEPS - 1 - j, j), 0)
    q_idx = lambda b, p, j: (b, jnp.where(p == 0, 0, j), 0)
    c_idx = lambda b, p, j: (b, 0, 0)
    lat_blk = (1, RET_TILE, RET_W)
    ctx_blk = (1, CTX_LEN, RET_W)
    in_specs = [
        _layer_spec((2 * HEADS, 1), layer),
        pl.BlockSpec(lat_blk, q_idx), pl.BlockSpec(lat_blk, kv_idx), pl.BlockSpec(lat_blk, kv_idx),
        pl.BlockSpec(lat_blk, q_idx),
        pl.BlockSpec(ctx_blk, c_idx), pl.BlockSpec(ctx_blk, c_idx),
    ]
    if ctx_out:
        qc, kc, vc, sgc = ctx
        args = [logits, *lat, kc, vc, qc, sgc]
        in_specs += [pl.BlockSpec(ctx_blk, c_idx), pl.BlockSpec(ctx_blk, c_idx)]
        out_specs = (pl.BlockSpec(lat_blk, q_idx), pl.BlockSpec(ctx_blk, c_idx))
        out_shape = (jax.ShapeDtypeStruct((nb, SEQ, RET_W), BF16), jax.ShapeDtypeStruct((nb, CTX_LEN, RET_W), BF16))
    else:
        kc, vc = ctx
        args = [logits, *lat, kc, vc]
        out_specs = (pl.BlockSpec(lat_blk, q_idx),)
        out_shape = (jax.ShapeDtypeStruct((nb, SEQ, RET_W), BF16),)
    grid = (nb, 2, RET_STEPS)
    c_in_specs, c_args, c_out_specs, c_out_shapes = _cast_plan(cast, grid)
    state = pltpu.VMEM((HEADS, HEAD_DIM, HEAD_DIM), F32)
    pos_tab = pltpu.VMEM((HEADS, RCHUNK, HEAD_DIM), F32)
    return pl.pallas_call(
        functools.partial(_ret_kernel, ctx_out=ctx_out, n_cast=len(cast)),
        grid=grid,
        in_specs=in_specs + c_in_specs,
        out_specs=out_specs + tuple(c_out_specs),
        out_shape=out_shape + tuple(c_out_shapes),
        scratch_shapes=[
            pltpu.VMEM((HEADS, RCHUNK, RCHUNK), F32),
            pos_tab, pos_tab, pos_tab, pos_tab,
            state, state,
            state, state,
            pltpu.VMEM((N_RCHUNKS, HEADS, HEAD_DIM, HEAD_DIM), BF16),
        ],
        compiler_params=_params(3),
        name="retention",
    )(*args, *c_args)


def _dft_cos_sin(n):
    idx = np.arange(n, dtype=np.float64)
    ang = 2.0 * np.pi * ((idx[:, None] * idx[None, :]) % n) / n
    return np.cos(ang), np.sin(ang)


def _channel_dft(scale):
    c, s = _dft_cos_sin(FNET_GROUP)
    eye = np.eye(FNET_W // FNET_GROUP)
    return np.concatenate([np.kron(eye, c), np.kron(eye, s)], axis=0) * scale


def _fft_constants():
    k1 = np.arange(FFT_N1, dtype=np.float64)[None, :, None]
    n1 = np.arange(FFT_N1, dtype=np.float64)[None, None, :]
    n2 = np.arange(FFT_N2, dtype=np.float64)[:, None, None]
    ang = 2.0 * np.pi * ((k1 * (FFT_N2 * n1 + n2)) % SEQ) / SEQ
    m1 = np.concatenate([np.cos(ang), -np.sin(ang)], axis=1)
    c, s = _dft_cos_sin(FFT_N2)
    m3 = np.block([[c, s], [-s, c]])
    return (_np_split_cat(m1, -1), _np_split_cat(m3, -1),
            _np_split_cat(_channel_dft((SEQ * FNET_GROUP) ** -0.5), 0))


def _ctx_fft_constants():
    c, s = _dft_cos_sin(CTX_LEN)
    return (_np_split_cat(np.concatenate([c, -s], axis=0), -1),
            _np_split_cat(_channel_dft((CTX_LEN * FNET_GROUP) ** -0.5), 0))


def _np_split_cat(a64, axis):
    hi, lo = _np_split(a64)
    return jnp.asarray(np.concatenate([hi, lo], axis=axis))


def _twice(x, axis):
    return jnp.concatenate([x, x], axis=axis)


def _fft1_kernel(x_ref, m_ref, o_ref):
    for jj in range(FFT_STEP):
        x = jnp.concatenate([x_ref[hf, pl.ds(jj, FFT_N1, stride=FFT_STEP), :] for hf in range(2)],
                            axis=1).astype(BF16)
        a = _dot(m_ref[jj], _twice(x, 0))
        for hf in range(2):
            for ri in range(2):
                for kb in range(FFT_N1 // FFT_STEP):
                    r0 = ri * FFT_N1 + kb * FFT_STEP
                    o_ref[hf, ri, kb, jj * FFT_STEP:(jj + 1) * FFT_STEP, :] = (
                        a[r0:r0 + FFT_STEP, hf * HALF:(hf + 1) * HALF])


def _fft3_kernel(a_ref, w_ref, cd_ref, o_ref):
    zs = []
    for jj in range(FFT_STEP):
        rhs = jnp.concatenate(
            [jnp.concatenate([a_ref[hf, ri, pl.ds(jj, FFT_N2, stride=FFT_STEP), :] for ri in range(2)], axis=0)
             for hf in range(2)], axis=1).astype(BF16)
        y = _dot(w_ref[...], _twice(rhs, 0))
        zs.append(jnp.concatenate([y[:FFT_N2], y[FFT_N2:]], axis=1).astype(BF16))
    z = jnp.concatenate(zs, axis=0)
    four = _dot(_twice(z, 1), cd_ref[...])
    for jj in range(FFT_STEP):
        for kt in range(FFT_N2 // MIX_K2):
            for hf in range(2):
                r0 = jj * FFT_N2 + kt * MIX_K2
                o_ref[hf, kt, jj * MIX_K2:(jj + 1) * MIX_K2, :] = four[r0:r0 + MIX_K2, hf * HALF:(hf + 1) * HALF]


def _fourier_latent(f_blk):
    nb = f_blk.shape[0]
    m1, m3, cd = _fft_constants()
    n_kb = FFT_N1 // FFT_STEP
    a = pl.pallas_call(
        _fft1_kernel,
        grid=(nb, FFT_N2 // FFT_STEP),
        in_specs=[
            pl.BlockSpec((None, 2, None, FFT_N1 * FFT_STEP, HALF), lambda b, j: (b, 0, j, 0, 0)),
            pl.BlockSpec((FFT_STEP, 2 * FFT_N1, 2 * FFT_N1), lambda b, j: (j, 0, 0)),
        ],
        out_specs=pl.BlockSpec((None, 2, 2, n_kb, FFT_STEP * FFT_STEP, HALF), lambda b, j: (b, 0, 0, 0, j, 0)),
        out_shape=jax.ShapeDtypeStruct((nb, 2, 2, n_kb, FFT_N2 * FFT_STEP, HALF), F32),
        compiler_params=_params(2),
        name="fourier_stage1",
    )(f_blk, m1)
    return pl.pallas_call(
        _fft3_kernel,
        grid=(nb, n_kb),
        in_specs=[
            pl.BlockSpec((None, 2, 2, None, FFT_N2 * FFT_STEP, HALF), lambda b, j: (b, 0, 0, j, 0, 0)),
            _const_spec((2 * FFT_N2, 4 * FFT_N2)), _const_spec((4 * FNET_W, FNET_W)),
        ],
        out_specs=pl.BlockSpec((None, 2, FFT_N2 // MIX_K2, FFT_STEP * MIX_K2, HALF), lambda b, j: (b, 0, 0, j, 0)),
        out_shape=jax.ShapeDtypeStruct((nb, 2, FFT_N2 // MIX_K2, FFT_N1 * MIX_K2, HALF), F32),
        compiler_params=_params(2),
        name="fourier_stage2",
    )(a, m3, cd)


def _fft_ctx_kernel(x_ref, w_ref, cd_ref, o_ref):
    x = jnp.concatenate([x_ref[hf] for hf in range(2)], axis=1).astype(BF16)
    z = _dot(w_ref[...], _twice(x, 0))
    z = jnp.concatenate([z[:CTX_LEN], z[CTX_LEN:]], axis=1).astype(BF16)
    four = _dot(_twice(z, 1), cd_ref[...])
    for hf in range(2):
        o_ref[hf] = four[:, hf * HALF:(hf + 1) * HALF]


def _fourier_ctx(f_ctx):
    nb = f_ctx.shape[0]
    w, cd = _ctx_fft_constants()
    return pl.pallas_call(
        _fft_ctx_kernel,
        grid=(nb,),
        in_specs=[
            pl.BlockSpec((None, 2, CTX_LEN, HALF), lambda b: (b, 0, 0, 0)),
            _const_spec((2 * CTX_LEN, 2 * CTX_LEN)), _const_spec((4 * FNET_W, FNET_W)),
        ],
        out_specs=pl.BlockSpec((None, 2, CTX_LEN, HALF), lambda b: (b, 0, 0, 0)),
        out_shape=jax.ShapeDtypeStruct((nb, 2, CTX_LEN, HALF), F32),
        compiler_params=_params(1),
        name="fourier_ctx",
    )(f_ctx, w, cd)


def _mix_mlp_kernel(*refs, tm, ctx, n_cast):
    (x_ref, ret_ref, four_ref, u_ref, vn_ref, gate_ref, mod_ref, ws_ref, bs_ref,
     wa_ref, wb_ref, wc_ref, wo_ref, gpm_ref, gprm_ref, gpom_ref, wup_ref, wdn_ref) = refs[:18]
    o_ref = refs[18 + n_cast]
    _cast_chunks(refs[18:18 + n_cast], refs[19 + n_cast:])
    group = lax.broadcasted_iota(jnp.int32, (GMLP_CHUNK, GMLP_W), 1) // GMLP_GROUP
    parts = []
    for cc in range(tm // GMLP_CHUNK):
        rows = slice(cc * GMLP_CHUNK, (cc + 1) * GMLP_CHUNK)
        vn = vn_ref[0, rows, :]
        s = bs_ref[...]
        for g in range(GMLP_W // GMLP_GROUP):
            s = s + jnp.where(group == g, _dot(ws_ref[g], vn), 0.0)
        parts.append((u_ref[0, rows, :].astype(F32) * s).astype(BF16))
    sgu = jnp.concatenate(parts, axis=0)

    if ctx:
        four = jnp.concatenate([four_ref[0, hf] for hf in range(2)], axis=1).astype(BF16)
    else:
        nk2 = tm // FFT_N1
        four = jnp.concatenate(
            [jnp.concatenate([four_ref[0, hf, pl.ds(k2, FFT_N1, stride=nk2), :] for k2 in range(nk2)], axis=0)
             for hf in range(2)], axis=1).astype(BF16)
    ga = gate_ref[0, :, 0:D_MODEL].astype(F32)
    gb = gate_ref[0, :, D_MODEL:2 * D_MODEL].astype(F32)
    gc = gate_ref[0, :, 2 * D_MODEL:3 * D_MODEL].astype(F32)
    merged = (ga * _dot(ret_ref[0], wa_ref[...]) + gb * _dot(four, wb_ref[...])
              + gc * _dot(sgu, wc_ref[...])).astype(BF16)

    gate1 = mod_ref[0, :, 2 * D_MODEL:3 * D_MODEL]
    shift2 = mod_ref[0, :, 3 * D_MODEL:4 * D_MODEL]
    scale2 = mod_ref[0, :, 4 * D_MODEL:5 * D_MODEL]
    gate2 = mod_ref[0, :, 5 * D_MODEL:6 * D_MODEL]

    halves = [slice(r * (tm // 2), (r + 1) * (tm // 2)) for r in range(2)]
    x1, h2 = [], []
    for rows in halves:
        y = _dot(merged[rows], wo_ref[...])
        x1.append(x_ref[0, rows, :] + gate1 * (_rms(y) * gpm_ref[...]))
        h2.append(((_rms(x1[-1]) * gprm_ref[...]) * (1.0 + scale2) + shift2).astype(BF16))
    h2 = jnp.concatenate(h2, axis=0)

    n_ff = D_FF // FF_CHUNK
    m = None
    for c in range(n_ff - 1):
        cols = slice(c * FF_CHUNK, (c + 1) * FF_CHUNK)
        a = jnp.maximum(_dot(h2, wup_ref[:, cols]), 0.0)
        part = _dot((a * a).astype(BF16), wdn_ref[cols, :])
        m = part if m is None else m + part
    cols = slice((n_ff - 1) * FF_CHUNK, n_ff * FF_CHUNK)
    a = jnp.maximum(_dot(h2, wup_ref[:, cols]), 0.0)
    a = (a * a).astype(BF16)
    for r, rows in enumerate(halves):
        mr = m[rows] + _dot(a[rows], wdn_ref[cols, :])
        o_ref[0, rows, :] = x1[r] + gate2 * (_rms(mr) * gpom_ref[...])


def _mix_mlp(xs, ret, four, ug, vn, gates, layer, mod, ws, bs_tab, wa, wb, wc, wo,
             g_post_mix, g_pre_mlp, g_post_mlp, w_up, w_down, *, tm, ctx, cast=()):
    nb, rows, _ = xs.shape
    tok_spec = lambda w: pl.BlockSpec((1, tm, w), lambda b, i: (b, i, 0))
    lspec = lambda *shape: _layer_spec(shape, layer)
    if ctx:
        four_spec = pl.BlockSpec((1, 2, tm, HALF), lambda b, i: (b, 0, i, 0))
    else:
        four_spec = pl.BlockSpec((1, 2, None, tm, HALF), lambda b, i: (b, 0, i, 0, 0))
    c_in_specs, c_args, c_out_specs, c_out_shapes = _cast_plan(cast, (nb, rows // tm))
    outs = pl.pallas_call(
        functools.partial(_mix_mlp_kernel, tm=tm, ctx=ctx, n_cast=len(cast)),
        grid=(nb, rows // tm),
        in_specs=[
            tok_spec(D_MODEL), tok_spec(RET_W),
            four_spec,
            tok_spec(GMLP_W), tok_spec(GMLP_W), tok_spec(3 * D_MODEL),
            _mod_spec(layer, ctx),
            lspec(GMLP_W // GMLP_GROUP, GMLP_CHUNK, GMLP_CHUNK),
            lspec(GMLP_CHUNK, GMLP_W),
            _const_spec((RET_W, D_MODEL)), _const_spec((FNET_W, D_MODEL)), _const_spec((GMLP_W, D_MODEL)),
            _const_spec((D_MODEL, D_MODEL)),
            lspec(1, D_MODEL), lspec(1, D_MODEL), lspec(1, D_MODEL),
            _const_spec((D_MODEL, D_FF)), _const_spec((D_FF, D_MODEL)),
        ] + c_in_specs,
        out_specs=(tok_spec(D_MODEL),) + tuple(c_out_specs),
        out_shape=(jax.ShapeDtypeStruct((nb, rows, D_MODEL), F32),) + tuple(c_out_shapes),
        compiler_params=_params(2),
        name="mix_mlp_ctx" if ctx else "mix_mlp",
    )(xs, ret, four, ug, vn, gates, mod, ws, bs_tab, wa, wb, wc, wo,
      g_post_mix, g_pre_mlp, g_post_mlp, w_up, w_down, *c_args)
    return outs if cast else outs[0]


def _rope_tables():
    rows = SEQ // GRID_W
    freqs = ROPE_BASE ** (-jnp.arange(ROPE_FREQS, dtype=F32) / ROPE_FREQS)
    ang_r = jnp.arange(rows, dtype=F32)[:, None] * freqs
    ang_c = jnp.arange(GRID_W, dtype=F32)[:, None] * freqs

    def table(fn):
        t = jnp.concatenate([jnp.repeat(fn(ang_r), GRID_W, axis=0), jnp.tile(fn(ang_c), (rows, 1))], axis=-1)
        return t

    cos = table(jnp.cos)
    sin = table(jnp.sin)
    return jnp.concatenate([cos, cos], axis=-1), jnp.concatenate([-sin, sin], axis=-1)


def kernel(x, c, ctx, c_ctx, w_mod, b_mod, g_pre_mix, g_post_mix, g_pre_mlp, g_post_mlp, w_in,
           ret_decay_logit, sgu_w_s, sgu_b_s, sgu_norm, w_branch_a, w_branch_b, w_branch_c, w_out,
           w_up, w_down):
    nb = x.shape[0]
    depth = w_mod.shape[0]
    assert x.shape == (nb, SEQ, D_MODEL) and ctx.shape == (nb, CTX_LEN, D_MODEL) and nb == 2

    cond_rows = jnp.concatenate([c, c_ctx[None, :], jnp.zeros((8 - nb - 1, D_MODEL), F32)], axis=0)
    mod = _modulation(cond_rows, w_mod, b_mod).reshape(depth, 8, 1, 6 * D_MODEL)
    rope_tabs = _rope_tables()
    pavg = jnp.asarray(np.kron(np.eye(GMLP_W // GMLP_GROUP), np.full((GMLP_GROUP, GMLP_GROUP), 1.0 / GMLP_GROUP)), BF16)

    vec = lambda p: p[:, None, :]
    g_pre, g_norm = vec(g_pre_mix), vec(sgu_norm)
    logits = ret_decay_logit.reshape(depth, 2 * HEADS, 1)
    ws = sgu_w_s.astype(BF16)
    bs_tab = jnp.repeat(jnp.swapaxes(sgu_b_s, 1, 2), GMLP_GROUP, axis=2)
    gains = (vec(g_post_mix), vec(g_pre_mlp), vec(g_post_mlp))
    w_in_l = w_in[0].astype(BF16)
    for l in range(depth):
        last = l == depth - 1
        in_args = (l, mod, g_pre, w_in_l, pavg, g_norm)
        *proj, wa, wb, wc, wo = _in_proj(x, *in_args, rope_tabs, tm=TM_IN, ctx=False,
                                         cast=[(w, l) for w in (w_branch_a, w_branch_b, w_branch_c, w_out)])
        q, k, v, sg, f_lat, ug, vn, gates = proj
        mlp_cast = [(w_up, l), (w_down, l)]
        if last:
            kc, vc = _in_proj(ctx, *in_args, None, tm=CTX_LEN, ctx=True, kv_only=True)
            ret, wup, wdn = _retention(l, logits, (q, k, v, sg), (kc, vc), ctx_out=False, cast=mlp_cast)
        else:
            qc, kc, vc, sgc, f_ctx, ugc, vnc, gatesc = _in_proj(ctx, *in_args, None, tm=CTX_LEN, ctx=True)
            ret, retc, wup, wdn = _retention(l, logits, (q, k, v, sg), (qc, kc, vc, sgc), ctx_out=True,
                                             cast=mlp_cast)
        mix_w = (l, mod, ws, bs_tab, wa, wb, wc, wo, *gains, wup, wdn)
        four = _fourier_latent(f_lat)
        if last:
            x = _mix_mlp(x, ret, four, ug, vn, gates, *mix_w, tm=TM_LAT, ctx=False)
        else:
            x, w_in_l = _mix_mlp(x, ret, four, ug, vn, gates, *mix_w, tm=TM_LAT, ctx=False, cast=[(w_in, l + 1)])
            ctx = _mix_mlp(ctx, retc, _fourier_ctx(f_ctx), ugc, vnc, gatesc, *mix_w, tm=CTX_LEN, ctx=True)
    return x
```

```python
import functools

import numpy as np
import jax
import jax.numpy as jnp
from jax import lax
from jax.experimental import pallas as pl
from jax.experimental.pallas import tpu as pltpu

F32 = jnp.float32
BF16 = jnp.bfloat16

D_MODEL = 1024
SEQ = 8192
CTX_LEN = 256
GRID_W = 64
RET_W = 512
HEADS = 4
HEAD_DIM = 128
ROPE_BASE = 10000.0
ROPE_FREQS = HEAD_DIM // 4
FNET_W = 256
FNET_GROUP = 64
GMLP_W = 256
GMLP_GROUP = 64
GMLP_CHUNK = 128
D_FF = 4 * D_MODEL
EPS = 1e-6
IN_W = 4 * RET_W + FNET_W + 2 * GMLP_W + 3 * D_MODEL
COL_F = 4 * RET_W
COL_U = COL_F + FNET_W
COL_VS = COL_U + GMLP_W
COL_GATE = COL_VS + GMLP_W

TM_LAT = 512
TM_IN = 1024
RCHUNK = 256
RET_TILE = 2048
RET_STEPS = SEQ // RET_TILE
RET_CPT = RET_TILE // RCHUNK
N_RCHUNKS = 1 + SEQ // RCHUNK
FF_CHUNK = 1024
FFT_N1 = 64
FFT_N2 = 128
HALF = 128
FFT_STEP = 16
MIX_K2 = TM_LAT // FFT_N1

_VMEM_LIMIT = 56 * 1024 * 1024


def _dot(a, b):
    return jnp.dot(a, b, preferred_element_type=F32)


def _split(x):
    hi = x.astype(BF16)
    lo = (x - hi.astype(F32)).astype(BF16)
    return hi, lo


def _np_split(a64):
    hi = np.asarray(a64, np.float32).astype(BF16)
    lo = (np.asarray(a64, np.float32) - hi.astype(np.float32)).astype(BF16)
    return hi, lo


def _rms(x):
    return x * lax.rsqrt(jnp.mean(x * x, axis=-1, keepdims=True) + EPS)


def _gelu(x):
    return x * (0.5 * (1.0 + jnp.tanh(0.7978845608028654 * (x + 0.044715 * (x * x * x)))))


def _sigmoid(x):
    return 1.0 / (1.0 + jnp.exp(-x))


def _const_spec(shape, nargs=None):
    zeros = (0,) * len(shape)
    return pl.BlockSpec(shape, lambda *_: zeros, pipeline_mode=pl.Buffered(1))


def _layer_spec(shape, layer):
    idx = (layer,) + (0,) * len(shape)
    return pl.BlockSpec((None,) + tuple(shape), lambda *_: idx, pipeline_mode=pl.Buffered(1))


def _params(n_axes):
    return pltpu.CompilerParams(dimension_semantics=("arbitrary",) * n_axes, vmem_limit_bytes=_VMEM_LIMIT)


def _mod_kernel(a_ref, w_ref, b_ref, o_ref):
    a = a_ref[...]
    a = a * _sigmoid(a)
    ah, al = _split(a)
    w = w_ref[0].astype(BF16)
    o_ref[0] = _dot(ah, w) + _dot(al, w) + b_ref[0]


def _modulation(cond_rows, w_mod, b_mod):
    depth = w_mod.shape[0]
    tn = 1536
    return pl.pallas_call(
        _mod_kernel,
        grid=(depth, (6 * D_MODEL) // tn),
        in_specs=[
            pl.BlockSpec((8, D_MODEL), lambda l, j: (0, 0)),
            pl.BlockSpec((1, D_MODEL, tn), lambda l, j: (l, 0, j)),
            pl.BlockSpec((1, 1, tn), lambda l, j: (l, 0, j)),
        ],
        out_specs=pl.BlockSpec((1, 8, tn), lambda l, j: (l, 0, j)),
        out_shape=jax.ShapeDtypeStruct((depth, 8, 6 * D_MODEL), F32),
        compiler_params=_params(2),
        name="modulation",
    )(cond_rows, w_mod, b_mod.reshape(depth, 1, 6 * D_MODEL))


def _cast_plan(cast, grid):
    steps = int(np.prod(grid))
    in_specs, args, out_specs, out_shapes = [], [], [], []

    def step(*idx):
        t = idx[0]
        for n, i in zip(grid[1:], idx[1:]):
            t = t * n + i
        return t

    for stack, layer in cast:
        _, rows, cols = stack.shape
        n_chunks = min(steps, rows // 16)
        assert rows % n_chunks == 0 and steps % n_chunks == 0
        per = steps // n_chunks
        in_specs.append(pl.BlockSpec((None, rows // n_chunks, cols),
                                     lambda *idx, layer=layer, per=per: (layer, step(*idx) // per, 0)))
        out_specs.append(pl.BlockSpec((rows // n_chunks, cols), lambda *idx, per=per: (step(*idx) // per, 0)))
        out_shapes.append(jax.ShapeDtypeStruct((rows, cols), BF16))
        args.append(stack)
    return in_specs, args, out_specs, out_shapes


def _cast_chunks(src_refs, dst_refs):
    for src, dst in zip(src_refs, dst_refs, strict=True):
        dst[...] = src[...].astype(BF16)


def _in_proj_kernel(*refs, rope, kv_only, n_cast):
    n_in = 8 if rope else 6
    ins, cast_src = refs[:n_in], refs[n_in:n_in + n_cast]
    outs, cast_dst = refs[n_in + n_cast:len(refs) - n_cast], refs[len(refs) - n_cast:]
    _cast_chunks(cast_src, cast_dst)
    if rope:
        x_ref, mod_ref, gpre_ref, w_ref, pavg_ref, gn_ref, rrow_ref, rcol_ref = ins
        n_rows = x_ref.shape[1] // GRID_W
        rot = []
        for t in range(2):
            by_row = jnp.concatenate(
                [jnp.broadcast_to(rrow_ref[t, r:r + 1, :], (GRID_W, HEAD_DIM)) for r in range(n_rows)], axis=0)
            rot.append(by_row + jnp.concatenate([rcol_ref[t]] * n_rows, axis=0))
    else:
        x_ref, mod_ref, gpre_ref, w_ref, pavg_ref, gn_ref = ins
    x = x_ref[0]
    shift = mod_ref[0, :, 0:D_MODEL]
    scale = mod_ref[0, :, D_MODEL:2 * D_MODEL]
    h = _rms(x) * gpre_ref[...]
    hb = (h * (1.0 + scale) + shift).astype(BF16)

    def proj(lo, hi):
        return _dot(hb, w_ref[:, lo:hi])

    def rope_store(z, dst):
        if not rope:
            dst[0] = z.astype(BF16)
            return
        cos, sin = rot
        for hd in range(HEADS):
            a = z[:, hd * HEAD_DIM:(hd + 1) * HEAD_DIM]
            r = a * cos + pltpu.roll(a, HEAD_DIM // 2, 1) * sin
            dst[0, :, hd * HEAD_DIM:(hd + 1) * HEAD_DIM] = r.astype(BF16)

    if kv_only:
        k_ref, v_ref = outs
    else:
        q_ref, k_ref, v_ref, sg_ref, f_ref, u_ref, vn_ref, gate_ref = outs
        rope_store(proj(0, RET_W), q_ref)
    rope_store(proj(RET_W, 2 * RET_W) * (HEAD_DIM ** -0.5), k_ref)
    v_ref[0] = proj(2 * RET_W, 3 * RET_W).astype(BF16)
    if kv_only:
        return
    g = proj(3 * RET_W, 4 * RET_W)
    sg_ref[0] = (g * _sigmoid(g)).astype(BF16)

    f = proj(COL_F, COL_U)
    for hf in range(2):
        fh = f[:, hf * HALF:(hf + 1) * HALF]
        if rope:
            for n1 in range(f.shape[0] // FFT_N2):
                for nj in range(FFT_N2 // FFT_STEP):
                    f_ref[0, hf, nj, n1 * FFT_STEP:(n1 + 1) * FFT_STEP, :] = (
                        fh[n1 * FFT_N2 + nj * FFT_STEP:n1 * FFT_N2 + (nj + 1) * FFT_STEP])
        else:
            f_ref[0, hf] = fh

    u_ref[0] = _gelu(proj(COL_U, COL_VS)).astype(BF16)
    vg = _gelu(proj(COL_VS, COL_GATE))
    sh, sl = _split(vg * vg)
    pavg = pavg_ref[...]
    ms = _dot(sh, pavg) + _dot(sl, pavg)
    vn_ref[0] = (vg * lax.rsqrt(ms + EPS) * gn_ref[...]).astype(BF16)

    for c in range(3):
        z = proj(COL_GATE + c * D_MODEL, COL_GATE + (c + 1) * D_MODEL)
        gate_ref[0, :, c * D_MODEL:(c + 1) * D_MODEL] = _sigmoid(z).astype(BF16)


def _mod_spec(layer, ctx):
    idx = (lambda b, i: (layer, 2, 0, 0)) if ctx else (lambda b, i: (layer, b, 0, 0))
    return pl.BlockSpec((None, 1, 1, 6 * D_MODEL), idx)


def _in_proj(xs, layer, mod, g_pre, w_in, pavg, g_norm, rope_tabs, *, tm, ctx, kv_only=False, cast=()):
    nb, rows, _ = xs.shape
    rope = rope_tabs is not None
    tok_spec = lambda w: pl.BlockSpec((1, tm, w), lambda b, i: (b, i, 0))
    in_specs = [
        tok_spec(D_MODEL),
        _mod_spec(layer, ctx),
        _layer_spec((1, D_MODEL), layer),
        _const_spec((D_MODEL, IN_W)),
        _const_spec((GMLP_W, GMLP_W)),
        _layer_spec((1, GMLP_W), layer),
    ]
    args = [xs, mod, g_pre, w_in, pavg, g_norm]
    if rope:
        in_specs += [pl.BlockSpec((2, tm // GRID_W, HEAD_DIM), lambda b, i: (0, i, 0)),
                     _const_spec((2, GRID_W, HEAD_DIM))]
        args += list(rope_tabs)
    c_in_specs, c_args, c_out_specs, c_out_shapes = _cast_plan(cast, (nb, rows // tm))
    in_specs += c_in_specs
    args += c_args
    bf = lambda w: jax.ShapeDtypeStruct((nb, rows, w), BF16)
    if kv_only:
        out_shapes = (bf(RET_W), bf(RET_W))
        out_specs = (tok_spec(RET_W), tok_spec(RET_W))
    else:
        if rope:
            n_nj = FFT_N2 // FFT_STEP
            f_shape = jax.ShapeDtypeStruct((nb, 2, n_nj, FFT_N1 * FFT_STEP, HALF), F32)
            f_spec = pl.BlockSpec((1, 2, n_nj, (tm // FFT_N2) * FFT_STEP, HALF), lambda b, i: (b, 0, 0, i, 0))
        else:
            f_shape = jax.ShapeDtypeStruct((nb, 2, rows, HALF), F32)
            f_spec = pl.BlockSpec((1, 2, tm, HALF), lambda b, i: (b, 0, i, 0))
        out_shapes = (
            bf(RET_W), bf(RET_W), bf(RET_W), bf(RET_W),
            f_shape,
            bf(GMLP_W), bf(GMLP_W),
            bf(3 * D_MODEL),
        )
        out_specs = (
            tok_spec(RET_W), tok_spec(RET_W), tok_spec(RET_W), tok_spec(RET_W),
            f_spec,
            tok_spec(GMLP_W), tok_spec(GMLP_W), tok_spec(3 * D_MODEL),
        )
    return pl.pallas_call(
        functools.partial(_in_proj_kernel, rope=rope, kv_only=kv_only, n_cast=len(cast)),
        grid=(nb, rows // tm),
        in_specs=in_specs,
        out_specs=tuple(out_specs) + tuple(c_out_specs),
        out_shape=tuple(out_shapes) + tuple(c_out_shapes),
        compiler_params=_params(2),
        name="in_proj_ctx" if ctx else "in_proj",
    )(*args)


def _ret_kernel(*refs, ctx_out, n_cast):
    n_in, n_out = (9, 2) if ctx_out else (7, 1)
    ins, refs = refs[:n_in], refs[n_in:]
    cast_src, refs = refs[:n_cast], refs[n_cast:]
    outs, refs = refs[:n_out], refs[n_out:]
    cast_dst, scratch = refs[:n_cast], refs[n_cast:]
    _cast_chunks(cast_src, cast_dst)
    (logit_ref, ql_ref, kl_ref, vl_ref, sgl_ref, kc_ref, vc_ref) = ins[:7]
    if ctx_out:
        qc_ref, sgc_ref = ins[7:]
        ol_ref, oc_ref = outs
    else:
        (ol_ref,) = outs
    mask_ref, wf_ref, wb_ref, qf_ref, qb_ref, df_ref, db_ref, sf_ref, sb_ref, sball_ref = scratch
    b = pl.program_id(0)
    phase = pl.program_id(1)
    j = pl.program_id(2)

    @pl.when((b == 0) & (phase == 0) & (j == 0))
    def _():
        x = logit_ref[...]
        lg = -(jnp.maximum(-x, 0.0) + jnp.log(1.0 + jnp.exp(-jnp.abs(x))))
        row = lax.broadcasted_iota(jnp.int32, (RCHUNK, RCHUNK), 0).astype(F32)
        col = lax.broadcasted_iota(jnp.int32, (RCHUNK, RCHUNK), 1).astype(F32)
        diff = row - col
        pos = row[:, :HEAD_DIM]
        for hd in range(HEADS):
            lf = lg[hd:hd + 1, :]
            lb = lg[HEADS + hd:HEADS + hd + 1, :]
            mask_ref[hd] = jnp.where(diff >= 0.0, jnp.exp(lf * jnp.maximum(diff, 0.0)),
                                     jnp.exp(lb * jnp.maximum(-diff, 0.0)))
            wf_ref[hd] = jnp.exp(lf * (RCHUNK - 1.0 - pos))
            wb_ref[hd] = jnp.exp(lb * pos)
            qf_ref[hd] = jnp.exp(lf * (pos + 1.0))
            qb_ref[hd] = jnp.exp(lb * (RCHUNK - pos))
            df_ref[hd] = jnp.exp(jnp.broadcast_to(lf, (HEAD_DIM, HEAD_DIM)) * float(RCHUNK))
            db_ref[hd] = jnp.exp(jnp.broadcast_to(lb, (HEAD_DIM, HEAD_DIM)) * float(RCHUNK))

    def chunk_updates(hd, k_ref, v_ref, n_chunks, w_ref):
        cols = slice(hd * HEAD_DIM, (hd + 1) * HEAD_DIM)
        us = []
        for c in range(n_chunks):
            rows = slice(c * RCHUNK, (c + 1) * RCHUNK)
            kw = (k_ref[0, rows, cols].astype(F32) * w_ref[hd]).astype(BF16)
            us.append(lax.dot_general(kw, v_ref[0, rows, cols], (((0,), (0,)), ((), ())),
                                      preferred_element_type=F32))
        return us

    def backward_chunks(k_ref, v_ref, n_chunks, first_chunk):
        for hd in range(HEADS):
            us = chunk_updates(hd, k_ref, v_ref, n_chunks, wb_ref)
            s = sb_ref[hd]
            for c in reversed(range(n_chunks)):
                sball_ref[first_chunk + c, hd] = s.astype(BF16)
                s = db_ref[hd] * s + us[c]
            sb_ref[hd] = s

    def forward_chunks(q_ref, k_ref, v_ref, sg_ref, o_ref, n_chunks, first_chunk):
        starts = []
        for hd in range(HEADS):
            us = chunk_updates(hd, k_ref, v_ref, n_chunks, wf_ref)
            st = [sf_ref[hd]]
            for c in range(n_chunks):
                st.append(df_ref[hd] * st[-1] + us[c])
            sf_ref[hd] = st[-1]
            starts.append(st)
        if o_ref is None:
            return
        for c in range(n_chunks):
            rows = slice(c * RCHUNK, (c + 1) * RCHUNK)
            for hd in range(HEADS):
                cols = slice(hd * HEAD_DIM, (hd + 1) * HEAD_DIM)
                qh = q_ref[0, rows, cols]
                sc = lax.dot_general(qh, k_ref[0, rows, cols], (((1,), (1,)), ((), ())),
                                     preferred_element_type=F32)
                qf = qh.astype(F32)
                lhs = jnp.concatenate([(sc * mask_ref[hd]).astype(BF16),
                                       (qf * qf_ref[hd]).astype(BF16),
                                       (qf * qb_ref[hd]).astype(BF16)], axis=1)
                rhs = jnp.concatenate([v_ref[0, rows, cols], starts[hd][c].astype(BF16),
                                       sball_ref[first_chunk + c, hd]], axis=0)
                o = _dot(lhs, rhs)
                o_ref[0, rows, cols] = (_rms(o) * sg_ref[0, rows, cols].astype(F32)).astype(BF16)

    @pl.when(phase == 0)
    def _():
        @pl.when(j == 0)
        def _():
            sb_ref[...] = jnp.zeros_like(sb_ref)
            backward_chunks(kc_ref, vc_ref, 1, 0)

        backward_chunks(kl_ref, vl_ref, RET_CPT, 1 + (RET_STEPS - 1 - j) * RET_CPT)

    @pl.when(phase == 1)
    def _():
        @pl.when(j == 0)
        def _():
            sf_ref[...] = jnp.zeros_like(sf_ref)
            if ctx_out:
                forward_chunks(qc_ref, kc_ref, vc_ref, sgc_ref, oc_ref, 1, 0)
            else:
                forward_chunks(None, kc_ref, vc_ref, None, None, 1, 0)

        forward_chunks(ql_ref, kl_ref, vl_ref, sgl_ref, ol_ref, RET_CPT, 1 + j * RET_CPT)


def _retention(layer, logits, lat, ctx, *, ctx_out, cast=()):
    nb = lat[0].shape[0]
    kv_idx = lambda b, p, j: (b, jnp.where(p == 0, RET_STEPS - 1 - j, j), 0)
    q_idx = lambda b, p, j: (b, jnp.where(p == 0, 0, j), 0)
    c_idx = lambda b, p, j: (b, 0, 0)
    lat_blk = (1, RET_TILE, RET_W)
    ctx_blk = (1, CTX_LEN, RET_W)
    in_specs = [
        _layer_spec((2 * HEADS, 1), layer),
        pl.BlockSpec(lat_blk, q_idx), pl.BlockSpec(lat_blk, kv_idx), pl.BlockSpec(lat_blk, kv_idx),
        pl.BlockSpec(lat_blk, q_idx),
        pl.BlockSpec(ctx_blk, c_idx), pl.BlockSpec(ctx_blk, c_idx),
    ]
    if ctx_out:
        qc, kc, vc, sgc = ctx
        args = [logits, *lat, kc, vc, qc, sgc]
        in_specs += [pl.BlockSpec(ctx_blk, c_idx), pl.BlockSpec(ctx_blk, c_idx)]
        out_specs = (pl.BlockSpec(lat_blk, q_idx), pl.BlockSpec(ctx_blk, c_idx))
        out_shape = (jax.ShapeDtypeStruct((nb, SEQ, RET_W), BF16), jax.ShapeDtypeStruct((nb, CTX_LEN, RET_W), BF16))
    else:
        kc, vc = ctx
        args = [logits, *lat, kc, vc]
        out_specs = (pl.BlockSpec(lat_blk, q_idx),)
        out_shape = (jax.ShapeDtypeStruct((nb, SEQ, RET_W), BF16),)
    grid = (nb, 2, RET_STEPS)
    c_in_specs, c_args, c_out_specs, c_out_shapes = _cast_plan(cast, grid)
    state = pltpu.VMEM((HEADS, HEAD_DIM, HEAD_DIM), F32)
    pos_tab = pltpu.VMEM((HEADS, RCHUNK, HEAD_DIM), F32)
    return pl.pallas_call(
        functools.partial(_ret_kernel, ctx_out=ctx_out, n_cast=len(cast)),
        grid=grid,
        in_specs=in_specs + c_in_specs,
        out_specs=out_specs + tuple(c_out_specs),
        out_shape=out_shape + tuple(c_out_shapes),
        scratch_shapes=[
            pltpu.VMEM((HEADS, RCHUNK, RCHUNK), F32),
            pos_tab, pos_tab, pos_tab, pos_tab,
            state, state,
            state, state,
            pltpu.VMEM((N_RCHUNKS, HEADS, HEAD_DIM, HEAD_DIM), BF16),
        ],
        compiler_params=_params(3),
        name="retention",
    )(*args, *c_args)


def _dft_cos_sin(n):
    idx = np.arange(n, dtype=np.float64)
    ang = 2.0 * np.pi * ((idx[:, None] * idx[None, :]) % n) / n
    return np.cos(ang), np.sin(ang)


def _channel_dft(scale):
    c, s = _dft_cos_sin(FNET_GROUP)
    eye = np.eye(FNET_W // FNET_GROUP)
    return np.concatenate([np.kron(eye, c), np.kron(eye, s)], axis=0) * scale


def _fft_constants():
    k1 = np.arange(FFT_N1, dtype=np.float64)[None, :, None]
    n1 = np.arange(FFT_N1, dtype=np.float64)[None, None, :]
    n2 = np.arange(FFT_N2, dtype=np.float64)[:, None, None]
    ang = 2.0 * np.pi * ((k1 * (FFT_N2 * n1 + n2)) % SEQ) / SEQ
    m1 = np.concatenate([np.cos(ang), -np.sin(ang)], axis=1)
    c, s = _dft_cos_sin(FFT_N2)
    m3 = np.block([[c, s], [-s, c]])
    return (_np_split_cat(m1, -1), jnp.asarray(m3, F32),
            jnp.asarray(_channel_dft((SEQ * FNET_GROUP) ** -0.5), F32))


def _ctx_fft_constants():
    c, s = _dft_cos_sin(CTX_LEN)
    return (jnp.asarray(np.concatenate([c, -s], axis=0), F32),
            jnp.asarray(_channel_dft((CTX_LEN * FNET_GROUP) ** -0.5), F32))


def _np_split_cat(a64, axis):
    hi, lo = _np_split(a64)
    return jnp.asarray(np.concatenate([hi, lo], axis=axis))


def _twice(x, axis):
    return jnp.concatenate([x, x], axis=axis)


def _fft1_kernel(x_ref, m_ref, o_ref):
    for jj in range(FFT_STEP):
        x = jnp.concatenate([x_ref[hf, pl.ds(jj, FFT_N1, stride=FFT_STEP), :] for hf in range(2)],
                            axis=1).astype(BF16)
        a = _dot(m_ref[jj], _twice(x, 0))
        for hf in range(2):
            for ri in range(2):
                for kb in range(FFT_N1 // FFT_STEP):
                    r0 = ri * FFT_N1 + kb * FFT_STEP
                    o_ref[hf, ri, kb, jj * FFT_STEP:(jj + 1) * FFT_STEP, :] = (
                        a[r0:r0 + FFT_STEP, hf * HALF:(hf + 1) * HALF])


def _fft3_kernel(a_ref, w_ref, cd_ref, o_ref):
    w = w_ref[...].astype(BF16)
    zs = []
    for jj in range(FFT_STEP):
        rhs = jnp.concatenate(
            [jnp.concatenate([a_ref[hf, ri, pl.ds(jj, FFT_N2, stride=FFT_STEP), :] for ri in range(2)], axis=0)
             for hf in range(2)], axis=1).astype(BF16)
        y = _dot(w, rhs)
        zs.append(jnp.concatenate([y[:FFT_N2], y[FFT_N2:]], axis=1).astype(BF16))
    z = jnp.concatenate(zs, axis=0)
    four = _dot(z, cd_ref[...].astype(BF16))
    for jj in range(FFT_STEP):
        for kt in range(FFT_N2 // MIX_K2):
            for hf in range(2):
                r0 = jj * FFT_N2 + kt * MIX_K2
                o_ref[hf, kt, jj * MIX_K2:(jj + 1) * MIX_K2, :] = four[r0:r0 + MIX_K2, hf * HALF:(hf + 1) * HALF]


def _fourier_latent(f_blk):
    nb = f_blk.shape[0]
    m1, m3, cd = _fft_constants()
    n_kb = FFT_N1 // FFT_STEP
    a = pl.pallas_call(
        _fft1_kernel,
        grid=(nb, FFT_N2 // FFT_STEP),
        in_specs=[
            pl.BlockSpec((None, 2, None, FFT_N1 * FFT_STEP, HALF), lambda b, j: (b, 0, j, 0, 0)),
            pl.BlockSpec((FFT_STEP, 2 * FFT_N1, 2 * FFT_N1), lambda b, j: (j, 0, 0)),
        ],
        out_specs=pl.BlockSpec((None, 2, 2, n_kb, FFT_STEP * FFT_STEP, HALF), lambda b, j: (b, 0, 0, 0, j, 0)),
        out_shape=jax.ShapeDtypeStruct((nb, 2, 2, n_kb, FFT_N2 * FFT_STEP, HALF), F32),
        compiler_params=_params(2),
        name="fourier_stage1",
    )(f_blk, m1)
    return pl.pallas_call(
        _fft3_kernel,
        grid=(nb, n_kb),
        in_specs=[
            pl.BlockSpec((None, 2, 2, None, FFT_N2 * FFT_STEP, HALF), lambda b, j: (b, 0, 0, j, 0, 0)),
            _const_spec((2 * FFT_N2, 2 * FFT_N2)), _const_spec((2 * FNET_W, FNET_W)),
        ],
        out_specs=pl.BlockSpec((None, 2, FFT_N2 // MIX_K2, FFT_STEP * MIX_K2, HALF), lambda b, j: (b, 0, 0, j, 0)),
        out_shape=jax.ShapeDtypeStruct((nb, 2, FFT_N2 // MIX_K2, FFT_N1 * MIX_K2, HALF), F32),
        compiler_params=_params(2),
        name="fourier_stage2",
    )(a, m3, cd)


def _fft_ctx_kernel(x_ref, w_ref, cd_ref, o_ref):
    x = jnp.concatenate([x_ref[hf] for hf in range(2)], axis=1).astype(BF16)
    z = _dot(w_ref[...].astype(BF16), x)
    z = jnp.concatenate([z[:CTX_LEN], z[CTX_LEN:]], axis=1).astype(BF16)
    four = _dot(z, cd_ref[...].astype(BF16))
    for hf in range(2):
        o_ref[hf] = four[:, hf * HALF:(hf + 1) * HALF]


def _fourier_ctx(f_ctx):
    nb = f_ctx.shape[0]
    w, cd = _ctx_fft_constants()
    return pl.pallas_call(
        _fft_ctx_kernel,
        grid=(nb,),
        in_specs=[
            pl.BlockSpec((None, 2, CTX_LEN, HALF), lambda b: (b, 0, 0, 0)),
            _const_spec((2 * CTX_LEN, CTX_LEN)), _const_spec((2 * FNET_W, FNET_W)),
        ],
        out_specs=pl.BlockSpec((None, 2, CTX_LEN, HALF), lambda b: (b, 0, 0, 0)),
        out_shape=jax.ShapeDtypeStruct((nb, 2, CTX_LEN, HALF), F32),
        compiler_params=_params(1),
        name="fourier_ctx",
    )(f_ctx, w, cd)


def _mix_mlp_kernel(*refs, tm, ctx, n_cast):
    (x_ref, ret_ref, four_ref, u_ref, vn_ref, gate_ref, mod_ref, ws_ref, bs_ref,
     wa_ref, wb_ref, wc_ref, wo_ref, gpm_ref, gprm_ref, gpom_ref, wup_ref, wdn_ref) = refs[:18]
    o_ref = refs[18 + n_cast]
    _cast_chunks(refs[18:18 + n_cast], refs[19 + n_cast:])
    group = lax.broadcasted_iota(jnp.int32, (GMLP_CHUNK, GMLP_W), 1) // GMLP_GROUP
    parts = []
    for cc in range(tm // GMLP_CHUNK):
        rows = slice(cc * GMLP_CHUNK, (cc + 1) * GMLP_CHUNK)
        vn = vn_ref[0, rows, :]
        s = bs_ref[...]
        for g in range(GMLP_W // GMLP_GROUP):
            s = s + jnp.where(group == g, _dot(ws_ref[g], vn), 0.0)
        parts.append((u_ref[0, rows, :].astype(F32) * s).astype(BF16))
    sgu = jnp.concatenate(parts, axis=0)

    if ctx:
        four = jnp.concatenate([four_ref[0, hf] for hf in range(2)], axis=1).astype(BF16)
    else:
        nk2 = tm // FFT_N1
        four = jnp.concatenate(
            [jnp.concatenate([four_ref[0, hf, pl.ds(k2, FFT_N1, stride=nk2), :] for k2 in range(nk2)], axis=0)
             for hf in range(2)], axis=1).astype(BF16)
    ga = gate_ref[0, :, 0:D_MODEL].astype(F32)
    gb = gate_ref[0, :, D_MODEL:2 * D_MODEL].astype(F32)
    gc = gate_ref[0, :, 2 * D_MODEL:3 * D_MODEL].astype(F32)
    merged = (ga * _dot(ret_ref[0], wa_ref[...]) + gb * _dot(four, wb_ref[...])
              + gc * _dot(sgu, wc_ref[...])).astype(BF16)

    gate1 = mod_ref[0, :, 2 * D_MODEL:3 * D_MODEL]
    shift2 = mod_ref[0, :, 3 * D_MODEL:4 * D_MODEL]
    scale2 = mod_ref[0, :, 4 * D_MODEL:5 * D_MODEL]
    gate2 = mod_ref[0, :, 5 * D_MODEL:6 * D_MODEL]

    halves = [slice(r * (tm // 2), (r + 1) * (tm // 2)) for r in range(2)]
    x1, h2 = [], []
    for rows in halves:
        y = _dot(merged[rows], wo_ref[...])
        x1.append(x_ref[0, rows, :] + gate1 * (_rms(y) * gpm_ref[...]))
        h2.append(((_rms(x1[-1]) * gprm_ref[...]) * (1.0 + scale2) + shift2).astype(BF16))
    h2 = jnp.concatenate(h2, axis=0)

    n_ff = D_FF // FF_CHUNK
    m = None
    for c in range(n_ff - 1):
        cols = slice(c * FF_CHUNK, (c + 1) * FF_CHUNK)
        a = jnp.maximum(_dot(h2, wup_ref[:, cols]), 0.0)
        part = _dot((a * a).astype(BF16), wdn_ref[cols, :])
        m = part if m is None else m + part
    cols = slice((n_ff - 1) * FF_CHUNK, n_ff * FF_CHUNK)
    a = jnp.maximum(_dot(h2, wup_ref[:, cols]), 0.0)
    a = (a * a).astype(BF16)
    for r, rows in enumerate(halves):
        mr = m[rows] + _dot(a[rows], wdn_ref[cols, :])
        o_ref[0, rows, :] = x1[r] + gate2 * (_rms(mr) * gpom_ref[...])


def _mix_mlp(xs, ret, four, ug, vn, gates, layer, mod, ws, bs_tab, wa, wb, wc, wo,
             g_post_mix, g_pre_mlp, g_post_mlp, w_up, w_down, *, tm, ctx, cast=()):
    nb, rows, _ = xs.shape
    tok_spec = lambda w: pl.BlockSpec((1, tm, w), lambda b, i: (b, i, 0))
    lspec = lambda *shape: _layer_spec(shape, layer)
    if ctx:
        four_spec = pl.BlockSpec((1, 2, tm, HALF), lambda b, i: (b, 0, i, 0))
    else:
        four_spec = pl.BlockSpec((1, 2, None, tm, HALF), lambda b, i: (b, 0, i, 0, 0))
    c_in_specs, c_args, c_out_specs, c_out_shapes = _cast_plan(cast, (nb, rows // tm))
    outs = pl.pallas_call(
        functools.partial(_mix_mlp_kernel, tm=tm, ctx=ctx, n_cast=len(cast)),
        grid=(nb, rows // tm),
        in_specs=[
            tok_spec(D_MODEL), tok_spec(RET_W),
            four_spec,
            tok_spec(GMLP_W), tok_spec(GMLP_W), tok_spec(3 * D_MODEL),
            _mod_spec(layer, ctx),
            lspec(GMLP_W // GMLP_GROUP, GMLP_CHUNK, GMLP_CHUNK),
            lspec(GMLP_CHUNK, GMLP_W),
            _const_spec((RET_W, D_MODEL)), _const_spec((FNET_W, D_MODEL)), _const_spec((GMLP_W, D_MODEL)),
            _const_spec((D_MODEL, D_MODEL)),
            lspec(1, D_MODEL), lspec(1, D_MODEL), lspec(1, D_MODEL),
            _const_spec((D_MODEL, D_FF)), _const_spec((D_FF, D_MODEL)),
        ] + c_in_specs,
        out_specs=(tok_spec(D_MODEL),) + tuple(c_out_specs),
        out_shape=(jax.ShapeDtypeStruct((nb, rows, D_MODEL), F32),) + tuple(c_out_shapes),
        compiler_params=_params(2),
        name="mix_mlp_ctx" if ctx else "mix_mlp",
    )(xs, ret, four, ug, vn, gates, mod, ws, bs_tab, wa, wb, wc, wo,
      g_post_mix, g_pre_mlp, g_post_mlp, w_up, w_down, *c_args)
    return outs if cast else outs[0]


def _rope_tables():
    rows = SEQ // GRID_W
    freqs = ROPE_BASE ** (-jnp.arange(ROPE_FREQS, dtype=F32) / ROPE_FREQS)
    ang_r = jnp.arange(rows, dtype=F32)[:, None] * freqs
    ang_c = jnp.arange(GRID_W, dtype=F32)[:, None] * freqs

    def lanes(a, first):
        z = jnp.zeros_like(a)
        return jnp.concatenate([a, z] if first else [z, a], axis=-1)

    def pair(ang, first):
        cos, sin = lanes(jnp.cos(ang), first), lanes(jnp.sin(ang), first)
        return jnp.stack([jnp.concatenate([cos, cos], axis=-1), jnp.concatenate([-sin, sin], axis=-1)])

    return pair(ang_r, True), pair(ang_c, False)


def kernel(x, c, ctx, c_ctx, w_mod, b_mod, g_pre_mix, g_post_mix, g_pre_mlp, g_post_mlp, w_in,
           ret_decay_logit, sgu_w_s, sgu_b_s, sgu_norm, w_branch_a, w_branch_b, w_branch_c, w_out,
           w_up, w_down):
    nb = x.shape[0]
    depth = w_mod.shape[0]
    assert x.shape == (nb, SEQ, D_MODEL) and ctx.shape == (nb, CTX_LEN, D_MODEL) and nb == 2

    cond_rows = jnp.concatenate([c, c_ctx[None, :], jnp.zeros((8 - nb - 1, D_MODEL), F32)], axis=0)
    mod = _modulation(cond_rows, w_mod, b_mod).reshape(depth, 8, 1, 6 * D_MODEL)
    rope_tabs = _rope_tables()
    pavg = jnp.asarray(np.kron(np.eye(GMLP_W // GMLP_GROUP), np.full((GMLP_GROUP, GMLP_GROUP), 1.0 / GMLP_GROUP)), BF16)

    vec = lambda p: p[:, None, :]
    g_pre, g_norm = vec(g_pre_mix), vec(sgu_norm)
    logits = ret_decay_logit.reshape(depth, 2 * HEADS, 1)
    ws = sgu_w_s.astype(BF16)
    bs_tab = jnp.repeat(jnp.swapaxes(sgu_b_s, 1, 2), GMLP_GROUP, axis=2)
    gains = (vec(g_post_mix), vec(g_pre_mlp), vec(g_post_mlp))
    w_in_l = w_in[0].astype(BF16)
    for l in range(depth):
        last = l == depth - 1
        in_args = (l, mod, g_pre, w_in_l, pavg, g_norm)
        *proj, wa, wb, wc, wo = _in_proj(x, *in_args, rope_tabs, tm=TM_IN, ctx=False,
                                         cast=[(w, l) for w in (w_branch_a, w_branch_b, w_branch_c, w_out)])
        q, k, v, sg, f_lat, ug, vn, gates = proj
        mlp_cast = [(w_up, l), (w_down, l)]
        if last:
            kc, vc = _in_proj(ctx, *in_args, None, tm=CTX_LEN, ctx=True, kv_only=True)
            ret, wup, wdn = _retention(l, logits, (q, k, v, sg), (kc, vc), ctx_out=False, cast=mlp_cast)
        else:
            qc, kc, vc, sgc, f_ctx, ugc, vnc, gatesc = _in_proj(ctx, *in_args, None, tm=CTX_LEN, ctx=True)
            ret, retc, wup, wdn = _retention(l, logits, (q, k, v, sg), (qc, kc, vc, sgc), ctx_out=True,
                                             cast=mlp_cast)
        mix_w = (l, mod, ws, bs_tab, wa, wb, wc, wo, *gains, wup, wdn)
        four = _fourier_latent(f_lat)
        if last:
            x = _mix_mlp(x, ret, four, ug, vn, gates, *mix_w, tm=TM_LAT, ctx=False)
        else:
            x, w_in_l = _mix_mlp(x, ret, four, ug, vn, gates, *mix_w, tm=TM_LAT, ctx=False, cast=[(w_in, l + 1)])
            ctx = _mix_mlp(ctx, retc, _fourier_ctx(f_ctx), ugc, vnc, gatesc, *mix_w, tm=CTX_LEN, ctx=True)
    return x
```

```python
import functools

import numpy as np
import jax
import jax.numpy as jnp
from jax import lax
from jax.experimental import pallas as pl
from jax.experimental.pallas import tpu as pltpu

F32 = jnp.float32
BF16 = jnp.bfloat16

D_MODEL = 1024
SEQ = 8192
CTX_LEN = 256
GRID_W = 64
RET_W = 512
HEADS = 4
HEAD_DIM = 128
ROPE_BASE = 10000.0
ROPE_FREQS = HEAD_DIM // 4
FNET_W = 256
FNET_GROUP = 64
GMLP_W = 256
GMLP_GROUP = 64
GMLP_CHUNK = 128
D_FF = 4 * D_MODEL
EPS = 1e-6
IN_W = 4 * RET_W + FNET_W + 2 * GMLP_W + 3 * D_MODEL
COL_F = 4 * RET_W
COL_U = COL_F + FNET_W
COL_VS = COL_U + GMLP_W
COL_GATE = COL_VS + GMLP_W

TM_LAT = 512
TM_IN = 1024
IN_SUB = TM_IN
RCHUNK = 256
RET_TILE = 2048
RET_STEPS = SEQ // RET_TILE
RET_CPT = RET_TILE // RCHUNK
RET_AHEAD = 4
N_RCHUNKS = 1 + SEQ // RCHUNK
FF_CHUNK = 1024
FFT_N1 = 64
FFT_N2 = 128
HALF = 128
FFT_STEP = 16
MIX_K2 = TM_LAT // FFT_N1

_VMEM_LIMIT = 56 * 1024 * 1024


def _dot(a, b):
    return jnp.dot(a, b, preferred_element_type=F32)


def _split(x):
    hi = x.astype(BF16)
    lo = (x - hi.astype(F32)).astype(BF16)
    return hi, lo


def _np_split(a64):
    hi = np.asarray(a64, np.float32).astype(BF16)
    lo = (np.asarray(a64, np.float32) - hi.astype(np.float32)).astype(BF16)
    return hi, lo


def _rms(x):
    return x * lax.rsqrt(jnp.mean(x * x, axis=-1, keepdims=True) + EPS)


def _gelu(x):
    return x * (0.5 * (1.0 + jnp.tanh(0.7978845608028654 * (x + 0.044715 * (x * x * x)))))


def _sigmoid(x):
    return 1.0 / (1.0 + jnp.exp(-x))


def _const_spec(shape, nargs=None):
    zeros = (0,) * len(shape)
    return pl.BlockSpec(shape, lambda *_: zeros, pipeline_mode=pl.Buffered(1))


def _layer_spec(shape, layer):
    idx = (layer,) + (0,) * len(shape)
    return pl.BlockSpec((None,) + tuple(shape), lambda *_: idx, pipeline_mode=pl.Buffered(1))


def _params(n_axes):
    return pltpu.CompilerParams(dimension_semantics=("arbitrary",) * n_axes, vmem_limit_bytes=_VMEM_LIMIT)


def _mod_kernel(a_ref, w_ref, b_ref, o_ref):
    a = a_ref[...]
    a = a * _sigmoid(a)
    ah, al = _split(a)
    w = w_ref[0].astype(BF16)
    o_ref[0] = _dot(ah, w) + _dot(al, w) + b_ref[0]


def _modulation(cond_rows, w_mod, b_mod):
    depth = w_mod.shape[0]
    tn = 1536
    return pl.pallas_call(
        _mod_kernel,
        grid=(depth, (6 * D_MODEL) // tn),
        in_specs=[
            pl.BlockSpec((8, D_MODEL), lambda l, j: (0, 0)),
            pl.BlockSpec((1, D_MODEL, tn), lambda l, j: (l, 0, j)),
            pl.BlockSpec((1, 1, tn), lambda l, j: (l, 0, j)),
        ],
        out_specs=pl.BlockSpec((1, 8, tn), lambda l, j: (l, 0, j)),
        out_shape=jax.ShapeDtypeStruct((depth, 8, 6 * D_MODEL), F32),
        compiler_params=_params(2),
        name="modulation",
    )(cond_rows, w_mod, b_mod.reshape(depth, 1, 6 * D_MODEL))


def _cast_plan(cast, grid):
    steps = int(np.prod(grid))
    in_specs, args, out_specs, out_shapes = [], [], [], []

    def step(*idx):
        t = idx[0]
        for n, i in zip(grid[1:], idx[1:]):
            t = t * n + i
        return t

    for stack, layer in cast:
        _, rows, cols = stack.shape
        n_chunks = min(steps, rows // 16)
        assert rows % n_chunks == 0 and steps % n_chunks == 0
        per = steps // n_chunks
        in_specs.append(pl.BlockSpec((None, rows // n_chunks, cols),
                                     lambda *idx, layer=layer, per=per: (layer, step(*idx) // per, 0)))
        out_specs.append(pl.BlockSpec((rows // n_chunks, cols), lambda *idx, per=per: (step(*idx) // per, 0)))
        out_shapes.append(jax.ShapeDtypeStruct((rows, cols), BF16))
        args.append(stack)
    return in_specs, args, out_specs, out_shapes


def _cast_chunks(src_refs, dst_refs):
    for src, dst in zip(src_refs, dst_refs, strict=True):
        dst[...] = src[...].astype(BF16)


def _in_proj_kernel(*refs, rope, kv_only, n_cast):
    n_in = 8 if rope else 6
    ins, cast_src = refs[:n_in], refs[n_in:n_in + n_cast]
    outs, cast_dst = refs[n_in + n_cast:len(refs) - n_cast], refs[len(refs) - n_cast:]
    _cast_chunks(cast_src, cast_dst)
    if rope:
        x_ref, mod_ref, gpre_ref, w_ref, pavg_ref, gn_ref, rrow_ref, rcol_ref = ins
        n_rows = x_ref.shape[1] // GRID_W
        rot = []
        for t in range(2):
            by_row = jnp.concatenate(
                [jnp.broadcast_to(rrow_ref[t, r:r + 1, :], (GRID_W, HEAD_DIM)) for r in range(n_rows)], axis=0)
            rot.append(by_row + jnp.concatenate([rcol_ref[t]] * n_rows, axis=0))
    else:
        x_ref, mod_ref, gpre_ref, w_ref, pavg_ref, gn_ref = ins
    shift = mod_ref[0, :, 0:D_MODEL]
    scale = mod_ref[0, :, D_MODEL:2 * D_MODEL]
    tm = x_ref.shape[1]
    sub = min(tm, IN_SUB)
    for r0 in range(0, tm, sub):
        rows = slice(r0, r0 + sub)
        h = _rms(x_ref[0, rows, :]) * gpre_ref[...]
        hb = (h * (1.0 + scale) + shift).astype(BF16)

        def proj(lo, hi, hb=hb):
            return _dot(hb, w_ref[:, lo:hi])

        def rope_store(z, dst):
            if not rope:
                dst[0, rows, :] = z.astype(BF16)
                return
            cos, sin = rot[0][rows], rot[1][rows]
            for hd in range(HEADS):
                a = z[:, hd * HEAD_DIM:(hd + 1) * HEAD_DIM]
                r = a * cos + pltpu.roll(a, HEAD_DIM // 2, 1) * sin
                dst[0, rows, hd * HEAD_DIM:(hd + 1) * HEAD_DIM] = r.astype(BF16)

        if kv_only:
            k_ref, v_ref = outs
        else:
            q_ref, k_ref, v_ref, sg_ref, f_ref, u_ref, vn_ref, gate_ref = outs
            rope_store(proj(0, RET_W), q_ref)
        rope_store(proj(RET_W, 2 * RET_W) * (HEAD_DIM ** -0.5), k_ref)
        v_ref[0, rows, :] = proj(2 * RET_W, 3 * RET_W).astype(BF16)
        if kv_only:
            continue
        g = proj(3 * RET_W, 4 * RET_W)
        sg_ref[0, rows, :] = (g * _sigmoid(g)).astype(BF16)

        f = proj(COL_F, COL_U)
        for hf in range(2):
            fh = f[:, hf * HALF:(hf + 1) * HALF]
            if rope:
                for n1 in range(sub // FFT_N2):
                    row1 = (r0 // FFT_N2 + n1) * FFT_STEP
                    for nj in range(FFT_N2 // FFT_STEP):
                        f_ref[0, hf, nj, row1:row1 + FFT_STEP, :] = (
                            fh[n1 * FFT_N2 + nj * FFT_STEP:n1 * FFT_N2 + (nj + 1) * FFT_STEP])
            else:
                f_ref[0, hf, rows, :] = fh

        u_ref[0, rows, :] = _gelu(proj(COL_U, COL_VS)).astype(BF16)
        vg = _gelu(proj(COL_VS, COL_GATE))
        sh, sl = _split(vg * vg)
        pavg = pavg_ref[...]
        ms = _dot(sh, pavg) + _dot(sl, pavg)
        vn_ref[0, rows, :] = (vg * lax.rsqrt(ms + EPS) * gn_ref[...]).astype(BF16)

        for c in range(3):
            z = proj(COL_GATE + c * D_MODEL, COL_GATE + (c + 1) * D_MODEL)
            gate_ref[0, rows, c * D_MODEL:(c + 1) * D_MODEL] = _sigmoid(z).astype(BF16)


def _mod_spec(layer, ctx):
    idx = (lambda b, i: (layer, 2, 0, 0)) if ctx else (lambda b, i: (layer, b, 0, 0))
    return pl.BlockSpec((None, 1, 1, 6 * D_MODEL), idx)


def _in_proj(xs, layer, mod, g_pre, w_in, pavg, g_norm, rope_tabs, *, tm, ctx, kv_only=False, cast=()):
    nb, rows, _ = xs.shape
    rope = rope_tabs is not None
    tok_spec = lambda w: pl.BlockSpec((1, tm, w), lambda b, i: (b, i, 0))
    in_specs = [
        tok_spec(D_MODEL),
        _mod_spec(layer, ctx),
        _layer_spec((1, D_MODEL), layer),
        _const_spec((D_MODEL, IN_W)),
        _const_spec((GMLP_W, GMLP_W)),
        _layer_spec((1, GMLP_W), layer),
    ]
    args = [xs, mod, g_pre, w_in, pavg, g_norm]
    if rope:
        in_specs += [pl.BlockSpec((2, tm // GRID_W, HEAD_DIM), lambda b, i: (0, i, 0)),
                     _const_spec((2, GRID_W, HEAD_DIM))]
        args += list(rope_tabs)
    c_in_specs, c_args, c_out_specs, c_out_shapes = _cast_plan(cast, (nb, rows // tm))
    in_specs += c_in_specs
    args += c_args
    bf = lambda w: jax.ShapeDtypeStruct((nb, rows, w), BF16)
    if kv_only:
        out_shapes = (bf(RET_W), bf(RET_W))
        out_specs = (tok_spec(RET_W), tok_spec(RET_W))
    else:
        if rope:
            n_nj = FFT_N2 // FFT_STEP
            f_shape = jax.ShapeDtypeStruct((nb, 2, n_nj, FFT_N1 * FFT_STEP, HALF), F32)
            f_spec = pl.BlockSpec((1, 2, n_nj, (tm // FFT_N2) * FFT_STEP, HALF), lambda b, i: (b, 0, 0, i, 0))
        else:
            f_shape = jax.ShapeDtypeStruct((nb, 2, rows, HALF), F32)
            f_spec = pl.BlockSpec((1, 2, tm, HALF), lambda b, i: (b, 0, i, 0))
        out_shapes = (
            bf(RET_W), bf(RET_W), bf(RET_W), bf(RET_W),
            f_shape,
            bf(GMLP_W), bf(GMLP_W),
            bf(3 * D_MODEL),
        )
        out_specs = (
            tok_spec(RET_W), tok_spec(RET_W), tok_spec(RET_W), tok_spec(RET_W),
            f_spec,
            tok_spec(GMLP_W), tok_spec(GMLP_W), tok_spec(3 * D_MODEL),
        )
    return pl.pallas_call(
        functools.partial(_in_proj_kernel, rope=rope, kv_only=kv_only, n_cast=len(cast)),
        grid=(nb, rows // tm),
        in_specs=in_specs,
        out_specs=tuple(out_specs) + tuple(c_out_specs),
        out_shape=tuple(out_shapes) + tuple(c_out_shapes),
        compiler_params=_params(2),
        name="in_proj_ctx" if ctx else "in_proj",
    )(*args)


def _ret_kernel(*refs, ctx_out, n_cast):
    n_in, n_out = (9, 2) if ctx_out else (7, 1)
    ins, refs = refs[:n_in], refs[n_in:]
    cast_src, refs = refs[:n_cast], refs[n_cast:]
    outs, refs = refs[:n_out], refs[n_out:]
    cast_dst, scratch = refs[:n_cast], refs[n_cast:]
    _cast_chunks(cast_src, cast_dst)
    (logit_ref, ql_ref, kl_ref, vl_ref, sgl_ref, kc_ref, vc_ref) = ins[:7]
    if ctx_out:
        qc_ref, sgc_ref = ins[7:]
        ol_ref, oc_ref = outs
    else:
        (ol_ref,) = outs
    mask_ref, wf_ref, wb_ref, qf_ref, qb_ref, df_ref, db_ref, sf_ref, sb_ref, sball_ref = scratch
    b = pl.program_id(0)
    phase = pl.program_id(1)
    j = pl.program_id(2)

    @pl.when((b == 0) & (phase == 0) & (j == 0))
    def _():
        x = logit_ref[...]
        lg = -(jnp.maximum(-x, 0.0) + jnp.log(1.0 + jnp.exp(-jnp.abs(x))))
        row = lax.broadcasted_iota(jnp.int32, (RCHUNK, RCHUNK), 0).astype(F32)
        col = lax.broadcasted_iota(jnp.int32, (RCHUNK, RCHUNK), 1).astype(F32)
        diff = row - col
        pos = row[:, :HEAD_DIM]
        for hd in range(HEADS):
            lf = lg[hd:hd + 1, :]
            lb = lg[HEADS + hd:HEADS + hd + 1, :]
            mask_ref[hd] = jnp.where(diff >= 0.0, jnp.exp(lf * jnp.maximum(diff, 0.0)),
                                     jnp.exp(lb * jnp.maximum(-diff, 0.0)))
            wf_ref[hd] = jnp.exp(lf * (RCHUNK - 1.0 - pos))
            wb_ref[hd] = jnp.exp(lb * pos)
            qf_ref[hd] = jnp.exp(lf * (pos + 1.0))
            qb_ref[hd] = jnp.exp(lb * (RCHUNK - pos))
            df_ref[hd] = jnp.exp(jnp.broadcast_to(lf, (HEAD_DIM, HEAD_DIM)) * float(RCHUNK))
            db_ref[hd] = jnp.exp(jnp.broadcast_to(lb, (HEAD_DIM, HEAD_DIM)) * float(RCHUNK))

    def chunk_updates(hd, k_ref, v_ref, n_chunks, w_ref):
        cols = slice(hd * HEAD_DIM, (hd + 1) * HEAD_DIM)
        us = []
        for c in range(n_chunks):
            rows = slice(c * RCHUNK, (c + 1) * RCHUNK)
            kw = (k_ref[0, rows, cols].astype(F32) * w_ref[hd]).astype(BF16)
            us.append(lax.dot_general(kw, v_ref[0, rows, cols], (((0,), (0,)), ((), ())),
                                      preferred_element_type=F32))
        return us

    def backward_chunks(k_ref, v_ref, n_chunks, first_chunk):
        for hd in range(HEADS):
            us = chunk_updates(hd, k_ref, v_ref, n_chunks, wb_ref)
            s = sb_ref[hd]
            for c in reversed(range(n_chunks)):
                sball_ref[first_chunk + c, hd] = s.astype(BF16)
                s = db_ref[hd] * s + us[c]
            sb_ref[hd] = s

    def forward_chunks(q_ref, k_ref, v_ref, sg_ref, o_ref, n_chunks, first_chunk):
        starts = []
        for hd in range(HEADS):
            us = chunk_updates(hd, k_ref, v_ref, n_chunks, wf_ref)
            st = [sf_ref[hd]]
            for c in range(n_chunks):
                st.append(df_ref[hd] * st[-1] + us[c])
            sf_ref[hd] = st[-1]
            starts.append(st)
        if o_ref is None:
            return
        units = [(c, hd) for c in range(n_chunks) for hd in range(HEADS)]

        def scores(c, hd):
            rows = slice(c * RCHUNK, (c + 1) * RCHUNK)
            cols = slice(hd * HEAD_DIM, (hd + 1) * HEAD_DIM)
            return lax.dot_general(q_ref[0, rows, cols], k_ref[0, rows, cols], (((1,), (1,)), ((), ())),
                                   preferred_element_type=F32)

        ahead = min(RET_AHEAD, len(units))
        pending = [scores(*u) for u in units[:ahead]]
        for n, (c, hd) in enumerate(units):
            if n + ahead < len(units):
                pending.append(scores(*units[n + ahead]))
            sc = pending.pop(0)
            rows = slice(c * RCHUNK, (c + 1) * RCHUNK)
            cols = slice(hd * HEAD_DIM, (hd + 1) * HEAD_DIM)
            qf = q_ref[0, rows, cols].astype(F32)
            lhs = jnp.concatenate([(sc * mask_ref[hd]).astype(BF16),
                                   (qf * qf_ref[hd]).astype(BF16),
                                   (qf * qb_ref[hd]).astype(BF16)], axis=1)
            rhs = jnp.concatenate([v_ref[0, rows, cols], starts[hd][c].astype(BF16),
                                   sball_ref[first_chunk + c, hd]], axis=0)
            o = _dot(lhs, rhs)
            o_ref[0, rows, cols] = (_rms(o) * sg_ref[0, rows, cols].astype(F32)).astype(BF16)

    @pl.when(phase == 0)
    def _():
        @pl.when(j == 0)
        def _():
            sb_ref[...] = jnp.zeros_like(sb_ref)
            backward_chunks(kc_ref, vc_ref, 1, 0)

        backward_chunks(kl_ref, vl_ref, RET_CPT, 1 + (RET_STEPS - 1 - j) * RET_CPT)

    @pl.when(phase == 1)
    def _():
        @pl.when(j == 0)
        def _():
            sf_ref[...] = jnp.zeros_like(sf_ref)
            if ctx_out:
                forward_chunks(qc_ref, kc_ref, vc_ref, sgc_ref, oc_ref, 1, 0)
            else:
                forward_chunks(None, kc_ref, vc_ref, None, None, 1, 0)

        forward_chunks(ql_ref, kl_ref, vl_ref, sgl_ref, ol_ref, RET_CPT, 1 + j * RET_CPT)


def _retention(layer, logits, lat, ctx, *, ctx_out, cast=()):
    nb = lat[0].shape[0]
    kv_idx = lambda b, p, j: (b, jnp.where(p == 0, RET_STEPS - 1 - j, j), 0)
    q_idx = lambda b, p, j: (b, jnp.where(p == 0, 0, j), 0)
    c_idx = lambda b, p, j: (b, 0, 0)
    lat_blk = (1, RET_TILE, RET_W)
    ctx_blk = (1, CTX_LEN, RET_W)
    in_specs = [
        _layer_spec((2 * HEADS, 1), layer),
        pl.BlockSpec(lat_blk, q_idx), pl.BlockSpec(lat_blk, kv_idx), pl.BlockSpec(lat_blk, kv_idx),
        pl.BlockSpec(lat_blk, q_idx),
        pl.BlockSpec(ctx_blk, c_idx), pl.BlockSpec(ctx_blk, c_idx),
    ]
    if ctx_out:
        qc, kc, vc, sgc = ctx
        args = [logits, *lat, kc, vc, qc, sgc]
        in_specs += [pl.BlockSpec(ctx_blk, c_idx), pl.BlockSpec(ctx_blk, c_idx)]
        out_specs = (pl.BlockSpec(lat_blk, q_idx), pl.BlockSpec(ctx_blk, c_idx))
        out_shape = (jax.ShapeDtypeStruct((nb, SEQ, RET_W), BF16), jax.ShapeDtypeStruct((nb, CTX_LEN, RET_W), BF16))
    else:
        kc, vc = ctx
        args = [logits, *lat, kc, vc]
        out_specs = (pl.BlockSpec(lat_blk, q_idx),)
        out_shape = (jax.ShapeDtypeStruct((nb, SEQ, RET_W), BF16),)
    grid = (nb, 2, RET_STEPS)
    c_in_specs, c_args, c_out_specs, c_out_shapes = _cast_plan(cast, grid)
    state = pltpu.VMEM((HEADS, HEAD_DIM, HEAD_DIM), F32)
    pos_tab = pltpu.VMEM((HEADS, RCHUNK, HEAD_DIM), F32)
    return pl.pallas_call(
        functools.partial(_ret_kernel, ctx_out=ctx_out, n_cast=len(cast)),
        grid=grid,
        in_specs=in_specs + c_in_specs,
        out_specs=out_specs + tuple(c_out_specs),
        out_shape=out_shape + tuple(c_out_shapes),
        scratch_shapes=[
            pltpu.VMEM((HEADS, RCHUNK, RCHUNK), F32),
            pos_tab, pos_tab, pos_tab, pos_tab,
            state, state,
            state, state,
            pltpu.VMEM((N_RCHUNKS, HEADS, HEAD_DIM, HEAD_DIM), BF16),
        ],
        compiler_params=_params(3),
        name="retention",
    )(*args, *c_args)


def _dft_cos_sin(n):
    idx = np.arange(n, dtype=np.float64)
    ang = 2.0 * np.pi * ((idx[:, None] * idx[None, :]) % n) / n
    return np.cos(ang), np.sin(ang)


def _channel_dft(scale):
    c, s = _dft_cos_sin(FNET_GROUP)
    eye = np.eye(FNET_W // FNET_GROUP)
    return np.concatenate([np.kron(eye, c), np.kron(eye, s)], axis=0) * scale


def _fft_constants():
    k1 = np.arange(FFT_N1, dtype=np.float64)[None, :, None]
    n1 = np.arange(FFT_N1, dtype=np.float64)[None, None, :]
    n2 = np.arange(FFT_N2, dtype=np.float64)[:, None, None]
    ang = 2.0 * np.pi * ((k1 * (FFT_N2 * n1 + n2)) % SEQ) / SEQ
    m1 = np.concatenate([np.cos(ang), -np.sin(ang)], axis=1)
    c, s = _dft_cos_sin(FFT_N2)
    m3 = np.block([[c, s], [-s, c]])
    return (_np_split_cat(m1, -1), jnp.asarray(m3, F32),
            jnp.asarray(_channel_dft((SEQ * FNET_GROUP) ** -0.5), F32))


def _ctx_fft_constants():
    c, s = _dft_cos_sin(CTX_LEN)
    return (jnp.asarray(np.concatenate([c, -s], axis=0), F32),
            jnp.asarray(_channel_dft((CTX_LEN * FNET_GROUP) ** -0.5), F32))


def _np_split_cat(a64, axis):
    hi, lo = _np_split(a64)
    return jnp.asarray(np.concatenate([hi, lo], axis=axis))


def _twice(x, axis):
    return jnp.concatenate([x, x], axis=axis)


def _fft1_kernel(x_ref, m_ref, o_ref):
    for jj in range(FFT_STEP):
        x = jnp.concatenate([x_ref[hf, pl.ds(jj, FFT_N1, stride=FFT_STEP), :] for hf in range(2)],
                            axis=1).astype(BF16)
        a = _dot(m_ref[jj], _twice(x, 0))
        for hf in range(2):
            for ri in range(2):
                for kb in range(FFT_N1 // FFT_STEP):
                    r0 = ri * FFT_N1 + kb * FFT_STEP
                    o_ref[hf, ri, kb, jj * FFT_STEP:(jj + 1) * FFT_STEP, :] = (
                        a[r0:r0 + FFT_STEP, hf * HALF:(hf + 1) * HALF])


def _fft3_kernel(a_ref, w_ref, cd_ref, o_ref):
    w = w_ref[...].astype(BF16)
    zs = []
    for jj in range(FFT_STEP):
        rhs = jnp.concatenate(
            [jnp.concatenate([a_ref[hf, ri, pl.ds(jj, FFT_N2, stride=FFT_STEP), :] for ri in range(2)], axis=0)
             for hf in range(2)], axis=1).astype(BF16)
        y = _dot(w, rhs)
        zs.append(jnp.concatenate([y[:FFT_N2], y[FFT_N2:]], axis=1).astype(BF16))
    z = jnp.concatenate(zs, axis=0)
    four = _dot(z, cd_ref[...].astype(BF16))
    for jj in range(FFT_STEP):
        for kt in range(FFT_N2 // MIX_K2):
            for hf in range(2):
                r0 = jj * FFT_N2 + kt * MIX_K2
                o_ref[hf, kt, jj * MIX_K2:(jj + 1) * MIX_K2, :] = four[r0:r0 + MIX_K2, hf * HALF:(hf + 1) * HALF]


def _fourier_latent(f_blk):
    nb = f_blk.shape[0]
    m1, m3, cd = _fft_constants()
    n_kb = FFT_N1 // FFT_STEP
    a = pl.pallas_call(
        _fft1_kernel,
        grid=(nb, FFT_N2 // FFT_STEP),
        in_specs=[
            pl.BlockSpec((None, 2, None, FFT_N1 * FFT_STEP, HALF), lambda b, j: (b, 0, j, 0, 0)),
            pl.BlockSpec((FFT_STEP, 2 * FFT_N1, 2 * FFT_N1), lambda b, j: (j, 0, 0)),
        ],
        out_specs=pl.BlockSpec((None, 2, 2, n_kb, FFT_STEP * FFT_STEP, HALF), lambda b, j: (b, 0, 0, 0, j, 0)),
        out_shape=jax.ShapeDtypeStruct((nb, 2, 2, n_kb, FFT_N2 * FFT_STEP, HALF), F32),
        compiler_params=_params(2),
        name="fourier_stage1",
    )(f_blk, m1)
    return pl.pallas_call(
        _fft3_kernel,
        grid=(nb, n_kb),
        in_specs=[
            pl.BlockSpec((None, 2, 2, None, FFT_N2 * FFT_STEP, HALF), lambda b, j: (b, 0, 0, j, 0, 0)),
            _const_spec((2 * FFT_N2, 2 * FFT_N2)), _const_spec((2 * FNET_W, FNET_W)),
        ],
        out_specs=pl.BlockSpec((None, 2, FFT_N2 // MIX_K2, FFT_STEP * MIX_K2, HALF), lambda b, j: (b, 0, 0, j, 0)),
        out_shape=jax.ShapeDtypeStruct((nb, 2, FFT_N2 // MIX_K2, FFT_N1 * MIX_K2, HALF), F32),
        compiler_params=_params(2),
        name="fourier_stage2",
    )(a, m3, cd)


def _fft_ctx_kernel(x_ref, w_ref, cd_ref, o_ref):
    x = jnp.concatenate([x_ref[hf] for hf in range(2)], axis=1).astype(BF16)
    z = _dot(w_ref[...].astype(BF16), x)
    z = jnp.concatenate([z[:CTX_LEN], z[CTX_LEN:]], axis=1).astype(BF16)
    four = _dot(z, cd_ref[...].astype(BF16))
    for hf in range(2):
        o_ref[hf] = four[:, hf * HALF:(hf + 1) * HALF]


def _fourier_ctx(f_ctx):
    nb = f_ctx.shape[0]
    w, cd = _ctx_fft_constants()
    return pl.pallas_call(
        _fft_ctx_kernel,
        grid=(nb,),
        in_specs=[
            pl.BlockSpec((None, 2, CTX_LEN, HALF), lambda b: (b, 0, 0, 0)),
            _const_spec((2 * CTX_LEN, CTX_LEN)), _const_spec((2 * FNET_W, FNET_W)),
        ],
        out_specs=pl.BlockSpec((None, 2, CTX_LEN, HALF), lambda b: (b, 0, 0, 0)),
        out_shape=jax.ShapeDtypeStruct((nb, 2, CTX_LEN, HALF), F32),
        compiler_params=_params(1),
        name="fourier_ctx",
    )(f_ctx, w, cd)


def _mix_mlp_kernel(*refs, tm, ctx, n_cast):
    (x_ref, ret_ref, four_ref, u_ref, vn_ref, gate_ref, mod_ref, ws_ref, bs_ref,
     wa_ref, wb_ref, wc_ref, wo_ref, gpm_ref, gprm_ref, gpom_ref, wup_ref, wdn_ref) = refs[:18]
    o_ref = refs[18 + n_cast]
    _cast_chunks(refs[18:18 + n_cast], refs[19 + n_cast:])
    group = lax.broadcasted_iota(jnp.int32, (GMLP_CHUNK, GMLP_W), 1) // GMLP_GROUP
    parts = []
    for cc in range(tm // GMLP_CHUNK):
        rows = slice(cc * GMLP_CHUNK, (cc + 1) * GMLP_CHUNK)
        vn = vn_ref[0, rows, :]
        s = bs_ref[...]
        for g in range(GMLP_W // GMLP_GROUP):
            s = s + jnp.where(group == g, _dot(ws_ref[g], vn), 0.0)
        parts.append((u_ref[0, rows, :].astype(F32) * s).astype(BF16))
    sgu = jnp.concatenate(parts, axis=0)

    if ctx:
        four = jnp.concatenate([four_ref[0, hf] for hf in range(2)], axis=1).astype(BF16)
    else:
        nk2 = tm // FFT_N1
        four = jnp.concatenate(
            [jnp.concatenate([four_ref[0, hf, pl.ds(k2, FFT_N1, stride=nk2), :] for k2 in range(nk2)], axis=0)
             for hf in range(2)], axis=1).astype(BF16)
    ga = gate_ref[0, :, 0:D_MODEL].astype(F32)
    gb = gate_ref[0, :, D_MODEL:2 * D_MODEL].astype(F32)
    gc = gate_ref[0, :, 2 * D_MODEL:3 * D_MODEL].astype(F32)
    merged = (ga * _dot(ret_ref[0], wa_ref[...]) + gb * _dot(four, wb_ref[...])
              + gc * _dot(sgu, wc_ref[...])).astype(BF16)

    gate1 = mod_ref[0, :, 2 * D_MODEL:3 * D_MODEL]
    shift2 = mod_ref[0, :, 3 * D_MODEL:4 * D_MODEL]
    scale2 = mod_ref[0, :, 4 * D_MODEL:5 * D_MODEL]
    gate2 = mod_ref[0, :, 5 * D_MODEL:6 * D_MODEL]

    halves = [slice(r * (tm // 2), (r + 1) * (tm // 2)) for r in range(2)]
    x1, h2 = [], []
    for rows in halves:
        y = _dot(merged[rows], wo_ref[...])
        x1.append(x_ref[0, rows, :] + gate1 * (_rms(y) * gpm_ref[...]))
        h2.append(((_rms(x1[-1]) * gprm_ref[...]) * (1.0 + scale2) + shift2).astype(BF16))
    h2 = jnp.concatenate(h2, axis=0)

    n_ff = D_FF // FF_CHUNK
    m = None
    for c in range(n_ff - 1):
        cols = slice(c * FF_CHUNK, (c + 1) * FF_CHUNK)
        a = jnp.maximum(_dot(h2, wup_ref[:, cols]), 0.0)
        part = _dot((a * a).astype(BF16), wdn_ref[cols, :])
        m = part if m is None else m + part
    cols = slice((n_ff - 1) * FF_CHUNK, n_ff * FF_CHUNK)
    a = jnp.maximum(_dot(h2, wup_ref[:, cols]), 0.0)
    a = (a * a).astype(BF16)
    for r, rows in enumerate(halves):
        mr = m[rows] + _dot(a[rows], wdn_ref[cols, :])
        o_ref[0, rows, :] = x1[r] + gate2 * (_rms(mr) * gpom_ref[...])


def _mix_mlp(xs, ret, four, ug, vn, gates, layer, mod, ws, bs_tab, wa, wb, wc, wo,
             g_post_mix, g_pre_mlp, g_post_mlp, w_up, w_down, *, tm, ctx, cast=()):
    nb, rows, _ = xs.shape
    tok_spec = lambda w: pl.BlockSpec((1, tm, w), lambda b, i: (b, i, 0))
    lspec = lambda *shape: _layer_spec(shape, layer)
    if ctx:
        four_spec = pl.BlockSpec((1, 2, tm, HALF), lambda b, i: (b, 0, i, 0))
    else:
        four_spec = pl.BlockSpec((1, 2, None, tm, HALF), lambda b, i: (b, 0, i, 0, 0))
    c_in_specs, c_args, c_out_specs, c_out_shapes = _cast_plan(cast, (nb, rows // tm))
    outs = pl.pallas_call(
        functools.partial(_mix_mlp_kernel, tm=tm, ctx=ctx, n_cast=len(cast)),
        grid=(nb, rows // tm),
        in_specs=[
            tok_spec(D_MODEL), tok_spec(RET_W),
            four_spec,
            tok_spec(GMLP_W), tok_spec(GMLP_W), tok_spec(3 * D_MODEL),
            _mod_spec(layer, ctx),
            lspec(GMLP_W // GMLP_GROUP, GMLP_CHUNK, GMLP_CHUNK),
            lspec(GMLP_CHUNK, GMLP_W),
            _const_spec((RET_W, D_MODEL)), _const_spec((FNET_W, D_MODEL)), _const_spec((GMLP_W, D_MODEL)),
            _const_spec((D_MODEL, D_MODEL)),
            lspec(1, D_MODEL), lspec(1, D_MODEL), lspec(1, D_MODEL),
            _const_spec((D_MODEL, D_FF)), _const_spec((D_FF, D_MODEL)),
        ] + c_in_specs,
        out_specs=(tok_spec(D_MODEL),) + tuple(c_out_specs),
        out_shape=(jax.ShapeDtypeStruct((nb, rows, D_MODEL), F32),) + tuple(c_out_shapes),
        compiler_params=_params(2),
        name="mix_mlp_ctx" if ctx else "mix_mlp",
    )(xs, ret, four, ug, vn, gates, mod, ws, bs_tab, wa, wb, wc, wo,
      g_post_mix, g_pre_mlp, g_post_mlp, w_up, w_down, *c_args)
    return outs if cast else outs[0]


def _rope_tables():
    rows = SEQ // GRID_W
    freqs = ROPE_BASE ** (-jnp.arange(ROPE_FREQS, dtype=F32) / ROPE_FREQS)
    ang_r = jnp.arange(rows, dtype=F32)[:, None] * freqs
    ang_c = jnp.arange(GRID_W, dtype=F32)[:, None] * freqs

    def lanes(a, first):
        z = jnp.zeros_like(a)
        return jnp.concatenate([a, z] if first else [z, a], axis=-1)

    def pair(ang, first):
        cos, sin = lanes(jnp.cos(ang), first), lanes(jnp.sin(ang), first)
        return jnp.stack([jnp.concatenate([cos, cos], axis=-1), jnp.concatenate([-sin, sin], axis=-1)])

    return pair(ang_r, True), pair(ang_c, False)


def kernel(x, c, ctx, c_ctx, w_mod, b_mod, g_pre_mix, g_post_mix, g_pre_mlp, g_post_mlp, w_in,
           ret_decay_logit, sgu_w_s, sgu_b_s, sgu_norm, w_branch_a, w_branch_b, w_branch_c, w_out,
           w_up, w_down):
    nb = x.shape[0]
    depth = w_mod.shape[0]
    assert x.shape == (nb, SEQ, D_MODEL) and ctx.shape == (nb, CTX_LEN, D_MODEL) and nb == 2

    cond_rows = jnp.concatenate([c, c_ctx[None, :], jnp.zeros((8 - nb - 1, D_MODEL), F32)], axis=0)
    mod = _modulation(cond_rows, w_mod, b_mod).reshape(depth, 8, 1, 6 * D_MODEL)
    rope_tabs = _rope_tables()
    pavg = jnp.asarray(np.kron(np.eye(GMLP_W // GMLP_GROUP), np.full((GMLP_GROUP, GMLP_GROUP), 1.0 / GMLP_GROUP)), BF16)

    vec = lambda p: p[:, None, :]
    g_pre, g_norm = vec(g_pre_mix), vec(sgu_norm)
    logits = ret_decay_logit.reshape(depth, 2 * HEADS, 1)
    ws = sgu_w_s.astype(BF16)
    bs_tab = jnp.repeat(jnp.swapaxes(sgu_b_s, 1, 2), GMLP_GROUP, axis=2)
    gains = (vec(g_post_mix), vec(g_pre_mlp), vec(g_post_mlp))
    w_in_l = w_in[0].astype(BF16)
    for l in range(depth):
        last = l == depth - 1
        in_args = (l, mod, g_pre, w_in_l, pavg, g_norm)
        *proj, wa, wb, wc, wo = _in_proj(x, *in_args, rope_tabs, tm=TM_IN, ctx=False,
                                         cast=[(w, l) for w in (w_branch_a, w_branch_b, w_branch_c, w_out)])
        q, k, v, sg, f_lat, ug, vn, gates = proj
        mlp_cast = [(w_up, l), (w_down, l)]
        if last:
            kc, vc = _in_proj(ctx, *in_args, None, tm=CTX_LEN, ctx=True, kv_only=True)
            ret, wup, wdn = _retention(l, logits, (q, k, v, sg), (kc, vc), ctx_out=False, cast=mlp_cast)
        else:
            qc, kc, vc, sgc, f_ctx, ugc, vnc, gatesc = _in_proj(ctx, *in_args, None, tm=CTX_LEN, ctx=True)
            ret, retc, wup, wdn = _retention(l, logits, (q, k, v, sg), (qc, kc, vc, sgc), ctx_out=True,
                                             cast=mlp_cast)
        mix_w = (l, mod, ws, bs_tab, wa, wb, wc, wo, *gains, wup, wdn)
        four = _fourier_latent(f_lat)
        if last:
            x = _mix_mlp(x, ret, four, ug, vn, gates, *mix_w, tm=TM_LAT, ctx=False)
        else:
            x, w_in_l = _mix_mlp(x, ret, four, ug, vn, gates, *mix_w, tm=TM_LAT, ctx=False, cast=[(w_in, l + 1)])
            ctx = _mix_mlp(ctx, retc, _fourier_ctx(f_ctx), ugc, vnc, gatesc, *mix_w, tm=CTX_LEN, ctx=True)
    return x
```

```python
import functools

import numpy as np
import jax
import jax.numpy as jnp
from jax import lax
from jax.experimental import pallas as pl
from jax.experimental.pallas import tpu as pltpu

F32 = jnp.float32
BF16 = jnp.bfloat16

D_MODEL = 1024
SEQ = 8192
CTX_LEN = 256
GRID_W = 64
RET_W = 512
HEADS = 4
HEAD_DIM = 128
ROPE_BASE = 10000.0
ROPE_FREQS = HEAD_DIM // 4
FNET_W = 256
FNET_GROUP = 64
GMLP_W = 256
GMLP_GROUP = 64
GMLP_CHUNK = 128
D_FF = 4 * D_MODEL
EPS = 1e-6
IN_W = 4 * RET_W + FNET_W + 2 * GMLP_W + 3 * D_MODEL
COL_F = 4 * RET_W
COL_U = COL_F + FNET_W
COL_VS = COL_U + GMLP_W
COL_GATE = COL_VS + GMLP_W

TM_LAT = 512
TM_IN = 1024
IN_SUB = TM_IN
RCHUNK = 256
RET_TILE = 2048
RET_STEPS = SEQ // RET_TILE
RET_CPT = RET_TILE // RCHUNK
RET_AHEAD = 4
N_RCHUNKS = 1 + SEQ // RCHUNK
FF_CHUNK = 1024
FFT_N1 = 64
FFT_N2 = 128
HALF = 128
FFT_STEP = 16
MIX_K2 = TM_LAT // FFT_N1

_VMEM_LIMIT = 56 * 1024 * 1024


def _dot(a, b):
    return jnp.dot(a, b, preferred_element_type=F32)


def _split(x):
    hi = x.astype(BF16)
    lo = (x - hi.astype(F32)).astype(BF16)
    return hi, lo


def _np_split(a64):
    hi = np.asarray(a64, np.float32).astype(BF16)
    lo = (np.asarray(a64, np.float32) - hi.astype(np.float32)).astype(BF16)
    return hi, lo


def _rms(x):
    return x * lax.rsqrt(jnp.mean(x * x, axis=-1, keepdims=True) + EPS)


def _gelu(x):
    return x * (0.5 * (1.0 + jnp.tanh(0.7978845608028654 * (x + 0.044715 * (x * x * x)))))


def _sigmoid(x):
    return 1.0 / (1.0 + jnp.exp(-x))


def _const_spec(shape, nargs=None):
    zeros = (0,) * len(shape)
    return pl.BlockSpec(shape, lambda *_: zeros, pipeline_mode=pl.Buffered(1))


def _layer_spec(shape, layer):
    idx = (layer,) + (0,) * len(shape)
    return pl.BlockSpec((None,) + tuple(shape), lambda *_: idx, pipeline_mode=pl.Buffered(1))


def _params(n_axes):
    return pltpu.CompilerParams(dimension_semantics=("arbitrary",) * n_axes, vmem_limit_bytes=_VMEM_LIMIT)


def _mod_kernel(a_ref, w_ref, b_ref, o_ref):
    a = a_ref[...]
    a = a * _sigmoid(a)
    ah, al = _split(a)
    w = w_ref[0].astype(BF16)
    o_ref[0] = _dot(ah, w) + _dot(al, w) + b_ref[0]


def _modulation(cond_rows, w_mod, b_mod):
    depth = w_mod.shape[0]
    tn = 1536
    return pl.pallas_call(
        _mod_kernel,
        grid=(depth, (6 * D_MODEL) // tn),
        in_specs=[
            pl.BlockSpec((8, D_MODEL), lambda l, j: (0, 0)),
            pl.BlockSpec((1, D_MODEL, tn), lambda l, j: (l, 0, j)),
            pl.BlockSpec((1, 1, tn), lambda l, j: (l, 0, j)),
        ],
        out_specs=pl.BlockSpec((1, 8, tn), lambda l, j: (l, 0, j)),
        out_shape=jax.ShapeDtypeStruct((depth, 8, 6 * D_MODEL), F32),
        compiler_params=_params(2),
        name="modulation",
    )(cond_rows, w_mod, b_mod.reshape(depth, 1, 6 * D_MODEL))


def _cast_plan(cast, grid):
    steps = int(np.prod(grid))
    in_specs, args, out_specs, out_shapes = [], [], [], []

    def step(*idx):
        t = idx[0]
        for n, i in zip(grid[1:], idx[1:]):
            t = t * n + i
        return t

    for stack, layer in cast:
        _, rows, cols = stack.shape
        n_chunks = min(steps, rows // 16)
        assert rows % n_chunks == 0 and steps % n_chunks == 0
        per = steps // n_chunks
        in_specs.append(pl.BlockSpec((None, rows // n_chunks, cols),
                                     lambda *idx, layer=layer, per=per: (layer, step(*idx) // per, 0)))
        out_specs.append(pl.BlockSpec((rows // n_chunks, cols), lambda *idx, per=per: (step(*idx) // per, 0)))
        out_shapes.append(jax.ShapeDtypeStruct((rows, cols), BF16))
        args.append(stack)
    return in_specs, args, out_specs, out_shapes


def _cast_chunks(src_refs, dst_refs):
    for src, dst in zip(src_refs, dst_refs, strict=True):
        dst[...] = src[...].astype(BF16)


def _in_proj_kernel(*refs, rope, kv_only, n_cast):
    n_in = 8 if rope else 6
    ins, cast_src = refs[:n_in], refs[n_in:n_in + n_cast]
    outs, cast_dst = refs[n_in + n_cast:len(refs) - n_cast], refs[len(refs) - n_cast:]
    _cast_chunks(cast_src, cast_dst)
    if rope:
        x_ref, mod_ref, gpre_ref, w_ref, pavg_ref, gn_ref, rrow_ref, rcol_ref = ins
        n_rows = x_ref.shape[1] // GRID_W
        rot = []
        for t in range(2):
            by_row = jnp.concatenate(
                [jnp.broadcast_to(rrow_ref[t, r:r + 1, :], (GRID_W, HEAD_DIM)) for r in range(n_rows)], axis=0)
            rot.append(by_row + jnp.concatenate([rcol_ref[t]] * n_rows, axis=0))
    else:
        x_ref, mod_ref, gpre_ref, w_ref, pavg_ref, gn_ref = ins
    shift = mod_ref[0, :, 0:D_MODEL]
    scale = mod_ref[0, :, D_MODEL:2 * D_MODEL]
    tm = x_ref.shape[1]
    sub = min(tm, IN_SUB)
    for r0 in range(0, tm, sub):
        rows = slice(r0, r0 + sub)
        h = _rms(x_ref[0, rows, :]) * gpre_ref[...]
        hb = (h * (1.0 + scale) + shift).astype(BF16)

        def proj(lo, hi, hb=hb):
            return _dot(hb, w_ref[:, lo:hi])

        def rope_store(z, dst):
            if not rope:
                dst[0, rows, :] = z.astype(BF16)
                return
            cos, sin = rot[0][rows], rot[1][rows]
            for hd in range(HEADS):
                a = z[:, hd * HEAD_DIM:(hd + 1) * HEAD_DIM]
                r = a * cos + pltpu.roll(a, HEAD_DIM // 2, 1) * sin
                dst[0, rows, hd * HEAD_DIM:(hd + 1) * HEAD_DIM] = r.astype(BF16)

        if kv_only:
            k_ref, v_ref = outs
        else:
            q_ref, k_ref, v_ref, sg_ref, f_ref, u_ref, vn_ref, gate_ref = outs
            rope_store(proj(0, RET_W), q_ref)
        rope_store(proj(RET_W, 2 * RET_W) * (HEAD_DIM ** -0.5), k_ref)
        v_ref[0, rows, :] = proj(2 * RET_W, 3 * RET_W).astype(BF16)
        if kv_only:
            continue
        g = proj(3 * RET_W, 4 * RET_W)
        sg_ref[0, rows, :] = (g * _sigmoid(g)).astype(BF16)

        f = proj(COL_F, COL_U)
        for hf in range(2):
            fh = f[:, hf * HALF:(hf + 1) * HALF]
            if rope:
                for n1 in range(sub // FFT_N2):
                    row1 = (r0 // FFT_N2 + n1) * FFT_STEP
                    for nj in range(FFT_N2 // FFT_STEP):
                        f_ref[0, hf, nj, row1:row1 + FFT_STEP, :] = (
                            fh[n1 * FFT_N2 + nj * FFT_STEP:n1 * FFT_N2 + (nj + 1) * FFT_STEP])
            else:
                f_ref[0, hf, rows, :] = fh

        u_ref[0, rows, :] = _gelu(proj(COL_U, COL_VS)).astype(BF16)
        vg = _gelu(proj(COL_VS, COL_GATE))
        sh, sl = _split(vg * vg)
        pavg = pavg_ref[...]
        ms = _dot(sh, pavg) + _dot(sl, pavg)
        vn_ref[0, rows, :] = (vg * lax.rsqrt(ms + EPS) * gn_ref[...]).astype(BF16)

        for c in range(3):
            z = proj(COL_GATE + c * D_MODEL, COL_GATE + (c + 1) * D_MODEL)
            gate_ref[0, rows, c * D_MODEL:(c + 1) * D_MODEL] = _sigmoid(z).astype(BF16)


def _mod_spec(layer, ctx):
    idx = (lambda b, i: (layer, 2, 0, 0)) if ctx else (lambda b, i: (layer, b, 0, 0))
    return pl.BlockSpec((None, 1, 1, 6 * D_MODEL), idx)


def _in_proj(xs, layer, mod, g_pre, w_in, pavg, g_norm, rope_tabs, *, tm, ctx, kv_only=False, cast=()):
    nb, rows, _ = xs.shape
    rope = rope_tabs is not None
    tok_spec = lambda w: pl.BlockSpec((1, tm, w), lambda b, i: (b, i, 0))
    in_specs = [
        tok_spec(D_MODEL),
        _mod_spec(layer, ctx),
        _layer_spec((1, D_MODEL), layer),
        _const_spec((D_MODEL, IN_W)),
        _const_spec((GMLP_W, GMLP_W)),
        _layer_spec((1, GMLP_W), layer),
    ]
    args = [xs, mod, g_pre, w_in, pavg, g_norm]
    if rope:
        in_specs += [pl.BlockSpec((2, tm // GRID_W, HEAD_DIM), lambda b, i: (0, i, 0)),
                     _const_spec((2, GRID_W, HEAD_DIM))]
        args += list(rope_tabs)
    c_in_specs, c_args, c_out_specs, c_out_shapes = _cast_plan(cast, (nb, rows // tm))
    in_specs += c_in_specs
    args += c_args
    bf = lambda w: jax.ShapeDtypeStruct((nb, rows, w), BF16)
    if kv_only:
        out_shapes = (bf(RET_W), bf(RET_W))
        out_specs = (tok_spec(RET_W), tok_spec(RET_W))
    else:
        if rope:
            n_nj = FFT_N2 // FFT_STEP
            f_shape = jax.ShapeDtypeStruct((nb, 2, n_nj, FFT_N1 * FFT_STEP, HALF), F32)
            f_spec = pl.BlockSpec((1, 2, n_nj, (tm // FFT_N2) * FFT_STEP, HALF), lambda b, i: (b, 0, 0, i, 0))
        else:
            f_shape = jax.ShapeDtypeStruct((nb, 2, rows, HALF), F32)
            f_spec = pl.BlockSpec((1, 2, tm, HALF), lambda b, i: (b, 0, i, 0))
        out_shapes = (
            bf(RET_W), bf(RET_W), bf(RET_W), bf(RET_W),
            f_shape,
            bf(GMLP_W), bf(GMLP_W),
            bf(3 * D_MODEL),
        )
        out_specs = (
            tok_spec(RET_W), tok_spec(RET_W), tok_spec(RET_W), tok_spec(RET_W),
            f_spec,
            tok_spec(GMLP_W), tok_spec(GMLP_W), tok_spec(3 * D_MODEL),
        )
    return pl.pallas_call(
        functools.partial(_in_proj_kernel, rope=rope, kv_only=kv_only, n_cast=len(cast)),
        grid=(nb, rows // tm),
        in_specs=in_specs,
        out_specs=tuple(out_specs) + tuple(c_out_specs),
        out_shape=tuple(out_shapes) + tuple(c_out_shapes),
        compiler_params=_params(2),
        name="in_proj_ctx" if ctx else "in_proj",
    )(*args)


def _ret_kernel(*refs, ctx_out, n_cast):
    n_in, n_out = (9, 2) if ctx_out else (7, 1)
    ins, refs = refs[:n_in], refs[n_in:]
    cast_src, refs = refs[:n_cast], refs[n_cast:]
    outs, refs = refs[:n_out], refs[n_out:]
    cast_dst, scratch = refs[:n_cast], refs[n_cast:]
    _cast_chunks(cast_src, cast_dst)
    (logit_ref, ql_ref, kl_ref, vl_ref, sgl_ref, kc_ref, vc_ref) = ins[:7]
    if ctx_out:
        qc_ref, sgc_ref = ins[7:]
        ol_ref, oc_ref = outs
    else:
        (ol_ref,) = outs
    mask_ref, wf_ref, wb_ref, qf_ref, qb_ref, df_ref, db_ref, sf_ref, sb_ref, sball_ref = scratch
    b = pl.program_id(0)
    phase = pl.program_id(1)
    j = pl.program_id(2)

    @pl.when((b == 0) & (phase == 0) & (j == 0))
    def _():
        x = logit_ref[...]
        lg = -(jnp.maximum(-x, 0.0) + jnp.log(1.0 + jnp.exp(-jnp.abs(x))))
        row = lax.broadcasted_iota(jnp.int32, (RCHUNK, RCHUNK), 0).astype(F32)
        col = lax.broadcasted_iota(jnp.int32, (RCHUNK, RCHUNK), 1).astype(F32)
        diff = row - col
        pos = row[:, :HEAD_DIM]
        for hd in range(HEADS):
            lf = lg[hd:hd + 1, :]
            lb = lg[HEADS + hd:HEADS + hd + 1, :]
            mask_ref[hd] = jnp.where(diff >= 0.0, jnp.exp(lf * jnp.maximum(diff, 0.0)),
                                     jnp.exp(lb * jnp.maximum(-diff, 0.0)))
            wf_ref[hd] = jnp.exp(lf * (RCHUNK - 1.0 - pos))
            wb_ref[hd] = jnp.exp(lb * pos)
            qf_ref[hd] = jnp.exp(lf * (pos + 1.0))
            qb_ref[hd] = jnp.exp(lb * (RCHUNK - pos))
            df_ref[hd] = jnp.exp(jnp.broadcast_to(lf, (HEAD_DIM, HEAD_DIM)) * float(RCHUNK))
            db_ref[hd] = jnp.exp(jnp.broadcast_to(lb, (HEAD_DIM, HEAD_DIM)) * float(RCHUNK))

    def chunk_updates(hd, k_ref, v_ref, n_chunks, w_ref):
        cols = slice(hd * HEAD_DIM, (hd + 1) * HEAD_DIM)
        us = []
        for c in range(n_chunks):
            rows = slice(c * RCHUNK, (c + 1) * RCHUNK)
            kw = (k_ref[0, rows, cols].astype(F32) * w_ref[hd]).astype(BF16)
            us.append(lax.dot_general(kw, v_ref[0, rows, cols], (((0,), (0,)), ((), ())),
                                      preferred_element_type=F32))
        return us

    def backward_chunks(k_ref, v_ref, n_chunks, first_chunk):
        for hd in range(HEADS):
            us = chunk_updates(hd, k_ref, v_ref, n_chunks, wb_ref)
            s = sb_ref[hd]
            for c in reversed(range(n_chunks)):
                sball_ref[first_chunk + c, hd] = s.astype(BF16)
                s = db_ref[hd] * s + us[c]
            sb_ref[hd] = s

    def forward_chunks(q_ref, k_ref, v_ref, sg_ref, o_ref, n_chunks, first_chunk):
        starts = []
        for hd in range(HEADS):
            us = chunk_updates(hd, k_ref, v_ref, n_chunks, wf_ref)
            st = [sf_ref[hd]]
            for c in range(n_chunks):
                st.append(df_ref[hd] * st[-1] + us[c])
            sf_ref[hd] = st[-1]
            starts.append(st)
        if o_ref is None:
            return
        units = [(c, hd) for c in range(n_chunks) for hd in range(HEADS)]

        def scores(c, hd):
            rows = slice(c * RCHUNK, (c + 1) * RCHUNK)
            cols = slice(hd * HEAD_DIM, (hd + 1) * HEAD_DIM)
            return lax.dot_general(q_ref[0, rows, cols], k_ref[0, rows, cols], (((1,), (1,)), ((), ())),
                                   preferred_element_type=F32)

        ahead = min(RET_AHEAD, len(units))
        pending = [scores(*u) for u in units[:ahead]]
        for n, (c, hd) in enumerate(units):
            if n + ahead < len(units):
                pending.append(scores(*units[n + ahead]))
            sc = pending.pop(0)
            rows = slice(c * RCHUNK, (c + 1) * RCHUNK)
            cols = slice(hd * HEAD_DIM, (hd + 1) * HEAD_DIM)
            qf = q_ref[0, rows, cols].astype(F32)
            lhs = jnp.concatenate([(sc * mask_ref[hd]).astype(BF16),
                                   (qf * qf_ref[hd]).astype(BF16),
                                   (qf * qb_ref[hd]).astype(BF16)], axis=1)
            rhs = jnp.concatenate([v_ref[0, rows, cols], starts[hd][c].astype(BF16),
                                   sball_ref[first_chunk + c, hd]], axis=0)
            o = _dot(lhs, rhs)
            o_ref[0, rows, cols] = (_rms(o) * sg_ref[0, rows, cols].astype(F32)).astype(BF16)

    @pl.when(phase == 0)
    def _():
        @pl.when(j == 0)
        def _():
            sb_ref[...] = jnp.zeros_like(sb_ref)
            backward_chunks(kc_ref, vc_ref, 1, 0)

        backward_chunks(kl_ref, vl_ref, RET_CPT, 1 + (RET_STEPS - 1 - j) * RET_CPT)

    @pl.when(phase == 1)
    def _():
        @pl.when(j == 0)
        def _():
            sf_ref[...] = jnp.zeros_like(sf_ref)
            if ctx_out:
                forward_chunks(qc_ref, kc_ref, vc_ref, sgc_ref, oc_ref, 1, 0)
            else:
                forward_chunks(None, kc_ref, vc_ref, None, None, 1, 0)

        forward_chunks(ql_ref, kl_ref, vl_ref, sgl_ref, ol_ref, RET_CPT, 1 + j * RET_CPT)


def _retention(layer, logits, lat, ctx, *, ctx_out, cast=()):
    nb = lat[0].shape[0]
    kv_idx = lambda b, p, j: (b, jnp.where(p == 0, RET_STEPS - 1 - j, j), 0)
    q_idx = lambda b, p, j: (b, jnp.where(p == 0, 0, j), 0)
    c_idx = lambda b, p, j: (b, 0, 0)
    lat_blk = (1, RET_TILE, RET_W)
    ctx_blk = (1, CTX_LEN, RET_W)
    in_specs = [
        _layer_spec((2 * HEADS, 1), layer),
        pl.BlockSpec(lat_blk, q_idx), pl.BlockSpec(lat_blk, kv_idx), pl.BlockSpec(lat_blk, kv_idx),
        pl.BlockSpec(lat_blk, q_idx),
        pl.BlockSpec(ctx_blk, c_idx), pl.BlockSpec(ctx_blk, c_idx),
    ]
    if ctx_out:
        qc, kc, vc, sgc = ctx
        args = [logits, *lat, kc, vc, qc, sgc]
        in_specs += [pl.BlockSpec(ctx_blk, c_idx), pl.BlockSpec(ctx_blk, c_idx)]
        out_specs = (pl.BlockSpec(lat_blk, q_idx), pl.BlockSpec(ctx_blk, c_idx))
        out_shape = (jax.ShapeDtypeStruct((nb, SEQ, RET_W), BF16), jax.ShapeDtypeStruct((nb, CTX_LEN, RET_W), BF16))
    else:
        kc, vc = ctx
        args = [logits, *lat, kc, vc]
        out_specs = (pl.BlockSpec(lat_blk, q_idx),)
        out_shape = (jax.ShapeDtypeStruct((nb, SEQ, RET_W), BF16),)
    grid = (nb, 2, RET_STEPS)
    c_in_specs, c_args, c_out_specs, c_out_shapes = _cast_plan(cast, grid)
    state = pltpu.VMEM((HEADS, HEAD_DIM, HEAD_DIM), F32)
    pos_tab = pltpu.VMEM((HEADS, RCHUNK, HEAD_DIM), F32)
    return pl.pallas_call(
        functools.partial(_ret_kernel, ctx_out=ctx_out, n_cast=len(cast)),
        grid=grid,
        in_specs=in_specs + c_in_specs,
        out_specs=out_specs + tuple(c_out_specs),
        out_shape=out_shape + tuple(c_out_shapes),
        scratch_shapes=[
            pltpu.VMEM((HEADS, RCHUNK, RCHUNK), F32),
            pos_tab, pos_tab, pos_tab, pos_tab,
            state, state,
            state, state,
            pltpu.VMEM((N_RCHUNKS, HEADS, HEAD_DIM, HEAD_DIM), BF16),
        ],
        compiler_params=_params(3),
        name="retention",
    )(*args, *c_args)


def _dft_cos_sin(n):
    idx = np.arange(n, dtype=np.float64)
    ang = 2.0 * np.pi * ((idx[:, None] * idx[None, :]) % n) / n
    return np.cos(ang), np.sin(ang)


def _channel_dft(scale):
    c, s = _dft_cos_sin(FNET_GROUP)
    eye = np.eye(FNET_W // FNET_GROUP)
    return np.concatenate([np.kron(eye, c), np.kron(eye, s)], axis=0) * scale


def _fft_constants():
    k1 = np.arange(FFT_N1, dtype=np.float64)[None, :, None]
    n1 = np.arange(FFT_N1, dtype=np.float64)[None, None, :]
    n2 = np.arange(FFT_N2, dtype=np.float64)[:, None, None]
    ang = 2.0 * np.pi * ((k1 * (FFT_N2 * n1 + n2)) % SEQ) / SEQ
    m1 = np.concatenate([np.cos(ang), -np.sin(ang)], axis=1)
    c, s = _dft_cos_sin(FFT_N2)
    m3 = np.block([[c, s], [-s, c]])
    return (_np_split_cat(m1, -1), jnp.asarray(m3, F32),
            jnp.asarray(_channel_dft((SEQ * FNET_GROUP) ** -0.5), F32))


def _ctx_fft_constants():
    c, s = _dft_cos_sin(CTX_LEN)
    return (jnp.asarray(np.concatenate([c, -s], axis=0), F32),
            jnp.asarray(_channel_dft((CTX_LEN * FNET_GROUP) ** -0.5), F32))


def _np_split_cat(a64, axis):
    hi, lo = _np_split(a64)
    return jnp.asarray(np.concatenate([hi, lo], axis=axis))


def _twice(x, axis):
    return jnp.concatenate([x, x], axis=axis)


def _fft1_kernel(x_ref, m_ref, o_ref):
    for jj in range(FFT_STEP):
        x = jnp.concatenate([x_ref[hf, pl.ds(jj, FFT_N1, stride=FFT_STEP), :] for hf in range(2)],
                            axis=1).astype(BF16)
        a = _dot(m_ref[jj], _twice(x, 0))
        for hf in range(2):
            for ri in range(2):
                for kb in range(FFT_N1 // FFT_STEP):
                    r0 = ri * FFT_N1 + kb * FFT_STEP
                    o_ref[hf, ri, kb, jj * FFT_STEP:(jj + 1) * FFT_STEP, :] = (
                        a[r0:r0 + FFT_STEP, hf * HALF:(hf + 1) * HALF])


def _fft3_kernel(a_ref, w_ref, cd_ref, o_ref):
    w = w_ref[...].astype(BF16)
    zs = []
    for jj in range(FFT_STEP):
        rhs = jnp.concatenate(
            [jnp.concatenate([a_ref[hf, ri, pl.ds(jj, FFT_N2, stride=FFT_STEP), :] for ri in range(2)], axis=0)
             for hf in range(2)], axis=1).astype(BF16)
        y = _dot(w, rhs)
        zs.append(jnp.concatenate([y[:FFT_N2], y[FFT_N2:]], axis=1).astype(BF16))
    z = jnp.concatenate(zs, axis=0)
    four = _dot(z, cd_ref[...].astype(BF16))
    for jj in range(FFT_STEP):
        for kt in range(FFT_N2 // MIX_K2):
            for hf in range(2):
                r0 = jj * FFT_N2 + kt * MIX_K2
                o_ref[hf, kt, jj * MIX_K2:(jj + 1) * MIX_K2, :] = four[r0:r0 + MIX_K2, hf * HALF:(hf + 1) * HALF]


def _fourier_latent(f_blk):
    nb = f_blk.shape[0]
    m1, m3, cd = _fft_constants()
    n_kb = FFT_N1 // FFT_STEP
    a = pl.pallas_call(
        _fft1_kernel,
        grid=(nb, FFT_N2 // FFT_STEP),
        in_specs=[
            pl.BlockSpec((None, 2, None, FFT_N1 * FFT_STEP, HALF), lambda b, j: (b, 0, j, 0, 0)),
            pl.BlockSpec((FFT_STEP, 2 * FFT_N1, 2 * FFT_N1), lambda b, j: (j, 0, 0)),
        ],
        out_specs=pl.BlockSpec((None, 2, 2, n_kb, FFT_STEP * FFT_STEP, HALF), lambda b, j: (b, 0, 0, 0, j, 0)),
        out_shape=jax.ShapeDtypeStruct((nb, 2, 2, n_kb, FFT_N2 * FFT_STEP, HALF), F32),
        compiler_params=_params(2),
        name="fourier_stage1",
    )(f_blk, m1)
    return pl.pallas_call(
        _fft3_kernel,
        grid=(nb, n_kb),
        in_specs=[
            pl.BlockSpec((None, 2, 2, None, FFT_N2 * FFT_STEP, HALF), lambda b, j: (b, 0, 0, j, 0, 0)),
            _const_spec((2 * FFT_N2, 2 * FFT_N2)), _const_spec((2 * FNET_W, FNET_W)),
        ],
        out_specs=pl.BlockSpec((None, 2, FFT_N2 // MIX_K2, FFT_STEP * MIX_K2, HALF), lambda b, j: (b, 0, 0, j, 0)),
        out_shape=jax.ShapeDtypeStruct((nb, 2, FFT_N2 // MIX_K2, FFT_N1 * MIX_K2, HALF), F32),
        compiler_params=_params(2),
        name="fourier_stage2",
    )(a, m3, cd)


def _fft_ctx_kernel(x_ref, w_ref, cd_ref, o_ref):
    x = jnp.concatenate([x_ref[hf] for hf in range(2)], axis=1).astype(BF16)
    z = _dot(w_ref[...].astype(BF16), x)
    z = jnp.concatenate([z[:CTX_LEN], z[CTX_LEN:]], axis=1).astype(BF16)
    four = _dot(z, cd_ref[...].astype(BF16))
    for hf in range(2):
        o_ref[hf] = four[:, hf * HALF:(hf + 1) * HALF]


def _fourier_ctx(f_ctx):
    nb = f_ctx.shape[0]
    w, cd = _ctx_fft_constants()
    return pl.pallas_call(
        _fft_ctx_kernel,
        grid=(nb,),
        in_specs=[
            pl.BlockSpec((None, 2, CTX_LEN, HALF), lambda b: (b, 0, 0, 0)),
            _const_spec((2 * CTX_LEN, CTX_LEN)), _const_spec((2 * FNET_W, FNET_W)),
        ],
        out_specs=pl.BlockSpec((None, 2, CTX_LEN, HALF), lambda b: (b, 0, 0, 0)),
        out_shape=jax.ShapeDtypeStruct((nb, 2, CTX_LEN, HALF), F32),
        compiler_params=_params(1),
        name="fourier_ctx",
    )(f_ctx, w, cd)


def _mix_mlp_kernel(*refs, tm, ctx, n_cast):
    (x_ref, ret_ref, four_ref, u_ref, vn_ref, gate_ref, mod_ref, ws_ref, bs_ref,
     wa_ref, wb_ref, wc_ref, wo_ref, gpm_ref, gprm_ref, gpom_ref, wup_ref, wdn_ref) = refs[:18]
    o_ref = refs[18 + n_cast]
    _cast_chunks(refs[18:18 + n_cast], refs[19 + n_cast:])
    group = lax.broadcasted_iota(jnp.int32, (GMLP_CHUNK, GMLP_W), 1) // GMLP_GROUP
    parts = []
    for cc in range(tm // GMLP_CHUNK):
        rows = slice(cc * GMLP_CHUNK, (cc + 1) * GMLP_CHUNK)
        vn = vn_ref[0, rows, :]
        s = bs_ref[...]
        for g in range(GMLP_W // GMLP_GROUP):
            s = s + jnp.where(group == g, _dot(ws_ref[g], vn), 0.0)
        parts.append((u_ref[0, rows, :].astype(F32) * s).astype(BF16))
    sgu = jnp.concatenate(parts, axis=0)

    if ctx:
        four = jnp.concatenate([four_ref[0, hf] for hf in range(2)], axis=1).astype(BF16)
    else:
        nk2 = tm // FFT_N1
        four = jnp.concatenate(
            [jnp.concatenate([four_ref[0, hf, pl.ds(k2, FFT_N1, stride=nk2), :] for k2 in range(nk2)], axis=0)
             for hf in range(2)], axis=1).astype(BF16)
    ga = gate_ref[0, :, 0:D_MODEL].astype(F32)
    gb = gate_ref[0, :, D_MODEL:2 * D_MODEL].astype(F32)
    gc = gate_ref[0, :, 2 * D_MODEL:3 * D_MODEL].astype(F32)
    merged = (ga * _dot(ret_ref[0], wa_ref[...]) + gb * _dot(four, wb_ref[...])
              + gc * _dot(sgu, wc_ref[...])).astype(BF16)

    gate1 = mod_ref[0, :, 2 * D_MODEL:3 * D_MODEL]
    shift2 = mod_ref[0, :, 3 * D_MODEL:4 * D_MODEL]
    scale2 = mod_ref[0, :, 4 * D_MODEL:5 * D_MODEL]
    gate2 = mod_ref[0, :, 5 * D_MODEL:6 * D_MODEL]

    halves = [slice(r * (tm // 2), (r + 1) * (tm // 2)) for r in range(2)]
    x1, h2 = [], []
    for rows in halves:
        y = _dot(merged[rows], wo_ref[...])
        x1.append(x_ref[0, rows, :] + gate1 * (_rms(y) * gpm_ref[...]))
        h2.append(((_rms(x1[-1]) * gprm_ref[...]) * (1.0 + scale2) + shift2).astype(BF16))
    up0 = jnp.concatenate([_dot(h, wup_ref[:, 0:FF_CHUNK]) for h in h2], axis=0)
    h2 = jnp.concatenate(h2, axis=0)

    n_ff = D_FF // FF_CHUNK
    m = None
    for c in range(n_ff - 1):
        cols = slice(c * FF_CHUNK, (c + 1) * FF_CHUNK)
        a = jnp.maximum(up0 if c == 0 else _dot(h2, wup_ref[:, cols]), 0.0)
        part = _dot((a * a).astype(BF16), wdn_ref[cols, :])
        m = part if m is None else m + part
    cols = slice((n_ff - 1) * FF_CHUNK, n_ff * FF_CHUNK)
    a = jnp.maximum(_dot(h2, wup_ref[:, cols]), 0.0)
    a = (a * a).astype(BF16)
    for r, rows in enumerate(halves):
        mr = m[rows] + _dot(a[rows], wdn_ref[cols, :])
        o_ref[0, rows, :] = x1[r] + gate2 * (_rms(mr) * gpom_ref[...])


def _mix_mlp(xs, ret, four, ug, vn, gates, layer, mod, ws, bs_tab, wa, wb, wc, wo,
             g_post_mix, g_pre_mlp, g_post_mlp, w_up, w_down, *, tm, ctx, cast=()):
    nb, rows, _ = xs.shape
    tok_spec = lambda w: pl.BlockSpec((1, tm, w), lambda b, i: (b, i, 0))
    lspec = lambda *shape: _layer_spec(shape, layer)
    if ctx:
        four_spec = pl.BlockSpec((1, 2, tm, HALF), lambda b, i: (b, 0, i, 0))
    else:
        four_spec = pl.BlockSpec((1, 2, None, tm, HALF), lambda b, i: (b, 0, i, 0, 0))
    c_in_specs, c_args, c_out_specs, c_out_shapes = _cast_plan(cast, (nb, rows // tm))
    outs = pl.pallas_call(
        functools.partial(_mix_mlp_kernel, tm=tm, ctx=ctx, n_cast=len(cast)),
        grid=(nb, rows // tm),
        in_specs=[
            tok_spec(D_MODEL), tok_spec(RET_W),
            four_spec,
            tok_spec(GMLP_W), tok_spec(GMLP_W), tok_spec(3 * D_MODEL),
            _mod_spec(layer, ctx),
            lspec(GMLP_W // GMLP_GROUP, GMLP_CHUNK, GMLP_CHUNK),
            lspec(GMLP_CHUNK, GMLP_W),
            _const_spec((RET_W, D_MODEL)), _const_spec((FNET_W, D_MODEL)), _const_spec((GMLP_W, D_MODEL)),
            _const_spec((D_MODEL, D_MODEL)),
            lspec(1, D_MODEL), lspec(1, D_MODEL), lspec(1, D_MODEL),
            _const_spec((D_MODEL, D_FF)), _const_spec((D_FF, D_MODEL)),
        ] + c_in_specs,
        out_specs=(tok_spec(D_MODEL),) + tuple(c_out_specs),
        out_shape=(jax.ShapeDtypeStruct((nb, rows, D_MODEL), F32),) + tuple(c_out_shapes),
        compiler_params=_params(2),
        name="mix_mlp_ctx" if ctx else "mix_mlp",
    )(xs, ret, four, ug, vn, gates, mod, ws, bs_tab, wa, wb, wc, wo,
      g_post_mix, g_pre_mlp, g_post_mlp, w_up, w_down, *c_args)
    return outs if cast else outs[0]


def _rope_tables():
    rows = SEQ // GRID_W
    freqs = ROPE_BASE ** (-jnp.arange(ROPE_FREQS, dtype=F32) / ROPE_FREQS)
    ang_r = jnp.arange(rows, dtype=F32)[:, None] * freqs
    ang_c = jnp.arange(GRID_W, dtype=F32)[:, None] * freqs

    def lanes(a, first):
        z = jnp.zeros_like(a)
        return jnp.concatenate([a, z] if first else [z, a], axis=-1)

    def pair(ang, first):
        cos, sin = lanes(jnp.cos(ang), first), lanes(jnp.sin(ang), first)
        return jnp.stack([jnp.concatenate([cos, cos], axis=-1), jnp.concatenate([-sin, sin], axis=-1)])

    return pair(ang_r, True), pair(ang_c, False)


def kernel(x, c, ctx, c_ctx, w_mod, b_mod, g_pre_mix, g_post_mix, g_pre_mlp, g_post_mlp, w_in,
           ret_decay_logit, sgu_w_s, sgu_b_s, sgu_norm, w_branch_a, w_branch_b, w_branch_c, w_out,
           w_up, w_down):
    nb = x.shape[0]
    depth = w_mod.shape[0]
    assert x.shape == (nb, SEQ, D_MODEL) and ctx.shape == (nb, CTX_LEN, D_MODEL) and nb == 2

    cond_rows = jnp.concatenate([c, c_ctx[None, :], jnp.zeros((8 - nb - 1, D_MODEL), F32)], axis=0)
    mod = _modulation(cond_rows, w_mod, b_mod).reshape(depth, 8, 1, 6 * D_MODEL)
    rope_tabs = _rope_tables()
    pavg = jnp.asarray(np.kron(np.eye(GMLP_W // GMLP_GROUP), np.full((GMLP_GROUP, GMLP_GROUP), 1.0 / GMLP_GROUP)), BF16)

    vec = lambda p: p[:, None, :]
    g_pre, g_norm = vec(g_pre_mix), vec(sgu_norm)
    logits = ret_decay_logit.reshape(depth, 2 * HEADS, 1)
    ws = sgu_w_s.astype(BF16)
    bs_tab = jnp.repeat(jnp.swapaxes(sgu_b_s, 1, 2), GMLP_GROUP, axis=2)
    gains = (vec(g_post_mix), vec(g_pre_mlp), vec(g_post_mlp))
    w_in_l = w_in[0].astype(BF16)
    for l in range(depth):
        last = l == depth - 1
        in_args = (l, mod, g_pre, w_in_l, pavg, g_norm)
        *proj, wa, wb, wc, wo = _in_proj(x, *in_args, rope_tabs, tm=TM_IN, ctx=False,
                                         cast=[(w, l) for w in (w_branch_a, w_branch_b, w_branch_c, w_out)])
        q, k, v, sg, f_lat, ug, vn, gates = proj
        mlp_cast = [(w_up, l), (w_down, l)]
        if last:
            kc, vc = _in_proj(ctx, *in_args, None, tm=CTX_LEN, ctx=True, kv_only=True)
            ret, wup, wdn = _retention(l, logits, (q, k, v, sg), (kc, vc), ctx_out=False, cast=mlp_cast)
        else:
            qc, kc, vc, sgc, f_ctx, ugc, vnc, gatesc = _in_proj(ctx, *in_args, None, tm=CTX_LEN, ctx=True)
            ret, retc, wup, wdn = _retention(l, logits, (q, k, v, sg), (qc, kc, vc, sgc), ctx_out=True,
                                             cast=mlp_cast)
        mix_w = (l, mod, ws, bs_tab, wa, wb, wc, wo, *gains, wup, wdn)
        four = _fourier_latent(f_lat)
        if last:
            x = _mix_mlp(x, ret, four, ug, vn, gates, *mix_w, tm=TM_LAT, ctx=False)
        else:
            x, w_in_l = _mix_mlp(x, ret, four, ug, vn, gates, *mix_w, tm=TM_LAT, ctx=False, cast=[(w_in, l + 1)])
            ctx = _mix_mlp(ctx, retc, _fourier_ctx(f_ctx), ugc, vnc, gatesc, *mix_w, tm=CTX_LEN, ctx=True)
    return x
```

```python
import functools

import numpy as np
import jax
import jax.numpy as jnp
from jax import lax
from jax.experimental import pallas as pl
from jax.experimental.pallas import tpu as pltpu

F32 = jnp.float32
BF16 = jnp.bfloat16

D_MODEL = 1024
SEQ = 8192
CTX_LEN = 256
GRID_W = 64
RET_W = 512
HEADS = 4
HEAD_DIM = 128
ROPE_BASE = 10000.0
ROPE_FREQS = HEAD_DIM // 4
FNET_W = 256
FNET_GROUP = 64
GMLP_W = 256
GMLP_GROUP = 64
GMLP_CHUNK = 128
D_FF = 4 * D_MODEL
EPS = 1e-6
IN_W = 4 * RET_W + FNET_W + 2 * GMLP_W + 3 * D_MODEL
COL_F = 4 * RET_W
COL_U = COL_F + FNET_W
COL_VS = COL_U + GMLP_W
COL_GATE = COL_VS + GMLP_W

TM_LAT = 512
TM_IN = 1024
IN_SUB = TM_IN
RCHUNK = 256
RET_TILE = 2048
RET_STEPS = SEQ // RET_TILE
RET_CPT = RET_TILE // RCHUNK
RET_AHEAD = 4
N_RCHUNKS = 1 + SEQ // RCHUNK
FF_CHUNK = 1024
FFT_N1 = 64
FFT_N2 = 128
HALF = 128
FFT_STEP = 16
MIX_K2 = TM_LAT // FFT_N1

_VMEM_LIMIT = 56 * 1024 * 1024


def _dot(a, b):
    return jnp.dot(a, b, preferred_element_type=F32)


def _split(x):
    hi = x.astype(BF16)
    lo = (x - hi.astype(F32)).astype(BF16)
    return hi, lo


def _np_split(a64):
    hi = np.asarray(a64, np.float32).astype(BF16)
    lo = (np.asarray(a64, np.float32) - hi.astype(np.float32)).astype(BF16)
    return hi, lo


def _rms(x):
    return x * lax.rsqrt(jnp.mean(x * x, axis=-1, keepdims=True) + EPS)


def _gelu(x):
    return x * (0.5 * (1.0 + jnp.tanh(0.7978845608028654 * (x + 0.044715 * (x * x * x)))))


def _sigmoid(x):
    return 1.0 / (1.0 + jnp.exp(-x))


def _const_spec(shape, nargs=None):
    zeros = (0,) * len(shape)
    return pl.BlockSpec(shape, lambda *_: zeros, pipeline_mode=pl.Buffered(1))


def _layer_spec(shape, layer):
    idx = (layer,) + (0,) * len(shape)
    return pl.BlockSpec((None,) + tuple(shape), lambda *_: idx, pipeline_mode=pl.Buffered(1))


def _params(n_axes):
    return pltpu.CompilerParams(dimension_semantics=("arbitrary",) * n_axes, vmem_limit_bytes=_VMEM_LIMIT)


def _mod_kernel(a_ref, w_ref, b_ref, o_ref):
    a = a_ref[...]
    a = a * _sigmoid(a)
    ah, al = _split(a)
    w = w_ref[0].astype(BF16)
    o_ref[0] = _dot(ah, w) + _dot(al, w) + b_ref[0]


def _modulation(cond_rows, w_mod, b_mod):
    depth = w_mod.shape[0]
    tn = 1536
    return pl.pallas_call(
        _mod_kernel,
        grid=(depth, (6 * D_MODEL) // tn),
        in_specs=[
            pl.BlockSpec((8, D_MODEL), lambda l, j: (0, 0)),
            pl.BlockSpec((1, D_MODEL, tn), lambda l, j: (l, 0, j)),
            pl.BlockSpec((1, 1, tn), lambda l, j: (l, 0, j)),
        ],
        out_specs=pl.BlockSpec((1, 8, tn), lambda l, j: (l, 0, j)),
        out_shape=jax.ShapeDtypeStruct((depth, 8, 6 * D_MODEL), F32),
        compiler_params=_params(2),
        name="modulation",
    )(cond_rows, w_mod, b_mod.reshape(depth, 1, 6 * D_MODEL))


def _cast_plan(cast, grid):
    steps = int(np.prod(grid))
    in_specs, args, out_specs, out_shapes = [], [], [], []

    def step(*idx):
        t = idx[0]
        for n, i in zip(grid[1:], idx[1:]):
            t = t * n + i
        return t

    for stack, layer in cast:
        _, rows, cols = stack.shape
        n_chunks = min(steps, rows // 16)
        assert rows % n_chunks == 0 and steps % n_chunks == 0
        per = steps // n_chunks
        in_specs.append(pl.BlockSpec((None, rows // n_chunks, cols),
                                     lambda *idx, layer=layer, per=per: (layer, step(*idx) // per, 0)))
        out_specs.append(pl.BlockSpec((rows // n_chunks, cols), lambda *idx, per=per: (step(*idx) // per, 0)))
        out_shapes.append(jax.ShapeDtypeStruct((rows, cols), BF16))
        args.append(stack)
    return in_specs, args, out_specs, out_shapes


def _cast_chunks(src_refs, dst_refs):
    for src, dst in zip(src_refs, dst_refs, strict=True):
        dst[...] = src[...].astype(BF16)


def _in_proj_kernel(*refs, rope, kv_only, n_cast):
    n_in = 8 if rope else 6
    ins, cast_src = refs[:n_in], refs[n_in:n_in + n_cast]
    outs, cast_dst = refs[n_in + n_cast:len(refs) - n_cast], refs[len(refs) - n_cast:]
    _cast_chunks(cast_src, cast_dst)
    if rope:
        x_ref, mod_ref, gpre_ref, w_ref, pavg_ref, gn_ref, rrow_ref, rcol_ref = ins
        n_rows = x_ref.shape[1] // GRID_W
        rot = []
        for t in range(2):
            by_row = jnp.concatenate(
                [jnp.broadcast_to(rrow_ref[t, r:r + 1, :], (GRID_W, HEAD_DIM)) for r in range(n_rows)], axis=0)
            rot.append(by_row + jnp.concatenate([rcol_ref[t]] * n_rows, axis=0))
    else:
        x_ref, mod_ref, gpre_ref, w_ref, pavg_ref, gn_ref = ins
    shift = mod_ref[0, :, 0:D_MODEL]
    gain = gpre_ref[...] * (1.0 + mod_ref[0, :, D_MODEL:2 * D_MODEL])
    tm = x_ref.shape[1]
    sub = min(tm, IN_SUB)
    for r0 in range(0, tm, sub):
        rows = slice(r0, r0 + sub)
        hb = (_rms(x_ref[0, rows, :]) * gain + shift).astype(BF16)

        def proj(lo, hi, hb=hb):
            return _dot(hb, w_ref[:, lo:hi])

        def rope_store(z, dst):
            if not rope:
                dst[0, rows, :] = z.astype(BF16)
                return
            cos, sin = rot[0][rows], rot[1][rows]
            for hd in range(HEADS):
                a = z[:, hd * HEAD_DIM:(hd + 1) * HEAD_DIM]
                r = a * cos + pltpu.roll(a, HEAD_DIM // 2, 1) * sin
                dst[0, rows, hd * HEAD_DIM:(hd + 1) * HEAD_DIM] = r.astype(BF16)

        if kv_only:
            k_ref, v_ref = outs
        else:
            q_ref, k_ref, v_ref, sg_ref, f_ref, u_ref, vn_ref, gate_ref = outs
            rope_store(proj(0, RET_W), q_ref)
        rope_store(proj(RET_W, 2 * RET_W) * (HEAD_DIM ** -0.5), k_ref)
        v_ref[0, rows, :] = proj(2 * RET_W, 3 * RET_W).astype(BF16)
        if kv_only:
            continue
        g = proj(3 * RET_W, 4 * RET_W)
        sg_ref[0, rows, :] = (g * _sigmoid(g)).astype(BF16)

        f = proj(COL_F, COL_U)
        for hf in range(2):
            fh = f[:, hf * HALF:(hf + 1) * HALF]
            if rope:
                for n1 in range(sub // FFT_N2):
                    row1 = (r0 // FFT_N2 + n1) * FFT_STEP
                    for nj in range(FFT_N2 // FFT_STEP):
                        f_ref[0, hf, nj, row1:row1 + FFT_STEP, :] = (
                            fh[n1 * FFT_N2 + nj * FFT_STEP:n1 * FFT_N2 + (nj + 1) * FFT_STEP])
            else:
                f_ref[0, hf, rows, :] = fh

        u_ref[0, rows, :] = _gelu(proj(COL_U, COL_VS)).astype(BF16)
        vg = _gelu(proj(COL_VS, COL_GATE))
        sh, sl = _split(vg * vg)
        pavg = pavg_ref[...]
        ms = _dot(sh, pavg) + _dot(sl, pavg)
        vn_ref[0, rows, :] = (vg * lax.rsqrt(ms + EPS) * gn_ref[...]).astype(BF16)

        for c in range(3):
            z = proj(COL_GATE + c * D_MODEL, COL_GATE + (c + 1) * D_MODEL)
            gate_ref[0, rows, c * D_MODEL:(c + 1) * D_MODEL] = _sigmoid(z).astype(BF16)


def _mod_spec(layer, ctx):
    idx = (lambda b, i: (layer, 2, 0, 0)) if ctx else (lambda b, i: (layer, b, 0, 0))
    return pl.BlockSpec((None, 1, 1, 6 * D_MODEL), idx)


def _in_proj(xs, layer, mod, g_pre, w_in, pavg, g_norm, rope_tabs, *, tm, ctx, kv_only=False, cast=()):
    nb, rows, _ = xs.shape
    rope = rope_tabs is not None
    tok_spec = lambda w: pl.BlockSpec((1, tm, w), lambda b, i: (b, i, 0))
    in_specs = [
        tok_spec(D_MODEL),
        _mod_spec(layer, ctx),
        _layer_spec((1, D_MODEL), layer),
        _const_spec((D_MODEL, IN_W)),
        _const_spec((GMLP_W, GMLP_W)),
        _layer_spec((1, GMLP_W), layer),
    ]
    args = [xs, mod, g_pre, w_in, pavg, g_norm]
    if rope:
        in_specs += [pl.BlockSpec((2, tm // GRID_W, HEAD_DIM), lambda b, i: (0, i, 0)),
                     _const_spec((2, GRID_W, HEAD_DIM))]
        args += list(rope_tabs)
    c_in_specs, c_args, c_out_specs, c_out_shapes = _cast_plan(cast, (nb, rows // tm))
    in_specs += c_in_specs
    args += c_args
    bf = lambda w: jax.ShapeDtypeStruct((nb, rows, w), BF16)
    if kv_only:
        out_shapes = (bf(RET_W), bf(RET_W))
        out_specs = (tok_spec(RET_W), tok_spec(RET_W))
    else:
        if rope:
            n_nj = FFT_N2 // FFT_STEP
            f_shape = jax.ShapeDtypeStruct((nb, 2, n_nj, FFT_N1 * FFT_STEP, HALF), F32)
            f_spec = pl.BlockSpec((1, 2, n_nj, (tm // FFT_N2) * FFT_STEP, HALF), lambda b, i: (b, 0, 0, i, 0))
        else:
            f_shape = jax.ShapeDtypeStruct((nb, 2, rows, HALF), F32)
            f_spec = pl.BlockSpec((1, 2, tm, HALF), lambda b, i: (b, 0, i, 0))
        out_shapes = (
            bf(RET_W), bf(RET_W), bf(RET_W), bf(RET_W),
            f_shape,
            bf(GMLP_W), bf(GMLP_W),
            bf(3 * D_MODEL),
        )
        out_specs = (
            tok_spec(RET_W), tok_spec(RET_W), tok_spec(RET_W), tok_spec(RET_W),
            f_spec,
            tok_spec(GMLP_W), tok_spec(GMLP_W), tok_spec(3 * D_MODEL),
        )
    return pl.pallas_call(
        functools.partial(_in_proj_kernel, rope=rope, kv_only=kv_only, n_cast=len(cast)),
        grid=(nb, rows // tm),
        in_specs=in_specs,
        out_specs=tuple(out_specs) + tuple(c_out_specs),
        out_shape=tuple(out_shapes) + tuple(c_out_shapes),
        compiler_params=_params(2),
        name="in_proj_ctx" if ctx else "in_proj",
    )(*args)


def _ret_kernel(*refs, ctx_out, n_cast):
    n_in, n_out = (9, 2) if ctx_out else (7, 1)
    ins, refs = refs[:n_in], refs[n_in:]
    cast_src, refs = refs[:n_cast], refs[n_cast:]
    outs, refs = refs[:n_out], refs[n_out:]
    cast_dst, scratch = refs[:n_cast], refs[n_cast:]
    _cast_chunks(cast_src, cast_dst)
    (logit_ref, ql_ref, kl_ref, vl_ref, sgl_ref, kc_ref, vc_ref) = ins[:7]
    if ctx_out:
        qc_ref, sgc_ref = ins[7:]
        ol_ref, oc_ref = outs
    else:
        (ol_ref,) = outs
    mask_ref, wf_ref, wb_ref, qf_ref, qb_ref, df_ref, db_ref, sf_ref, sb_ref, sball_ref = scratch
    b = pl.program_id(0)
    phase = pl.program_id(1)
    j = pl.program_id(2)

    @pl.when((b == 0) & (phase == 0) & (j == 0))
    def _():
        x = logit_ref[...]
        lg = -(jnp.maximum(-x, 0.0) + jnp.log(1.0 + jnp.exp(-jnp.abs(x))))
        row = lax.broadcasted_iota(jnp.int32, (RCHUNK, RCHUNK), 0).astype(F32)
        col = lax.broadcasted_iota(jnp.int32, (RCHUNK, RCHUNK), 1).astype(F32)
        diff = row - col
        pos = row[:, :HEAD_DIM]
        for hd in range(HEADS):
            lf = lg[hd:hd + 1, :]
            lb = lg[HEADS + hd:HEADS + hd + 1, :]
            mask_ref[hd] = jnp.where(diff >= 0.0, jnp.exp(lf * jnp.maximum(diff, 0.0)),
                                     jnp.exp(lb * jnp.maximum(-diff, 0.0)))
            wf_ref[hd] = jnp.exp(lf * (RCHUNK - 1.0 - pos))
            wb_ref[hd] = jnp.exp(lb * pos)
            qf_ref[hd] = jnp.exp(lf * (pos + 1.0))
            qb_ref[hd] = jnp.exp(lb * (RCHUNK - pos))
            df_ref[hd] = jnp.exp(jnp.broadcast_to(lf, (HEAD_DIM, HEAD_DIM)) * float(RCHUNK))
            db_ref[hd] = jnp.exp(jnp.broadcast_to(lb, (HEAD_DIM, HEAD_DIM)) * float(RCHUNK))

    def chunk_updates(hd, k_ref, v_ref, n_chunks, w_ref):
        cols = slice(hd * HEAD_DIM, (hd + 1) * HEAD_DIM)
        us = []
        for c in range(n_chunks):
            rows = slice(c * RCHUNK, (c + 1) * RCHUNK)
            kw = (k_ref[0, rows, cols].astype(F32) * w_ref[hd]).astype(BF16)
            us.append(lax.dot_general(kw, v_ref[0, rows, cols], (((0,), (0,)), ((), ())),
                                      preferred_element_type=F32))
        return us

    def backward_chunks(k_ref, v_ref, n_chunks, first_chunk):
        for hd in range(HEADS):
            us = chunk_updates(hd, k_ref, v_ref, n_chunks, wb_ref)
            s = sb_ref[hd]
            for c in reversed(range(n_chunks)):
                sball_ref[first_chunk + c, hd] = s.astype(BF16)
                s = db_ref[hd] * s + us[c]
            sb_ref[hd] = s

    def forward_chunks(q_ref, k_ref, v_ref, sg_ref, o_ref, n_chunks, first_chunk):
        starts = []
        for hd in range(HEADS):
            us = chunk_updates(hd, k_ref, v_ref, n_chunks, wf_ref)
            st = [sf_ref[hd]]
            for c in range(n_chunks):
                st.append(df_ref[hd] * st[-1] + us[c])
            sf_ref[hd] = st[-1]
            starts.append(st)
        if o_ref is None:
            return
        units = [(c, hd) for c in range(n_chunks) for hd in range(HEADS)]

        def scores(c, hd):
            rows = slice(c * RCHUNK, (c + 1) * RCHUNK)
            cols = slice(hd * HEAD_DIM, (hd + 1) * HEAD_DIM)
            return lax.dot_general(q_ref[0, rows, cols], k_ref[0, rows, cols], (((1,), (1,)), ((), ())),
                                   preferred_element_type=F32)

        ahead = min(RET_AHEAD, len(units))
        pending = [scores(*u) for u in units[:ahead]]
        for n, (c, hd) in enumerate(units):
            if n + ahead < len(units):
                pending.append(scores(*units[n + ahead]))
            sc = pending.pop(0)
            rows = slice(c * RCHUNK, (c + 1) * RCHUNK)
            cols = slice(hd * HEAD_DIM, (hd + 1) * HEAD_DIM)
            qf = q_ref[0, rows, cols].astype(F32)
            lhs = jnp.concatenate([(sc * mask_ref[hd]).astype(BF16),
                                   (qf * qf_ref[hd]).astype(BF16),
                                   (qf * qb_ref[hd]).astype(BF16)], axis=1)
            rhs = jnp.concatenate([v_ref[0, rows, cols], starts[hd][c].astype(BF16),
                                   sball_ref[first_chunk + c, hd]], axis=0)
            o = _dot(lhs, rhs)
            o_ref[0, rows, cols] = (_rms(o) * sg_ref[0, rows, cols].astype(F32)).astype(BF16)

    @pl.when(phase == 0)
    def _():
        @pl.when(j == 0)
        def _():
            sb_ref[...] = jnp.zeros_like(sb_ref)
            backward_chunks(kc_ref, vc_ref, 1, 0)

        backward_chunks(kl_ref, vl_ref, RET_CPT, 1 + (RET_STEPS - 1 - j) * RET_CPT)

    @pl.when(phase == 1)
    def _():
        @pl.when(j == 0)
        def _():
            sf_ref[...] = jnp.zeros_like(sf_ref)
            if ctx_out:
                forward_chunks(qc_ref, kc_ref, vc_ref, sgc_ref, oc_ref, 1, 0)
            else:
                forward_chunks(None, kc_ref, vc_ref, None, None, 1, 0)

        forward_chunks(ql_ref, kl_ref, vl_ref, sgl_ref, ol_ref, RET_CPT, 1 + j * RET_CPT)


def _retention(layer, logits, lat, ctx, *, ctx_out, cast=()):
    nb = lat[0].shape[0]
    kv_idx = lambda b, p, j: (b, jnp.where(p == 0, RET_STEPS - 1 - j, j), 0)
    q_idx = lambda b, p, j: (b, jnp.where(p == 0, 0, j), 0)
    c_idx = lambda b, p, j: (b, 0, 0)
    lat_blk = (1, RET_TILE, RET_W)
    ctx_blk = (1, CTX_LEN, RET_W)
    in_specs = [
        _layer_spec((2 * HEADS, 1), layer),
        pl.BlockSpec(lat_blk, q_idx), pl.BlockSpec(lat_blk, kv_idx), pl.BlockSpec(lat_blk, kv_idx),
        pl.BlockSpec(lat_blk, q_idx),
        pl.BlockSpec(ctx_blk, c_idx), pl.BlockSpec(ctx_blk, c_idx),
    ]
    if ctx_out:
        qc, kc, vc, sgc = ctx
        args = [logits, *lat, kc, vc, qc, sgc]
        in_specs += [pl.BlockSpec(ctx_blk, c_idx), pl.BlockSpec(ctx_blk, c_idx)]
        out_specs = (pl.BlockSpec(lat_blk, q_idx), pl.BlockSpec(ctx_blk, c_idx))
        out_shape = (jax.ShapeDtypeStruct((nb, SEQ, RET_W), BF16), jax.ShapeDtypeStruct((nb, CTX_LEN, RET_W), BF16))
    else:
        kc, vc = ctx
        args = [logits, *lat, kc, vc]
        out_specs = (pl.BlockSpec(lat_blk, q_idx),)
        out_shape = (jax.ShapeDtypeStruct((nb, SEQ, RET_W), BF16),)
    grid = (nb, 2, RET_STEPS)
    c_in_specs, c_args, c_out_specs, c_out_shapes = _cast_plan(cast, grid)
    state = pltpu.VMEM((HEADS, HEAD_DIM, HEAD_DIM), F32)
    pos_tab = pltpu.VMEM((HEADS, RCHUNK, HEAD_DIM), F32)
    return pl.pallas_call(
        functools.partial(_ret_kernel, ctx_out=ctx_out, n_cast=len(cast)),
        grid=grid,
        in_specs=in_specs + c_in_specs,
        out_specs=out_specs + tuple(c_out_specs),
        out_shape=out_shape + tuple(c_out_shapes),
        scratch_shapes=[
            pltpu.VMEM((HEADS, RCHUNK, RCHUNK), F32),
            pos_tab, pos_tab, pos_tab, pos_tab,
            state, state,
            state, state,
            pltpu.VMEM((N_RCHUNKS, HEADS, HEAD_DIM, HEAD_DIM), BF16),
        ],
        compiler_params=_params(3),
        name="retention",
    )(*args, *c_args)


def _dft_cos_sin(n):
    idx = np.arange(n, dtype=np.float64)
    ang = 2.0 * np.pi * ((idx[:, None] * idx[None, :]) % n) / n
    return np.cos(ang), np.sin(ang)


def _channel_dft(scale):
    c, s = _dft_cos_sin(FNET_GROUP)
    eye = np.eye(FNET_W // FNET_GROUP)
    return np.concatenate([np.kron(eye, c), np.kron(eye, s)], axis=0) * scale


def _fft_constants():
    k1 = np.arange(FFT_N1, dtype=np.float64)[None, :, None]
    n1 = np.arange(FFT_N1, dtype=np.float64)[None, None, :]
    n2 = np.arange(FFT_N2, dtype=np.float64)[:, None, None]
    ang = 2.0 * np.pi * ((k1 * (FFT_N2 * n1 + n2)) % SEQ) / SEQ
    m1 = np.concatenate([np.cos(ang), -np.sin(ang)], axis=1)
    c, s = _dft_cos_sin(FFT_N2)
    m3 = np.block([[c, s], [-s, c]])
    return (_np_split_cat(m1, -1), jnp.asarray(m3, F32),
            jnp.asarray(_channel_dft((SEQ * FNET_GROUP) ** -0.5), F32))


def _ctx_fft_constants():
    c, s = _dft_cos_sin(CTX_LEN)
    return (jnp.asarray(np.concatenate([c, -s], axis=0), F32),
            jnp.asarray(_channel_dft((CTX_LEN * FNET_GROUP) ** -0.5), F32))


def _np_split_cat(a64, axis):
    hi, lo = _np_split(a64)
    return jnp.asarray(np.concatenate([hi, lo], axis=axis))


def _twice(x, axis):
    return jnp.concatenate([x, x], axis=axis)


def _fft1_kernel(x_ref, m_ref, o_ref):
    for jj in range(FFT_STEP):
        x = jnp.concatenate([x_ref[hf, pl.ds(jj, FFT_N1, stride=FFT_STEP), :] for hf in range(2)],
                            axis=1).astype(BF16)
        a = _dot(m_ref[jj], _twice(x, 0))
        for hf in range(2):
            for ri in range(2):
                for kb in range(FFT_N1 // FFT_STEP):
                    r0 = ri * FFT_N1 + kb * FFT_STEP
                    o_ref[hf, ri, kb, jj * FFT_STEP:(jj + 1) * FFT_STEP, :] = (
                        a[r0:r0 + FFT_STEP, hf * HALF:(hf + 1) * HALF])


def _fft3_kernel(a_ref, w_ref, cd_ref, o_ref):
    w = w_ref[...].astype(BF16)
    zs = []
    for jj in range(FFT_STEP):
        rhs = jnp.concatenate(
            [jnp.concatenate([a_ref[hf, ri, pl.ds(jj, FFT_N2, stride=FFT_STEP), :] for ri in range(2)], axis=0)
             for hf in range(2)], axis=1).astype(BF16)
        y = _dot(w, rhs)
        zs.append(jnp.concatenate([y[:FFT_N2], y[FFT_N2:]], axis=1).astype(BF16))
    z = jnp.concatenate(zs, axis=0)
    four = _dot(z, cd_ref[...].astype(BF16))
    for jj in range(FFT_STEP):
        for kt in range(FFT_N2 // MIX_K2):
            for hf in range(2):
                r0 = jj * FFT_N2 + kt * MIX_K2
                o_ref[hf, kt, jj * MIX_K2:(jj + 1) * MIX_K2, :] = four[r0:r0 + MIX_K2, hf * HALF:(hf + 1) * HALF]


def _fourier_latent(f_blk):
    nb = f_blk.shape[0]
    m1, m3, cd = _fft_constants()
    n_kb = FFT_N1 // FFT_STEP
    a = pl.pallas_call(
        _fft1_kernel,
        grid=(nb, FFT_N2 // FFT_STEP),
        in_specs=[
            pl.BlockSpec((None, 2, None, FFT_N1 * FFT_STEP, HALF), lambda b, j: (b, 0, j, 0, 0)),
            pl.BlockSpec((FFT_STEP, 2 * FFT_N1, 2 * FFT_N1), lambda b, j: (j, 0, 0)),
        ],
        out_specs=pl.BlockSpec((None, 2, 2, n_kb, FFT_STEP * FFT_STEP, HALF), lambda b, j: (b, 0, 0, 0, j, 0)),
        out_shape=jax.ShapeDtypeStruct((nb, 2, 2, n_kb, FFT_N2 * FFT_STEP, HALF), F32),
        compiler_params=_params(2),
        name="fourier_stage1",
    )(f_blk, m1)
    return pl.pallas_call(
        _fft3_kernel,
        grid=(nb, n_kb),
        in_specs=[
            pl.BlockSpec((None, 2, 2, None, FFT_N2 * FFT_STEP, HALF), lambda b, j: (b, 0, 0, j, 0, 0)),
            _const_spec((2 * FFT_N2, 2 * FFT_N2)), _const_spec((2 * FNET_W, FNET_W)),
        ],
        out_specs=pl.BlockSpec((None, 2, FFT_N2 // MIX_K2, FFT_STEP * MIX_K2, HALF), lambda b, j: (b, 0, 0, j, 0)),
        out_shape=jax.ShapeDtypeStruct((nb, 2, FFT_N2 // MIX_K2, FFT_N1 * MIX_K2, HALF), F32),
        compiler_params=_params(2),
        name="fourier_stage2",
    )(a, m3, cd)


def _fft_ctx_kernel(x_ref, w_ref, cd_ref, o_ref):
    x = jnp.concatenate([x_ref[hf] for hf in range(2)], axis=1).astype(BF16)
    z = _dot(w_ref[...].astype(BF16), x)
    z = jnp.concatenate([z[:CTX_LEN], z[CTX_LEN:]], axis=1).astype(BF16)
    four = _dot(z, cd_ref[...].astype(BF16))
    for hf in range(2):
        o_ref[hf] = four[:, hf * HALF:(hf + 1) * HALF]


def _fourier_ctx(f_ctx):
    nb = f_ctx.shape[0]
    w, cd = _ctx_fft_constants()
    return pl.pallas_call(
        _fft_ctx_kernel,
        grid=(nb,),
        in_specs=[
            pl.BlockSpec((None, 2, CTX_LEN, HALF), lambda b: (b, 0, 0, 0)),
            _const_spec((2 * CTX_LEN, CTX_LEN)), _const_spec((2 * FNET_W, FNET_W)),
        ],
        out_specs=pl.BlockSpec((None, 2, CTX_LEN, HALF), lambda b: (b, 0, 0, 0)),
        out_shape=jax.ShapeDtypeStruct((nb, 2, CTX_LEN, HALF), F32),
        compiler_params=_params(1),
        name="fourier_ctx",
    )(f_ctx, w, cd)


def _mix_mlp_kernel(*refs, tm, ctx, n_cast):
    (x_ref, ret_ref, four_ref, u_ref, vn_ref, gate_ref, mod_ref, ws_ref, bs_ref,
     wa_ref, wb_ref, wc_ref, wo_ref, gpm_ref, gprm_ref, gpom_ref, wup_ref, wdn_ref) = refs[:18]
    o_ref = refs[18 + n_cast]
    _cast_chunks(refs[18:18 + n_cast], refs[19 + n_cast:])
    group = lax.broadcasted_iota(jnp.int32, (GMLP_CHUNK, GMLP_W), 1) // GMLP_GROUP
    parts = []
    for cc in range(tm // GMLP_CHUNK):
        rows = slice(cc * GMLP_CHUNK, (cc + 1) * GMLP_CHUNK)
        vn = vn_ref[0, rows, :]
        s = bs_ref[...]
        for g in range(GMLP_W // GMLP_GROUP):
            s = s + jnp.where(group == g, _dot(ws_ref[g], vn), 0.0)
        parts.append((u_ref[0, rows, :].astype(F32) * s).astype(BF16))
    sgu = jnp.concatenate(parts, axis=0)

    if ctx:
        four = jnp.concatenate([four_ref[0, hf] for hf in range(2)], axis=1).astype(BF16)
    else:
        nk2 = tm // FFT_N1
        four = jnp.concatenate(
            [jnp.concatenate([four_ref[0, hf, pl.ds(k2, FFT_N1, stride=nk2), :] for k2 in range(nk2)], axis=0)
             for hf in range(2)], axis=1).astype(BF16)
    ga = gate_ref[0, :, 0:D_MODEL].astype(F32)
    gb = gate_ref[0, :, D_MODEL:2 * D_MODEL].astype(F32)
    gc = gate_ref[0, :, 2 * D_MODEL:3 * D_MODEL].astype(F32)
    merged = (ga * _dot(ret_ref[0], wa_ref[...]) + gb * _dot(four, wb_ref[...])
              + gc * _dot(sgu, wc_ref[...])).astype(BF16)

    gain1 = mod_ref[0, :, 2 * D_MODEL:3 * D_MODEL] * gpm_ref[...]
    shift2 = mod_ref[0, :, 3 * D_MODEL:4 * D_MODEL]
    gain2 = gprm_ref[...] * (1.0 + mod_ref[0, :, 4 * D_MODEL:5 * D_MODEL])
    gain3 = mod_ref[0, :, 5 * D_MODEL:6 * D_MODEL] * gpom_ref[...]

    halves = [slice(r * (tm // 2), (r + 1) * (tm // 2)) for r in range(2)]
    x1, h2 = [], []
    for rows in halves:
        y = _dot(merged[rows], wo_ref[...])
        x1.append(x_ref[0, rows, :] + _rms(y) * gain1)
        h2.append((_rms(x1[-1]) * gain2 + shift2).astype(BF16))
    up0 = jnp.concatenate([_dot(h, wup_ref[:, 0:FF_CHUNK]) for h in h2], axis=0)
    h2 = jnp.concatenate(h2, axis=0)

    n_ff = D_FF // FF_CHUNK
    m = None
    for c in range(n_ff - 1):
        cols = slice(c * FF_CHUNK, (c + 1) * FF_CHUNK)
        a = jnp.maximum(up0 if c == 0 else _dot(h2, wup_ref[:, cols]), 0.0)
        part = _dot((a * a).astype(BF16), wdn_ref[cols, :])
        m = part if m is None else m + part
    cols = slice((n_ff - 1) * FF_CHUNK, n_ff * FF_CHUNK)
    a = jnp.maximum(_dot(h2, wup_ref[:, cols]), 0.0)
    a = (a * a).astype(BF16)
    for r, rows in enumerate(halves):
        mr = m[rows] + _dot(a[rows], wdn_ref[cols, :])
        o_ref[0, rows, :] = x1[r] + _rms(mr) * gain3


def _mix_mlp(xs, ret, four, ug, vn, gates, layer, mod, ws, bs_tab, wa, wb, wc, wo,
             g_post_mix, g_pre_mlp, g_post_mlp, w_up, w_down, *, tm, ctx, cast=()):
    nb, rows, _ = xs.shape
    tok_spec = lambda w: pl.BlockSpec((1, tm, w), lambda b, i: (b, i, 0))
    lspec = lambda *shape: _layer_spec(shape, layer)
    if ctx:
        four_spec = pl.BlockSpec((1, 2, tm, HALF), lambda b, i: (b, 0, i, 0))
    else:
        four_spec = pl.BlockSpec((1, 2, None, tm, HALF), lambda b, i: (b, 0, i, 0, 0))
    c_in_specs, c_args, c_out_specs, c_out_shapes = _cast_plan(cast, (nb, rows // tm))
    outs = pl.pallas_call(
        functools.partial(_mix_mlp_kernel, tm=tm, ctx=ctx, n_cast=len(cast)),
        grid=(nb, rows // tm),
        in_specs=[
            tok_spec(D_MODEL), tok_spec(RET_W),
            four_spec,
            tok_spec(GMLP_W), tok_spec(GMLP_W), tok_spec(3 * D_MODEL),
            _mod_spec(layer, ctx),
            lspec(GMLP_W // GMLP_GROUP, GMLP_CHUNK, GMLP_CHUNK),
            lspec(GMLP_CHUNK, GMLP_W),
            _const_spec((RET_W, D_MODEL)), _const_spec((FNET_W, D_MODEL)), _const_spec((GMLP_W, D_MODEL)),
            _const_spec((D_MODEL, D_MODEL)),
            lspec(1, D_MODEL), lspec(1, D_MODEL), lspec(1, D_MODEL),
            _const_spec((D_MODEL, D_FF)), _const_spec((D_FF, D_MODEL)),
        ] + c_in_specs,
        out_specs=(tok_spec(D_MODEL),) + tuple(c_out_specs),
        out_shape=(jax.ShapeDtypeStruct((nb, rows, D_MODEL), F32),) + tuple(c_out_shapes),
        compiler_params=_params(2),
        name="mix_mlp_ctx" if ctx else "mix_mlp",
    )(xs, ret, four, ug, vn, gates, mod, ws, bs_tab, wa, wb, wc, wo,
      g_post_mix, g_pre_mlp, g_post_mlp, w_up, w_down, *c_args)
    return outs if cast else outs[0]


def _rope_tables():
    rows = SEQ // GRID_W
    freqs = ROPE_BASE ** (-jnp.arange(ROPE_FREQS, dtype=F32) / ROPE_FREQS)
    ang_r = jnp.arange(rows, dtype=F32)[:, None] * freqs
    ang_c = jnp.arange(GRID_W, dtype=F32)[:, None] * freqs

    def lanes(a, first):
        z = jnp.zeros_like(a)
        return jnp.concatenate([a, z] if first else [z, a], axis=-1)

    def pair(ang, first):
        cos, sin = lanes(jnp.cos(ang), first), lanes(jnp.sin(ang), first)
        return jnp.stack([jnp.concatenate([cos, cos], axis=-1), jnp.concatenate([-sin, sin], axis=-1)])

    return pair(ang_r, True), pair(ang_c, False)


def kernel(x, c, ctx, c_ctx, w_mod, b_mod, g_pre_mix, g_post_mix, g_pre_mlp, g_post_mlp, w_in,
           ret_decay_logit, sgu_w_s, sgu_b_s, sgu_norm, w_branch_a, w_branch_b, w_branch_c, w_out,
           w_up, w_down):
    nb = x.shape[0]
    depth = w_mod.shape[0]
    assert x.shape == (nb, SEQ, D_MODEL) and ctx.shape == (nb, CTX_LEN, D_MODEL) and nb == 2

    cond_rows = jnp.concatenate([c, c_ctx[None, :], jnp.zeros((8 - nb - 1, D_MODEL), F32)], axis=0)
    mod = _modulation(cond_rows, w_mod, b_mod).reshape(depth, 8, 1, 6 * D_MODEL)
    rope_tabs = _rope_tables()
    pavg = jnp.asarray(np.kron(np.eye(GMLP_W // GMLP_GROUP), np.full((GMLP_GROUP, GMLP_GROUP), 1.0 / GMLP_GROUP)), BF16)

    vec = lambda p: p[:, None, :]
    g_pre, g_norm = vec(g_pre_mix), vec(sgu_norm)
    logits = ret_decay_logit.reshape(depth, 2 * HEADS, 1)
    ws = sgu_w_s.astype(BF16)
    bs_tab = jnp.repeat(jnp.swapaxes(sgu_b_s, 1, 2), GMLP_GROUP, axis=2)
    gains = (vec(g_post_mix), vec(g_pre_mlp), vec(g_post_mlp))
    w_in_l = w_in[0].astype(BF16)
    for l in range(depth):
        last = l == depth - 1
        in_args = (l, mod, g_pre, w_in_l, pavg, g_norm)
        *proj, wa, wb, wc, wo = _in_proj(x, *in_args, rope_tabs, tm=TM_IN, ctx=False,
                                         cast=[(w, l) for w in (w_branch_a, w_branch_b, w_branch_c, w_out)])
        q, k, v, sg, f_lat, ug, vn, gates = proj
        mlp_cast = [(w_up, l), (w_down, l)]
        if last:
            kc, vc = _in_proj(ctx, *in_args, None, tm=CTX_LEN, ctx=True, kv_only=True)
            ret, wup, wdn = _retention(l, logits, (q, k, v, sg), (kc, vc), ctx_out=False, cast=mlp_cast)
        else:
            qc, kc, vc, sgc, f_ctx, ugc, vnc, gatesc = _in_proj(ctx, *in_args, None, tm=CTX_LEN, ctx=True)
            ret, retc, wup, wdn = _retention(l, logits, (q, k, v, sg), (qc, kc, vc, sgc), ctx_out=True,
                                             cast=mlp_cast)
        mix_w = (l, mod, ws, bs_tab, wa, wb, wc, wo, *gains, wup, wdn)
        four = _fourier_latent(f_lat)
        if last:
            x = _mix_mlp(x, ret, four, ug, vn, gates, *mix_w, tm=TM_LAT, ctx=False)
        else:
            x, w_in_l = _mix_mlp(x, ret, four, ug, vn, gates, *mix_w, tm=TM_LAT, ctx=False, cast=[(w_in, l + 1)])
            ctx = _mix_mlp(ctx, retc, _fourier_ctx(f_ctx), ugc, vnc, gatesc, *mix_w, tm=CTX_LEN, ctx=True)
    return x
```

```python
import functools

import numpy as np
import jax
import jax.numpy as jnp
from jax import lax
from jax.experimental import pallas as pl
from jax.experimental.pallas import tpu as pltpu

F32 = jnp.float32
BF16 = jnp.bfloat16

D_MODEL = 1024
SEQ = 8192
CTX_LEN = 256
GRID_W = 64
RET_W = 512
HEADS = 4
HEAD_DIM = 128
ROPE_BASE = 10000.0
ROPE_FREQS = HEAD_DIM // 4
FNET_W = 256
FNET_GROUP = 64
GMLP_W = 256
GMLP_GROUP = 64
GMLP_CHUNK = 128
D_FF = 4 * D_MODEL
EPS = 1e-6
IN_W = 4 * RET_W + FNET_W + 2 * GMLP_W + 3 * D_MODEL
COL_F = 4 * RET_W
COL_U = COL_F + FNET_W
COL_VS = COL_U + GMLP_W
COL_GATE = COL_VS + GMLP_W

TM_LAT = 512
TM_IN = 1024
RCHUNK = 256
RET_TILE = 2048
RET_STEPS = SEQ // RET_TILE
RET_CPT = RET_TILE // RCHUNK
RET_AHEAD = 4
N_RCHUNKS = 1 + SEQ // RCHUNK
FF_CHUNK = 1024
FFT_N1 = 64
FFT_N2 = 128
HALF = 128
FFT_STEP = 16
MIX_K2 = TM_LAT // FFT_N1

_VMEM_LIMIT = 56 * 1024 * 1024


def _dot(a, b):
    return jnp.dot(a, b, preferred_element_type=F32)


def _split(x):
    hi = x.astype(BF16)
    lo = (x - hi.astype(F32)).astype(BF16)
    return hi, lo


def _np_split(a64):
    hi = np.asarray(a64, np.float32).astype(BF16)
    lo = (np.asarray(a64, np.float32) - hi.astype(np.float32)).astype(BF16)
    return hi, lo


def _rms(x):
    return x * lax.rsqrt(jnp.mean(x * x, axis=-1, keepdims=True) + EPS)


def _gelu(x):
    return x * (0.5 * (1.0 + jnp.tanh(0.7978845608028654 * (x + 0.044715 * (x * x * x)))))


def _sigmoid(x):
    return 1.0 / (1.0 + jnp.exp(-x))


def _const_spec(shape, nargs=None):
    zeros = (0,) * len(shape)
    return pl.BlockSpec(shape, lambda *_: zeros, pipeline_mode=pl.Buffered(1))


def _layer_spec(shape, layer):
    idx = (layer,) + (0,) * len(shape)
    return pl.BlockSpec((None,) + tuple(shape), lambda *_: idx, pipeline_mode=pl.Buffered(1))


def _params(n_axes):
    return pltpu.CompilerParams(dimension_semantics=("arbitrary",) * n_axes, vmem_limit_bytes=_VMEM_LIMIT)


def _mod_kernel(a_ref, w_ref, b_ref, o_ref):
    a = a_ref[...]
    a = a * _sigmoid(a)
    ah, al = _split(a)
    w = w_ref[0].astype(BF16)
    o_ref[0] = _dot(ah, w) + _dot(al, w) + b_ref[0]


def _modulation(cond_rows, w_mod, b_mod):
    depth = w_mod.shape[0]
    tn = 1536
    return pl.pallas_call(
        _mod_kernel,
        grid=(depth, (6 * D_MODEL) // tn),
        in_specs=[
            pl.BlockSpec((8, D_MODEL), lambda l, j: (0, 0)),
            pl.BlockSpec((1, D_MODEL, tn), lambda l, j: (l, 0, j)),
            pl.BlockSpec((1, 1, tn), lambda l, j: (l, 0, j)),
        ],
        out_specs=pl.BlockSpec((1, 8, tn), lambda l, j: (l, 0, j)),
        out_shape=jax.ShapeDtypeStruct((depth, 8, 6 * D_MODEL), F32),
        compiler_params=_params(2),
        name="modulation",
    )(cond_rows, w_mod, b_mod.reshape(depth, 1, 6 * D_MODEL))


def _cast_plan(cast, grid):
    steps = int(np.prod(grid))
    in_specs, args, out_specs, out_shapes = [], [], [], []

    def step(*idx):
        t = idx[0]
        for n, i in zip(grid[1:], idx[1:]):
            t = t * n + i
        return t

    for stack, layer in cast:
        _, rows, cols = stack.shape
        n_chunks = rows // 16
        while n_chunks > steps:
            n_chunks //= 2
        assert rows % n_chunks == 0
        per = steps // n_chunks
        chunk = lambda *idx, per=per, last=n_chunks - 1: jnp.minimum(step(*idx) // per, last)
        in_specs.append(pl.BlockSpec((None, rows // n_chunks, cols),
                                     lambda *idx, layer=layer, chunk=chunk: (layer, chunk(*idx), 0)))
        out_specs.append(pl.BlockSpec((rows // n_chunks, cols), lambda *idx, chunk=chunk: (chunk(*idx), 0)))
        out_shapes.append(jax.ShapeDtypeStruct((rows, cols), BF16))
        args.append(stack)
    return in_specs, args, out_specs, out_shapes


def _cast_chunks(src_refs, dst_refs):
    for src, dst in zip(src_refs, dst_refs, strict=True):
        dst[...] = src[...].astype(BF16)


def _in_proj_kernel(*refs, rope, kv_only, n_cast, pipelined):
    n_in = 8 if rope else 6
    ins, refs = refs[:n_in], refs[n_in:]
    cast_src, refs = refs[:n_cast], refs[n_cast:]
    n_out = 2 if kv_only else 8
    outs, refs = refs[:n_out], refs[n_out:]
    cast_dst, scratch = refs[:n_cast], refs[n_cast:]
    _cast_chunks(cast_src, cast_dst)
    if rope:
        x_ref, mod_ref, gpre_ref, w_ref, pavg_ref, gn_ref, rrow_ref, rcol_ref = ins
    else:
        x_ref, mod_ref, gpre_ref, w_ref, pavg_ref, gn_ref = ins
    tm = x_ref.shape[1]

    def prepare():
        shift = mod_ref[0, :, 0:D_MODEL]
        gain = gpre_ref[...] * (1.0 + mod_ref[0, :, D_MODEL:2 * D_MODEL])
        return (_rms(x_ref[0]) * gain + shift).astype(BF16)

    def project(operand):
        def proj(lo, hi):
            return _dot(operand(), w_ref[:, lo:hi])

        if rope:
            n_rows = tm // GRID_W
            rot = []
            for t in range(2):
                by_row = jnp.concatenate(
                    [jnp.broadcast_to(rrow_ref[t, r:r + 1, :], (GRID_W, HEAD_DIM)) for r in range(n_rows)], axis=0)
                rot.append(by_row + jnp.concatenate([rcol_ref[t]] * n_rows, axis=0))

        def rope_store(z, dst):
            if not rope:
                dst[0] = z.astype(BF16)
                return
            cos, sin = rot
            for hd in range(HEADS):
                a = z[:, hd * HEAD_DIM:(hd + 1) * HEAD_DIM]
                r = a * cos + pltpu.roll(a, HEAD_DIM // 2, 1) * sin
                dst[0, :, hd * HEAD_DIM:(hd + 1) * HEAD_DIM] = r.astype(BF16)

        if kv_only:
            k_ref, v_ref = outs
        else:
            q_ref, k_ref, v_ref, sg_ref, f_ref, u_ref, vn_ref, gate_ref = outs
            rope_store(proj(0, RET_W), q_ref)
        rope_store(proj(RET_W, 2 * RET_W) * (HEAD_DIM ** -0.5), k_ref)
        v_ref[0] = proj(2 * RET_W, 3 * RET_W).astype(BF16)
        if kv_only:
            return
        g = proj(3 * RET_W, 4 * RET_W)
        sg_ref[0] = (g * _sigmoid(g)).astype(BF16)

        f = proj(COL_F, COL_U)
        for hf in range(2):
            fh = f[:, hf * HALF:(hf + 1) * HALF]
            if rope:
                for n1 in range(tm // FFT_N2):
                    for nj in range(FFT_N2 // FFT_STEP):
                        f_ref[0, hf, nj, n1 * FFT_STEP:(n1 + 1) * FFT_STEP, :] = (
                            fh[n1 * FFT_N2 + nj * FFT_STEP:n1 * FFT_N2 + (nj + 1) * FFT_STEP])
            else:
                f_ref[0, hf] = fh

        u_ref[0] = _gelu(proj(COL_U, COL_VS)).astype(BF16)
        vg = _gelu(proj(COL_VS, COL_GATE))
        sh, sl = _split(vg * vg)
        pavg = pavg_ref[...]
        ms = _dot(sh, pavg) + _dot(sl, pavg)
        vn_ref[0] = (vg * lax.rsqrt(ms + EPS) * gn_ref[...]).astype(BF16)

        for c in range(3):
            z = proj(COL_GATE + c * D_MODEL, COL_GATE + (c + 1) * D_MODEL)
            gate_ref[0, :, c * D_MODEL:(c + 1) * D_MODEL] = _sigmoid(z).astype(BF16)

    if not pipelined:
        hb = prepare()
        project(lambda: hb)
        return
    (hbuf,) = scratch
    t = pl.program_id(0)

    @pl.when(t == 0)
    def _():
        hbuf[0] = prepare()

    @pl.when(t > 0)
    def _():
        project(lambda: hbuf[(t - 1) % 2])
        hbuf[t % 2] = prepare()


def _mod_spec(layer, ctx):
    idx = (lambda b, i: (layer, 2, 0, 0)) if ctx else (lambda b, i: (layer, b, 0, 0))
    return pl.BlockSpec((None, 1, 1, 6 * D_MODEL), idx)


def _in_proj(xs, layer, mod, g_pre, w_in, pavg, g_norm, rope_tabs, *, tm, ctx, kv_only=False, cast=()):
    nb, rows, _ = xs.shape
    rope = rope_tabs is not None
    pipelined = not ctx
    nt = rows // tm
    if pipelined:
        grid = (nb * nt + 1,)
        split = lambda tile: (tile // nt, tile % nt)
        rd = lambda t: split(jnp.minimum(t, nb * nt - 1))
        wr = lambda t: split(jnp.maximum(t - 1, 0))
    else:
        grid = (nb, nt)
        rd = wr = lambda b, i: (b, i)
    tok_spec = lambda w, at=wr: pl.BlockSpec((1, tm, w), lambda *g: (*at(*g), 0))
    mod_row = (lambda *g: 2) if ctx else (lambda *g: rd(*g)[0])
    in_specs = [
        tok_spec(D_MODEL, rd),
        pl.BlockSpec((None, 1, 1, 6 * D_MODEL), lambda *g: (layer, mod_row(*g), 0, 0)),
        _layer_spec((1, D_MODEL), layer),
        _const_spec((D_MODEL, IN_W)),
        _const_spec((GMLP_W, GMLP_W)),
        _layer_spec((1, GMLP_W), layer),
    ]
    args = [xs, mod, g_pre, w_in, pavg, g_norm]
    if rope:
        in_specs += [pl.BlockSpec((2, tm // GRID_W, HEAD_DIM), lambda *g: (0, wr(*g)[1], 0)),
                     _const_spec((2, GRID_W, HEAD_DIM))]
        args += list(rope_tabs)
    c_in_specs, c_args, c_out_specs, c_out_shapes = _cast_plan(cast, grid)
    in_specs += c_in_specs
    args += c_args
    bf = lambda w: jax.ShapeDtypeStruct((nb, rows, w), BF16)
    if kv_only:
        out_shapes = (bf(RET_W), bf(RET_W))
        out_specs = (tok_spec(RET_W), tok_spec(RET_W))
    else:
        if rope:
            n_nj = FFT_N2 // FFT_STEP
            f_shape = jax.ShapeDtypeStruct((nb, 2, n_nj, FFT_N1 * FFT_STEP, HALF), F32)
            f_spec = pl.BlockSpec((1, 2, n_nj, (tm // FFT_N2) * FFT_STEP, HALF),
                                  lambda *g: (wr(*g)[0], 0, 0, wr(*g)[1], 0))
        else:
            f_shape = jax.ShapeDtypeStruct((nb, 2, rows, HALF), F32)
            f_spec = pl.BlockSpec((1, 2, tm, HALF), lambda *g: (wr(*g)[0], 0, wr(*g)[1], 0))
        out_shapes = (
            bf(RET_W), bf(RET_W), bf(RET_W), bf(RET_W),
            f_shape,
            bf(GMLP_W), bf(GMLP_W),
            bf(3 * D_MODEL),
        )
        out_specs = (
            tok_spec(RET_W), tok_spec(RET_W), tok_spec(RET_W), tok_spec(RET_W),
            f_spec,
            tok_spec(GMLP_W), tok_spec(GMLP_W), tok_spec(3 * D_MODEL),
        )
    return pl.pallas_call(
        functools.partial(_in_proj_kernel, rope=rope, kv_only=kv_only, n_cast=len(cast), pipelined=pipelined),
        grid=grid,
        in_specs=in_specs,
        out_specs=tuple(out_specs) + tuple(c_out_specs),
        out_shape=tuple(out_shapes) + tuple(c_out_shapes),
        scratch_shapes=[pltpu.VMEM((2, tm, D_MODEL), BF16)] if pipelined else [],
        compiler_params=_params(len(grid)),
        name="in_proj_ctx" if ctx else "in_proj",
    )(*args)


def _ret_kernel(*refs, ctx_out, n_cast):
    n_in, n_out = (9, 2) if ctx_out else (7, 1)
    ins, refs = refs[:n_in], refs[n_in:]
    cast_src, refs = refs[:n_cast], refs[n_cast:]
    outs, refs = refs[:n_out], refs[n_out:]
    cast_dst, scratch = refs[:n_cast], refs[n_cast:]
    _cast_chunks(cast_src, cast_dst)
    (logit_ref, ql_ref, kl_ref, vl_ref, sgl_ref, kc_ref, vc_ref) = ins[:7]
    if ctx_out:
        qc_ref, sgc_ref = ins[7:]
        ol_ref, oc_ref = outs
    else:
        (ol_ref,) = outs
    mask_ref, wf_ref, wb_ref, qf_ref, qb_ref, df_ref, db_ref, sf_ref, sb_ref, sball_ref = scratch
    b = pl.program_id(0)
    phase = pl.program_id(1)
    j = pl.program_id(2)

    @pl.when((b == 0) & (phase == 0) & (j == 0))
    def _():
        x = logit_ref[...]
        lg = -(jnp.maximum(-x, 0.0) + jnp.log(1.0 + jnp.exp(-jnp.abs(x))))
        row = lax.broadcasted_iota(jnp.int32, (RCHUNK, RCHUNK), 0).astype(F32)
        col = lax.broadcasted_iota(jnp.int32, (RCHUNK, RCHUNK), 1).astype(F32)
        diff = row - col
        pos = row[:, :HEAD_DIM]
        for hd in range(HEADS):
            lf = lg[hd:hd + 1, :]
            lb = lg[HEADS + hd:HEADS + hd + 1, :]
            mask_ref[hd] = jnp.where(diff >= 0.0, jnp.exp(lf * jnp.maximum(diff, 0.0)),
                                     jnp.exp(lb * jnp.maximum(-diff, 0.0)))
            wf_ref[hd] = jnp.exp(lf * (RCHUNK - 1.0 - pos))
            wb_ref[hd] = jnp.exp(lb * pos)
            qf_ref[hd] = jnp.exp(lf * (pos + 1.0))
            qb_ref[hd] = jnp.exp(lb * (RCHUNK - pos))
            df_ref[hd] = jnp.exp(jnp.broadcast_to(lf, (HEAD_DIM, HEAD_DIM)) * float(RCHUNK))
            db_ref[hd] = jnp.exp(jnp.broadcast_to(lb, (HEAD_DIM, HEAD_DIM)) * float(RCHUNK))

    def chunk_updates(hd, k_ref, v_ref, n_chunks, w_ref):
        cols = slice(hd * HEAD_DIM, (hd + 1) * HEAD_DIM)
        us = []
        for c in range(n_chunks):
            rows = slice(c * RCHUNK, (c + 1) * RCHUNK)
            kw = (k_ref[0, rows, cols].astype(F32) * w_ref[hd]).astype(BF16)
            us.append(lax.dot_general(kw, v_ref[0, rows, cols], (((0,), (0,)), ((), ())),
                                      preferred_element_type=F32))
        return us

    def backward_chunks(k_ref, v_ref, n_chunks, first_chunk):
        for hd in range(HEADS):
            us = chunk_updates(hd, k_ref, v_ref, n_chunks, wb_ref)
            s = sb_ref[hd]
            for c in reversed(range(n_chunks)):
                sball_ref[first_chunk + c, hd] = s.astype(BF16)
                s = db_ref[hd] * s + us[c]
            sb_ref[hd] = s

    def forward_chunks(q_ref, k_ref, v_ref, sg_ref, o_ref, n_chunks, first_chunk):
        starts = []
        for hd in range(HEADS):
            us = chunk_updates(hd, k_ref, v_ref, n_chunks, wf_ref)
            st = [sf_ref[hd]]
            for c in range(n_chunks):
                st.append(df_ref[hd] * st[-1] + us[c])
            sf_ref[hd] = st[-1]
            starts.append(st)
        if o_ref is None:
            return
        units = [(c, hd) for c in range(n_chunks) for hd in range(HEADS)]

        def scores(c, hd):
            rows = slice(c * RCHUNK, (c + 1) * RCHUNK)
            cols = slice(hd * HEAD_DIM, (hd + 1) * HEAD_DIM)
            return lax.dot_general(q_ref[0, rows, cols], k_ref[0, rows, cols], (((1,), (1,)), ((), ())),
                                   preferred_element_type=F32)

        ahead = min(RET_AHEAD, len(units))
        pending = [scores(*u) for u in units[:ahead]]
        for n, (c, hd) in enumerate(units):
            if n + ahead < len(units):
                pending.append(scores(*units[n + ahead]))
            sc = pending.pop(0)
            rows = slice(c * RCHUNK, (c + 1) * RCHUNK)
            cols = slice(hd * HEAD_DIM, (hd + 1) * HEAD_DIM)
            qf = q_ref[0, rows, cols].astype(F32)
            lhs = jnp.concatenate([(sc * mask_ref[hd]).astype(BF16),
                                   (qf * qf_ref[hd]).astype(BF16),
                                   (qf * qb_ref[hd]).astype(BF16)], axis=1)
            rhs = jnp.concatenate([v_ref[0, rows, cols], starts[hd][c].astype(BF16),
                                   sball_ref[first_chunk + c, hd]], axis=0)
            o = _dot(lhs, rhs)
            o_ref[0, rows, cols] = (_rms(o) * sg_ref[0, rows, cols].astype(F32)).astype(BF16)

    @pl.when(phase == 0)
    def _():
        @pl.when(j == 0)
        def _():
            sb_ref[...] = jnp.zeros_like(sb_ref)
            backward_chunks(kc_ref, vc_ref, 1, 0)

        backward_chunks(kl_ref, vl_ref, RET_CPT, 1 + (RET_STEPS - 1 - j) * RET_CPT)

    @pl.when(phase == 1)
    def _():
        @pl.when(j == 0)
        def _():
            sf_ref[...] = jnp.zeros_like(sf_ref)
            if ctx_out:
                forward_chunks(qc_ref, kc_ref, vc_ref, sgc_ref, oc_ref, 1, 0)
            else:
                forward_chunks(None, kc_ref, vc_ref, None, None, 1, 0)

        forward_chunks(ql_ref, kl_ref, vl_ref, sgl_ref, ol_ref, RET_CPT, 1 + j * RET_CPT)


def _retention(layer, logits, lat, ctx, *, ctx_out, cast=()):
    nb = lat[0].shape[0]
    kv_idx = lambda b, p, j: (b, jnp.where(p == 0, RET_STEPS - 1 - j, j), 0)
    q_idx = lambda b, p, j: (b, jnp.where(p == 0, 0, j), 0)
    c_idx = lambda b, p, j: (b, 0, 0)
    lat_blk = (1, RET_TILE, RET_W)
    ctx_blk = (1, CTX_LEN, RET_W)
    in_specs = [
        _layer_spec((2 * HEADS, 1), layer),
        pl.BlockSpec(lat_blk, q_idx), pl.BlockSpec(lat_blk, kv_idx), pl.BlockSpec(lat_blk, kv_idx),
        pl.BlockSpec(lat_blk, q_idx),
        pl.BlockSpec(ctx_blk, c_idx), pl.BlockSpec(ctx_blk, c_idx),
    ]
    if ctx_out:
        qc, kc, vc, sgc = ctx
        args = [logits, *lat, kc, vc, qc, sgc]
        in_specs += [pl.BlockSpec(ctx_blk, c_idx), pl.BlockSpec(ctx_blk, c_idx)]
        out_specs = (pl.BlockSpec(lat_blk, q_idx), pl.BlockSpec(ctx_blk, c_idx))
        out_shape = (jax.ShapeDtypeStruct((nb, SEQ, RET_W), BF16), jax.ShapeDtypeStruct((nb, CTX_LEN, RET_W), BF16))
    else:
        kc, vc = ctx
        args = [logits, *lat, kc, vc]
        out_specs = (pl.BlockSpec(lat_blk, q_idx),)
        out_shape = (jax.ShapeDtypeStruct((nb, SEQ, RET_W), BF16),)
    grid = (nb, 2, RET_STEPS)
    c_in_specs, c_args, c_out_specs, c_out_shapes = _cast_plan(cast, grid)
    state = pltpu.VMEM((HEADS, HEAD_DIM, HEAD_DIM), F32)
    pos_tab = pltpu.VMEM((HEADS, RCHUNK, HEAD_DIM), F32)
    return pl.pallas_call(
        functools.partial(_ret_kernel, ctx_out=ctx_out, n_cast=len(cast)),
        grid=grid,
        in_specs=in_specs + c_in_specs,
        out_specs=out_specs + tuple(c_out_specs),
        out_shape=out_shape + tuple(c_out_shapes),
        scratch_shapes=[
            pltpu.VMEM((HEADS, RCHUNK, RCHUNK), F32),
            pos_tab, pos_tab, pos_tab, pos_tab,
            state, state,
            state, state,
            pltpu.VMEM((N_RCHUNKS, HEADS, HEAD_DIM, HEAD_DIM), BF16),
        ],
        compiler_params=_params(3),
        name="retention",
    )(*args, *c_args)


def _dft_cos_sin(n):
    idx = np.arange(n, dtype=np.float64)
    ang = 2.0 * np.pi * ((idx[:, None] * idx[None, :]) % n) / n
    return np.cos(ang), np.sin(ang)


def _channel_dft(scale):
    c, s = _dft_cos_sin(FNET_GROUP)
    eye = np.eye(FNET_W // FNET_GROUP)
    return np.concatenate([np.kron(eye, c), np.kron(eye, s)], axis=0) * scale


def _fft_constants():
    k1 = np.arange(FFT_N1, dtype=np.float64)[None, :, None]
    n1 = np.arange(FFT_N1, dtype=np.float64)[None, None, :]
    n2 = np.arange(FFT_N2, dtype=np.float64)[:, None, None]
    ang = 2.0 * np.pi * ((k1 * (FFT_N2 * n1 + n2)) % SEQ) / SEQ
    m1 = np.concatenate([np.cos(ang), -np.sin(ang)], axis=1)
    c, s = _dft_cos_sin(FFT_N2)
    m3 = np.block([[c, s], [-s, c]])
    return (_np_split_cat(m1, -1), jnp.asarray(m3, F32),
            jnp.asarray(_channel_dft((SEQ * FNET_GROUP) ** -0.5), F32))


def _ctx_fft_constants():
    c, s = _dft_cos_sin(CTX_LEN)
    return (jnp.asarray(np.concatenate([c, -s], axis=0), F32),
            jnp.asarray(_channel_dft((CTX_LEN * FNET_GROUP) ** -0.5), F32))


def _np_split_cat(a64, axis):
    hi, lo = _np_split(a64)
    return jnp.asarray(np.concatenate([hi, lo], axis=axis))


def _twice(x, axis):
    return jnp.concatenate([x, x], axis=axis)


def _fft1_kernel(x_ref, m_ref, o_ref):
    for jj in range(FFT_STEP):
        x = jnp.concatenate([x_ref[hf, pl.ds(jj, FFT_N1, stride=FFT_STEP), :] for hf in range(2)],
                            axis=1).astype(BF16)
        a = _dot(m_ref[jj], _twice(x, 0))
        for hf in range(2):
            for ri in range(2):
                for kb in range(FFT_N1 // FFT_STEP):
                    r0 = ri * FFT_N1 + kb * FFT_STEP
                    o_ref[hf, ri, kb, jj * FFT_STEP:(jj + 1) * FFT_STEP, :] = (
                        a[r0:r0 + FFT_STEP, hf * HALF:(hf + 1) * HALF])


def _fft3_kernel(a_ref, w_ref, cd_ref, o_ref):
    w = w_ref[...].astype(BF16)
    zs = []
    for jj in range(FFT_STEP):
        rhs = jnp.concatenate(
            [jnp.concatenate([a_ref[hf, ri, pl.ds(jj, FFT_N2, stride=FFT_STEP), :] for ri in range(2)], axis=0)
             for hf in range(2)], axis=1).astype(BF16)
        y = _dot(w, rhs)
        zs.append(jnp.concatenate([y[:FFT_N2], y[FFT_N2:]], axis=1).astype(BF16))
    z = jnp.concatenate(zs, axis=0)
    four = _dot(z, cd_ref[...].astype(BF16))
    for jj in range(FFT_STEP):
        for kt in range(FFT_N2 // MIX_K2):
            for hf in range(2):
                r0 = jj * FFT_N2 + kt * MIX_K2
                o_ref[hf, kt, jj * MIX_K2:(jj + 1) * MIX_K2, :] = four[r0:r0 + MIX_K2, hf * HALF:(hf + 1) * HALF]


def _fourier_latent(f_blk):
    nb = f_blk.shape[0]
    m1, m3, cd = _fft_constants()
    n_kb = FFT_N1 // FFT_STEP
    a = pl.pallas_call(
        _fft1_kernel,
        grid=(nb, FFT_N2 // FFT_STEP),
        in_specs=[
            pl.BlockSpec((None, 2, None, FFT_N1 * FFT_STEP, HALF), lambda b, j: (b, 0, j, 0, 0)),
            pl.BlockSpec((FFT_STEP, 2 * FFT_N1, 2 * FFT_N1), lambda b, j: (j, 0, 0)),
        ],
        out_specs=pl.BlockSpec((None, 2, 2, n_kb, FFT_STEP * FFT_STEP, HALF), lambda b, j: (b, 0, 0, 0, j, 0)),
        out_shape=jax.ShapeDtypeStruct((nb, 2, 2, n_kb, FFT_N2 * FFT_STEP, HALF), F32),
        compiler_params=_params(2),
        name="fourier_stage1",
    )(f_blk, m1)
    return pl.pallas_call(
        _fft3_kernel,
        grid=(nb, n_kb),
        in_specs=[
            pl.BlockSpec((None, 2, 2, None, FFT_N2 * FFT_STEP, HALF), lambda b, j: (b, 0, 0, j, 0, 0)),
            _const_spec((2 * FFT_N2, 2 * FFT_N2)), _const_spec((2 * FNET_W, FNET_W)),
        ],
        out_specs=pl.BlockSpec((None, 2, FFT_N2 // MIX_K2, FFT_STEP * MIX_K2, HALF), lambda b, j: (b, 0, 0, j, 0)),
        out_shape=jax.ShapeDtypeStruct((nb, 2, FFT_N2 // MIX_K2, FFT_N1 * MIX_K2, HALF), F32),
        compiler_params=_params(2),
        name="fourier_stage2",
    )(a, m3, cd)


def _fft_ctx_kernel(x_ref, w_ref, cd_ref, o_ref):
    x = jnp.concatenate([x_ref[hf] for hf in range(2)], axis=1).astype(BF16)
    z = _dot(w_ref[...].astype(BF16), x)
    z = jnp.concatenate([z[:CTX_LEN], z[CTX_LEN:]], axis=1).astype(BF16)
    four = _dot(z, cd_ref[...].astype(BF16))
    for hf in range(2):
        o_ref[hf] = four[:, hf * HALF:(hf + 1) * HALF]


def _fourier_ctx(f_ctx):
    nb = f_ctx.shape[0]
    w, cd = _ctx_fft_constants()
    return pl.pallas_call(
        _fft_ctx_kernel,
        grid=(nb,),
        in_specs=[
            pl.BlockSpec((None, 2, CTX_LEN, HALF), lambda b: (b, 0, 0, 0)),
            _const_spec((2 * CTX_LEN, CTX_LEN)), _const_spec((2 * FNET_W, FNET_W)),
        ],
        out_specs=pl.BlockSpec((None, 2, CTX_LEN, HALF), lambda b: (b, 0, 0, 0)),
        out_shape=jax.ShapeDtypeStruct((nb, 2, CTX_LEN, HALF), F32),
        compiler_params=_params(1),
        name="fourier_ctx",
    )(f_ctx, w, cd)


def _mix_mlp_kernel(*refs, tm, ctx, n_cast):
    (x_ref, ret_ref, four_ref, u_ref, vn_ref, gate_ref, mod_ref, ws_ref, bs_ref,
     wa_ref, wb_ref, wc_ref, wo_ref, gpm_ref, gprm_ref, gpom_ref, wup_ref, wdn_ref) = refs[:18]
    o_ref = refs[18 + n_cast]
    _cast_chunks(refs[18:18 + n_cast], refs[19 + n_cast:])
    group = lax.broadcasted_iota(jnp.int32, (GMLP_CHUNK, GMLP_W), 1) // GMLP_GROUP
    parts = []
    for cc in range(tm // GMLP_CHUNK):
        rows = slice(cc * GMLP_CHUNK, (cc + 1) * GMLP_CHUNK)
        vn = vn_ref[0, rows, :]
        s = bs_ref[...]
        for g in range(GMLP_W // GMLP_GROUP):
            s = s + jnp.where(group == g, _dot(ws_ref[g], vn), 0.0)
        parts.append((u_ref[0, rows, :].astype(F32) * s).astype(BF16))
    sgu = jnp.concatenate(parts, axis=0)

    if ctx:
        four = jnp.concatenate([four_ref[0, hf] for hf in range(2)], axis=1).astype(BF16)
    else:
        nk2 = tm // FFT_N1
        four = jnp.concatenate(
            [jnp.concatenate([four_ref[0, hf, pl.ds(k2, FFT_N1, stride=nk2), :] for k2 in range(nk2)], axis=0)
             for hf in range(2)], axis=1).astype(BF16)
    ga = gate_ref[0, :, 0:D_MODEL].astype(F32)
    gb = gate_ref[0, :, D_MODEL:2 * D_MODEL].astype(F32)
    gc = gate_ref[0, :, 2 * D_MODEL:3 * D_MODEL].astype(F32)
    merged = (ga * _dot(ret_ref[0], wa_ref[...]) + gb * _dot(four, wb_ref[...])
              + gc * _dot(sgu, wc_ref[...])).astype(BF16)

    gain1 = mod_ref[0, :, 2 * D_MODEL:3 * D_MODEL] * gpm_ref[...]
    shift2 = mod_ref[0, :, 3 * D_MODEL:4 * D_MODEL]
    gain2 = gprm_ref[...] * (1.0 + mod_ref[0, :, 4 * D_MODEL:5 * D_MODEL])
    gain3 = mod_ref[0, :, 5 * D_MODEL:6 * D_MODEL] * gpom_ref[...]

    halves = [slice(r * (tm // 2), (r + 1) * (tm // 2)) for r in range(2)]
    x1, h2 = [], []
    for rows in halves:
        y = _dot(merged[rows], wo_ref[...])
        x1.append(x_ref[0, rows, :] + _rms(y) * gain1)
        h2.append((_rms(x1[-1]) * gain2 + shift2).astype(BF16))
    up0 = jnp.concatenate([_dot(h, wup_ref[:, 0:FF_CHUNK]) for h in h2], axis=0)
    h2 = jnp.concatenate(h2, axis=0)

    n_ff = D_FF // FF_CHUNK
    m = None
    for c in range(n_ff - 1):
        cols = slice(c * FF_CHUNK, (c + 1) * FF_CHUNK)
        a = jnp.maximum(up0 if c == 0 else _dot(h2, wup_ref[:, cols]), 0.0)
        part = _dot((a * a).astype(BF16), wdn_ref[cols, :])
        m = part if m is None else m + part
    cols = slice((n_ff - 1) * FF_CHUNK, n_ff * FF_CHUNK)
    a = jnp.maximum(_dot(h2, wup_ref[:, cols]), 0.0)
    a = (a * a).astype(BF16)
    for r, rows in enumerate(halves):
        mr = m[rows] + _dot(a[rows], wdn_ref[cols, :])
        o_ref[0, rows, :] = x1[r] + _rms(mr) * gain3


def _mix_mlp(xs, ret, four, ug, vn, gates, layer, mod, ws, bs_tab, wa, wb, wc, wo,
             g_post_mix, g_pre_mlp, g_post_mlp, w_up, w_down, *, tm, ctx, cast=()):
    nb, rows, _ = xs.shape
    tok_spec = lambda w: pl.BlockSpec((1, tm, w), lambda b, i: (b, i, 0))
    lspec = lambda *shape: _layer_spec(shape, layer)
    if ctx:
        four_spec = pl.BlockSpec((1, 2, tm, HALF), lambda b, i: (b, 0, i, 0))
    else:
        four_spec = pl.BlockSpec((1, 2, None, tm, HALF), lambda b, i: (b, 0, i, 0, 0))
    c_in_specs, c_args, c_out_specs, c_out_shapes = _cast_plan(cast, (nb, rows // tm))
    outs = pl.pallas_call(
        functools.partial(_mix_mlp_kernel, tm=tm, ctx=ctx, n_cast=len(cast)),
        grid=(nb, rows // tm),
        in_specs=[
            tok_spec(D_MODEL), tok_spec(RET_W),
            four_spec,
            tok_spec(GMLP_W), tok_spec(GMLP_W), tok_spec(3 * D_MODEL),
            _mod_spec(layer, ctx),
            lspec(GMLP_W // GMLP_GROUP, GMLP_CHUNK, GMLP_CHUNK),
            lspec(GMLP_CHUNK, GMLP_W),
            _const_spec((RET_W, D_MODEL)), _const_spec((FNET_W, D_MODEL)), _const_spec((GMLP_W, D_MODEL)),
            _const_spec((D_MODEL, D_MODEL)),
            lspec(1, D_MODEL), lspec(1, D_MODEL), lspec(1, D_MODEL),
            _const_spec((D_MODEL, D_FF)), _const_spec((D_FF, D_MODEL)),
        ] + c_in_specs,
        out_specs=(tok_spec(D_MODEL),) + tuple(c_out_specs),
        out_shape=(jax.ShapeDtypeStruct((nb, rows, D_MODEL), F32),) + tuple(c_out_shapes),
        compiler_params=_params(2),
        name="mix_mlp_ctx" if ctx else "mix_mlp",
    )(xs, ret, four, ug, vn, gates, mod, ws, bs_tab, wa, wb, wc, wo,
      g_post_mix, g_pre_mlp, g_post_mlp, w_up, w_down, *c_args)
    return outs if cast else outs[0]


def _rope_tables():
    rows = SEQ // GRID_W
    freqs = ROPE_BASE ** (-jnp.arange(ROPE_FREQS, dtype=F32) / ROPE_FREQS)
    ang_r = jnp.arange(rows, dtype=F32)[:, None] * freqs
    ang_c = jnp.arange(GRID_W, dtype=F32)[:, None] * freqs

    def lanes(a, first):
        z = jnp.zeros_like(a)
        return jnp.concatenate([a, z] if first else [z, a], axis=-1)

    def pair(ang, first):
        cos, sin = lanes(jnp.cos(ang), first), lanes(jnp.sin(ang), first)
        return jnp.stack([jnp.concatenate([cos, cos], axis=-1), jnp.concatenate([-sin, sin], axis=-1)])

    return pair(ang_r, True), pair(ang_c, False)


def kernel(x, c, ctx, c_ctx, w_mod, b_mod, g_pre_mix, g_post_mix, g_pre_mlp, g_post_mlp, w_in,
           ret_decay_logit, sgu_w_s, sgu_b_s, sgu_norm, w_branch_a, w_branch_b, w_branch_c, w_out,
           w_up, w_down):
    nb = x.shape[0]
    depth = w_mod.shape[0]
    assert x.shape == (nb, SEQ, D_MODEL) and ctx.shape == (nb, CTX_LEN, D_MODEL) and nb == 2

    cond_rows = jnp.concatenate([c, c_ctx[None, :], jnp.zeros((8 - nb - 1, D_MODEL), F32)], axis=0)
    mod = _modulation(cond_rows, w_mod, b_mod).reshape(depth, 8, 1, 6 * D_MODEL)
    rope_tabs = _rope_tables()
    pavg = jnp.asarray(np.kron(np.eye(GMLP_W // GMLP_GROUP), np.full((GMLP_GROUP, GMLP_GROUP), 1.0 / GMLP_GROUP)), BF16)

    vec = lambda p: p[:, None, :]
    g_pre, g_norm = vec(g_pre_mix), vec(sgu_norm)
    logits = ret_decay_logit.reshape(depth, 2 * HEADS, 1)
    ws = sgu_w_s.astype(BF16)
    bs_tab = jnp.repeat(jnp.swapaxes(sgu_b_s, 1, 2), GMLP_GROUP, axis=2)
    gains = (vec(g_post_mix), vec(g_pre_mlp), vec(g_post_mlp))
    w_in_l = w_in[0].astype(BF16)
    for l in range(depth):
        last = l == depth - 1
        in_args = (l, mod, g_pre, w_in_l, pavg, g_norm)
        *proj, wa, wb, wc, wo = _in_proj(x, *in_args, rope_tabs, tm=TM_IN, ctx=False,
                                         cast=[(w, l) for w in (w_branch_a, w_branch_b, w_branch_c, w_out)])
        q, k, v, sg, f_lat, ug, vn, gates = proj
        mlp_cast = [(w_up, l), (w_down, l)]
        if last:
            kc, vc = _in_proj(ctx, *in_args, None, tm=CTX_LEN, ctx=True, kv_only=True)
            ret, wup, wdn = _retention(l, logits, (q, k, v, sg), (kc, vc), ctx_out=False, cast=mlp_cast)
        else:
            qc, kc, vc, sgc, f_ctx, ugc, vnc, gatesc = _in_proj(ctx, *in_args, None, tm=CTX_LEN, ctx=True)
            ret, retc, wup, wdn = _retention(l, logits, (q, k, v, sg), (qc, kc, vc, sgc), ctx_out=True,
                                             cast=mlp_cast)
        mix_w = (l, mod, ws, bs_tab, wa, wb, wc, wo, *gains, wup, wdn)
        four = _fourier_latent(f_lat)
        if last:
            x = _mix_mlp(x, ret, four, ug, vn, gates, *mix_w, tm=TM_LAT, ctx=False)
        else:
            x, w_in_l = _mix_mlp(x, ret, four, ug, vn, gates, *mix_w, tm=TM_LAT, ctx=False, cast=[(w_in, l + 1)])
            ctx = _mix_mlp(ctx, retc, _fourier_ctx(f_ctx), ugc, vnc, gatesc, *mix_w, tm=CTX_LEN, ctx=True)
    return x
```

```python
import functools

import numpy as np
import jax
import jax.numpy as jnp
from jax import lax
from jax.experimental import pallas as pl
from jax.experimental.pallas import tpu as pltpu

F32 = jnp.float32
BF16 = jnp.bfloat16

D_MODEL = 1024
SEQ = 8192
CTX_LEN = 256
GRID_W = 64
RET_W = 512
HEADS = 4
HEAD_DIM = 128
ROPE_BASE = 10000.0
ROPE_FREQS = HEAD_DIM // 4
FNET_W = 256
FNET_GROUP = 64
GMLP_W = 256
GMLP_GROUP = 64
GMLP_CHUNK = 128
D_FF = 4 * D_MODEL
EPS = 1e-6
IN_W = 4 * RET_W + FNET_W + 2 * GMLP_W + 3 * D_MODEL
COL_F = 4 * RET_W
COL_U = COL_F + FNET_W
COL_VS = COL_U + GMLP_W
COL_GATE = COL_VS + GMLP_W

TM_LAT = 512
TM_IN = 1024
RCHUNK = 256
RET_TILE = 2048
RET_STEPS = SEQ // RET_TILE
RET_CPT = RET_TILE // RCHUNK
RET_AHEAD = 4
N_RCHUNKS = 1 + SEQ // RCHUNK
FF_CHUNK = 1024
FFT_N1 = 64
FFT_N2 = 128
HALF = 128
FFT_STEP = 16
MIX_K2 = TM_LAT // FFT_N1

_VMEM_LIMIT = 56 * 1024 * 1024


def _dot(a, b):
    return jnp.dot(a, b, preferred_element_type=F32)


def _split(x):
    hi = x.astype(BF16)
    lo = (x - hi.astype(F32)).astype(BF16)
    return hi, lo


def _np_split(a64):
    hi = np.asarray(a64, np.float32).astype(BF16)
    lo = (np.asarray(a64, np.float32) - hi.astype(np.float32)).astype(BF16)
    return hi, lo


def _rms(x):
    return x * lax.rsqrt(jnp.mean(x * x, axis=-1, keepdims=True) + EPS)


def _gelu(x):
    return x * (0.5 * (1.0 + jnp.tanh(0.7978845608028654 * (x + 0.044715 * (x * x * x)))))


def _sigmoid(x):
    return 1.0 / (1.0 + jnp.exp(-x))


def _const_spec(shape, nargs=None):
    zeros = (0,) * len(shape)
    return pl.BlockSpec(shape, lambda *_: zeros, pipeline_mode=pl.Buffered(1))


def _layer_spec(shape, layer):
    idx = (layer,) + (0,) * len(shape)
    return pl.BlockSpec((None,) + tuple(shape), lambda *_: idx, pipeline_mode=pl.Buffered(1))


def _params(n_axes):
    return pltpu.CompilerParams(dimension_semantics=("arbitrary",) * n_axes, vmem_limit_bytes=_VMEM_LIMIT)


def _mod_kernel(a_ref, w_ref, b_ref, o_ref):
    a = a_ref[...]
    a = a * _sigmoid(a)
    ah, al = _split(a)
    w = w_ref[0].astype(BF16)
    o_ref[0] = _dot(ah, w) + _dot(al, w) + b_ref[0]


def _modulation(cond_rows, w_mod, b_mod):
    depth = w_mod.shape[0]
    tn = 1536
    return pl.pallas_call(
        _mod_kernel,
        grid=(depth, (6 * D_MODEL) // tn),
        in_specs=[
            pl.BlockSpec((8, D_MODEL), lambda l, j: (0, 0)),
            pl.BlockSpec((1, D_MODEL, tn), lambda l, j: (l, 0, j)),
            pl.BlockSpec((1, 1, tn), lambda l, j: (l, 0, j)),
        ],
        out_specs=pl.BlockSpec((1, 8, tn), lambda l, j: (l, 0, j)),
        out_shape=jax.ShapeDtypeStruct((depth, 8, 6 * D_MODEL), F32),
        compiler_params=_params(2),
        name="modulation",
    )(cond_rows, w_mod, b_mod.reshape(depth, 1, 6 * D_MODEL))


def _cast_plan(cast, grid):
    steps = int(np.prod(grid))
    in_specs, args, out_specs, out_shapes = [], [], [], []

    def step(*idx):
        t = idx[0]
        for n, i in zip(grid[1:], idx[1:]):
            t = t * n + i
        return t

    for stack, layer in cast:
        _, rows, cols = stack.shape
        n_chunks = rows // 16
        while n_chunks > steps:
            n_chunks //= 2
        assert rows % n_chunks == 0
        per = steps // n_chunks
        chunk = lambda *idx, per=per, last=n_chunks - 1: jnp.minimum(step(*idx) // per, last)
        in_specs.append(pl.BlockSpec((None, rows // n_chunks, cols),
                                     lambda *idx, layer=layer, chunk=chunk: (layer, chunk(*idx), 0)))
        out_specs.append(pl.BlockSpec((rows // n_chunks, cols), lambda *idx, chunk=chunk: (chunk(*idx), 0)))
        out_shapes.append(jax.ShapeDtypeStruct((rows, cols), BF16))
        args.append(stack)
    return in_specs, args, out_specs, out_shapes


def _cast_chunks(src_refs, dst_refs):
    for src, dst in zip(src_refs, dst_refs, strict=True):
        dst[...] = src[...].astype(BF16)


def _in_proj_kernel(*refs, rope, kv_only, n_cast):
    n_in = 8 if rope else 6
    ins, cast_src = refs[:n_in], refs[n_in:n_in + n_cast]
    outs, cast_dst = refs[n_in + n_cast:len(refs) - n_cast], refs[len(refs) - n_cast:]
    _cast_chunks(cast_src, cast_dst)
    if rope:
        x_ref, mod_ref, gpre_ref, w_ref, pavg_ref, gn_ref, rrow_ref, rcol_ref = ins
    else:
        x_ref, mod_ref, gpre_ref, w_ref, pavg_ref, gn_ref = ins
    tm = x_ref.shape[1]

    def prepare():
        shift = mod_ref[0, :, 0:D_MODEL]
        gain = gpre_ref[...] * (1.0 + mod_ref[0, :, D_MODEL:2 * D_MODEL])
        return (_rms(x_ref[0]) * gain + shift).astype(BF16)

    def project(hb):
        def proj(lo, hi):
            return _dot(hb, w_ref[:, lo:hi])

        if rope:
            n_rows = tm // GRID_W
            rot = []
            for t in range(2):
                by_row = jnp.concatenate(
                    [jnp.broadcast_to(rrow_ref[t, r:r + 1, :], (GRID_W, HEAD_DIM)) for r in range(n_rows)], axis=0)
                rot.append(by_row + jnp.concatenate([rcol_ref[t]] * n_rows, axis=0))

        def rope_store(z, dst):
            if not rope:
                dst[0] = z.astype(BF16)
                return
            cos, sin = rot
            for hd in range(HEADS):
                a = z[:, hd * HEAD_DIM:(hd + 1) * HEAD_DIM]
                r = a * cos + pltpu.roll(a, HEAD_DIM // 2, 1) * sin
                dst[0, :, hd * HEAD_DIM:(hd + 1) * HEAD_DIM] = r.astype(BF16)

        if kv_only:
            k_ref, v_ref = outs
        else:
            q_ref, k_ref, v_ref, sg_ref, f_ref, u_ref, vn_ref, gate_ref = outs
            rope_store(proj(0, RET_W), q_ref)
        rope_store(proj(RET_W, 2 * RET_W) * (HEAD_DIM ** -0.5), k_ref)
        v_ref[0] = proj(2 * RET_W, 3 * RET_W).astype(BF16)
        if kv_only:
            return
        g = proj(3 * RET_W, 4 * RET_W)
        sg_ref[0] = (g * _sigmoid(g)).astype(BF16)

        f = proj(COL_F, COL_U)
        for hf in range(2):
            fh = f[:, hf * HALF:(hf + 1) * HALF]
            if rope:
                for n1 in range(tm // FFT_N2):
                    for nj in range(FFT_N2 // FFT_STEP):
                        f_ref[0, hf, nj, n1 * FFT_STEP:(n1 + 1) * FFT_STEP, :] = (
                            fh[n1 * FFT_N2 + nj * FFT_STEP:n1 * FFT_N2 + (nj + 1) * FFT_STEP])
            else:
                f_ref[0, hf] = fh

        u_ref[0] = _gelu(proj(COL_U, COL_VS)).astype(BF16)
        vg = _gelu(proj(COL_VS, COL_GATE))
        sh, sl = _split(vg * vg)
        pavg = pavg_ref[...]
        ms = _dot(sh, pavg) + _dot(sl, pavg)
        vn_ref[0] = (vg * lax.rsqrt(ms + EPS) * gn_ref[...]).astype(BF16)

        for c in range(3):
            z = proj(COL_GATE + c * D_MODEL, COL_GATE + (c + 1) * D_MODEL)
            gate_ref[0, :, c * D_MODEL:(c + 1) * D_MODEL] = _sigmoid(z).astype(BF16)

    project(prepare())


def _mod_spec(layer, ctx):
    idx = (lambda b, i: (layer, 2, 0, 0)) if ctx else (lambda b, i: (layer, b, 0, 0))
    return pl.BlockSpec((None, 1, 1, 6 * D_MODEL), idx)


def _in_proj(xs, layer, mod, g_pre, w_in, pavg, g_norm, rope_tabs, *, tm, ctx, kv_only=False, cast=()):
    nb, rows, _ = xs.shape
    rope = rope_tabs is not None
    grid = (nb, rows // tm)
    tok_spec = lambda w: pl.BlockSpec((1, tm, w), lambda b, i: (b, i, 0))
    in_specs = [
        tok_spec(D_MODEL),
        _mod_spec(layer, ctx),
        _layer_spec((1, D_MODEL), layer),
        _const_spec((D_MODEL, IN_W)),
        _const_spec((GMLP_W, GMLP_W)),
        _layer_spec((1, GMLP_W), layer),
    ]
    args = [xs, mod, g_pre, w_in, pavg, g_norm]
    if rope:
        in_specs += [pl.BlockSpec((2, tm // GRID_W, HEAD_DIM), lambda b, i: (0, i, 0)),
                     _const_spec((2, GRID_W, HEAD_DIM))]
        args += list(rope_tabs)
    c_in_specs, c_args, c_out_specs, c_out_shapes = _cast_plan(cast, grid)
    in_specs += c_in_specs
    args += c_args
    bf = lambda w: jax.ShapeDtypeStruct((nb, rows, w), BF16)
    if kv_only:
        out_shapes = (bf(RET_W), bf(RET_W))
        out_specs = (tok_spec(RET_W), tok_spec(RET_W))
    else:
        if rope:
            n_nj = FFT_N2 // FFT_STEP
            f_shape = jax.ShapeDtypeStruct((nb, 2, n_nj, FFT_N1 * FFT_STEP, HALF), F32)
            f_spec = pl.BlockSpec((1, 2, n_nj, (tm // FFT_N2) * FFT_STEP, HALF), lambda b, i: (b, 0, 0, i, 0))
        else:
            f_shape = jax.ShapeDtypeStruct((nb, 2, rows, HALF), F32)
            f_spec = pl.BlockSpec((1, 2, tm, HALF), lambda b, i: (b, 0, i, 0))
        out_shapes = (
            bf(RET_W), bf(RET_W), bf(RET_W), bf(RET_W),
            f_shape,
            bf(GMLP_W), bf(GMLP_W),
            bf(3 * D_MODEL),
        )
        out_specs = (
            tok_spec(RET_W), tok_spec(RET_W), tok_spec(RET_W), tok_spec(RET_W),
            f_spec,
            tok_spec(GMLP_W), tok_spec(GMLP_W), tok_spec(3 * D_MODEL),
        )
    return pl.pallas_call(
        functools.partial(_in_proj_kernel, rope=rope, kv_only=kv_only, n_cast=len(cast)),
        grid=grid,
        in_specs=in_specs,
        out_specs=tuple(out_specs) + tuple(c_out_specs),
        out_shape=tuple(out_shapes) + tuple(c_out_shapes),
        compiler_params=_params(2),
        name="in_proj_ctx" if ctx else "in_proj",
    )(*args)


def _ret_kernel(*refs, ctx_out, n_cast):
    n_in, n_out = (9, 2) if ctx_out else (7, 1)
    ins, refs = refs[:n_in], refs[n_in:]
    cast_src, refs = refs[:n_cast], refs[n_cast:]
    outs, refs = refs[:n_out], refs[n_out:]
    cast_dst, scratch = refs[:n_cast], refs[n_cast:]
    _cast_chunks(cast_src, cast_dst)
    (logit_ref, ql_ref, kl_ref, vl_ref, sgl_ref, kc_ref, vc_ref) = ins[:7]
    if ctx_out:
        qc_ref, sgc_ref = ins[7:]
        ol_ref, oc_ref = outs
    else:
        (ol_ref,) = outs
    mask_ref, wf_ref, wb_ref, qf_ref, qb_ref, df_ref, db_ref, sf_ref, sb_ref, sball_ref = scratch
    b = pl.program_id(0)
    phase = pl.program_id(1)
    j = pl.program_id(2)

    @pl.when((b == 0) & (phase == 0) & (j == 0))
    def _():
        x = logit_ref[...]
        lg = -(jnp.maximum(-x, 0.0) + jnp.log(1.0 + jnp.exp(-jnp.abs(x))))
        row = lax.broadcasted_iota(jnp.int32, (RCHUNK, RCHUNK), 0).astype(F32)
        col = lax.broadcasted_iota(jnp.int32, (RCHUNK, RCHUNK), 1).astype(F32)
        diff = row - col
        pos = row[:, :HEAD_DIM]
        for hd in range(HEADS):
            lf = lg[hd:hd + 1, :]
            lb = lg[HEADS + hd:HEADS + hd + 1, :]
            mask_ref[hd] = jnp.where(diff >= 0.0, jnp.exp(lf * jnp.maximum(diff, 0.0)),
                                     jnp.exp(lb * jnp.maximum(-diff, 0.0)))
            wf_ref[hd] = jnp.exp(lf * (RCHUNK - 1.0 - pos))
            wb_ref[hd] = jnp.exp(lb * pos)
            qf_ref[hd] = jnp.exp(lf * (pos + 1.0))
            qb_ref[hd] = jnp.exp(lb * (RCHUNK - pos))
            df_ref[hd] = jnp.exp(jnp.broadcast_to(lf, (HEAD_DIM, HEAD_DIM)) * float(RCHUNK))
            db_ref[hd] = jnp.exp(jnp.broadcast_to(lb, (HEAD_DIM, HEAD_DIM)) * float(RCHUNK))

    def chunk_updates(hd, k_ref, v_ref, n_chunks, w_ref):
        cols = slice(hd * HEAD_DIM, (hd + 1) * HEAD_DIM)
        us = []
        for c in range(n_chunks):
            rows = slice(c * RCHUNK, (c + 1) * RCHUNK)
            kw = (k_ref[0, rows, cols].astype(F32) * w_ref[hd]).astype(BF16)
            us.append(lax.dot_general(kw, v_ref[0, rows, cols], (((0,), (0,)), ((), ())),
                                      preferred_element_type=F32))
        return us

    def backward_chunks(k_ref, v_ref, n_chunks, first_chunk):
        for hd in range(HEADS):
            us = chunk_updates(hd, k_ref, v_ref, n_chunks, wb_ref)
            s = sb_ref[hd]
            for c in reversed(range(n_chunks)):
                sball_ref[first_chunk + c, hd] = s.astype(BF16)
                s = db_ref[hd] * s + us[c]
            sb_ref[hd] = s

    def forward_chunks(q_ref, k_ref, v_ref, sg_ref, o_ref, n_chunks, first_chunk):
        starts = []
        for hd in range(HEADS):
            us = chunk_updates(hd, k_ref, v_ref, n_chunks, wf_ref)
            st = [sf_ref[hd]]
            for c in range(n_chunks):
                st.append(df_ref[hd] * st[-1] + us[c])
            sf_ref[hd] = st[-1]
            starts.append(st)
        if o_ref is None:
            return
        units = [(c, hd) for c in range(n_chunks) for hd in range(HEADS)]

        def scores(c, hd):
            rows = slice(c * RCHUNK, (c + 1) * RCHUNK)
            cols = slice(hd * HEAD_DIM, (hd + 1) * HEAD_DIM)
            return lax.dot_general(q_ref[0, rows, cols], k_ref[0, rows, cols], (((1,), (1,)), ((), ())),
                                   preferred_element_type=F32)

        ahead = min(RET_AHEAD, len(units))
        pending = [scores(*u) for u in units[:ahead]]
        for n, (c, hd) in enumerate(units):
            if n + ahead < len(units):
                pending.append(scores(*units[n + ahead]))
            sc = pending.pop(0)
            rows = slice(c * RCHUNK, (c + 1) * RCHUNK)
            cols = slice(hd * HEAD_DIM, (hd + 1) * HEAD_DIM)
            qf = q_ref[0, rows, cols].astype(F32)
            lhs = jnp.concatenate([(sc * mask_ref[hd]).astype(BF16),
                                   (qf * qf_ref[hd]).astype(BF16),
                                   (qf * qb_ref[hd]).astype(BF16)], axis=1)
            rhs = jnp.concatenate([v_ref[0, rows, cols], starts[hd][c].astype(BF16),
                                   sball_ref[first_chunk + c, hd]], axis=0)
            o = _dot(lhs, rhs)
            o_ref[0, rows, cols] = (_rms(o) * sg_ref[0, rows, cols].astype(F32)).astype(BF16)

    @pl.when(phase == 0)
    def _():
        @pl.when(j == 0)
        def _():
            sb_ref[...] = jnp.zeros_like(sb_ref)
            backward_chunks(kc_ref, vc_ref, 1, 0)

        backward_chunks(kl_ref, vl_ref, RET_CPT, 1 + (RET_STEPS - 1 - j) * RET_CPT)

    @pl.when(phase == 1)
    def _():
        @pl.when(j == 0)
        def _():
            sf_ref[...] = jnp.zeros_like(sf_ref)
            if ctx_out:
                forward_chunks(qc_ref, kc_ref, vc_ref, sgc_ref, oc_ref, 1, 0)
            else:
                forward_chunks(None, kc_ref, vc_ref, None, None, 1, 0)

        forward_chunks(ql_ref, kl_ref, vl_ref, sgl_ref, ol_ref, RET_CPT, 1 + j * RET_CPT)


def _retention(layer, logits, lat, ctx, *, ctx_out, cast=()):
    nb = lat[0].shape[0]
    kv_idx = lambda b, p, j: (b, jnp.where(p == 0, RET_STEPS - 1 - j, j), 0)
    q_idx = lambda b, p, j: (b, jnp.where(p == 0, 0, j), 0)
    c_idx = lambda b, p, j: (b, 0, 0)
    lat_blk = (1, RET_TILE, RET_W)
    ctx_blk = (1, CTX_LEN, RET_W)
    in_specs = [
        _layer_spec((2 * HEADS, 1), layer),
        pl.BlockSpec(lat_blk, q_idx), pl.BlockSpec(lat_blk, kv_idx), pl.BlockSpec(lat_blk, kv_idx),
        pl.BlockSpec(lat_blk, q_idx),
        pl.BlockSpec(ctx_blk, c_idx), pl.BlockSpec(ctx_blk, c_idx),
    ]
    if ctx_out:
        qc, kc, vc, sgc = ctx
        args = [logits, *lat, kc, vc, qc, sgc]
        in_specs += [pl.BlockSpec(ctx_blk, c_idx), pl.BlockSpec(ctx_blk, c_idx)]
        out_specs = (pl.BlockSpec(lat_blk, q_idx), pl.BlockSpec(ctx_blk, c_idx))
        out_shape = (jax.ShapeDtypeStruct((nb, SEQ, RET_W), BF16), jax.ShapeDtypeStruct((nb, CTX_LEN, RET_W), BF16))
    else:
        kc, vc = ctx
        args = [logits, *lat, kc, vc]
        out_specs = (pl.BlockSpec(lat_blk, q_idx),)
        out_shape = (jax.ShapeDtypeStruct((nb, SEQ, RET_W), BF16),)
    grid = (nb, 2, RET_STEPS)
    c_in_specs, c_args, c_out_specs, c_out_shapes = _cast_plan(cast, grid)
    state = pltpu.VMEM((HEADS, HEAD_DIM, HEAD_DIM), F32)
    pos_tab = pltpu.VMEM((HEADS, RCHUNK, HEAD_DIM), F32)
    return pl.pallas_call(
        functools.partial(_ret_kernel, ctx_out=ctx_out, n_cast=len(cast)),
        grid=grid,
        in_specs=in_specs + c_in_specs,
        out_specs=out_specs + tuple(c_out_specs),
        out_shape=out_shape + tuple(c_out_shapes),
        scratch_shapes=[
            pltpu.VMEM((HEADS, RCHUNK, RCHUNK), F32),
            pos_tab, pos_tab, pos_tab, pos_tab,
            state, state,
            state, state,
            pltpu.VMEM((N_RCHUNKS, HEADS, HEAD_DIM, HEAD_DIM), BF16),
        ],
        compiler_params=_params(3),
        name="retention",
    )(*args, *c_args)


def _dft_cos_sin(n):
    idx = np.arange(n, dtype=np.float64)
    ang = 2.0 * np.pi * ((idx[:, None] * idx[None, :]) % n) / n
    return np.cos(ang), np.sin(ang)


def _channel_dft(scale):
    c, s = _dft_cos_sin(FNET_GROUP)
    eye = np.eye(FNET_W // FNET_GROUP)
    return np.concatenate([np.kron(eye, c), np.kron(eye, s)], axis=0) * scale


def _fft_constants():
    k1 = np.arange(FFT_N1, dtype=np.float64)[None, :, None]
    n1 = np.arange(FFT_N1, dtype=np.float64)[None, None, :]
    n2 = np.arange(FFT_N2, dtype=np.float64)[:, None, None]
    ang = 2.0 * np.pi * ((k1 * (FFT_N2 * n1 + n2)) % SEQ) / SEQ
    m1 = np.concatenate([np.cos(ang), -np.sin(ang)], axis=1)
    c, s = _dft_cos_sin(FFT_N2)
    m3 = np.block([[c, s], [-s, c]])
    return (_np_split_cat(m1, -1), jnp.asarray(m3, F32),
            jnp.asarray(_channel_dft((SEQ * FNET_GROUP) ** -0.5), F32))


def _ctx_fft_constants():
    c, s = _dft_cos_sin(CTX_LEN)
    return (jnp.asarray(np.concatenate([c, -s], axis=0), F32),
            jnp.asarray(_channel_dft((CTX_LEN * FNET_GROUP) ** -0.5), F32))


def _np_split_cat(a64, axis):
    hi, lo = _np_split(a64)
    return jnp.asarray(np.concatenate([hi, lo], axis=axis))


def _twice(x, axis):
    return jnp.concatenate([x, x], axis=axis)


def _fft1_kernel(x_ref, m_ref, o_ref):
    for jj in range(FFT_STEP):
        x = jnp.concatenate([x_ref[hf, pl.ds(jj, FFT_N1, stride=FFT_STEP), :] for hf in range(2)],
                            axis=1).astype(BF16)
        a = _dot(m_ref[jj], _twice(x, 0))
        for hf in range(2):
            for ri in range(2):
                for kb in range(FFT_N1 // FFT_STEP):
                    r0 = ri * FFT_N1 + kb * FFT_STEP
                    o_ref[hf, ri, kb, jj * FFT_STEP:(jj + 1) * FFT_STEP, :] = (
                        a[r0:r0 + FFT_STEP, hf * HALF:(hf + 1) * HALF])


def _fft3_kernel(a_ref, w_ref, cd_ref, o_ref):
    w = w_ref[...].astype(BF16)
    zs = []
    for jj in range(FFT_STEP):
        rhs = jnp.concatenate(
            [jnp.concatenate([a_ref[hf, ri, pl.ds(jj, FFT_N2, stride=FFT_STEP), :] for ri in range(2)], axis=0)
             for hf in range(2)], axis=1).astype(BF16)
        y = _dot(w, rhs)
        zs.append(jnp.concatenate([y[:FFT_N2], y[FFT_N2:]], axis=1).astype(BF16))
    z = jnp.concatenate(zs, axis=0)
    four = _dot(z, cd_ref[...].astype(BF16))
    for jj in range(FFT_STEP):
        for kt in range(FFT_N2 // MIX_K2):
            for hf in range(2):
                r0 = jj * FFT_N2 + kt * MIX_K2
                o_ref[hf, kt, jj * MIX_K2:(jj + 1) * MIX_K2, :] = four[r0:r0 + MIX_K2, hf * HALF:(hf + 1) * HALF]


def _fourier_latent(f_blk):
    nb = f_blk.shape[0]
    m1, m3, cd = _fft_constants()
    n_kb = FFT_N1 // FFT_STEP
    a = pl.pallas_call(
        _fft1_kernel,
        grid=(nb, FFT_N2 // FFT_STEP),
        in_specs=[
            pl.BlockSpec((None, 2, None, FFT_N1 * FFT_STEP, HALF), lambda b, j: (b, 0, j, 0, 0)),
            pl.BlockSpec((FFT_STEP, 2 * FFT_N1, 2 * FFT_N1), lambda b, j: (j, 0, 0)),
        ],
        out_specs=pl.BlockSpec((None, 2, 2, n_kb, FFT_STEP * FFT_STEP, HALF), lambda b, j: (b, 0, 0, 0, j, 0)),
        out_shape=jax.ShapeDtypeStruct((nb, 2, 2, n_kb, FFT_N2 * FFT_STEP, HALF), F32),
        compiler_params=_params(2),
        name="fourier_stage1",
    )(f_blk, m1)
    return pl.pallas_call(
        _fft3_kernel,
        grid=(nb, n_kb),
        in_specs=[
            pl.BlockSpec((None, 2, 2, None, FFT_N2 * FFT_STEP, HALF), lambda b, j: (b, 0, 0, j, 0, 0)),
            _const_spec((2 * FFT_N2, 2 * FFT_N2)), _const_spec((2 * FNET_W, FNET_W)),
        ],
        out_specs=pl.BlockSpec((None, 2, FFT_N2 // MIX_K2, FFT_STEP * MIX_K2, HALF), lambda b, j: (b, 0, 0, j, 0)),
        out_shape=jax.ShapeDtypeStruct((nb, 2, FFT_N2 // MIX_K2, FFT_N1 * MIX_K2, HALF), F32),
        compiler_params=_params(2),
        name="fourier_stage2",
    )(a, m3, cd)


def _fft_ctx_kernel(x_ref, w_ref, cd_ref, o_ref):
    x = jnp.concatenate([x_ref[hf] for hf in range(2)], axis=1).astype(BF16)
    z = _dot(w_ref[...].astype(BF16), x)
    z = jnp.concatenate([z[:CTX_LEN], z[CTX_LEN:]], axis=1).astype(BF16)
    four = _dot(z, cd_ref[...].astype(BF16))
    for hf in range(2):
        o_ref[hf] = four[:, hf * HALF:(hf + 1) * HALF]


def _fourier_ctx(f_ctx):
    nb = f_ctx.shape[2] // CTX_LEN
    w, cd = _ctx_fft_constants()
    return pl.pallas_call(
        _fft_ctx_kernel,
        grid=(nb,),
        in_specs=[
            pl.BlockSpec((None, 2, CTX_LEN, HALF), lambda b: (0, 0, b, 0)),
            _const_spec((2 * CTX_LEN, CTX_LEN)), _const_spec((2 * FNET_W, FNET_W)),
        ],
        out_specs=pl.BlockSpec((None, 2, CTX_LEN, HALF), lambda b: (0, 0, b, 0)),
        out_shape=jax.ShapeDtypeStruct(f_ctx.shape, F32),
        compiler_params=_params(1),
        name="fourier_ctx",
    )(f_ctx, w, cd)


def _mix_mlp_kernel(*refs, tm, ctx, n_cast):
    (x_ref, ret_ref, four_ref, u_ref, vn_ref, gate_ref, mod_ref, ws_ref, bs_ref,
     wa_ref, wb_ref, wc_ref, wo_ref, gpm_ref, gprm_ref, gpom_ref, wup_ref, wdn_ref) = refs[:18]
    o_ref = refs[18 + n_cast]
    _cast_chunks(refs[18:18 + n_cast], refs[19 + n_cast:])
    group = lax.broadcasted_iota(jnp.int32, (GMLP_CHUNK, GMLP_W), 1) // GMLP_GROUP
    parts = []
    for cc in range(tm // GMLP_CHUNK):
        rows = slice(cc * GMLP_CHUNK, (cc + 1) * GMLP_CHUNK)
        vn = vn_ref[0, rows, :]
        s = bs_ref[...]
        for g in range(GMLP_W // GMLP_GROUP):
            s = s + jnp.where(group == g, _dot(ws_ref[g], vn), 0.0)
        parts.append((u_ref[0, rows, :].astype(F32) * s).astype(BF16))
    sgu = jnp.concatenate(parts, axis=0)

    if ctx:
        four = jnp.concatenate([four_ref[0, hf] for hf in range(2)], axis=1).astype(BF16)
    else:
        nk2 = tm // FFT_N1
        four = jnp.concatenate(
            [jnp.concatenate([four_ref[0, hf, pl.ds(k2, FFT_N1, stride=nk2), :] for k2 in range(nk2)], axis=0)
             for hf in range(2)], axis=1).astype(BF16)
    ga = gate_ref[0, :, 0:D_MODEL].astype(F32)
    gb = gate_ref[0, :, D_MODEL:2 * D_MODEL].astype(F32)
    gc = gate_ref[0, :, 2 * D_MODEL:3 * D_MODEL].astype(F32)
    merged = (ga * _dot(ret_ref[0], wa_ref[...]) + gb * _dot(four, wb_ref[...])
              + gc * _dot(sgu, wc_ref[...])).astype(BF16)

    gain1 = mod_ref[0, :, 2 * D_MODEL:3 * D_MODEL] * gpm_ref[...]
    shift2 = mod_ref[0, :, 3 * D_MODEL:4 * D_MODEL]
    gain2 = gprm_ref[...] * (1.0 + mod_ref[0, :, 4 * D_MODEL:5 * D_MODEL])
    gain3 = mod_ref[0, :, 5 * D_MODEL:6 * D_MODEL] * gpom_ref[...]

    halves = [slice(r * (tm // 2), (r + 1) * (tm // 2)) for r in range(2)]
    x1, h2 = [], []
    for rows in halves:
        y = _dot(merged[rows], wo_ref[...])
        x1.append(x_ref[0, rows, :] + _rms(y) * gain1)
        h2.append((_rms(x1[-1]) * gain2 + shift2).astype(BF16))
    up0 = jnp.concatenate([_dot(h, wup_ref[:, 0:FF_CHUNK]) for h in h2], axis=0)
    h2 = jnp.concatenate(h2, axis=0)

    n_ff = D_FF // FF_CHUNK
    m = None
    for c in range(n_ff - 1):
        cols = slice(c * FF_CHUNK, (c + 1) * FF_CHUNK)
        a = jnp.maximum(up0 if c == 0 else _dot(h2, wup_ref[:, cols]), 0.0)
        part = _dot((a * a).astype(BF16), wdn_ref[cols, :])
        m = part if m is None else m + part
    cols = slice((n_ff - 1) * FF_CHUNK, n_ff * FF_CHUNK)
    a = jnp.maximum(_dot(h2, wup_ref[:, cols]), 0.0)
    a = (a * a).astype(BF16)
    for r, rows in enumerate(halves):
        mr = m[rows] + _dot(a[rows], wdn_ref[cols, :])
        o_ref[0, rows, :] = x1[r] + _rms(mr) * gain3


def _mix_mlp(xs, ret, four, ug, vn, gates, layer, mod, ws, bs_tab, wa, wb, wc, wo,
             g_post_mix, g_pre_mlp, g_post_mlp, w_up, w_down, *, tm, ctx, cast=()):
    nb, rows, _ = xs.shape
    tok_spec = lambda w: pl.BlockSpec((1, tm, w), lambda b, i: (b, i, 0))
    lspec = lambda *shape: _layer_spec(shape, layer)
    if ctx:
        four_spec = pl.BlockSpec((1, 2, tm, HALF), lambda b, i: (b, 0, i, 0))
    else:
        four_spec = pl.BlockSpec((1, 2, None, tm, HALF), lambda b, i: (b, 0, i, 0, 0))
    c_in_specs, c_args, c_out_specs, c_out_shapes = _cast_plan(cast, (nb, rows // tm))
    outs = pl.pallas_call(
        functools.partial(_mix_mlp_kernel, tm=tm, ctx=ctx, n_cast=len(cast)),
        grid=(nb, rows // tm),
        in_specs=[
            tok_spec(D_MODEL), tok_spec(RET_W),
            four_spec,
            tok_spec(GMLP_W), tok_spec(GMLP_W), tok_spec(3 * D_MODEL),
            _mod_spec(layer, ctx),
            lspec(GMLP_W // GMLP_GROUP, GMLP_CHUNK, GMLP_CHUNK),
            lspec(GMLP_CHUNK, GMLP_W),
            _const_spec((RET_W, D_MODEL)), _const_spec((FNET_W, D_MODEL)), _const_spec((GMLP_W, D_MODEL)),
            _const_spec((D_MODEL, D_MODEL)),
            lspec(1, D_MODEL), lspec(1, D_MODEL), lspec(1, D_MODEL),
            _const_spec((D_MODEL, D_FF)), _const_spec((D_FF, D_MODEL)),
        ] + c_in_specs,
        out_specs=(tok_spec(D_MODEL),) + tuple(c_out_specs),
        out_shape=(jax.ShapeDtypeStruct((nb, rows, D_MODEL), F32),) + tuple(c_out_shapes),
        compiler_params=_params(2),
        name="mix_mlp_ctx" if ctx else "mix_mlp",
    )(xs, ret, four, ug, vn, gates, mod, ws, bs_tab, wa, wb, wc, wo,
      g_post_mix, g_pre_mlp, g_post_mlp, w_up, w_down, *c_args)
    return outs if cast else outs[0]


def _rope_tables():
    rows = SEQ // GRID_W
    freqs = ROPE_BASE ** (-jnp.arange(ROPE_FREQS, dtype=F32) / ROPE_FREQS)
    ang_r = jnp.arange(rows, dtype=F32)[:, None] * freqs
    ang_c = jnp.arange(GRID_W, dtype=F32)[:, None] * freqs

    def lanes(a, first):
        z = jnp.zeros_like(a)
        return jnp.concatenate([a, z] if first else [z, a], axis=-1)

    def pair(ang, first):
        cos, sin = lanes(jnp.cos(ang), first), lanes(jnp.sin(ang), first)
        return jnp.stack([jnp.concatenate([cos, cos], axis=-1), jnp.concatenate([-sin, sin], axis=-1)])

    return pair(ang_r, True), pair(ang_c, False)


def kernel(x, c, ctx, c_ctx, w_mod, b_mod, g_pre_mix, g_post_mix, g_pre_mlp, g_post_mlp, w_in,
           ret_decay_logit, sgu_w_s, sgu_b_s, sgu_norm, w_branch_a, w_branch_b, w_branch_c, w_out,
           w_up, w_down):
    nb = x.shape[0]
    depth = w_mod.shape[0]
    assert x.shape == (nb, SEQ, D_MODEL) and ctx.shape == (nb, CTX_LEN, D_MODEL) and nb == 2

    cond_rows = jnp.concatenate([c, c_ctx[None, :], jnp.zeros((8 - nb - 1, D_MODEL), F32)], axis=0)
    mod = _modulation(cond_rows, w_mod, b_mod).reshape(depth, 8, 1, 6 * D_MODEL)
    rope_tabs = _rope_tables()
    pavg = jnp.asarray(np.kron(np.eye(GMLP_W // GMLP_GROUP), np.full((GMLP_GROUP, GMLP_GROUP), 1.0 / GMLP_GROUP)), BF16)

    vec = lambda p: p[:, None, :]
    g_pre, g_norm = vec(g_pre_mix), vec(sgu_norm)
    logits = ret_decay_logit.reshape(depth, 2 * HEADS, 1)
    ws = sgu_w_s.astype(BF16)
    bs_tab = jnp.repeat(jnp.swapaxes(sgu_b_s, 1, 2), GMLP_GROUP, axis=2)
    gains = (vec(g_post_mix), vec(g_pre_mlp), vec(g_post_mlp))
    w_in_l = w_in[0].astype(BF16)
    ctx = ctx.reshape(1, nb * CTX_LEN, D_MODEL)
    for l in range(depth):
        last = l == depth - 1
        in_args = (l, mod, g_pre, w_in_l, pavg, g_norm)
        *proj, wa, wb, wc, wo = _in_proj(x, *in_args, rope_tabs, tm=TM_IN, ctx=False,
                                         cast=[(w, l) for w in (w_branch_a, w_branch_b, w_branch_c, w_out)])
        q, k, v, sg, f_lat, ug, vn, gates = proj
        mlp_cast = [(w_up, l), (w_down, l)]
        per_batch = lambda a: a.reshape(nb, CTX_LEN, RET_W)
        if last:
            kc, vc = _in_proj(ctx, *in_args, None, tm=nb * CTX_LEN, ctx=True, kv_only=True)
            ret, wup, wdn = _retention(l, logits, (q, k, v, sg), (per_batch(kc), per_batch(vc)), ctx_out=False,
                                       cast=mlp_cast)
        else:
            qc, kc, vc, sgc, f_ctx, ugc, vnc, gatesc = _in_proj(ctx, *in_args, None, tm=nb * CTX_LEN, ctx=True)
            ret, retc, wup, wdn = _retention(l, logits, (q, k, v, sg), tuple(map(per_batch, (qc, kc, vc, sgc))),
                                             ctx_out=True, cast=mlp_cast)
            retc = retc.reshape(1, nb * CTX_LEN, RET_W)
        mix_w = (l, mod, ws, bs_tab, wa, wb, wc, wo, *gains, wup, wdn)
        four = _fourier_latent(f_lat)
        if last:
            x = _mix_mlp(x, ret, four, ug, vn, gates, *mix_w, tm=TM_LAT, ctx=False)
        else:
            x, w_in_l = _mix_mlp(x, ret, four, ug, vn, gates, *mix_w, tm=TM_LAT, ctx=False, cast=[(w_in, l + 1)])
            ctx = _mix_mlp(ctx, retc, _fourier_ctx(f_ctx), ugc, vnc, gatesc, *mix_w, tm=nb * CTX_LEN, ctx=True)
    return x
```

```python
import functools

import numpy as np
import jax
import jax.numpy as jnp
from jax import lax
from jax.experimental import pallas as pl
from jax.experimental.pallas import tpu as pltpu

F32 = jnp.float32
BF16 = jnp.bfloat16

D_MODEL = 1024
SEQ = 8192
CTX_LEN = 256
GRID_W = 64
RET_W = 512
HEADS = 4
HEAD_DIM = 128
ROPE_BASE = 10000.0
ROPE_FREQS = HEAD_DIM // 4
FNET_W = 256
FNET_GROUP = 64
GMLP_W = 256
GMLP_GROUP = 64
GMLP_CHUNK = 128
D_FF = 4 * D_MODEL
EPS = 1e-6
IN_W = 4 * RET_W + FNET_W + 2 * GMLP_W + 3 * D_MODEL
COL_F = 4 * RET_W
COL_U = COL_F + FNET_W
COL_VS = COL_U + GMLP_W
COL_GATE = COL_VS + GMLP_W

TM_LAT = 512
TM_IN = 1024
RCHUNK = 256
RET_TILE = 2048
RET_STEPS = SEQ // RET_TILE
RET_CPT = RET_TILE // RCHUNK
RET_AHEAD = 4
N_RCHUNKS = 1 + SEQ // RCHUNK
FF_CHUNK = 1024
FFT_N1 = 64
FFT_N2 = 128
HALF = 128
FFT_STEP = 16
MIX_K2 = TM_LAT // FFT_N1

_VMEM_LIMIT = 56 * 1024 * 1024


def _dot(a, b):
    return jnp.dot(a, b, preferred_element_type=F32)


def _split(x):
    hi = x.astype(BF16)
    lo = (x - hi.astype(F32)).astype(BF16)
    return hi, lo


def _np_split(a64):
    hi = np.asarray(a64, np.float32).astype(BF16)
    lo = (np.asarray(a64, np.float32) - hi.astype(np.float32)).astype(BF16)
    return hi, lo


def _rms(x):
    return x * lax.rsqrt(jnp.mean(x * x, axis=-1, keepdims=True) + EPS)


def _gelu(x):
    return x * (0.5 * (1.0 + jnp.tanh(0.7978845608028654 * (x + 0.044715 * (x * x * x)))))


def _sigmoid(x):
    return 1.0 / (1.0 + jnp.exp(-x))


def _const_spec(shape, nargs=None):
    zeros = (0,) * len(shape)
    return pl.BlockSpec(shape, lambda *_: zeros, pipeline_mode=pl.Buffered(1))


def _layer_spec(shape, layer):
    idx = (layer,) + (0,) * len(shape)
    return pl.BlockSpec((None,) + tuple(shape), lambda *_: idx, pipeline_mode=pl.Buffered(1))


def _params(n_axes):
    return pltpu.CompilerParams(dimension_semantics=("arbitrary",) * n_axes, vmem_limit_bytes=_VMEM_LIMIT)


def _mod_kernel(a_ref, w_ref, b_ref, o_ref):
    a = a_ref[...]
    a = a * _sigmoid(a)
    ah, al = _split(a)
    w = w_ref[0].astype(BF16)
    o_ref[0] = _dot(ah, w) + _dot(al, w) + b_ref[0]


def _modulation(cond_rows, w_mod, b_mod):
    depth = w_mod.shape[0]
    tn = 1536
    return pl.pallas_call(
        _mod_kernel,
        grid=(depth, (6 * D_MODEL) // tn),
        in_specs=[
            pl.BlockSpec((8, D_MODEL), lambda l, j: (0, 0)),
            pl.BlockSpec((1, D_MODEL, tn), lambda l, j: (l, 0, j)),
            pl.BlockSpec((1, 1, tn), lambda l, j: (l, 0, j)),
        ],
        out_specs=pl.BlockSpec((1, 8, tn), lambda l, j: (l, 0, j)),
        out_shape=jax.ShapeDtypeStruct((depth, 8, 6 * D_MODEL), F32),
        compiler_params=_params(2),
        name="modulation",
    )(cond_rows, w_mod, b_mod.reshape(depth, 1, 6 * D_MODEL))


def _cast_plan(cast, grid):
    steps = int(np.prod(grid))
    in_specs, args, out_specs, out_shapes = [], [], [], []

    def step(*idx):
        t = idx[0]
        for n, i in zip(grid[1:], idx[1:]):
            t = t * n + i
        return t

    for stack, layer in cast:
        _, rows, cols = stack.shape
        n_chunks = rows // 16
        while n_chunks > steps:
            n_chunks //= 2
        assert rows % n_chunks == 0
        per = steps // n_chunks
        chunk = lambda *idx, per=per, last=n_chunks - 1: jnp.minimum(step(*idx) // per, last)
        in_specs.append(pl.BlockSpec((None, rows // n_chunks, cols),
                                     lambda *idx, layer=layer, chunk=chunk: (layer, chunk(*idx), 0)))
        out_specs.append(pl.BlockSpec((rows // n_chunks, cols), lambda *idx, chunk=chunk: (chunk(*idx), 0)))
        out_shapes.append(jax.ShapeDtypeStruct((rows, cols), BF16))
        args.append(stack)
    return in_specs, args, out_specs, out_shapes


def _cast_chunks(src_refs, dst_refs):
    for src, dst in zip(src_refs, dst_refs, strict=True):
        dst[...] = src[...].astype(BF16)


def _in_proj_kernel(*refs, rope, kv_only, n_cast):
    n_in = 8 if rope else 6
    ins, cast_src = refs[:n_in], refs[n_in:n_in + n_cast]
    outs, cast_dst = refs[n_in + n_cast:len(refs) - n_cast], refs[len(refs) - n_cast:]
    _cast_chunks(cast_src, cast_dst)
    if rope:
        x_ref, mod_ref, gpre_ref, w_ref, pavg_ref, gn_ref, rrow_ref, rcol_ref = ins
    else:
        x_ref, mod_ref, gpre_ref, w_ref, pavg_ref, gn_ref = ins
    tm = x_ref.shape[1]

    def prepare():
        shift = mod_ref[0, :, 0:D_MODEL]
        gain = gpre_ref[...] * (1.0 + mod_ref[0, :, D_MODEL:2 * D_MODEL])
        return (_rms(x_ref[0]) * gain + shift).astype(BF16)

    def project(hb):
        def proj(lo, hi):
            return _dot(hb, w_ref[:, lo:hi])

        if rope:
            n_rows = tm // GRID_W
            rot = []
            for t in range(2):
                by_row = jnp.concatenate(
                    [jnp.broadcast_to(rrow_ref[t, r:r + 1, :], (GRID_W, HEAD_DIM)) for r in range(n_rows)], axis=0)
                rot.append(by_row + jnp.concatenate([rcol_ref[t]] * n_rows, axis=0))

        def rope_store(z, dst):
            if not rope:
                dst[0] = z.astype(BF16)
                return
            cos, sin = rot
            for hd in range(HEADS):
                a = z[:, hd * HEAD_DIM:(hd + 1) * HEAD_DIM]
                r = a * cos + pltpu.roll(a, HEAD_DIM // 2, 1) * sin
                dst[0, :, hd * HEAD_DIM:(hd + 1) * HEAD_DIM] = r.astype(BF16)

        if kv_only:
            k_ref, v_ref = outs
        else:
            q_ref, k_ref, v_ref, sg_ref, f_ref, u_ref, vn_ref, gate_ref = outs
            rope_store(proj(0, RET_W), q_ref)
        rope_store(proj(RET_W, 2 * RET_W) * (HEAD_DIM ** -0.5), k_ref)
        v_ref[0] = proj(2 * RET_W, 3 * RET_W).astype(BF16)
        if kv_only:
            return
        g = proj(3 * RET_W, 4 * RET_W)
        sg_ref[0] = (g * _sigmoid(g)).astype(BF16)

        f = proj(COL_F, COL_U)
        for hf in range(2):
            fh = f[:, hf * HALF:(hf + 1) * HALF]
            if rope:
                for n1 in range(tm // FFT_N2):
                    for nj in range(FFT_N2 // FFT_STEP):
                        f_ref[0, hf, nj, n1 * FFT_STEP:(n1 + 1) * FFT_STEP, :] = (
                            fh[n1 * FFT_N2 + nj * FFT_STEP:n1 * FFT_N2 + (nj + 1) * FFT_STEP])
            else:
                f_ref[0, hf] = fh

        u_ref[0] = _gelu(proj(COL_U, COL_VS)).astype(BF16)
        vg = _gelu(proj(COL_VS, COL_GATE))
        sh, sl = _split(vg * vg)
        pavg = pavg_ref[...]
        ms = _dot(sh, pavg) + _dot(sl, pavg)
        vn_ref[0] = (vg * lax.rsqrt(ms + EPS) * gn_ref[...]).astype(BF16)

        for c in range(3):
            z = proj(COL_GATE + c * D_MODEL, COL_GATE + (c + 1) * D_MODEL)
            gate_ref[0, :, c * D_MODEL:(c + 1) * D_MODEL] = _sigmoid(z).astype(BF16)

    project(prepare())


def _mod_spec(layer, ctx):
    idx = (lambda b, i: (layer, 2, 0, 0)) if ctx else (lambda b, i: (layer, b, 0, 0))
    return pl.BlockSpec((None, 1, 1, 6 * D_MODEL), idx)


def _in_proj(xs, layer, mod, g_pre, w_in, pavg, g_norm, rope_tabs, *, tm, ctx, kv_only=False, cast=()):
    nb, rows, _ = xs.shape
    rope = rope_tabs is not None
    grid = (nb, rows // tm)
    tok_spec = lambda w: pl.BlockSpec((1, tm, w), lambda b, i: (b, i, 0))
    in_specs = [
        tok_spec(D_MODEL),
        _mod_spec(layer, ctx),
        _layer_spec((1, D_MODEL), layer),
        _const_spec((D_MODEL, IN_W)),
        _const_spec((GMLP_W, GMLP_W)),
        _layer_spec((1, GMLP_W), layer),
    ]
    args = [xs, mod, g_pre, w_in, pavg, g_norm]
    if rope:
        in_specs += [pl.BlockSpec((2, tm // GRID_W, HEAD_DIM), lambda b, i: (0, i, 0)),
                     _const_spec((2, GRID_W, HEAD_DIM))]
        args += list(rope_tabs)
    c_in_specs, c_args, c_out_specs, c_out_shapes = _cast_plan(cast, grid)
    in_specs += c_in_specs
    args += c_args
    bf = lambda w: jax.ShapeDtypeStruct((nb, rows, w), BF16)
    if kv_only:
        out_shapes = (bf(RET_W), bf(RET_W))
        out_specs = (tok_spec(RET_W), tok_spec(RET_W))
    else:
        if rope:
            n_nj = FFT_N2 // FFT_STEP
            f_shape = jax.ShapeDtypeStruct((nb, 2, n_nj, FFT_N1 * FFT_STEP, HALF), F32)
            f_spec = pl.BlockSpec((1, 2, n_nj, (tm // FFT_N2) * FFT_STEP, HALF), lambda b, i: (b, 0, 0, i, 0))
        else:
            f_shape = jax.ShapeDtypeStruct((nb, 2, rows, HALF), F32)
            f_spec = pl.BlockSpec((1, 2, tm, HALF), lambda b, i: (b, 0, i, 0))
        out_shapes = (
            bf(RET_W), bf(RET_W), bf(RET_W), bf(RET_W),
            f_shape,
            bf(GMLP_W), bf(GMLP_W),
            bf(3 * D_MODEL),
        )
        out_specs = (
            tok_spec(RET_W), tok_spec(RET_W), tok_spec(RET_W), tok_spec(RET_W),
            f_spec,
            tok_spec(GMLP_W), tok_spec(GMLP_W), tok_spec(3 * D_MODEL),
        )
    return pl.pallas_call(
        functools.partial(_in_proj_kernel, rope=rope, kv_only=kv_only, n_cast=len(cast)),
        grid=grid,
        in_specs=in_specs,
        out_specs=tuple(out_specs) + tuple(c_out_specs),
        out_shape=tuple(out_shapes) + tuple(c_out_shapes),
        compiler_params=_params(2),
        name="in_proj_ctx" if ctx else "in_proj",
    )(*args)


def _ret_kernel(*refs, ctx_out, n_cast):
    n_in, n_out = (9, 2) if ctx_out else (7, 1)
    ins, refs = refs[:n_in], refs[n_in:]
    cast_src, refs = refs[:n_cast], refs[n_cast:]
    outs, refs = refs[:n_out], refs[n_out:]
    cast_dst, scratch = refs[:n_cast], refs[n_cast:]
    _cast_chunks(cast_src, cast_dst)
    (logit_ref, ql_ref, kl_ref, vl_ref, sgl_ref, kc_ref, vc_ref) = ins[:7]
    if ctx_out:
        qc_ref, sgc_ref = ins[7:]
        ol_ref, oc_ref = outs
    else:
        (ol_ref,) = outs
    mask_ref, wf_ref, wb_ref, qf_ref, qb_ref, df_ref, db_ref, sf_ref, sb_ref, sball_ref = scratch
    b = pl.program_id(0)
    phase = pl.program_id(1)
    j = pl.program_id(2)

    @pl.when((b == 0) & (phase == 0) & (j == 0))
    def _():
        x = logit_ref[...]
        lg = -(jnp.maximum(-x, 0.0) + jnp.log(1.0 + jnp.exp(-jnp.abs(x))))
        row = lax.broadcasted_iota(jnp.int32, (RCHUNK, RCHUNK), 0).astype(F32)
        col = lax.broadcasted_iota(jnp.int32, (RCHUNK, RCHUNK), 1).astype(F32)
        diff = row - col
        pos = row[:, :HEAD_DIM]
        for hd in range(HEADS):
            lf = lg[hd:hd + 1, :]
            lb = lg[HEADS + hd:HEADS + hd + 1, :]
            mask_ref[hd] = jnp.where(diff >= 0.0, jnp.exp(lf * jnp.maximum(diff, 0.0)),
                                     jnp.exp(lb * jnp.maximum(-diff, 0.0)))
            wf_ref[hd] = jnp.exp(lf * (RCHUNK - 1.0 - pos))
            wb_ref[hd] = jnp.exp(lb * pos)
            qf_ref[hd] = jnp.exp(lf * (pos + 1.0))
            qb_ref[hd] = jnp.exp(lb * (RCHUNK - pos))
            df_ref[hd] = jnp.exp(jnp.broadcast_to(lf, (HEAD_DIM, HEAD_DIM)) * float(RCHUNK))
            db_ref[hd] = jnp.exp(jnp.broadcast_to(lb, (HEAD_DIM, HEAD_DIM)) * float(RCHUNK))

    def chunk_updates(hd, k_ref, v_ref, n_chunks, w_ref):
        cols = slice(hd * HEAD_DIM, (hd + 1) * HEAD_DIM)
        us = []
        for c in range(n_chunks):
            rows = slice(c * RCHUNK, (c + 1) * RCHUNK)
            kw = (k_ref[0, rows, cols].astype(F32) * w_ref[hd]).astype(BF16)
            us.append(lax.dot_general(kw, v_ref[0, rows, cols], (((0,), (0,)), ((), ())),
                                      preferred_element_type=F32))
        return us

    def backward_chunks(k_ref, v_ref, n_chunks, first_chunk):
        for hd in range(HEADS):
            us = chunk_updates(hd, k_ref, v_ref, n_chunks, wb_ref)
            s = sb_ref[hd]
            for c in reversed(range(n_chunks)):
                sball_ref[first_chunk + c, hd] = s.astype(BF16)
                s = db_ref[hd] * s + us[c]
            sb_ref[hd] = s

    def forward_chunks(q_ref, k_ref, v_ref, sg_ref, o_ref, n_chunks, first_chunk):
        starts = []
        for hd in range(HEADS):
            us = chunk_updates(hd, k_ref, v_ref, n_chunks, wf_ref)
            st = [sf_ref[hd]]
            for c in range(n_chunks):
                st.append(df_ref[hd] * st[-1] + us[c])
            sf_ref[hd] = st[-1]
            starts.append(st)
        if o_ref is None:
            return
        units = [(c, hd) for c in range(n_chunks) for hd in range(HEADS)]

        def scores(c, hd):
            rows = slice(c * RCHUNK, (c + 1) * RCHUNK)
            cols = slice(hd * HEAD_DIM, (hd + 1) * HEAD_DIM)
            return lax.dot_general(q_ref[0, rows, cols], k_ref[0, rows, cols], (((1,), (1,)), ((), ())),
                                   preferred_element_type=F32)

        ahead = min(RET_AHEAD, len(units))
        pending = [scores(*u) for u in units[:ahead]]
        for n, (c, hd) in enumerate(units):
            if n + ahead < len(units):
                pending.append(scores(*units[n + ahead]))
            sc = pending.pop(0)
            rows = slice(c * RCHUNK, (c + 1) * RCHUNK)
            cols = slice(hd * HEAD_DIM, (hd + 1) * HEAD_DIM)
            qf = q_ref[0, rows, cols].astype(F32)
            lhs = jnp.concatenate([(sc * mask_ref[hd]).astype(BF16),
                                   (qf * qf_ref[hd]).astype(BF16),
                                   (qf * qb_ref[hd]).astype(BF16)], axis=1)
            rhs = jnp.concatenate([v_ref[0, rows, cols], starts[hd][c].astype(BF16),
                                   sball_ref[first_chunk + c, hd]], axis=0)
            o = _dot(lhs, rhs)
            o_ref[0, rows, cols] = (_rms(o) * sg_ref[0, rows, cols].astype(F32)).astype(BF16)

    @pl.when(phase == 0)
    def _():
        @pl.when(j == 0)
        def _():
            sb_ref[...] = jnp.zeros_like(sb_ref)
            backward_chunks(kc_ref, vc_ref, 1, 0)

        backward_chunks(kl_ref, vl_ref, RET_CPT, 1 + (RET_STEPS - 1 - j) * RET_CPT)

    @pl.when(phase == 1)
    def _():
        @pl.when(j == 0)
        def _():
            sf_ref[...] = jnp.zeros_like(sf_ref)
            if ctx_out:
                forward_chunks(qc_ref, kc_ref, vc_ref, sgc_ref, oc_ref, 1, 0)
            else:
                forward_chunks(None, kc_ref, vc_ref, None, None, 1, 0)

        forward_chunks(ql_ref, kl_ref, vl_ref, sgl_ref, ol_ref, RET_CPT, 1 + j * RET_CPT)


def _retention(layer, logits, lat, ctx, *, ctx_out, cast=()):
    nb = lat[0].shape[0]
    kv_idx = lambda b, p, j: (b, jnp.where(p == 0, RET_STEPS - 1 - j, j), 0)
    q_idx = lambda b, p, j: (b, jnp.where(p == 0, 0, j), 0)
    c_idx = lambda b, p, j: (b, 0, 0)
    lat_blk = (1, RET_TILE, RET_W)
    ctx_blk = (1, CTX_LEN, RET_W)
    in_specs = [
        _layer_spec((2 * HEADS, 1), layer),
        pl.BlockSpec(lat_blk, q_idx), pl.BlockSpec(lat_blk, kv_idx), pl.BlockSpec(lat_blk, kv_idx),
        pl.BlockSpec(lat_blk, q_idx),
        pl.BlockSpec(ctx_blk, c_idx), pl.BlockSpec(ctx_blk, c_idx),
    ]
    if ctx_out:
        qc, kc, vc, sgc = ctx
        args = [logits, *lat, kc, vc, qc, sgc]
        in_specs += [pl.BlockSpec(ctx_blk, c_idx), pl.BlockSpec(ctx_blk, c_idx)]
        out_specs = (pl.BlockSpec(lat_blk, q_idx), pl.BlockSpec(ctx_blk, c_idx))
        out_shape = (jax.ShapeDtypeStruct((nb, SEQ, RET_W), BF16), jax.ShapeDtypeStruct((nb, CTX_LEN, RET_W), BF16))
    else:
        kc, vc = ctx
        args = [logits, *lat, kc, vc]
        out_specs = (pl.BlockSpec(lat_blk, q_idx),)
        out_shape = (jax.ShapeDtypeStruct((nb, SEQ, RET_W), BF16),)
    grid = (nb, 2, RET_STEPS)
    c_in_specs, c_args, c_out_specs, c_out_shapes = _cast_plan(cast, grid)
    state = pltpu.VMEM((HEADS, HEAD_DIM, HEAD_DIM), F32)
    pos_tab = pltpu.VMEM((HEADS, RCHUNK, HEAD_DIM), F32)
    return pl.pallas_call(
        functools.partial(_ret_kernel, ctx_out=ctx_out, n_cast=len(cast)),
        grid=grid,
        in_specs=in_specs + c_in_specs,
        out_specs=out_specs + tuple(c_out_specs),
        out_shape=out_shape + tuple(c_out_shapes),
        scratch_shapes=[
            pltpu.VMEM((HEADS, RCHUNK, RCHUNK), F32),
            pos_tab, pos_tab, pos_tab, pos_tab,
            state, state,
            state, state,
            pltpu.VMEM((N_RCHUNKS, HEADS, HEAD_DIM, HEAD_DIM), BF16),
        ],
        compiler_params=_params(3),
        name="retention",
    )(*args, *c_args)


def _dft_cos_sin(n):
    idx = np.arange(n, dtype=np.float64)
    ang = 2.0 * np.pi * ((idx[:, None] * idx[None, :]) % n) / n
    return np.cos(ang), np.sin(ang)


def _channel_dft(scale):
    c, s = _dft_cos_sin(FNET_GROUP)
    eye = np.eye(FNET_W // FNET_GROUP)
    return np.concatenate([np.kron(eye, c), np.kron(eye, s)], axis=0) * scale


def _fft_constants():
    k1 = np.arange(FFT_N1, dtype=np.float64)[None, :, None]
    n1 = np.arange(FFT_N1, dtype=np.float64)[None, None, :]
    n2 = np.arange(FFT_N2, dtype=np.float64)[:, None, None]
    ang = 2.0 * np.pi * ((k1 * (FFT_N2 * n1 + n2)) % SEQ) / SEQ
    m1 = np.concatenate([np.cos(ang), -np.sin(ang)], axis=1)
    c, s = _dft_cos_sin(FFT_N2)
    m3 = np.block([[c, s], [-s, c]])
    return (_np_split_cat(m1, -1), jnp.asarray(m3, F32),
            jnp.asarray(_channel_dft((SEQ * FNET_GROUP) ** -0.5), F32))


def _ctx_fft_constants():
    c, s = _dft_cos_sin(CTX_LEN)
    return (jnp.asarray(np.concatenate([c, -s], axis=0), F32),
            jnp.asarray(_channel_dft((CTX_LEN * FNET_GROUP) ** -0.5), F32))


def _np_split_cat(a64, axis):
    hi, lo = _np_split(a64)
    return jnp.asarray(np.concatenate([hi, lo], axis=axis))


def _twice(x, axis):
    return jnp.concatenate([x, x], axis=axis)


def _fft_kernel(x_ref, m_ref, w_ref, cd_ref, o_ref, a_ref):
    n_kb = FFT_N1 // FFT_STEP

    def stage1(nj, carry):
        for jj in range(FFT_STEP):
            x = jnp.concatenate([x_ref[hf, nj, pl.ds(jj, FFT_N1, stride=FFT_STEP), :] for hf in range(2)],
                                axis=1).astype(BF16)
            a = _dot(m_ref[nj * FFT_STEP + jj], _twice(x, 0))
            row = pl.multiple_of((nj * FFT_STEP + jj) * FFT_STEP, FFT_STEP)
            for hf in range(2):
                for ri in range(2):
                    for kb in range(n_kb):
                        r0 = ri * FFT_N1 + kb * FFT_STEP
                        a_ref[hf, ri, kb, pl.ds(row, FFT_STEP), :] = a[r0:r0 + FFT_STEP, hf * HALF:(hf + 1) * HALF]
        return carry

    lax.fori_loop(0, FFT_N2 // FFT_STEP, stage1, 0)

    def stage2(kb, carry):
        w = w_ref[...].astype(BF16)
        zs = []
        for jj in range(FFT_STEP):
            rhs = jnp.concatenate(
                [jnp.concatenate([a_ref[hf, ri, kb, pl.ds(jj, FFT_N2, stride=FFT_STEP), :] for ri in range(2)],
                                 axis=0) for hf in range(2)], axis=1).astype(BF16)
            y = _dot(w, rhs)
            zs.append(jnp.concatenate([y[:FFT_N2], y[FFT_N2:]], axis=1).astype(BF16))
        z = jnp.concatenate(zs, axis=0)
        four = _dot(z, cd_ref[...].astype(BF16))
        for jj in range(FFT_STEP):
            row = pl.multiple_of((kb * FFT_STEP + jj) * MIX_K2, MIX_K2)
            for kt in range(FFT_N2 // MIX_K2):
                for hf in range(2):
                    r0 = jj * FFT_N2 + kt * MIX_K2
                    o_ref[hf, kt, pl.ds(row, MIX_K2), :] = four[r0:r0 + MIX_K2, hf * HALF:(hf + 1) * HALF]
        return carry

    lax.fori_loop(0, n_kb, stage2, 0)


def _fourier_latent(f_blk):
    nb = f_blk.shape[0]
    m1, m3, cd = _fft_constants()
    n_nj, n_kb = FFT_N2 // FFT_STEP, FFT_N1 // FFT_STEP
    return pl.pallas_call(
        _fft_kernel,
        grid=(nb,),
        in_specs=[
            pl.BlockSpec((None, 2, n_nj, FFT_N1 * FFT_STEP, HALF), lambda b: (b, 0, 0, 0, 0),
                         pipeline_mode=pl.Buffered(1)),
            _const_spec((FFT_N2, 2 * FFT_N1, 2 * FFT_N1)),
            _const_spec((2 * FFT_N2, 2 * FFT_N2)), _const_spec((2 * FNET_W, FNET_W)),
        ],
        out_specs=pl.BlockSpec((None, 2, FFT_N2 // MIX_K2, FFT_N1 * MIX_K2, HALF), lambda b: (b, 0, 0, 0, 0)),
        out_shape=jax.ShapeDtypeStruct((nb, 2, FFT_N2 // MIX_K2, FFT_N1 * MIX_K2, HALF), F32),
        scratch_shapes=[pltpu.VMEM((2, 2, n_kb, FFT_N2 * FFT_STEP, HALF), F32)],
        compiler_params=_params(1),
        name="fourier",
    )(f_blk, m1, m3, cd)


def _fft_ctx_kernel(x_ref, w_ref, cd_ref, o_ref):
    x = jnp.concatenate([x_ref[hf] for hf in range(2)], axis=1).astype(BF16)
    z = _dot(w_ref[...].astype(BF16), x)
    z = jnp.concatenate([z[:CTX_LEN], z[CTX_LEN:]], axis=1).astype(BF16)
    four = _dot(z, cd_ref[...].astype(BF16))
    for hf in range(2):
        o_ref[hf] = four[:, hf * HALF:(hf + 1) * HALF]


def _fourier_ctx(f_ctx):
    nb = f_ctx.shape[2] // CTX_LEN
    w, cd = _ctx_fft_constants()
    return pl.pallas_call(
        _fft_ctx_kernel,
        grid=(nb,),
        in_specs=[
            pl.BlockSpec((None, 2, CTX_LEN, HALF), lambda b: (0, 0, b, 0)),
            _const_spec((2 * CTX_LEN, CTX_LEN)), _const_spec((2 * FNET_W, FNET_W)),
        ],
        out_specs=pl.BlockSpec((None, 2, CTX_LEN, HALF), lambda b: (0, 0, b, 0)),
        out_shape=jax.ShapeDtypeStruct(f_ctx.shape, F32),
        compiler_params=_params(1),
        name="fourier_ctx",
    )(f_ctx, w, cd)


def _mix_mlp_kernel(*refs, tm, ctx, n_cast):
    (x_ref, ret_ref, four_ref, u_ref, vn_ref, gate_ref, mod_ref, ws_ref, bs_ref,
     wa_ref, wb_ref, wc_ref, wo_ref, gpm_ref, gprm_ref, gpom_ref, wup_ref, wdn_ref) = refs[:18]
    o_ref = refs[18 + n_cast]
    _cast_chunks(refs[18:18 + n_cast], refs[19 + n_cast:])
    group = lax.broadcasted_iota(jnp.int32, (GMLP_CHUNK, GMLP_W), 1) // GMLP_GROUP
    parts = []
    for cc in range(tm // GMLP_CHUNK):
        rows = slice(cc * GMLP_CHUNK, (cc + 1) * GMLP_CHUNK)
        vn = vn_ref[0, rows, :]
        s = bs_ref[...]
        for g in range(GMLP_W // GMLP_GROUP):
            s = s + jnp.where(group == g, _dot(ws_ref[g], vn), 0.0)
        parts.append((u_ref[0, rows, :].astype(F32) * s).astype(BF16))
    sgu = jnp.concatenate(parts, axis=0)

    if ctx:
        four = jnp.concatenate([four_ref[0, hf] for hf in range(2)], axis=1).astype(BF16)
    else:
        nk2 = tm // FFT_N1
        four = jnp.concatenate(
            [jnp.concatenate([four_ref[0, hf, pl.ds(k2, FFT_N1, stride=nk2), :] for k2 in range(nk2)], axis=0)
             for hf in range(2)], axis=1).astype(BF16)
    ga = gate_ref[0, :, 0:D_MODEL].astype(F32)
    gb = gate_ref[0, :, D_MODEL:2 * D_MODEL].astype(F32)
    gc = gate_ref[0, :, 2 * D_MODEL:3 * D_MODEL].astype(F32)
    merged = (ga * _dot(ret_ref[0], wa_ref[...]) + gb * _dot(four, wb_ref[...])
              + gc * _dot(sgu, wc_ref[...])).astype(BF16)

    gain1 = mod_ref[0, :, 2 * D_MODEL:3 * D_MODEL] * gpm_ref[...]
    shift2 = mod_ref[0, :, 3 * D_MODEL:4 * D_MODEL]
    gain2 = gprm_ref[...] * (1.0 + mod_ref[0, :, 4 * D_MODEL:5 * D_MODEL])
    gain3 = mod_ref[0, :, 5 * D_MODEL:6 * D_MODEL] * gpom_ref[...]

    halves = [slice(r * (tm // 2), (r + 1) * (tm // 2)) for r in range(2)]
    x1, h2 = [], []
    for rows in halves:
        y = _dot(merged[rows], wo_ref[...])
        x1.append(x_ref[0, rows, :] + _rms(y) * gain1)
        h2.append((_rms(x1[-1]) * gain2 + shift2).astype(BF16))
    up0 = jnp.concatenate([_dot(h, wup_ref[:, 0:FF_CHUNK]) for h in h2], axis=0)
    h2 = jnp.concatenate(h2, axis=0)

    n_ff = D_FF // FF_CHUNK
    m = None
    for c in range(n_ff - 1):
        cols = slice(c * FF_CHUNK, (c + 1) * FF_CHUNK)
        a = jnp.maximum(up0 if c == 0 else _dot(h2, wup_ref[:, cols]), 0.0)
        part = _dot((a * a).astype(BF16), wdn_ref[cols, :])
        m = part if m is None else m + part
    cols = slice((n_ff - 1) * FF_CHUNK, n_ff * FF_CHUNK)
    a = jnp.maximum(_dot(h2, wup_ref[:, cols]), 0.0)
    a = (a * a).astype(BF16)
    for r, rows in enumerate(halves):
        mr = m[rows] + _dot(a[rows], wdn_ref[cols, :])
        o_ref[0, rows, :] = x1[r] + _rms(mr) * gain3


def _mix_mlp(xs, ret, four, ug, vn, gates, layer, mod, ws, bs_tab, wa, wb, wc, wo,
             g_post_mix, g_pre_mlp, g_post_mlp, w_up, w_down, *, tm, ctx, cast=()):
    nb, rows, _ = xs.shape
    tok_spec = lambda w: pl.BlockSpec((1, tm, w), lambda b, i: (b, i, 0))
    lspec = lambda *shape: _layer_spec(shape, layer)
    if ctx:
        four_spec = pl.BlockSpec((1, 2, tm, HALF), lambda b, i: (b, 0, i, 0))
    else:
        four_spec = pl.BlockSpec((1, 2, None, tm, HALF), lambda b, i: (b, 0, i, 0, 0))
    c_in_specs, c_args, c_out_specs, c_out_shapes = _cast_plan(cast, (nb, rows // tm))
    outs = pl.pallas_call(
        functools.partial(_mix_mlp_kernel, tm=tm, ctx=ctx, n_cast=len(cast)),
        grid=(nb, rows // tm),
        in_specs=[
            tok_spec(D_MODEL), tok_spec(RET_W),
            four_spec,
            tok_spec(GMLP_W), tok_spec(GMLP_W), tok_spec(3 * D_MODEL),
            _mod_spec(layer, ctx),
            lspec(GMLP_W // GMLP_GROUP, GMLP_CHUNK, GMLP_CHUNK),
            lspec(GMLP_CHUNK, GMLP_W),
            _const_spec((RET_W, D_MODEL)), _const_spec((FNET_W, D_MODEL)), _const_spec((GMLP_W, D_MODEL)),
            _const_spec((D_MODEL, D_MODEL)),
            lspec(1, D_MODEL), lspec(1, D_MODEL), lspec(1, D_MODEL),
            _const_spec((D_MODEL, D_FF)), _const_spec((D_FF, D_MODEL)),
        ] + c_in_specs,
        out_specs=(tok_spec(D_MODEL),) + tuple(c_out_specs),
        out_shape=(jax.ShapeDtypeStruct((nb, rows, D_MODEL), F32),) + tuple(c_out_shapes),
        compiler_params=_params(2),
        name="mix_mlp_ctx" if ctx else "mix_mlp",
    )(xs, ret, four, ug, vn, gates, mod, ws, bs_tab, wa, wb, wc, wo,
      g_post_mix, g_pre_mlp, g_post_mlp, w_up, w_down, *c_args)
    return outs if cast else outs[0]


def _rope_tables():
    rows = SEQ // GRID_W
    freqs = ROPE_BASE ** (-jnp.arange(ROPE_FREQS, dtype=F32) / ROPE_FREQS)
    ang_r = jnp.arange(rows, dtype=F32)[:, None] * freqs
    ang_c = jnp.arange(GRID_W, dtype=F32)[:, None] * freqs

    def lanes(a, first):
        z = jnp.zeros_like(a)
        return jnp.concatenate([a, z] if first else [z, a], axis=-1)

    def pair(ang, first):
        cos, sin = lanes(jnp.cos(ang), first), lanes(jnp.sin(ang), first)
        return jnp.stack([jnp.concatenate([cos, cos], axis=-1), jnp.concatenate([-sin, sin], axis=-1)])

    return pair(ang_r, True), pair(ang_c, False)


def kernel(x, c, ctx, c_ctx, w_mod, b_mod, g_pre_mix, g_post_mix, g_pre_mlp, g_post_mlp, w_in,
           ret_decay_logit, sgu_w_s, sgu_b_s, sgu_norm, w_branch_a, w_branch_b, w_branch_c, w_out,
           w_up, w_down):
    nb = x.shape[0]
    depth = w_mod.shape[0]
    assert x.shape == (nb, SEQ, D_MODEL) and ctx.shape == (nb, CTX_LEN, D_MODEL) and nb == 2

    cond_rows = jnp.concatenate([c, c_ctx[None, :], jnp.zeros((8 - nb - 1, D_MODEL), F32)], axis=0)
    mod = _modulation(cond_rows, w_mod, b_mod).reshape(depth, 8, 1, 6 * D_MODEL)
    rope_tabs = _rope_tables()
    pavg = jnp.asarray(np.kron(np.eye(GMLP_W // GMLP_GROUP), np.full((GMLP_GROUP, GMLP_GROUP), 1.0 / GMLP_GROUP)), BF16)

    vec = lambda p: p[:, None, :]
    g_pre, g_norm = vec(g_pre_mix), vec(sgu_norm)
    logits = ret_decay_logit.reshape(depth, 2 * HEADS, 1)
    ws = sgu_w_s.astype(BF16)
    bs_tab = jnp.repeat(jnp.swapaxes(sgu_b_s, 1, 2), GMLP_GROUP, axis=2)
    gains = (vec(g_post_mix), vec(g_pre_mlp), vec(g_post_mlp))
    w_in_l = w_in[0].astype(BF16)
    ctx = ctx.reshape(1, nb * CTX_LEN, D_MODEL)
    for l in range(depth):
        last = l == depth - 1
        in_args = (l, mod, g_pre, w_in_l, pavg, g_norm)
        *proj, wa, wb, wc, wo = _in_proj(x, *in_args, rope_tabs, tm=TM_IN, ctx=False,
                                         cast=[(w, l) for w in (w_branch_a, w_branch_b, w_branch_c, w_out)])
        q, k, v, sg, f_lat, ug, vn, gates = proj
        mlp_cast = [(w_up, l), (w_down, l)]
        per_batch = lambda a: a.reshape(nb, CTX_LEN, RET_W)
        if last:
            kc, vc = _in_proj(ctx, *in_args, None, tm=nb * CTX_LEN, ctx=True, kv_only=True)
            ret, wup, wdn = _retention(l, logits, (q, k, v, sg), (per_batch(kc), per_batch(vc)), ctx_out=False,
                                       cast=mlp_cast)
        else:
            qc, kc, vc, sgc, f_ctx, ugc, vnc, gatesc = _in_proj(ctx, *in_args, None, tm=nb * CTX_LEN, ctx=True)
            ret, retc, wup, wdn = _retention(l, logits, (q, k, v, sg), tuple(map(per_batch, (qc, kc, vc, sgc))),
                                             ctx_out=True, cast=mlp_cast)
            retc = retc.reshape(1, nb * CTX_LEN, RET_W)
        mix_w = (l, mod, ws, bs_tab, wa, wb, wc, wo, *gains, wup, wdn)
        four = _fourier_latent(f_lat)
        if last:
            x = _mix_mlp(x, ret, four, ug, vn, gates, *mix_w, tm=TM_LAT, ctx=False)
        else:
            x, w_in_l = _mix_mlp(x, ret, four, ug, vn, gates, *mix_w, tm=TM_LAT, ctx=False, cast=[(w_in, l + 1)])
            ctx = _mix_mlp(ctx, retc, _fourier_ctx(f_ctx), ugc, vnc, gatesc, *mix_w, tm=nb * CTX_LEN, ctx=True)
    return x
```

```python
import functools

import numpy as np
import jax
import jax.numpy as jnp
from jax import lax
from jax.experimental import pallas as pl
from jax.experimental.pallas import tpu as pltpu

F32 = jnp.float32
BF16 = jnp.bfloat16

D_MODEL = 1024
SEQ = 8192
CTX_LEN = 256
GRID_W = 64
RET_W = 512
HEADS = 4
HEAD_DIM = 128
ROPE_BASE = 10000.0
ROPE_FREQS = HEAD_DIM // 4
FNET_W = 256
FNET_GROUP = 64
GMLP_W = 256
GMLP_GROUP = 64
GMLP_CHUNK = 128
D_FF = 4 * D_MODEL
EPS = 1e-6
IN_W = 4 * RET_W + FNET_W + 2 * GMLP_W + 3 * D_MODEL
COL_F = 4 * RET_W
COL_U = COL_F + FNET_W
COL_VS = COL_U + GMLP_W
COL_GATE = COL_VS + GMLP_W

TM_LAT = 512
TM_IN = 1024
RCHUNK = 256
RET_TILE = 2048
RET_STEPS = SEQ // RET_TILE
RET_CPT = RET_TILE // RCHUNK
RET_AHEAD = 4
N_RCHUNKS = 1 + SEQ // RCHUNK
FF_CHUNK = 1024
FFT_N1 = 64
FFT_N2 = 128
HALF = 128
FFT_STEP = 16
MIX_K2 = TM_LAT // FFT_N1

_VMEM_LIMIT = 56 * 1024 * 1024


def _dot(a, b):
    return jnp.dot(a, b, preferred_element_type=F32)


def _split(x):
    hi = x.astype(BF16)
    lo = (x - hi.astype(F32)).astype(BF16)
    return hi, lo


def _np_split(a64):
    hi = np.asarray(a64, np.float32).astype(BF16)
    lo = (np.asarray(a64, np.float32) - hi.astype(np.float32)).astype(BF16)
    return hi, lo


def _rms(x):
    return x * lax.rsqrt(jnp.mean(x * x, axis=-1, keepdims=True) + EPS)


def _gelu(x):
    return x * (0.5 * (1.0 + jnp.tanh(0.7978845608028654 * (x + 0.044715 * (x * x * x)))))


def _sigmoid(x):
    return 1.0 / (1.0 + jnp.exp(-x))


def _const_spec(shape, nargs=None):
    zeros = (0,) * len(shape)
    return pl.BlockSpec(shape, lambda *_: zeros, pipeline_mode=pl.Buffered(1))


def _layer_spec(shape, layer):
    idx = (layer,) + (0,) * len(shape)
    return pl.BlockSpec((None,) + tuple(shape), lambda *_: idx, pipeline_mode=pl.Buffered(1))


def _params(n_axes):
    return pltpu.CompilerParams(dimension_semantics=("arbitrary",) * n_axes, vmem_limit_bytes=_VMEM_LIMIT)


def _mod_kernel(a_ref, w_ref, b_ref, o_ref):
    a = a_ref[...]
    a = a * _sigmoid(a)
    ah, al = _split(a)
    w = w_ref[0].astype(BF16)
    o_ref[0] = _dot(ah, w) + _dot(al, w) + b_ref[0]


def _modulation(cond_rows, w_mod, b_mod):
    depth = w_mod.shape[0]
    tn = 1536
    return pl.pallas_call(
        _mod_kernel,
        grid=(depth, (6 * D_MODEL) // tn),
        in_specs=[
            pl.BlockSpec((8, D_MODEL), lambda l, j: (0, 0)),
            pl.BlockSpec((1, D_MODEL, tn), lambda l, j: (l, 0, j)),
            pl.BlockSpec((1, 1, tn), lambda l, j: (l, 0, j)),
        ],
        out_specs=pl.BlockSpec((1, 8, tn), lambda l, j: (l, 0, j)),
        out_shape=jax.ShapeDtypeStruct((depth, 8, 6 * D_MODEL), F32),
        compiler_params=_params(2),
        name="modulation",
    )(cond_rows, w_mod, b_mod.reshape(depth, 1, 6 * D_MODEL))


def _cast_plan(cast, grid):
    steps = int(np.prod(grid))
    in_specs, args, out_specs, out_shapes = [], [], [], []

    def step(*idx):
        t = idx[0]
        for n, i in zip(grid[1:], idx[1:]):
            t = t * n + i
        return t

    for stack, layer in cast:
        _, rows, cols = stack.shape
        n_chunks = rows // 16
        while n_chunks > steps:
            n_chunks //= 2
        assert rows % n_chunks == 0
        per = steps // n_chunks
        chunk = lambda *idx, per=per, last=n_chunks - 1: jnp.minimum(step(*idx) // per, last)
        in_specs.append(pl.BlockSpec((None, rows // n_chunks, cols),
                                     lambda *idx, layer=layer, chunk=chunk: (layer, chunk(*idx), 0)))
        out_specs.append(pl.BlockSpec((rows // n_chunks, cols), lambda *idx, chunk=chunk: (chunk(*idx), 0)))
        out_shapes.append(jax.ShapeDtypeStruct((rows, cols), BF16))
        args.append(stack)
    return in_specs, args, out_specs, out_shapes


def _cast_chunks(src_refs, dst_refs):
    for src, dst in zip(src_refs, dst_refs, strict=True):
        dst[...] = src[...].astype(BF16)


def _in_proj_kernel(*refs, rope, kv_only, n_cast):
    n_in = 8 if rope else 6
    ins, cast_src = refs[:n_in], refs[n_in:n_in + n_cast]
    outs, cast_dst = refs[n_in + n_cast:len(refs) - n_cast], refs[len(refs) - n_cast:]
    _cast_chunks(cast_src, cast_dst)
    if rope:
        x_ref, mod_ref, gpre_ref, w_ref, pavg_ref, gn_ref, rrow_ref, rcol_ref = ins
    else:
        x_ref, mod_ref, gpre_ref, w_ref, pavg_ref, gn_ref = ins
    tm = x_ref.shape[1]

    def prepare():
        shift = mod_ref[0, :, 0:D_MODEL]
        gain = gpre_ref[...] * (1.0 + mod_ref[0, :, D_MODEL:2 * D_MODEL])
        return (_rms(x_ref[0]) * gain + shift).astype(BF16)

    def project(hb):
        def proj(lo, hi):
            return _dot(hb, w_ref[:, lo:hi])

        if rope:
            n_rows = tm // GRID_W
            rot = []
            for t in range(2):
                by_row = jnp.concatenate(
                    [jnp.broadcast_to(rrow_ref[t, r:r + 1, :], (GRID_W, HEAD_DIM)) for r in range(n_rows)], axis=0)
                rot.append(by_row + jnp.concatenate([rcol_ref[t]] * n_rows, axis=0))

        def rope_store(z, dst):
            if not rope:
                dst[0] = z.astype(BF16)
                return
            cos, sin = rot
            for hd in range(HEADS):
                a = z[:, hd * HEAD_DIM:(hd + 1) * HEAD_DIM]
                r = a * cos + pltpu.roll(a, HEAD_DIM // 2, 1) * sin
                dst[0, :, hd * HEAD_DIM:(hd + 1) * HEAD_DIM] = r.astype(BF16)

        if kv_only:
            k_ref, v_ref = outs
        else:
            q_ref, k_ref, v_ref, sg_ref, f_ref, u_ref, vn_ref, gate_ref = outs
            rope_store(proj(0, RET_W), q_ref)
        rope_store(proj(RET_W, 2 * RET_W) * (HEAD_DIM ** -0.5), k_ref)
        v_ref[0] = proj(2 * RET_W, 3 * RET_W).astype(BF16)
        if kv_only:
            return
        g = proj(3 * RET_W, 4 * RET_W)
        sg_ref[0] = (g * _sigmoid(g)).astype(BF16)

        f = proj(COL_F, COL_U)
        for hf in range(2):
            fh = f[:, hf * HALF:(hf + 1) * HALF]
            if rope:
                for n1 in range(tm // FFT_N2):
                    for nj in range(FFT_N2 // FFT_STEP):
                        f_ref[0, hf, nj, n1 * FFT_STEP:(n1 + 1) * FFT_STEP, :] = (
                            fh[n1 * FFT_N2 + nj * FFT_STEP:n1 * FFT_N2 + (nj + 1) * FFT_STEP])
            else:
                f_ref[0, hf] = fh

        u_ref[0] = _gelu(proj(COL_U, COL_VS)).astype(BF16)
        vg = _gelu(proj(COL_VS, COL_GATE))
        sh, sl = _split(vg * vg)
        pavg = pavg_ref[...]
        ms = _dot(sh, pavg) + _dot(sl, pavg)
        vn_ref[0] = (vg * lax.rsqrt(ms + EPS) * gn_ref[...]).astype(BF16)

        for c in range(3):
            z = proj(COL_GATE + c * D_MODEL, COL_GATE + (c + 1) * D_MODEL)
            gate_ref[0, :, c * D_MODEL:(c + 1) * D_MODEL] = _sigmoid(z).astype(BF16)

    project(prepare())


def _mod_spec(layer, ctx):
    idx = (lambda b, i: (layer, 2, 0, 0)) if ctx else (lambda b, i: (layer, b, 0, 0))
    return pl.BlockSpec((None, 1, 1, 6 * D_MODEL), idx)


def _in_proj(xs, layer, mod, g_pre, w_in, pavg, g_norm, rope_tabs, *, tm, ctx, kv_only=False, cast=()):
    nb, rows, _ = xs.shape
    rope = rope_tabs is not None
    grid = (nb, rows // tm)
    tok_spec = lambda w: pl.BlockSpec((1, tm, w), lambda b, i: (b, i, 0))
    in_specs = [
        tok_spec(D_MODEL),
        _mod_spec(layer, ctx),
        _layer_spec((1, D_MODEL), layer),
        _const_spec((D_MODEL, IN_W)),
        _const_spec((GMLP_W, GMLP_W)),
        _layer_spec((1, GMLP_W), layer),
    ]
    args = [xs, mod, g_pre, w_in, pavg, g_norm]
    if rope:
        in_specs += [pl.BlockSpec((2, tm // GRID_W, HEAD_DIM), lambda b, i: (0, i, 0)),
                     _const_spec((2, GRID_W, HEAD_DIM))]
        args += list(rope_tabs)
    c_in_specs, c_args, c_out_specs, c_out_shapes = _cast_plan(cast, grid)
    in_specs += c_in_specs
    args += c_args
    bf = lambda w: jax.ShapeDtypeStruct((nb, rows, w), BF16)
    if kv_only:
        out_shapes = (bf(RET_W), bf(RET_W))
        out_specs = (tok_spec(RET_W), tok_spec(RET_W))
    else:
        if rope:
            n_nj = FFT_N2 // FFT_STEP
            f_shape = jax.ShapeDtypeStruct((nb, 2, n_nj, FFT_N1 * FFT_STEP, HALF), F32)
            f_spec = pl.BlockSpec((1, 2, n_nj, (tm // FFT_N2) * FFT_STEP, HALF), lambda b, i: (b, 0, 0, i, 0))
        else:
            f_shape = jax.ShapeDtypeStruct((nb, 2, rows, HALF), F32)
            f_spec = pl.BlockSpec((1, 2, tm, HALF), lambda b, i: (b, 0, i, 0))
        out_shapes = (
            bf(RET_W), bf(RET_W), bf(RET_W), bf(RET_W),
            f_shape,
            bf(GMLP_W), bf(GMLP_W),
            bf(3 * D_MODEL),
        )
        out_specs = (
            tok_spec(RET_W), tok_spec(RET_W), tok_spec(RET_W), tok_spec(RET_W),
            f_spec,
            tok_spec(GMLP_W), tok_spec(GMLP_W), tok_spec(3 * D_MODEL),
        )
    return pl.pallas_call(
        functools.partial(_in_proj_kernel, rope=rope, kv_only=kv_only, n_cast=len(cast)),
        grid=grid,
        in_specs=in_specs,
        out_specs=tuple(out_specs) + tuple(c_out_specs),
        out_shape=tuple(out_shapes) + tuple(c_out_shapes),
        compiler_params=_params(2),
        name="in_proj_ctx" if ctx else "in_proj",
    )(*args)


def _ret_kernel(*refs, ctx_out, n_cast):
    n_in, n_out = (9, 2) if ctx_out else (7, 1)
    ins, refs = refs[:n_in], refs[n_in:]
    cast_src, refs = refs[:n_cast], refs[n_cast:]
    outs, refs = refs[:n_out], refs[n_out:]
    cast_dst, scratch = refs[:n_cast], refs[n_cast:]
    _cast_chunks(cast_src, cast_dst)
    (logit_ref, ql_ref, kl_ref, vl_ref, sgl_ref, kc_ref, vc_ref) = ins[:7]
    if ctx_out:
        qc_ref, sgc_ref = ins[7:]
        ol_ref, oc_ref = outs
    else:
        (ol_ref,) = outs
    mask_ref, wf_ref, wb_ref, qf_ref, qb_ref, df_ref, db_ref, sf_ref, sb_ref, sball_ref = scratch
    b = pl.program_id(0)
    phase = pl.program_id(1)
    j = pl.program_id(2)

    @pl.when((b == 0) & (phase == 0) & (j == 0))
    def _():
        x = logit_ref[...]
        lg = -(jnp.maximum(-x, 0.0) + jnp.log(1.0 + jnp.exp(-jnp.abs(x))))
        row = lax.broadcasted_iota(jnp.int32, (RCHUNK, RCHUNK), 0).astype(F32)
        col = lax.broadcasted_iota(jnp.int32, (RCHUNK, RCHUNK), 1).astype(F32)
        diff = row - col
        pos = row[:, :HEAD_DIM]
        for hd in range(HEADS):
            lf = lg[hd:hd + 1, :]
            lb = lg[HEADS + hd:HEADS + hd + 1, :]
            mask_ref[hd] = jnp.where(diff >= 0.0, jnp.exp(lf * jnp.maximum(diff, 0.0)),
                                     jnp.exp(lb * jnp.maximum(-diff, 0.0)))
            wf_ref[hd] = jnp.exp(lf * (RCHUNK - 1.0 - pos))
            wb_ref[hd] = jnp.exp(lb * pos)
            qf_ref[hd] = jnp.exp(lf * (pos + 1.0))
            qb_ref[hd] = jnp.exp(lb * (RCHUNK - pos))
            df_ref[hd] = jnp.exp(jnp.broadcast_to(lf, (HEAD_DIM, HEAD_DIM)) * float(RCHUNK))
            db_ref[hd] = jnp.exp(jnp.broadcast_to(lb, (HEAD_DIM, HEAD_DIM)) * float(RCHUNK))

    def chunk_updates(hd, k_ref, v_ref, n_chunks, w_ref):
        cols = slice(hd * HEAD_DIM, (hd + 1) * HEAD_DIM)
        us = []
        for c in range(n_chunks):
            rows = slice(c * RCHUNK, (c + 1) * RCHUNK)
            kw = (k_ref[0, rows, cols].astype(F32) * w_ref[hd]).astype(BF16)
            us.append(lax.dot_general(kw, v_ref[0, rows, cols], (((0,), (0,)), ((), ())),
                                      preferred_element_type=F32))
        return us

    def backward_chunks(k_ref, v_ref, n_chunks, first_chunk):
        for hd in range(HEADS):
            us = chunk_updates(hd, k_ref, v_ref, n_chunks, wb_ref)
            s = sb_ref[hd]
            for c in reversed(range(n_chunks)):
                sball_ref[first_chunk + c, hd] = s.astype(BF16)
                s = db_ref[hd] * s + us[c]
            sb_ref[hd] = s

    def forward_chunks(q_ref, k_ref, v_ref, sg_ref, o_ref, n_chunks, first_chunk):
        starts = []
        for hd in range(HEADS):
            us = chunk_updates(hd, k_ref, v_ref, n_chunks, wf_ref)
            st = [sf_ref[hd]]
            for c in range(n_chunks):
                st.append(df_ref[hd] * st[-1] + us[c])
            sf_ref[hd] = st[-1]
            starts.append(st)
        if o_ref is None:
            return
        units = [(c, hd) for c in range(n_chunks) for hd in range(HEADS)]

        def scores(c, hd):
            rows = slice(c * RCHUNK, (c + 1) * RCHUNK)
            cols = slice(hd * HEAD_DIM, (hd + 1) * HEAD_DIM)
            return lax.dot_general(q_ref[0, rows, cols], k_ref[0, rows, cols], (((1,), (1,)), ((), ())),
                                   preferred_element_type=F32)

        ahead = min(RET_AHEAD, len(units))
        pending = [scores(*u) for u in units[:ahead]]
        for n, (c, hd) in enumerate(units):
            if n + ahead < len(units):
                pending.append(scores(*units[n + ahead]))
            sc = pending.pop(0)
            rows = slice(c * RCHUNK, (c + 1) * RCHUNK)
            cols = slice(hd * HEAD_DIM, (hd + 1) * HEAD_DIM)
            qf = q_ref[0, rows, cols].astype(F32)
            lhs = jnp.concatenate([(sc * mask_ref[hd]).astype(BF16),
                                   (qf * qf_ref[hd]).astype(BF16),
                                   (qf * qb_ref[hd]).astype(BF16)], axis=1)
            rhs = jnp.concatenate([v_ref[0, rows, cols], starts[hd][c].astype(BF16),
                                   sball_ref[first_chunk + c, hd]], axis=0)
            o = _dot(lhs, rhs)
            o_ref[0, rows, cols] = (_rms(o) * sg_ref[0, rows, cols].astype(F32)).astype(BF16)

    @pl.when(phase == 0)
    def _():
        @pl.when(j == 0)
        def _():
            sb_ref[...] = jnp.zeros_like(sb_ref)
            backward_chunks(kc_ref, vc_ref, 1, 0)

        backward_chunks(kl_ref, vl_ref, RET_CPT, 1 + (RET_STEPS - 1 - j) * RET_CPT)

    @pl.when(phase == 1)
    def _():
        @pl.when(j == 0)
        def _():
            sf_ref[...] = jnp.zeros_like(sf_ref)
            if ctx_out:
                forward_chunks(qc_ref, kc_ref, vc_ref, sgc_ref, oc_ref, 1, 0)
            else:
                forward_chunks(None, kc_ref, vc_ref, None, None, 1, 0)

        forward_chunks(ql_ref, kl_ref, vl_ref, sgl_ref, ol_ref, RET_CPT, 1 + j * RET_CPT)


def _retention(layer, logits, lat, ctx, *, ctx_out, cast=()):
    nb = lat[0].shape[0]
    kv_idx = lambda b, p, j: (b, jnp.where(p == 0, RET_STEPS - 1 - j, j), 0)
    q_idx = lambda b, p, j: (b, jnp.where(p == 0, 0, j), 0)
    c_idx = lambda b, p, j: (b, 0, 0)
    lat_blk = (1, RET_TILE, RET_W)
    ctx_blk = (1, CTX_LEN, RET_W)
    in_specs = [
        _layer_spec((2 * HEADS, 1), layer),
        pl.BlockSpec(lat_blk, q_idx), pl.BlockSpec(lat_blk, kv_idx), pl.BlockSpec(lat_blk, kv_idx),
        pl.BlockSpec(lat_blk, q_idx),
        pl.BlockSpec(ctx_blk, c_idx), pl.BlockSpec(ctx_blk, c_idx),
    ]
    if ctx_out:
        qc, kc, vc, sgc = ctx
        args = [logits, *lat, kc, vc, qc, sgc]
        in_specs += [pl.BlockSpec(ctx_blk, c_idx), pl.BlockSpec(ctx_blk, c_idx)]
        out_specs = (pl.BlockSpec(lat_blk, q_idx), pl.BlockSpec(ctx_blk, c_idx))
        out_shape = (jax.ShapeDtypeStruct((nb, SEQ, RET_W), BF16), jax.ShapeDtypeStruct((nb, CTX_LEN, RET_W), BF16))
    else:
        kc, vc = ctx
        args = [logits, *lat, kc, vc]
        out_specs = (pl.BlockSpec(lat_blk, q_idx),)
        out_shape = (jax.ShapeDtypeStruct((nb, SEQ, RET_W), BF16),)
    grid = (nb, 2, RET_STEPS)
    c_in_specs, c_args, c_out_specs, c_out_shapes = _cast_plan(cast, grid)
    state = pltpu.VMEM((HEADS, HEAD_DIM, HEAD_DIM), F32)
    pos_tab = pltpu.VMEM((HEADS, RCHUNK, HEAD_DIM), F32)
    return pl.pallas_call(
        functools.partial(_ret_kernel, ctx_out=ctx_out, n_cast=len(cast)),
        grid=grid,
        in_specs=in_specs + c_in_specs,
        out_specs=out_specs + tuple(c_out_specs),
        out_shape=out_shape + tuple(c_out_shapes),
        scratch_shapes=[
            pltpu.VMEM((HEADS, RCHUNK, RCHUNK), F32),
            pos_tab, pos_tab, pos_tab, pos_tab,
            state, state,
            state, state,
            pltpu.VMEM((N_RCHUNKS, HEADS, HEAD_DIM, HEAD_DIM), BF16),
        ],
        compiler_params=_params(3),
        name="retention",
    )(*args, *c_args)


def _dft_cos_sin(n):
    idx = np.arange(n, dtype=np.float64)
    ang = 2.0 * np.pi * ((idx[:, None] * idx[None, :]) % n) / n
    return np.cos(ang), np.sin(ang)


def _channel_dft(scale):
    c, s = _dft_cos_sin(FNET_GROUP)
    eye = np.eye(FNET_W // FNET_GROUP)
    return np.concatenate([np.kron(eye, c), np.kron(eye, s)], axis=0) * scale


def _fft_constants():
    k1 = np.arange(FFT_N1, dtype=np.float64)[None, :, None]
    n1 = np.arange(FFT_N1, dtype=np.float64)[None, None, :]
    n2 = np.arange(FFT_N2, dtype=np.float64)[:, None, None]
    ang = 2.0 * np.pi * ((k1 * (FFT_N2 * n1 + n2)) % SEQ) / SEQ
    m1 = np.concatenate([np.cos(ang), -np.sin(ang)], axis=1)
    c, s = _dft_cos_sin(FFT_N2)
    m3 = np.block([[c, s], [-s, c]])
    return (_np_split_cat(m1, -1), jnp.asarray(m3, F32),
            jnp.asarray(_channel_dft((SEQ * FNET_GROUP) ** -0.5), F32))


def _ctx_fft_constants():
    c, s = _dft_cos_sin(CTX_LEN)
    return (jnp.asarray(np.concatenate([c, -s], axis=0), F32),
            jnp.asarray(_channel_dft((CTX_LEN * FNET_GROUP) ** -0.5), F32))


def _np_split_cat(a64, axis):
    hi, lo = _np_split(a64)
    return jnp.asarray(np.concatenate([hi, lo], axis=axis))


def _twice(x, axis):
    return jnp.concatenate([x, x], axis=axis)


def _fft_kernel(x_ref, m_ref, w_ref, cd_ref, o_ref, a_ref):
    n_kb = FFT_N1 // FFT_STEP

    def stage1(nj, carry):
        for jj in range(FFT_STEP):
            x = jnp.concatenate([x_ref[hf, nj, pl.ds(jj, FFT_N1, stride=FFT_STEP), :] for hf in range(2)],
                                axis=1).astype(BF16)
            a = _dot(m_ref[nj * FFT_STEP + jj], _twice(x, 0))
            row = pl.multiple_of((nj * FFT_STEP + jj) * FFT_STEP, FFT_STEP)
            for hf in range(2):
                for ri in range(2):
                    for kb in range(n_kb):
                        r0 = ri * FFT_N1 + kb * FFT_STEP
                        a_ref[hf, ri, kb, pl.ds(row, FFT_STEP), :] = a[r0:r0 + FFT_STEP, hf * HALF:(hf + 1) * HALF]
        return carry

    kb = pl.program_id(1)

    @pl.when(kb == 0)
    def _():
        lax.fori_loop(0, FFT_N2 // FFT_STEP, stage1, 0)

    w = w_ref[...].astype(BF16)
    zs = []
    for jj in range(FFT_STEP):
        rhs = jnp.concatenate(
            [jnp.concatenate([a_ref[hf, ri, kb, pl.ds(jj, FFT_N2, stride=FFT_STEP), :] for ri in range(2)], axis=0)
             for hf in range(2)], axis=1).astype(BF16)
        y = _dot(w, rhs)
        zs.append(jnp.concatenate([y[:FFT_N2], y[FFT_N2:]], axis=1).astype(BF16))
    z = jnp.concatenate(zs, axis=0)
    four = _dot(z, cd_ref[...].astype(BF16))
    for jj in range(FFT_STEP):
        for kt in range(FFT_N2 // MIX_K2):
            for hf in range(2):
                r0 = jj * FFT_N2 + kt * MIX_K2
                o_ref[hf, kt, jj * MIX_K2:(jj + 1) * MIX_K2, :] = four[r0:r0 + MIX_K2, hf * HALF:(hf + 1) * HALF]


def _fourier_latent(f_blk):
    nb = f_blk.shape[0]
    m1, m3, cd = _fft_constants()
    n_nj, n_kb = FFT_N2 // FFT_STEP, FFT_N1 // FFT_STEP
    return pl.pallas_call(
        _fft_kernel,
        grid=(nb, n_kb),
        in_specs=[
            pl.BlockSpec((None, 2, n_nj, FFT_N1 * FFT_STEP, HALF), lambda b, j: (b, 0, 0, 0, 0)),
            _const_spec((FFT_N2, 2 * FFT_N1, 2 * FFT_N1)),
            _const_spec((2 * FFT_N2, 2 * FFT_N2)), _const_spec((2 * FNET_W, FNET_W)),
        ],
        out_specs=pl.BlockSpec((None, 2, FFT_N2 // MIX_K2, FFT_STEP * MIX_K2, HALF), lambda b, j: (b, 0, 0, j, 0)),
        out_shape=jax.ShapeDtypeStruct((nb, 2, FFT_N2 // MIX_K2, FFT_N1 * MIX_K2, HALF), F32),
        scratch_shapes=[pltpu.VMEM((2, 2, n_kb, FFT_N2 * FFT_STEP, HALF), F32)],
        compiler_params=_params(2),
        name="fourier",
    )(f_blk, m1, m3, cd)


def _fft_ctx_kernel(x_ref, w_ref, cd_ref, o_ref):
    x = jnp.concatenate([x_ref[hf] for hf in range(2)], axis=1).astype(BF16)
    z = _dot(w_ref[...].astype(BF16), x)
    z = jnp.concatenate([z[:CTX_LEN], z[CTX_LEN:]], axis=1).astype(BF16)
    four = _dot(z, cd_ref[...].astype(BF16))
    for hf in range(2):
        o_ref[hf] = four[:, hf * HALF:(hf + 1) * HALF]


def _fourier_ctx(f_ctx):
    nb = f_ctx.shape[2] // CTX_LEN
    w, cd = _ctx_fft_constants()
    return pl.pallas_call(
        _fft_ctx_kernel,
        grid=(nb,),
        in_specs=[
            pl.BlockSpec((None, 2, CTX_LEN, HALF), lambda b: (0, 0, b, 0)),
            _const_spec((2 * CTX_LEN, CTX_LEN)), _const_spec((2 * FNET_W, FNET_W)),
        ],
        out_specs=pl.BlockSpec((None, 2, CTX_LEN, HALF), lambda b: (0, 0, b, 0)),
        out_shape=jax.ShapeDtypeStruct(f_ctx.shape, F32),
        compiler_params=_params(1),
        name="fourier_ctx",
    )(f_ctx, w, cd)


def _mix_mlp_kernel(*refs, tm, ctx, n_cast):
    (x_ref, ret_ref, four_ref, u_ref, vn_ref, gate_ref, mod_ref, ws_ref, bs_ref,
     wa_ref, wb_ref, wc_ref, wo_ref, gpm_ref, gprm_ref, gpom_ref, wup_ref, wdn_ref) = refs[:18]
    o_ref = refs[18 + n_cast]
    _cast_chunks(refs[18:18 + n_cast], refs[19 + n_cast:])
    group = lax.broadcasted_iota(jnp.int32, (GMLP_CHUNK, GMLP_W), 1) // GMLP_GROUP
    parts = []
    for cc in range(tm // GMLP_CHUNK):
        rows = slice(cc * GMLP_CHUNK, (cc + 1) * GMLP_CHUNK)
        vn = vn_ref[0, rows, :]
        s = bs_ref[...]
        for g in range(GMLP_W // GMLP_GROUP):
            s = s + jnp.where(group == g, _dot(ws_ref[g], vn), 0.0)
        parts.append((u_ref[0, rows, :].astype(F32) * s).astype(BF16))
    sgu = jnp.concatenate(parts, axis=0)

    if ctx:
        four = jnp.concatenate([four_ref[0, hf] for hf in range(2)], axis=1).astype(BF16)
    else:
        nk2 = tm // FFT_N1
        four = jnp.concatenate(
            [jnp.concatenate([four_ref[0, hf, pl.ds(k2, FFT_N1, stride=nk2), :] for k2 in range(nk2)], axis=0)
             for hf in range(2)], axis=1).astype(BF16)
    ga = gate_ref[0, :, 0:D_MODEL].astype(F32)
    gb = gate_ref[0, :, D_MODEL:2 * D_MODEL].astype(F32)
    gc = gate_ref[0, :, 2 * D_MODEL:3 * D_MODEL].astype(F32)
    merged = (ga * _dot(ret_ref[0], wa_ref[...]) + gb * _dot(four, wb_ref[...])
              + gc * _dot(sgu, wc_ref[...])).astype(BF16)

    gain1 = mod_ref[0, :, 2 * D_MODEL:3 * D_MODEL] * gpm_ref[...]
    shift2 = mod_ref[0, :, 3 * D_MODEL:4 * D_MODEL]
    gain2 = gprm_ref[...] * (1.0 + mod_ref[0, :, 4 * D_MODEL:5 * D_MODEL])
    gain3 = mod_ref[0, :, 5 * D_MODEL:6 * D_MODEL] * gpom_ref[...]

    halves = [slice(r * (tm // 2), (r + 1) * (tm // 2)) for r in range(2)]
    x1, h2 = [], []
    for rows in halves:
        y = _dot(merged[rows], wo_ref[...])
        x1.append(x_ref[0, rows, :] + _rms(y) * gain1)
        h2.append((_rms(x1[-1]) * gain2 + shift2).astype(BF16))
    up0 = jnp.concatenate([_dot(h, wup_ref[:, 0:FF_CHUNK]) for h in h2], axis=0)
    h2 = jnp.concatenate(h2, axis=0)

    n_ff = D_FF // FF_CHUNK
    m = None
    for c in range(n_ff - 1):
        cols = slice(c * FF_CHUNK, (c + 1) * FF_CHUNK)
        a = jnp.maximum(up0 if c == 0 else _dot(h2, wup_ref[:, cols]), 0.0)
        part = _dot((a * a).astype(BF16), wdn_ref[cols, :])
        m = part if m is None else m + part
    cols = slice((n_ff - 1) * FF_CHUNK, n_ff * FF_CHUNK)
    a = jnp.maximum(_dot(h2, wup_ref[:, cols]), 0.0)
    a = (a * a).astype(BF16)
    for r, rows in enumerate(halves):
        mr = m[rows] + _dot(a[rows], wdn_ref[cols, :])
        o_ref[0, rows, :] = x1[r] + _rms(mr) * gain3


def _mix_mlp(xs, ret, four, ug, vn, gates, layer, mod, ws, bs_tab, wa, wb, wc, wo,
             g_post_mix, g_pre_mlp, g_post_mlp, w_up, w_down, *, tm, ctx, cast=()):
    nb, rows, _ = xs.shape
    tok_spec = lambda w: pl.BlockSpec((1, tm, w), lambda b, i: (b, i, 0))
    lspec = lambda *shape: _layer_spec(shape, layer)
    if ctx:
        four_spec = pl.BlockSpec((1, 2, tm, HALF), lambda b, i: (b, 0, i, 0))
    else:
        four_spec = pl.BlockSpec((1, 2, None, tm, HALF), lambda b, i: (b, 0, i, 0, 0))
    c_in_specs, c_args, c_out_specs, c_out_shapes = _cast_plan(cast, (nb, rows // tm))
    outs = pl.pallas_call(
        functools.partial(_mix_mlp_kernel, tm=tm, ctx=ctx, n_cast=len(cast)),
        grid=(nb, rows // tm),
        in_specs=[
            tok_spec(D_MODEL), tok_spec(RET_W),
            four_spec,
            tok_spec(GMLP_W), tok_spec(GMLP_W), tok_spec(3 * D_MODEL),
            _mod_spec(layer, ctx),
            lspec(GMLP_W // GMLP_GROUP, GMLP_CHUNK, GMLP_CHUNK),
            lspec(GMLP_CHUNK, GMLP_W),
            _const_spec((RET_W, D_MODEL)), _const_spec((FNET_W, D_MODEL)), _const_spec((GMLP_W, D_MODEL)),
            _const_spec((D_MODEL, D_MODEL)),
            lspec(1, D_MODEL), lspec(1, D_MODEL), lspec(1, D_MODEL),
            _const_spec((D_MODEL, D_FF)), _const_spec((D_FF, D_MODEL)),
        ] + c_in_specs,
        out_specs=(tok_spec(D_MODEL),) + tuple(c_out_specs),
        out_shape=(jax.ShapeDtypeStruct((nb, rows, D_MODEL), F32),) + tuple(c_out_shapes),
        compiler_params=_params(2),
        name="mix_mlp_ctx" if ctx else "mix_mlp",
    )(xs, ret, four, ug, vn, gates, mod, ws, bs_tab, wa, wb, wc, wo,
      g_post_mix, g_pre_mlp, g_post_mlp, w_up, w_down, *c_args)
    return outs if cast else outs[0]


def _rope_tables():
    rows = SEQ // GRID_W
    freqs = ROPE_BASE ** (-jnp.arange(ROPE_FREQS, dtype=F32) / ROPE_FREQS)
    ang_r = jnp.arange(rows, dtype=F32)[:, None] * freqs
    ang_c = jnp.arange(GRID_W, dtype=F32)[:, None] * freqs

    def lanes(a, first):
        z = jnp.zeros_like(a)
        return jnp.concatenate([a, z] if first else [z, a], axis=-1)

    def pair(ang, first):
        cos, sin = lanes(jnp.cos(ang), first), lanes(jnp.sin(ang), first)
        return jnp.stack([jnp.concatenate([cos, cos], axis=-1), jnp.concatenate([-sin, sin], axis=-1)])

    return pair(ang_r, True), pair(ang_c, False)


def kernel(x, c, ctx, c_ctx, w_mod, b_mod, g_pre_mix, g_post_mix, g_pre_mlp, g_post_mlp, w_in,
           ret_decay_logit, sgu_w_s, sgu_b_s, sgu_norm, w_branch_a, w_branch_b, w_branch_c, w_out,
           w_up, w_down):
    nb = x.shape[0]
    depth = w_mod.shape[0]
    assert x.shape == (nb, SEQ, D_MODEL) and ctx.shape == (nb, CTX_LEN, D_MODEL) and nb == 2

    cond_rows = jnp.concatenate([c, c_ctx[None, :], jnp.zeros((8 - nb - 1, D_MODEL), F32)], axis=0)
    mod = _modulation(cond_rows, w_mod, b_mod).reshape(depth, 8, 1, 6 * D_MODEL)
    rope_tabs = _rope_tables()
    pavg = jnp.asarray(np.kron(np.eye(GMLP_W // GMLP_GROUP), np.full((GMLP_GROUP, GMLP_GROUP), 1.0 / GMLP_GROUP)), BF16)

    vec = lambda p: p[:, None, :]
    g_pre, g_norm = vec(g_pre_mix), vec(sgu_norm)
    logits = ret_decay_logit.reshape(depth, 2 * HEADS, 1)
    ws = sgu_w_s.astype(BF16)
    bs_tab = jnp.repeat(jnp.swapaxes(sgu_b_s, 1, 2), GMLP_GROUP, axis=2)
    gains = (vec(g_post_mix), vec(g_pre_mlp), vec(g_post_mlp))
    w_in_l = w_in[0].astype(BF16)
    ctx = ctx.reshape(1, nb * CTX_LEN, D_MODEL)
    for l in range(depth):
        last = l == depth - 1
        in_args = (l, mod, g_pre, w_in_l, pavg, g_norm)
        *proj, wa, wb, wc, wo = _in_proj(x, *in_args, rope_tabs, tm=TM_IN, ctx=False,
                                         cast=[(w, l) for w in (w_branch_a, w_branch_b, w_branch_c, w_out)])
        q, k, v, sg, f_lat, ug, vn, gates = proj
        mlp_cast = [(w_up, l), (w_down, l)]
        per_batch = lambda a: a.reshape(nb, CTX_LEN, RET_W)
        if last:
            kc, vc = _in_proj(ctx, *in_args, None, tm=nb * CTX_LEN, ctx=True, kv_only=True)
            ret, wup, wdn = _retention(l, logits, (q, k, v, sg), (per_batch(kc), per_batch(vc)), ctx_out=False,
                                       cast=mlp_cast)
        else:
            qc, kc, vc, sgc, f_ctx, ugc, vnc, gatesc = _in_proj(ctx, *in_args, None, tm=nb * CTX_LEN, ctx=True)
            ret, retc, wup, wdn = _retention(l, logits, (q, k, v, sg), tuple(map(per_batch, (qc, kc, vc, sgc))),
                                             ctx_out=True, cast=mlp_cast)
            retc = retc.reshape(1, nb * CTX_LEN, RET_W)
        mix_w = (l, mod, ws, bs_tab, wa, wb, wc, wo, *gains, wup, wdn)
        four = _fourier_latent(f_lat)
        if last:
            x = _mix_mlp(x, ret, four, ug, vn, gates, *mix_w, tm=TM_LAT, ctx=False)
        else:
            x, w_in_l = _mix_mlp(x, ret, four, ug, vn, gates, *mix_w, tm=TM_LAT, ctx=False, cast=[(w_in, l + 1)])
            ctx = _mix_mlp(ctx, retc, _fourier_ctx(f_ctx), ugc, vnc, gatesc, *mix_w, tm=nb * CTX_LEN, ctx=True)
    return x
```

```python
import functools

import numpy as np
import jax
import jax.numpy as jnp
from jax import lax
from jax.experimental import pallas as pl
from jax.experimental.pallas import tpu as pltpu

F32 = jnp.float32
BF16 = jnp.bfloat16

D_MODEL = 1024
SEQ = 8192
CTX_LEN = 256
GRID_W = 64
RET_W = 512
HEADS = 4
HEAD_DIM = 128
ROPE_BASE = 10000.0
ROPE_FREQS = HEAD_DIM // 4
FNET_W = 256
FNET_GROUP = 64
GMLP_W = 256
GMLP_GROUP = 64
GMLP_CHUNK = 128
D_FF = 4 * D_MODEL
EPS = 1e-6
IN_W = 4 * RET_W + FNET_W + 2 * GMLP_W + 3 * D_MODEL
COL_F = 4 * RET_W
COL_U = COL_F + FNET_W
COL_VS = COL_U + GMLP_W
COL_GATE = COL_VS + GMLP_W

TM_LAT = 512
TM_IN = 1024
RCHUNK = 256
RET_TILE = 2048
RET_STEPS = SEQ // RET_TILE
RET_CPT = RET_TILE // RCHUNK
RET_AHEAD = 4
N_RCHUNKS = 1 + SEQ // RCHUNK
FF_CHUNK = 1024
FFT_N1 = 64
FFT_N2 = 128
HALF = 128
FFT_STEP = 16
MIX_K2 = TM_LAT // FFT_N1

_VMEM_LIMIT = 56 * 1024 * 1024


def _dot(a, b):
    return jnp.dot(a, b, preferred_element_type=F32)


def _split(x):
    hi = x.astype(BF16)
    lo = (x - hi.astype(F32)).astype(BF16)
    return hi, lo


def _np_split(a64):
    hi = np.asarray(a64, np.float32).astype(BF16)
    lo = (np.asarray(a64, np.float32) - hi.astype(np.float32)).astype(BF16)
    return hi, lo


def _rms(x):
    return x * lax.rsqrt(jnp.mean(x * x, axis=-1, keepdims=True) + EPS)


def _gelu(x):
    return x * (0.5 * (1.0 + jnp.tanh(0.7978845608028654 * (x + 0.044715 * (x * x * x)))))


def _sigmoid(x):
    return 1.0 / (1.0 + jnp.exp(-x))


def _const_spec(shape, nargs=None):
    zeros = (0,) * len(shape)
    return pl.BlockSpec(shape, lambda *_: zeros, pipeline_mode=pl.Buffered(1))


def _layer_spec(shape, layer):
    idx = (layer,) + (0,) * len(shape)
    return pl.BlockSpec((None,) + tuple(shape), lambda *_: idx, pipeline_mode=pl.Buffered(1))


def _params(n_axes):
    return pltpu.CompilerParams(dimension_semantics=("arbitrary",) * n_axes, vmem_limit_bytes=_VMEM_LIMIT)


def _mod_kernel(a_ref, w_ref, b_ref, o_ref):
    a = a_ref[...]
    a = a * _sigmoid(a)
    ah, al = _split(a)
    w = w_ref[0].astype(BF16)
    o_ref[0] = _dot(ah, w) + _dot(al, w) + b_ref[0]


def _modulation(cond_rows, w_mod, b_mod):
    depth = w_mod.shape[0]
    tn = 1536
    return pl.pallas_call(
        _mod_kernel,
        grid=(depth, (6 * D_MODEL) // tn),
        in_specs=[
            pl.BlockSpec((8, D_MODEL), lambda l, j: (0, 0)),
            pl.BlockSpec((1, D_MODEL, tn), lambda l, j: (l, 0, j)),
            pl.BlockSpec((1, 1, tn), lambda l, j: (l, 0, j)),
        ],
        out_specs=pl.BlockSpec((1, 8, tn), lambda l, j: (l, 0, j)),
        out_shape=jax.ShapeDtypeStruct((depth, 8, 6 * D_MODEL), F32),
        compiler_params=_params(2),
        name="modulation",
    )(cond_rows, w_mod, b_mod.reshape(depth, 1, 6 * D_MODEL))


def _cast_plan(cast, grid):
    steps = int(np.prod(grid))
    in_specs, args, out_specs, out_shapes = [], [], [], []

    def step(*idx):
        t = idx[0]
        for n, i in zip(grid[1:], idx[1:]):
            t = t * n + i
        return t

    for stack, layer in cast:
        _, rows, cols = stack.shape
        n_chunks = rows // 16
        while n_chunks > steps:
            n_chunks //= 2
        assert rows % n_chunks == 0
        per = steps // n_chunks
        chunk = lambda *idx, per=per, last=n_chunks - 1: jnp.minimum(step(*idx) // per, last)
        in_specs.append(pl.BlockSpec((None, rows // n_chunks, cols),
                                     lambda *idx, layer=layer, chunk=chunk: (layer, chunk(*idx), 0)))
        out_specs.append(pl.BlockSpec((rows // n_chunks, cols), lambda *idx, chunk=chunk: (chunk(*idx), 0)))
        out_shapes.append(jax.ShapeDtypeStruct((rows, cols), BF16))
        args.append(stack)
    return in_specs, args, out_specs, out_shapes


def _cast_chunks(src_refs, dst_refs):
    for src, dst in zip(src_refs, dst_refs, strict=True):
        dst[...] = src[...].astype(BF16)


def _in_proj_kernel(*refs, rope, kv_only, n_cast):
    n_in = 8 if rope else 6
    ins, cast_src = refs[:n_in], refs[n_in:n_in + n_cast]
    outs, cast_dst = refs[n_in + n_cast:len(refs) - n_cast], refs[len(refs) - n_cast:]
    _cast_chunks(cast_src, cast_dst)
    if rope:
        x_ref, mod_ref, gpre_ref, w_ref, pavg_ref, gn_ref, rrow_ref, rcol_ref = ins
    else:
        x_ref, mod_ref, gpre_ref, w_ref, pavg_ref, gn_ref = ins
    tm = x_ref.shape[1]

    def prepare():
        shift = mod_ref[0, :, 0:D_MODEL]
        gain = gpre_ref[...] * (1.0 + mod_ref[0, :, D_MODEL:2 * D_MODEL])
        return (_rms(x_ref[0]) * gain + shift).astype(BF16)

    def project(hb):
        def proj(lo, hi):
            return _dot(hb, w_ref[:, lo:hi])

        if rope:
            n_rows = tm // GRID_W
            rot = []
            for t in range(2):
                by_row = jnp.concatenate(
                    [jnp.broadcast_to(rrow_ref[t, r:r + 1, :], (GRID_W, HEAD_DIM)) for r in range(n_rows)], axis=0)
                rot.append(by_row + jnp.concatenate([rcol_ref[t]] * n_rows, axis=0))

        def rope_store(z, dst):
            if not rope:
                dst[0] = z.astype(BF16)
                return
            cos, sin = rot
            for hd in range(HEADS):
                a = z[:, hd * HEAD_DIM:(hd + 1) * HEAD_DIM]
                r = a * cos + pltpu.roll(a, HEAD_DIM // 2, 1) * sin
                dst[0, :, hd * HEAD_DIM:(hd + 1) * HEAD_DIM] = r.astype(BF16)

        if kv_only:
            k_ref, v_ref = outs
        else:
            q_ref, k_ref, v_ref, sg_ref, f_ref, u_ref, vn_ref, gate_ref = outs
            rope_store(proj(0, RET_W), q_ref)
        rope_store(proj(RET_W, 2 * RET_W) * (HEAD_DIM ** -0.5), k_ref)
        v_ref[0] = proj(2 * RET_W, 3 * RET_W).astype(BF16)
        if kv_only:
            return
        g = proj(3 * RET_W, 4 * RET_W)
        sg_ref[0] = (g * _sigmoid(g)).astype(BF16)

        f = proj(COL_F, COL_U)
        for hf in range(2):
            fh = f[:, hf * HALF:(hf + 1) * HALF]
            if rope:
                for n1 in range(tm // FFT_N2):
                    for nj in range(FFT_N2 // FFT_STEP):
                        f_ref[0, hf, nj, n1 * FFT_STEP:(n1 + 1) * FFT_STEP, :] = (
                            fh[n1 * FFT_N2 + nj * FFT_STEP:n1 * FFT_N2 + (nj + 1) * FFT_STEP])
            else:
                f_ref[0, hf] = fh

        u_ref[0] = _gelu(proj(COL_U, COL_VS)).astype(BF16)
        vg = _gelu(proj(COL_VS, COL_GATE))
        sh, sl = _split(vg * vg)
        pavg = pavg_ref[...]
        ms = _dot(sh, pavg) + _dot(sl, pavg)
        vn_ref[0] = (vg * lax.rsqrt(ms + EPS) * gn_ref[...]).astype(BF16)

        for c in range(3):
            z = proj(COL_GATE + c * D_MODEL, COL_GATE + (c + 1) * D_MODEL)
            gate_ref[0, :, c * D_MODEL:(c + 1) * D_MODEL] = _sigmoid(z).astype(BF16)

    project(prepare())


def _mod_spec(layer, ctx):
    idx = (lambda b, i: (layer, 2, 0, 0)) if ctx else (lambda b, i: (layer, b, 0, 0))
    return pl.BlockSpec((None, 1, 1, 6 * D_MODEL), idx)


def _in_proj(xs, layer, mod, g_pre, w_in, pavg, g_norm, rope_tabs, *, tm, ctx, kv_only=False, cast=()):
    nb, rows, _ = xs.shape
    rope = rope_tabs is not None
    grid = (nb, rows // tm)
    tok_spec = lambda w: pl.BlockSpec((1, tm, w), lambda b, i: (b, i, 0))
    in_specs = [
        tok_spec(D_MODEL),
        _mod_spec(layer, ctx),
        _layer_spec((1, D_MODEL), layer),
        _const_spec((D_MODEL, IN_W)),
        _const_spec((GMLP_W, GMLP_W)),
        _layer_spec((1, GMLP_W), layer),
    ]
    args = [xs, mod, g_pre, w_in, pavg, g_norm]
    if rope:
        in_specs += [pl.BlockSpec((2, tm // GRID_W, HEAD_DIM), lambda b, i: (0, i, 0)),
                     _const_spec((2, GRID_W, HEAD_DIM))]
        args += list(rope_tabs)
    c_in_specs, c_args, c_out_specs, c_out_shapes = _cast_plan(cast, grid)
    in_specs += c_in_specs
    args += c_args
    bf = lambda w: jax.ShapeDtypeStruct((nb, rows, w), BF16)
    if kv_only:
        out_shapes = (bf(RET_W), bf(RET_W))
        out_specs = (tok_spec(RET_W), tok_spec(RET_W))
    else:
        if rope:
            n_nj = FFT_N2 // FFT_STEP
            f_shape = jax.ShapeDtypeStruct((nb, 2, n_nj, FFT_N1 * FFT_STEP, HALF), F32)
            f_spec = pl.BlockSpec((1, 2, n_nj, (tm // FFT_N2) * FFT_STEP, HALF), lambda b, i: (b, 0, 0, i, 0))
        else:
            f_shape = jax.ShapeDtypeStruct((nb, 2, rows, HALF), F32)
            f_spec = pl.BlockSpec((1, 2, tm, HALF), lambda b, i: (b, 0, i, 0))
        out_shapes = (
            bf(RET_W), bf(RET_W), bf(RET_W), bf(RET_W),
            f_shape,
            bf(GMLP_W), bf(GMLP_W),
            bf(3 * D_MODEL),
        )
        out_specs = (
            tok_spec(RET_W), tok_spec(RET_W), tok_spec(RET_W), tok_spec(RET_W),
            f_spec,
            tok_spec(GMLP_W), tok_spec(GMLP_W), tok_spec(3 * D_MODEL),
        )
    return pl.pallas_call(
        functools.partial(_in_proj_kernel, rope=rope, kv_only=kv_only, n_cast=len(cast)),
        grid=grid,
        in_specs=in_specs,
        out_specs=tuple(out_specs) + tuple(c_out_specs),
        out_shape=tuple(out_shapes) + tuple(c_out_shapes),
        compiler_params=_params(2),
        name="in_proj_ctx" if ctx else "in_proj",
    )(*args)


def _ret_kernel(*refs, ctx_out, n_cast):
    n_in, n_out = (9, 2) if ctx_out else (7, 1)
    ins, refs = refs[:n_in], refs[n_in:]
    cast_src, refs = refs[:n_cast], refs[n_cast:]
    outs, refs = refs[:n_out], refs[n_out:]
    cast_dst, scratch = refs[:n_cast], refs[n_cast:]
    _cast_chunks(cast_src, cast_dst)
    (logit_ref, ql_ref, kl_ref, vl_ref, sgl_ref, kc_ref, vc_ref) = ins[:7]
    if ctx_out:
        qc_ref, sgc_ref = ins[7:]
        ol_ref, oc_ref = outs
    else:
        (ol_ref,) = outs
    mask_ref, wf_ref, wb_ref, qf_ref, qb_ref, df_ref, db_ref, sf_ref, sb_ref, sball_ref = scratch
    b = pl.program_id(0)
    phase = pl.program_id(1)
    j = pl.program_id(2)

    @pl.when((b == 0) & (phase == 0) & (j == 0))
    def _():
        x = logit_ref[...]
        lg = -(jnp.maximum(-x, 0.0) + jnp.log(1.0 + jnp.exp(-jnp.abs(x))))
        row = lax.broadcasted_iota(jnp.int32, (RCHUNK, RCHUNK), 0).astype(F32)
        col = lax.broadcasted_iota(jnp.int32, (RCHUNK, RCHUNK), 1).astype(F32)
        diff = row - col
        pos = row[:, :HEAD_DIM]
        for hd in range(HEADS):
            lf = lg[hd:hd + 1, :]
            lb = lg[HEADS + hd:HEADS + hd + 1, :]
            mask_ref[hd] = jnp.where(diff >= 0.0, jnp.exp(lf * jnp.maximum(diff, 0.0)),
                                     jnp.exp(lb * jnp.maximum(-diff, 0.0)))
            wf_ref[hd] = jnp.exp(lf * (RCHUNK - 1.0 - pos))
            wb_ref[hd] = jnp.exp(lb * pos)
            qf_ref[hd] = jnp.exp(lf * (pos + 1.0))
            qb_ref[hd] = jnp.exp(lb * (RCHUNK - pos))
            df_ref[hd] = jnp.exp(jnp.broadcast_to(lf, (HEAD_DIM, HEAD_DIM)) * float(RCHUNK))
            db_ref[hd] = jnp.exp(jnp.broadcast_to(lb, (HEAD_DIM, HEAD_DIM)) * float(RCHUNK))

    def chunk_updates(hd, k_ref, v_ref, n_chunks, w_ref):
        cols = slice(hd * HEAD_DIM, (hd + 1) * HEAD_DIM)
        us = []
        for c in range(n_chunks):
            rows = slice(c * RCHUNK, (c + 1) * RCHUNK)
            kw = (k_ref[0, rows, cols].astype(F32) * w_ref[hd]).astype(BF16)
            us.append(lax.dot_general(kw, v_ref[0, rows, cols], (((0,), (0,)), ((), ())),
                                      preferred_element_type=F32))
        return us

    def backward_chunks(k_ref, v_ref, n_chunks, first_chunk):
        for hd in range(HEADS):
            us = chunk_updates(hd, k_ref, v_ref, n_chunks, wb_ref)
            s = sb_ref[hd]
            for c in reversed(range(n_chunks)):
                sball_ref[first_chunk + c, hd] = s.astype(BF16)
                s = db_ref[hd] * s + us[c]
            sb_ref[hd] = s

    def forward_chunks(q_ref, k_ref, v_ref, sg_ref, o_ref, n_chunks, first_chunk):
        starts = []
        for hd in range(HEADS):
            us = chunk_updates(hd, k_ref, v_ref, n_chunks, wf_ref)
            st = [sf_ref[hd]]
            for c in range(n_chunks):
                st.append(df_ref[hd] * st[-1] + us[c])
            sf_ref[hd] = st[-1]
            starts.append(st)
        if o_ref is None:
            return
        units = [(c, hd) for c in range(n_chunks) for hd in range(HEADS)]

        def scores(c, hd):
            rows = slice(c * RCHUNK, (c + 1) * RCHUNK)
            cols = slice(hd * HEAD_DIM, (hd + 1) * HEAD_DIM)
            return lax.dot_general(q_ref[0, rows, cols], k_ref[0, rows, cols], (((1,), (1,)), ((), ())),
                                   preferred_element_type=F32)

        ahead = min(RET_AHEAD, len(units))
        pending = [scores(*u) for u in units[:ahead]]
        for n, (c, hd) in enumerate(units):
            if n + ahead < len(units):
                pending.append(scores(*units[n + ahead]))
            sc = pending.pop(0)
            rows = slice(c * RCHUNK, (c + 1) * RCHUNK)
            cols = slice(hd * HEAD_DIM, (hd + 1) * HEAD_DIM)
            qf = q_ref[0, rows, cols].astype(F32)
            lhs = jnp.concatenate([(sc * mask_ref[hd]).astype(BF16),
                                   (qf * qf_ref[hd]).astype(BF16),
                                   (qf * qb_ref[hd]).astype(BF16)], axis=1)
            rhs = jnp.concatenate([v_ref[0, rows, cols], starts[hd][c].astype(BF16),
                                   sball_ref[first_chunk + c, hd]], axis=0)
            o = _dot(lhs, rhs)
            o_ref[0, rows, cols] = (_rms(o) * sg_ref[0, rows, cols].astype(F32)).astype(BF16)

    @pl.when(phase == 0)
    def _():
        @pl.when(j == 0)
        def _():
            sb_ref[...] = jnp.zeros_like(sb_ref)
            backward_chunks(kc_ref, vc_ref, 1, 0)

        backward_chunks(kl_ref, vl_ref, RET_CPT, 1 + (RET_STEPS - 1 - j) * RET_CPT)

    @pl.when(phase == 1)
    def _():
        @pl.when(j == 0)
        def _():
            sf_ref[...] = jnp.zeros_like(sf_ref)
            if ctx_out:
                forward_chunks(qc_ref, kc_ref, vc_ref, sgc_ref, oc_ref, 1, 0)
            else:
                forward_chunks(None, kc_ref, vc_ref, None, None, 1, 0)

        forward_chunks(ql_ref, kl_ref, vl_ref, sgl_ref, ol_ref, RET_CPT, 1 + j * RET_CPT)


def _retention(layer, logits, lat, ctx, *, ctx_out, cast=()):
    nb = lat[0].shape[0]
    kv_idx = lambda b, p, j: (b, jnp.where(p == 0, RET_STEPS - 1 - j, j), 0)
    q_idx = lambda b, p, j: (b, jnp.where(p == 0, 0, j), 0)
    c_idx = lambda b, p, j: (b, 0, 0)
    lat_blk = (1, RET_TILE, RET_W)
    ctx_blk = (1, CTX_LEN, RET_W)
    in_specs = [
        _layer_spec((2 * HEADS, 1), layer),
        pl.BlockSpec(lat_blk, q_idx), pl.BlockSpec(lat_blk, kv_idx), pl.BlockSpec(lat_blk, kv_idx),
        pl.BlockSpec(lat_blk, q_idx),
        pl.BlockSpec(ctx_blk, c_idx), pl.BlockSpec(ctx_blk, c_idx),
    ]
    if ctx_out:
        qc, kc, vc, sgc = ctx
        args = [logits, *lat, kc, vc, qc, sgc]
        in_specs += [pl.BlockSpec(ctx_blk, c_idx), pl.BlockSpec(ctx_blk, c_idx)]
        out_specs = (pl.BlockSpec(lat_blk, q_idx), pl.BlockSpec(ctx_blk, c_idx))
        out_shape = (jax.ShapeDtypeStruct((nb, SEQ, RET_W), BF16), jax.ShapeDtypeStruct((nb, CTX_LEN, RET_W), BF16))
    else:
        kc, vc = ctx
        args = [logits, *lat, kc, vc]
        out_specs = (pl.BlockSpec(lat_blk, q_idx),)
        out_shape = (jax.ShapeDtypeStruct((nb, SEQ, RET_W), BF16),)
    grid = (nb, 2, RET_STEPS)
    c_in_specs, c_args, c_out_specs, c_out_shapes = _cast_plan(cast, grid)
    state = pltpu.VMEM((HEADS, HEAD_DIM, HEAD_DIM), F32)
    pos_tab = pltpu.VMEM((HEADS, RCHUNK, HEAD_DIM), F32)
    return pl.pallas_call(
        functools.partial(_ret_kernel, ctx_out=ctx_out, n_cast=len(cast)),
        grid=grid,
        in_specs=in_specs + c_in_specs,
        out_specs=out_specs + tuple(c_out_specs),
        out_shape=out_shape + tuple(c_out_shapes),
        scratch_shapes=[
            pltpu.VMEM((HEADS, RCHUNK, RCHUNK), F32),
            pos_tab, pos_tab, pos_tab, pos_tab,
            state, state,
            state, state,
            pltpu.VMEM((N_RCHUNKS, HEADS, HEAD_DIM, HEAD_DIM), BF16),
        ],
        compiler_params=_params(3),
        name="retention",
    )(*args, *c_args)


def _dft_cos_sin(n):
    idx = np.arange(n, dtype=np.float64)
    ang = 2.0 * np.pi * ((idx[:, None] * idx[None, :]) % n) / n
    return np.cos(ang), np.sin(ang)


def _channel_dft(scale):
    c, s = _dft_cos_sin(FNET_GROUP)
    eye = np.eye(FNET_W // FNET_GROUP)
    return np.concatenate([np.kron(eye, c), np.kron(eye, s)], axis=0) * scale


def _fft_constants():
    k1 = np.arange(FFT_N1, dtype=np.float64)[None, :, None]
    n1 = np.arange(FFT_N1, dtype=np.float64)[None, None, :]
    n2 = np.arange(FFT_N2, dtype=np.float64)[:, None, None]
    ang = 2.0 * np.pi * ((k1 * (FFT_N2 * n1 + n2)) % SEQ) / SEQ
    m1 = np.concatenate([np.cos(ang), -np.sin(ang)], axis=1)
    c, s = _dft_cos_sin(FFT_N2)
    m3 = np.block([[c, s], [-s, c]])
    return (_np_split_cat(m1, -1), jnp.asarray(m3, F32),
            jnp.asarray(_channel_dft((SEQ * FNET_GROUP) ** -0.5), F32))


def _ctx_fft_constants():
    c, s = _dft_cos_sin(CTX_LEN)
    return (jnp.asarray(np.concatenate([c, -s], axis=0), F32),
            jnp.asarray(_channel_dft((CTX_LEN * FNET_GROUP) ** -0.5), F32))


def _np_split_cat(a64, axis):
    hi, lo = _np_split(a64)
    return jnp.asarray(np.concatenate([hi, lo], axis=axis))


def _twice(x, axis):
    return jnp.concatenate([x, x], axis=axis)


def _fft_kernel(*refs, n_cast):
    x_ref, m_ref, w_ref, cd_ref = refs[:4]
    o_ref, a_ref = refs[4 + n_cast], refs[-1]
    _cast_chunks(refs[4:4 + n_cast], refs[5 + n_cast:-1])
    _fft_body(x_ref, m_ref, w_ref, cd_ref, o_ref, a_ref)


def _fft_body(x_ref, m_ref, w_ref, cd_ref, o_ref, a_ref):
    n_kb = FFT_N1 // FFT_STEP

    def stage1(nj, carry):
        for jj in range(FFT_STEP):
            x = jnp.concatenate([x_ref[hf, nj, pl.ds(jj, FFT_N1, stride=FFT_STEP), :] for hf in range(2)],
                                axis=1).astype(BF16)
            a = _dot(m_ref[nj * FFT_STEP + jj], _twice(x, 0))
            row = pl.multiple_of((nj * FFT_STEP + jj) * FFT_STEP, FFT_STEP)
            for hf in range(2):
                for ri in range(2):
                    for kb in range(n_kb):
                        r0 = ri * FFT_N1 + kb * FFT_STEP
                        a_ref[hf, ri, kb, pl.ds(row, FFT_STEP), :] = a[r0:r0 + FFT_STEP, hf * HALF:(hf + 1) * HALF]
        return carry

    kb = pl.program_id(1)

    @pl.when(kb == 0)
    def _():
        lax.fori_loop(0, FFT_N2 // FFT_STEP, stage1, 0)

    w = w_ref[...].astype(BF16)
    zs = []
    for jj in range(FFT_STEP):
        rhs = jnp.concatenate(
            [jnp.concatenate([a_ref[hf, ri, kb, pl.ds(jj, FFT_N2, stride=FFT_STEP), :] for ri in range(2)], axis=0)
             for hf in range(2)], axis=1).astype(BF16)
        y = _dot(w, rhs)
        zs.append(jnp.concatenate([y[:FFT_N2], y[FFT_N2:]], axis=1).astype(BF16))
    z = jnp.concatenate(zs, axis=0)
    four = _dot(z, cd_ref[...].astype(BF16))
    for jj in range(FFT_STEP):
        for kt in range(FFT_N2 // MIX_K2):
            for hf in range(2):
                r0 = jj * FFT_N2 + kt * MIX_K2
                o_ref[hf, kt, jj * MIX_K2:(jj + 1) * MIX_K2, :] = four[r0:r0 + MIX_K2, hf * HALF:(hf + 1) * HALF]


def _fourier_latent(f_blk, cast=()):
    nb = f_blk.shape[0]
    m1, m3, cd = _fft_constants()
    n_nj, n_kb = FFT_N2 // FFT_STEP, FFT_N1 // FFT_STEP
    grid = (nb, n_kb)
    c_in_specs, c_args, c_out_specs, c_out_shapes = _cast_plan(cast, grid)
    return pl.pallas_call(
        functools.partial(_fft_kernel, n_cast=len(cast)),
        grid=grid,
        in_specs=[
            pl.BlockSpec((None, 2, n_nj, FFT_N1 * FFT_STEP, HALF), lambda b, j: (b, 0, 0, 0, 0)),
            _const_spec((FFT_N2, 2 * FFT_N1, 2 * FFT_N1)),
            _const_spec((2 * FFT_N2, 2 * FFT_N2)), _const_spec((2 * FNET_W, FNET_W)),
        ] + c_in_specs,
        out_specs=(pl.BlockSpec((None, 2, FFT_N2 // MIX_K2, FFT_STEP * MIX_K2, HALF), lambda b, j: (b, 0, 0, j, 0)),
                   *c_out_specs),
        out_shape=(jax.ShapeDtypeStruct((nb, 2, FFT_N2 // MIX_K2, FFT_N1 * MIX_K2, HALF), F32), *c_out_shapes),
        scratch_shapes=[pltpu.VMEM((2, 2, n_kb, FFT_N2 * FFT_STEP, HALF), F32)],
        compiler_params=_params(2),
        name="fourier",
    )(f_blk, m1, m3, cd, *c_args)


def _fft_ctx_kernel(x_ref, w_ref, cd_ref, o_ref):
    x = jnp.concatenate([x_ref[hf] for hf in range(2)], axis=1).astype(BF16)
    z = _dot(w_ref[...].astype(BF16), x)
    z = jnp.concatenate([z[:CTX_LEN], z[CTX_LEN:]], axis=1).astype(BF16)
    four = _dot(z, cd_ref[...].astype(BF16))
    for hf in range(2):
        o_ref[hf] = four[:, hf * HALF:(hf + 1) * HALF]


def _fourier_ctx(f_ctx):
    nb = f_ctx.shape[2] // CTX_LEN
    w, cd = _ctx_fft_constants()
    return pl.pallas_call(
        _fft_ctx_kernel,
        grid=(nb,),
        in_specs=[
            pl.BlockSpec((None, 2, CTX_LEN, HALF), lambda b: (0, 0, b, 0)),
            _const_spec((2 * CTX_LEN, CTX_LEN)), _const_spec((2 * FNET_W, FNET_W)),
        ],
        out_specs=pl.BlockSpec((None, 2, CTX_LEN, HALF), lambda b: (0, 0, b, 0)),
        out_shape=jax.ShapeDtypeStruct(f_ctx.shape, F32),
        compiler_params=_params(1),
        name="fourier_ctx",
    )(f_ctx, w, cd)


def _mix_mlp_kernel(*refs, tm, ctx, n_cast):
    (x_ref, ret_ref, four_ref, u_ref, vn_ref, gate_ref, mod_ref, ws_ref, bs_ref,
     wa_ref, wb_ref, wc_ref, wo_ref, gpm_ref, gprm_ref, gpom_ref, wup_ref, wdn_ref) = refs[:18]
    o_ref = refs[18 + n_cast]
    _cast_chunks(refs[18:18 + n_cast], refs[19 + n_cast:])
    group = lax.broadcasted_iota(jnp.int32, (GMLP_CHUNK, GMLP_W), 1) // GMLP_GROUP
    parts = []
    for cc in range(tm // GMLP_CHUNK):
        rows = slice(cc * GMLP_CHUNK, (cc + 1) * GMLP_CHUNK)
        vn = vn_ref[0, rows, :]
        s = bs_ref[...]
        for g in range(GMLP_W // GMLP_GROUP):
            s = s + jnp.where(group == g, _dot(ws_ref[g], vn), 0.0)
        parts.append((u_ref[0, rows, :].astype(F32) * s).astype(BF16))
    sgu = jnp.concatenate(parts, axis=0)

    if ctx:
        four = jnp.concatenate([four_ref[0, hf] for hf in range(2)], axis=1).astype(BF16)
    else:
        nk2 = tm // FFT_N1
        four = jnp.concatenate(
            [jnp.concatenate([four_ref[0, hf, pl.ds(k2, FFT_N1, stride=nk2), :] for k2 in range(nk2)], axis=0)
             for hf in range(2)], axis=1).astype(BF16)
    ga = gate_ref[0, :, 0:D_MODEL].astype(F32)
    gb = gate_ref[0, :, D_MODEL:2 * D_MODEL].astype(F32)
    gc = gate_ref[0, :, 2 * D_MODEL:3 * D_MODEL].astype(F32)
    merged = (ga * _dot(ret_ref[0], wa_ref[...]) + gb * _dot(four, wb_ref[...])
              + gc * _dot(sgu, wc_ref[...])).astype(BF16)

    gain1 = mod_ref[0, :, 2 * D_MODEL:3 * D_MODEL] * gpm_ref[...]
    shift2 = mod_ref[0, :, 3 * D_MODEL:4 * D_MODEL]
    gain2 = gprm_ref[...] * (1.0 + mod_ref[0, :, 4 * D_MODEL:5 * D_MODEL])
    gain3 = mod_ref[0, :, 5 * D_MODEL:6 * D_MODEL] * gpom_ref[...]

    halves = [slice(r * (tm // 2), (r + 1) * (tm // 2)) for r in range(2)]
    x1, h2 = [], []
    for rows in halves:
        y = _dot(merged[rows], wo_ref[...])
        x1.append(x_ref[0, rows, :] + _rms(y) * gain1)
        h2.append((_rms(x1[-1]) * gain2 + shift2).astype(BF16))
    up0 = jnp.concatenate([_dot(h, wup_ref[:, 0:FF_CHUNK]) for h in h2], axis=0)
    h2 = jnp.concatenate(h2, axis=0)

    n_ff = D_FF // FF_CHUNK
    m = None
    for c in range(n_ff - 1):
        cols = slice(c * FF_CHUNK, (c + 1) * FF_CHUNK)
        a = jnp.maximum(up0 if c == 0 else _dot(h2, wup_ref[:, cols]), 0.0)
        part = _dot((a * a).astype(BF16), wdn_ref[cols, :])
        m = part if m is None else m + part
    cols = slice((n_ff - 1) * FF_CHUNK, n_ff * FF_CHUNK)
    a = jnp.maximum(_dot(h2, wup_ref[:, cols]), 0.0)
    a = (a * a).astype(BF16)
    for r, rows in enumerate(halves):
        mr = m[rows] + _dot(a[rows], wdn_ref[cols, :])
        o_ref[0, rows, :] = x1[r] + _rms(mr) * gain3


def _mix_mlp(xs, ret, four, ug, vn, gates, layer, mod, ws, bs_tab, wa, wb, wc, wo,
             g_post_mix, g_pre_mlp, g_post_mlp, w_up, w_down, *, tm, ctx, cast=()):
    nb, rows, _ = xs.shape
    tok_spec = lambda w: pl.BlockSpec((1, tm, w), lambda b, i: (b, i, 0))
    lspec = lambda *shape: _layer_spec(shape, layer)
    if ctx:
        four_spec = pl.BlockSpec((1, 2, tm, HALF), lambda b, i: (b, 0, i, 0))
    else:
        four_spec = pl.BlockSpec((1, 2, None, tm, HALF), lambda b, i: (b, 0, i, 0, 0))
    c_in_specs, c_args, c_out_specs, c_out_shapes = _cast_plan(cast, (nb, rows // tm))
    outs = pl.pallas_call(
        functools.partial(_mix_mlp_kernel, tm=tm, ctx=ctx, n_cast=len(cast)),
        grid=(nb, rows // tm),
        in_specs=[
            tok_spec(D_MODEL), tok_spec(RET_W),
            four_spec,
            tok_spec(GMLP_W), tok_spec(GMLP_W), tok_spec(3 * D_MODEL),
            _mod_spec(layer, ctx),
            lspec(GMLP_W // GMLP_GROUP, GMLP_CHUNK, GMLP_CHUNK),
            lspec(GMLP_CHUNK, GMLP_W),
            _const_spec((RET_W, D_MODEL)), _const_spec((FNET_W, D_MODEL)), _const_spec((GMLP_W, D_MODEL)),
            _const_spec((D_MODEL, D_MODEL)),
            lspec(1, D_MODEL), lspec(1, D_MODEL), lspec(1, D_MODEL),
            _const_spec((D_MODEL, D_FF)), _const_spec((D_FF, D_MODEL)),
        ] + c_in_specs,
        out_specs=(tok_spec(D_MODEL),) + tuple(c_out_specs),
        out_shape=(jax.ShapeDtypeStruct((nb, rows, D_MODEL), F32),) + tuple(c_out_shapes),
        compiler_params=_params(2),
        name="mix_mlp_ctx" if ctx else "mix_mlp",
    )(xs, ret, four, ug, vn, gates, mod, ws, bs_tab, wa, wb, wc, wo,
      g_post_mix, g_pre_mlp, g_post_mlp, w_up, w_down, *c_args)
    return outs if cast else outs[0]


def _rope_tables():
    rows = SEQ // GRID_W
    freqs = ROPE_BASE ** (-jnp.arange(ROPE_FREQS, dtype=F32) / ROPE_FREQS)
    ang_r = jnp.arange(rows, dtype=F32)[:, None] * freqs
    ang_c = jnp.arange(GRID_W, dtype=F32)[:, None] * freqs

    def lanes(a, first):
        z = jnp.zeros_like(a)
        return jnp.concatenate([a, z] if first else [z, a], axis=-1)

    def pair(ang, first):
        cos, sin = lanes(jnp.cos(ang), first), lanes(jnp.sin(ang), first)
        return jnp.stack([jnp.concatenate([cos, cos], axis=-1), jnp.concatenate([-sin, sin], axis=-1)])

    return pair(ang_r, True), pair(ang_c, False)


def kernel(x, c, ctx, c_ctx, w_mod, b_mod, g_pre_mix, g_post_mix, g_pre_mlp, g_post_mlp, w_in,
           ret_decay_logit, sgu_w_s, sgu_b_s, sgu_norm, w_branch_a, w_branch_b, w_branch_c, w_out,
           w_up, w_down):
    nb = x.shape[0]
    depth = w_mod.shape[0]
    assert x.shape == (nb, SEQ, D_MODEL) and ctx.shape == (nb, CTX_LEN, D_MODEL) and nb == 2

    cond_rows = jnp.concatenate([c, c_ctx[None, :], jnp.zeros((8 - nb - 1, D_MODEL), F32)], axis=0)
    mod = _modulation(cond_rows, w_mod, b_mod).reshape(depth, 8, 1, 6 * D_MODEL)
    rope_tabs = _rope_tables()
    pavg = jnp.asarray(np.kron(np.eye(GMLP_W // GMLP_GROUP), np.full((GMLP_GROUP, GMLP_GROUP), 1.0 / GMLP_GROUP)), BF16)

    vec = lambda p: p[:, None, :]
    g_pre, g_norm = vec(g_pre_mix), vec(sgu_norm)
    logits = ret_decay_logit.reshape(depth, 2 * HEADS, 1)
    ws = sgu_w_s.astype(BF16)
    bs_tab = jnp.repeat(jnp.swapaxes(sgu_b_s, 1, 2), GMLP_GROUP, axis=2)
    gains = (vec(g_post_mix), vec(g_pre_mlp), vec(g_post_mlp))
    w_in_l = w_in[0].astype(BF16)
    ctx = ctx.reshape(1, nb * CTX_LEN, D_MODEL)
    for l in range(depth):
        last = l == depth - 1
        in_args = (l, mod, g_pre, w_in_l, pavg, g_norm)
        *proj, wa, wb, wc, wo, wup = _in_proj(
            x, *in_args, rope_tabs, tm=TM_IN, ctx=False,
            cast=[(w, l) for w in (w_branch_a, w_branch_b, w_branch_c, w_out, w_up)])
        q, k, v, sg, f_lat, ug, vn, gates = proj
        per_batch = lambda a: a.reshape(nb, CTX_LEN, RET_W)
        if last:
            kc, vc = _in_proj(ctx, *in_args, None, tm=nb * CTX_LEN, ctx=True, kv_only=True)
            (ret,) = _retention(l, logits, (q, k, v, sg), (per_batch(kc), per_batch(vc)), ctx_out=False)
        else:
            qc, kc, vc, sgc, f_ctx, ugc, vnc, gatesc = _in_proj(ctx, *in_args, None, tm=nb * CTX_LEN, ctx=True)
            ret, retc = _retention(l, logits, (q, k, v, sg), tuple(map(per_batch, (qc, kc, vc, sgc))), ctx_out=True)
            retc = retc.reshape(1, nb * CTX_LEN, RET_W)
        four, wdn = _fourier_latent(f_lat, cast=[(w_down, l)])
        mix_w = (l, mod, ws, bs_tab, wa, wb, wc, wo, *gains, wup, wdn)
        if last:
            x = _mix_mlp(x, ret, four, ug, vn, gates, *mix_w, tm=TM_LAT, ctx=False)
        else:
            x, w_in_l = _mix_mlp(x, ret, four, ug, vn, gates, *mix_w, tm=TM_LAT, ctx=False, cast=[(w_in, l + 1)])
            ctx = _mix_mlp(ctx, retc, _fourier_ctx(f_ctx), ugc, vnc, gatesc, *mix_w, tm=nb * CTX_LEN, ctx=True)
    return x
```

```python
import functools

import numpy as np
import jax
import jax.numpy as jnp
from jax import lax
from jax.experimental import pallas as pl
from jax.experimental.pallas import tpu as pltpu

F32 = jnp.float32
BF16 = jnp.bfloat16

D_MODEL = 1024
SEQ = 8192
CTX_LEN = 256
GRID_W = 64
RET_W = 512
HEADS = 4
HEAD_DIM = 128
ROPE_BASE = 10000.0
ROPE_FREQS = HEAD_DIM // 4
FNET_W = 256
FNET_GROUP = 64
GMLP_W = 256
GMLP_GROUP = 64
GMLP_CHUNK = 128
D_FF = 4 * D_MODEL
EPS = 1e-6
IN_W = 4 * RET_W + FNET_W + 2 * GMLP_W + 3 * D_MODEL
COL_F = 4 * RET_W
COL_U = COL_F + FNET_W
COL_VS = COL_U + GMLP_W
COL_GATE = COL_VS + GMLP_W

TM_LAT = 512
TM_IN = 1024
RCHUNK = 256
RET_TILE = 2048
RET_STEPS = SEQ // RET_TILE
RET_CPT = RET_TILE // RCHUNK
RET_AHEAD = 4
N_RCHUNKS = 1 + SEQ // RCHUNK
FF_CHUNK = 1024
FFT_N1 = 64
FFT_N2 = 128
HALF = 128
FFT_STEP = 8
MIX_K2 = TM_LAT // FFT_N1

_VMEM_LIMIT = 56 * 1024 * 1024


def _dot(a, b):
    return jnp.dot(a, b, preferred_element_type=F32)


def _split(x):
    hi = x.astype(BF16)
    lo = (x - hi.astype(F32)).astype(BF16)
    return hi, lo


def _np_split(a64):
    hi = np.asarray(a64, np.float32).astype(BF16)
    lo = (np.asarray(a64, np.float32) - hi.astype(np.float32)).astype(BF16)
    return hi, lo


def _rms(x):
    return x * lax.rsqrt(jnp.mean(x * x, axis=-1, keepdims=True) + EPS)


def _gelu(x):
    return x * (0.5 * (1.0 + jnp.tanh(0.7978845608028654 * (x + 0.044715 * (x * x * x)))))


def _sigmoid(x):
    return 1.0 / (1.0 + jnp.exp(-x))


def _const_spec(shape, nargs=None):
    zeros = (0,) * len(shape)
    return pl.BlockSpec(shape, lambda *_: zeros, pipeline_mode=pl.Buffered(1))


def _layer_spec(shape, layer):
    idx = (layer,) + (0,) * len(shape)
    return pl.BlockSpec((None,) + tuple(shape), lambda *_: idx, pipeline_mode=pl.Buffered(1))


def _params(n_axes):
    return pltpu.CompilerParams(dimension_semantics=("arbitrary",) * n_axes, vmem_limit_bytes=_VMEM_LIMIT)


def _mod_kernel(a_ref, w_ref, b_ref, o_ref):
    a = a_ref[...]
    a = a * _sigmoid(a)
    ah, al = _split(a)
    w = w_ref[0].astype(BF16)
    o_ref[0] = _dot(ah, w) + _dot(al, w) + b_ref[0]


def _modulation(cond_rows, w_mod, b_mod):
    depth = w_mod.shape[0]
    tn = 1536
    return pl.pallas_call(
        _mod_kernel,
        grid=(depth, (6 * D_MODEL) // tn),
        in_specs=[
            pl.BlockSpec((8, D_MODEL), lambda l, j: (0, 0)),
            pl.BlockSpec((1, D_MODEL, tn), lambda l, j: (l, 0, j)),
            pl.BlockSpec((1, 1, tn), lambda l, j: (l, 0, j)),
        ],
        out_specs=pl.BlockSpec((1, 8, tn), lambda l, j: (l, 0, j)),
        out_shape=jax.ShapeDtypeStruct((depth, 8, 6 * D_MODEL), F32),
        compiler_params=_params(2),
        name="modulation",
    )(cond_rows, w_mod, b_mod.reshape(depth, 1, 6 * D_MODEL))


def _cast_plan(cast, grid):
    steps = int(np.prod(grid))
    in_specs, args, out_specs, out_shapes = [], [], [], []

    def step(*idx):
        t = idx[0]
        for n, i in zip(grid[1:], idx[1:]):
            t = t * n + i
        return t

    for stack, layer in cast:
        _, rows, cols = stack.shape
        n_chunks = rows // 16
        while n_chunks > steps:
            n_chunks //= 2
        assert rows % n_chunks == 0
        per = steps // n_chunks
        chunk = lambda *idx, per=per, last=n_chunks - 1: jnp.minimum(step(*idx) // per, last)
        in_specs.append(pl.BlockSpec((None, rows // n_chunks, cols),
                                     lambda *idx, layer=layer, chunk=chunk: (layer, chunk(*idx), 0)))
        out_specs.append(pl.BlockSpec((rows // n_chunks, cols), lambda *idx, chunk=chunk: (chunk(*idx), 0)))
        out_shapes.append(jax.ShapeDtypeStruct((rows, cols), BF16))
        args.append(stack)
    return in_specs, args, out_specs, out_shapes


def _cast_chunks(src_refs, dst_refs):
    for src, dst in zip(src_refs, dst_refs, strict=True):
        dst[...] = src[...].astype(BF16)


def _in_proj_kernel(*refs, rope, kv_only, n_cast):
    n_in = 8 if rope else 6
    ins, cast_src = refs[:n_in], refs[n_in:n_in + n_cast]
    outs, cast_dst = refs[n_in + n_cast:len(refs) - n_cast], refs[len(refs) - n_cast:]
    _cast_chunks(cast_src, cast_dst)
    if rope:
        x_ref, mod_ref, gpre_ref, w_ref, pavg_ref, gn_ref, rrow_ref, rcol_ref = ins
    else:
        x_ref, mod_ref, gpre_ref, w_ref, pavg_ref, gn_ref = ins
    tm = x_ref.shape[1]

    def prepare():
        shift = mod_ref[0, :, 0:D_MODEL]
        gain = gpre_ref[...] * (1.0 + mod_ref[0, :, D_MODEL:2 * D_MODEL])
        return (_rms(x_ref[0]) * gain + shift).astype(BF16)

    def project(hb):
        def proj(lo, hi):
            return _dot(hb, w_ref[:, lo:hi])

        if rope:
            n_rows = tm // GRID_W
            rot = []
            for t in range(2):
                by_row = jnp.concatenate(
                    [jnp.broadcast_to(rrow_ref[t, r:r + 1, :], (GRID_W, HEAD_DIM)) for r in range(n_rows)], axis=0)
                rot.append(by_row + jnp.concatenate([rcol_ref[t]] * n_rows, axis=0))

        def rope_store(z, dst):
            if not rope:
                dst[0] = z.astype(BF16)
                return
            cos, sin = rot
            for hd in range(HEADS):
                a = z[:, hd * HEAD_DIM:(hd + 1) * HEAD_DIM]
                r = a * cos + pltpu.roll(a, HEAD_DIM // 2, 1) * sin
                dst[0, :, hd * HEAD_DIM:(hd + 1) * HEAD_DIM] = r.astype(BF16)

        if kv_only:
            k_ref, v_ref = outs
        else:
            q_ref, k_ref, v_ref, sg_ref, f_ref, u_ref, vn_ref, gate_ref = outs
            rope_store(proj(0, RET_W), q_ref)
        rope_store(proj(RET_W, 2 * RET_W) * (HEAD_DIM ** -0.5), k_ref)
        v_ref[0] = proj(2 * RET_W, 3 * RET_W).astype(BF16)
        if kv_only:
            return
        g = proj(3 * RET_W, 4 * RET_W)
        sg_ref[0] = (g * _sigmoid(g)).astype(BF16)

        f = proj(COL_F, COL_U)
        for hf in range(2):
            fh = f[:, hf * HALF:(hf + 1) * HALF]
            if rope:
                for n1 in range(tm // FFT_N2):
                    for nj in range(FFT_N2 // FFT_STEP):
                        f_ref[0, hf, nj, n1 * FFT_STEP:(n1 + 1) * FFT_STEP, :] = (
                            fh[n1 * FFT_N2 + nj * FFT_STEP:n1 * FFT_N2 + (nj + 1) * FFT_STEP])
            else:
                f_ref[0, hf] = fh

        u_ref[0] = _gelu(proj(COL_U, COL_VS)).astype(BF16)
        vg = _gelu(proj(COL_VS, COL_GATE))
        sh, sl = _split(vg * vg)
        pavg = pavg_ref[...]
        ms = _dot(sh, pavg) + _dot(sl, pavg)
        vn_ref[0] = (vg * lax.rsqrt(ms + EPS) * gn_ref[...]).astype(BF16)

        for c in range(3):
            z = proj(COL_GATE + c * D_MODEL, COL_GATE + (c + 1) * D_MODEL)
            gate_ref[0, :, c * D_MODEL:(c + 1) * D_MODEL] = _sigmoid(z).astype(BF16)

    project(prepare())


def _mod_spec(layer, ctx):
    idx = (lambda b, i: (layer, 2, 0, 0)) if ctx else (lambda b, i: (layer, b, 0, 0))
    return pl.BlockSpec((None, 1, 1, 6 * D_MODEL), idx)


def _in_proj(xs, layer, mod, g_pre, w_in, pavg, g_norm, rope_tabs, *, tm, ctx, kv_only=False, cast=()):
    nb, rows, _ = xs.shape
    rope = rope_tabs is not None
    grid = (nb, rows // tm)
    tok_spec = lambda w: pl.BlockSpec((1, tm, w), lambda b, i: (b, i, 0))
    in_specs = [
        tok_spec(D_MODEL),
        _mod_spec(layer, ctx),
        _layer_spec((1, D_MODEL), layer),
        _const_spec((D_MODEL, IN_W)),
        _const_spec((GMLP_W, GMLP_W)),
        _layer_spec((1, GMLP_W), layer),
    ]
    args = [xs, mod, g_pre, w_in, pavg, g_norm]
    if rope:
        in_specs += [pl.BlockSpec((2, tm // GRID_W, HEAD_DIM), lambda b, i: (0, i, 0)),
                     _const_spec((2, GRID_W, HEAD_DIM))]
        args += list(rope_tabs)
    c_in_specs, c_args, c_out_specs, c_out_shapes = _cast_plan(cast, grid)
    in_specs += c_in_specs
    args += c_args
    bf = lambda w: jax.ShapeDtypeStruct((nb, rows, w), BF16)
    if kv_only:
        out_shapes = (bf(RET_W), bf(RET_W))
        out_specs = (tok_spec(RET_W), tok_spec(RET_W))
    else:
        if rope:
            n_nj = FFT_N2 // FFT_STEP
            f_shape = jax.ShapeDtypeStruct((nb, 2, n_nj, FFT_N1 * FFT_STEP, HALF), F32)
            f_spec = pl.BlockSpec((1, 2, n_nj, (tm // FFT_N2) * FFT_STEP, HALF), lambda b, i: (b, 0, 0, i, 0))
        else:
            f_shape = jax.ShapeDtypeStruct((nb, 2, rows, HALF), F32)
            f_spec = pl.BlockSpec((1, 2, tm, HALF), lambda b, i: (b, 0, i, 0))
        out_shapes = (
            bf(RET_W), bf(RET_W), bf(RET_W), bf(RET_W),
            f_shape,
            bf(GMLP_W), bf(GMLP_W),
            bf(3 * D_MODEL),
        )
        out_specs = (
            tok_spec(RET_W), tok_spec(RET_W), tok_spec(RET_W), tok_spec(RET_W),
            f_spec,
            tok_spec(GMLP_W), tok_spec(GMLP_W), tok_spec(3 * D_MODEL),
        )
    return pl.pallas_call(
        functools.partial(_in_proj_kernel, rope=rope, kv_only=kv_only, n_cast=len(cast)),
        grid=grid,
        in_specs=in_specs,
        out_specs=tuple(out_specs) + tuple(c_out_specs),
        out_shape=tuple(out_shapes) + tuple(c_out_shapes),
        compiler_params=_params(2),
        name="in_proj_ctx" if ctx else "in_proj",
    )(*args)


def _ret_kernel(*refs, ctx_out, n_cast):
    n_in, n_out = (9, 2) if ctx_out else (7, 1)
    ins, refs = refs[:n_in], refs[n_in:]
    cast_src, refs = refs[:n_cast], refs[n_cast:]
    outs, refs = refs[:n_out], refs[n_out:]
    cast_dst, scratch = refs[:n_cast], refs[n_cast:]
    _cast_chunks(cast_src, cast_dst)
    (logit_ref, ql_ref, kl_ref, vl_ref, sgl_ref, kc_ref, vc_ref) = ins[:7]
    if ctx_out:
        qc_ref, sgc_ref = ins[7:]
        ol_ref, oc_ref = outs
    else:
        (ol_ref,) = outs
    mask_ref, wf_ref, wb_ref, qf_ref, qb_ref, df_ref, db_ref, sf_ref, sb_ref, sball_ref = scratch
    b = pl.program_id(0)
    phase = pl.program_id(1)
    j = pl.program_id(2)

    @pl.when((b == 0) & (phase == 0) & (j == 0))
    def _():
        x = logit_ref[...]
        lg = -(jnp.maximum(-x, 0.0) + jnp.log(1.0 + jnp.exp(-jnp.abs(x))))
        row = lax.broadcasted_iota(jnp.int32, (RCHUNK, RCHUNK), 0).astype(F32)
        col = lax.broadcasted_iota(jnp.int32, (RCHUNK, RCHUNK), 1).astype(F32)
        diff = row - col
        pos = row[:, :HEAD_DIM]
        for hd in range(HEADS):
            lf = lg[hd:hd + 1, :]
            lb = lg[HEADS + hd:HEADS + hd + 1, :]
            mask_ref[hd] = jnp.where(diff >= 0.0, jnp.exp(lf * jnp.maximum(diff, 0.0)),
                                     jnp.exp(lb * jnp.maximum(-diff, 0.0)))
            wf_ref[hd] = jnp.exp(lf * (RCHUNK - 1.0 - pos))
            wb_ref[hd] = jnp.exp(lb * pos)
            qf_ref[hd] = jnp.exp(lf * (pos + 1.0))
            qb_ref[hd] = jnp.exp(lb * (RCHUNK - pos))
            df_ref[hd] = jnp.exp(jnp.broadcast_to(lf, (HEAD_DIM, HEAD_DIM)) * float(RCHUNK))
            db_ref[hd] = jnp.exp(jnp.broadcast_to(lb, (HEAD_DIM, HEAD_DIM)) * float(RCHUNK))

    def chunk_updates(hd, k_ref, v_ref, n_chunks, w_ref):
        cols = slice(hd * HEAD_DIM, (hd + 1) * HEAD_DIM)
        us = []
        for c in range(n_chunks):
            rows = slice(c * RCHUNK, (c + 1) * RCHUNK)
            kw = (k_ref[0, rows, cols].astype(F32) * w_ref[hd]).astype(BF16)
            us.append(lax.dot_general(kw, v_ref[0, rows, cols], (((0,), (0,)), ((), ())),
                                      preferred_element_type=F32))
        return us

    def backward_chunks(k_ref, v_ref, n_chunks, first_chunk):
        for hd in range(HEADS):
            us = chunk_updates(hd, k_ref, v_ref, n_chunks, wb_ref)
            s = sb_ref[hd]
            for c in reversed(range(n_chunks)):
                sball_ref[first_chunk + c, hd] = s.astype(BF16)
                s = db_ref[hd] * s + us[c]
            sb_ref[hd] = s

    def forward_chunks(q_ref, k_ref, v_ref, sg_ref, o_ref, n_chunks, first_chunk):
        starts = []
        for hd in range(HEADS):
            us = chunk_updates(hd, k_ref, v_ref, n_chunks, wf_ref)
            st = [sf_ref[hd]]
            for c in range(n_chunks):
                st.append(df_ref[hd] * st[-1] + us[c])
            sf_ref[hd] = st[-1]
            starts.append(st)
        if o_ref is None:
            return
        units = [(c, hd) for c in range(n_chunks) for hd in range(HEADS)]

        def scores(c, hd):
            rows = slice(c * RCHUNK, (c + 1) * RCHUNK)
            cols = slice(hd * HEAD_DIM, (hd + 1) * HEAD_DIM)
            return lax.dot_general(q_ref[0, rows, cols], k_ref[0, rows, cols], (((1,), (1,)), ((), ())),
                                   preferred_element_type=F32)

        ahead = min(RET_AHEAD, len(units))
        pending = [scores(*u) for u in units[:ahead]]
        for n, (c, hd) in enumerate(units):
            if n + ahead < len(units):
                pending.append(scores(*units[n + ahead]))
            sc = pending.pop(0)
            rows = slice(c * RCHUNK, (c + 1) * RCHUNK)
            cols = slice(hd * HEAD_DIM, (hd + 1) * HEAD_DIM)
            qf = q_ref[0, rows, cols].astype(F32)
            lhs = jnp.concatenate([(sc * mask_ref[hd]).astype(BF16),
                                   (qf * qf_ref[hd]).astype(BF16),
                                   (qf * qb_ref[hd]).astype(BF16)], axis=1)
            rhs = jnp.concatenate([v_ref[0, rows, cols], starts[hd][c].astype(BF16),
                                   sball_ref[first_chunk + c, hd]], axis=0)
            o = _dot(lhs, rhs)
            o_ref[0, rows, cols] = (_rms(o) * sg_ref[0, rows, cols].astype(F32)).astype(BF16)

    @pl.when(phase == 0)
    def _():
        @pl.when(j == 0)
        def _():
            sb_ref[...] = jnp.zeros_like(sb_ref)
            backward_chunks(kc_ref, vc_ref, 1, 0)

        backward_chunks(kl_ref, vl_ref, RET_CPT, 1 + (RET_STEPS - 1 - j) * RET_CPT)

    @pl.when(phase == 1)
    def _():
        @pl.when(j == 0)
        def _():
            sf_ref[...] = jnp.zeros_like(sf_ref)
            if ctx_out:
                forward_chunks(qc_ref, kc_ref, vc_ref, sgc_ref, oc_ref, 1, 0)
            else:
                forward_chunks(None, kc_ref, vc_ref, None, None, 1, 0)

        forward_chunks(ql_ref, kl_ref, vl_ref, sgl_ref, ol_ref, RET_CPT, 1 + j * RET_CPT)


def _retention(layer, logits, lat, ctx, *, ctx_out, cast=()):
    nb = lat[0].shape[0]
    kv_idx = lambda b, p, j: (b, jnp.where(p == 0, RET_STEPS - 1 - j, j), 0)
    q_idx = lambda b, p, j: (b, jnp.where(p == 0, 0, j), 0)
    c_idx = lambda b, p, j: (b, 0, 0)
    lat_blk = (1, RET_TILE, RET_W)
    ctx_blk = (1, CTX_LEN, RET_W)
    in_specs = [
        _layer_spec((2 * HEADS, 1), layer),
        pl.BlockSpec(lat_blk, q_idx), pl.BlockSpec(lat_blk, kv_idx), pl.BlockSpec(lat_blk, kv_idx),
        pl.BlockSpec(lat_blk, q_idx),
        pl.BlockSpec(ctx_blk, c_idx), pl.BlockSpec(ctx_blk, c_idx),
    ]
    if ctx_out:
        qc, kc, vc, sgc = ctx
        args = [logits, *lat, kc, vc, qc, sgc]
        in_specs += [pl.BlockSpec(ctx_blk, c_idx), pl.BlockSpec(ctx_blk, c_idx)]
        out_specs = (pl.BlockSpec(lat_blk, q_idx), pl.BlockSpec(ctx_blk, c_idx))
        out_shape = (jax.ShapeDtypeStruct((nb, SEQ, RET_W), BF16), jax.ShapeDtypeStruct((nb, CTX_LEN, RET_W), BF16))
    else:
        kc, vc = ctx
        args = [logits, *lat, kc, vc]
        out_specs = (pl.BlockSpec(lat_blk, q_idx),)
        out_shape = (jax.ShapeDtypeStruct((nb, SEQ, RET_W), BF16),)
    grid = (nb, 2, RET_STEPS)
    c_in_specs, c_args, c_out_specs, c_out_shapes = _cast_plan(cast, grid)
    state = pltpu.VMEM((HEADS, HEAD_DIM, HEAD_DIM), F32)
    pos_tab = pltpu.VMEM((HEADS, RCHUNK, HEAD_DIM), F32)
    return pl.pallas_call(
        functools.partial(_ret_kernel, ctx_out=ctx_out, n_cast=len(cast)),
        grid=grid,
        in_specs=in_specs + c_in_specs,
        out_specs=out_specs + tuple(c_out_specs),
        out_shape=out_shape + tuple(c_out_shapes),
        scratch_shapes=[
            pltpu.VMEM((HEADS, RCHUNK, RCHUNK), F32),
            pos_tab, pos_tab, pos_tab, pos_tab,
            state, state,
            state, state,
            pltpu.VMEM((N_RCHUNKS, HEADS, HEAD_DIM, HEAD_DIM), BF16),
        ],
        compiler_params=_params(3),
        name="retention",
    )(*args, *c_args)


def _dft_cos_sin(n):
    idx = np.arange(n, dtype=np.float64)
    ang = 2.0 * np.pi * ((idx[:, None] * idx[None, :]) % n) / n
    return np.cos(ang), np.sin(ang)


def _channel_dft(scale):
    c, s = _dft_cos_sin(FNET_GROUP)
    eye = np.eye(FNET_W // FNET_GROUP)
    return np.concatenate([np.kron(eye, c), np.kron(eye, s)], axis=0) * scale


def _fft_constants():
    k1 = np.arange(FFT_N1, dtype=np.float64)[None, :, None]
    n1 = np.arange(FFT_N1, dtype=np.float64)[None, None, :]
    n2 = np.arange(FFT_N2, dtype=np.float64)[:, None, None]
    ang = 2.0 * np.pi * ((k1 * (FFT_N2 * n1 + n2)) % SEQ) / SEQ
    m1 = np.concatenate([np.cos(ang), -np.sin(ang)], axis=1)
    c, s = _dft_cos_sin(FFT_N2)
    m3 = np.block([[c, s], [-s, c]])
    return (_np_split_cat(m1, -1), jnp.asarray(m3, F32),
            jnp.asarray(_channel_dft((SEQ * FNET_GROUP) ** -0.5), F32))


def _ctx_fft_constants():
    c, s = _dft_cos_sin(CTX_LEN)
    return (jnp.asarray(np.concatenate([c, -s], axis=0), F32),
            jnp.asarray(_channel_dft((CTX_LEN * FNET_GROUP) ** -0.5), F32))


def _np_split_cat(a64, axis):
    hi, lo = _np_split(a64)
    return jnp.asarray(np.concatenate([hi, lo], axis=axis))


def _twice(x, axis):
    return jnp.concatenate([x, x], axis=axis)


def _fft_kernel(*refs, n_cast):
    x_ref, m_ref, w_ref, cd_ref = refs[:4]
    o_ref, a_ref = refs[4 + n_cast], refs[-1]
    _cast_chunks(refs[4:4 + n_cast], refs[5 + n_cast:-1])
    _fft_body(x_ref, m_ref, w_ref, cd_ref, o_ref, a_ref)


def _fft_body(x_ref, m_ref, w_ref, cd_ref, o_ref, a_ref):
    n_kb = FFT_N1 // FFT_STEP

    def stage1(nj, carry):
        for jj in range(FFT_STEP):
            x = jnp.concatenate([x_ref[hf, nj, pl.ds(jj, FFT_N1, stride=FFT_STEP), :] for hf in range(2)],
                                axis=1).astype(BF16)
            a = _dot(m_ref[nj * FFT_STEP + jj], _twice(x, 0))
            row = pl.multiple_of((nj * FFT_STEP + jj) * FFT_STEP, FFT_STEP)
            for hf in range(2):
                for ri in range(2):
                    for kb in range(n_kb):
                        r0 = ri * FFT_N1 + kb * FFT_STEP
                        a_ref[hf, ri, kb, pl.ds(row, FFT_STEP), :] = a[r0:r0 + FFT_STEP, hf * HALF:(hf + 1) * HALF]
        return carry

    kb = pl.program_id(1)

    @pl.when(kb == 0)
    def _():
        lax.fori_loop(0, FFT_N2 // FFT_STEP, stage1, 0)

    w = w_ref[...].astype(BF16)
    zs = []
    for jj in range(FFT_STEP):
        rhs = jnp.concatenate(
            [jnp.concatenate([a_ref[hf, ri, kb, pl.ds(jj, FFT_N2, stride=FFT_STEP), :] for ri in range(2)], axis=0)
             for hf in range(2)], axis=1).astype(BF16)
        y = _dot(w, rhs)
        zs.append(jnp.concatenate([y[:FFT_N2], y[FFT_N2:]], axis=1).astype(BF16))
    z = jnp.concatenate(zs, axis=0)
    four = _dot(z, cd_ref[...].astype(BF16))
    for jj in range(FFT_STEP):
        for kt in range(FFT_N2 // MIX_K2):
            for hf in range(2):
                r0 = jj * FFT_N2 + kt * MIX_K2
                o_ref[hf, kt, jj * MIX_K2:(jj + 1) * MIX_K2, :] = four[r0:r0 + MIX_K2, hf * HALF:(hf + 1) * HALF]


def _fourier_latent(f_blk, cast=()):
    nb = f_blk.shape[0]
    m1, m3, cd = _fft_constants()
    n_nj, n_kb = FFT_N2 // FFT_STEP, FFT_N1 // FFT_STEP
    grid = (nb, n_kb)
    c_in_specs, c_args, c_out_specs, c_out_shapes = _cast_plan(cast, grid)
    return pl.pallas_call(
        functools.partial(_fft_kernel, n_cast=len(cast)),
        grid=grid,
        in_specs=[
            pl.BlockSpec((None, 2, n_nj, FFT_N1 * FFT_STEP, HALF), lambda b, j: (b, 0, 0, 0, 0)),
            _const_spec((FFT_N2, 2 * FFT_N1, 2 * FFT_N1)),
            _const_spec((2 * FFT_N2, 2 * FFT_N2)), _const_spec((2 * FNET_W, FNET_W)),
        ] + c_in_specs,
        out_specs=(pl.BlockSpec((None, 2, FFT_N2 // MIX_K2, FFT_STEP * MIX_K2, HALF), lambda b, j: (b, 0, 0, j, 0)),
                   *c_out_specs),
        out_shape=(jax.ShapeDtypeStruct((nb, 2, FFT_N2 // MIX_K2, FFT_N1 * MIX_K2, HALF), F32), *c_out_shapes),
        scratch_shapes=[pltpu.VMEM((2, 2, n_kb, FFT_N2 * FFT_STEP, HALF), F32)],
        compiler_params=_params(2),
        name="fourier",
    )(f_blk, m1, m3, cd, *c_args)


def _fft_ctx_kernel(x_ref, w_ref, cd_ref, o_ref):
    x = jnp.concatenate([x_ref[hf] for hf in range(2)], axis=1).astype(BF16)
    z = _dot(w_ref[...].astype(BF16), x)
    z = jnp.concatenate([z[:CTX_LEN], z[CTX_LEN:]], axis=1).astype(BF16)
    four = _dot(z, cd_ref[...].astype(BF16))
    for hf in range(2):
        o_ref[hf] = four[:, hf * HALF:(hf + 1) * HALF]


def _fourier_ctx(f_ctx):
    nb = f_ctx.shape[2] // CTX_LEN
    w, cd = _ctx_fft_constants()
    return pl.pallas_call(
        _fft_ctx_kernel,
        grid=(nb,),
        in_specs=[
            pl.BlockSpec((None, 2, CTX_LEN, HALF), lambda b: (0, 0, b, 0)),
            _const_spec((2 * CTX_LEN, CTX_LEN)), _const_spec((2 * FNET_W, FNET_W)),
        ],
        out_specs=pl.BlockSpec((None, 2, CTX_LEN, HALF), lambda b: (0, 0, b, 0)),
        out_shape=jax.ShapeDtypeStruct(f_ctx.shape, F32),
        compiler_params=_params(1),
        name="fourier_ctx",
    )(f_ctx, w, cd)


def _mix_mlp_kernel(*refs, tm, ctx, n_cast):
    (x_ref, ret_ref, four_ref, u_ref, vn_ref, gate_ref, mod_ref, ws_ref, bs_ref,
     wa_ref, wb_ref, wc_ref, wo_ref, gpm_ref, gprm_ref, gpom_ref, wup_ref, wdn_ref) = refs[:18]
    o_ref = refs[18 + n_cast]
    _cast_chunks(refs[18:18 + n_cast], refs[19 + n_cast:])
    group = lax.broadcasted_iota(jnp.int32, (GMLP_CHUNK, GMLP_W), 1) // GMLP_GROUP
    parts = []
    for cc in range(tm // GMLP_CHUNK):
        rows = slice(cc * GMLP_CHUNK, (cc + 1) * GMLP_CHUNK)
        vn = vn_ref[0, rows, :]
        s = bs_ref[...]
        for g in range(GMLP_W // GMLP_GROUP):
            s = s + jnp.where(group == g, _dot(ws_ref[g], vn), 0.0)
        parts.append((u_ref[0, rows, :].astype(F32) * s).astype(BF16))
    sgu = jnp.concatenate(parts, axis=0)

    if ctx:
        four = jnp.concatenate([four_ref[0, hf] for hf in range(2)], axis=1).astype(BF16)
    else:
        nk2 = tm // FFT_N1
        four = jnp.concatenate(
            [jnp.concatenate([four_ref[0, hf, pl.ds(k2, FFT_N1, stride=nk2), :] for k2 in range(nk2)], axis=0)
             for hf in range(2)], axis=1).astype(BF16)
    ga = gate_ref[0, :, 0:D_MODEL].astype(F32)
    gb = gate_ref[0, :, D_MODEL:2 * D_MODEL].astype(F32)
    gc = gate_ref[0, :, 2 * D_MODEL:3 * D_MODEL].astype(F32)
    merged = (ga * _dot(ret_ref[0], wa_ref[...]) + gb * _dot(four, wb_ref[...])
              + gc * _dot(sgu, wc_ref[...])).astype(BF16)

    gain1 = mod_ref[0, :, 2 * D_MODEL:3 * D_MODEL] * gpm_ref[...]
    shift2 = mod_ref[0, :, 3 * D_MODEL:4 * D_MODEL]
    gain2 = gprm_ref[...] * (1.0 + mod_ref[0, :, 4 * D_MODEL:5 * D_MODEL])
    gain3 = mod_ref[0, :, 5 * D_MODEL:6 * D_MODEL] * gpom_ref[...]

    halves = [slice(r * (tm // 2), (r + 1) * (tm // 2)) for r in range(2)]
    x1, h2 = [], []
    for rows in halves:
        y = _dot(merged[rows], wo_ref[...])
        x1.append(x_ref[0, rows, :] + _rms(y) * gain1)
        h2.append((_rms(x1[-1]) * gain2 + shift2).astype(BF16))
    up0 = jnp.concatenate([_dot(h, wup_ref[:, 0:FF_CHUNK]) for h in h2], axis=0)
    h2 = jnp.concatenate(h2, axis=0)

    n_ff = D_FF // FF_CHUNK
    m = None
    for c in range(n_ff - 1):
        cols = slice(c * FF_CHUNK, (c + 1) * FF_CHUNK)
        a = jnp.maximum(up0 if c == 0 else _dot(h2, wup_ref[:, cols]), 0.0)
        part = _dot((a * a).astype(BF16), wdn_ref[cols, :])
        m = part if m is None else m + part
    cols = slice((n_ff - 1) * FF_CHUNK, n_ff * FF_CHUNK)
    a = jnp.maximum(_dot(h2, wup_ref[:, cols]), 0.0)
    a = (a * a).astype(BF16)
    for r, rows in enumerate(halves):
        mr = m[rows] + _dot(a[rows], wdn_ref[cols, :])
        o_ref[0, rows, :] = x1[r] + _rms(mr) * gain3


def _mix_mlp(xs, ret, four, ug, vn, gates, layer, mod, ws, bs_tab, wa, wb, wc, wo,
             g_post_mix, g_pre_mlp, g_post_mlp, w_up, w_down, *, tm, ctx, cast=()):
    nb, rows, _ = xs.shape
    tok_spec = lambda w: pl.BlockSpec((1, tm, w), lambda b, i: (b, i, 0))
    lspec = lambda *shape: _layer_spec(shape, layer)
    if ctx:
        four_spec = pl.BlockSpec((1, 2, tm, HALF), lambda b, i: (b, 0, i, 0))
    else:
        four_spec = pl.BlockSpec((1, 2, None, tm, HALF), lambda b, i: (b, 0, i, 0, 0))
    c_in_specs, c_args, c_out_specs, c_out_shapes = _cast_plan(cast, (nb, rows // tm))
    outs = pl.pallas_call(
        functools.partial(_mix_mlp_kernel, tm=tm, ctx=ctx, n_cast=len(cast)),
        grid=(nb, rows // tm),
        in_specs=[
            tok_spec(D_MODEL), tok_spec(RET_W),
            four_spec,
            tok_spec(GMLP_W), tok_spec(GMLP_W), tok_spec(3 * D_MODEL),
            _mod_spec(layer, ctx),
            lspec(GMLP_W // GMLP_GROUP, GMLP_CHUNK, GMLP_CHUNK),
            lspec(GMLP_CHUNK, GMLP_W),
            _const_spec((RET_W, D_MODEL)), _const_spec((FNET_W, D_MODEL)), _const_spec((GMLP_W, D_MODEL)),
            _const_spec((D_MODEL, D_MODEL)),
            lspec(1, D_MODEL), lspec(1, D_MODEL), lspec(1, D_MODEL),
            _const_spec((D_MODEL, D_FF)), _const_spec((D_FF, D_MODEL)),
        ] + c_in_specs,
        out_specs=(tok_spec(D_MODEL),) + tuple(c_out_specs),
        out_shape=(jax.ShapeDtypeStruct((nb, rows, D_MODEL), F32),) + tuple(c_out_shapes),
        compiler_params=_params(2),
        name="mix_mlp_ctx" if ctx else "mix_mlp",
    )(xs, ret, four, ug, vn, gates, mod, ws, bs_tab, wa, wb, wc, wo,
      g_post_mix, g_pre_mlp, g_post_mlp, w_up, w_down, *c_args)
    return outs if cast else outs[0]


def _rope_tables():
    rows = SEQ // GRID_W
    freqs = ROPE_BASE ** (-jnp.arange(ROPE_FREQS, dtype=F32) / ROPE_FREQS)
    ang_r = jnp.arange(rows, dtype=F32)[:, None] * freqs
    ang_c = jnp.arange(GRID_W, dtype=F32)[:, None] * freqs

    def lanes(a, first):
        z = jnp.zeros_like(a)
        return jnp.concatenate([a, z] if first else [z, a], axis=-1)

    def pair(ang, first):
        cos, sin = lanes(jnp.cos(ang), first), lanes(jnp.sin(ang), first)
        return jnp.stack([jnp.concatenate([cos, cos], axis=-1), jnp.concatenate([-sin, sin], axis=-1)])

    return pair(ang_r, True), pair(ang_c, False)


def kernel(x, c, ctx, c_ctx, w_mod, b_mod, g_pre_mix, g_post_mix, g_pre_mlp, g_post_mlp, w_in,
           ret_decay_logit, sgu_w_s, sgu_b_s, sgu_norm, w_branch_a, w_branch_b, w_branch_c, w_out,
           w_up, w_down):
    nb = x.shape[0]
    depth = w_mod.shape[0]
    assert x.shape == (nb, SEQ, D_MODEL) and ctx.shape == (nb, CTX_LEN, D_MODEL) and nb == 2

    cond_rows = jnp.concatenate([c, c_ctx[None, :], jnp.zeros((8 - nb - 1, D_MODEL), F32)], axis=0)
    mod = _modulation(cond_rows, w_mod, b_mod).reshape(depth, 8, 1, 6 * D_MODEL)
    rope_tabs = _rope_tables()
    pavg = jnp.asarray(np.kron(np.eye(GMLP_W // GMLP_GROUP), np.full((GMLP_GROUP, GMLP_GROUP), 1.0 / GMLP_GROUP)), BF16)

    vec = lambda p: p[:, None, :]
    g_pre, g_norm = vec(g_pre_mix), vec(sgu_norm)
    logits = ret_decay_logit.reshape(depth, 2 * HEADS, 1)
    ws = sgu_w_s.astype(BF16)
    bs_tab = jnp.repeat(jnp.swapaxes(sgu_b_s, 1, 2), GMLP_GROUP, axis=2)
    gains = (vec(g_post_mix), vec(g_pre_mlp), vec(g_post_mlp))
    w_in_l = w_in[0].astype(BF16)
    ctx = ctx.reshape(1, nb * CTX_LEN, D_MODEL)
    for l in range(depth):
        last = l == depth - 1
        in_args = (l, mod, g_pre, w_in_l, pavg, g_norm)
        *proj, wa, wb, wc, wo, wup = _in_proj(
            x, *in_args, rope_tabs, tm=TM_IN, ctx=False,
            cast=[(w, l) for w in (w_branch_a, w_branch_b, w_branch_c, w_out, w_up)])
        q, k, v, sg, f_lat, ug, vn, gates = proj
        per_batch = lambda a: a.reshape(nb, CTX_LEN, RET_W)
        if last:
            kc, vc = _in_proj(ctx, *in_args, None, tm=nb * CTX_LEN, ctx=True, kv_only=True)
            (ret,) = _retention(l, logits, (q, k, v, sg), (per_batch(kc), per_batch(vc)), ctx_out=False)
        else:
            qc, kc, vc, sgc, f_ctx, ugc, vnc, gatesc = _in_proj(ctx, *in_args, None, tm=nb * CTX_LEN, ctx=True)
            ret, retc = _retention(l, logits, (q, k, v, sg), tuple(map(per_batch, (qc, kc, vc, sgc))), ctx_out=True)
            retc = retc.reshape(1, nb * CTX_LEN, RET_W)
        four, wdn = _fourier_latent(f_lat, cast=[(w_down, l)])
        mix_w = (l, mod, ws, bs_tab, wa, wb, wc, wo, *gains, wup, wdn)
        if last:
            x = _mix_mlp(x, ret, four, ug, vn, gates, *mix_w, tm=TM_LAT, ctx=False)
        else:
            x, w_in_l = _mix_mlp(x, ret, four, ug, vn, gates, *mix_w, tm=TM_LAT, ctx=False, cast=[(w_in, l + 1)])
            ctx = _mix_mlp(ctx, retc, _fourier_ctx(f_ctx), ugc, vnc, gatesc, *mix_w, tm=nb * CTX_LEN, ctx=True)
    return x
```

```python
import functools

import numpy as np
import jax
import jax.numpy as jnp
from jax import lax
from jax.experimental import pallas as pl
from jax.experimental.pallas import tpu as pltpu

F32 = jnp.float32
BF16 = jnp.bfloat16

D_MODEL = 1024
SEQ = 8192
CTX_LEN = 256
GRID_W = 64
RET_W = 512
HEADS = 4
HEAD_DIM = 128
ROPE_BASE = 10000.0
ROPE_FREQS = HEAD_DIM // 4
FNET_W = 256
FNET_GROUP = 64
GMLP_W = 256
GMLP_GROUP = 64
GMLP_CHUNK = 128
D_FF = 4 * D_MODEL
EPS = 1e-6
IN_W = 4 * RET_W + FNET_W + 2 * GMLP_W + 3 * D_MODEL
COL_F = 4 * RET_W
COL_U = COL_F + FNET_W
COL_VS = COL_U + GMLP_W
COL_GATE = COL_VS + GMLP_W

TM_LAT = 512
TM_IN = 1024
RCHUNK = 256
RET_TILE = 2048
RET_STEPS = SEQ // RET_TILE
RET_CPT = RET_TILE // RCHUNK
RET_AHEAD = 4
N_RCHUNKS = 1 + SEQ // RCHUNK
FF_CHUNK = 1024
FFT_N1 = 64
FFT_N2 = 128
HALF = 128
FFT_STEP = 16
MIX_K2 = TM_LAT // FFT_N1

_VMEM_LIMIT = 56 * 1024 * 1024


def _dot(a, b):
    return jnp.dot(a, b, preferred_element_type=F32)


def _split(x):
    hi = x.astype(BF16)
    lo = (x - hi.astype(F32)).astype(BF16)
    return hi, lo


def _np_split(a64):
    hi = np.asarray(a64, np.float32).astype(BF16)
    lo = (np.asarray(a64, np.float32) - hi.astype(np.float32)).astype(BF16)
    return hi, lo


def _rms(x):
    return x * lax.rsqrt(jnp.mean(x * x, axis=-1, keepdims=True) + EPS)


def _gelu(x):
    return x * (0.5 * (1.0 + jnp.tanh(0.7978845608028654 * (x + 0.044715 * (x * x * x)))))


def _sigmoid(x):
    return 1.0 / (1.0 + jnp.exp(-x))


def _const_spec(shape, nargs=None):
    zeros = (0,) * len(shape)
    return pl.BlockSpec(shape, lambda *_: zeros, pipeline_mode=pl.Buffered(1))


def _layer_spec(shape, layer):
    idx = (layer,) + (0,) * len(shape)
    return pl.BlockSpec((None,) + tuple(shape), lambda *_: idx, pipeline_mode=pl.Buffered(1))


def _params(n_axes):
    return pltpu.CompilerParams(dimension_semantics=("arbitrary",) * n_axes, vmem_limit_bytes=_VMEM_LIMIT)


def _mod_kernel(a_ref, w_ref, b_ref, o_ref):
    @pl.when(pl.program_id(1) == 0)
    def _():
        o_ref[0] = jnp.broadcast_to(b_ref[0], o_ref.shape[1:])

    a = a_ref[...]
    a = a * _sigmoid(a)
    ah, al = _split(a)
    w = w_ref[0].astype(BF16)
    o_ref[0] += _dot(ah, w) + _dot(al, w)


def _modulation(cond_rows, w_mod, b_mod):
    depth = w_mod.shape[0]
    tk = 128
    return pl.pallas_call(
        _mod_kernel,
        grid=(depth, D_MODEL // tk),
        in_specs=[
            pl.BlockSpec((8, tk), lambda l, k: (0, k)),
            pl.BlockSpec((1, tk, 6 * D_MODEL), lambda l, k: (l, k, 0)),
            pl.BlockSpec((1, 1, 6 * D_MODEL), lambda l, k: (l, 0, 0)),
        ],
        out_specs=pl.BlockSpec((1, 8, 6 * D_MODEL), lambda l, k: (l, 0, 0)),
        out_shape=jax.ShapeDtypeStruct((depth, 8, 6 * D_MODEL), F32),
        compiler_params=_params(2),
        name="modulation",
    )(cond_rows, w_mod, b_mod.reshape(depth, 1, 6 * D_MODEL))


def _cast_plan(cast, grid):
    steps = int(np.prod(grid))
    in_specs, args, out_specs, out_shapes = [], [], [], []

    def step(*idx):
        t = idx[0]
        for n, i in zip(grid[1:], idx[1:]):
            t = t * n + i
        return t

    for stack, layer in cast:
        _, rows, cols = stack.shape
        n_chunks = rows // 16
        while n_chunks > steps:
            n_chunks //= 2
        assert rows % n_chunks == 0
        per = steps // n_chunks
        chunk = lambda *idx, per=per, last=n_chunks - 1: jnp.minimum(step(*idx) // per, last)
        in_specs.append(pl.BlockSpec((None, rows // n_chunks, cols),
                                     lambda *idx, layer=layer, chunk=chunk: (layer, chunk(*idx), 0)))
        out_specs.append(pl.BlockSpec((rows // n_chunks, cols), lambda *idx, chunk=chunk: (chunk(*idx), 0)))
        out_shapes.append(jax.ShapeDtypeStruct((rows, cols), BF16))
        args.append(stack)
    return in_specs, args, out_specs, out_shapes


def _cast_chunks(src_refs, dst_refs):
    for src, dst in zip(src_refs, dst_refs, strict=True):
        dst[...] = src[...].astype(BF16)


def _in_proj_kernel(*refs, rope, kv_only, n_cast):
    n_in = 8 if rope else 6
    ins, cast_src = refs[:n_in], refs[n_in:n_in + n_cast]
    outs, cast_dst = refs[n_in + n_cast:len(refs) - n_cast], refs[len(refs) - n_cast:]
    _cast_chunks(cast_src, cast_dst)
    if rope:
        x_ref, mod_ref, gpre_ref, w_ref, pavg_ref, gn_ref, rrow_ref, rcol_ref = ins
    else:
        x_ref, mod_ref, gpre_ref, w_ref, pavg_ref, gn_ref = ins
    tm = x_ref.shape[1]

    def prepare():
        shift = mod_ref[0, :, 0:D_MODEL]
        gain = gpre_ref[...] * (1.0 + mod_ref[0, :, D_MODEL:2 * D_MODEL])
        return (_rms(x_ref[0]) * gain + shift).astype(BF16)

    def project(hb):
        def proj(lo, hi):
            return _dot(hb, w_ref[:, lo:hi])

        if rope:
            n_rows = tm // GRID_W
            rot = []
            for t in range(2):
                by_row = jnp.concatenate(
                    [jnp.broadcast_to(rrow_ref[t, r:r + 1, :], (GRID_W, HEAD_DIM)) for r in range(n_rows)], axis=0)
                rot.append(by_row + jnp.concatenate([rcol_ref[t]] * n_rows, axis=0))

        def rope_store(z, dst):
            if not rope:
                dst[0] = z.astype(BF16)
                return
            cos, sin = rot
            for hd in range(HEADS):
                a = z[:, hd * HEAD_DIM:(hd + 1) * HEAD_DIM]
                r = a * cos + pltpu.roll(a, HEAD_DIM // 2, 1) * sin
                dst[0, :, hd * HEAD_DIM:(hd + 1) * HEAD_DIM] = r.astype(BF16)

        if kv_only:
            k_ref, v_ref = outs
        else:
            q_ref, k_ref, v_ref, sg_ref, f_ref, u_ref, vn_ref, gate_ref = outs
            rope_store(proj(0, RET_W), q_ref)
        rope_store(proj(RET_W, 2 * RET_W) * (HEAD_DIM ** -0.5), k_ref)
        v_ref[0] = proj(2 * RET_W, 3 * RET_W).astype(BF16)
        if kv_only:
            return
        g = proj(3 * RET_W, 4 * RET_W)
        sg_ref[0] = (g * _sigmoid(g)).astype(BF16)

        f = proj(COL_F, COL_U)
        for hf in range(2):
            fh = f[:, hf * HALF:(hf + 1) * HALF]
            if rope:
                for n1 in range(tm // FFT_N2):
                    for nj in range(FFT_N2 // FFT_STEP):
                        f_ref[0, hf, nj, n1 * FFT_STEP:(n1 + 1) * FFT_STEP, :] = (
                            fh[n1 * FFT_N2 + nj * FFT_STEP:n1 * FFT_N2 + (nj + 1) * FFT_STEP])
            else:
                f_ref[0, hf] = fh

        u_ref[0] = _gelu(proj(COL_U, COL_VS)).astype(BF16)
        vg = _gelu(proj(COL_VS, COL_GATE))
        sh, sl = _split(vg * vg)
        pavg = pavg_ref[...]
        ms = _dot(sh, pavg) + _dot(sl, pavg)
        vn_ref[0] = (vg * lax.rsqrt(ms + EPS) * gn_ref[...]).astype(BF16)

        for c in range(3):
            z = proj(COL_GATE + c * D_MODEL, COL_GATE + (c + 1) * D_MODEL)
            gate_ref[0, :, c * D_MODEL:(c + 1) * D_MODEL] = _sigmoid(z).astype(BF16)

    project(prepare())


def _mod_spec(layer, ctx):
    idx = (lambda b, i: (layer, 2, 0, 0)) if ctx else (lambda b, i: (layer, b, 0, 0))
    return pl.BlockSpec((None, 1, 1, 6 * D_MODEL), idx)


def _in_proj(xs, layer, mod, g_pre, w_in, pavg, g_norm, rope_tabs, *, tm, ctx, kv_only=False, cast=()):
    nb, rows, _ = xs.shape
    rope = rope_tabs is not None
    grid = (nb, rows // tm)
    tok_spec = lambda w: pl.BlockSpec((1, tm, w), lambda b, i: (b, i, 0))
    in_specs = [
        tok_spec(D_MODEL),
        _mod_spec(layer, ctx),
        _layer_spec((1, D_MODEL), layer),
        _const_spec((D_MODEL, IN_W)),
        _const_spec((GMLP_W, GMLP_W)),
        _layer_spec((1, GMLP_W), layer),
    ]
    args = [xs, mod, g_pre, w_in, pavg, g_norm]
    if rope:
        in_specs += [pl.BlockSpec((2, tm // GRID_W, HEAD_DIM), lambda b, i: (0, i, 0)),
                     _const_spec((2, GRID_W, HEAD_DIM))]
        args += list(rope_tabs)
    c_in_specs, c_args, c_out_specs, c_out_shapes = _cast_plan(cast, grid)
    in_specs += c_in_specs
    args += c_args
    bf = lambda w: jax.ShapeDtypeStruct((nb, rows, w), BF16)
    if kv_only:
        out_shapes = (bf(RET_W), bf(RET_W))
        out_specs = (tok_spec(RET_W), tok_spec(RET_W))
    else:
        if rope:
            n_nj = FFT_N2 // FFT_STEP
            f_shape = jax.ShapeDtypeStruct((nb, 2, n_nj, FFT_N1 * FFT_STEP, HALF), F32)
            f_spec = pl.BlockSpec((1, 2, n_nj, (tm // FFT_N2) * FFT_STEP, HALF), lambda b, i: (b, 0, 0, i, 0))
        else:
            f_shape = jax.ShapeDtypeStruct((nb, 2, rows, HALF), F32)
            f_spec = pl.BlockSpec((1, 2, tm, HALF), lambda b, i: (b, 0, i, 0))
        out_shapes = (
            bf(RET_W), bf(RET_W), bf(RET_W), bf(RET_W),
            f_shape,
            bf(GMLP_W), bf(GMLP_W),
            bf(3 * D_MODEL),
        )
        out_specs = (
            tok_spec(RET_W), tok_spec(RET_W), tok_spec(RET_W), tok_spec(RET_W),
            f_spec,
            tok_spec(GMLP_W), tok_spec(GMLP_W), tok_spec(3 * D_MODEL),
        )
    return pl.pallas_call(
        functools.partial(_in_proj_kernel, rope=rope, kv_only=kv_only, n_cast=len(cast)),
        grid=grid,
        in_specs=in_specs,
        out_specs=tuple(out_specs) + tuple(c_out_specs),
        out_shape=tuple(out_shapes) + tuple(c_out_shapes),
        compiler_params=_params(2),
        name="in_proj_ctx" if ctx else "in_proj",
    )(*args)


def _ret_kernel(*refs, ctx_out, n_cast):
    n_in, n_out = (9, 2) if ctx_out else (7, 1)
    ins, refs = refs[:n_in], refs[n_in:]
    cast_src, refs = refs[:n_cast], refs[n_cast:]
    outs, refs = refs[:n_out], refs[n_out:]
    cast_dst, scratch = refs[:n_cast], refs[n_cast:]
    _cast_chunks(cast_src, cast_dst)
    (logit_ref, ql_ref, kl_ref, vl_ref, sgl_ref, kc_ref, vc_ref) = ins[:7]
    if ctx_out:
        qc_ref, sgc_ref = ins[7:]
        ol_ref, oc_ref = outs
    else:
        (ol_ref,) = outs
    mask_ref, wf_ref, wb_ref, qf_ref, qb_ref, df_ref, db_ref, sf_ref, sb_ref, sball_ref = scratch
    b = pl.program_id(0)
    phase = pl.program_id(1)
    j = pl.program_id(2)

    @pl.when((b == 0) & (phase == 0) & (j == 0))
    def _():
        x = logit_ref[...]
        lg = -(jnp.maximum(-x, 0.0) + jnp.log(1.0 + jnp.exp(-jnp.abs(x))))
        row = lax.broadcasted_iota(jnp.int32, (RCHUNK, RCHUNK), 0).astype(F32)
        col = lax.broadcasted_iota(jnp.int32, (RCHUNK, RCHUNK), 1).astype(F32)
        diff = row - col
        pos = row[:, :HEAD_DIM]
        for hd in range(HEADS):
            lf = lg[hd:hd + 1, :]
            lb = lg[HEADS + hd:HEADS + hd + 1, :]
            mask_ref[hd] = jnp.where(diff >= 0.0, jnp.exp(lf * jnp.maximum(diff, 0.0)),
                                     jnp.exp(lb * jnp.maximum(-diff, 0.0)))
            wf_ref[hd] = jnp.exp(lf * (RCHUNK - 1.0 - pos))
            wb_ref[hd] = jnp.exp(lb * pos)
            qf_ref[hd] = jnp.exp(lf * (pos + 1.0))
            qb_ref[hd] = jnp.exp(lb * (RCHUNK - pos))
            df_ref[hd] = jnp.exp(jnp.broadcast_to(lf, (HEAD_DIM, HEAD_DIM)) * float(RCHUNK))
            db_ref[hd] = jnp.exp(jnp.broadcast_to(lb, (HEAD_DIM, HEAD_DIM)) * float(RCHUNK))

    def chunk_updates(hd, k_ref, v_ref, n_chunks, w_ref):
        cols = slice(hd * HEAD_DIM, (hd + 1) * HEAD_DIM)
        us = []
        for c in range(n_chunks):
            rows = slice(c * RCHUNK, (c + 1) * RCHUNK)
            kw = (k_ref[0, rows, cols].astype(F32) * w_ref[hd]).astype(BF16)
            us.append(lax.dot_general(kw, v_ref[0, rows, cols], (((0,), (0,)), ((), ())),
                                      preferred_element_type=F32))
        return us

    def backward_chunks(k_ref, v_ref, n_chunks, first_chunk):
        for hd in range(HEADS):
            us = chunk_updates(hd, k_ref, v_ref, n_chunks, wb_ref)
            s = sb_ref[hd]
            for c in reversed(range(n_chunks)):
                sball_ref[first_chunk + c, hd] = s.astype(BF16)
                s = db_ref[hd] * s + us[c]
            sb_ref[hd] = s

    def forward_chunks(q_ref, k_ref, v_ref, sg_ref, o_ref, n_chunks, first_chunk):
        starts = []
        for hd in range(HEADS):
            us = chunk_updates(hd, k_ref, v_ref, n_chunks, wf_ref)
            st = [sf_ref[hd]]
            for c in range(n_chunks):
                st.append(df_ref[hd] * st[-1] + us[c])
            sf_ref[hd] = st[-1]
            starts.append(st)
        if o_ref is None:
            return
        units = [(c, hd) for c in range(n_chunks) for hd in range(HEADS)]

        def scores(c, hd):
            rows = slice(c * RCHUNK, (c + 1) * RCHUNK)
            cols = slice(hd * HEAD_DIM, (hd + 1) * HEAD_DIM)
            return lax.dot_general(q_ref[0, rows, cols], k_ref[0, rows, cols], (((1,), (1,)), ((), ())),
                                   preferred_element_type=F32)

        ahead = min(RET_AHEAD, len(units))
        pending = [scores(*u) for u in units[:ahead]]
        for n, (c, hd) in enumerate(units):
            if n + ahead < len(units):
                pending.append(scores(*units[n + ahead]))
            sc = pending.pop(0)
            rows = slice(c * RCHUNK, (c + 1) * RCHUNK)
            cols = slice(hd * HEAD_DIM, (hd + 1) * HEAD_DIM)
            qf = q_ref[0, rows, cols].astype(F32)
            lhs = jnp.concatenate([(sc * mask_ref[hd]).astype(BF16),
                                   (qf * qf_ref[hd]).astype(BF16),
                                   (qf * qb_ref[hd]).astype(BF16)], axis=1)
            rhs = jnp.concatenate([v_ref[0, rows, cols], starts[hd][c].astype(BF16),
                                   sball_ref[first_chunk + c, hd]], axis=0)
            o = _dot(lhs, rhs)
            o_ref[0, rows, cols] = (_rms(o) * sg_ref[0, rows, cols].astype(F32)).astype(BF16)

    @pl.when(phase == 0)
    def _():
        @pl.when(j == 0)
        def _():
            sb_ref[...] = jnp.zeros_like(sb_ref)
            backward_chunks(kc_ref, vc_ref, 1, 0)

        backward_chunks(kl_ref, vl_ref, RET_CPT, 1 + (RET_STEPS - 1 - j) * RET_CPT)

    @pl.when(phase == 1)
    def _():
        @pl.when(j == 0)
        def _():
            sf_ref[...] = jnp.zeros_like(sf_ref)
            if ctx_out:
                forward_chunks(qc_ref, kc_ref, vc_ref, sgc_ref, oc_ref, 1, 0)
            else:
                forward_chunks(None, kc_ref, vc_ref, None, None, 1, 0)

        forward_chunks(ql_ref, kl_ref, vl_ref, sgl_ref, ol_ref, RET_CPT, 1 + j * RET_CPT)


def _retention(layer, logits, lat, ctx, *, ctx_out, cast=()):
    nb = lat[0].shape[0]
    kv_idx = lambda b, p, j: (b, jnp.where(p == 0, RET_STEPS - 1 - j, j), 0)
    q_idx = lambda b, p, j: (b, jnp.where(p == 0, 0, j), 0)
    c_idx = lambda b, p, j: (b, 0, 0)
    lat_blk = (1, RET_TILE, RET_W)
    ctx_blk = (1, CTX_LEN, RET_W)
    in_specs = [
        _layer_spec((2 * HEADS, 1), layer),
        pl.BlockSpec(lat_blk, q_idx), pl.BlockSpec(lat_blk, kv_idx), pl.BlockSpec(lat_blk, kv_idx),
        pl.BlockSpec(lat_blk, q_idx),
        pl.BlockSpec(ctx_blk, c_idx), pl.BlockSpec(ctx_blk, c_idx),
    ]
    if ctx_out:
        qc, kc, vc, sgc = ctx
        args = [logits, *lat, kc, vc, qc, sgc]
        in_specs += [pl.BlockSpec(ctx_blk, c_idx), pl.BlockSpec(ctx_blk, c_idx)]
        out_specs = (pl.BlockSpec(lat_blk, q_idx), pl.BlockSpec(ctx_blk, c_idx))
        out_shape = (jax.ShapeDtypeStruct((nb, SEQ, RET_W), BF16), jax.ShapeDtypeStruct((nb, CTX_LEN, RET_W), BF16))
    else:
        kc, vc = ctx
        args = [logits, *lat, kc, vc]
        out_specs = (pl.BlockSpec(lat_blk, q_idx),)
        out_shape = (jax.ShapeDtypeStruct((nb, SEQ, RET_W), BF16),)
    grid = (nb, 2, RET_STEPS)
    c_in_specs, c_args, c_out_specs, c_out_shapes = _cast_plan(cast, grid)
    state = pltpu.VMEM((HEADS, HEAD_DIM, HEAD_DIM), F32)
    pos_tab = pltpu.VMEM((HEADS, RCHUNK, HEAD_DIM), F32)
    return pl.pallas_call(
        functools.partial(_ret_kernel, ctx_out=ctx_out, n_cast=len(cast)),
        grid=grid,
        in_specs=in_specs + c_in_specs,
        out_specs=out_specs + tuple(c_out_specs),
        out_shape=out_shape + tuple(c_out_shapes),
        scratch_shapes=[
            pltpu.VMEM((HEADS, RCHUNK, RCHUNK), F32),
            pos_tab, pos_tab, pos_tab, pos_tab,
            state, state,
            state, state,
            pltpu.VMEM((N_RCHUNKS, HEADS, HEAD_DIM, HEAD_DIM), BF16),
        ],
        compiler_params=_params(3),
        name="retention",
    )(*args, *c_args)


def _dft_cos_sin(n):
    idx = np.arange(n, dtype=np.float64)
    ang = 2.0 * np.pi * ((idx[:, None] * idx[None, :]) % n) / n
    return np.cos(ang), np.sin(ang)


def _channel_dft(scale):
    c, s = _dft_cos_sin(FNET_GROUP)
    eye = np.eye(FNET_W // FNET_GROUP)
    return np.concatenate([np.kron(eye, c), np.kron(eye, s)], axis=0) * scale


def _fft_constants():
    k1 = np.arange(FFT_N1, dtype=np.float64)[None, :, None]
    n1 = np.arange(FFT_N1, dtype=np.float64)[None, None, :]
    n2 = np.arange(FFT_N2, dtype=np.float64)[:, None, None]
    ang = 2.0 * np.pi * ((k1 * (FFT_N2 * n1 + n2)) % SEQ) / SEQ
    m1 = np.concatenate([np.cos(ang), -np.sin(ang)], axis=1)
    c, s = _dft_cos_sin(FFT_N2)
    m3 = np.block([[c, s], [-s, c]])
    return (_np_split_cat(m1, -1), jnp.asarray(m3, F32),
            jnp.asarray(_channel_dft((SEQ * FNET_GROUP) ** -0.5), F32))


def _ctx_fft_constants():
    c, s = _dft_cos_sin(CTX_LEN)
    return (jnp.asarray(np.concatenate([c, -s], axis=0), F32),
            jnp.asarray(_channel_dft((CTX_LEN * FNET_GROUP) ** -0.5), F32))


def _np_split_cat(a64, axis):
    hi, lo = _np_split(a64)
    return jnp.asarray(np.concatenate([hi, lo], axis=axis))


def _twice(x, axis):
    return jnp.concatenate([x, x], axis=axis)


def _fft_kernel(*refs, n_cast):
    x_ref, m_ref, w_ref, cd_ref = refs[:4]
    o_ref, a_ref = refs[4 + n_cast], refs[-1]
    _cast_chunks(refs[4:4 + n_cast], refs[5 + n_cast:-1])
    _fft_body(x_ref, m_ref, w_ref, cd_ref, o_ref, a_ref)


def _fft_body(x_ref, m_ref, w_ref, cd_ref, o_ref, a_ref):
    n_kb = FFT_N1 // FFT_STEP

    def stage1(nj, carry):
        for jj in range(FFT_STEP):
            x = jnp.concatenate([x_ref[hf, nj, pl.ds(jj, FFT_N1, stride=FFT_STEP), :] for hf in range(2)],
                                axis=1).astype(BF16)
            a = _dot(m_ref[nj * FFT_STEP + jj], _twice(x, 0))
            row = pl.multiple_of((nj * FFT_STEP + jj) * FFT_STEP, FFT_STEP)
            for hf in range(2):
                for ri in range(2):
                    for kb in range(n_kb):
                        r0 = ri * FFT_N1 + kb * FFT_STEP
                        a_ref[hf, ri, kb, pl.ds(row, FFT_STEP), :] = a[r0:r0 + FFT_STEP, hf * HALF:(hf + 1) * HALF]
        return carry

    kb = pl.program_id(1)

    @pl.when(kb == 0)
    def _():
        lax.fori_loop(0, FFT_N2 // FFT_STEP, stage1, 0)

    w = w_ref[...].astype(BF16)
    zs = []
    for jj in range(FFT_STEP):
        rhs = jnp.concatenate(
            [jnp.concatenate([a_ref[hf, ri, kb, pl.ds(jj, FFT_N2, stride=FFT_STEP), :] for ri in range(2)], axis=0)
             for hf in range(2)], axis=1).astype(BF16)
        y = _dot(w, rhs)
        zs.append(jnp.concatenate([y[:FFT_N2], y[FFT_N2:]], axis=1).astype(BF16))
    z = jnp.concatenate(zs, axis=0)
    four = _dot(z, cd_ref[...].astype(BF16))
    for jj in range(FFT_STEP):
        for kt in range(FFT_N2 // MIX_K2):
            for hf in range(2):
                r0 = jj * FFT_N2 + kt * MIX_K2
                o_ref[hf, kt, jj * MIX_K2:(jj + 1) * MIX_K2, :] = four[r0:r0 + MIX_K2, hf * HALF:(hf + 1) * HALF]


def _fourier_latent(f_blk, cast=()):
    nb = f_blk.shape[0]
    m1, m3, cd = _fft_constants()
    n_nj, n_kb = FFT_N2 // FFT_STEP, FFT_N1 // FFT_STEP
    grid = (nb, n_kb)
    c_in_specs, c_args, c_out_specs, c_out_shapes = _cast_plan(cast, grid)
    return pl.pallas_call(
        functools.partial(_fft_kernel, n_cast=len(cast)),
        grid=grid,
        in_specs=[
            pl.BlockSpec((None, 2, n_nj, FFT_N1 * FFT_STEP, HALF), lambda b, j: (b, 0, 0, 0, 0)),
            _const_spec((FFT_N2, 2 * FFT_N1, 2 * FFT_N1)),
            _const_spec((2 * FFT_N2, 2 * FFT_N2)), _const_spec((2 * FNET_W, FNET_W)),
        ] + c_in_specs,
        out_specs=(pl.BlockSpec((None, 2, FFT_N2 // MIX_K2, FFT_STEP * MIX_K2, HALF), lambda b, j: (b, 0, 0, j, 0)),
                   *c_out_specs),
        out_shape=(jax.ShapeDtypeStruct((nb, 2, FFT_N2 // MIX_K2, FFT_N1 * MIX_K2, HALF), F32), *c_out_shapes),
        scratch_shapes=[pltpu.VMEM((2, 2, n_kb, FFT_N2 * FFT_STEP, HALF), F32)],
        compiler_params=_params(2),
        name="fourier",
    )(f_blk, m1, m3, cd, *c_args)


def _fft_ctx_kernel(x_ref, w_ref, cd_ref, o_ref):
    x = jnp.concatenate([x_ref[hf] for hf in range(2)], axis=1).astype(BF16)
    z = _dot(w_ref[...].astype(BF16), x)
    z = jnp.concatenate([z[:CTX_LEN], z[CTX_LEN:]], axis=1).astype(BF16)
    four = _dot(z, cd_ref[...].astype(BF16))
    for hf in range(2):
        o_ref[hf] = four[:, hf * HALF:(hf + 1) * HALF]


def _fourier_ctx(f_ctx):
    nb = f_ctx.shape[2] // CTX_LEN
    w, cd = _ctx_fft_constants()
    return pl.pallas_call(
        _fft_ctx_kernel,
        grid=(nb,),
        in_specs=[
            pl.BlockSpec((None, 2, CTX_LEN, HALF), lambda b: (0, 0, b, 0)),
            _const_spec((2 * CTX_LEN, CTX_LEN)), _const_spec((2 * FNET_W, FNET_W)),
        ],
        out_specs=pl.BlockSpec((None, 2, CTX_LEN, HALF), lambda b: (0, 0, b, 0)),
        out_shape=jax.ShapeDtypeStruct(f_ctx.shape, F32),
        compiler_params=_params(1),
        name="fourier_ctx",
    )(f_ctx, w, cd)


def _mix_mlp_kernel(*refs, tm, ctx, n_cast):
    (x_ref, ret_ref, four_ref, u_ref, vn_ref, gate_ref, mod_ref, ws_ref, bs_ref,
     wa_ref, wb_ref, wc_ref, wo_ref, gpm_ref, gprm_ref, gpom_ref, wup_ref, wdn_ref) = refs[:18]
    o_ref = refs[18 + n_cast]
    _cast_chunks(refs[18:18 + n_cast], refs[19 + n_cast:])
    group = lax.broadcasted_iota(jnp.int32, (GMLP_CHUNK, GMLP_W), 1) // GMLP_GROUP
    parts = []
    for cc in range(tm // GMLP_CHUNK):
        rows = slice(cc * GMLP_CHUNK, (cc + 1) * GMLP_CHUNK)
        vn = vn_ref[0, rows, :]
        s = bs_ref[...]
        for g in range(GMLP_W // GMLP_GROUP):
            s = s + jnp.where(group == g, _dot(ws_ref[g], vn), 0.0)
        parts.append((u_ref[0, rows, :].astype(F32) * s).astype(BF16))
    sgu = jnp.concatenate(parts, axis=0)

    if ctx:
        four = jnp.concatenate([four_ref[0, hf] for hf in range(2)], axis=1).astype(BF16)
    else:
        nk2 = tm // FFT_N1
        four = jnp.concatenate(
            [jnp.concatenate([four_ref[0, hf, pl.ds(k2, FFT_N1, stride=nk2), :] for k2 in range(nk2)], axis=0)
             for hf in range(2)], axis=1).astype(BF16)
    ga = gate_ref[0, :, 0:D_MODEL].astype(F32)
    gb = gate_ref[0, :, D_MODEL:2 * D_MODEL].astype(F32)
    gc = gate_ref[0, :, 2 * D_MODEL:3 * D_MODEL].astype(F32)
    merged = (ga * _dot(ret_ref[0], wa_ref[...]) + gb * _dot(four, wb_ref[...])
              + gc * _dot(sgu, wc_ref[...])).astype(BF16)

    gain1 = mod_ref[0, :, 2 * D_MODEL:3 * D_MODEL] * gpm_ref[...]
    shift2 = mod_ref[0, :, 3 * D_MODEL:4 * D_MODEL]
    gain2 = gprm_ref[...] * (1.0 + mod_ref[0, :, 4 * D_MODEL:5 * D_MODEL])
    gain3 = mod_ref[0, :, 5 * D_MODEL:6 * D_MODEL] * gpom_ref[...]

    halves = [slice(r * (tm // 2), (r + 1) * (tm // 2)) for r in range(2)]
    x1, h2 = [], []
    for rows in halves:
        y = _dot(merged[rows], wo_ref[...])
        x1.append(x_ref[0, rows, :] + _rms(y) * gain1)
        h2.append((_rms(x1[-1]) * gain2 + shift2).astype(BF16))
    up0 = jnp.concatenate([_dot(h, wup_ref[:, 0:FF_CHUNK]) for h in h2], axis=0)
    h2 = jnp.concatenate(h2, axis=0)

    n_ff = D_FF // FF_CHUNK
    m = None
    for c in range(n_ff - 1):
        cols = slice(c * FF_CHUNK, (c + 1) * FF_CHUNK)
        a = jnp.maximum(up0 if c == 0 else _dot(h2, wup_ref[:, cols]), 0.0)
        part = _dot((a * a).astype(BF16), wdn_ref[cols, :])
        m = part if m is None else m + part
    cols = slice((n_ff - 1) * FF_CHUNK, n_ff * FF_CHUNK)
    a = jnp.maximum(_dot(h2, wup_ref[:, cols]), 0.0)
    a = (a * a).astype(BF16)
    for r, rows in enumerate(halves):
        mr = m[rows] + _dot(a[rows], wdn_ref[cols, :])
        o_ref[0, rows, :] = x1[r] + _rms(mr) * gain3


def _mix_mlp(xs, ret, four, ug, vn, gates, layer, mod, ws, bs_tab, wa, wb, wc, wo,
             g_post_mix, g_pre_mlp, g_post_mlp, w_up, w_down, *, tm, ctx, cast=()):
    nb, rows, _ = xs.shape
    tok_spec = lambda w: pl.BlockSpec((1, tm, w), lambda b, i: (b, i, 0))
    lspec = lambda *shape: _layer_spec(shape, layer)
    if ctx:
        four_spec = pl.BlockSpec((1, 2, tm, HALF), lambda b, i: (b, 0, i, 0))
    else:
        four_spec = pl.BlockSpec((1, 2, None, tm, HALF), lambda b, i: (b, 0, i, 0, 0))
    c_in_specs, c_args, c_out_specs, c_out_shapes = _cast_plan(cast, (nb, rows // tm))
    outs = pl.pallas_call(
        functools.partial(_mix_mlp_kernel, tm=tm, ctx=ctx, n_cast=len(cast)),
        grid=(nb, rows // tm),
        in_specs=[
            tok_spec(D_MODEL), tok_spec(RET_W),
            four_spec,
            tok_spec(GMLP_W), tok_spec(GMLP_W), tok_spec(3 * D_MODEL),
            _mod_spec(layer, ctx),
            lspec(GMLP_W // GMLP_GROUP, GMLP_CHUNK, GMLP_CHUNK),
            lspec(GMLP_CHUNK, GMLP_W),
            _const_spec((RET_W, D_MODEL)), _const_spec((FNET_W, D_MODEL)), _const_spec((GMLP_W, D_MODEL)),
            _const_spec((D_MODEL, D_MODEL)),
            lspec(1, D_MODEL), lspec(1, D_MODEL), lspec(1, D_MODEL),
            _const_spec((D_MODEL, D_FF)), _const_spec((D_FF, D_MODEL)),
        ] + c_in_specs,
        out_specs=(tok_spec(D_MODEL),) + tuple(c_out_specs),
        out_shape=(jax.ShapeDtypeStruct((nb, rows, D_MODEL), F32),) + tuple(c_out_shapes),
        compiler_params=_params(2),
        name="mix_mlp_ctx" if ctx else "mix_mlp",
    )(xs, ret, four, ug, vn, gates, mod, ws, bs_tab, wa, wb, wc, wo,
      g_post_mix, g_pre_mlp, g_post_mlp, w_up, w_down, *c_args)
    return outs if cast else outs[0]


def _rope_tables():
    rows = SEQ // GRID_W
    freqs = ROPE_BASE ** (-jnp.arange(ROPE_FREQS, dtype=F32) / ROPE_FREQS)
    ang_r = jnp.arange(rows, dtype=F32)[:, None] * freqs
    ang_c = jnp.arange(GRID_W, dtype=F32)[:, None] * freqs

    def lanes(a, first):
        z = jnp.zeros_like(a)
        return jnp.concatenate([a, z] if first else [z, a], axis=-1)

    def pair(ang, first):
        cos, sin = lanes(jnp.cos(ang), first), lanes(jnp.sin(ang), first)
        return jnp.stack([jnp.concatenate([cos, cos], axis=-1), jnp.concatenate([-sin, sin], axis=-1)])

    return pair(ang_r, True), pair(ang_c, False)


def kernel(x, c, ctx, c_ctx, w_mod, b_mod, g_pre_mix, g_post_mix, g_pre_mlp, g_post_mlp, w_in,
           ret_decay_logit, sgu_w_s, sgu_b_s, sgu_norm, w_branch_a, w_branch_b, w_branch_c, w_out,
           w_up, w_down):
    nb = x.shape[0]
    depth = w_mod.shape[0]
    assert x.shape == (nb, SEQ, D_MODEL) and ctx.shape == (nb, CTX_LEN, D_MODEL) and nb == 2

    cond_rows = jnp.concatenate([c, c_ctx[None, :], jnp.zeros((8 - nb - 1, D_MODEL), F32)], axis=0)
    mod = _modulation(cond_rows, w_mod, b_mod).reshape(depth, 8, 1, 6 * D_MODEL)
    rope_tabs = _rope_tables()
    pavg = jnp.asarray(np.kron(np.eye(GMLP_W // GMLP_GROUP), np.full((GMLP_GROUP, GMLP_GROUP), 1.0 / GMLP_GROUP)), BF16)

    vec = lambda p: p[:, None, :]
    g_pre, g_norm = vec(g_pre_mix), vec(sgu_norm)
    logits = ret_decay_logit.reshape(depth, 2 * HEADS, 1)
    ws = sgu_w_s.astype(BF16)
    bs_tab = jnp.repeat(jnp.swapaxes(sgu_b_s, 1, 2), GMLP_GROUP, axis=2)
    gains = (vec(g_post_mix), vec(g_pre_mlp), vec(g_post_mlp))
    w_in_l = w_in[0].astype(BF16)
    ctx = ctx.reshape(1, nb * CTX_LEN, D_MODEL)
    for l in range(depth):
        last = l == depth - 1
        in_args = (l, mod, g_pre, w_in_l, pavg, g_norm)
        *proj, wa, wb, wc, wo, wup = _in_proj(
            x, *in_args, rope_tabs, tm=TM_IN, ctx=False,
            cast=[(w, l) for w in (w_branch_a, w_branch_b, w_branch_c, w_out, w_up)])
        q, k, v, sg, f_lat, ug, vn, gates = proj
        per_batch = lambda a: a.reshape(nb, CTX_LEN, RET_W)
        if last:
            kc, vc = _in_proj(ctx, *in_args, None, tm=nb * CTX_LEN, ctx=True, kv_only=True)
            (ret,) = _retention(l, logits, (q, k, v, sg), (per_batch(kc), per_batch(vc)), ctx_out=False)
        else:
            qc, kc, vc, sgc, f_ctx, ugc, vnc, gatesc = _in_proj(ctx, *in_args, None, tm=nb * CTX_LEN, ctx=True)
            ret, retc = _retention(l, logits, (q, k, v, sg), tuple(map(per_batch, (qc, kc, vc, sgc))), ctx_out=True)
            retc = retc.reshape(1, nb * CTX_LEN, RET_W)
        four, wdn = _fourier_latent(f_lat, cast=[(w_down, l)])
        mix_w = (l, mod, ws, bs_tab, wa, wb, wc, wo, *gains, wup, wdn)
        if last:
            x = _mix_mlp(x, ret, four, ug, vn, gates, *mix_w, tm=TM_LAT, ctx=False)
        else:
            x, w_in_l = _mix_mlp(x, ret, four, ug, vn, gates, *mix_w, tm=TM_LAT, ctx=False, cast=[(w_in, l + 1)])
            ctx = _mix_mlp(ctx, retc, _fourier_ctx(f_ctx), ugc, vnc, gatesc, *mix_w, tm=nb * CTX_LEN, ctx=True)
    return x
```

```python
import functools

import numpy as np
import jax
import jax.numpy as jnp
from jax import lax
from jax.experimental import pallas as pl
from jax.experimental.pallas import tpu as pltpu

F32 = jnp.float32
BF16 = jnp.bfloat16

D_MODEL = 1024
SEQ = 8192
CTX_LEN = 256
GRID_W = 64
RET_W = 512
HEADS = 4
HEAD_DIM = 128
ROPE_BASE = 10000.0
ROPE_FREQS = HEAD_DIM // 4
FNET_W = 256
FNET_GROUP = 64
GMLP_W = 256
GMLP_GROUP = 64
GMLP_CHUNK = 128
D_FF = 4 * D_MODEL
EPS = 1e-6
IN_W = 4 * RET_W + FNET_W + 2 * GMLP_W + 3 * D_MODEL
COL_F = 4 * RET_W
COL_U = COL_F + FNET_W
COL_VS = COL_U + GMLP_W
COL_GATE = COL_VS + GMLP_W

TM_LAT = 512
TM_IN = 1024
RCHUNK = 256
RET_TILE = 2048
RET_STEPS = SEQ // RET_TILE
RET_CPT = RET_TILE // RCHUNK
RET_AHEAD = 4
N_RCHUNKS = 1 + SEQ // RCHUNK
FF_CHUNK = 1024
FFT_N1 = 64
FFT_N2 = 128
HALF = 128
FFT_STEP = 16
MIX_K2 = TM_LAT // FFT_N1

_VMEM_LIMIT = 56 * 1024 * 1024


def _dot(a, b):
    return jnp.dot(a, b, preferred_element_type=F32)


def _split(x):
    hi = x.astype(BF16)
    lo = (x - hi.astype(F32)).astype(BF16)
    return hi, lo


def _np_split(a64):
    hi = np.asarray(a64, np.float32).astype(BF16)
    lo = (np.asarray(a64, np.float32) - hi.astype(np.float32)).astype(BF16)
    return hi, lo


def _rms(x):
    return x * lax.rsqrt(jnp.mean(x * x, axis=-1, keepdims=True) + EPS)


def _gelu(x):
    return x * (0.5 * (1.0 + jnp.tanh(0.7978845608028654 * (x + 0.044715 * (x * x * x)))))


def _sigmoid(x):
    return 1.0 / (1.0 + jnp.exp(-x))


def _const_spec(shape, nargs=None):
    zeros = (0,) * len(shape)
    return pl.BlockSpec(shape, lambda *_: zeros, pipeline_mode=pl.Buffered(1))


def _layer_spec(shape, layer):
    idx = (layer,) + (0,) * len(shape)
    return pl.BlockSpec((None,) + tuple(shape), lambda *_: idx, pipeline_mode=pl.Buffered(1))


def _params(n_axes):
    return pltpu.CompilerParams(dimension_semantics=("arbitrary",) * n_axes, vmem_limit_bytes=_VMEM_LIMIT)


def _mod_kernel(a_ref, w_ref, b_ref, o_ref):
    a = a_ref[...]
    a = a * _sigmoid(a)
    ah, al = _split(a)
    w = w_ref[0].astype(BF16)
    o_ref[0] = _dot(ah, w) + _dot(al, w) + b_ref[0]


def _modulation(cond_rows, w_mod, b_mod):
    depth = w_mod.shape[0]
    tn = 1536
    return pl.pallas_call(
        _mod_kernel,
        grid=(depth, (6 * D_MODEL) // tn),
        in_specs=[
            pl.BlockSpec((8, D_MODEL), lambda l, j: (0, 0)),
            pl.BlockSpec((1, D_MODEL, tn), lambda l, j: (l, 0, j)),
            pl.BlockSpec((1, 1, tn), lambda l, j: (l, 0, j)),
        ],
        out_specs=pl.BlockSpec((1, 8, tn), lambda l, j: (l, 0, j)),
        out_shape=jax.ShapeDtypeStruct((depth, 8, 6 * D_MODEL), F32),
        compiler_params=_params(2),
        name="modulation",
    )(cond_rows, w_mod, b_mod.reshape(depth, 1, 6 * D_MODEL))


def _cast_plan(cast, grid):
    steps = int(np.prod(grid))
    in_specs, args, out_specs, out_shapes = [], [], [], []

    def step(*idx):
        t = idx[0]
        for n, i in zip(grid[1:], idx[1:]):
            t = t * n + i
        return t

    for stack, layer in cast:
        _, rows, cols = stack.shape
        n_chunks = rows // 16
        while n_chunks > steps:
            n_chunks //= 2
        assert rows % n_chunks == 0
        per = steps // n_chunks
        chunk = lambda *idx, per=per, last=n_chunks - 1: jnp.minimum(step(*idx) // per, last)
        in_specs.append(pl.BlockSpec((None, rows // n_chunks, cols),
                                     lambda *idx, layer=layer, chunk=chunk: (layer, chunk(*idx), 0)))
        out_specs.append(pl.BlockSpec((rows // n_chunks, cols), lambda *idx, chunk=chunk: (chunk(*idx), 0)))
        out_shapes.append(jax.ShapeDtypeStruct((rows, cols), BF16))
        args.append(stack)
    return in_specs, args, out_specs, out_shapes


def _cast_chunks(src_refs, dst_refs):
    for src, dst in zip(src_refs, dst_refs, strict=True):
        dst[...] = src[...].astype(BF16)


def _in_proj_kernel(*refs, rope, kv_only, n_cast):
    n_in = 8 if rope else 6
    ins, cast_src = refs[:n_in], refs[n_in:n_in + n_cast]
    outs, cast_dst = refs[n_in + n_cast:len(refs) - n_cast], refs[len(refs) - n_cast:]
    _cast_chunks(cast_src, cast_dst)
    if rope:
        x_ref, mod_ref, gpre_ref, w_ref, pavg_ref, gn_ref, rrow_ref, rcol_ref = ins
    else:
        x_ref, mod_ref, gpre_ref, w_ref, pavg_ref, gn_ref = ins
    tm = x_ref.shape[1]

    def prepare():
        shift = mod_ref[0, :, 0:D_MODEL]
        gain = gpre_ref[...] * (1.0 + mod_ref[0, :, D_MODEL:2 * D_MODEL])
        return (_rms(x_ref[0]) * gain + shift).astype(BF16)

    def project(hb):
        def proj(lo, hi):
            return _dot(hb, w_ref[:, lo:hi])

        if rope:
            n_rows = tm // GRID_W
            rot = []
            for t in range(2):
                by_row = jnp.concatenate(
                    [jnp.broadcast_to(rrow_ref[t, r:r + 1, :], (GRID_W, HEAD_DIM)) for r in range(n_rows)], axis=0)
                rot.append(by_row + jnp.concatenate([rcol_ref[t]] * n_rows, axis=0))

        def rope_store(z, dst):
            if not rope:
                dst[0] = z.astype(BF16)
                return
            cos, sin = rot
            for hd in range(HEADS):
                a = z[:, hd * HEAD_DIM:(hd + 1) * HEAD_DIM]
                r = a * cos + pltpu.roll(a, HEAD_DIM // 2, 1) * sin
                dst[0, :, hd * HEAD_DIM:(hd + 1) * HEAD_DIM] = r.astype(BF16)

        if kv_only:
            k_ref, v_ref = outs
        else:
            q_ref, k_ref, v_ref, sg_ref, f_ref, u_ref, vn_ref, gate_ref = outs
            rope_store(proj(0, RET_W), q_ref)
        rope_store(proj(RET_W, 2 * RET_W) * (HEAD_DIM ** -0.5), k_ref)
        v_ref[0] = proj(2 * RET_W, 3 * RET_W).astype(BF16)
        if kv_only:
            return
        g = proj(3 * RET_W, 4 * RET_W)
        sg_ref[0] = (g * _sigmoid(g)).astype(BF16)

        f = proj(COL_F, COL_U)
        for hf in range(2):
            fh = f[:, hf * HALF:(hf + 1) * HALF]
            if rope:
                for n1 in range(tm // FFT_N2):
                    for nj in range(FFT_N2 // FFT_STEP):
                        f_ref[0, hf, nj, n1 * FFT_STEP:(n1 + 1) * FFT_STEP, :] = (
                            fh[n1 * FFT_N2 + nj * FFT_STEP:n1 * FFT_N2 + (nj + 1) * FFT_STEP])
            else:
                f_ref[0, hf] = fh

        u_ref[0] = _gelu(proj(COL_U, COL_VS)).astype(BF16)
        vg = _gelu(proj(COL_VS, COL_GATE))
        sh, sl = _split(vg * vg)
        pavg = pavg_ref[...]
        ms = _dot(sh, pavg) + _dot(sl, pavg)
        vn_ref[0] = (vg * lax.rsqrt(ms + EPS) * gn_ref[...]).astype(BF16)

        for c in range(3):
            z = proj(COL_GATE + c * D_MODEL, COL_GATE + (c + 1) * D_MODEL)
            gate_ref[0, :, c * D_MODEL:(c + 1) * D_MODEL] = _sigmoid(z).astype(BF16)

    project(prepare())


def _mod_spec(layer, ctx):
    idx = (lambda b, i: (layer, 2, 0, 0)) if ctx else (lambda b, i: (layer, b, 0, 0))
    return pl.BlockSpec((None, 1, 1, 6 * D_MODEL), idx)


def _in_proj(xs, layer, mod, g_pre, w_in, pavg, g_norm, rope_tabs, *, tm, ctx, kv_only=False, cast=()):
    nb, rows, _ = xs.shape
    rope = rope_tabs is not None
    grid = (nb, rows // tm)
    tok_spec = lambda w: pl.BlockSpec((1, tm, w), lambda b, i: (b, i, 0))
    in_specs = [
        tok_spec(D_MODEL),
        _mod_spec(layer, ctx),
        _layer_spec((1, D_MODEL), layer),
        _const_spec((D_MODEL, IN_W)),
        _const_spec((GMLP_W, GMLP_W)),
        _layer_spec((1, GMLP_W), layer),
    ]
    args = [xs, mod, g_pre, w_in, pavg, g_norm]
    if rope:
        in_specs += [pl.BlockSpec((2, tm // GRID_W, HEAD_DIM), lambda b, i: (0, i, 0)),
                     _const_spec((2, GRID_W, HEAD_DIM))]
        args += list(rope_tabs)
    c_in_specs, c_args, c_out_specs, c_out_shapes = _cast_plan(cast, grid)
    in_specs += c_in_specs
    args += c_args
    bf = lambda w: jax.ShapeDtypeStruct((nb, rows, w), BF16)
    if kv_only:
        out_shapes = (bf(RET_W), bf(RET_W))
        out_specs = (tok_spec(RET_W), tok_spec(RET_W))
    else:
        if rope:
            n_nj = FFT_N2 // FFT_STEP
            f_shape = jax.ShapeDtypeStruct((nb, 2, n_nj, FFT_N1 * FFT_STEP, HALF), F32)
            f_spec = pl.BlockSpec((1, 2, n_nj, (tm // FFT_N2) * FFT_STEP, HALF), lambda b, i: (b, 0, 0, i, 0))
        else:
            f_shape = jax.ShapeDtypeStruct((nb, 2, rows, HALF), F32)
            f_spec = pl.BlockSpec((1, 2, tm, HALF), lambda b, i: (b, 0, i, 0))
        out_shapes = (
            bf(RET_W), bf(RET_W), bf(RET_W), bf(RET_W),
            f_shape,
            bf(GMLP_W), bf(GMLP_W),
            bf(3 * D_MODEL),
        )
        out_specs = (
            tok_spec(RET_W), tok_spec(RET_W), tok_spec(RET_W), tok_spec(RET_W),
            f_spec,
            tok_spec(GMLP_W), tok_spec(GMLP_W), tok_spec(3 * D_MODEL),
        )
    return pl.pallas_call(
        functools.partial(_in_proj_kernel, rope=rope, kv_only=kv_only, n_cast=len(cast)),
        grid=grid,
        in_specs=in_specs,
        out_specs=tuple(out_specs) + tuple(c_out_specs),
        out_shape=tuple(out_shapes) + tuple(c_out_shapes),
        compiler_params=_params(2),
        name="in_proj_ctx" if ctx else "in_proj",
    )(*args)


def _ret_kernel(*refs, ctx_out, n_cast):
    n_in, n_out = (9, 2) if ctx_out else (7, 1)
    ins, refs = refs[:n_in], refs[n_in:]
    cast_src, refs = refs[:n_cast], refs[n_cast:]
    outs, refs = refs[:n_out], refs[n_out:]
    cast_dst, scratch = refs[:n_cast], refs[n_cast:]
    _cast_chunks(cast_src, cast_dst)
    (logit_ref, ql_ref, kl_ref, vl_ref, sgl_ref, kc_ref, vc_ref) = ins[:7]
    if ctx_out:
        qc_ref, sgc_ref = ins[7:]
        ol_ref, oc_ref = outs
    else:
        (ol_ref,) = outs
    mask_ref, wf_ref, wb_ref, qf_ref, qb_ref, df_ref, db_ref, sf_ref, sb_ref, sball_ref = scratch
    b = pl.program_id(0)
    phase = pl.program_id(1)
    j = pl.program_id(2)

    @pl.when((b == 0) & (phase == 0) & (j == 0))
    def _():
        x = logit_ref[...]
        lg = -(jnp.maximum(-x, 0.0) + jnp.log(1.0 + jnp.exp(-jnp.abs(x))))
        row = lax.broadcasted_iota(jnp.int32, (RCHUNK, RCHUNK), 0).astype(F32)
        col = lax.broadcasted_iota(jnp.int32, (RCHUNK, RCHUNK), 1).astype(F32)
        diff = row - col
        pos = row[:, :HEAD_DIM]
        for hd in range(HEADS):
            lf = lg[hd:hd + 1, :]
            lb = lg[HEADS + hd:HEADS + hd + 1, :]
            mask_ref[hd] = jnp.where(diff >= 0.0, jnp.exp(lf * jnp.maximum(diff, 0.0)),
                                     jnp.exp(lb * jnp.maximum(-diff, 0.0)))
            wf_ref[hd] = jnp.exp(lf * (RCHUNK - 1.0 - pos))
            wb_ref[hd] = jnp.exp(lb * pos)
            qf_ref[hd] = jnp.exp(lf * (pos + 1.0))
            qb_ref[hd] = jnp.exp(lb * (RCHUNK - pos))
            df_ref[hd] = jnp.exp(jnp.broadcast_to(lf, (HEAD_DIM, HEAD_DIM)) * float(RCHUNK))
            db_ref[hd] = jnp.exp(jnp.broadcast_to(lb, (HEAD_DIM, HEAD_DIM)) * float(RCHUNK))

    def chunk_updates(hd, k_ref, v_ref, n_chunks, w_ref):
        cols = slice(hd * HEAD_DIM, (hd + 1) * HEAD_DIM)
        us = []
        for c in range(n_chunks):
            rows = slice(c * RCHUNK, (c + 1) * RCHUNK)
            kw = (k_ref[0, rows, cols].astype(F32) * w_ref[hd]).astype(BF16)
            us.append(lax.dot_general(kw, v_ref[0, rows, cols], (((0,), (0,)), ((), ())),
                                      preferred_element_type=F32))
        return us

    def backward_chunks(k_ref, v_ref, n_chunks, first_chunk):
        for hd in range(HEADS):
            us = chunk_updates(hd, k_ref, v_ref, n_chunks, wb_ref)
            s = sb_ref[hd]
            for c in reversed(range(n_chunks)):
                sball_ref[first_chunk + c, hd] = s.astype(BF16)
                s = db_ref[hd] * s + us[c]
            sb_ref[hd] = s

    def forward_chunks(q_ref, k_ref, v_ref, sg_ref, o_ref, n_chunks, first_chunk):
        starts = []
        for hd in range(HEADS):
            us = chunk_updates(hd, k_ref, v_ref, n_chunks, wf_ref)
            st = [sf_ref[hd]]
            for c in range(n_chunks):
                st.append(df_ref[hd] * st[-1] + us[c])
            sf_ref[hd] = st[-1]
            starts.append(st)
        if o_ref is None:
            return
        units = [(c, hd) for c in range(n_chunks) for hd in range(HEADS)]

        def scores(c, hd):
            rows = slice(c * RCHUNK, (c + 1) * RCHUNK)
            cols = slice(hd * HEAD_DIM, (hd + 1) * HEAD_DIM)
            return lax.dot_general(q_ref[0, rows, cols], k_ref[0, rows, cols], (((1,), (1,)), ((), ())),
                                   preferred_element_type=F32)

        ahead = min(RET_AHEAD, len(units))
        pending = [scores(*u) for u in units[:ahead]]
        for n, (c, hd) in enumerate(units):
            if n + ahead < len(units):
                pending.append(scores(*units[n + ahead]))
            sc = pending.pop(0)
            rows = slice(c * RCHUNK, (c + 1) * RCHUNK)
            cols = slice(hd * HEAD_DIM, (hd + 1) * HEAD_DIM)
            qf = q_ref[0, rows, cols].astype(F32)
            lhs = jnp.concatenate([(sc * mask_ref[hd]).astype(BF16),
                                   (qf * qf_ref[hd]).astype(BF16),
                                   (qf * qb_ref[hd]).astype(BF16)], axis=1)
            rhs = jnp.concatenate([v_ref[0, rows, cols], starts[hd][c].astype(BF16),
                                   sball_ref[first_chunk + c, hd]], axis=0)
            o = _dot(lhs, rhs)
            o_ref[0, rows, cols] = (_rms(o) * sg_ref[0, rows, cols].astype(F32)).astype(BF16)

    @pl.when(phase == 0)
    def _():
        @pl.when(j == 0)
        def _():
            sb_ref[...] = jnp.zeros_like(sb_ref)
            backward_chunks(kc_ref, vc_ref, 1, 0)

        backward_chunks(kl_ref, vl_ref, RET_CPT, 1 + (RET_STEPS - 1 - j) * RET_CPT)

    @pl.when(phase == 1)
    def _():
        @pl.when(j == 0)
        def _():
            sf_ref[...] = jnp.zeros_like(sf_ref)
            if ctx_out:
                forward_chunks(qc_ref, kc_ref, vc_ref, sgc_ref, oc_ref, 1, 0)
            else:
                forward_chunks(None, kc_ref, vc_ref, None, None, 1, 0)

        forward_chunks(ql_ref, kl_ref, vl_ref, sgl_ref, ol_ref, RET_CPT, 1 + j * RET_CPT)


def _retention(layer, logits, lat, ctx, *, ctx_out, cast=()):
    nb = lat[0].shape[0]
    kv_idx = lambda b, p, j: (b, jnp.where(p == 0, RET_STEPS - 1 - j, j), 0)
    q_idx = lambda b, p, j: (b, jnp.where(p == 0, 0, j), 0)
    c_idx = lambda b, p, j: (b, 0, 0)
    lat_blk = (1, RET_TILE, RET_W)
    ctx_blk = (1, CTX_LEN, RET_W)
    in_specs = [
        _layer_spec((2 * HEADS, 1), layer),
        pl.BlockSpec(lat_blk, q_idx), pl.BlockSpec(lat_blk, kv_idx), pl.BlockSpec(lat_blk, kv_idx),
        pl.BlockSpec(lat_blk, q_idx),
        pl.BlockSpec(ctx_blk, c_idx), pl.BlockSpec(ctx_blk, c_idx),
    ]
    if ctx_out:
        qc, kc, vc, sgc = ctx
        args = [logits, *lat, kc, vc, qc, sgc]
        in_specs += [pl.BlockSpec(ctx_blk, c_idx), pl.BlockSpec(ctx_blk, c_idx)]
        out_specs = (pl.BlockSpec(lat_blk, q_idx), pl.BlockSpec(ctx_blk, c_idx))
        out_shape = (jax.ShapeDtypeStruct((nb, SEQ, RET_W), BF16), jax.ShapeDtypeStruct((nb, CTX_LEN, RET_W), BF16))
    else:
        kc, vc = ctx
        args = [logits, *lat, kc, vc]
        out_specs = (pl.BlockSpec(lat_blk, q_idx),)
        out_shape = (jax.ShapeDtypeStruct((nb, SEQ, RET_W), BF16),)
    grid = (nb, 2, RET_STEPS)
    c_in_specs, c_args, c_out_specs, c_out_shapes = _cast_plan(cast, grid)
    state = pltpu.VMEM((HEADS, HEAD_DIM, HEAD_DIM), F32)
    pos_tab = pltpu.VMEM((HEADS, RCHUNK, HEAD_DIM), F32)
    return pl.pallas_call(
        functools.partial(_ret_kernel, ctx_out=ctx_out, n_cast=len(cast)),
        grid=grid,
        in_specs=in_specs + c_in_specs,
        out_specs=out_specs + tuple(c_out_specs),
        out_shape=out_shape + tuple(c_out_shapes),
        scratch_shapes=[
            pltpu.VMEM((HEADS, RCHUNK, RCHUNK), F32),
            pos_tab, pos_tab, pos_tab, pos_tab,
            state, state,
            state, state,
            pltpu.VMEM((N_RCHUNKS, HEADS, HEAD_DIM, HEAD_DIM), BF16),
        ],
        compiler_params=_params(3),
        name="retention",
    )(*args, *c_args)


def _dft_cos_sin(n):
    idx = np.arange(n, dtype=np.float64)
    ang = 2.0 * np.pi * ((idx[:, None] * idx[None, :]) % n) / n
    return np.cos(ang), np.sin(ang)


def _channel_dft(scale):
    c, s = _dft_cos_sin(FNET_GROUP)
    eye = np.eye(FNET_W // FNET_GROUP)
    return np.concatenate([np.kron(eye, c), np.kron(eye, s)], axis=0) * scale


def _fft_constants():
    k1 = np.arange(FFT_N1, dtype=np.float64)[None, :, None]
    n1 = np.arange(FFT_N1, dtype=np.float64)[None, None, :]
    n2 = np.arange(FFT_N2, dtype=np.float64)[:, None, None]
    ang = 2.0 * np.pi * ((k1 * (FFT_N2 * n1 + n2)) % SEQ) / SEQ
    m1 = np.concatenate([np.cos(ang), -np.sin(ang)], axis=1)
    c, s = _dft_cos_sin(FFT_N2)
    m3 = np.block([[c, s], [-s, c]])
    return (_np_split_cat(m1, -1), jnp.asarray(m3, F32),
            jnp.asarray(_channel_dft((SEQ * FNET_GROUP) ** -0.5), F32))


def _ctx_fft_constants():
    c, s = _dft_cos_sin(CTX_LEN)
    return (jnp.asarray(np.concatenate([c, -s], axis=0), F32),
            jnp.asarray(_channel_dft((CTX_LEN * FNET_GROUP) ** -0.5), F32))


def _np_split_cat(a64, axis):
    hi, lo = _np_split(a64)
    return jnp.asarray(np.concatenate([hi, lo], axis=axis))


def _twice(x, axis):
    return jnp.concatenate([x, x], axis=axis)


def _fft_kernel(*refs, n_cast):
    x_ref, m_ref, w_ref, cd_ref = refs[:4]
    o_ref, a_ref = refs[4 + n_cast], refs[-1]
    _cast_chunks(refs[4:4 + n_cast], refs[5 + n_cast:-1])
    _fft_body(x_ref, m_ref, w_ref, cd_ref, o_ref, a_ref)


def _fft_body(x_ref, m_ref, w_ref, cd_ref, o_ref, a_ref):
    n_kb = FFT_N1 // FFT_STEP

    def stage1(nj, carry):
        for jj in range(FFT_STEP):
            x = jnp.concatenate([x_ref[hf, nj, pl.ds(jj, FFT_N1, stride=FFT_STEP), :] for hf in range(2)],
                                axis=1).astype(BF16)
            a = _dot(m_ref[nj * FFT_STEP + jj], _twice(x, 0))
            row = pl.multiple_of((nj * FFT_STEP + jj) * FFT_STEP, FFT_STEP)
            for hf in range(2):
                for ri in range(2):
                    for kb in range(n_kb):
                        r0 = ri * FFT_N1 + kb * FFT_STEP
                        a_ref[hf, ri, kb, pl.ds(row, FFT_STEP), :] = a[r0:r0 + FFT_STEP, hf * HALF:(hf + 1) * HALF]
        return carry

    kb = pl.program_id(1)

    @pl.when(kb == 0)
    def _():
        lax.fori_loop(0, FFT_N2 // FFT_STEP, stage1, 0)

    w = w_ref[...].astype(BF16)
    zs = []
    for jj in range(FFT_STEP):
        rhs = jnp.concatenate(
            [jnp.concatenate([a_ref[hf, ri, kb, pl.ds(jj, FFT_N2, stride=FFT_STEP), :] for ri in range(2)], axis=0)
             for hf in range(2)], axis=1).astype(BF16)
        y = _dot(w, rhs)
        zs.append(jnp.concatenate([y[:FFT_N2], y[FFT_N2:]], axis=1).astype(BF16))
    z = jnp.concatenate(zs, axis=0)
    four = _dot(z, cd_ref[...].astype(BF16))
    for jj in range(FFT_STEP):
        for kt in range(FFT_N2 // MIX_K2):
            for hf in range(2):
                r0 = jj * FFT_N2 + kt * MIX_K2
                o_ref[hf, kt, jj * MIX_K2:(jj + 1) * MIX_K2, :] = four[r0:r0 + MIX_K2, hf * HALF:(hf + 1) * HALF]


def _fourier_latent(f_blk, cast=()):
    nb = f_blk.shape[0]
    m1, m3, cd = _fft_constants()
    n_nj, n_kb = FFT_N2 // FFT_STEP, FFT_N1 // FFT_STEP
    grid = (nb, n_kb)
    c_in_specs, c_args, c_out_specs, c_out_shapes = _cast_plan(cast, grid)
    return pl.pallas_call(
        functools.partial(_fft_kernel, n_cast=len(cast)),
        grid=grid,
        in_specs=[
            pl.BlockSpec((None, 2, n_nj, FFT_N1 * FFT_STEP, HALF), lambda b, j: (b, 0, 0, 0, 0)),
            _const_spec((FFT_N2, 2 * FFT_N1, 2 * FFT_N1)),
            _const_spec((2 * FFT_N2, 2 * FFT_N2)), _const_spec((2 * FNET_W, FNET_W)),
        ] + c_in_specs,
        out_specs=(pl.BlockSpec((None, 2, FFT_N2 // MIX_K2, FFT_STEP * MIX_K2, HALF), lambda b, j: (b, 0, 0, j, 0)),
                   *c_out_specs),
        out_shape=(jax.ShapeDtypeStruct((nb, 2, FFT_N2 // MIX_K2, FFT_N1 * MIX_K2, HALF), F32), *c_out_shapes),
        scratch_shapes=[pltpu.VMEM((2, 2, n_kb, FFT_N2 * FFT_STEP, HALF), F32)],
        compiler_params=_params(2),
        name="fourier",
    )(f_blk, m1, m3, cd, *c_args)


def _fft_ctx_kernel(x_ref, w_ref, cd_ref, o_ref):
    x = jnp.concatenate([x_ref[hf] for hf in range(2)], axis=1).astype(BF16)
    z = _dot(w_ref[...].astype(BF16), x)
    z = jnp.concatenate([z[:CTX_LEN], z[CTX_LEN:]], axis=1).astype(BF16)
    four = _dot(z, cd_ref[...].astype(BF16))
    for hf in range(2):
        o_ref[hf] = four[:, hf * HALF:(hf + 1) * HALF]


def _fourier_ctx(f_ctx):
    nb = f_ctx.shape[2] // CTX_LEN
    w, cd = _ctx_fft_constants()
    return pl.pallas_call(
        _fft_ctx_kernel,
        grid=(nb,),
        in_specs=[
            pl.BlockSpec((None, 2, CTX_LEN, HALF), lambda b: (0, 0, b, 0)),
            _const_spec((2 * CTX_LEN, CTX_LEN)), _const_spec((2 * FNET_W, FNET_W)),
        ],
        out_specs=pl.BlockSpec((None, 2, CTX_LEN, HALF), lambda b: (0, 0, b, 0)),
        out_shape=jax.ShapeDtypeStruct(f_ctx.shape, F32),
        compiler_params=_params(1),
        name="fourier_ctx",
    )(f_ctx, w, cd)


def _mix_mlp_kernel(*refs, tm, ctx, n_cast):
    o_ref = refs[18 + n_cast]
    _cast_chunks(refs[18:18 + n_cast], refs[19 + n_cast:])
    _mix_tile(refs[:7], refs[7:18], o_ref, tm=tm, ctx=ctx)


def _mix_mlp_merged_kernel(*refs, tm, tc, n_cast, n_lat):
    lat, cx, shared, refs = refs[:7], refs[7:14], refs[14:25], refs[25:]
    cast_src, (o_lat, o_ctx), cast_dst = refs[:n_cast], refs[n_cast:n_cast + 2], refs[n_cast + 2:]
    _cast_chunks(cast_src, cast_dst)
    t = pl.program_id(0)

    @pl.when(t < n_lat)
    def _():
        _mix_tile(lat, shared, o_lat, tm=tm, ctx=False)

    @pl.when(t == n_lat)
    def _():
        _mix_tile(cx, shared, o_ctx, tm=tc, ctx=True)


def _mix_tile(tile_refs, shared_refs, o_ref, *, tm, ctx):
    x_ref, ret_ref, four_ref, u_ref, vn_ref, gate_ref, mod_ref = tile_refs
    ws_ref, bs_ref, wa_ref, wb_ref, wc_ref, wo_ref, gpm_ref, gprm_ref, gpom_ref, wup_ref, wdn_ref = shared_refs
    group = lax.broadcasted_iota(jnp.int32, (GMLP_CHUNK, GMLP_W), 1) // GMLP_GROUP
    parts = []
    for cc in range(tm // GMLP_CHUNK):
        rows = slice(cc * GMLP_CHUNK, (cc + 1) * GMLP_CHUNK)
        vn = vn_ref[0, rows, :]
        s = bs_ref[...]
        for g in range(GMLP_W // GMLP_GROUP):
            s = s + jnp.where(group == g, _dot(ws_ref[g], vn), 0.0)
        parts.append((u_ref[0, rows, :].astype(F32) * s).astype(BF16))
    sgu = jnp.concatenate(parts, axis=0)

    if ctx:
        four = jnp.concatenate([four_ref[0, hf] for hf in range(2)], axis=1).astype(BF16)
    else:
        nk2 = tm // FFT_N1
        four = jnp.concatenate(
            [jnp.concatenate([four_ref[0, hf, pl.ds(k2, FFT_N1, stride=nk2), :] for k2 in range(nk2)], axis=0)
             for hf in range(2)], axis=1).astype(BF16)
    ga = gate_ref[0, :, 0:D_MODEL].astype(F32)
    gb = gate_ref[0, :, D_MODEL:2 * D_MODEL].astype(F32)
    gc = gate_ref[0, :, 2 * D_MODEL:3 * D_MODEL].astype(F32)
    merged = (ga * _dot(ret_ref[0], wa_ref[...]) + gb * _dot(four, wb_ref[...])
              + gc * _dot(sgu, wc_ref[...])).astype(BF16)

    gain1 = mod_ref[0, :, 2 * D_MODEL:3 * D_MODEL] * gpm_ref[...]
    shift2 = mod_ref[0, :, 3 * D_MODEL:4 * D_MODEL]
    gain2 = gprm_ref[...] * (1.0 + mod_ref[0, :, 4 * D_MODEL:5 * D_MODEL])
    gain3 = mod_ref[0, :, 5 * D_MODEL:6 * D_MODEL] * gpom_ref[...]

    halves = [slice(r * (tm // 2), (r + 1) * (tm // 2)) for r in range(2)]
    x1, h2 = [], []
    for rows in halves:
        y = _dot(merged[rows], wo_ref[...])
        x1.append(x_ref[0, rows, :] + _rms(y) * gain1)
        h2.append((_rms(x1[-1]) * gain2 + shift2).astype(BF16))
    up0 = jnp.concatenate([_dot(h, wup_ref[:, 0:FF_CHUNK]) for h in h2], axis=0)
    h2 = jnp.concatenate(h2, axis=0)

    n_ff = D_FF // FF_CHUNK
    m = None
    for c in range(n_ff - 1):
        cols = slice(c * FF_CHUNK, (c + 1) * FF_CHUNK)
        a = jnp.maximum(up0 if c == 0 else _dot(h2, wup_ref[:, cols]), 0.0)
        part = _dot((a * a).astype(BF16), wdn_ref[cols, :])
        m = part if m is None else m + part
    cols = slice((n_ff - 1) * FF_CHUNK, n_ff * FF_CHUNK)
    a = jnp.maximum(_dot(h2, wup_ref[:, cols]), 0.0)
    a = (a * a).astype(BF16)
    for r, rows in enumerate(halves):
        mr = m[rows] + _dot(a[rows], wdn_ref[cols, :])
        o_ref[0, rows, :] = x1[r] + _rms(mr) * gain3


def _mix_mlp_merged(lat, cx, layer, mod, ws, bs_tab, wa, wb, wc, wo,
                    g_post_mix, g_pre_mlp, g_post_mlp, w_up, w_down, *, tm, cast=()):
    nb, rows, _ = lat[0].shape
    tc = cx[0].shape[1]
    nt = rows // tm
    n_lat = nb * nt
    grid = (n_lat + 1,)
    tile = lambda t: jnp.minimum(t, n_lat - 1)
    lat_spec = lambda w: pl.BlockSpec((1, tm, w), lambda t: (tile(t) // nt, tile(t) % nt, 0))
    ctx_spec = lambda w: _const_spec((1, tc, w))
    lspec = lambda *shape: _layer_spec(shape, layer)
    mod_blk = (None, 1, 1, 6 * D_MODEL)
    c_in_specs, c_args, c_out_specs, c_out_shapes = _cast_plan(cast, grid)
    in_specs = [
        lat_spec(D_MODEL), lat_spec(RET_W),
        pl.BlockSpec((1, 2, None, tm, HALF), lambda t: (tile(t) // nt, 0, tile(t) % nt, 0, 0)),
        lat_spec(GMLP_W), lat_spec(GMLP_W), lat_spec(3 * D_MODEL),
        pl.BlockSpec(mod_blk, lambda t: (layer, tile(t) // nt, 0, 0)),
        ctx_spec(D_MODEL), ctx_spec(RET_W),
        _const_spec((1, 2, tc, HALF)),
        ctx_spec(GMLP_W), ctx_spec(GMLP_W), ctx_spec(3 * D_MODEL),
        pl.BlockSpec(mod_blk, lambda t: (layer, 2, 0, 0)),
        lspec(GMLP_W // GMLP_GROUP, GMLP_CHUNK, GMLP_CHUNK),
        lspec(GMLP_CHUNK, GMLP_W),
        _const_spec((RET_W, D_MODEL)), _const_spec((FNET_W, D_MODEL)), _const_spec((GMLP_W, D_MODEL)),
        _const_spec((D_MODEL, D_MODEL)),
        lspec(1, D_MODEL), lspec(1, D_MODEL), lspec(1, D_MODEL),
        _const_spec((D_MODEL, D_FF)), _const_spec((D_FF, D_MODEL)),
    ] + c_in_specs
    return pl.pallas_call(
        functools.partial(_mix_mlp_merged_kernel, tm=tm, tc=tc, n_cast=len(cast), n_lat=n_lat),
        grid=grid,
        in_specs=in_specs,
        out_specs=(lat_spec(D_MODEL), pl.BlockSpec((1, tc, D_MODEL), lambda t: (0, 0, 0)), *c_out_specs),
        out_shape=(jax.ShapeDtypeStruct((nb, rows, D_MODEL), F32), jax.ShapeDtypeStruct(cx[0].shape, F32),
                   *c_out_shapes),
        compiler_params=_params(1),
        name="mix_mlp",
    )(*lat, mod, *cx, mod, ws, bs_tab, wa, wb, wc, wo, g_post_mix, g_pre_mlp, g_post_mlp, w_up, w_down, *c_args)


def _mix_mlp(xs, ret, four, ug, vn, gates, layer, mod, ws, bs_tab, wa, wb, wc, wo,
             g_post_mix, g_pre_mlp, g_post_mlp, w_up, w_down, *, tm, ctx, cast=()):
    nb, rows, _ = xs.shape
    tok_spec = lambda w: pl.BlockSpec((1, tm, w), lambda b, i: (b, i, 0))
    lspec = lambda *shape: _layer_spec(shape, layer)
    if ctx:
        four_spec = pl.BlockSpec((1, 2, tm, HALF), lambda b, i: (b, 0, i, 0))
    else:
        four_spec = pl.BlockSpec((1, 2, None, tm, HALF), lambda b, i: (b, 0, i, 0, 0))
    c_in_specs, c_args, c_out_specs, c_out_shapes = _cast_plan(cast, (nb, rows // tm))
    outs = pl.pallas_call(
        functools.partial(_mix_mlp_kernel, tm=tm, ctx=ctx, n_cast=len(cast)),
        grid=(nb, rows // tm),
        in_specs=[
            tok_spec(D_MODEL), tok_spec(RET_W),
            four_spec,
            tok_spec(GMLP_W), tok_spec(GMLP_W), tok_spec(3 * D_MODEL),
            _mod_spec(layer, ctx),
            lspec(GMLP_W // GMLP_GROUP, GMLP_CHUNK, GMLP_CHUNK),
            lspec(GMLP_CHUNK, GMLP_W),
            _const_spec((RET_W, D_MODEL)), _const_spec((FNET_W, D_MODEL)), _const_spec((GMLP_W, D_MODEL)),
            _const_spec((D_MODEL, D_MODEL)),
            lspec(1, D_MODEL), lspec(1, D_MODEL), lspec(1, D_MODEL),
            _const_spec((D_MODEL, D_FF)), _const_spec((D_FF, D_MODEL)),
        ] + c_in_specs,
        out_specs=(tok_spec(D_MODEL),) + tuple(c_out_specs),
        out_shape=(jax.ShapeDtypeStruct((nb, rows, D_MODEL), F32),) + tuple(c_out_shapes),
        compiler_params=_params(2),
        name="mix_mlp_ctx" if ctx else "mix_mlp",
    )(xs, ret, four, ug, vn, gates, mod, ws, bs_tab, wa, wb, wc, wo,
      g_post_mix, g_pre_mlp, g_post_mlp, w_up, w_down, *c_args)
    return outs if cast else outs[0]


def _rope_tables():
    rows = SEQ // GRID_W
    freqs = ROPE_BASE ** (-jnp.arange(ROPE_FREQS, dtype=F32) / ROPE_FREQS)
    ang_r = jnp.arange(rows, dtype=F32)[:, None] * freqs
    ang_c = jnp.arange(GRID_W, dtype=F32)[:, None] * freqs

    def lanes(a, first):
        z = jnp.zeros_like(a)
        return jnp.concatenate([a, z] if first else [z, a], axis=-1)

    def pair(ang, first):
        cos, sin = lanes(jnp.cos(ang), first), lanes(jnp.sin(ang), first)
        return jnp.stack([jnp.concatenate([cos, cos], axis=-1), jnp.concatenate([-sin, sin], axis=-1)])

    return pair(ang_r, True), pair(ang_c, False)


def kernel(x, c, ctx, c_ctx, w_mod, b_mod, g_pre_mix, g_post_mix, g_pre_mlp, g_post_mlp, w_in,
           ret_decay_logit, sgu_w_s, sgu_b_s, sgu_norm, w_branch_a, w_branch_b, w_branch_c, w_out,
           w_up, w_down):
    nb = x.shape[0]
    depth = w_mod.shape[0]
    assert x.shape == (nb, SEQ, D_MODEL) and ctx.shape == (nb, CTX_LEN, D_MODEL) and nb == 2

    cond_rows = jnp.concatenate([c, c_ctx[None, :], jnp.zeros((8 - nb - 1, D_MODEL), F32)], axis=0)
    mod = _modulation(cond_rows, w_mod, b_mod).reshape(depth, 8, 1, 6 * D_MODEL)
    rope_tabs = _rope_tables()
    pavg = jnp.asarray(np.kron(np.eye(GMLP_W // GMLP_GROUP), np.full((GMLP_GROUP, GMLP_GROUP), 1.0 / GMLP_GROUP)), BF16)

    vec = lambda p: p[:, None, :]
    g_pre, g_norm = vec(g_pre_mix), vec(sgu_norm)
    logits = ret_decay_logit.reshape(depth, 2 * HEADS, 1)
    ws = sgu_w_s.astype(BF16)
    bs_tab = jnp.repeat(jnp.swapaxes(sgu_b_s, 1, 2), GMLP_GROUP, axis=2)
    gains = (vec(g_post_mix), vec(g_pre_mlp), vec(g_post_mlp))
    w_in_l = w_in[0].astype(BF16)
    ctx = ctx.reshape(1, nb * CTX_LEN, D_MODEL)
    for l in range(depth):
        last = l == depth - 1
        in_args = (l, mod, g_pre, w_in_l, pavg, g_norm)
        *proj, wa, wb, wc, wo, wup = _in_proj(
            x, *in_args, rope_tabs, tm=TM_IN, ctx=False,
            cast=[(w, l) for w in (w_branch_a, w_branch_b, w_branch_c, w_out, w_up)])
        q, k, v, sg, f_lat, ug, vn, gates = proj
        per_batch = lambda a: a.reshape(nb, CTX_LEN, RET_W)
        if last:
            kc, vc = _in_proj(ctx, *in_args, None, tm=nb * CTX_LEN, ctx=True, kv_only=True)
            (ret,) = _retention(l, logits, (q, k, v, sg), (per_batch(kc), per_batch(vc)), ctx_out=False)
        else:
            qc, kc, vc, sgc, f_ctx, ugc, vnc, gatesc = _in_proj(ctx, *in_args, None, tm=nb * CTX_LEN, ctx=True)
            ret, retc = _retention(l, logits, (q, k, v, sg), tuple(map(per_batch, (qc, kc, vc, sgc))), ctx_out=True)
            retc = retc.reshape(1, nb * CTX_LEN, RET_W)
        four, wdn = _fourier_latent(f_lat, cast=[(w_down, l)])
        mix_w = (l, mod, ws, bs_tab, wa, wb, wc, wo, *gains, wup, wdn)
        if last:
            x = _mix_mlp(x, ret, four, ug, vn, gates, *mix_w, tm=TM_LAT, ctx=False)
        else:
            x, ctx, w_in_l = _mix_mlp_merged(
                (x, ret, four, ug, vn, gates), (ctx, retc, _fourier_ctx(f_ctx), ugc, vnc, gatesc), *mix_w,
                tm=TM_LAT, cast=[(w_in, l + 1)])
    return x
```

```python
import functools

import numpy as np
import jax
import jax.numpy as jnp
from jax import lax
from jax.experimental import pallas as pl
from jax.experimental.pallas import tpu as pltpu

F32 = jnp.float32
BF16 = jnp.bfloat16

D_MODEL = 1024
SEQ = 8192
CTX_LEN = 256
GRID_W = 64
RET_W = 512
HEADS = 4
HEAD_DIM = 128
ROPE_BASE = 10000.0
ROPE_FREQS = HEAD_DIM // 4
FNET_W = 256
FNET_GROUP = 64
GMLP_W = 256
GMLP_GROUP = 64
GMLP_CHUNK = 128
D_FF = 4 * D_MODEL
EPS = 1e-6
IN_W = 4 * RET_W + FNET_W + 2 * GMLP_W + 3 * D_MODEL
COL_F = 4 * RET_W
COL_U = COL_F + FNET_W
COL_VS = COL_U + GMLP_W
COL_GATE = COL_VS + GMLP_W

TM_LAT = 512
TM_IN = 1024
RCHUNK = 256
RET_TILE = 2048
RET_STEPS = SEQ // RET_TILE
RET_CPT = RET_TILE // RCHUNK
RET_AHEAD = 4
N_RCHUNKS = 1 + SEQ // RCHUNK
FF_CHUNK = 1024
FFT_N1 = 64
FFT_N2 = 128
HALF = 128
FFT_STEP = 16
MIX_K2 = TM_LAT // FFT_N1

_VMEM_LIMIT = 56 * 1024 * 1024


def _dot(a, b):
    return jnp.dot(a, b, preferred_element_type=F32)


def _split(x):
    hi = x.astype(BF16)
    lo = (x - hi.astype(F32)).astype(BF16)
    return hi, lo


def _np_split(a64):
    hi = np.asarray(a64, np.float32).astype(BF16)
    lo = (np.asarray(a64, np.float32) - hi.astype(np.float32)).astype(BF16)
    return hi, lo


def _rms(x):
    return x * lax.rsqrt(jnp.mean(x * x, axis=-1, keepdims=True) + EPS)


def _gelu(x):
    return x * (0.5 * (1.0 + jnp.tanh(0.7978845608028654 * (x + 0.044715 * (x * x * x)))))


def _sigmoid(x):
    return 1.0 / (1.0 + jnp.exp(-x))


def _const_spec(shape, nargs=None):
    zeros = (0,) * len(shape)
    return pl.BlockSpec(shape, lambda *_: zeros, pipeline_mode=pl.Buffered(1))


def _layer_spec(shape, layer):
    idx = (layer,) + (0,) * len(shape)
    return pl.BlockSpec((None,) + tuple(shape), lambda *_: idx, pipeline_mode=pl.Buffered(1))


def _params(n_axes):
    return pltpu.CompilerParams(dimension_semantics=("arbitrary",) * n_axes, vmem_limit_bytes=_VMEM_LIMIT)


def _mod_kernel(a_ref, w_ref, b_ref, o_ref):
    a = a_ref[...]
    a = a * _sigmoid(a)
    ah, al = _split(a)
    w = w_ref[...].astype(BF16)
    o_ref[...] = _dot(ah, w) + _dot(al, w) + b_ref[...]


def _mod_plan(cond_rows, w_mod, b_mod, layer, grid):
    steps = int(np.prod(grid))
    n_chunks = max(n for n in (4, 8, 16, 24, 48) if n <= steps)
    tn = (6 * D_MODEL) // n_chunks

    def chunk(*idx):
        t = idx[0]
        for n, i in zip(grid[1:], idx[1:]):
            t = t * n + i
        return jnp.minimum(t, n_chunks - 1)

    in_specs = [
        _const_spec((8, D_MODEL)),
        pl.BlockSpec((None, D_MODEL, tn), lambda *idx: (layer, 0, chunk(*idx))),
        pl.BlockSpec((None, 1, tn), lambda *idx: (layer, 0, chunk(*idx))),
    ]
    args = [cond_rows, w_mod, b_mod.reshape(b_mod.shape[0], 1, 6 * D_MODEL)]
    return (in_specs, args, pl.BlockSpec((8, tn), lambda *idx: (0, chunk(*idx))),
            jax.ShapeDtypeStruct((8, 6 * D_MODEL), F32))


def _modulation(cond_rows, w_mod, b_mod, layer):
    grid = (4,)
    in_specs, args, out_spec, out_shape = _mod_plan(cond_rows, w_mod, b_mod, layer, grid)
    return pl.pallas_call(
        _mod_kernel, grid=grid, in_specs=in_specs, out_specs=out_spec, out_shape=out_shape,
        compiler_params=_params(1), name="modulation",
    )(*args)


def _cast_plan(cast, grid):
    steps = int(np.prod(grid))
    in_specs, args, out_specs, out_shapes = [], [], [], []

    def step(*idx):
        t = idx[0]
        for n, i in zip(grid[1:], idx[1:]):
            t = t * n + i
        return t

    for stack, layer in cast:
        _, rows, cols = stack.shape
        n_chunks = rows // 16
        while n_chunks > steps:
            n_chunks //= 2
        assert rows % n_chunks == 0
        per = steps // n_chunks
        chunk = lambda *idx, per=per, last=n_chunks - 1: jnp.minimum(step(*idx) // per, last)
        in_specs.append(pl.BlockSpec((None, rows // n_chunks, cols),
                                     lambda *idx, layer=layer, chunk=chunk: (layer, chunk(*idx), 0)))
        out_specs.append(pl.BlockSpec((rows // n_chunks, cols), lambda *idx, chunk=chunk: (chunk(*idx), 0)))
        out_shapes.append(jax.ShapeDtypeStruct((rows, cols), BF16))
        args.append(stack)
    return in_specs, args, out_specs, out_shapes


def _cast_chunks(src_refs, dst_refs):
    for src, dst in zip(src_refs, dst_refs, strict=True):
        dst[...] = src[...].astype(BF16)


def _in_proj_kernel(*refs, rope, kv_only, n_cast):
    n_in = 8 if rope else 6
    ins, cast_src = refs[:n_in], refs[n_in:n_in + n_cast]
    outs, cast_dst = refs[n_in + n_cast:len(refs) - n_cast], refs[len(refs) - n_cast:]
    _cast_chunks(cast_src, cast_dst)
    if rope:
        x_ref, mod_ref, gpre_ref, w_ref, pavg_ref, gn_ref, rrow_ref, rcol_ref = ins
    else:
        x_ref, mod_ref, gpre_ref, w_ref, pavg_ref, gn_ref = ins
    tm = x_ref.shape[1]

    def prepare():
        shift = mod_ref[0, :, 0:D_MODEL]
        gain = gpre_ref[...] * (1.0 + mod_ref[0, :, D_MODEL:2 * D_MODEL])
        return (_rms(x_ref[0]) * gain + shift).astype(BF16)

    def project(hb):
        def proj(lo, hi):
            return _dot(hb, w_ref[:, lo:hi])

        if rope:
            n_rows = tm // GRID_W
            rot = []
            for t in range(2):
                by_row = jnp.concatenate(
                    [jnp.broadcast_to(rrow_ref[t, r:r + 1, :], (GRID_W, HEAD_DIM)) for r in range(n_rows)], axis=0)
                rot.append(by_row + jnp.concatenate([rcol_ref[t]] * n_rows, axis=0))

        def rope_store(z, dst):
            if not rope:
                dst[0] = z.astype(BF16)
                return
            cos, sin = rot
            for hd in range(HEADS):
                a = z[:, hd * HEAD_DIM:(hd + 1) * HEAD_DIM]
                r = a * cos + pltpu.roll(a, HEAD_DIM // 2, 1) * sin
                dst[0, :, hd * HEAD_DIM:(hd + 1) * HEAD_DIM] = r.astype(BF16)

        if kv_only:
            k_ref, v_ref = outs
        else:
            q_ref, k_ref, v_ref, sg_ref, f_ref, u_ref, vn_ref, gate_ref = outs
            rope_store(proj(0, RET_W), q_ref)
        rope_store(proj(RET_W, 2 * RET_W) * (HEAD_DIM ** -0.5), k_ref)
        v_ref[0] = proj(2 * RET_W, 3 * RET_W).astype(BF16)
        if kv_only:
            return
        g = proj(3 * RET_W, 4 * RET_W)
        sg_ref[0] = (g * _sigmoid(g)).astype(BF16)

        f = proj(COL_F, COL_U)
        for hf in range(2):
            fh = f[:, hf * HALF:(hf + 1) * HALF]
            if rope:
                for n1 in range(tm // FFT_N2):
                    for nj in range(FFT_N2 // FFT_STEP):
                        f_ref[0, hf, nj, n1 * FFT_STEP:(n1 + 1) * FFT_STEP, :] = (
                            fh[n1 * FFT_N2 + nj * FFT_STEP:n1 * FFT_N2 + (nj + 1) * FFT_STEP])
            else:
                f_ref[0, hf] = fh

        u_ref[0] = _gelu(proj(COL_U, COL_VS)).astype(BF16)
        vg = _gelu(proj(COL_VS, COL_GATE))
        sh, sl = _split(vg * vg)
        pavg = pavg_ref[...]
        ms = _dot(sh, pavg) + _dot(sl, pavg)
        vn_ref[0] = (vg * lax.rsqrt(ms + EPS) * gn_ref[...]).astype(BF16)

        for c in range(3):
            z = proj(COL_GATE + c * D_MODEL, COL_GATE + (c + 1) * D_MODEL)
            gate_ref[0, :, c * D_MODEL:(c + 1) * D_MODEL] = _sigmoid(z).astype(BF16)

    project(prepare())


def _mod_spec(ctx):
    idx = (lambda b, i: (2, 0, 0)) if ctx else (lambda b, i: (b, 0, 0))
    return pl.BlockSpec((1, 1, 6 * D_MODEL), idx)


def _in_proj(xs, layer, mod, g_pre, w_in, pavg, g_norm, rope_tabs, *, tm, ctx, kv_only=False, cast=()):
    nb, rows, _ = xs.shape
    rope = rope_tabs is not None
    grid = (nb, rows // tm)
    tok_spec = lambda w: pl.BlockSpec((1, tm, w), lambda b, i: (b, i, 0))
    in_specs = [
        tok_spec(D_MODEL),
        _mod_spec(ctx),
        _layer_spec((1, D_MODEL), layer),
        _const_spec((D_MODEL, IN_W)),
        _const_spec((GMLP_W, GMLP_W)),
        _layer_spec((1, GMLP_W), layer),
    ]
    args = [xs, mod, g_pre, w_in, pavg, g_norm]
    if rope:
        in_specs += [pl.BlockSpec((2, tm // GRID_W, HEAD_DIM), lambda b, i: (0, i, 0)),
                     _const_spec((2, GRID_W, HEAD_DIM))]
        args += list(rope_tabs)
    c_in_specs, c_args, c_out_specs, c_out_shapes = _cast_plan(cast, grid)
    in_specs += c_in_specs
    args += c_args
    bf = lambda w: jax.ShapeDtypeStruct((nb, rows, w), BF16)
    if kv_only:
        out_shapes = (bf(RET_W), bf(RET_W))
        out_specs = (tok_spec(RET_W), tok_spec(RET_W))
    else:
        if rope:
            n_nj = FFT_N2 // FFT_STEP
            f_shape = jax.ShapeDtypeStruct((nb, 2, n_nj, FFT_N1 * FFT_STEP, HALF), F32)
            f_spec = pl.BlockSpec((1, 2, n_nj, (tm // FFT_N2) * FFT_STEP, HALF), lambda b, i: (b, 0, 0, i, 0))
        else:
            f_shape = jax.ShapeDtypeStruct((nb, 2, rows, HALF), F32)
            f_spec = pl.BlockSpec((1, 2, tm, HALF), lambda b, i: (b, 0, i, 0))
        out_shapes = (
            bf(RET_W), bf(RET_W), bf(RET_W), bf(RET_W),
            f_shape,
            bf(GMLP_W), bf(GMLP_W),
            bf(3 * D_MODEL),
        )
        out_specs = (
            tok_spec(RET_W), tok_spec(RET_W), tok_spec(RET_W), tok_spec(RET_W),
            f_spec,
            tok_spec(GMLP_W), tok_spec(GMLP_W), tok_spec(3 * D_MODEL),
        )
    return pl.pallas_call(
        functools.partial(_in_proj_kernel, rope=rope, kv_only=kv_only, n_cast=len(cast)),
        grid=grid,
        in_specs=in_specs,
        out_specs=tuple(out_specs) + tuple(c_out_specs),
        out_shape=tuple(out_shapes) + tuple(c_out_shapes),
        compiler_params=_params(2),
        name="in_proj_ctx" if ctx else "in_proj",
    )(*args)


def _ret_kernel(*refs, ctx_out, n_cast, mod_job):
    n_in, n_out = (9, 2) if ctx_out else (7, 1)
    n_mod = 3 if mod_job else 0
    ins, refs = refs[:n_in], refs[n_in:]
    cast_src, refs = refs[:n_cast], refs[n_cast:]
    mod_in, refs = refs[:n_mod], refs[n_mod:]
    outs, refs = refs[:n_out], refs[n_out:]
    cast_dst, refs = refs[:n_cast], refs[n_cast:]
    _cast_chunks(cast_src, cast_dst)
    if mod_job:
        _mod_kernel(*mod_in, refs[0])
        refs = refs[1:]
    scratch = refs
    (logit_ref, ql_ref, kl_ref, vl_ref, sgl_ref, kc_ref, vc_ref) = ins[:7]
    if ctx_out:
        qc_ref, sgc_ref = ins[7:]
        ol_ref, oc_ref = outs
    else:
        (ol_ref,) = outs
    mask_ref, wf_ref, wb_ref, qf_ref, qb_ref, df_ref, db_ref, sf_ref, sb_ref, sball_ref = scratch
    b = pl.program_id(0)
    phase = pl.program_id(1)
    j = pl.program_id(2)

    @pl.when((b == 0) & (phase == 0) & (j == 0))
    def _():
        x = logit_ref[...]
        lg = -(jnp.maximum(-x, 0.0) + jnp.log(1.0 + jnp.exp(-jnp.abs(x))))
        row = lax.broadcasted_iota(jnp.int32, (RCHUNK, RCHUNK), 0).astype(F32)
        col = lax.broadcasted_iota(jnp.int32, (RCHUNK, RCHUNK), 1).astype(F32)
        diff = row - col
        pos = row[:, :HEAD_DIM]
        for hd in range(HEADS):
            lf = lg[hd:hd + 1, :]
            lb = lg[HEADS + hd:HEADS + hd + 1, :]
            mask_ref[hd] = jnp.where(diff >= 0.0, jnp.exp(lf * jnp.maximum(diff, 0.0)),
                                     jnp.exp(lb * jnp.maximum(-diff, 0.0)))
            wf_ref[hd] = jnp.exp(lf * (RCHUNK - 1.0 - pos))
            wb_ref[hd] = jnp.exp(lb * pos)
            qf_ref[hd] = jnp.exp(lf * (pos + 1.0))
            qb_ref[hd] = jnp.exp(lb * (RCHUNK - pos))
            df_ref[hd] = jnp.exp(jnp.broadcast_to(lf, (HEAD_DIM, HEAD_DIM)) * float(RCHUNK))
            db_ref[hd] = jnp.exp(jnp.broadcast_to(lb, (HEAD_DIM, HEAD_DIM)) * float(RCHUNK))

    def chunk_updates(hd, k_ref, v_ref, n_chunks, w_ref):
        cols = slice(hd * HEAD_DIM, (hd + 1) * HEAD_DIM)
        us = []
        for c in range(n_chunks):
            rows = slice(c * RCHUNK, (c + 1) * RCHUNK)
            kw = (k_ref[0, rows, cols].astype(F32) * w_ref[hd]).astype(BF16)
            us.append(lax.dot_general(kw, v_ref[0, rows, cols], (((0,), (0,)), ((), ())),
                                      preferred_element_type=F32))
        return us

    def backward_chunks(k_ref, v_ref, n_chunks, first_chunk):
        for hd in range(HEADS):
            us = chunk_updates(hd, k_ref, v_ref, n_chunks, wb_ref)
            s = sb_ref[hd]
            for c in reversed(range(n_chunks)):
                sball_ref[first_chunk + c, hd] = s.astype(BF16)
                s = db_ref[hd] * s + us[c]
            sb_ref[hd] = s

    def forward_chunks(q_ref, k_ref, v_ref, sg_ref, o_ref, n_chunks, first_chunk):
        starts = []
        for hd in range(HEADS):
            us = chunk_updates(hd, k_ref, v_ref, n_chunks, wf_ref)
            st = [sf_ref[hd]]
            for c in range(n_chunks):
                st.append(df_ref[hd] * st[-1] + us[c])
            sf_ref[hd] = st[-1]
            starts.append(st)
        if o_ref is None:
            return
        units = [(c, hd) for c in range(n_chunks) for hd in range(HEADS)]

        def scores(c, hd):
            rows = slice(c * RCHUNK, (c + 1) * RCHUNK)
            cols = slice(hd * HEAD_DIM, (hd + 1) * HEAD_DIM)
            return lax.dot_general(q_ref[0, rows, cols], k_ref[0, rows, cols], (((1,), (1,)), ((), ())),
                                   preferred_element_type=F32)

        ahead = min(RET_AHEAD, len(units))
        pending = [scores(*u) for u in units[:ahead]]
        for n, (c, hd) in enumerate(units):
            if n + ahead < len(units):
                pending.append(scores(*units[n + ahead]))
            sc = pending.pop(0)
            rows = slice(c * RCHUNK, (c + 1) * RCHUNK)
            cols = slice(hd * HEAD_DIM, (hd + 1) * HEAD_DIM)
            qf = q_ref[0, rows, cols].astype(F32)
            lhs = jnp.concatenate([(sc * mask_ref[hd]).astype(BF16),
                                   (qf * qf_ref[hd]).astype(BF16),
                                   (qf * qb_ref[hd]).astype(BF16)], axis=1)
            rhs = jnp.concatenate([v_ref[0, rows, cols], starts[hd][c].astype(BF16),
                                   sball_ref[first_chunk + c, hd]], axis=0)
            o = _dot(lhs, rhs)
            o_ref[0, rows, cols] = (_rms(o) * sg_ref[0, rows, cols].astype(F32)).astype(BF16)

    @pl.when(phase == 0)
    def _():
        @pl.when(j == 0)
        def _():
            sb_ref[...] = jnp.zeros_like(sb_ref)
            backward_chunks(kc_ref, vc_ref, 1, 0)

        backward_chunks(kl_ref, vl_ref, RET_CPT, 1 + (RET_STEPS - 1 - j) * RET_CPT)

    @pl.when(phase == 1)
    def _():
        @pl.when(j == 0)
        def _():
            sf_ref[...] = jnp.zeros_like(sf_ref)
            if ctx_out:
                forward_chunks(qc_ref, kc_ref, vc_ref, sgc_ref, oc_ref, 1, 0)
            else:
                forward_chunks(None, kc_ref, vc_ref, None, None, 1, 0)

        forward_chunks(ql_ref, kl_ref, vl_ref, sgl_ref, ol_ref, RET_CPT, 1 + j * RET_CPT)


def _retention(layer, logits, lat, ctx, *, ctx_out, cast=(), mod_job=None):
    nb = lat[0].shape[0]
    kv_idx = lambda b, p, j: (b, jnp.where(p == 0, RET_STEPS - 1 - j, j), 0)
    q_idx = lambda b, p, j: (b, jnp.where(p == 0, 0, j), 0)
    c_idx = lambda b, p, j: (b, 0, 0)
    lat_blk = (1, RET_TILE, RET_W)
    ctx_blk = (1, CTX_LEN, RET_W)
    in_specs = [
        _layer_spec((2 * HEADS, 1), layer),
        pl.BlockSpec(lat_blk, q_idx), pl.BlockSpec(lat_blk, kv_idx), pl.BlockSpec(lat_blk, kv_idx),
        pl.BlockSpec(lat_blk, q_idx),
        pl.BlockSpec(ctx_blk, c_idx), pl.BlockSpec(ctx_blk, c_idx),
    ]
    if ctx_out:
        qc, kc, vc, sgc = ctx
        args = [logits, *lat, kc, vc, qc, sgc]
        in_specs += [pl.BlockSpec(ctx_blk, c_idx), pl.BlockSpec(ctx_blk, c_idx)]
        out_specs = (pl.BlockSpec(lat_blk, q_idx), pl.BlockSpec(ctx_blk, c_idx))
        out_shape = (jax.ShapeDtypeStruct((nb, SEQ, RET_W), BF16), jax.ShapeDtypeStruct((nb, CTX_LEN, RET_W), BF16))
    else:
        kc, vc = ctx
        args = [logits, *lat, kc, vc]
        out_specs = (pl.BlockSpec(lat_blk, q_idx),)
        out_shape = (jax.ShapeDtypeStruct((nb, SEQ, RET_W), BF16),)
    grid = (nb, 2, RET_STEPS)
    c_in_specs, c_args, c_out_specs, c_out_shapes = _cast_plan(cast, grid)
    if mod_job:
        m_in_specs, m_args, m_out_spec, m_out_shape = _mod_plan(*mod_job, grid)
        c_in_specs, c_args = c_in_specs + m_in_specs, c_args + m_args
        c_out_specs, c_out_shapes = c_out_specs + [m_out_spec], c_out_shapes + [m_out_shape]
    state = pltpu.VMEM((HEADS, HEAD_DIM, HEAD_DIM), F32)
    pos_tab = pltpu.VMEM((HEADS, RCHUNK, HEAD_DIM), F32)
    return pl.pallas_call(
        functools.partial(_ret_kernel, ctx_out=ctx_out, n_cast=len(cast), mod_job=bool(mod_job)),
        grid=grid,
        in_specs=in_specs + c_in_specs,
        out_specs=out_specs + tuple(c_out_specs),
        out_shape=out_shape + tuple(c_out_shapes),
        scratch_shapes=[
            pltpu.VMEM((HEADS, RCHUNK, RCHUNK), F32),
            pos_tab, pos_tab, pos_tab, pos_tab,
            state, state,
            state, state,
            pltpu.VMEM((N_RCHUNKS, HEADS, HEAD_DIM, HEAD_DIM), BF16),
        ],
        compiler_params=_params(3),
        name="retention",
    )(*args, *c_args)


def _dft_cos_sin(n):
    idx = np.arange(n, dtype=np.float64)
    ang = 2.0 * np.pi * ((idx[:, None] * idx[None, :]) % n) / n
    return np.cos(ang), np.sin(ang)


def _channel_dft(scale):
    c, s = _dft_cos_sin(FNET_GROUP)
    eye = np.eye(FNET_W // FNET_GROUP)
    return np.concatenate([np.kron(eye, c), np.kron(eye, s)], axis=0) * scale


def _fft_constants():
    k1 = np.arange(FFT_N1, dtype=np.float64)[None, :, None]
    n1 = np.arange(FFT_N1, dtype=np.float64)[None, None, :]
    n2 = np.arange(FFT_N2, dtype=np.float64)[:, None, None]
    ang = 2.0 * np.pi * ((k1 * (FFT_N2 * n1 + n2)) % SEQ) / SEQ
    m1 = np.concatenate([np.cos(ang), -np.sin(ang)], axis=1)
    c, s = _dft_cos_sin(FFT_N2)
    m3 = np.block([[c, s], [-s, c]])
    return (_np_split_cat(m1, -1), jnp.asarray(m3, F32),
            jnp.asarray(_channel_dft((SEQ * FNET_GROUP) ** -0.5), F32))


def _ctx_fft_constants():
    c, s = _dft_cos_sin(CTX_LEN)
    return (jnp.asarray(np.concatenate([c, -s], axis=0), F32),
            jnp.asarray(_channel_dft((CTX_LEN * FNET_GROUP) ** -0.5), F32))


def _np_split_cat(a64, axis):
    hi, lo = _np_split(a64)
    return jnp.asarray(np.concatenate([hi, lo], axis=axis))


def _twice(x, axis):
    return jnp.concatenate([x, x], axis=axis)


def _fft_kernel(*refs, n_cast):
    x_ref, m_ref, w_ref, cd_ref = refs[:4]
    o_ref, a_ref = refs[4 + n_cast], refs[-1]
    _cast_chunks(refs[4:4 + n_cast], refs[5 + n_cast:-1])
    _fft_body(x_ref, m_ref, w_ref, cd_ref, o_ref, a_ref)


def _fft_body(x_ref, m_ref, w_ref, cd_ref, o_ref, a_ref):
    n_kb = FFT_N1 // FFT_STEP

    def stage1(nj, carry):
        for jj in range(FFT_STEP):
            x = jnp.concatenate([x_ref[hf, nj, pl.ds(jj, FFT_N1, stride=FFT_STEP), :] for hf in range(2)],
                                axis=1).astype(BF16)
            a = _dot(m_ref[nj * FFT_STEP + jj], _twice(x, 0))
            row = pl.multiple_of((nj * FFT_STEP + jj) * FFT_STEP, FFT_STEP)
            for hf in range(2):
                for ri in range(2):
                    for kb in range(n_kb):
                        r0 = ri * FFT_N1 + kb * FFT_STEP
                        a_ref[hf, ri, kb, pl.ds(row, FFT_STEP), :] = a[r0:r0 + FFT_STEP, hf * HALF:(hf + 1) * HALF]
        return carry

    kb = pl.program_id(1)

    @pl.when(kb == 0)
    def _():
        lax.fori_loop(0, FFT_N2 // FFT_STEP, stage1, 0)

    w = w_ref[...].astype(BF16)
    zs = []
    for jj in range(FFT_STEP):
        rhs = jnp.concatenate(
            [jnp.concatenate([a_ref[hf, ri, kb, pl.ds(jj, FFT_N2, stride=FFT_STEP), :] for ri in range(2)], axis=0)
             for hf in range(2)], axis=1).astype(BF16)
        y = _dot(w, rhs)
        zs.append(jnp.concatenate([y[:FFT_N2], y[FFT_N2:]], axis=1).astype(BF16))
    z = jnp.concatenate(zs, axis=0)
    four = _dot(z, cd_ref[...].astype(BF16))
    for jj in range(FFT_STEP):
        for kt in range(FFT_N2 // MIX_K2):
            for hf in range(2):
                r0 = jj * FFT_N2 + kt * MIX_K2
                o_ref[hf, kt, jj * MIX_K2:(jj + 1) * MIX_K2, :] = four[r0:r0 + MIX_K2, hf * HALF:(hf + 1) * HALF]


def _fourier_latent(f_blk, cast=()):
    nb = f_blk.shape[0]
    m1, m3, cd = _fft_constants()
    n_nj, n_kb = FFT_N2 // FFT_STEP, FFT_N1 // FFT_STEP
    grid = (nb, n_kb)
    c_in_specs, c_args, c_out_specs, c_out_shapes = _cast_plan(cast, grid)
    return pl.pallas_call(
        functools.partial(_fft_kernel, n_cast=len(cast)),
        grid=grid,
        in_specs=[
            pl.BlockSpec((None, 2, n_nj, FFT_N1 * FFT_STEP, HALF), lambda b, j: (b, 0, 0, 0, 0)),
            _const_spec((FFT_N2, 2 * FFT_N1, 2 * FFT_N1)),
            _const_spec((2 * FFT_N2, 2 * FFT_N2)), _const_spec((2 * FNET_W, FNET_W)),
        ] + c_in_specs,
        out_specs=(pl.BlockSpec((None, 2, FFT_N2 // MIX_K2, FFT_STEP * MIX_K2, HALF), lambda b, j: (b, 0, 0, j, 0)),
                   *c_out_specs),
        out_shape=(jax.ShapeDtypeStruct((nb, 2, FFT_N2 // MIX_K2, FFT_N1 * MIX_K2, HALF), F32), *c_out_shapes),
        scratch_shapes=[pltpu.VMEM((2, 2, n_kb, FFT_N2 * FFT_STEP, HALF), F32)],
        compiler_params=_params(2),
        name="fourier",
    )(f_blk, m1, m3, cd, *c_args)


def _fft_ctx_kernel(x_ref, w_ref, cd_ref, o_ref):
    x = jnp.concatenate([x_ref[hf] for hf in range(2)], axis=1).astype(BF16)
    z = _dot(w_ref[...].astype(BF16), x)
    z = jnp.concatenate([z[:CTX_LEN], z[CTX_LEN:]], axis=1).astype(BF16)
    four = _dot(z, cd_ref[...].astype(BF16))
    for hf in range(2):
        o_ref[hf] = four[:, hf * HALF:(hf + 1) * HALF]


def _fourier_ctx(f_ctx):
    nb = f_ctx.shape[2] // CTX_LEN
    w, cd = _ctx_fft_constants()
    return pl.pallas_call(
        _fft_ctx_kernel,
        grid=(nb,),
        in_specs=[
            pl.BlockSpec((None, 2, CTX_LEN, HALF), lambda b: (0, 0, b, 0)),
            _const_spec((2 * CTX_LEN, CTX_LEN)), _const_spec((2 * FNET_W, FNET_W)),
        ],
        out_specs=pl.BlockSpec((None, 2, CTX_LEN, HALF), lambda b: (0, 0, b, 0)),
        out_shape=jax.ShapeDtypeStruct(f_ctx.shape, F32),
        compiler_params=_params(1),
        name="fourier_ctx",
    )(f_ctx, w, cd)


def _mix_mlp_kernel(*refs, tm, ctx, n_cast):
    o_ref = refs[18 + n_cast]
    _cast_chunks(refs[18:18 + n_cast], refs[19 + n_cast:])
    _mix_tile(refs[:7], refs[7:18], o_ref, tm=tm, ctx=ctx)


def _mix_mlp_merged_kernel(*refs, tm, tc, n_cast, n_lat):
    lat, cx, shared, refs = refs[:7], refs[7:14], refs[14:25], refs[25:]
    cast_src, (o_lat, o_ctx), cast_dst = refs[:n_cast], refs[n_cast:n_cast + 2], refs[n_cast + 2:]
    _cast_chunks(cast_src, cast_dst)
    t = pl.program_id(0)

    @pl.when(t < n_lat)
    def _():
        _mix_tile(lat, shared, o_lat, tm=tm, ctx=False)

    @pl.when(t == n_lat)
    def _():
        _mix_tile(cx, shared, o_ctx, tm=tc, ctx=True)


def _mix_tile(tile_refs, shared_refs, o_ref, *, tm, ctx):
    x_ref, ret_ref, four_ref, u_ref, vn_ref, gate_ref, mod_ref = tile_refs
    ws_ref, bs_ref, wa_ref, wb_ref, wc_ref, wo_ref, gpm_ref, gprm_ref, gpom_ref, wup_ref, wdn_ref = shared_refs
    group = lax.broadcasted_iota(jnp.int32, (GMLP_CHUNK, GMLP_W), 1) // GMLP_GROUP
    parts = []
    for cc in range(tm // GMLP_CHUNK):
        rows = slice(cc * GMLP_CHUNK, (cc + 1) * GMLP_CHUNK)
        vn = vn_ref[0, rows, :]
        s = bs_ref[...]
        for g in range(GMLP_W // GMLP_GROUP):
            s = s + jnp.where(group == g, _dot(ws_ref[g], vn), 0.0)
        parts.append((u_ref[0, rows, :].astype(F32) * s).astype(BF16))
    sgu = jnp.concatenate(parts, axis=0)

    if ctx:
        four = jnp.concatenate([four_ref[0, hf] for hf in range(2)], axis=1).astype(BF16)
    else:
        nk2 = tm // FFT_N1
        four = jnp.concatenate(
            [jnp.concatenate([four_ref[0, hf, pl.ds(k2, FFT_N1, stride=nk2), :] for k2 in range(nk2)], axis=0)
             for hf in range(2)], axis=1).astype(BF16)
    ga = gate_ref[0, :, 0:D_MODEL].astype(F32)
    gb = gate_ref[0, :, D_MODEL:2 * D_MODEL].astype(F32)
    gc = gate_ref[0, :, 2 * D_MODEL:3 * D_MODEL].astype(F32)
    merged = (ga * _dot(ret_ref[0], wa_ref[...]) + gb * _dot(four, wb_ref[...])
              + gc * _dot(sgu, wc_ref[...])).astype(BF16)

    gain1 = mod_ref[0, :, 2 * D_MODEL:3 * D_MODEL] * gpm_ref[...]
    shift2 = mod_ref[0, :, 3 * D_MODEL:4 * D_MODEL]
    gain2 = gprm_ref[...] * (1.0 + mod_ref[0, :, 4 * D_MODEL:5 * D_MODEL])
    gain3 = mod_ref[0, :, 5 * D_MODEL:6 * D_MODEL] * gpom_ref[...]

    halves = [slice(r * (tm // 2), (r + 1) * (tm // 2)) for r in range(2)]
    x1, h2 = [], []
    for rows in halves:
        y = _dot(merged[rows], wo_ref[...])
        x1.append(x_ref[0, rows, :] + _rms(y) * gain1)
        h2.append((_rms(x1[-1]) * gain2 + shift2).astype(BF16))
    up0 = jnp.concatenate([_dot(h, wup_ref[:, 0:FF_CHUNK]) for h in h2], axis=0)
    h2 = jnp.concatenate(h2, axis=0)

    n_ff = D_FF // FF_CHUNK
    m = None
    for c in range(n_ff - 1):
        cols = slice(c * FF_CHUNK, (c + 1) * FF_CHUNK)
        a = jnp.maximum(up0 if c == 0 else _dot(h2, wup_ref[:, cols]), 0.0)
        part = _dot((a * a).astype(BF16), wdn_ref[cols, :])
        m = part if m is None else m + part
    cols = slice((n_ff - 1) * FF_CHUNK, n_ff * FF_CHUNK)
    a = jnp.maximum(_dot(h2, wup_ref[:, cols]), 0.0)
    a = (a * a).astype(BF16)
    for r, rows in enumerate(halves):
        mr = m[rows] + _dot(a[rows], wdn_ref[cols, :])
        o_ref[0, rows, :] = x1[r] + _rms(mr) * gain3


def _mix_mlp_merged(lat, cx, layer, mod, ws, bs_tab, wa, wb, wc, wo,
                    g_post_mix, g_pre_mlp, g_post_mlp, w_up, w_down, *, tm, cast=()):
    nb, rows, _ = lat[0].shape
    tc = cx[0].shape[1]
    nt = rows // tm
    n_lat = nb * nt
    grid = (n_lat + 1,)
    tile = lambda t: jnp.minimum(t, n_lat - 1)
    lat_spec = lambda w: pl.BlockSpec((1, tm, w), lambda t: (tile(t) // nt, tile(t) % nt, 0))
    ctx_spec = lambda w: _const_spec((1, tc, w))
    lspec = lambda *shape: _layer_spec(shape, layer)
    mod_blk = (1, 1, 6 * D_MODEL)
    c_in_specs, c_args, c_out_specs, c_out_shapes = _cast_plan(cast, grid)
    in_specs = [
        lat_spec(D_MODEL), lat_spec(RET_W),
        pl.BlockSpec((1, 2, None, tm, HALF), lambda t: (tile(t) // nt, 0, tile(t) % nt, 0, 0)),
        lat_spec(GMLP_W), lat_spec(GMLP_W), lat_spec(3 * D_MODEL),
        pl.BlockSpec(mod_blk, lambda t: (tile(t) // nt, 0, 0)),
        ctx_spec(D_MODEL), ctx_spec(RET_W),
        _const_spec((1, 2, tc, HALF)),
        ctx_spec(GMLP_W), ctx_spec(GMLP_W), ctx_spec(3 * D_MODEL),
        pl.BlockSpec(mod_blk, lambda t: (2, 0, 0)),
        lspec(GMLP_W // GMLP_GROUP, GMLP_CHUNK, GMLP_CHUNK),
        lspec(GMLP_CHUNK, GMLP_W),
        _const_spec((RET_W, D_MODEL)), _const_spec((FNET_W, D_MODEL)), _const_spec((GMLP_W, D_MODEL)),
        _const_spec((D_MODEL, D_MODEL)),
        lspec(1, D_MODEL), lspec(1, D_MODEL), lspec(1, D_MODEL),
        _const_spec((D_MODEL, D_FF)), _const_spec((D_FF, D_MODEL)),
    ] + c_in_specs
    return pl.pallas_call(
        functools.partial(_mix_mlp_merged_kernel, tm=tm, tc=tc, n_cast=len(cast), n_lat=n_lat),
        grid=grid,
        in_specs=in_specs,
        out_specs=(lat_spec(D_MODEL), pl.BlockSpec((1, tc, D_MODEL), lambda t: (0, 0, 0)), *c_out_specs),
        out_shape=(jax.ShapeDtypeStruct((nb, rows, D_MODEL), F32), jax.ShapeDtypeStruct(cx[0].shape, F32),
                   *c_out_shapes),
        compiler_params=_params(1),
        name="mix_mlp",
    )(*lat, mod, *cx, mod, ws, bs_tab, wa, wb, wc, wo, g_post_mix, g_pre_mlp, g_post_mlp, w_up, w_down, *c_args)


def _mix_mlp(xs, ret, four, ug, vn, gates, layer, mod, ws, bs_tab, wa, wb, wc, wo,
             g_post_mix, g_pre_mlp, g_post_mlp, w_up, w_down, *, tm, ctx, cast=()):
    nb, rows, _ = xs.shape
    tok_spec = lambda w: pl.BlockSpec((1, tm, w), lambda b, i: (b, i, 0))
    lspec = lambda *shape: _layer_spec(shape, layer)
    if ctx:
        four_spec = pl.BlockSpec((1, 2, tm, HALF), lambda b, i: (b, 0, i, 0))
    else:
        four_spec = pl.BlockSpec((1, 2, None, tm, HALF), lambda b, i: (b, 0, i, 0, 0))
    c_in_specs, c_args, c_out_specs, c_out_shapes = _cast_plan(cast, (nb, rows // tm))
    outs = pl.pallas_call(
        functools.partial(_mix_mlp_kernel, tm=tm, ctx=ctx, n_cast=len(cast)),
        grid=(nb, rows // tm),
        in_specs=[
            tok_spec(D_MODEL), tok_spec(RET_W),
            four_spec,
            tok_spec(GMLP_W), tok_spec(GMLP_W), tok_spec(3 * D_MODEL),
            _mod_spec(ctx),
            lspec(GMLP_W // GMLP_GROUP, GMLP_CHUNK, GMLP_CHUNK),
            lspec(GMLP_CHUNK, GMLP_W),
            _const_spec((RET_W, D_MODEL)), _const_spec((FNET_W, D_MODEL)), _const_spec((GMLP_W, D_MODEL)),
            _const_spec((D_MODEL, D_MODEL)),
            lspec(1, D_MODEL), lspec(1, D_MODEL), lspec(1, D_MODEL),
            _const_spec((D_MODEL, D_FF)), _const_spec((D_FF, D_MODEL)),
        ] + c_in_specs,
        out_specs=(tok_spec(D_MODEL),) + tuple(c_out_specs),
        out_shape=(jax.ShapeDtypeStruct((nb, rows, D_MODEL), F32),) + tuple(c_out_shapes),
        compiler_params=_params(2),
        name="mix_mlp_ctx" if ctx else "mix_mlp",
    )(xs, ret, four, ug, vn, gates, mod, ws, bs_tab, wa, wb, wc, wo,
      g_post_mix, g_pre_mlp, g_post_mlp, w_up, w_down, *c_args)
    return outs if cast else outs[0]


def _rope_tables():
    rows = SEQ // GRID_W
    freqs = ROPE_BASE ** (-jnp.arange(ROPE_FREQS, dtype=F32) / ROPE_FREQS)
    ang_r = jnp.arange(rows, dtype=F32)[:, None] * freqs
    ang_c = jnp.arange(GRID_W, dtype=F32)[:, None] * freqs

    def lanes(a, first):
        z = jnp.zeros_like(a)
        return jnp.concatenate([a, z] if first else [z, a], axis=-1)

    def pair(ang, first):
        cos, sin = lanes(jnp.cos(ang), first), lanes(jnp.sin(ang), first)
        return jnp.stack([jnp.concatenate([cos, cos], axis=-1), jnp.concatenate([-sin, sin], axis=-1)])

    return pair(ang_r, True), pair(ang_c, False)


def kernel(x, c, ctx, c_ctx, w_mod, b_mod, g_pre_mix, g_post_mix, g_pre_mlp, g_post_mlp, w_in,
           ret_decay_logit, sgu_w_s, sgu_b_s, sgu_norm, w_branch_a, w_branch_b, w_branch_c, w_out,
           w_up, w_down):
    nb = x.shape[0]
    depth = w_mod.shape[0]
    assert x.shape == (nb, SEQ, D_MODEL) and ctx.shape == (nb, CTX_LEN, D_MODEL) and nb == 2

    cond_rows = jnp.concatenate([c, c_ctx[None, :], jnp.zeros((8 - nb - 1, D_MODEL), F32)], axis=0)
    mod = _modulation(cond_rows, w_mod, b_mod, 0).reshape(8, 1, 6 * D_MODEL)
    rope_tabs = _rope_tables()
    pavg = jnp.asarray(np.kron(np.eye(GMLP_W // GMLP_GROUP), np.full((GMLP_GROUP, GMLP_GROUP), 1.0 / GMLP_GROUP)), BF16)

    vec = lambda p: p[:, None, :]
    g_pre, g_norm = vec(g_pre_mix), vec(sgu_norm)
    logits = ret_decay_logit.reshape(depth, 2 * HEADS, 1)
    ws = sgu_w_s.astype(BF16)
    bs_tab = jnp.repeat(jnp.swapaxes(sgu_b_s, 1, 2), GMLP_GROUP, axis=2)
    gains = (vec(g_post_mix), vec(g_pre_mlp), vec(g_post_mlp))
    w_in_l = w_in[0].astype(BF16)
    ctx = ctx.reshape(1, nb * CTX_LEN, D_MODEL)
    for l in range(depth):
        last = l == depth - 1
        in_args = (l, mod, g_pre, w_in_l, pavg, g_norm)
        *proj, wa, wb, wc, wo, wup = _in_proj(
            x, *in_args, rope_tabs, tm=TM_IN, ctx=False,
            cast=[(w, l) for w in (w_branch_a, w_branch_b, w_branch_c, w_out, w_up)])
        q, k, v, sg, f_lat, ug, vn, gates = proj
        per_batch = lambda a: a.reshape(nb, CTX_LEN, RET_W)
        if last:
            kc, vc = _in_proj(ctx, *in_args, None, tm=nb * CTX_LEN, ctx=True, kv_only=True)
            (ret,) = _retention(l, logits, (q, k, v, sg), (per_batch(kc), per_batch(vc)), ctx_out=False)
        else:
            qc, kc, vc, sgc, f_ctx, ugc, vnc, gatesc = _in_proj(ctx, *in_args, None, tm=nb * CTX_LEN, ctx=True)
            ret, retc, mod_next = _retention(l, logits, (q, k, v, sg), tuple(map(per_batch, (qc, kc, vc, sgc))),
                                             ctx_out=True, mod_job=(cond_rows, w_mod, b_mod, l + 1))
            retc = retc.reshape(1, nb * CTX_LEN, RET_W)
        four, wdn = _fourier_latent(f_lat, cast=[(w_down, l)])
        mix_w = (l, mod, ws, bs_tab, wa, wb, wc, wo, *gains, wup, wdn)
        if last:
            x = _mix_mlp(x, ret, four, ug, vn, gates, *mix_w, tm=TM_LAT, ctx=False)
        else:
            x, ctx, w_in_l = _mix_mlp_merged(
                (x, ret, four, ug, vn, gates), (ctx, retc, _fourier_ctx(f_ctx), ugc, vnc, gatesc), *mix_w,
                tm=TM_LAT, cast=[(w_in, l + 1)])
            mod = mod_next.reshape(8, 1, 6 * D_MODEL)
    return x
```

```python
import functools

import numpy as np
import jax
import jax.numpy as jnp
from jax import lax
from jax.experimental import pallas as pl
from jax.experimental.pallas import tpu as pltpu

F32 = jnp.float32
BF16 = jnp.bfloat16

D_MODEL = 1024
SEQ = 8192
CTX_LEN = 256
GRID_W = 64
RET_W = 512
HEADS = 4
HEAD_DIM = 128
ROPE_BASE = 10000.0
ROPE_FREQS = HEAD_DIM // 4
FNET_W = 256
FNET_GROUP = 64
GMLP_W = 256
GMLP_GROUP = 64
GMLP_CHUNK = 128
D_FF = 4 * D_MODEL
EPS = 1e-6
IN_W = 4 * RET_W + FNET_W + 2 * GMLP_W + 3 * D_MODEL
COL_F = 4 * RET_W
COL_U = COL_F + FNET_W
COL_VS = COL_U + GMLP_W
COL_GATE = COL_VS + GMLP_W

TM_LAT = 512
TM_IN = 1024
RCHUNK = 256
RET_TILE = 2048
RET_STEPS = SEQ // RET_TILE
RET_CPT = RET_TILE // RCHUNK
RET_AHEAD = 4
N_RCHUNKS = 1 + SEQ // RCHUNK
FF_CHUNK = 1024
FFT_N1 = 64
FFT_N2 = 128
HALF = 128
FFT_STEP = 16
MIX_K2 = TM_LAT // FFT_N1

_VMEM_LIMIT = 56 * 1024 * 1024


def _dot(a, b):
    return jnp.dot(a, b, preferred_element_type=F32)


def _split(x):
    hi = x.astype(BF16)
    lo = (x - hi.astype(F32)).astype(BF16)
    return hi, lo


def _np_split(a64):
    hi = np.asarray(a64, np.float32).astype(BF16)
    lo = (np.asarray(a64, np.float32) - hi.astype(np.float32)).astype(BF16)
    return hi, lo


def _rms(x):
    return x * lax.rsqrt(jnp.mean(x * x, axis=-1, keepdims=True) + EPS)


def _gelu(x):
    return x * (0.5 * (1.0 + jnp.tanh(0.7978845608028654 * (x + 0.044715 * (x * x * x)))))


def _sigmoid(x):
    return 1.0 / (1.0 + jnp.exp(-x))


def _const_spec(shape):
    zeros = (0,) * len(shape)
    return pl.BlockSpec(shape, lambda *_: zeros, pipeline_mode=pl.Buffered(1))


def _layer_spec(shape, layer):
    idx = (layer,) + (0,) * len(shape)
    return pl.BlockSpec((None,) + tuple(shape), lambda *_: idx, pipeline_mode=pl.Buffered(1))


def _params(n_axes):
    return pltpu.CompilerParams(dimension_semantics=("arbitrary",) * n_axes, vmem_limit_bytes=_VMEM_LIMIT)


def _mod_kernel(a_ref, w_ref, b_ref, o_ref):
    a = a_ref[...]
    a = a * _sigmoid(a)
    ah, al = _split(a)
    w = w_ref[...].astype(BF16)
    o_ref[...] = _dot(ah, w) + _dot(al, w) + b_ref[...]


def _mod_plan(cond_rows, w_mod, b_mod, layer, grid):
    steps = int(np.prod(grid))
    n_chunks = max(n for n in (4, 8, 16, 24, 48) if n <= steps)
    tn = (6 * D_MODEL) // n_chunks

    def chunk(*idx):
        t = idx[0]
        for n, i in zip(grid[1:], idx[1:]):
            t = t * n + i
        return jnp.minimum(t, n_chunks - 1)

    in_specs = [
        _const_spec((8, D_MODEL)),
        pl.BlockSpec((None, D_MODEL, tn), lambda *idx: (layer, 0, chunk(*idx))),
        pl.BlockSpec((None, 1, tn), lambda *idx: (layer, 0, chunk(*idx))),
    ]
    args = [cond_rows, w_mod, b_mod.reshape(b_mod.shape[0], 1, 6 * D_MODEL)]
    return (in_specs, args, pl.BlockSpec((8, tn), lambda *idx: (0, chunk(*idx))),
            jax.ShapeDtypeStruct((8, 6 * D_MODEL), F32))


def _modulation(cond_rows, w_mod, b_mod, layer):
    grid = (4,)
    in_specs, args, out_spec, out_shape = _mod_plan(cond_rows, w_mod, b_mod, layer, grid)
    return pl.pallas_call(
        _mod_kernel, grid=grid, in_specs=in_specs, out_specs=out_spec, out_shape=out_shape,
        compiler_params=_params(1), name="modulation",
    )(*args)


def _cast_plan(cast, grid):
    steps = int(np.prod(grid))
    in_specs, args, out_specs, out_shapes = [], [], [], []

    def step(*idx):
        t = idx[0]
        for n, i in zip(grid[1:], idx[1:]):
            t = t * n + i
        return t

    for stack, layer in cast:
        _, rows, cols = stack.shape
        n_chunks = rows // 16
        while n_chunks > steps:
            n_chunks //= 2
        assert rows % n_chunks == 0
        per = steps // n_chunks
        chunk = lambda *idx, per=per, last=n_chunks - 1: jnp.minimum(step(*idx) // per, last)
        in_specs.append(pl.BlockSpec((None, rows // n_chunks, cols),
                                     lambda *idx, layer=layer, chunk=chunk: (layer, chunk(*idx), 0)))
        out_specs.append(pl.BlockSpec((rows // n_chunks, cols), lambda *idx, chunk=chunk: (chunk(*idx), 0)))
        out_shapes.append(jax.ShapeDtypeStruct((rows, cols), BF16))
        args.append(stack)
    return in_specs, args, out_specs, out_shapes


def _cast_chunks(src_refs, dst_refs):
    for src, dst in zip(src_refs, dst_refs, strict=True):
        dst[...] = src[...].astype(BF16)


def _in_proj_kernel(*refs, rope, kv_only, n_cast):
    n_in = 8 if rope else 6
    ins, cast_src = refs[:n_in], refs[n_in:n_in + n_cast]
    outs, cast_dst = refs[n_in + n_cast:len(refs) - n_cast], refs[len(refs) - n_cast:]
    _cast_chunks(cast_src, cast_dst)
    if rope:
        x_ref, mod_ref, gpre_ref, w_ref, pavg_ref, gn_ref, rrow_ref, rcol_ref = ins
    else:
        x_ref, mod_ref, gpre_ref, w_ref, pavg_ref, gn_ref = ins
    tm = x_ref.shape[1]

    def prepare():
        shift = mod_ref[0, :, 0:D_MODEL]
        gain = gpre_ref[...] * (1.0 + mod_ref[0, :, D_MODEL:2 * D_MODEL])
        return (_rms(x_ref[0]) * gain + shift).astype(BF16)

    def project(hb):
        def proj(lo, hi):
            return _dot(hb, w_ref[:, lo:hi])

        if rope:
            n_rows = tm // GRID_W
            rot = []
            for t in range(2):
                by_row = jnp.concatenate(
                    [jnp.broadcast_to(rrow_ref[t, r:r + 1, :], (GRID_W, HEAD_DIM)) for r in range(n_rows)], axis=0)
                rot.append(by_row + jnp.concatenate([rcol_ref[t]] * n_rows, axis=0))

        def rope_store(z, dst):
            if not rope:
                dst[0] = z.astype(BF16)
                return
            cos, sin = rot
            for hd in range(HEADS):
                a = z[:, hd * HEAD_DIM:(hd + 1) * HEAD_DIM]
                r = a * cos + pltpu.roll(a, HEAD_DIM // 2, 1) * sin
                dst[0, :, hd * HEAD_DIM:(hd + 1) * HEAD_DIM] = r.astype(BF16)

        if kv_only:
            k_ref, v_ref = outs
        else:
            q_ref, k_ref, v_ref, sg_ref, f_ref, u_ref, vn_ref, gate_ref = outs
            rope_store(proj(0, RET_W), q_ref)
        rope_store(proj(RET_W, 2 * RET_W) * (HEAD_DIM ** -0.5), k_ref)
        v_ref[0] = proj(2 * RET_W, 3 * RET_W).astype(BF16)
        if kv_only:
            return
        g = proj(3 * RET_W, 4 * RET_W)
        sg_ref[0] = (g * _sigmoid(g)).astype(BF16)

        f = proj(COL_F, COL_U)
        for hf in range(2):
            fh = f[:, hf * HALF:(hf + 1) * HALF]
            if rope:
                for n1 in range(tm // FFT_N2):
                    for nj in range(FFT_N2 // FFT_STEP):
                        f_ref[0, hf, nj, n1 * FFT_STEP:(n1 + 1) * FFT_STEP, :] = (
                            fh[n1 * FFT_N2 + nj * FFT_STEP:n1 * FFT_N2 + (nj + 1) * FFT_STEP])
            else:
                f_ref[0, hf] = fh

        u_ref[0] = _gelu(proj(COL_U, COL_VS)).astype(BF16)
        vg = _gelu(proj(COL_VS, COL_GATE))
        sh, sl = _split(vg * vg)
        pavg = pavg_ref[...]
        ms = _dot(sh, pavg) + _dot(sl, pavg)
        vn_ref[0] = (vg * lax.rsqrt(ms + EPS) * gn_ref[...]).astype(BF16)

        for c in range(3):
            z = proj(COL_GATE + c * D_MODEL, COL_GATE + (c + 1) * D_MODEL)
            gate_ref[0, :, c * D_MODEL:(c + 1) * D_MODEL] = _sigmoid(z).astype(BF16)

    project(prepare())


def _mod_spec(ctx):
    idx = (lambda b, i: (2, 0, 0)) if ctx else (lambda b, i: (b, 0, 0))
    return pl.BlockSpec((1, 1, 6 * D_MODEL), idx)


def _in_proj(xs, layer, mod, g_pre, w_in, pavg, g_norm, rope_tabs, *, tm, ctx, kv_only=False, cast=()):
    nb, rows, _ = xs.shape
    rope = rope_tabs is not None
    grid = (nb, rows // tm)
    tok_spec = lambda w: pl.BlockSpec((1, tm, w), lambda b, i: (b, i, 0))
    in_specs = [
        tok_spec(D_MODEL),
        _mod_spec(ctx),
        _layer_spec((1, D_MODEL), layer),
        _const_spec((D_MODEL, IN_W)),
        _const_spec((GMLP_W, GMLP_W)),
        _layer_spec((1, GMLP_W), layer),
    ]
    args = [xs, mod, g_pre, w_in, pavg, g_norm]
    if rope:
        in_specs += [pl.BlockSpec((2, tm // GRID_W, HEAD_DIM), lambda b, i: (0, i, 0)),
                     _const_spec((2, GRID_W, HEAD_DIM))]
        args += list(rope_tabs)
    c_in_specs, c_args, c_out_specs, c_out_shapes = _cast_plan(cast, grid)
    in_specs += c_in_specs
    args += c_args
    bf = lambda w: jax.ShapeDtypeStruct((nb, rows, w), BF16)
    if kv_only:
        out_shapes = (bf(RET_W), bf(RET_W))
        out_specs = (tok_spec(RET_W), tok_spec(RET_W))
    else:
        if rope:
            n_nj = FFT_N2 // FFT_STEP
            f_shape = jax.ShapeDtypeStruct((nb, 2, n_nj, FFT_N1 * FFT_STEP, HALF), F32)
            f_spec = pl.BlockSpec((1, 2, n_nj, (tm // FFT_N2) * FFT_STEP, HALF), lambda b, i: (b, 0, 0, i, 0))
        else:
            f_shape = jax.ShapeDtypeStruct((nb, 2, rows, HALF), F32)
            f_spec = pl.BlockSpec((1, 2, tm, HALF), lambda b, i: (b, 0, i, 0))
        out_shapes = (
            bf(RET_W), bf(RET_W), bf(RET_W), bf(RET_W),
            f_shape,
            bf(GMLP_W), bf(GMLP_W),
            bf(3 * D_MODEL),
        )
        out_specs = (
            tok_spec(RET_W), tok_spec(RET_W), tok_spec(RET_W), tok_spec(RET_W),
            f_spec,
            tok_spec(GMLP_W), tok_spec(GMLP_W), tok_spec(3 * D_MODEL),
        )
    return pl.pallas_call(
        functools.partial(_in_proj_kernel, rope=rope, kv_only=kv_only, n_cast=len(cast)),
        grid=grid,
        in_specs=in_specs,
        out_specs=tuple(out_specs) + tuple(c_out_specs),
        out_shape=tuple(out_shapes) + tuple(c_out_shapes),
        compiler_params=_params(2),
        name="in_proj_ctx" if ctx else "in_proj",
    )(*args)


def _ret_kernel(*refs, ctx_out, n_cast, mod_job):
    n_in, n_out = (9, 2) if ctx_out else (7, 1)
    n_mod = 3 if mod_job else 0
    ins, refs = refs[:n_in], refs[n_in:]
    cast_src, refs = refs[:n_cast], refs[n_cast:]
    mod_in, refs = refs[:n_mod], refs[n_mod:]
    outs, refs = refs[:n_out], refs[n_out:]
    cast_dst, refs = refs[:n_cast], refs[n_cast:]
    _cast_chunks(cast_src, cast_dst)
    if mod_job:
        _mod_kernel(*mod_in, refs[0])
        refs = refs[1:]
    scratch = refs
    (logit_ref, ql_ref, kl_ref, vl_ref, sgl_ref, kc_ref, vc_ref) = ins[:7]
    if ctx_out:
        qc_ref, sgc_ref = ins[7:]
        ol_ref, oc_ref = outs
    else:
        (ol_ref,) = outs
    mask_ref, wf_ref, wb_ref, qf_ref, qb_ref, df_ref, db_ref, sf_ref, sb_ref, sball_ref = scratch
    b = pl.program_id(0)
    phase = pl.program_id(1)
    j = pl.program_id(2)

    @pl.when((b == 0) & (phase == 0) & (j == 0))
    def _():
        x = logit_ref[...]
        lg = -(jnp.maximum(-x, 0.0) + jnp.log(1.0 + jnp.exp(-jnp.abs(x))))
        row = lax.broadcasted_iota(jnp.int32, (RCHUNK, RCHUNK), 0).astype(F32)
        col = lax.broadcasted_iota(jnp.int32, (RCHUNK, RCHUNK), 1).astype(F32)
        diff = row - col
        pos = row[:, :HEAD_DIM]
        for hd in range(HEADS):
            lf = lg[hd:hd + 1, :]
            lb = lg[HEADS + hd:HEADS + hd + 1, :]
            mask_ref[hd] = jnp.where(diff >= 0.0, jnp.exp(lf * jnp.maximum(diff, 0.0)),
                                     jnp.exp(lb * jnp.maximum(-diff, 0.0)))
            wf_ref[hd] = jnp.exp(lf * (RCHUNK - 1.0 - pos))
            wb_ref[hd] = jnp.exp(lb * pos)
            qf_ref[hd] = jnp.exp(lf * (pos + 1.0))
            qb_ref[hd] = jnp.exp(lb * (RCHUNK - pos))
            df_ref[hd] = jnp.exp(jnp.broadcast_to(lf, (HEAD_DIM, HEAD_DIM)) * float(RCHUNK))
            db_ref[hd] = jnp.exp(jnp.broadcast_to(lb, (HEAD_DIM, HEAD_DIM)) * float(RCHUNK))

    def chunk_updates(hd, k_ref, v_ref, n_chunks, w_ref):
        cols = slice(hd * HEAD_DIM, (hd + 1) * HEAD_DIM)
        us = []
        for c in range(n_chunks):
            rows = slice(c * RCHUNK, (c + 1) * RCHUNK)
            kw = (k_ref[0, rows, cols].astype(F32) * w_ref[hd]).astype(BF16)
            us.append(lax.dot_general(kw, v_ref[0, rows, cols], (((0,), (0,)), ((), ())),
                                      preferred_element_type=F32))
        return us

    def backward_chunks(k_ref, v_ref, n_chunks, first_chunk):
        for hd in range(HEADS):
            us = chunk_updates(hd, k_ref, v_ref, n_chunks, wb_ref)
            s = sb_ref[hd]
            for c in reversed(range(n_chunks)):
                sball_ref[first_chunk + c, hd] = s.astype(BF16)
                s = db_ref[hd] * s + us[c]
            sb_ref[hd] = s

    def forward_chunks(q_ref, k_ref, v_ref, sg_ref, o_ref, n_chunks, first_chunk):
        starts = []
        for hd in range(HEADS):
            us = chunk_updates(hd, k_ref, v_ref, n_chunks, wf_ref)
            st = [sf_ref[hd]]
            for c in range(n_chunks):
                st.append(df_ref[hd] * st[-1] + us[c])
            sf_ref[hd] = st[-1]
            starts.append(st)
        if o_ref is None:
            return
        units = [(c, hd) for c in range(n_chunks) for hd in range(HEADS)]

        def scores(c, hd):
            rows = slice(c * RCHUNK, (c + 1) * RCHUNK)
            cols = slice(hd * HEAD_DIM, (hd + 1) * HEAD_DIM)
            return lax.dot_general(q_ref[0, rows, cols], k_ref[0, rows, cols], (((1,), (1,)), ((), ())),
                                   preferred_element_type=F32)

        ahead = min(RET_AHEAD, len(units))
        pending = [scores(*u) for u in units[:ahead]]
        for n, (c, hd) in enumerate(units):
            if n + ahead < len(units):
                pending.append(scores(*units[n + ahead]))
            sc = pending.pop(0)
            rows = slice(c * RCHUNK, (c + 1) * RCHUNK)
            cols = slice(hd * HEAD_DIM, (hd + 1) * HEAD_DIM)
            qf = q_ref[0, rows, cols].astype(F32)
            lhs = jnp.concatenate([(sc * mask_ref[hd]).astype(BF16),
                                   (qf * qf_ref[hd]).astype(BF16),
                                   (qf * qb_ref[hd]).astype(BF16)], axis=1)
            rhs = jnp.concatenate([v_ref[0, rows, cols], starts[hd][c].astype(BF16),
                                   sball_ref[first_chunk + c, hd]], axis=0)
            o = _dot(lhs, rhs)
            o_ref[0, rows, cols] = (_rms(o) * sg_ref[0, rows, cols].astype(F32)).astype(BF16)

    @pl.when(phase == 0)
    def _():
        @pl.when(j == 0)
        def _():
            sb_ref[...] = jnp.zeros_like(sb_ref)
            backward_chunks(kc_ref, vc_ref, 1, 0)

        backward_chunks(kl_ref, vl_ref, RET_CPT, 1 + (RET_STEPS - 1 - j) * RET_CPT)

    @pl.when(phase == 1)
    def _():
        @pl.when(j == 0)
        def _():
            sf_ref[...] = jnp.zeros_like(sf_ref)
            if ctx_out:
                forward_chunks(qc_ref, kc_ref, vc_ref, sgc_ref, oc_ref, 1, 0)
            else:
                forward_chunks(None, kc_ref, vc_ref, None, None, 1, 0)

        forward_chunks(ql_ref, kl_ref, vl_ref, sgl_ref, ol_ref, RET_CPT, 1 + j * RET_CPT)


def _retention(layer, logits, lat, ctx, *, ctx_out, cast=(), mod_job=None):
    nb = lat[0].shape[0]
    kv_idx = lambda b, p, j: (b, jnp.where(p == 0, RET_STEPS - 1 - j, j), 0)
    q_idx = lambda b, p, j: (b, jnp.where(p == 0, 0, j), 0)
    c_idx = lambda b, p, j: (b, 0, 0)
    lat_blk = (1, RET_TILE, RET_W)
    ctx_blk = (1, CTX_LEN, RET_W)
    in_specs = [
        _layer_spec((2 * HEADS, 1), layer),
        pl.BlockSpec(lat_blk, q_idx), pl.BlockSpec(lat_blk, kv_idx), pl.BlockSpec(lat_blk, kv_idx),
        pl.BlockSpec(lat_blk, q_idx),
        pl.BlockSpec(ctx_blk, c_idx), pl.BlockSpec(ctx_blk, c_idx),
    ]
    if ctx_out:
        qc, kc, vc, sgc = ctx
        args = [logits, *lat, kc, vc, qc, sgc]
        in_specs += [pl.BlockSpec(ctx_blk, c_idx), pl.BlockSpec(ctx_blk, c_idx)]
        out_specs = (pl.BlockSpec(lat_blk, q_idx), pl.BlockSpec(ctx_blk, c_idx))
        out_shape = (jax.ShapeDtypeStruct((nb, SEQ, RET_W), BF16), jax.ShapeDtypeStruct((nb, CTX_LEN, RET_W), BF16))
    else:
        kc, vc = ctx
        args = [logits, *lat, kc, vc]
        out_specs = (pl.BlockSpec(lat_blk, q_idx),)
        out_shape = (jax.ShapeDtypeStruct((nb, SEQ, RET_W), BF16),)
    grid = (nb, 2, RET_STEPS)
    c_in_specs, c_args, c_out_specs, c_out_shapes = _cast_plan(cast, grid)
    if mod_job:
        m_in_specs, m_args, m_out_spec, m_out_shape = _mod_plan(*mod_job, grid)
        c_in_specs, c_args = c_in_specs + m_in_specs, c_args + m_args
        c_out_specs, c_out_shapes = c_out_specs + [m_out_spec], c_out_shapes + [m_out_shape]
    state = pltpu.VMEM((HEADS, HEAD_DIM, HEAD_DIM), F32)
    pos_tab = pltpu.VMEM((HEADS, RCHUNK, HEAD_DIM), F32)
    return pl.pallas_call(
        functools.partial(_ret_kernel, ctx_out=ctx_out, n_cast=len(cast), mod_job=bool(mod_job)),
        grid=grid,
        in_specs=in_specs + c_in_specs,
        out_specs=out_specs + tuple(c_out_specs),
        out_shape=out_shape + tuple(c_out_shapes),
        scratch_shapes=[
            pltpu.VMEM((HEADS, RCHUNK, RCHUNK), F32),
            pos_tab, pos_tab, pos_tab, pos_tab,
            state, state,
            state, state,
            pltpu.VMEM((N_RCHUNKS, HEADS, HEAD_DIM, HEAD_DIM), BF16),
        ],
        compiler_params=_params(3),
        name="retention",
    )(*args, *c_args)


def _dft_cos_sin(n):
    idx = np.arange(n, dtype=np.float64)
    ang = 2.0 * np.pi * ((idx[:, None] * idx[None, :]) % n) / n
    return np.cos(ang), np.sin(ang)


def _channel_dft(scale):
    c, s = _dft_cos_sin(FNET_GROUP)
    eye = np.eye(FNET_W // FNET_GROUP)
    return np.concatenate([np.kron(eye, c), np.kron(eye, s)], axis=0) * scale


def _fft_constants():
    k1 = np.arange(FFT_N1, dtype=np.float64)[None, :, None]
    n1 = np.arange(FFT_N1, dtype=np.float64)[None, None, :]
    n2 = np.arange(FFT_N2, dtype=np.float64)[:, None, None]
    ang = 2.0 * np.pi * ((k1 * (FFT_N2 * n1 + n2)) % SEQ) / SEQ
    m1 = np.concatenate([np.cos(ang), -np.sin(ang)], axis=1)
    c, s = _dft_cos_sin(FFT_N2)
    m3 = np.block([[c, s], [-s, c]])
    return (_np_split_cat(m1, -1), jnp.asarray(m3, F32),
            jnp.asarray(_channel_dft((SEQ * FNET_GROUP) ** -0.5), F32))


def _ctx_fft_constants():
    c, s = _dft_cos_sin(CTX_LEN)
    return (jnp.asarray(np.concatenate([c, -s], axis=0), F32),
            jnp.asarray(_channel_dft((CTX_LEN * FNET_GROUP) ** -0.5), F32))


def _np_split_cat(a64, axis):
    hi, lo = _np_split(a64)
    return jnp.asarray(np.concatenate([hi, lo], axis=axis))


def _twice(x, axis):
    return jnp.concatenate([x, x], axis=axis)


def _fft_kernel(*refs, n_cast):
    x_ref, m_ref, w_ref, cd_ref = refs[:4]
    o_ref, a_ref = refs[4 + n_cast], refs[-1]
    _cast_chunks(refs[4:4 + n_cast], refs[5 + n_cast:-1])
    _fft_body(x_ref, m_ref, w_ref, cd_ref, o_ref, a_ref)


def _fft_body(x_ref, m_ref, w_ref, cd_ref, o_ref, a_ref):
    n_kb = FFT_N1 // FFT_STEP

    def stage1(nj, carry):
        for jj in range(FFT_STEP):
            x = jnp.concatenate([x_ref[hf, nj, pl.ds(jj, FFT_N1, stride=FFT_STEP), :] for hf in range(2)],
                                axis=1).astype(BF16)
            a = _dot(m_ref[nj * FFT_STEP + jj], _twice(x, 0))
            row = pl.multiple_of((nj * FFT_STEP + jj) * FFT_STEP, FFT_STEP)
            for hf in range(2):
                for ri in range(2):
                    for kb in range(n_kb):
                        r0 = ri * FFT_N1 + kb * FFT_STEP
                        a_ref[hf, ri, kb, pl.ds(row, FFT_STEP), :] = a[r0:r0 + FFT_STEP, hf * HALF:(hf + 1) * HALF]
        return carry

    kb = pl.program_id(1)

    @pl.when(kb == 0)
    def _():
        lax.fori_loop(0, FFT_N2 // FFT_STEP, stage1, 0)

    w = w_ref[...].astype(BF16)
    zs = []
    for jj in range(FFT_STEP):
        rhs = jnp.concatenate(
            [jnp.concatenate([a_ref[hf, ri, kb, pl.ds(jj, FFT_N2, stride=FFT_STEP), :] for ri in range(2)], axis=0)
             for hf in range(2)], axis=1).astype(BF16)
        y = _dot(w, rhs)
        zs.append(jnp.concatenate([y[:FFT_N2], y[FFT_N2:]], axis=1).astype(BF16))
    z = jnp.concatenate(zs, axis=0)
    four = _dot(z, cd_ref[...].astype(BF16))
    for jj in range(FFT_STEP):
        for kt in range(FFT_N2 // MIX_K2):
            for hf in range(2):
                r0 = jj * FFT_N2 + kt * MIX_K2
                o_ref[hf, kt, jj * MIX_K2:(jj + 1) * MIX_K2, :] = four[r0:r0 + MIX_K2, hf * HALF:(hf + 1) * HALF]


def _fourier_latent(f_blk, cast=()):
    nb = f_blk.shape[0]
    m1, m3, cd = _fft_constants()
    n_nj, n_kb = FFT_N2 // FFT_STEP, FFT_N1 // FFT_STEP
    grid = (nb, n_kb)
    c_in_specs, c_args, c_out_specs, c_out_shapes = _cast_plan(cast, grid)
    return pl.pallas_call(
        functools.partial(_fft_kernel, n_cast=len(cast)),
        grid=grid,
        in_specs=[
            pl.BlockSpec((None, 2, n_nj, FFT_N1 * FFT_STEP, HALF), lambda b, j: (b, 0, 0, 0, 0)),
            _const_spec((FFT_N2, 2 * FFT_N1, 2 * FFT_N1)),
            _const_spec((2 * FFT_N2, 2 * FFT_N2)), _const_spec((2 * FNET_W, FNET_W)),
        ] + c_in_specs,
        out_specs=(pl.BlockSpec((None, 2, FFT_N2 // MIX_K2, FFT_STEP * MIX_K2, HALF), lambda b, j: (b, 0, 0, j, 0)),
                   *c_out_specs),
        out_shape=(jax.ShapeDtypeStruct((nb, 2, FFT_N2 // MIX_K2, FFT_N1 * MIX_K2, HALF), F32), *c_out_shapes),
        scratch_shapes=[pltpu.VMEM((2, 2, n_kb, FFT_N2 * FFT_STEP, HALF), F32)],
        compiler_params=_params(2),
        name="fourier",
    )(f_blk, m1, m3, cd, *c_args)


def _fft_ctx_kernel(x_ref, w_ref, cd_ref, o_ref):
    x = jnp.concatenate([x_ref[hf] for hf in range(2)], axis=1).astype(BF16)
    z = _dot(w_ref[...].astype(BF16), x)
    z = jnp.concatenate([z[:CTX_LEN], z[CTX_LEN:]], axis=1).astype(BF16)
    four = _dot(z, cd_ref[...].astype(BF16))
    for hf in range(2):
        o_ref[hf] = four[:, hf * HALF:(hf + 1) * HALF]


def _fourier_ctx(f_ctx):
    nb = f_ctx.shape[2] // CTX_LEN
    w, cd = _ctx_fft_constants()
    return pl.pallas_call(
        _fft_ctx_kernel,
        grid=(nb,),
        in_specs=[
            pl.BlockSpec((None, 2, CTX_LEN, HALF), lambda b: (0, 0, b, 0)),
            _const_spec((2 * CTX_LEN, CTX_LEN)), _const_spec((2 * FNET_W, FNET_W)),
        ],
        out_specs=pl.BlockSpec((None, 2, CTX_LEN, HALF), lambda b: (0, 0, b, 0)),
        out_shape=jax.ShapeDtypeStruct(f_ctx.shape, F32),
        compiler_params=_params(1),
        name="fourier_ctx",
    )(f_ctx, w, cd)


def _mix_mlp_kernel(*refs, tm, ctx, n_cast):
    o_ref = refs[18 + n_cast]
    _cast_chunks(refs[18:18 + n_cast], refs[19 + n_cast:])
    _mix_tile(refs[:7], refs[7:18], o_ref, tm=tm, ctx=ctx)


def _mix_mlp_merged_kernel(*refs, tm, tc, n_cast, n_lat):
    lat, cx, shared, refs = refs[:7], refs[7:14], refs[14:25], refs[25:]
    cast_src, (o_lat, o_ctx), cast_dst = refs[:n_cast], refs[n_cast:n_cast + 2], refs[n_cast + 2:]
    _cast_chunks(cast_src, cast_dst)
    t = pl.program_id(0)

    @pl.when(t < n_lat)
    def _():
        _mix_tile(lat, shared, o_lat, tm=tm, ctx=False)

    @pl.when(t == n_lat)
    def _():
        _mix_tile(cx, shared, o_ctx, tm=tc, ctx=True)


def _mix_tile(tile_refs, shared_refs, o_ref, *, tm, ctx):
    x_ref, ret_ref, four_ref, u_ref, vn_ref, gate_ref, mod_ref = tile_refs
    ws_ref, bs_ref, wa_ref, wb_ref, wc_ref, wo_ref, gpm_ref, gprm_ref, gpom_ref, wup_ref, wdn_ref = shared_refs
    group = lax.broadcasted_iota(jnp.int32, (GMLP_CHUNK, GMLP_W), 1) // GMLP_GROUP
    parts = []
    for cc in range(tm // GMLP_CHUNK):
        rows = slice(cc * GMLP_CHUNK, (cc + 1) * GMLP_CHUNK)
        vn = vn_ref[0, rows, :]
        s = bs_ref[...]
        for g in range(GMLP_W // GMLP_GROUP):
            s = s + jnp.where(group == g, _dot(ws_ref[g], vn), 0.0)
        parts.append((u_ref[0, rows, :].astype(F32) * s).astype(BF16))
    sgu = jnp.concatenate(parts, axis=0)

    if ctx:
        four = jnp.concatenate([four_ref[0, hf] for hf in range(2)], axis=1).astype(BF16)
    else:
        nk2 = tm // FFT_N1
        four = jnp.concatenate(
            [jnp.concatenate([four_ref[0, hf, pl.ds(k2, FFT_N1, stride=nk2), :] for k2 in range(nk2)], axis=0)
             for hf in range(2)], axis=1).astype(BF16)
    ga = gate_ref[0, :, 0:D_MODEL].astype(F32)
    gb = gate_ref[0, :, D_MODEL:2 * D_MODEL].astype(F32)
    gc = gate_ref[0, :, 2 * D_MODEL:3 * D_MODEL].astype(F32)
    merged = (ga * _dot(ret_ref[0], wa_ref[...]) + gb * _dot(four, wb_ref[...])
              + gc * _dot(sgu, wc_ref[...])).astype(BF16)

    gain1 = mod_ref[0, :, 2 * D_MODEL:3 * D_MODEL] * gpm_ref[...]
    shift2 = mod_ref[0, :, 3 * D_MODEL:4 * D_MODEL]
    gain2 = gprm_ref[...] * (1.0 + mod_ref[0, :, 4 * D_MODEL:5 * D_MODEL])
    gain3 = mod_ref[0, :, 5 * D_MODEL:6 * D_MODEL] * gpom_ref[...]

    halves = [slice(r * (tm // 2), (r + 1) * (tm // 2)) for r in range(2)]
    x1, h2 = [], []
    for rows in halves:
        y = _dot(merged[rows], wo_ref[...])
        x1.append(x_ref[0, rows, :] + _rms(y) * gain1)
        h2.append((_rms(x1[-1]) * gain2 + shift2).astype(BF16))
    up0 = jnp.concatenate([_dot(h, wup_ref[:, 0:FF_CHUNK]) for h in h2], axis=0)
    h2 = jnp.concatenate(h2, axis=0)

    n_ff = D_FF // FF_CHUNK
    m = None
    for c in range(n_ff - 1):
        cols = slice(c * FF_CHUNK, (c + 1) * FF_CHUNK)
        a = jnp.maximum(up0 if c == 0 else _dot(h2, wup_ref[:, cols]), 0.0)
        part = _dot((a * a).astype(BF16), wdn_ref[cols, :])
        m = part if m is None else m + part
    cols = slice((n_ff - 1) * FF_CHUNK, n_ff * FF_CHUNK)
    a = jnp.maximum(_dot(h2, wup_ref[:, cols]), 0.0)
    a = (a * a).astype(BF16)
    for r, rows in enumerate(halves):
        mr = m[rows] + _dot(a[rows], wdn_ref[cols, :])
        o_ref[0, rows, :] = x1[r] + _rms(mr) * gain3


def _mix_mlp_merged(lat, cx, layer, mod, ws, bs_tab, wa, wb, wc, wo,
                    g_post_mix, g_pre_mlp, g_post_mlp, w_up, w_down, *, tm, cast=()):
    nb, rows, _ = lat[0].shape
    tc = cx[0].shape[1]
    nt = rows // tm
    n_lat = nb * nt
    grid = (n_lat + 1,)
    tile = lambda t: jnp.minimum(t, n_lat - 1)
    lat_spec = lambda w: pl.BlockSpec((1, tm, w), lambda t: (tile(t) // nt, tile(t) % nt, 0))
    ctx_spec = lambda w: _const_spec((1, tc, w))
    lspec = lambda *shape: _layer_spec(shape, layer)
    mod_blk = (1, 1, 6 * D_MODEL)
    c_in_specs, c_args, c_out_specs, c_out_shapes = _cast_plan(cast, grid)
    in_specs = [
        lat_spec(D_MODEL), lat_spec(RET_W),
        pl.BlockSpec((1, 2, None, tm, HALF), lambda t: (tile(t) // nt, 0, tile(t) % nt, 0, 0)),
        lat_spec(GMLP_W), lat_spec(GMLP_W), lat_spec(3 * D_MODEL),
        pl.BlockSpec(mod_blk, lambda t: (tile(t) // nt, 0, 0)),
        ctx_spec(D_MODEL), ctx_spec(RET_W),
        _const_spec((1, 2, tc, HALF)),
        ctx_spec(GMLP_W), ctx_spec(GMLP_W), ctx_spec(3 * D_MODEL),
        pl.BlockSpec(mod_blk, lambda t: (2, 0, 0)),
        lspec(GMLP_W // GMLP_GROUP, GMLP_CHUNK, GMLP_CHUNK),
        lspec(GMLP_CHUNK, GMLP_W),
        _const_spec((RET_W, D_MODEL)), _const_spec((FNET_W, D_MODEL)), _const_spec((GMLP_W, D_MODEL)),
        _const_spec((D_MODEL, D_MODEL)),
        lspec(1, D_MODEL), lspec(1, D_MODEL), lspec(1, D_MODEL),
        _const_spec((D_MODEL, D_FF)), _const_spec((D_FF, D_MODEL)),
    ] + c_in_specs
    return pl.pallas_call(
        functools.partial(_mix_mlp_merged_kernel, tm=tm, tc=tc, n_cast=len(cast), n_lat=n_lat),
        grid=grid,
        in_specs=in_specs,
        out_specs=(lat_spec(D_MODEL), pl.BlockSpec((1, tc, D_MODEL), lambda t: (0, 0, 0)), *c_out_specs),
        out_shape=(jax.ShapeDtypeStruct((nb, rows, D_MODEL), F32), jax.ShapeDtypeStruct(cx[0].shape, F32),
                   *c_out_shapes),
        compiler_params=_params(1),
        name="mix_mlp",
    )(*lat, mod, *cx, mod, ws, bs_tab, wa, wb, wc, wo, g_post_mix, g_pre_mlp, g_post_mlp, w_up, w_down, *c_args)


def _mix_mlp(xs, ret, four, ug, vn, gates, layer, mod, ws, bs_tab, wa, wb, wc, wo,
             g_post_mix, g_pre_mlp, g_post_mlp, w_up, w_down, *, tm, ctx, cast=()):
    nb, rows, _ = xs.shape
    tok_spec = lambda w: pl.BlockSpec((1, tm, w), lambda b, i: (b, i, 0))
    lspec = lambda *shape: _layer_spec(shape, layer)
    if ctx:
        four_spec = pl.BlockSpec((1, 2, tm, HALF), lambda b, i: (b, 0, i, 0))
    else:
        four_spec = pl.BlockSpec((1, 2, None, tm, HALF), lambda b, i: (b, 0, i, 0, 0))
    c_in_specs, c_args, c_out_specs, c_out_shapes = _cast_plan(cast, (nb, rows // tm))
    outs = pl.pallas_call(
        functools.partial(_mix_mlp_kernel, tm=tm, ctx=ctx, n_cast=len(cast)),
        grid=(nb, rows // tm),
        in_specs=[
            tok_spec(D_MODEL), tok_spec(RET_W),
            four_spec,
            tok_spec(GMLP_W), tok_spec(GMLP_W), tok_spec(3 * D_MODEL),
            _mod_spec(ctx),
            lspec(GMLP_W // GMLP_GROUP, GMLP_CHUNK, GMLP_CHUNK),
            lspec(GMLP_CHUNK, GMLP_W),
            _const_spec((RET_W, D_MODEL)), _const_spec((FNET_W, D_MODEL)), _const_spec((GMLP_W, D_MODEL)),
            _const_spec((D_MODEL, D_MODEL)),
            lspec(1, D_MODEL), lspec(1, D_MODEL), lspec(1, D_MODEL),
            _const_spec((D_MODEL, D_FF)), _const_spec((D_FF, D_MODEL)),
        ] + c_in_specs,
        out_specs=(tok_spec(D_MODEL),) + tuple(c_out_specs),
        out_shape=(jax.ShapeDtypeStruct((nb, rows, D_MODEL), F32),) + tuple(c_out_shapes),
        compiler_params=_params(2),
        name="mix_mlp_ctx" if ctx else "mix_mlp",
    )(xs, ret, four, ug, vn, gates, mod, ws, bs_tab, wa, wb, wc, wo,
      g_post_mix, g_pre_mlp, g_post_mlp, w_up, w_down, *c_args)
    return outs if cast else outs[0]


def _rope_tables():
    rows = SEQ // GRID_W
    freqs = ROPE_BASE ** (-jnp.arange(ROPE_FREQS, dtype=F32) / ROPE_FREQS)
    ang_r = jnp.arange(rows, dtype=F32)[:, None] * freqs
    ang_c = jnp.arange(GRID_W, dtype=F32)[:, None] * freqs

    def lanes(a, first):
        z = jnp.zeros_like(a)
        return jnp.concatenate([a, z] if first else [z, a], axis=-1)

    def pair(ang, first):
        cos, sin = lanes(jnp.cos(ang), first), lanes(jnp.sin(ang), first)
        return jnp.stack([jnp.concatenate([cos, cos], axis=-1), jnp.concatenate([-sin, sin], axis=-1)])

    return pair(ang_r, True), pair(ang_c, False)


def kernel(x, c, ctx, c_ctx, w_mod, b_mod, g_pre_mix, g_post_mix, g_pre_mlp, g_post_mlp, w_in,
           ret_decay_logit, sgu_w_s, sgu_b_s, sgu_norm, w_branch_a, w_branch_b, w_branch_c, w_out,
           w_up, w_down):
    nb = x.shape[0]
    depth = w_mod.shape[0]
    assert x.shape == (nb, SEQ, D_MODEL) and ctx.shape == (nb, CTX_LEN, D_MODEL) and nb == 2

    cond_rows = jnp.concatenate([c, c_ctx[None, :], jnp.zeros((8 - nb - 1, D_MODEL), F32)], axis=0)
    mod = _modulation(cond_rows, w_mod, b_mod, 0).reshape(8, 1, 6 * D_MODEL)
    rope_tabs = _rope_tables()
    pavg = jnp.asarray(np.kron(np.eye(GMLP_W // GMLP_GROUP), np.full((GMLP_GROUP, GMLP_GROUP), 1.0 / GMLP_GROUP)), BF16)

    vec = lambda p: p[:, None, :]
    g_pre, g_norm = vec(g_pre_mix), vec(sgu_norm)
    logits = ret_decay_logit.reshape(depth, 2 * HEADS, 1)
    ws = sgu_w_s.astype(BF16)
    bs_tab = jnp.repeat(jnp.swapaxes(sgu_b_s, 1, 2), GMLP_GROUP, axis=2)
    gains = (vec(g_post_mix), vec(g_pre_mlp), vec(g_post_mlp))
    w_in_l = w_in[0].astype(BF16)
    ctx = ctx.reshape(1, nb * CTX_LEN, D_MODEL)
    for l in range(depth):
        last = l == depth - 1
        in_args = (l, mod, g_pre, w_in_l, pavg, g_norm)
        *proj, wa, wb, wc, wo, wup = _in_proj(
            x, *in_args, rope_tabs, tm=TM_IN, ctx=False,
            cast=[(w, l) for w in (w_branch_a, w_branch_b, w_branch_c, w_out, w_up)])
        q, k, v, sg, f_lat, ug, vn, gates = proj
        per_batch = lambda a: a.reshape(nb, CTX_LEN, RET_W)
        if last:
            kc, vc = _in_proj(ctx, *in_args, None, tm=nb * CTX_LEN, ctx=True, kv_only=True)
            (ret,) = _retention(l, logits, (q, k, v, sg), (per_batch(kc), per_batch(vc)), ctx_out=False)
        else:
            qc, kc, vc, sgc, f_ctx, ugc, vnc, gatesc = _in_proj(ctx, *in_args, None, tm=nb * CTX_LEN, ctx=True)
            ret, retc, mod_next = _retention(l, logits, (q, k, v, sg), tuple(map(per_batch, (qc, kc, vc, sgc))),
                                             ctx_out=True, mod_job=(cond_rows, w_mod, b_mod, l + 1))
            retc = retc.reshape(1, nb * CTX_LEN, RET_W)
        four, wdn = _fourier_latent(f_lat, cast=[(w_down, l)])
        mix_w = (l, mod, ws, bs_tab, wa, wb, wc, wo, *gains, wup, wdn)
        if last:
            x = _mix_mlp(x, ret, four, ug, vn, gates, *mix_w, tm=TM_LAT, ctx=False)
        else:
            x, ctx, w_in_l = _mix_mlp_merged(
                (x, ret, four, ug, vn, gates), (ctx, retc, _fourier_ctx(f_ctx), ugc, vnc, gatesc), *mix_w,
                tm=TM_LAT, cast=[(w_in, l + 1)])
            mod = mod_next.reshape(8, 1, 6 * D_MODEL)
    return x
```

```python
import functools

import numpy as np
import jax
import jax.numpy as jnp
from jax import lax
from jax.experimental import pallas as pl
from jax.experimental.pallas import tpu as pltpu

F32 = jnp.float32
BF16 = jnp.bfloat16

D_MODEL = 1024
SEQ = 8192
CTX_LEN = 256
GRID_W = 64
RET_W = 512
HEADS = 4
HEAD_DIM = 128
ROPE_BASE = 10000.0
ROPE_FREQS = HEAD_DIM // 4
FNET_W = 256
FNET_GROUP = 64
GMLP_W = 256
GMLP_GROUP = 64
GMLP_CHUNK = 128
D_FF = 4 * D_MODEL
EPS = 1e-6
IN_W = 4 * RET_W + FNET_W + 2 * GMLP_W + 3 * D_MODEL
COL_F = 4 * RET_W
COL_U = COL_F + FNET_W
COL_VS = COL_U + GMLP_W
COL_GATE = COL_VS + GMLP_W

TM_LAT = 512
TM_IN = 1024
RCHUNK = 256
RET_TILE = 4096
RET_STEPS = SEQ // RET_TILE
RET_CPT = RET_TILE // RCHUNK
RET_AHEAD = 4
N_RCHUNKS = 1 + SEQ // RCHUNK
FF_CHUNK = 1024
FFT_N1 = 64
FFT_N2 = 128
HALF = 128
FFT_STEP = 16
MIX_K2 = TM_LAT // FFT_N1

_VMEM_LIMIT = 56 * 1024 * 1024


def _dot(a, b):
    return jnp.dot(a, b, preferred_element_type=F32)


def _split(x):
    hi = x.astype(BF16)
    lo = (x - hi.astype(F32)).astype(BF16)
    return hi, lo


def _np_split(a64):
    hi = np.asarray(a64, np.float32).astype(BF16)
    lo = (np.asarray(a64, np.float32) - hi.astype(np.float32)).astype(BF16)
    return hi, lo


def _rms(x):
    return x * lax.rsqrt(jnp.mean(x * x, axis=-1, keepdims=True) + EPS)


def _gelu(x):
    return x * (0.5 * (1.0 + jnp.tanh(0.7978845608028654 * (x + 0.044715 * (x * x * x)))))


def _sigmoid(x):
    return 1.0 / (1.0 + jnp.exp(-x))


def _const_spec(shape):
    zeros = (0,) * len(shape)
    return pl.BlockSpec(shape, lambda *_: zeros, pipeline_mode=pl.Buffered(1))


def _layer_spec(shape, layer):
    idx = (layer,) + (0,) * len(shape)
    return pl.BlockSpec((None,) + tuple(shape), lambda *_: idx, pipeline_mode=pl.Buffered(1))


def _params(n_axes):
    return pltpu.CompilerParams(dimension_semantics=("arbitrary",) * n_axes, vmem_limit_bytes=_VMEM_LIMIT)


def _mod_kernel(a_ref, w_ref, b_ref, o_ref):
    a = a_ref[...]
    a = a * _sigmoid(a)
    ah, al = _split(a)
    w = w_ref[...].astype(BF16)
    o_ref[...] = _dot(ah, w) + _dot(al, w) + b_ref[...]


def _mod_plan(cond_rows, w_mod, b_mod, layer, grid):
    steps = int(np.prod(grid))
    n_chunks = max(n for n in (4, 8, 16, 24, 48) if n <= steps)
    tn = (6 * D_MODEL) // n_chunks

    def chunk(*idx):
        t = idx[0]
        for n, i in zip(grid[1:], idx[1:]):
            t = t * n + i
        return jnp.minimum(t, n_chunks - 1)

    in_specs = [
        _const_spec((8, D_MODEL)),
        pl.BlockSpec((None, D_MODEL, tn), lambda *idx: (layer, 0, chunk(*idx))),
        pl.BlockSpec((None, 1, tn), lambda *idx: (layer, 0, chunk(*idx))),
    ]
    args = [cond_rows, w_mod, b_mod.reshape(b_mod.shape[0], 1, 6 * D_MODEL)]
    return (in_specs, args, pl.BlockSpec((8, tn), lambda *idx: (0, chunk(*idx))),
            jax.ShapeDtypeStruct((8, 6 * D_MODEL), F32))


def _modulation_kernel(*refs, n_cast):
    _mod_kernel(*refs[:3], refs[3 + n_cast])
    _cast_chunks(refs[3:3 + n_cast], refs[4 + n_cast:])


def _modulation(cond_rows, w_mod, b_mod, layer, cast=()):
    grid = (4,)
    in_specs, args, out_spec, out_shape = _mod_plan(cond_rows, w_mod, b_mod, layer, grid)
    c_in_specs, c_args, c_out_specs, c_out_shapes = _cast_plan(cast, grid)
    return pl.pallas_call(
        functools.partial(_modulation_kernel, n_cast=len(cast)),
        grid=grid, in_specs=in_specs + c_in_specs, out_specs=(out_spec, *c_out_specs),
        out_shape=(out_shape, *c_out_shapes), compiler_params=_params(1), name="modulation",
    )(*args, *c_args)


def _cast_plan(cast, grid):
    steps = int(np.prod(grid))
    in_specs, args, out_specs, out_shapes = [], [], [], []

    def step(*idx):
        t = idx[0]
        for n, i in zip(grid[1:], idx[1:]):
            t = t * n + i
        return t

    for stack, layer in cast:
        _, rows, cols = stack.shape
        n_chunks = rows // 16
        while n_chunks > steps:
            n_chunks //= 2
        assert rows % n_chunks == 0
        per = steps // n_chunks
        chunk = lambda *idx, per=per, last=n_chunks - 1: jnp.minimum(step(*idx) // per, last)
        in_specs.append(pl.BlockSpec((None, rows // n_chunks, cols),
                                     lambda *idx, layer=layer, chunk=chunk: (layer, chunk(*idx), 0)))
        out_specs.append(pl.BlockSpec((rows // n_chunks, cols), lambda *idx, chunk=chunk: (chunk(*idx), 0)))
        out_shapes.append(jax.ShapeDtypeStruct((rows, cols), BF16))
        args.append(stack)
    return in_specs, args, out_specs, out_shapes


def _cast_chunks(src_refs, dst_refs):
    for src, dst in zip(src_refs, dst_refs, strict=True):
        dst[...] = src[...].astype(BF16)


def _in_proj_kernel(*refs, rope, kv_only, n_cast, mod_job):
    n_in = 8 if rope else 6
    n_mod = 3 if mod_job else 0
    ins, refs = refs[:n_in], refs[n_in:]
    cast_src, refs = refs[:n_cast], refs[n_cast:]
    mod_in, refs = refs[:n_mod], refs[n_mod:]
    outs, refs = refs[:2 if kv_only else 8], refs[2 if kv_only else 8:]
    cast_dst, mod_out = refs[:n_cast], refs[n_cast:]
    _cast_chunks(cast_src, cast_dst)
    if mod_job:
        _mod_kernel(*mod_in, *mod_out)
    if rope:
        x_ref, mod_ref, gpre_ref, w_ref, pavg_ref, gn_ref, rrow_ref, rcol_ref = ins
    else:
        x_ref, mod_ref, gpre_ref, w_ref, pavg_ref, gn_ref = ins
    tm = x_ref.shape[1]

    def prepare():
        shift = mod_ref[0, :, 0:D_MODEL]
        gain = gpre_ref[...] * (1.0 + mod_ref[0, :, D_MODEL:2 * D_MODEL])
        return (_rms(x_ref[0]) * gain + shift).astype(BF16)

    def project(hb):
        def proj(lo, hi):
            return _dot(hb, w_ref[:, lo:hi])

        if rope:
            n_rows = tm // GRID_W
            rot = []
            for t in range(2):
                by_row = jnp.concatenate(
                    [jnp.broadcast_to(rrow_ref[t, r:r + 1, :], (GRID_W, HEAD_DIM)) for r in range(n_rows)], axis=0)
                rot.append(by_row + jnp.concatenate([rcol_ref[t]] * n_rows, axis=0))

        def rope_store(z, dst):
            if not rope:
                dst[0] = z.astype(BF16)
                return
            cos, sin = rot
            for hd in range(HEADS):
                a = z[:, hd * HEAD_DIM:(hd + 1) * HEAD_DIM]
                r = a * cos + pltpu.roll(a, HEAD_DIM // 2, 1) * sin
                dst[0, :, hd * HEAD_DIM:(hd + 1) * HEAD_DIM] = r.astype(BF16)

        if kv_only:
            k_ref, v_ref = outs
        else:
            q_ref, k_ref, v_ref, sg_ref, f_ref, u_ref, vn_ref, gate_ref = outs
            rope_store(proj(0, RET_W), q_ref)
        rope_store(proj(RET_W, 2 * RET_W) * (HEAD_DIM ** -0.5), k_ref)
        v_ref[0] = proj(2 * RET_W, 3 * RET_W).astype(BF16)
        if kv_only:
            return
        g = proj(3 * RET_W, 4 * RET_W)
        sg_ref[0] = (g * _sigmoid(g)).astype(BF16)

        f = proj(COL_F, COL_U)
        for hf in range(2):
            fh = f[:, hf * HALF:(hf + 1) * HALF]
            if rope:
                for n1 in range(tm // FFT_N2):
                    for nj in range(FFT_N2 // FFT_STEP):
                        f_ref[0, hf, nj, n1 * FFT_STEP:(n1 + 1) * FFT_STEP, :] = (
                            fh[n1 * FFT_N2 + nj * FFT_STEP:n1 * FFT_N2 + (nj + 1) * FFT_STEP])
            else:
                f_ref[0, hf] = fh

        u_ref[0] = _gelu(proj(COL_U, COL_VS)).astype(BF16)
        vg = _gelu(proj(COL_VS, COL_GATE))
        sh, sl = _split(vg * vg)
        pavg = pavg_ref[...]
        ms = _dot(sh, pavg) + _dot(sl, pavg)
        vn_ref[0] = (vg * lax.rsqrt(ms + EPS) * gn_ref[...]).astype(BF16)

        for c in range(3):
            z = proj(COL_GATE + c * D_MODEL, COL_GATE + (c + 1) * D_MODEL)
            gate_ref[0, :, c * D_MODEL:(c + 1) * D_MODEL] = _sigmoid(z).astype(BF16)

    project(prepare())


def _mod_spec(ctx):
    idx = (lambda b, i: (2, 0, 0)) if ctx else (lambda b, i: (b, 0, 0))
    return pl.BlockSpec((1, 1, 6 * D_MODEL), idx)


def _in_proj(xs, layer, mod, g_pre, w_in, pavg, g_norm, rope_tabs, *, tm, ctx, kv_only=False, cast=(),
             mod_job=None):
    nb, rows, _ = xs.shape
    rope = rope_tabs is not None
    grid = (nb, rows // tm)
    tok_spec = lambda w: pl.BlockSpec((1, tm, w), lambda b, i: (b, i, 0))
    in_specs = [
        tok_spec(D_MODEL),
        _mod_spec(ctx),
        _layer_spec((1, D_MODEL), layer),
        _const_spec((D_MODEL, IN_W)),
        _const_spec((GMLP_W, GMLP_W)),
        _layer_spec((1, GMLP_W), layer),
    ]
    args = [xs, mod, g_pre, w_in, pavg, g_norm]
    if rope:
        in_specs += [pl.BlockSpec((2, tm // GRID_W, HEAD_DIM), lambda b, i: (0, i, 0)),
                     _const_spec((2, GRID_W, HEAD_DIM))]
        args += list(rope_tabs)
    c_in_specs, c_args, c_out_specs, c_out_shapes = _cast_plan(cast, grid)
    if mod_job:
        m_in_specs, m_args, m_out_spec, m_out_shape = _mod_plan(*mod_job, grid)
        c_in_specs, c_args = c_in_specs + m_in_specs, c_args + m_args
        c_out_specs, c_out_shapes = c_out_specs + [m_out_spec], c_out_shapes + [m_out_shape]
    in_specs += c_in_specs
    args += c_args
    bf = lambda w: jax.ShapeDtypeStruct((nb, rows, w), BF16)
    if kv_only:
        out_shapes = (bf(RET_W), bf(RET_W))
        out_specs = (tok_spec(RET_W), tok_spec(RET_W))
    else:
        if rope:
            n_nj = FFT_N2 // FFT_STEP
            f_shape = jax.ShapeDtypeStruct((nb, 2, n_nj, FFT_N1 * FFT_STEP, HALF), F32)
            f_spec = pl.BlockSpec((1, 2, n_nj, (tm // FFT_N2) * FFT_STEP, HALF), lambda b, i: (b, 0, 0, i, 0))
        else:
            f_shape = jax.ShapeDtypeStruct((nb, 2, rows, HALF), F32)
            f_spec = pl.BlockSpec((1, 2, tm, HALF), lambda b, i: (b, 0, i, 0))
        out_shapes = (
            bf(RET_W), bf(RET_W), bf(RET_W), bf(RET_W),
            f_shape,
            bf(GMLP_W), bf(GMLP_W),
            bf(3 * D_MODEL),
        )
        out_specs = (
            tok_spec(RET_W), tok_spec(RET_W), tok_spec(RET_W), tok_spec(RET_W),
            f_spec,
            tok_spec(GMLP_W), tok_spec(GMLP_W), tok_spec(3 * D_MODEL),
        )
    return pl.pallas_call(
        functools.partial(_in_proj_kernel, rope=rope, kv_only=kv_only, n_cast=len(cast), mod_job=bool(mod_job)),
        grid=grid,
        in_specs=in_specs,
        out_specs=tuple(out_specs) + tuple(c_out_specs),
        out_shape=tuple(out_shapes) + tuple(c_out_shapes),
        compiler_params=_params(2),
        name="in_proj_ctx" if ctx else "in_proj",
    )(*args)


def _ret_kernel(*refs, ctx_out, n_cast, mod_job):
    n_in, n_out = (9, 2) if ctx_out else (7, 1)
    n_mod = 3 if mod_job else 0
    ins, refs = refs[:n_in], refs[n_in:]
    cast_src, refs = refs[:n_cast], refs[n_cast:]
    mod_in, refs = refs[:n_mod], refs[n_mod:]
    outs, refs = refs[:n_out], refs[n_out:]
    cast_dst, refs = refs[:n_cast], refs[n_cast:]
    _cast_chunks(cast_src, cast_dst)
    if mod_job:
        _mod_kernel(*mod_in, refs[0])
        refs = refs[1:]
    scratch = refs
    (logit_ref, ql_ref, kl_ref, vl_ref, sgl_ref, kc_ref, vc_ref) = ins[:7]
    if ctx_out:
        qc_ref, sgc_ref = ins[7:]
        ol_ref, oc_ref = outs
    else:
        (ol_ref,) = outs
    mask_ref, wf_ref, wb_ref, qf_ref, qb_ref, df_ref, db_ref, sf_ref, sb_ref, sball_ref = scratch
    b = pl.program_id(0)
    phase = pl.program_id(1)
    j = pl.program_id(2)

    @pl.when((b == 0) & (phase == 0) & (j == 0))
    def _():
        x = logit_ref[...]
        lg = -(jnp.maximum(-x, 0.0) + jnp.log(1.0 + jnp.exp(-jnp.abs(x))))
        row = lax.broadcasted_iota(jnp.int32, (RCHUNK, RCHUNK), 0).astype(F32)
        col = lax.broadcasted_iota(jnp.int32, (RCHUNK, RCHUNK), 1).astype(F32)
        diff = row - col
        pos = row[:, :HEAD_DIM]
        for hd in range(HEADS):
            lf = lg[hd:hd + 1, :]
            lb = lg[HEADS + hd:HEADS + hd + 1, :]
            mask_ref[hd] = jnp.where(diff >= 0.0, jnp.exp(lf * jnp.maximum(diff, 0.0)),
                                     jnp.exp(lb * jnp.maximum(-diff, 0.0)))
            wf_ref[hd] = jnp.exp(lf * (RCHUNK - 1.0 - pos))
            wb_ref[hd] = jnp.exp(lb * pos)
            qf_ref[hd] = jnp.exp(lf * (pos + 1.0))
            qb_ref[hd] = jnp.exp(lb * (RCHUNK - pos))
            df_ref[hd] = jnp.exp(jnp.broadcast_to(lf, (HEAD_DIM, HEAD_DIM)) * float(RCHUNK))
            db_ref[hd] = jnp.exp(jnp.broadcast_to(lb, (HEAD_DIM, HEAD_DIM)) * float(RCHUNK))

    def chunk_updates(hd, k_ref, v_ref, n_chunks, w_ref):
        cols = slice(hd * HEAD_DIM, (hd + 1) * HEAD_DIM)
        us = []
        for c in range(n_chunks):
            rows = slice(c * RCHUNK, (c + 1) * RCHUNK)
            kw = (k_ref[0, rows, cols].astype(F32) * w_ref[hd]).astype(BF16)
            us.append(lax.dot_general(kw, v_ref[0, rows, cols], (((0,), (0,)), ((), ())),
                                      preferred_element_type=F32))
        return us

    def backward_chunks(k_ref, v_ref, n_chunks, first_chunk):
        for hd in range(HEADS):
            us = chunk_updates(hd, k_ref, v_ref, n_chunks, wb_ref)
            s = sb_ref[hd]
            for c in reversed(range(n_chunks)):
                sball_ref[first_chunk + c, hd] = s.astype(BF16)
                s = db_ref[hd] * s + us[c]
            sb_ref[hd] = s

    def forward_chunks(q_ref, k_ref, v_ref, sg_ref, o_ref, n_chunks, first_chunk):
        starts = []
        for hd in range(HEADS):
            us = chunk_updates(hd, k_ref, v_ref, n_chunks, wf_ref)
            st = [sf_ref[hd]]
            for c in range(n_chunks):
                st.append(df_ref[hd] * st[-1] + us[c])
            sf_ref[hd] = st[-1]
            starts.append(st)
        if o_ref is None:
            return
        units = [(c, hd) for c in range(n_chunks) for hd in range(HEADS)]

        def scores(c, hd):
            rows = slice(c * RCHUNK, (c + 1) * RCHUNK)
            cols = slice(hd * HEAD_DIM, (hd + 1) * HEAD_DIM)
            return lax.dot_general(q_ref[0, rows, cols], k_ref[0, rows, cols], (((1,), (1,)), ((), ())),
                                   preferred_element_type=F32)

        ahead = min(RET_AHEAD, len(units))
        pending = [scores(*u) for u in units[:ahead]]
        for n, (c, hd) in enumerate(units):
            if n + ahead < len(units):
                pending.append(scores(*units[n + ahead]))
            sc = pending.pop(0)
            rows = slice(c * RCHUNK, (c + 1) * RCHUNK)
            cols = slice(hd * HEAD_DIM, (hd + 1) * HEAD_DIM)
            qf = q_ref[0, rows, cols].astype(F32)
            lhs = jnp.concatenate([(sc * mask_ref[hd]).astype(BF16),
                                   (qf * qf_ref[hd]).astype(BF16),
                                   (qf * qb_ref[hd]).astype(BF16)], axis=1)
            rhs = jnp.concatenate([v_ref[0, rows, cols], starts[hd][c].astype(BF16),
                                   sball_ref[first_chunk + c, hd]], axis=0)
            o = _dot(lhs, rhs)
            o_ref[0, rows, cols] = (_rms(o) * sg_ref[0, rows, cols].astype(F32)).astype(BF16)

    @pl.when(phase == 0)
    def _():
        @pl.when(j == 0)
        def _():
            sb_ref[...] = jnp.zeros_like(sb_ref)
            backward_chunks(kc_ref, vc_ref, 1, 0)

        backward_chunks(kl_ref, vl_ref, RET_CPT, 1 + (RET_STEPS - 1 - j) * RET_CPT)

    @pl.when(phase == 1)
    def _():
        @pl.when(j == 0)
        def _():
            sf_ref[...] = jnp.zeros_like(sf_ref)
            if ctx_out:
                forward_chunks(qc_ref, kc_ref, vc_ref, sgc_ref, oc_ref, 1, 0)
            else:
                forward_chunks(None, kc_ref, vc_ref, None, None, 1, 0)

        forward_chunks(ql_ref, kl_ref, vl_ref, sgl_ref, ol_ref, RET_CPT, 1 + j * RET_CPT)


def _retention(layer, logits, lat, ctx, *, ctx_out, cast=(), mod_job=None):
    nb = lat[0].shape[0]
    kv_idx = lambda b, p, j: (b, jnp.where(p == 0, RET_STEPS - 1 - j, j), 0)
    q_idx = lambda b, p, j: (b, jnp.where(p == 0, 0, j), 0)
    c_idx = lambda b, p, j: (b, 0, 0)
    lat_blk = (1, RET_TILE, RET_W)
    ctx_blk = (1, CTX_LEN, RET_W)
    in_specs = [
        _layer_spec((2 * HEADS, 1), layer),
        pl.BlockSpec(lat_blk, q_idx), pl.BlockSpec(lat_blk, kv_idx), pl.BlockSpec(lat_blk, kv_idx),
        pl.BlockSpec(lat_blk, q_idx),
        pl.BlockSpec(ctx_blk, c_idx), pl.BlockSpec(ctx_blk, c_idx),
    ]
    if ctx_out:
        qc, kc, vc, sgc = ctx
        args = [logits, *lat, kc, vc, qc, sgc]
        in_specs += [pl.BlockSpec(ctx_blk, c_idx), pl.BlockSpec(ctx_blk, c_idx)]
        out_specs = (pl.BlockSpec(lat_blk, q_idx), pl.BlockSpec(ctx_blk, c_idx))
        out_shape = (jax.ShapeDtypeStruct((nb, SEQ, RET_W), BF16), jax.ShapeDtypeStruct((nb, CTX_LEN, RET_W), BF16))
    else:
        kc, vc = ctx
        args = [logits, *lat, kc, vc]
        out_specs = (pl.BlockSpec(lat_blk, q_idx),)
        out_shape = (jax.ShapeDtypeStruct((nb, SEQ, RET_W), BF16),)
    grid = (nb, 2, RET_STEPS)
    c_in_specs, c_args, c_out_specs, c_out_shapes = _cast_plan(cast, grid)
    if mod_job:
        m_in_specs, m_args, m_out_spec, m_out_shape = _mod_plan(*mod_job, grid)
        c_in_specs, c_args = c_in_specs + m_in_specs, c_args + m_args
        c_out_specs, c_out_shapes = c_out_specs + [m_out_spec], c_out_shapes + [m_out_shape]
    state = pltpu.VMEM((HEADS, HEAD_DIM, HEAD_DIM), F32)
    pos_tab = pltpu.VMEM((HEADS, RCHUNK, HEAD_DIM), F32)
    return pl.pallas_call(
        functools.partial(_ret_kernel, ctx_out=ctx_out, n_cast=len(cast), mod_job=bool(mod_job)),
        grid=grid,
        in_specs=in_specs + c_in_specs,
        out_specs=out_specs + tuple(c_out_specs),
        out_shape=out_shape + tuple(c_out_shapes),
        scratch_shapes=[
            pltpu.VMEM((HEADS, RCHUNK, RCHUNK), F32),
            pos_tab, pos_tab, pos_tab, pos_tab,
            state, state,
            state, state,
            pltpu.VMEM((N_RCHUNKS, HEADS, HEAD_DIM, HEAD_DIM), BF16),
        ],
        compiler_params=_params(3),
        name="retention",
    )(*args, *c_args)


def _dft_cos_sin(n):
    idx = np.arange(n, dtype=np.float64)
    ang = 2.0 * np.pi * ((idx[:, None] * idx[None, :]) % n) / n
    return np.cos(ang), np.sin(ang)


def _channel_dft(scale):
    c, s = _dft_cos_sin(FNET_GROUP)
    eye = np.eye(FNET_W // FNET_GROUP)
    return np.concatenate([np.kron(eye, c), np.kron(eye, s)], axis=0) * scale


def _fft_constants():
    k1 = np.arange(FFT_N1, dtype=np.float64)[None, :, None]
    n1 = np.arange(FFT_N1, dtype=np.float64)[None, None, :]
    n2 = np.arange(FFT_N2, dtype=np.float64)[:, None, None]
    ang = 2.0 * np.pi * ((k1 * (FFT_N2 * n1 + n2)) % SEQ) / SEQ
    m1 = np.concatenate([np.cos(ang), -np.sin(ang)], axis=1)
    c, s = _dft_cos_sin(FFT_N2)
    m3 = np.block([[c, s], [-s, c]])
    return (_np_split_cat(m1, -1), jnp.asarray(m3, F32),
            jnp.asarray(_channel_dft((SEQ * FNET_GROUP) ** -0.5), F32))


def _ctx_fft_constants():
    c, s = _dft_cos_sin(CTX_LEN)
    return (jnp.asarray(np.concatenate([c, -s], axis=0), F32),
            jnp.asarray(_channel_dft((CTX_LEN * FNET_GROUP) ** -0.5), F32))


def _np_split_cat(a64, axis):
    hi, lo = _np_split(a64)
    return jnp.asarray(np.concatenate([hi, lo], axis=axis))


def _twice(x, axis):
    return jnp.concatenate([x, x], axis=axis)


def _fft_kernel(*refs, n_cast):
    x_ref, m_ref, w_ref, cd_ref = refs[:4]
    o_ref, a_ref = refs[4 + n_cast], refs[-1]
    _cast_chunks(refs[4:4 + n_cast], refs[5 + n_cast:-1])
    _fft_body(x_ref, m_ref, w_ref, cd_ref, o_ref, a_ref)


def _fft_body(x_ref, m_ref, w_ref, cd_ref, o_ref, a_ref):
    n_kb = FFT_N1 // FFT_STEP

    def stage1(nj, carry):
        for jj in range(FFT_STEP):
            x = jnp.concatenate([x_ref[hf, nj, pl.ds(jj, FFT_N1, stride=FFT_STEP), :] for hf in range(2)],
                                axis=1).astype(BF16)
            a = _dot(m_ref[nj * FFT_STEP + jj], _twice(x, 0))
            row = pl.multiple_of((nj * FFT_STEP + jj) * FFT_STEP, FFT_STEP)
            for hf in range(2):
                for ri in range(2):
                    for kb in range(n_kb):
                        r0 = ri * FFT_N1 + kb * FFT_STEP
                        a_ref[hf, ri, kb, pl.ds(row, FFT_STEP), :] = a[r0:r0 + FFT_STEP, hf * HALF:(hf + 1) * HALF]
        return carry

    kb = pl.program_id(1)

    @pl.when(kb == 0)
    def _():
        lax.fori_loop(0, FFT_N2 // FFT_STEP, stage1, 0)

    w = w_ref[...].astype(BF16)
    zs = []
    for jj in range(FFT_STEP):
        rhs = jnp.concatenate(
            [jnp.concatenate([a_ref[hf, ri, kb, pl.ds(jj, FFT_N2, stride=FFT_STEP), :] for ri in range(2)], axis=0)
             for hf in range(2)], axis=1).astype(BF16)
        y = _dot(w, rhs)
        zs.append(jnp.concatenate([y[:FFT_N2], y[FFT_N2:]], axis=1).astype(BF16))
    z = jnp.concatenate(zs, axis=0)
    four = _dot(z, cd_ref[...].astype(BF16))
    for jj in range(FFT_STEP):
        for kt in range(FFT_N2 // MIX_K2):
            for hf in range(2):
                r0 = jj * FFT_N2 + kt * MIX_K2
                o_ref[hf, kt, jj * MIX_K2:(jj + 1) * MIX_K2, :] = four[r0:r0 + MIX_K2, hf * HALF:(hf + 1) * HALF]


def _fourier_latent(f_blk, cast=()):
    nb = f_blk.shape[0]
    m1, m3, cd = _fft_constants()
    n_nj, n_kb = FFT_N2 // FFT_STEP, FFT_N1 // FFT_STEP
    grid = (nb, n_kb)
    c_in_specs, c_args, c_out_specs, c_out_shapes = _cast_plan(cast, grid)
    return pl.pallas_call(
        functools.partial(_fft_kernel, n_cast=len(cast)),
        grid=grid,
        in_specs=[
            pl.BlockSpec((None, 2, n_nj, FFT_N1 * FFT_STEP, HALF), lambda b, j: (b, 0, 0, 0, 0)),
            _const_spec((FFT_N2, 2 * FFT_N1, 2 * FFT_N1)),
            _const_spec((2 * FFT_N2, 2 * FFT_N2)), _const_spec((2 * FNET_W, FNET_W)),
        ] + c_in_specs,
        out_specs=(pl.BlockSpec((None, 2, FFT_N2 // MIX_K2, FFT_STEP * MIX_K2, HALF), lambda b, j: (b, 0, 0, j, 0)),
                   *c_out_specs),
        out_shape=(jax.ShapeDtypeStruct((nb, 2, FFT_N2 // MIX_K2, FFT_N1 * MIX_K2, HALF), F32), *c_out_shapes),
        scratch_shapes=[pltpu.VMEM((2, 2, n_kb, FFT_N2 * FFT_STEP, HALF), F32)],
        compiler_params=_params(2),
        name="fourier",
    )(f_blk, m1, m3, cd, *c_args)


def _fft_ctx_kernel(x_ref, w_ref, cd_ref, o_ref):
    x = jnp.concatenate([x_ref[hf] for hf in range(2)], axis=1).astype(BF16)
    z = _dot(w_ref[...].astype(BF16), x)
    z = jnp.concatenate([z[:CTX_LEN], z[CTX_LEN:]], axis=1).astype(BF16)
    four = _dot(z, cd_ref[...].astype(BF16))
    for hf in range(2):
        o_ref[hf] = four[:, hf * HALF:(hf + 1) * HALF]


def _fourier_ctx(f_ctx):
    nb = f_ctx.shape[2] // CTX_LEN
    w, cd = _ctx_fft_constants()
    return pl.pallas_call(
        _fft_ctx_kernel,
        grid=(nb,),
        in_specs=[
            pl.BlockSpec((None, 2, CTX_LEN, HALF), lambda b: (0, 0, b, 0)),
            _const_spec((2 * CTX_LEN, CTX_LEN)), _const_spec((2 * FNET_W, FNET_W)),
        ],
        out_specs=pl.BlockSpec((None, 2, CTX_LEN, HALF), lambda b: (0, 0, b, 0)),
        out_shape=jax.ShapeDtypeStruct(f_ctx.shape, F32),
        compiler_params=_params(1),
        name="fourier_ctx",
    )(f_ctx, w, cd)


def _mix_mlp_kernel(*refs, tm, ctx, n_cast):
    o_ref = refs[18 + n_cast]
    _cast_chunks(refs[18:18 + n_cast], refs[19 + n_cast:])
    _mix_tile(refs[:7], refs[7:18], o_ref, tm=tm, ctx=ctx)


def _mix_mlp_merged_kernel(*refs, tm, tc, n_cast, n_lat):
    lat, cx, shared, refs = refs[:7], refs[7:14], refs[14:25], refs[25:]
    cast_src, (o_lat, o_ctx), cast_dst = refs[:n_cast], refs[n_cast:n_cast + 2], refs[n_cast + 2:]
    _cast_chunks(cast_src, cast_dst)
    t = pl.program_id(0)

    @pl.when(t < n_lat)
    def _():
        _mix_tile(lat, shared, o_lat, tm=tm, ctx=False)

    @pl.when(t == n_lat)
    def _():
        _mix_tile(cx, shared, o_ctx, tm=tc, ctx=True)


def _mix_tile(tile_refs, shared_refs, o_ref, *, tm, ctx):
    x_ref, ret_ref, four_ref, u_ref, vn_ref, gate_ref, mod_ref = tile_refs
    ws_ref, bs_ref, wa_ref, wb_ref, wc_ref, wo_ref, gpm_ref, gprm_ref, gpom_ref, wup_ref, wdn_ref = shared_refs
    group = lax.broadcasted_iota(jnp.int32, (GMLP_CHUNK, GMLP_W), 1) // GMLP_GROUP
    parts = []
    for cc in range(tm // GMLP_CHUNK):
        rows = slice(cc * GMLP_CHUNK, (cc + 1) * GMLP_CHUNK)
        vn = vn_ref[0, rows, :]
        s = bs_ref[...]
        for g in range(GMLP_W // GMLP_GROUP):
            s = s + jnp.where(group == g, _dot(ws_ref[g], vn), 0.0)
        parts.append((u_ref[0, rows, :].astype(F32) * s).astype(BF16))
    sgu = jnp.concatenate(parts, axis=0)

    if ctx:
        four = jnp.concatenate([four_ref[0, hf] for hf in range(2)], axis=1).astype(BF16)
    else:
        nk2 = tm // FFT_N1
        four = jnp.concatenate(
            [jnp.concatenate([four_ref[0, hf, pl.ds(k2, FFT_N1, stride=nk2), :] for k2 in range(nk2)], axis=0)
             for hf in range(2)], axis=1).astype(BF16)
    ga = gate_ref[0, :, 0:D_MODEL].astype(F32)
    gb = gate_ref[0, :, D_MODEL:2 * D_MODEL].astype(F32)
    gc = gate_ref[0, :, 2 * D_MODEL:3 * D_MODEL].astype(F32)
    merged = (ga * _dot(ret_ref[0], wa_ref[...]) + gb * _dot(four, wb_ref[...])
              + gc * _dot(sgu, wc_ref[...])).astype(BF16)

    gain1 = mod_ref[0, :, 2 * D_MODEL:3 * D_MODEL] * gpm_ref[...]
    shift2 = mod_ref[0, :, 3 * D_MODEL:4 * D_MODEL]
    gain2 = gprm_ref[...] * (1.0 + mod_ref[0, :, 4 * D_MODEL:5 * D_MODEL])
    gain3 = mod_ref[0, :, 5 * D_MODEL:6 * D_MODEL] * gpom_ref[...]

    halves = [slice(r * (tm // 2), (r + 1) * (tm // 2)) for r in range(2)]
    x1, h2 = [], []
    for rows in halves:
        y = _dot(merged[rows], wo_ref[...])
        x1.append(x_ref[0, rows, :] + _rms(y) * gain1)
        h2.append((_rms(x1[-1]) * gain2 + shift2).astype(BF16))
    up0 = jnp.concatenate([_dot(h, wup_ref[:, 0:FF_CHUNK]) for h in h2], axis=0)
    h2 = jnp.concatenate(h2, axis=0)

    n_ff = D_FF // FF_CHUNK
    m = None
    for c in range(n_ff - 1):
        cols = slice(c * FF_CHUNK, (c + 1) * FF_CHUNK)
        a = jnp.maximum(up0 if c == 0 else _dot(h2, wup_ref[:, cols]), 0.0)
        part = _dot((a * a).astype(BF16), wdn_ref[cols, :])
        m = part if m is None else m + part
    cols = slice((n_ff - 1) * FF_CHUNK, n_ff * FF_CHUNK)
    a = jnp.maximum(_dot(h2, wup_ref[:, cols]), 0.0)
    a = (a * a).astype(BF16)
    for r, rows in enumerate(halves):
        mr = m[rows] + _dot(a[rows], wdn_ref[cols, :])
        o_ref[0, rows, :] = x1[r] + _rms(mr) * gain3


def _mix_mlp_merged(lat, cx, layer, mod, ws, bs_tab, wa, wb, wc, wo,
                    g_post_mix, g_pre_mlp, g_post_mlp, w_up, w_down, *, tm, cast=()):
    nb, rows, _ = lat[0].shape
    tc = cx[0].shape[1]
    nt = rows // tm
    n_lat = nb * nt
    grid = (n_lat + 1,)
    tile = lambda t: jnp.minimum(t, n_lat - 1)
    lat_spec = lambda w: pl.BlockSpec((1, tm, w), lambda t: (tile(t) // nt, tile(t) % nt, 0))
    ctx_spec = lambda w: _const_spec((1, tc, w))
    lspec = lambda *shape: _layer_spec(shape, layer)
    mod_blk = (1, 1, 6 * D_MODEL)
    c_in_specs, c_args, c_out_specs, c_out_shapes = _cast_plan(cast, grid)
    in_specs = [
        lat_spec(D_MODEL), lat_spec(RET_W),
        pl.BlockSpec((1, 2, None, tm, HALF), lambda t: (tile(t) // nt, 0, tile(t) % nt, 0, 0)),
        lat_spec(GMLP_W), lat_spec(GMLP_W), lat_spec(3 * D_MODEL),
        pl.BlockSpec(mod_blk, lambda t: (tile(t) // nt, 0, 0)),
        ctx_spec(D_MODEL), ctx_spec(RET_W),
        _const_spec((1, 2, tc, HALF)),
        ctx_spec(GMLP_W), ctx_spec(GMLP_W), ctx_spec(3 * D_MODEL),
        pl.BlockSpec(mod_blk, lambda t: (2, 0, 0)),
        lspec(GMLP_W // GMLP_GROUP, GMLP_CHUNK, GMLP_CHUNK),
        lspec(GMLP_CHUNK, GMLP_W),
        _const_spec((RET_W, D_MODEL)), _const_spec((FNET_W, D_MODEL)), _const_spec((GMLP_W, D_MODEL)),
        _const_spec((D_MODEL, D_MODEL)),
        lspec(1, D_MODEL), lspec(1, D_MODEL), lspec(1, D_MODEL),
        _const_spec((D_MODEL, D_FF)), _const_spec((D_FF, D_MODEL)),
    ] + c_in_specs
    return pl.pallas_call(
        functools.partial(_mix_mlp_merged_kernel, tm=tm, tc=tc, n_cast=len(cast), n_lat=n_lat),
        grid=grid,
        in_specs=in_specs,
        out_specs=(lat_spec(D_MODEL), pl.BlockSpec((1, tc, D_MODEL), lambda t: (0, 0, 0)), *c_out_specs),
        out_shape=(jax.ShapeDtypeStruct((nb, rows, D_MODEL), F32), jax.ShapeDtypeStruct(cx[0].shape, F32),
                   *c_out_shapes),
        compiler_params=_params(1),
        name="mix_mlp",
    )(*lat, mod, *cx, mod, ws, bs_tab, wa, wb, wc, wo, g_post_mix, g_pre_mlp, g_post_mlp, w_up, w_down, *c_args)


def _mix_mlp(xs, ret, four, ug, vn, gates, layer, mod, ws, bs_tab, wa, wb, wc, wo,
             g_post_mix, g_pre_mlp, g_post_mlp, w_up, w_down, *, tm, ctx, cast=()):
    nb, rows, _ = xs.shape
    tok_spec = lambda w: pl.BlockSpec((1, tm, w), lambda b, i: (b, i, 0))
    lspec = lambda *shape: _layer_spec(shape, layer)
    if ctx:
        four_spec = pl.BlockSpec((1, 2, tm, HALF), lambda b, i: (b, 0, i, 0))
    else:
        four_spec = pl.BlockSpec((1, 2, None, tm, HALF), lambda b, i: (b, 0, i, 0, 0))
    c_in_specs, c_args, c_out_specs, c_out_shapes = _cast_plan(cast, (nb, rows // tm))
    outs = pl.pallas_call(
        functools.partial(_mix_mlp_kernel, tm=tm, ctx=ctx, n_cast=len(cast)),
        grid=(nb, rows // tm),
        in_specs=[
            tok_spec(D_MODEL), tok_spec(RET_W),
            four_spec,
            tok_spec(GMLP_W), tok_spec(GMLP_W), tok_spec(3 * D_MODEL),
            _mod_spec(ctx),
            lspec(GMLP_W // GMLP_GROUP, GMLP_CHUNK, GMLP_CHUNK),
            lspec(GMLP_CHUNK, GMLP_W),
            _const_spec((RET_W, D_MODEL)), _const_spec((FNET_W, D_MODEL)), _const_spec((GMLP_W, D_MODEL)),
            _const_spec((D_MODEL, D_MODEL)),
            lspec(1, D_MODEL), lspec(1, D_MODEL), lspec(1, D_MODEL),
            _const_spec((D_MODEL, D_FF)), _const_spec((D_FF, D_MODEL)),
        ] + c_in_specs,
        out_specs=(tok_spec(D_MODEL),) + tuple(c_out_specs),
        out_shape=(jax.ShapeDtypeStruct((nb, rows, D_MODEL), F32),) + tuple(c_out_shapes),
        compiler_params=_params(2),
        name="mix_mlp_ctx" if ctx else "mix_mlp",
    )(xs, ret, four, ug, vn, gates, mod, ws, bs_tab, wa, wb, wc, wo,
      g_post_mix, g_pre_mlp, g_post_mlp, w_up, w_down, *c_args)
    return outs if cast else outs[0]


def _rope_tables():
    rows = SEQ // GRID_W
    freqs = ROPE_BASE ** (-jnp.arange(ROPE_FREQS, dtype=F32) / ROPE_FREQS)
    ang_r = jnp.arange(rows, dtype=F32)[:, None] * freqs
    ang_c = jnp.arange(GRID_W, dtype=F32)[:, None] * freqs

    def lanes(a, first):
        z = jnp.zeros_like(a)
        return jnp.concatenate([a, z] if first else [z, a], axis=-1)

    def pair(ang, first):
        cos, sin = lanes(jnp.cos(ang), first), lanes(jnp.sin(ang), first)
        return jnp.stack([jnp.concatenate([cos, cos], axis=-1), jnp.concatenate([-sin, sin], axis=-1)])

    return pair(ang_r, True), pair(ang_c, False)


def kernel(x, c, ctx, c_ctx, w_mod, b_mod, g_pre_mix, g_post_mix, g_pre_mlp, g_post_mlp, w_in,
           ret_decay_logit, sgu_w_s, sgu_b_s, sgu_norm, w_branch_a, w_branch_b, w_branch_c, w_out,
           w_up, w_down):
    nb = x.shape[0]
    depth = w_mod.shape[0]
    assert x.shape == (nb, SEQ, D_MODEL) and ctx.shape == (nb, CTX_LEN, D_MODEL) and nb == 2

    cond_rows = jnp.concatenate([c, c_ctx[None, :], jnp.zeros((8 - nb - 1, D_MODEL), F32)], axis=0)
    mod, w_in_l = _modulation(cond_rows, w_mod, b_mod, 0, cast=[(w_in, 0)])
    mod = mod.reshape(8, 1, 6 * D_MODEL)
    rope_tabs = _rope_tables()
    pavg = jnp.asarray(np.kron(np.eye(GMLP_W // GMLP_GROUP), np.full((GMLP_GROUP, GMLP_GROUP), 1.0 / GMLP_GROUP)), BF16)

    vec = lambda p: p[:, None, :]
    g_pre, g_norm = vec(g_pre_mix), vec(sgu_norm)
    logits = ret_decay_logit.reshape(depth, 2 * HEADS, 1)
    ws = sgu_w_s.astype(BF16)
    bs_tab = jnp.repeat(jnp.swapaxes(sgu_b_s, 1, 2), GMLP_GROUP, axis=2)
    gains = (vec(g_post_mix), vec(g_pre_mlp), vec(g_post_mlp))
    ctx = ctx.reshape(1, nb * CTX_LEN, D_MODEL)
    for l in range(depth):
        last = l == depth - 1
        in_args = (l, mod, g_pre, w_in_l, pavg, g_norm)
        outs = _in_proj(
            x, *in_args, rope_tabs, tm=TM_IN, ctx=False,
            cast=[(w, l) for w in (w_branch_a, w_branch_b, w_branch_c, w_out, w_up)],
            mod_job=None if last else (cond_rows, w_mod, b_mod, l + 1))
        q, k, v, sg, f_lat, ug, vn, gates, wa, wb, wc, wo, wup = outs[:13]
        mod_next = None if last else outs[13]
        per_batch = lambda a: a.reshape(nb, CTX_LEN, RET_W)
        if last:
            kc, vc = _in_proj(ctx, *in_args, None, tm=nb * CTX_LEN, ctx=True, kv_only=True)
            (ret,) = _retention(l, logits, (q, k, v, sg), (per_batch(kc), per_batch(vc)), ctx_out=False)
        else:
            qc, kc, vc, sgc, f_ctx, ugc, vnc, gatesc = _in_proj(ctx, *in_args, None, tm=nb * CTX_LEN, ctx=True)
            ret, retc = _retention(l, logits, (q, k, v, sg), tuple(map(per_batch, (qc, kc, vc, sgc))), ctx_out=True)
            retc = retc.reshape(1, nb * CTX_LEN, RET_W)
        four, wdn = _fourier_latent(f_lat, cast=[(w_down, l)])
        mix_w = (l, mod, ws, bs_tab, wa, wb, wc, wo, *gains, wup, wdn)
        if last:
            x = _mix_mlp(x, ret, four, ug, vn, gates, *mix_w, tm=TM_LAT, ctx=False)
        else:
            x, ctx, w_in_l = _mix_mlp_merged(
                (x, ret, four, ug, vn, gates), (ctx, retc, _fourier_ctx(f_ctx), ugc, vnc, gatesc), *mix_w,
                tm=TM_LAT, cast=[(w_in, l + 1)])
            mod = mod_next.reshape(8, 1, 6 * D_MODEL)
    return x
```
